```python
import jax, jax.numpy as jnp
from jax import lax
import numpy as np

D_MODEL = 1024
BATCH = 8
SEQ = 8192
DEPTH = 2

CHUNK = 64
N_META = 16
MIX = D_MODEL
CONV_WIDTH_CH = MIX // 2
CONV_HEADS = 8
CONV_HEAD_DIM = CONV_WIDTH_CH // CONV_HEADS
CONV_K = 31
POOL_WIDTH_CH = MIX - CONV_WIDTH_CH
POOL_WINDOWS = (2, 4, 8, 16)
POOL_GROUPS = len(POOL_WINDOWS)
POOL_GROUP_DIM = POOL_WIDTH_CH // POOL_GROUPS
IN_COLS = 2 * CONV_WIDTH_CH + POOL_WIDTH_CH
D_FF = 2816
FFN_CONV_K = 3
EPS = 1e-6

kernel_name = "hybrid_conformer_conv_pool_encoder"


def rmsnorm(x, g):
    xf = x.astype(jnp.float32)
    y = xf * lax.rsqrt(jnp.mean(xf * xf, axis=-1, keepdims=True) + EPS)
    return (y * g.astype(jnp.float32)).astype(x.dtype)


def causal_dwconv(x, k):
    w, c = k.shape
    xp = jnp.pad(x, ((0, 0), (w - 1, 0), (0, 0)))
    return lax.conv_general_dilated(
        xp, k[:, None, :].astype(x.dtype), window_strides=(1,), padding="VALID",
        dimension_numbers=("NWC", "WIO", "NWC"), feature_group_count=c)


def conformer_conv_group(a, g, dw_k, dw_b, ln_g, ln_b):
    u = a * jax.nn.sigmoid(g)
    u = causal_dwconv(u, dw_k) + dw_b.astype(a.dtype)
    bsz, length, c = u.shape
    uh = u.reshape(bsz, length, CONV_HEADS, CONV_HEAD_DIM).astype(jnp.float32)
    mu = jnp.mean(uh, axis=-1, keepdims=True)
    var = jnp.mean(jnp.square(uh - mu), axis=-1, keepdims=True)
    uh = (uh - mu) * lax.rsqrt(var + EPS)
    u = uh.reshape(bsz, length, c) * ln_g.astype(jnp.float32) + ln_b.astype(jnp.float32)
    return jax.nn.silu(u).astype(a.dtype)


def multiscale_pool_group(p, pool_w, pool_scale):
    bsz, length, c = p.shape
    pf = p.astype(jnp.float32)
    cs = jnp.pad(jnp.cumsum(pf, axis=1), ((0, 0), (1, 0), (0, 0)))
    t = jnp.arange(length)
    outs = []
    for gi, w in enumerate(POOL_WINDOWS):
        sl = slice(gi * POOL_GROUP_DIM, (gi + 1) * POOL_GROUP_DIM)
        cg = cs[:, :, sl]
        upper = cg[:, 1:]
        lower = jnp.pad(cg[:, :length + 1 - w], ((0, 0), (w - 1, 0), (0, 0)))
        cnt = jnp.minimum(t + 1, w).astype(jnp.float32)[None, :, None]
        outs.append((upper - lower) / cnt - pf[:, :, sl])
    d = jnp.stack(outs, axis=2).astype(p.dtype)
    y = jnp.einsum("blgc,gcd->blgd", d, pool_w).reshape(bsz, length, c)
    return y * pool_scale


def _fwd_setup_inputs(seed: int = 0) -> dict:
    key = jax.random.key(seed)
    ks = jax.random.split(key, 16)
    f32 = jnp.float32
    nrm = lambda k, shape, s: (jax.random.normal(k, shape, f32) * s)
    return {
        "x": nrm(ks[0], (BATCH, SEQ, D_MODEL), 1.0),
        "meta_tokens": nrm(ks[1], (N_META, D_MODEL), 1.0),
        "norm1_g": 1.0 + nrm(ks[2], (DEPTH, D_MODEL), 0.02),
        "w_in": nrm(ks[3], (DEPTH, D_MODEL, IN_COLS), D_MODEL ** -0.5),
        "conv_dw_k": nrm(ks[4], (DEPTH, CONV_K, CONV_WIDTH_CH), CONV_K ** -0.5),
        "conv_dw_b": nrm(ks[5], (DEPTH, CONV_WIDTH_CH), 0.02),
        "conv_ln_g": 1.0 + nrm(ks[6], (DEPTH, CONV_WIDTH_CH), 0.02),
        "conv_ln_b": nrm(ks[7], (DEPTH, CONV_WIDTH_CH), 0.02),
        "pool_w": nrm(ks[8], (DEPTH, POOL_GROUPS, POOL_GROUP_DIM, POOL_GROUP_DIM), POOL_GROUP_DIM ** -0.5),
        "pool_scale": 1.0 + nrm(ks[9], (DEPTH, POOL_WIDTH_CH), 0.02),
        "w_out": nrm(ks[10], (DEPTH, MIX, D_MODEL), MIX ** -0.5),
        "norm2_g": 1.0 + nrm(ks[11], (DEPTH, D_MODEL), 0.02),
        "w_up": nrm(ks[12], (DEPTH, D_MODEL, 2 * D_FF), D_MODEL ** -0.5),
        "ffn_dw_k": nrm(ks[13], (DEPTH, FFN_CONV_K, 2 * D_FF), FFN_CONV_K ** -0.5),
        "w_down": nrm(ks[14], (DEPTH, D_FF, D_MODEL), D_FF ** -0.5),
        "final_g": 1.0 + nrm(ks[15], (D_MODEL,), 0.02),
    }


def _fwd_reference(x, meta_tokens, norm1_g, w_in, conv_dw_k, conv_dw_b, conv_ln_g, conv_ln_b,
              pool_w, pool_scale, w_out, norm2_g, w_up, ffn_dw_k, w_down, final_g):
    bsz = x.shape[0]
    meta = jnp.broadcast_to(meta_tokens[None].astype(x.dtype), (bsz, N_META, D_MODEL))
    h = jnp.concatenate([meta, x], axis=1)
    for i in range(DEPTH):
        hn = rmsnorm(h, norm1_g[i])
        z = hn @ w_in[i]
        a = z[..., :CONV_WIDTH_CH]
        g = z[..., CONV_WIDTH_CH:2 * CONV_WIDTH_CH]
        p = z[..., 2 * CONV_WIDTH_CH:]
        y_conv = conformer_conv_group(a, g, conv_dw_k[i], conv_dw_b[i], conv_ln_g[i], conv_ln_b[i])
        y_pool = multiscale_pool_group(p, pool_w[i], pool_scale[i])
        h = h + jnp.concatenate([y_conv, y_pool], axis=-1) @ w_out[i]
        hn = rmsnorm(h, norm2_g[i])
        ug = causal_dwconv(hn @ w_up[i], ffn_dw_k[i])
        gate, val = ug[..., :D_FF], ug[..., D_FF:]
        h = h + (jax.nn.silu(gate) * val) @ w_down[i]
    return rmsnorm(h, final_g)[:, N_META:]


import jax as _jax
import jax.numpy as _jnp

TWIN_FORMAT = 'train_step'
FWD_PARAMS = ['x', 'meta_tokens', 'norm1_g', 'w_in', 'conv_dw_k', 'conv_dw_b', 'conv_ln_g', 'conv_ln_b', 'pool_w', 'pool_scale', 'w_out', 'norm2_g', 'w_up', 'ffn_dw_k', 'w_down', 'final_g']
TWIN_WEIGHTS = ['meta_tokens', 'norm1_g', 'w_in', 'conv_dw_k', 'conv_dw_b', 'conv_ln_g', 'conv_ln_b', 'pool_w', 'pool_scale', 'w_out', 'norm2_g', 'w_up', 'ffn_dw_k', 'w_down', 'final_g']
TWIN_DIFF_INPUT = 'x'
TWIN_INPUTS = ['x', 'meta_tokens', 'norm1_g', 'w_in', 'conv_dw_k', 'conv_dw_b', 'conv_ln_g', 'conv_ln_b', 'pool_w', 'pool_scale', 'w_out', 'norm2_g', 'w_up', 'ffn_dw_k', 'w_down', 'final_g', 'loss_target', 'm_meta_tokens', 'm_norm1_g', 'm_w_in', 'm_conv_dw_k', 'm_conv_dw_b', 'm_conv_ln_g', 'm_conv_ln_b', 'm_pool_w', 'm_pool_scale', 'm_w_out', 'm_norm2_g', 'm_w_up', 'm_ffn_dw_k', 'm_w_down', 'm_final_g', 'v_meta_tokens', 'v_norm1_g', 'v_w_in', 'v_conv_dw_k', 'v_conv_dw_b', 'v_conv_ln_g', 'v_conv_ln_b', 'v_pool_w', 'v_pool_scale', 'v_w_out', 'v_norm2_g', 'v_w_up', 'v_ffn_dw_k', 'v_w_down', 'v_final_g']
TWIN_OUTPUTS = ['loss', 'grad_x', 'grad_meta_tokens', 'grad_norm1_g', 'grad_w_in', 'grad_conv_dw_k', 'grad_conv_dw_b', 'grad_conv_ln_g', 'grad_conv_ln_b', 'grad_pool_w', 'grad_pool_scale', 'grad_w_out', 'grad_norm2_g', 'grad_w_up', 'grad_ffn_dw_k', 'grad_w_down', 'grad_final_g', 'delta_meta_tokens', 'delta_norm1_g', 'delta_w_in', 'delta_conv_dw_k', 'delta_conv_dw_b', 'delta_conv_ln_g', 'delta_conv_ln_b', 'delta_pool_w', 'delta_pool_scale', 'delta_w_out', 'delta_norm2_g', 'delta_w_up', 'delta_ffn_dw_k', 'delta_w_down', 'delta_final_g', 'new_m_meta_tokens', 'new_m_norm1_g', 'new_m_w_in', 'new_m_conv_dw_k', 'new_m_conv_dw_b', 'new_m_conv_ln_g', 'new_m_conv_ln_b', 'new_m_pool_w', 'new_m_pool_scale', 'new_m_w_out', 'new_m_norm2_g', 'new_m_w_up', 'new_m_ffn_dw_k', 'new_m_w_down', 'new_m_final_g', 'new_v_meta_tokens', 'new_v_norm1_g', 'new_v_w_in', 'new_v_conv_dw_k', 'new_v_conv_dw_b', 'new_v_conv_ln_g', 'new_v_conv_ln_b', 'new_v_pool_w', 'new_v_pool_scale', 'new_v_w_out', 'new_v_norm2_g', 'new_v_w_up', 'new_v_ffn_dw_k', 'new_v_w_down', 'new_v_final_g']
TWIN_LEAF_KINDS = {'loss': 'loss', 'grad_x': 'grad_x', 'grad_meta_tokens': 'grad_w', 'grad_norm1_g': 'grad_w', 'grad_w_in': 'grad_w', 'grad_conv_dw_k': 'grad_w', 'grad_conv_dw_b': 'grad_w', 'grad_conv_ln_g': 'grad_w', 'grad_conv_ln_b': 'grad_w', 'grad_pool_w': 'grad_w', 'grad_pool_scale': 'grad_w', 'grad_w_out': 'grad_w', 'grad_norm2_g': 'grad_w', 'grad_w_up': 'grad_w', 'grad_ffn_dw_k': 'grad_w', 'grad_w_down': 'grad_w', 'grad_final_g': 'grad_w', 'delta_meta_tokens': 'delta_w', 'delta_norm1_g': 'delta_w', 'delta_w_in': 'delta_w', 'delta_conv_dw_k': 'delta_w', 'delta_conv_dw_b': 'delta_w', 'delta_conv_ln_g': 'delta_w', 'delta_conv_ln_b': 'delta_w', 'delta_pool_w': 'delta_w', 'delta_pool_scale': 'delta_w', 'delta_w_out': 'delta_w', 'delta_norm2_g': 'delta_w', 'delta_w_up': 'delta_w', 'delta_ffn_dw_k': 'delta_w', 'delta_w_down': 'delta_w', 'delta_final_g': 'delta_w', 'new_m_meta_tokens': 'new_m', 'new_m_norm1_g': 'new_m', 'new_m_w_in': 'new_m', 'new_m_conv_dw_k': 'new_m', 'new_m_conv_dw_b': 'new_m', 'new_m_conv_ln_g': 'new_m', 'new_m_conv_ln_b': 'new_m', 'new_m_pool_w': 'new_m', 'new_m_pool_scale': 'new_m', 'new_m_w_out': 'new_m', 'new_m_norm2_g': 'new_m', 'new_m_w_up': 'new_m', 'new_m_ffn_dw_k': 'new_m', 'new_m_w_down': 'new_m', 'new_m_final_g': 'new_m', 'new_v_meta_tokens': 'new_v', 'new_v_norm1_g': 'new_v', 'new_v_w_in': 'new_v', 'new_v_conv_dw_k': 'new_v', 'new_v_conv_dw_b': 'new_v', 'new_v_conv_ln_g': 'new_v', 'new_v_conv_ln_b': 'new_v', 'new_v_pool_w': 'new_v', 'new_v_pool_scale': 'new_v', 'new_v_w_out': 'new_v', 'new_v_norm2_g': 'new_v', 'new_v_w_up': 'new_v', 'new_v_ffn_dw_k': 'new_v', 'new_v_w_down': 'new_v', 'new_v_final_g': 'new_v'}


def _forward(args):
    return _fwd_reference(*[args[k] for k in FWD_PARAMS])


def _output_shape():
    def fwd():
        inp = _fwd_setup_inputs(0)
        return _fwd_reference(*[inp[k] for k in FWD_PARAMS])
    out = _jax.eval_shape(fwd)
    return out.shape, out.dtype

N_MICROBATCH = 1
ADAM_LR = 0.001
ADAM_B1 = 0.9
ADAM_B2 = 0.999
ADAM_EPS = 1e-08
ADAM_WD = 0.01
ADAM_STEP = 10
PER_EXAMPLE_BATCH_AXIS = {'x': 0, 'loss_target': 0}
SHARED_INPUTS = []
_WEIGHT_DTYPES = {'meta_tokens': _jnp.float32, 'norm1_g': _jnp.float32, 'w_in': _jnp.float32, 'conv_dw_k': _jnp.float32, 'conv_dw_b': _jnp.float32, 'conv_ln_g': _jnp.float32, 'conv_ln_b': _jnp.float32, 'pool_w': _jnp.float32, 'pool_scale': _jnp.float32, 'w_out': _jnp.float32, 'norm2_g': _jnp.float32, 'w_up': _jnp.float32, 'ffn_dw_k': _jnp.float32, 'w_down': _jnp.float32, 'final_g': _jnp.float32}
MOMENT_SCALE = {'meta_tokens': 9.248630e-03, 'norm1_g': 1.614528e-01, 'w_in': 1.322764e-01, 'conv_dw_k': 1.298677e-01, 'conv_dw_b': 2.645510e-01, 'conv_ln_g': 1.546947e-01, 'conv_ln_b': 1.433673e-01, 'pool_w': 1.830504e-01, 'pool_scale': 1.937109e-01, 'w_out': 1.570076e-01, 'norm2_g': 1.621050e-01, 'w_up': 6.547581e-02, 'ffn_dw_k': 6.629134e-02, 'w_down': 1.069834e-01, 'final_g': 6.399299e+01}


def _to_microbatches(a, axis):
    t = _jnp.moveaxis(a, axis, 0)
    t = t.reshape((N_MICROBATCH, t.shape[0] // N_MICROBATCH) + t.shape[1:])
    return _jnp.moveaxis(t, 1, axis + 1)


def setup_inputs(seed: int = 0) -> dict:
    inp = _fwd_setup_inputs(seed)
    key = _jax.random.fold_in(_jax.random.key(seed), 7919)
    shape, _ = _output_shape()
    out = dict(inp)
    out["loss_target"] = _jax.random.normal(_jax.random.fold_in(key, 0), shape, _jnp.float32)
    for i, name in enumerate(TWIN_WEIGHTS):
        w = inp[name].astype(_jnp.float32)
        if MOMENT_SCALE is None:
            s = _jnp.sqrt(_jnp.mean(_jnp.square(w)) + 1e-30)
        else:
            s = MOMENT_SCALE[name]
        km, kv = _jax.random.split(_jax.random.fold_in(key, i + 1))
        out[name] = w
        out["m_" + name] = s * _jax.random.normal(km, w.shape, _jnp.float32)
        out["v_" + name] = (s * s) * _jax.random.uniform(kv, w.shape, _jnp.float32, 0.5, 1.5)
    if N_MICROBATCH > 1:
        for name, axis in PER_EXAMPLE_BATCH_AXIS.items():
            out[name] = _to_microbatches(out[name], axis)
    return {'x': out['x'], 'meta_tokens': out['meta_tokens'], 'norm1_g': out['norm1_g'], 'w_in': out['w_in'], 'conv_dw_k': out['conv_dw_k'], 'conv_dw_b': out['conv_dw_b'], 'conv_ln_g': out['conv_ln_g'], 'conv_ln_b': out['conv_ln_b'], 'pool_w': out['pool_w'], 'pool_scale': out['pool_scale'], 'w_out': out['w_out'], 'norm2_g': out['norm2_g'], 'w_up': out['w_up'], 'ffn_dw_k': out['ffn_dw_k'], 'w_down': out['w_down'], 'final_g': out['final_g'], 'loss_target': out['loss_target'], 'm_meta_tokens': out['m_meta_tokens'], 'm_norm1_g': out['m_norm1_g'], 'm_w_in': out['m_w_in'], 'm_conv_dw_k': out['m_conv_dw_k'], 'm_conv_dw_b': out['m_conv_dw_b'], 'm_conv_ln_g': out['m_conv_ln_g'], 'm_conv_ln_b': out['m_conv_ln_b'], 'm_pool_w': out['m_pool_w'], 'm_pool_scale': out['m_pool_scale'], 'm_w_out': out['m_w_out'], 'm_norm2_g': out['m_norm2_g'], 'm_w_up': out['m_w_up'], 'm_ffn_dw_k': out['m_ffn_dw_k'], 'm_w_down': out['m_w_down'], 'm_final_g': out['m_final_g'], 'v_meta_tokens': out['v_meta_tokens'], 'v_norm1_g': out['v_norm1_g'], 'v_w_in': out['v_w_in'], 'v_conv_dw_k': out['v_conv_dw_k'], 'v_conv_dw_b': out['v_conv_dw_b'], 'v_conv_ln_g': out['v_conv_ln_g'], 'v_conv_ln_b': out['v_conv_ln_b'], 'v_pool_w': out['v_pool_w'], 'v_pool_scale': out['v_pool_scale'], 'v_w_out': out['v_w_out'], 'v_norm2_g': out['v_norm2_g'], 'v_w_up': out['v_w_up'], 'v_ffn_dw_k': out['v_ffn_dw_k'], 'v_w_down': out['v_w_down'], 'v_final_g': out['v_final_g']}


def _loss(weights, diff, rest, loss_target):
    with _jax.named_scope("forward"):
        args = {**rest, TWIN_DIFF_INPUT: diff, **{k: w.astype(_WEIGHT_DTYPES[k]) for k, w in weights.items()}}
        y = _forward(args)
    with _jax.named_scope("loss_head"):
        err = _jnp.square(y.astype(_jnp.float32) - loss_target)
        return 0.5 * _jnp.sum(_jnp.mean(err, axis=-1)) if err.ndim else 0.5 * err


def _adamw(w, g, m, v):
    m = ADAM_B1 * m + (1.0 - ADAM_B1) * g
    v = ADAM_B2 * v + (1.0 - ADAM_B2) * _jnp.square(g)
    m_hat = m / (1.0 - ADAM_B1 ** ADAM_STEP)
    v_hat = v / (1.0 - ADAM_B2 ** ADAM_STEP)
    delta = -ADAM_LR * (m_hat / (_jnp.sqrt(v_hat) + ADAM_EPS) + ADAM_WD * w)
    return delta, m, v


def reference(x, meta_tokens, norm1_g, w_in, conv_dw_k, conv_dw_b, conv_ln_g, conv_ln_b, pool_w, pool_scale, w_out, norm2_g, w_up, ffn_dw_k, w_down, final_g, loss_target, m_meta_tokens, m_norm1_g, m_w_in, m_conv_dw_k, m_conv_dw_b, m_conv_ln_g, m_conv_ln_b, m_pool_w, m_pool_scale, m_w_out, m_norm2_g, m_w_up, m_ffn_dw_k, m_w_down, m_final_g, v_meta_tokens, v_norm1_g, v_w_in, v_conv_dw_k, v_conv_dw_b, v_conv_ln_g, v_conv_ln_b, v_pool_w, v_pool_scale, v_w_out, v_norm2_g, v_w_up, v_ffn_dw_k, v_w_down, v_final_g):
    given = dict(x=x, meta_tokens=meta_tokens, norm1_g=norm1_g, w_in=w_in, conv_dw_k=conv_dw_k, conv_dw_b=conv_dw_b, conv_ln_g=conv_ln_g, conv_ln_b=conv_ln_b, pool_w=pool_w, pool_scale=pool_scale, w_out=w_out, norm2_g=norm2_g, w_up=w_up, ffn_dw_k=ffn_dw_k, w_down=w_down, final_g=final_g, loss_target=loss_target, m_meta_tokens=m_meta_tokens, m_norm1_g=m_norm1_g, m_w_in=m_w_in, m_conv_dw_k=m_conv_dw_k, m_conv_dw_b=m_conv_dw_b, m_conv_ln_g=m_conv_ln_g, m_conv_ln_b=m_conv_ln_b, m_pool_w=m_pool_w, m_pool_scale=m_pool_scale, m_w_out=m_w_out, m_norm2_g=m_norm2_g, m_w_up=m_w_up, m_ffn_dw_k=m_ffn_dw_k, m_w_down=m_w_down, m_final_g=m_final_g, v_meta_tokens=v_meta_tokens, v_norm1_g=v_norm1_g, v_w_in=v_w_in, v_conv_dw_k=v_conv_dw_k, v_conv_dw_b=v_conv_dw_b, v_conv_ln_g=v_conv_ln_g, v_conv_ln_b=v_conv_ln_b, v_pool_w=v_pool_w, v_pool_scale=v_pool_scale, v_w_out=v_w_out, v_norm2_g=v_norm2_g, v_w_up=v_w_up, v_ffn_dw_k=v_ffn_dw_k, v_w_down=v_w_down, v_final_g=v_final_g)
    weights = {n: given[n] for n in TWIN_WEIGHTS}
    shared = {n: given[n] for n in SHARED_INPUTS}
    per_example = {n: given[n] for n in ['x']}
    grad_fn = _jax.value_and_grad(_loss, argnums=(0, 1))

    def one_microbatch(ex, loss_target):
        ex = dict(ex)
        diff = ex.pop(TWIN_DIFF_INPUT)
        return grad_fn(weights, diff, {**shared, **ex}, loss_target)

    if N_MICROBATCH == 1:
        loss, (grad_w, grad_x) = one_microbatch(per_example, given["loss_target"])
    else:
        def body(carry, xs):
            loss_sum, grad_sum = carry
            l_k, (gw_k, gx_k) = one_microbatch(xs[0], xs[1])
            with _jax.named_scope("update"):
                return (loss_sum + l_k, _jax.tree.map(_jnp.add, grad_sum, gw_k)), gx_k

        init = (_jnp.zeros((), _jnp.float32), _jax.tree.map(_jnp.zeros_like, weights))
        (loss, grad_w), grad_x = _jax.lax.scan(body, init, (per_example, given["loss_target"]))
    with _jax.named_scope("update"):
        delta_w, new_m, new_v = {}, {}, {}
        for n in TWIN_WEIGHTS:
            delta_w[n], new_m[n], new_v[n] = _adamw(weights[n], grad_w[n], given["m_" + n], given["v_" + n])
    return (loss, grad_x, *[grad_w[n] for n in TWIN_WEIGHTS], *[delta_w[n] for n in TWIN_WEIGHTS],
            *[new_m[n] for n in TWIN_WEIGHTS], *[new_v[n] for n in TWIN_WEIGHTS])
```

```python
import functools

import jax
import jax.numpy as jnp
from jax import lax
from jax.experimental import pallas as pl
from jax.experimental.pallas import tpu as pltpu

F32 = jnp.float32
BF16 = jnp.bfloat16

EPS = 1e-6
HEAD_DIM = 64
POOL_WINDOWS = (2, 4, 8, 16)
ADAM_LR = 0.001
ADAM_B1 = 0.9
ADAM_B2 = 0.999
ADAM_EPS = 1e-08
ADAM_WD = 0.01
ADAM_STEP = 10

N_DEV = 8
SUBLANES = 8
HALO = 48
CONV_PAD = 32
POOL_PAD = 16
FFN_PAD = 8
ROW_CHUNK = 24
MAX_TILE_ROWS = 1024
VMEM_LIMIT = 52 * 1024 * 1024


def _divisor(n, cap, mult):
    best = None
    for d in range(mult, min(n, cap) + 1, mult):
        if n % d == 0:
            best = d
    return n if best is None else best


def _token_tile(L):
    return _divisor(L, MAX_TILE_ROWS, HALO)


def _row_tile(L):
    return _divisor(L, 320, 2 * SUBLANES)


def _stat_rows(tl):
    return _divisor(tl, 256, SUBLANES)


def _params(sem=None):
    return pltpu.CompilerParams(dimension_semantics=sem, vmem_limit_bytes=VMEM_LIMIT)


def _rowsum8(x):
    acc = x[0:SUBLANES]
    for k in range(1, x.shape[0] // SUBLANES):
        acc = acc + x[k * SUBLANES:(k + 1) * SUBLANES]
    return acc


def _sigmoid(x):
    return jax.nn.sigmoid(x)


def _head_mean(x, am_ref):
    bw = am_ref.shape[0]
    am = am_ref[...]
    outs = []
    for blk in range(x.shape[1] // bw):
        xb = x[:, blk * bw:(blk + 1) * bw]
        hi = xb.astype(BF16)
        lo = (xb - hi.astype(F32)).astype(BF16)
        outs.append(jnp.dot(hi, am, preferred_element_type=F32) + jnp.dot(lo, am, preferred_element_type=F32))
    return outs[0] if len(outs) == 1 else jnp.concatenate(outs, axis=-1)


def _exchange(srcs, modes, name):
    n = len(srcs)
    out_shapes = []
    for s, m in zip(srcs, modes):
        shp = ((N_DEV,) + tuple(s.shape)) if m == "gather" else tuple(s.shape)
        out_shapes.append(jax.ShapeDtypeStruct(shp, s.dtype))

    def body(*refs):
        src_refs, out_refs = refs[:n], refs[n:2 * n]
        send_sems, recv_sems, local_sems = refs[2 * n:]
        x, y, c = lax.axis_index("x"), lax.axis_index("y"), lax.axis_index("c")
        me = 4 * x + 2 * y + c

        def peer(d):
            return (x ^ ((d >> 2) & 1), y ^ ((d >> 1) & 1), c ^ (d & 1))

        def peer_id(d):
            px, py, pc = peer(d)
            return 4 * px + 2 * py + pc

        def remote(t, d):
            src = src_refs[t] if modes[t] == "gather" else src_refs[t].at[peer_id(d)]
            return pltpu.make_async_remote_copy(
                src_ref=src, dst_ref=out_refs[t].at[me], send_sem=send_sems.at[t, d - 1], recv_sem=recv_sems.at[t, d - 1],
                device_id=peer(d), device_id_type=pl.DeviceIdType.MESH)

        def arrival(t, d):
            src = src_refs[t] if modes[t] == "gather" else src_refs[t].at[me]
            return pltpu.make_async_remote_copy(
                src_ref=src, dst_ref=out_refs[t].at[peer_id(d)], send_sem=send_sems.at[t, d - 1],
                recv_sem=recv_sems.at[t, d - 1], device_id=peer(d), device_id_type=pl.DeviceIdType.MESH)

        local = []
        for t in range(n):
            src = src_refs[t] if modes[t] == "gather" else src_refs[t].at[me]
            cp = pltpu.make_async_copy(src, out_refs[t].at[me], local_sems.at[t])
            cp.start()
            local.append(cp)
        sends = [remote(t, d) for t in range(n) for d in range(1, N_DEV)]
        for cp in sends:
            cp.start()
        for t in range(n):
            for d in range(1, N_DEV):
                arrival(t, d).wait_recv()
        for cp in sends:
            cp.wait_send()
        for cp in local:
            cp.wait()

    any_spec = pl.BlockSpec(memory_space=pl.ANY)
    return pl.pallas_call(
        body, name=name, out_shape=tuple(out_shapes),
        in_specs=[any_spec] * n, out_specs=tuple([any_spec] * n),
        scratch_shapes=[pltpu.SemaphoreType.DMA((n, N_DEV - 1)), pltpu.SemaphoreType.DMA((n, N_DEV - 1)),
                        pltpu.SemaphoreType.DMA((n,))],
        compiler_params=pltpu.CompilerParams(has_side_effects=True),
    )(*srcs)


def _rms_fwd(h, g, name):
    L, D = h.shape
    tl = _row_tile(L)

    def body(h_ref, g_ref, o_ref):
        x = h_ref[...]
        r = lax.rsqrt(jnp.mean(x * x, axis=-1, keepdims=True) + EPS)
        o_ref[...] = ((x * r) * g_ref[...]).astype(o_ref.dtype)

    return pl.pallas_call(
        body, name=name, grid=(L // tl,),
        in_specs=[pl.BlockSpec((tl, D), lambda i: (i, 0)), pl.BlockSpec((1, D), lambda i: (0, 0))],
        out_specs=pl.BlockSpec((tl, D), lambda i: (i, 0)),
        out_shape=jax.ShapeDtypeStruct((L, D), BF16),
        compiler_params=_params(("parallel",)),
    )(h, g)


def _mm(a, b, name, *, res=None, out_dtype=F32, tn_cap=1408, halves=1):
    if halves > 1:
        _, M, kh = a.shape
        K = kh * halves
    else:
        M, K = a.shape
        kh = K
    N = b.shape[1]
    tm = _token_tile(M)
    tn = _divisor(N, tn_cap, 128)
    tk = kh if kh <= 2816 else _divisor(kh, 2816, 128)
    kper = kh // tk
    nk = halves * kper
    grid = (M // tm, N // tn, nk)

    def body(*refs):
        if res is None:
            a_ref, b_ref, o_ref = refs[:3]
            r_ref = None
            scratch = refs[3:]
        else:
            a_ref, b_ref, r_ref, o_ref = refs[:4]
            scratch = refs[4:]
        av = a_ref[0] if halves > 1 else a_ref[...]
        prod = jnp.dot(av.astype(BF16), b_ref[...], preferred_element_type=F32)
        if nk == 1:
            if r_ref is not None:
                prod = prod + r_ref[...]
            o_ref[...] = prod.astype(o_ref.dtype)
        else:
            acc = scratch[0]
            k = pl.program_id(2)

            @pl.when(k == 0)
            def _():
                acc[...] = prod

            @pl.when(k > 0)
            def _():
                acc[...] += prod

            @pl.when(k == nk - 1)
            def _():
                tot = acc[...]
                if r_ref is not None:
                    tot = tot + r_ref[...]
                o_ref[...] = tot.astype(o_ref.dtype)

    if halves > 1:
        a_spec = pl.BlockSpec((1, tm, tk), lambda i, j, k: (k // kper, i, k % kper))
    else:
        a_spec = pl.BlockSpec((tm, tk), lambda i, j, k: (i, k))
    in_specs = [a_spec, pl.BlockSpec((tk, tn), lambda i, j, k: (k, j))]
    args = [a, b]
    if res is not None:
        in_specs.append(pl.BlockSpec((tm, tn), lambda i, j, k: (i, j)))
        args.append(res)
    return pl.pallas_call(
        body, name=name, grid=grid, in_specs=in_specs,
        out_specs=pl.BlockSpec((tm, tn), lambda i, j, k: (i, j)),
        out_shape=jax.ShapeDtypeStruct((M, N), out_dtype),
        scratch_shapes=[pltpu.VMEM((tm, tn), F32)] if nk > 1 else [],
        compiler_params=_params(("parallel", "parallel", "arbitrary")),
    )(*args)


def _mm_tn(a, b, name, *, halves=1, tq_cap=1408):
    L, P = a.shape
    if halves > 1:
        qh = b.shape[2]
        Q = qh * halves
    else:
        Q = b.shape[1]
        qh = Q
    tl = _token_tile(L)
    tp = _divisor(P, 1408, 128)
    tq = _divisor(qh, tq_cap, 128)
    qper = qh // tq
    grid = (P // tp, Q // tq, L // tl)

    def body(a_ref, b_ref, o_ref):
        bv = b_ref[0] if halves > 1 else b_ref[...]
        prod = lax.dot_general(a_ref[...].astype(BF16), bv.astype(BF16), (((0,), (0,)), ((), ())),
                               preferred_element_type=F32)
        l = pl.program_id(2)

        @pl.when(l == 0)
        def _():
            o_ref[...] = prod

        @pl.when(l > 0)
        def _():
            o_ref[...] += prod

    if halves > 1:
        b_spec = pl.BlockSpec((1, tl, tq), lambda p, q, l: (q // qper, l, q % qper))
    else:
        b_spec = pl.BlockSpec((tl, tq), lambda p, q, l: (l, q))
    return pl.pallas_call(
        body, name=name, grid=grid,
        in_specs=[pl.BlockSpec((tl, tp), lambda p, q, l: (l, p)), b_spec],
        out_specs=pl.BlockSpec((tp, tq), lambda p, q, l: (p, q)),
        out_shape=jax.ShapeDtypeStruct((P, Q), F32),
        compiler_params=_params(("parallel", "parallel", "arbitrary")),
    )(a, b)


def _rms_bwd(h, g, dhn, dres, name):
    L, D = h.shape
    tl = _row_tile(L)
    nt = L // tl

    def body(h_ref, g_ref, dhn_ref, dres_ref, dh_ref, dg_ref):
        i = pl.program_id(0)
        x = h_ref[...]
        r = lax.rsqrt(jnp.mean(x * x, axis=-1, keepdims=True) + EPS)
        xhat = x * r
        dhn = dhn_ref[...]
        dxhat = dhn * g_ref[...]
        dh_ref[...] = dres_ref[...] + r * (dxhat - xhat * jnp.mean(dxhat * xhat, axis=-1, keepdims=True))
        part = jnp.sum(_rowsum8(dhn * xhat), axis=0, keepdims=True)

        @pl.when(i == 0)
        def _():
            dg_ref[...] = part

        @pl.when(i > 0)
        def _():
            dg_ref[...] += part

    tile = pl.BlockSpec((tl, D), lambda i: (i, 0))
    row = pl.BlockSpec((1, D), lambda i: (0, 0))
    return pl.pallas_call(
        body, name=name, grid=(nt,), in_specs=[tile, row, tile, tile], out_specs=(tile, row),
        out_shape=(jax.ShapeDtypeStruct((L, D), F32), jax.ShapeDtypeStruct((1, D), F32)),
        compiler_params=_params(("arbitrary",)),
    )(h, g, dhn, dres)


def _loss_head(h, g, tgt, n_meta, name):
    L, D = h.shape
    tl = _row_tile(L)
    nt = L // tl

    def body(h_ref, g_ref, t_ref, dh_ref, dg_ref, loss_ref):
        i = pl.program_id(0)
        x = h_ref[...]
        r = lax.rsqrt(jnp.mean(x * x, axis=-1, keepdims=True) + EPS)
        xhat = x * r
        gg = g_ref[...]
        y = xhat * gg
        rows = i * tl + lax.broadcasted_iota(jnp.int32, (tl, 1), 0)
        err = jnp.where(rows >= n_meta, y - t_ref[...], 0.0)
        dy = err * (1.0 / D)
        dxhat = dy * gg
        dh_ref[...] = r * (dxhat - xhat * jnp.mean(dxhat * xhat, axis=-1, keepdims=True))
        dg_part = jnp.sum(_rowsum8(dy * xhat), axis=0, keepdims=True)
        per_row = jnp.mean(err * err, axis=-1, keepdims=True)
        loss_part = jnp.broadcast_to(0.5 * jnp.sum(per_row, axis=0, keepdims=True), (1, 128))

        @pl.when(i == 0)
        def _():
            dg_ref[...] = dg_part
            loss_ref[...] = loss_part

        @pl.when(i > 0)
        def _():
            dg_ref[...] += dg_part
            loss_ref[...] += loss_part

    tile = pl.BlockSpec((tl, D), lambda i: (i, 0))
    row = pl.BlockSpec((1, D), lambda i: (0, 0))
    return pl.pallas_call(
        body, name=name, grid=(nt,), in_specs=[tile, row, tile],
        out_specs=(tile, row, pl.BlockSpec((1, 128), lambda i: (0, 0))),
        out_shape=(jax.ShapeDtypeStruct((L, D), F32), jax.ShapeDtypeStruct((1, D), F32),
                   jax.ShapeDtypeStruct((1, 128), F32)),
        compiler_params=_params(("arbitrary",)),
    )(h, g, tgt)


def _pool_fwd_block(pwin, pw_ref, row0, rb, g, gd, w, t0):
    wv = pwin[pl.ds(row0 + HALO - POOL_PAD, rb + POOL_PAD), g * gd:(g + 1) * gd]
    s = wv
    sh = 1
    while sh < w:
        s = s + pltpu.roll(s, sh, axis=0)
        sh *= 2
    win = s[POOL_PAD:POOL_PAD + rb]
    pt = wv[POOL_PAD:POOL_PAD + rb]
    tg = t0 + lax.broadcasted_iota(jnp.int32, (rb, 1), 0)
    cnt = jnp.minimum(tg + 1, w).astype(F32)
    return win / cnt - pt


def _fill_windows(i, zp_ref, zc_ref, u0w, pwin, tl, cc):
    keep = i > 0
    zp = zp_ref[...]
    u0w[0:HALO, :] = jnp.where(keep, zp[:, :cc] * _sigmoid(zp[:, cc:2 * cc]), 0.0)
    pwin[0:HALO, :] = jnp.where(keep, zp[:, 2 * cc:], 0.0)

    def fill(c, carry):
        b = pl.multiple_of(c * ROW_CHUNK, SUBLANES)
        zc = zc_ref[pl.ds(b, ROW_CHUNK), :]
        u0w[pl.ds(HALO + b, ROW_CHUNK), :] = zc[:, :cc] * _sigmoid(zc[:, cc:2 * cc])
        pwin[pl.ds(HALO + b, ROW_CHUNK), :] = zc[:, 2 * cc:]
        return carry

    lax.fori_loop(0, tl // ROW_CHUNK, fill, 0)


def _mixer_fwd(z, ck, cb, lg, lb, pw, ps, am, name):
    L, ci = z.shape
    kw, cc = ck.shape
    cp = ci - 2 * cc
    ng, gd = pw.shape[0], pw.shape[1]
    tl = _token_tile(L)
    nt = L // tl
    hb = tl // HALO
    rb = _stat_rows(tl)
    tap0 = CONV_PAD - (kw - 1)

    def body(zp_ref, zc_ref, ck_ref, cb_ref, lg_ref, lb_ref, pw_ref, ps_ref, am_ref, y_ref, u1_ref, u0w, pwin):
        i = pl.program_id(0)
        _fill_windows(i, zp_ref, zc_ref, u0w, pwin, tl, cc)

        def conv(c, carry):
            b = pl.multiple_of(c * ROW_CHUNK, SUBLANES)
            w = u0w[pl.ds(b + HALO - CONV_PAD, ROW_CHUNK + CONV_PAD), :]
            acc = jnp.broadcast_to(cb_ref[...], (ROW_CHUNK, cc))
            for j in range(kw):
                acc = acc + ck_ref[j:j + 1, :] * w[tap0 + j:tap0 + j + ROW_CHUNK]
            u1_ref[pl.ds(b, ROW_CHUNK), :] = acc
            return carry

        lax.fori_loop(0, tl // ROW_CHUNK, conv, 0)

        def blocks(k, carry):
            b = pl.multiple_of(k * rb, SUBLANES)
            u1 = u1_ref[pl.ds(b, rb), :]
            xc = u1 - _head_mean(u1, am_ref)
            var = _head_mean(xc * xc, am_ref)
            u2 = (xc * lax.rsqrt(var + EPS)) * lg_ref[...] + lb_ref[...]
            y_ref[pl.ds(b, rb), 0:cc] = (u2 * _sigmoid(u2)).astype(y_ref.dtype)
            for g in range(ng):
                d = _pool_fwd_block(pwin, pw_ref, b, rb, g, gd, POOL_WINDOWS[g], i * tl + b)
                yp = jnp.dot(d.astype(BF16), pw_ref[g].astype(BF16), preferred_element_type=F32)
                yp = yp * ps_ref[:, g * gd:(g + 1) * gd]
                y_ref[pl.ds(b, rb), cc + g * gd:cc + (g + 1) * gd] = yp.astype(y_ref.dtype)
            return carry

        lax.fori_loop(0, tl // rb, blocks, 0)

    def full(a):
        nd = a.ndim
        return pl.BlockSpec(a.shape, lambda i: (0,) * nd)

    return pl.pallas_call(
        body, name=name, grid=(nt,),
        in_specs=[pl.BlockSpec((HALO, ci), lambda i: (jnp.maximum(i * hb - 1, 0), 0)),
                  pl.BlockSpec((tl, ci), lambda i: (i, 0)),
                  full(ck), full(cb), full(lg), full(lb), full(pw), full(ps), full(am)],
        out_specs=(pl.BlockSpec((tl, cc + cp), lambda i: (i, 0)), pl.BlockSpec((tl, cc), lambda i: (i, 0))),
        out_shape=(jax.ShapeDtypeStruct((L, cc + cp), BF16), jax.ShapeDtypeStruct((L, cc), F32)),
        scratch_shapes=[pltpu.VMEM((HALO + tl, cc), F32), pltpu.VMEM((HALO + tl, cp), F32)],
        compiler_params=_params(("parallel",)),
    )(z, z, ck, cb, lg, lb, pw, ps, am)


def _mixer_bwd(z, u1, dy, ck, lg, lb, pw, ps, am, name):
    L, ci = z.shape
    kw, cc = ck.shape
    cp = ci - 2 * cc
    ng, gd = pw.shape[0], pw.shape[1]
    tl = _token_tile(L)
    nt = L // tl
    hb = tl // HALO
    rb = _stat_rows(tl)
    tap0 = CONV_PAD - (kw - 1)

    def body(zp_ref, zc_ref, u1c_ref, u1n_ref, dyc_ref, dyn_ref, ck_ref, lg_ref, lb_ref, pw_ref, ps_ref, am_ref,
             dz_ref, dck_ref, dcb_ref, dlg_ref, dlb_ref, dpw_ref, dps_ref,
             u0w, pwin, du1w, ddw, ew, dkacc, dcb8, dlg8, dlb8, dps8):
        i = pl.program_id(0)
        has_next = i < nt - 1

        @pl.when(i == 0)
        def _():
            dck_ref[...] = jnp.zeros_like(dck_ref)
            dcb_ref[...] = jnp.zeros_like(dcb_ref)
            dlg_ref[...] = jnp.zeros_like(dlg_ref)
            dlb_ref[...] = jnp.zeros_like(dlb_ref)
            dpw_ref[...] = jnp.zeros_like(dpw_ref)
            dps_ref[...] = jnp.zeros_like(dps_ref)

        dkacc[...] = jnp.zeros_like(dkacc)
        dcb8[...] = jnp.zeros_like(dcb8)
        dlg8[...] = jnp.zeros_like(dlg8)
        dlb8[...] = jnp.zeros_like(dlb8)
        dps8[...] = jnp.zeros_like(dps8)

        _fill_windows(i, zp_ref, zc_ref, u0w, pwin, tl, cc)

        def conv_side(u1, dyc, own):
            xc = u1 - _head_mean(u1, am_ref)
            rstd = lax.rsqrt(_head_mean(xc * xc, am_ref) + EPS)
            uh = xc * rstd
            lgv = lg_ref[...]
            u2 = uh * lgv + lb_ref[...]
            sg = _sigmoid(u2)
            du2 = dyc * (sg * (1.0 + u2 * (1.0 - sg)))
            if own:
                dlg8[...] += _rowsum8(du2 * uh)
                dlb8[...] += _rowsum8(du2)
            duh = du2 * lgv
            return rstd * (duh - _head_mean(duh, am_ref) - uh * _head_mean(duh * uh, am_ref))

        def pool_side(dyp, t0, rows):
            dds, es = [], []
            tg = t0 + lax.broadcasted_iota(jnp.int32, (rows, 1), 0)
            for g in range(ng):
                dypre = dyp[:, g * gd:(g + 1) * gd] * ps_ref[:, g * gd:(g + 1) * gd]
                dd = lax.dot_general(dypre.astype(BF16), pw_ref[g].astype(BF16), (((1,), (1,)), ((), ())),
                                     preferred_element_type=F32)
                cnt = jnp.minimum(tg + 1, POOL_WINDOWS[g]).astype(F32)
                dds.append(dd)
                es.append(dd / cnt)
            return jnp.concatenate(dds, axis=-1), jnp.concatenate(es, axis=-1)

        def blocks(k, carry):
            b = pl.multiple_of(k * rb, SUBLANES)
            dyb = dyc_ref[pl.ds(b, rb), :]
            du1 = conv_side(u1c_ref[pl.ds(b, rb), :], dyb[:, :cc], True)
            du1w[pl.ds(b, rb), :] = du1
            dcb8[...] += _rowsum8(du1)
            dyp = dyb[:, cc:]
            dd, e = pool_side(dyp, i * tl + b, rb)
            ddw[pl.ds(b, rb), :] = dd
            ew[pl.ds(b, rb), :] = e
            for g in range(ng):
                d = _pool_fwd_block(pwin, pw_ref, b, rb, g, gd, POOL_WINDOWS[g], i * tl + b)
                db16 = d.astype(BF16)
                dypg = dyp[:, g * gd:(g + 1) * gd]
                ypre = jnp.dot(db16, pw_ref[g].astype(BF16), preferred_element_type=F32)
                dps8[:, g * gd:(g + 1) * gd] += _rowsum8(dypg * ypre)
                dypre = (dypg * ps_ref[:, g * gd:(g + 1) * gd]).astype(BF16)
                dpw_ref[g] += lax.dot_general(db16, dypre, (((0,), (0,)), ((), ())), preferred_element_type=F32)
            return carry

        lax.fori_loop(0, tl // rb, blocks, 0)

        dyn = dyn_ref[...]
        du1n = conv_side(u1n_ref[...], dyn[:, :cc], False)
        du1w[tl:tl + HALO, :] = jnp.where(has_next, du1n, 0.0)
        ddn, en = pool_side(dyn[:, cc:], (i + 1) * tl, HALO)
        ew[tl:tl + HALO, :] = jnp.where(has_next, en, 0.0)

        def taps(c, carry):
            b = pl.multiple_of(c * ROW_CHUNK, SUBLANES)
            w = du1w[pl.ds(b, ROW_CHUNK + CONV_PAD), :]
            acc = jnp.zeros((ROW_CHUNK, cc), F32)
            for j in range(kw):
                o = kw - 1 - j
                acc = acc + ck_ref[j:j + 1, :] * w[o:o + ROW_CHUNK]
            zc = zc_ref[pl.ds(b, ROW_CHUNK), :]
            a = zc[:, :cc]
            sg = _sigmoid(zc[:, cc:2 * cc])
            dz_ref[pl.ds(b, ROW_CHUNK), 0:cc] = (acc * sg).astype(dz_ref.dtype)
            dz_ref[pl.ds(b, ROW_CHUNK), cc:2 * cc] = (acc * a * sg * (1.0 - sg)).astype(dz_ref.dtype)
            du1c = w[0:ROW_CHUNK]
            uw = u0w[pl.ds(b + HALO - CONV_PAD, ROW_CHUNK + CONV_PAD), :]
            for j in range(kw):
                dkacc[j] += _rowsum8(du1c * uw[tap0 + j:tap0 + j + ROW_CHUNK])
            return carry

        lax.fori_loop(0, tl // ROW_CHUNK, taps, 0)

        def pool_back(k, carry):
            b = pl.multiple_of(k * rb, SUBLANES)
            n = rb + POOL_PAD
            for g in range(ng):
                s = ew[pl.ds(b, n), g * gd:(g + 1) * gd]
                sh = 1
                while sh < POOL_WINDOWS[g]:
                    s = s + pltpu.roll(s, n - sh, axis=0)
                    sh *= 2
                dp = s[0:rb] - ddw[pl.ds(b, rb), g * gd:(g + 1) * gd]
                dz_ref[pl.ds(b, rb), 2 * cc + g * gd:2 * cc + (g + 1) * gd] = dp.astype(dz_ref.dtype)
            return carry

        lax.fori_loop(0, tl // rb, pool_back, 0)

        dck_ref[...] += jnp.sum(dkacc[...], axis=1)
        dcb_ref[...] += jnp.sum(dcb8[...], axis=0, keepdims=True)
        dlg_ref[...] += jnp.sum(dlg8[...], axis=0, keepdims=True)
        dlb_ref[...] += jnp.sum(dlb8[...], axis=0, keepdims=True)
        dps_ref[...] += jnp.sum(dps8[...], axis=0, keepdims=True)

    def full(a):
        nd = a.ndim
        return pl.BlockSpec(a.shape, lambda i: (0,) * nd)

    nhb = L // HALO

    def prev_map(i):
        return (jnp.maximum(i * hb - 1, 0), 0)

    def next_map(i):
        return (jnp.minimum((i + 1) * hb, nhb - 1), 0)

    dcc = cc + cp
    row_cc = jax.ShapeDtypeStruct((1, cc), F32)
    out_shape = (jax.ShapeDtypeStruct((L, ci), BF16), jax.ShapeDtypeStruct((kw, cc), F32), row_cc, row_cc, row_cc,
                 jax.ShapeDtypeStruct((ng, gd, gd), F32), jax.ShapeDtypeStruct((1, cp), F32))
    acc_spec = [pl.BlockSpec((kw, cc), lambda i: (0, 0))] + [pl.BlockSpec((1, cc), lambda i: (0, 0))] * 3 + [
        pl.BlockSpec((ng, gd, gd), lambda i: (0, 0, 0)), pl.BlockSpec((1, cp), lambda i: (0, 0))]
    return pl.pallas_call(
        body, name=name, grid=(nt,),
        in_specs=[pl.BlockSpec((HALO, ci), prev_map), pl.BlockSpec((tl, ci), lambda i: (i, 0)),
                  pl.BlockSpec((tl, cc), lambda i: (i, 0)), pl.BlockSpec((HALO, cc), next_map),
                  pl.BlockSpec((tl, dcc), lambda i: (i, 0)), pl.BlockSpec((HALO, dcc), next_map),
                  full(ck), full(lg), full(lb), full(pw), full(ps), full(am)],
        out_specs=tuple([pl.BlockSpec((tl, ci), lambda i: (i, 0))] + acc_spec),
        out_shape=out_shape,
        scratch_shapes=[pltpu.VMEM((HALO + tl, cc), F32), pltpu.VMEM((HALO + tl, cp), F32),
                        pltpu.VMEM((tl + HALO, cc), F32), pltpu.VMEM((tl, cp), F32), pltpu.VMEM((tl + HALO, cp), F32),
                        pltpu.VMEM((kw, SUBLANES, cc), F32), pltpu.VMEM((SUBLANES, cc), F32),
                        pltpu.VMEM((SUBLANES, cc), F32), pltpu.VMEM((SUBLANES, cc), F32), pltpu.VMEM((SUBLANES, cp), F32)],
        compiler_params=_params(("arbitrary",)),
    )(z, z, u1, u1, dy, dy, ck, lg, lb, pw, ps, am)


def _ffn_tiles(L, f):
    return _token_tile(L), _divisor(f, 256, 128)


def _ffn_conv(win, k_ref, kw, rows):
    o = FFN_PAD - (kw - 1)
    acc = k_ref[0:1, :] * win[o:o + rows]
    for j in range(1, kw):
        acc = acc + k_ref[j:j + 1, :] * win[o + j:o + j + rows]
    return acc


def _ffn_fwd(ug0, kf, name):
    L, f2 = ug0.shape
    f = f2 // 2
    kw = kf.shape[0]
    tl, tc = _ffn_tiles(L, f)
    nj = f // tc
    hb = tl // FFN_PAD

    def body(gp_ref, gc_ref, vp_ref, vc_ref, kg_ref, kv_ref, o_ref):
        i = pl.program_id(1)
        keep = i > 0

        def block(b, gwin, vwin):
            gate = _ffn_conv(gwin, kg_ref, kw, ROW_CHUNK)
            val = _ffn_conv(vwin, kv_ref, kw, ROW_CHUNK)
            o_ref[pl.ds(b, ROW_CHUNK), :] = ((gate * _sigmoid(gate)) * val).astype(o_ref.dtype)

        block(0, jnp.concatenate([jnp.where(keep, gp_ref[...], 0.0), gc_ref[0:ROW_CHUNK]], axis=0),
              jnp.concatenate([jnp.where(keep, vp_ref[...], 0.0), vc_ref[0:ROW_CHUNK]], axis=0))

        def chunk(c, carry):
            b = pl.multiple_of(c * ROW_CHUNK, SUBLANES)
            block(b, gc_ref[pl.ds(b - FFN_PAD, ROW_CHUNK + FFN_PAD), :], vc_ref[pl.ds(b - FFN_PAD, ROW_CHUNK + FFN_PAD), :])
            return carry

        lax.fori_loop(1, tl // ROW_CHUNK, chunk, 0)

    def prev_g(j, i):
        return (jnp.maximum(i * hb - 1, 0), j)

    def prev_v(j, i):
        return (jnp.maximum(i * hb - 1, 0), j + nj)

    return pl.pallas_call(
        body, name=name, grid=(nj, L // tl),
        in_specs=[pl.BlockSpec((FFN_PAD, tc), prev_g), pl.BlockSpec((tl, tc), lambda j, i: (i, j)),
                  pl.BlockSpec((FFN_PAD, tc), prev_v), pl.BlockSpec((tl, tc), lambda j, i: (i, j + nj)),
                  pl.BlockSpec((kw, tc), lambda j, i: (0, j)), pl.BlockSpec((kw, tc), lambda j, i: (0, j + nj))],
        out_specs=pl.BlockSpec((tl, tc), lambda j, i: (i, j)),
        out_shape=jax.ShapeDtypeStruct((L, f), BF16),
        compiler_params=_params(("parallel", "parallel")),
    )(ug0, ug0, ug0, ug0, kf, kf)


def _ffn_bwd(ug0, dact, kf, name):
    L, f2 = ug0.shape
    f = f2 // 2
    kw = kf.shape[0]
    tl, tc = _ffn_tiles(L, f)
    nj = f // tc
    nt = L // tl
    hb = tl // FFN_PAD
    nhb = L // FFN_PAD

    def body(gp_ref, gc_ref, gn_ref, vp_ref, vc_ref, vn_ref, dc_ref, dn_ref, kg_ref, kv_ref, du_ref, dk_ref,
             gwin, vwin, dgw, dvw, dk8):
        i = pl.program_id(1)
        keep = i > 0
        has_next = i < nt - 1

        @pl.when(i == 0)
        def _():
            dk_ref[...] = jnp.zeros_like(dk_ref)

        dk8[...] = jnp.zeros_like(dk8)

        gwin[0:FFN_PAD, :] = jnp.where(keep, gp_ref[...], 0.0)
        vwin[0:FFN_PAD, :] = jnp.where(keep, vp_ref[...], 0.0)
        gwin[FFN_PAD + tl:FFN_PAD + tl + FFN_PAD, :] = gn_ref[...]
        vwin[FFN_PAD + tl:FFN_PAD + tl + FFN_PAD, :] = vn_ref[...]

        def copy(c, carry):
            b = pl.multiple_of(c * ROW_CHUNK, SUBLANES)
            gwin[pl.ds(FFN_PAD + b, ROW_CHUNK), :] = gc_ref[pl.ds(b, ROW_CHUNK), :]
            vwin[pl.ds(FFN_PAD + b, ROW_CHUNK), :] = vc_ref[pl.ds(b, ROW_CHUNK), :]
            return carry

        lax.fori_loop(0, tl // ROW_CHUNK, copy, 0)

        def grads(b, rows, dact):
            gate = _ffn_conv(gwin[pl.ds(b, rows + FFN_PAD), :], kg_ref, kw, rows)
            val = _ffn_conv(vwin[pl.ds(b, rows + FFN_PAD), :], kv_ref, kw, rows)
            sg = _sigmoid(gate)
            return dact * val * (sg * (1.0 + gate * (1.0 - sg))), dact * (gate * sg)

        def first(c, carry):
            b = pl.multiple_of(c * ROW_CHUNK, SUBLANES)
            dg, dv = grads(b, ROW_CHUNK, dc_ref[pl.ds(b, ROW_CHUNK), :])
            dgw[pl.ds(b, ROW_CHUNK), :] = dg
            dvw[pl.ds(b, ROW_CHUNK), :] = dv
            return carry

        lax.fori_loop(0, tl // ROW_CHUNK, first, 0)
        dgn, dvn = grads(tl, FFN_PAD, dn_ref[...])
        dgw[tl:tl + FFN_PAD, :] = jnp.where(has_next, dgn, 0.0)
        dvw[tl:tl + FFN_PAD, :] = jnp.where(has_next, dvn, 0.0)

        def second(c, carry):
            b = pl.multiple_of(c * ROW_CHUNK, SUBLANES)
            for h, (dw, xwin, k_ref) in enumerate(((dgw, gwin, kg_ref), (dvw, vwin, kv_ref))):
                w = dw[pl.ds(b, ROW_CHUNK + FFN_PAD), :]
                acc = k_ref[kw - 1:kw, :] * w[0:ROW_CHUNK]
                for j in range(kw - 1):
                    o = kw - 1 - j
                    acc = acc + k_ref[j:j + 1, :] * w[o:o + ROW_CHUNK]
                du_ref[h, pl.ds(b, ROW_CHUNK), :] = acc.astype(du_ref.dtype)
                xw = xwin[pl.ds(b, ROW_CHUNK + FFN_PAD), :]
                o0 = FFN_PAD - (kw - 1)
                for j in range(kw):
                    dk8[h, j] += _rowsum8(w[0:ROW_CHUNK] * xw[o0 + j:o0 + j + ROW_CHUNK])
            return carry

        lax.fori_loop(0, tl // ROW_CHUNK, second, 0)
        dk_ref[...] += jnp.sum(dk8[...], axis=2)

    def prev_g(j, i):
        return (jnp.maximum(i * hb - 1, 0), j)

    def prev_v(j, i):
        return (jnp.maximum(i * hb - 1, 0), j + nj)

    def next_g(j, i):
        return (jnp.minimum((i + 1) * hb, nhb - 1), j)

    def next_v(j, i):
        return (jnp.minimum((i + 1) * hb, nhb - 1), j + nj)

    return pl.pallas_call(
        body, name=name, grid=(nj, nt),
        in_specs=[pl.BlockSpec((FFN_PAD, tc), prev_g), pl.BlockSpec((tl, tc), lambda j, i: (i, j)),
                  pl.BlockSpec((FFN_PAD, tc), next_g),
                  pl.BlockSpec((FFN_PAD, tc), prev_v), pl.BlockSpec((tl, tc), lambda j, i: (i, j + nj)),
                  pl.BlockSpec((FFN_PAD, tc), next_v),
                  pl.BlockSpec((tl, tc), lambda j, i: (i, j)), pl.BlockSpec((FFN_PAD, tc), next_g),
                  pl.BlockSpec((kw, tc), lambda j, i: (0, j)), pl.BlockSpec((kw, tc), lambda j, i: (0, j + nj))],
        out_specs=(pl.BlockSpec((2, tl, tc), lambda j, i: (0, i, j)), pl.BlockSpec((2, kw, tc), lambda j, i: (0, 0, j))),
        out_shape=(jax.ShapeDtypeStruct((2, L, f), BF16), jax.ShapeDtypeStruct((2, kw, f), F32)),
        scratch_shapes=[pltpu.VMEM((tl + 2 * FFN_PAD, tc), F32), pltpu.VMEM((tl + 2 * FFN_PAD, tc), F32),
                        pltpu.VMEM((tl + FFN_PAD, tc), F32), pltpu.VMEM((tl + FFN_PAD, tc), F32),
                        pltpu.VMEM((2, kw, SUBLANES, tc), F32)],
        compiler_params=_params(("parallel", "arbitrary")),
    )(ug0, ug0, ug0, ug0, ug0, ug0, dact, dact, kf, kf)


def _adamw_math(w, g, m, v):
    m = ADAM_B1 * m + (1.0 - ADAM_B1) * g
    v = ADAM_B2 * v + (1.0 - ADAM_B2) * (g * g)
    m_hat = m / (1.0 - ADAM_B1 ** ADAM_STEP)
    v_hat = v / (1.0 - ADAM_B2 ** ADAM_STEP)
    delta = -ADAM_LR * (m_hat / (jnp.sqrt(v_hat) + ADAM_EPS) + ADAM_WD * w)
    return delta, m, v


def _sum_parts(parts_ref, idx):
    g = parts_ref[(0,) + idx].astype(F32)
    for q in range(1, N_DEV):
        g = g + parts_ref[(q,) + idx].astype(F32)
    return g


def _adamw_big(parts, w, m, v, name):
    nl, R, C = w.shape
    tr = _divisor(R, 256, SUBLANES)

    def body(p_ref, w_ref, m_ref, v_ref, g_ref, d_ref, nm_ref, nv_ref):
        g = _sum_parts(p_ref, (0,))
        d, nm, nv = _adamw_math(w_ref[0], g, m_ref[0], v_ref[0])
        g_ref[0] = g
        d_ref[0] = d
        nm_ref[0] = nm
        nv_ref[0] = nv

    blk = pl.BlockSpec((1, tr, C), lambda l, r: (l, r, 0))
    shp = jax.ShapeDtypeStruct((nl, R, C), F32)
    return pl.pallas_call(
        body, name=name, grid=(nl, R // tr),
        in_specs=[pl.BlockSpec((N_DEV, 1, tr, C), lambda l, r: (0, l, r, 0)), blk, blk, blk],
        out_specs=(blk, blk, blk, blk), out_shape=(shp, shp, shp, shp),
        compiler_params=_params(("parallel", "parallel")),
    )(parts, w, m, v)


def _adamw_small(entries, name):
    n = len(entries)
    uniq = []
    for e in entries:
        if not any(e[0] is u for u in uniq):
            uniq.append(e[0])
    pidx = [next(k for k, u in enumerate(uniq) if u is e[0]) for e in entries]
    npart = len(uniq)

    def body(*refs):
        p_refs = refs[:npart]
        wmv = refs[npart:npart + 3 * n]
        outs = refs[npart + 3 * n:]
        for t, e in enumerate(entries):
            lo, w = e[1], e[2]
            rows = w.shape[0]
            pr = p_refs[pidx[t]]
            g = pr[0, lo:lo + rows].astype(F32)
            for q in range(1, N_DEV):
                g = g + pr[q, lo:lo + rows].astype(F32)
            d, nm, nv = _adamw_math(wmv[3 * t][...], g, wmv[3 * t + 1][...], wmv[3 * t + 2][...])
            outs[4 * t][...] = g
            outs[4 * t + 1][...] = d
            outs[4 * t + 2][...] = nm
            outs[4 * t + 3][...] = nv

    vm = pl.BlockSpec(memory_space=pltpu.VMEM)
    args = list(uniq)
    out_shape = []
    for e in entries:
        args += [e[2], e[3], e[4]]
        out_shape += [jax.ShapeDtypeStruct(e[2].shape, F32)] * 4
    res = pl.pallas_call(
        body, name=name, in_specs=[vm] * len(args), out_specs=tuple([vm] * len(out_shape)),
        out_shape=tuple(out_shape), compiler_params=_params(),
    )(*args)
    return [tuple(res[4 * t:4 * t + 4]) for t in range(n)]


def _head_matrix(cc):
    bw = min(256, cc)
    r = lax.broadcasted_iota(jnp.int32, (bw, bw), 0) // HEAD_DIM
    c = lax.broadcasted_iota(jnp.int32, (bw, bw), 1) // HEAD_DIM
    return jnp.where(r == c, 1.0 / HEAD_DIM, 0.0).astype(BF16)


def _cols_from_shards(g):
    nd = g.ndim
    perm = tuple(range(1, nd - 1)) + (0, nd - 1)
    t = jnp.transpose(g, perm)
    return t.reshape(t.shape[:-2] + (t.shape[-2] * t.shape[-1],))


def _cols_to_shards(a):
    nd = a.ndim
    t = a.reshape(a.shape[:-1] + (N_DEV, a.shape[-1] // N_DEV))
    perm = (nd - 1,) + tuple(range(nd - 1)) + (nd,)
    return jnp.transpose(t, perm)


def kernel(x, meta_tokens, norm1_g, w_in, conv_dw_k, conv_dw_b, conv_ln_g, conv_ln_b, pool_w, pool_scale, w_out, norm2_g, w_up, ffn_dw_k, w_down, final_g, loss_target, m_meta_tokens, m_norm1_g, m_w_in, m_conv_dw_k, m_conv_dw_b, m_conv_ln_g, m_conv_ln_b, m_pool_w, m_pool_scale, m_w_out, m_norm2_g, m_w_up, m_ffn_dw_k, m_w_down, m_final_g, v_meta_tokens, v_norm1_g, v_w_in, v_conv_dw_k, v_conv_dw_b, v_conv_ln_g, v_conv_ln_b, v_pool_w, v_pool_scale, v_w_out, v_norm2_g, v_w_up, v_ffn_dw_k, v_w_down, v_final_g):
    depth, D = norm1_g.shape
    n_meta = meta_tokens.shape[0]
    seq = x.shape[1]
    L = n_meta + seq
    cc = conv_dw_b.shape[1]
    ng, gd = pool_w.shape[1], pool_w.shape[2]
    f = w_down.shape[1] * N_DEV

    g_in, g_out, g_up, g_down, g_ck, g_kf, g_meta = _exchange(
        [w_in.astype(BF16), w_out.astype(BF16), w_up.astype(BF16), w_down.astype(BF16), conv_dw_k, ffn_dw_k, meta_tokens],
        ["gather"] * 7, "gather_weights")
    win = _cols_from_shards(g_in)
    win_t = jnp.swapaxes(win, 1, 2)
    wout = jnp.transpose(g_out, (1, 0, 2, 3)).reshape(depth, -1, D)
    wout_t = jnp.swapaxes(wout, 1, 2)
    wup = _cols_from_shards(g_up)
    wup_t = jnp.swapaxes(wup, 1, 2)
    wdown = jnp.transpose(g_down, (1, 0, 2, 3)).reshape(depth, f, D)
    wdown_t = jnp.swapaxes(wdown, 1, 2)
    ck_full = _cols_from_shards(g_ck)
    kf_full = _cols_from_shards(g_kf)
    meta_full = _cols_from_shards(g_meta)
    am = _head_matrix(cc)

    h = jnp.concatenate([meta_full, x[0]], axis=0)
    saved = []
    for l in range(depth):
        hn1 = _rms_fwd(h, norm1_g[l:l + 1], f"rms1_fwd_{l}")
        z = _mm(hn1, win[l], f"in_proj_{l}", tn_cap=768)
        ymix, u1 = _mixer_fwd(z, ck_full[l], conv_dw_b[l:l + 1], conv_ln_g[l:l + 1], conv_ln_b[l:l + 1], pool_w[l],
                              pool_scale[l:l + 1], am, f"mixer_fwd_{l}")
        h_mid = _mm(ymix, wout[l], f"out_proj_{l}", res=h, tn_cap=512)
        hn2 = _rms_fwd(h_mid, norm2_g[l:l + 1], f"rms2_fwd_{l}")
        ug0 = _mm(hn2, wup[l], f"up_proj_{l}")
        act = _ffn_fwd(ug0, kf_full[l], f"ffn_fwd_{l}")
        h_out = _mm(act, wdown[l], f"down_proj_{l}", res=h_mid, tn_cap=512)
        saved.append((h, hn1, z, u1, ymix, h_mid, hn2, ug0, act))
        h = h_out

    tgt = jnp.concatenate([jnp.zeros((n_meta, D), F32), loss_target[0]], axis=0)
    dh, d_final_g, loss_part = _loss_head(h, final_g.reshape(1, D), tgt, n_meta, "loss_head")

    gw = {k: [None] * depth for k in ("in", "out", "up", "down", "ck", "cb", "lg", "lb", "pw", "ps", "kf", "n1", "n2")}
    for l in reversed(range(depth)):
        h_in, hn1, z, u1, ymix, h_mid, hn2, ug0, act = saved[l]
        dact = _mm(dh, wdown_t[l], f"down_proj_bwd_{l}")
        gw["down"][l] = _mm_tn(act, dh, f"down_proj_wgrad_{l}", tq_cap=512)
        dug0, dkf = _ffn_bwd(ug0, dact, kf_full[l], f"ffn_bwd_{l}")
        gw["kf"][l] = jnp.concatenate([dkf[0], dkf[1]], axis=-1)
        dhn2 = _mm(dug0, wup_t[l], f"up_proj_bwd_{l}", halves=2, tn_cap=512)
        gw["up"][l] = _mm_tn(hn2, dug0, f"up_proj_wgrad_{l}", halves=2)
        dh_mid, gw["n2"][l] = _rms_bwd(h_mid, norm2_g[l:l + 1], dhn2, dh, f"rms2_bwd_{l}")
        dymix = _mm(dh_mid, wout_t[l], f"out_proj_bwd_{l}", tn_cap=512)
        gw["out"][l] = _mm_tn(ymix, dh_mid, f"out_proj_wgrad_{l}", tq_cap=512)
        dz, gw["ck"][l], gw["cb"][l], gw["lg"][l], gw["lb"][l], gw["pw"][l], gw["ps"][l] = _mixer_bwd(
            z, u1, dymix, ck_full[l], conv_ln_g[l:l + 1], conv_ln_b[l:l + 1], pool_w[l], pool_scale[l:l + 1], am,
            f"mixer_bwd_{l}")
        dhn1 = _mm(dz, win_t[l], f"in_proj_bwd_{l}", tn_cap=512)
        gw["in"][l] = _mm_tn(hn1, dz, f"in_proj_wgrad_{l}", tq_cap=768)
        dh, gw["n1"][l] = _rms_bwd(h_in, norm1_g[l:l + 1], dhn1, dh_mid, f"rms1_bwd_{l}")
    grad_x = dh[n_meta:][None]
    d_meta = dh[:n_meta]

    zero_row = jnp.zeros((1, D), F32)
    pack_d = jnp.concatenate(gw["n1"] + gw["n2"] + [d_final_g, jnp.broadcast_to(loss_part[:, :1], (1, D)), zero_row, zero_row], axis=0)
    pack_c = jnp.concatenate(gw["cb"] + gw["lg"] + gw["lb"] + gw["ps"], axis=0)
    pack_pw = jnp.stack(gw["pw"]).reshape(depth * ng * gd, gd)
    src = [
        _cols_to_shards(jnp.stack(gw["in"])).astype(BF16),
        jnp.stack(gw["out"]).reshape(depth, N_DEV, -1, D).transpose(1, 0, 2, 3).astype(BF16),
        _cols_to_shards(jnp.stack(gw["up"])).astype(BF16),
        jnp.stack(gw["down"]).reshape(depth, N_DEV, -1, D).transpose(1, 0, 2, 3).astype(BF16),
        _cols_to_shards(jnp.stack(gw["ck"])),
        _cols_to_shards(jnp.stack(gw["kf"])),
        _cols_to_shards(d_meta),
        pack_d, pack_c, pack_pw,
    ]
    r_in, r_out, r_up, r_down, r_ck, r_kf, r_meta, r_d, r_c, r_pw = _exchange(
        src, ["a2a"] * 7 + ["gather"] * 3, "exchange_grads")

    big = {
        "w_in": _adamw_big(r_in, w_in, m_w_in, v_w_in, "adamw_w_in"),
        "w_out": _adamw_big(r_out, w_out, m_w_out, v_w_out, "adamw_w_out"),
        "w_up": _adamw_big(r_up, w_up, m_w_up, v_w_up, "adamw_w_up"),
        "w_down": _adamw_big(r_down, w_down, m_w_down, v_w_down, "adamw_w_down"),
    }
    kwid = conv_dw_k.shape[1]
    fkw = ffn_dw_k.shape[1]
    row = lambda a: a.reshape(1, -1)
    entries = [
        (r_d, 0, norm1_g, m_norm1_g, v_norm1_g),
        (r_d, depth, norm2_g, m_norm2_g, v_norm2_g),
        (r_d, 2 * depth, row(final_g), row(m_final_g), row(v_final_g)),
        (r_c, 0, conv_dw_b, m_conv_dw_b, v_conv_dw_b),
        (r_c, depth, conv_ln_g, m_conv_ln_g, v_conv_ln_g),
        (r_c, 2 * depth, conv_ln_b, m_conv_ln_b, v_conv_ln_b),
        (r_c, 3 * depth, pool_scale, m_pool_scale, v_pool_scale),
        (r_pw, 0, pool_w.reshape(-1, gd), m_pool_w.reshape(-1, gd), v_pool_w.reshape(-1, gd)),
        (r_ck.reshape(N_DEV, depth * kwid, -1), 0, conv_dw_k.reshape(depth * kwid, -1),
         m_conv_dw_k.reshape(depth * kwid, -1), v_conv_dw_k.reshape(depth * kwid, -1)),
        (r_kf.reshape(N_DEV, depth * fkw, -1), 0, ffn_dw_k.reshape(depth * fkw, -1),
         m_ffn_dw_k.reshape(depth * fkw, -1), v_ffn_dw_k.reshape(depth * fkw, -1)),
        (r_meta, 0, meta_tokens, m_meta_tokens, v_meta_tokens),
        (r_d, 2 * depth + 1, zero_row, zero_row, zero_row),
    ]
    small = _adamw_small(entries, "adamw_small")
    names = ["norm1_g", "norm2_g", "final_g", "conv_dw_b", "conv_ln_g", "conv_ln_b", "pool_scale", "pool_w",
             "conv_dw_k", "ffn_dw_k", "meta_tokens"]
    shapes = {"final_g": final_g.shape, "pool_w": pool_w.shape, "conv_dw_k": conv_dw_k.shape, "ffn_dw_k": ffn_dw_k.shape}
    res = dict(big)
    for nme, quad in zip(names, small[:-1]):
        res[nme] = tuple(a.reshape(shapes[nme]) if nme in shapes else a for a in quad)
    loss = small[-1][0][0, 0]

    order = ["meta_tokens", "norm1_g", "w_in", "conv_dw_k", "conv_dw_b", "conv_ln_g", "conv_ln_b", "pool_w", "pool_scale",
             "w_out", "norm2_g", "w_up", "ffn_dw_k", "w_down", "final_g"]
    return (loss, grad_x, *[res[k][0] for k in order], *[res[k][1] for k in order], *[res[k][2] for k in order],
            *[res[k][3] for k in order])
```

```python
import functools

import jax
import jax.numpy as jnp
from jax import lax
from jax.experimental import pallas as pl
from jax.experimental.pallas import tpu as pltpu

F32 = jnp.float32
BF16 = jnp.bfloat16

EPS = 1e-6
HEAD_DIM = 64
POOL_WINDOWS = (2, 4, 8, 16)
ADAM_LR = 0.001
ADAM_B1 = 0.9
ADAM_B2 = 0.999
ADAM_EPS = 1e-08
ADAM_WD = 0.01
ADAM_STEP = 10

N_DEV = 8
SUBLANES = 8
HALO = 48
CONV_PAD = 32
POOL_PAD = 16
FFN_PAD = 8
ROW_CHUNK = 24
MAX_TILE_ROWS = 1024
VMEM_LIMIT = 52 * 1024 * 1024


def _divisor(n, cap, mult):
    best = None
    for d in range(mult, min(n, cap) + 1, mult):
        if n % d == 0:
            best = d
    return n if best is None else best


def _token_tile(L):
    return _divisor(L, MAX_TILE_ROWS, HALO)


def _row_tile(L):
    return _divisor(L, 320, 2 * SUBLANES)


def _stat_rows(tl):
    return _divisor(tl, 256, SUBLANES)


def _params(sem=None):
    return pltpu.CompilerParams(dimension_semantics=sem, vmem_limit_bytes=VMEM_LIMIT)


def _rowsum8(x):
    acc = x[0:SUBLANES]
    for k in range(1, x.shape[0] // SUBLANES):
        acc = acc + x[k * SUBLANES:(k + 1) * SUBLANES]
    return acc


def _sigmoid(x):
    return jax.nn.sigmoid(x)


def _head_mean(x, am_ref):
    bw = am_ref.shape[0]
    am = am_ref[...]
    outs = []
    for blk in range(x.shape[1] // bw):
        xb = x[:, blk * bw:(blk + 1) * bw]
        hi = xb.astype(BF16)
        lo = (xb - hi.astype(F32)).astype(BF16)
        outs.append(jnp.dot(hi, am, preferred_element_type=F32) + jnp.dot(lo, am, preferred_element_type=F32))
    return outs[0] if len(outs) == 1 else jnp.concatenate(outs, axis=-1)


def _xchg_out_shapes(srcs, modes):
    out = []
    for s, m in zip(srcs, modes):
        shp = ((N_DEV,) + tuple(s.shape)) if m == "gather" else tuple(s.shape)
        out.append(jax.ShapeDtypeStruct(shp, s.dtype))
    return out


def _xchg_sems(n):
    return [pltpu.SemaphoreType.DMA((n, N_DEV - 1)), pltpu.SemaphoreType.DMA((n, N_DEV - 1)), pltpu.SemaphoreType.DMA((n,))]


def _xchg_ops(src_refs, out_refs, sems, modes):
    n = len(src_refs)
    send_sems, recv_sems, local_sems = sems
    x, y, c = lax.axis_index("x"), lax.axis_index("y"), lax.axis_index("c")
    me = 4 * x + 2 * y + c

    def peer(d):
        return (x ^ ((d >> 2) & 1), y ^ ((d >> 1) & 1), c ^ (d & 1))

    def peer_id(d):
        px, py, pc = peer(d)
        return 4 * px + 2 * py + pc

    def remote(t, d):
        src = src_refs[t] if modes[t] == "gather" else src_refs[t].at[peer_id(d)]
        return pltpu.make_async_remote_copy(
            src_ref=src, dst_ref=out_refs[t].at[me], send_sem=send_sems.at[t, d - 1], recv_sem=recv_sems.at[t, d - 1],
            device_id=peer(d), device_id_type=pl.DeviceIdType.MESH)

    def arrival(t, d):
        src = src_refs[t] if modes[t] == "gather" else src_refs[t].at[me]
        return pltpu.make_async_remote_copy(
            src_ref=src, dst_ref=out_refs[t].at[peer_id(d)], send_sem=send_sems.at[t, d - 1],
            recv_sem=recv_sems.at[t, d - 1], device_id=peer(d), device_id_type=pl.DeviceIdType.MESH)

    def local(t):
        src = src_refs[t] if modes[t] == "gather" else src_refs[t].at[me]
        return pltpu.make_async_copy(src, out_refs[t].at[me], local_sems.at[t])

    def start():
        for t in range(n):
            local(t).start()
        for t in range(n):
            for d in range(1, N_DEV):
                remote(t, d).start()

    def wait():
        for t in range(n):
            for d in range(1, N_DEV):
                arrival(t, d).wait_recv()
        for t in range(n):
            for d in range(1, N_DEV):
                remote(t, d).wait_send()
        for t in range(n):
            local(t).wait()

    return start, wait


def _exchange(srcs, modes, name):
    n = len(srcs)

    def body(*refs):
        start, wait = _xchg_ops(refs[:n], refs[n:2 * n], refs[2 * n:], modes)
        start()
        wait()

    any_spec = pl.BlockSpec(memory_space=pl.ANY)
    return pl.pallas_call(
        body, name=name, out_shape=tuple(_xchg_out_shapes(srcs, modes)),
        in_specs=[any_spec] * n, out_specs=tuple([any_spec] * n),
        scratch_shapes=_xchg_sems(n),
        compiler_params=pltpu.CompilerParams(has_side_effects=True),
    )(*srcs)


def _call(body, *, name, grid, in_specs, out_specs, out_shape, args, scratch_shapes=(), sem=None, xchg=None):
    single = not isinstance(out_shape, (tuple, list))
    outs_shape = [out_shape] if single else list(out_shape)
    outs_spec = [out_specs] if single else list(out_specs)
    if xchg is None:
        res = pl.pallas_call(
            body, name=name, grid=grid, in_specs=list(in_specs), out_specs=out_specs, out_shape=out_shape,
            scratch_shapes=list(scratch_shapes), compiler_params=_params(sem))(*args)
        return res, ()
    srcs, modes = xchg
    n_in, n_out, n_scr, nx = len(in_specs), len(outs_shape), len(scratch_shapes), len(srcs)

    def wrapped(*refs):
        ins = refs[:n_in]
        xs = refs[n_in:n_in + nx]
        o0 = n_in + nx
        outs = refs[o0:o0 + n_out]
        xo = refs[o0 + n_out:o0 + n_out + nx]
        s0 = o0 + n_out + nx
        scr = refs[s0:s0 + n_scr]
        start, wait = _xchg_ops(xs, xo, refs[s0 + n_scr:], modes)
        first = functools.reduce(jnp.logical_and, [pl.program_id(a) == 0 for a in range(len(grid))])
        last = functools.reduce(jnp.logical_and, [pl.program_id(a) == grid[a] - 1 for a in range(len(grid))])

        @pl.when(first)
        def _():
            start()

        body(*ins, *outs, *scr)

        @pl.when(last)
        def _():
            wait()

    any_spec = pl.BlockSpec(memory_space=pl.ANY)
    res = pl.pallas_call(
        wrapped, name=name, grid=grid, in_specs=list(in_specs) + [any_spec] * nx,
        out_specs=tuple(outs_spec + [any_spec] * nx), out_shape=tuple(outs_shape + _xchg_out_shapes(srcs, modes)),
        scratch_shapes=list(scratch_shapes) + _xchg_sems(nx),
        compiler_params=_params(("arbitrary",) * len(grid)))(*args, *srcs)
    comp = res[:n_out]
    return (comp[0] if single else tuple(comp)), tuple(res[n_out:])


def _rms_fwd(h, g, name):
    L, D = h.shape
    tl = _row_tile(L)

    def body(h_ref, g_ref, o_ref):
        x = h_ref[...]
        r = lax.rsqrt(jnp.mean(x * x, axis=-1, keepdims=True) + EPS)
        o_ref[...] = ((x * r) * g_ref[...]).astype(o_ref.dtype)

    return pl.pallas_call(
        body, name=name, grid=(L // tl,),
        in_specs=[pl.BlockSpec((tl, D), lambda i: (i, 0)), pl.BlockSpec((1, D), lambda i: (0, 0))],
        out_specs=pl.BlockSpec((tl, D), lambda i: (i, 0)),
        out_shape=jax.ShapeDtypeStruct((L, D), BF16),
        compiler_params=_params(("parallel",)),
    )(h, g)


def _mm(a, b, name, *, res=None, out_dtype=F32, tn_cap=1408, halves=1, xchg=None):
    if halves > 1:
        _, M, kh = a.shape
        K = kh * halves
    else:
        M, K = a.shape
        kh = K
    N = b.shape[1]
    tm = _token_tile(M)
    tn = _divisor(N, tn_cap, 128)
    tk = kh if kh <= 2816 else _divisor(kh, 2816, 128)
    kper = kh // tk
    nk = halves * kper
    grid = (M // tm, N // tn, nk)

    def body(*refs):
        if res is None:
            a_ref, b_ref, o_ref = refs[:3]
            r_ref = None
            scratch = refs[3:]
        else:
            a_ref, b_ref, r_ref, o_ref = refs[:4]
            scratch = refs[4:]
        av = a_ref[0] if halves > 1 else a_ref[...]
        prod = jnp.dot(av.astype(BF16), b_ref[...], preferred_element_type=F32)
        if nk == 1:
            if r_ref is not None:
                prod = prod + r_ref[...]
            o_ref[...] = prod.astype(o_ref.dtype)
        else:
            acc = scratch[0]
            k = pl.program_id(2)

            @pl.when(k == 0)
            def _():
                acc[...] = prod

            @pl.when(k > 0)
            def _():
                acc[...] += prod

            @pl.when(k == nk - 1)
            def _():
                tot = acc[...]
                if r_ref is not None:
                    tot = tot + r_ref[...]
                o_ref[...] = tot.astype(o_ref.dtype)

    if halves > 1:
        a_spec = pl.BlockSpec((1, tm, tk), lambda i, j, k: (k // kper, i, k % kper))
    else:
        a_spec = pl.BlockSpec((tm, tk), lambda i, j, k: (i, k))
    in_specs = [a_spec, pl.BlockSpec((tk, tn), lambda i, j, k: (k, j))]
    args = [a, b]
    if res is not None:
        in_specs.append(pl.BlockSpec((tm, tn), lambda i, j, k: (i, j)))
        args.append(res)
    out, xo = _call(
        body, name=name, grid=grid, in_specs=in_specs,
        out_specs=pl.BlockSpec((tm, tn), lambda i, j, k: (i, j)),
        out_shape=jax.ShapeDtypeStruct((M, N), out_dtype),
        scratch_shapes=[pltpu.VMEM((tm, tn), F32)] if nk > 1 else [],
        sem=("parallel", "parallel", "arbitrary"), args=args, xchg=xchg)
    return out if xchg is None else (out, xo)


def _mm_tn(a, b, name, *, halves=1, tq_cap=1408):
    L, P = a.shape
    if halves > 1:
        qh = b.shape[2]
        Q = qh * halves
    else:
        Q = b.shape[1]
        qh = Q
    tl = _token_tile(L)
    tp = _divisor(P, 1408, 128)
    tq = _divisor(qh, tq_cap, 128)
    qper = qh // tq
    grid = (P // tp, Q // tq, L // tl)

    def body(a_ref, b_ref, o_ref):
        bv = b_ref[0] if halves > 1 else b_ref[...]
        prod = lax.dot_general(a_ref[...].astype(BF16), bv.astype(BF16), (((0,), (0,)), ((), ())),
                               preferred_element_type=F32)
        l = pl.program_id(2)

        @pl.when(l == 0)
        def _():
            o_ref[...] = prod

        @pl.when(l > 0)
        def _():
            o_ref[...] += prod

    if halves > 1:
        b_spec = pl.BlockSpec((1, tl, tq), lambda p, q, l: (q // qper, l, q % qper))
    else:
        b_spec = pl.BlockSpec((tl, tq), lambda p, q, l: (l, q))
    return pl.pallas_call(
        body, name=name, grid=grid,
        in_specs=[pl.BlockSpec((tl, tp), lambda p, q, l: (l, p)), b_spec],
        out_specs=pl.BlockSpec((tp, tq), lambda p, q, l: (p, q)),
        out_shape=jax.ShapeDtypeStruct((P, Q), F32),
        compiler_params=_params(("parallel", "parallel", "arbitrary")),
    )(a, b)


def _rms_bwd(h, g, dhn, dres, name):
    L, D = h.shape
    tl = _row_tile(L)
    nt = L // tl

    def body(h_ref, g_ref, dhn_ref, dres_ref, dh_ref, dg_ref):
        i = pl.program_id(0)
        x = h_ref[...]
        r = lax.rsqrt(jnp.mean(x * x, axis=-1, keepdims=True) + EPS)
        xhat = x * r
        dhn = dhn_ref[...]
        dxhat = dhn * g_ref[...]
        dh_ref[...] = dres_ref[...] + r * (dxhat - xhat * jnp.mean(dxhat * xhat, axis=-1, keepdims=True))
        part = jnp.sum(_rowsum8(dhn * xhat), axis=0, keepdims=True)

        @pl.when(i == 0)
        def _():
            dg_ref[...] = part

        @pl.when(i > 0)
        def _():
            dg_ref[...] += part

    tile = pl.BlockSpec((tl, D), lambda i: (i, 0))
    row = pl.BlockSpec((1, D), lambda i: (0, 0))
    return pl.pallas_call(
        body, name=name, grid=(nt,), in_specs=[tile, row, tile, tile], out_specs=(tile, row),
        out_shape=(jax.ShapeDtypeStruct((L, D), F32), jax.ShapeDtypeStruct((1, D), F32)),
        compiler_params=_params(("arbitrary",)),
    )(h, g, dhn, dres)


def _loss_head(h, g, tgt, n_meta, name):
    L, D = h.shape
    tl = _row_tile(L)
    nt = L // tl

    def body(h_ref, g_ref, t_ref, dh_ref, dg_ref, loss_ref):
        i = pl.program_id(0)
        x = h_ref[...]
        r = lax.rsqrt(jnp.mean(x * x, axis=-1, keepdims=True) + EPS)
        xhat = x * r
        gg = g_ref[...]
        y = xhat * gg
        rows = i * tl + lax.broadcasted_iota(jnp.int32, (tl, 1), 0)
        err = jnp.where(rows >= n_meta, y - t_ref[...], 0.0)
        dy = err * (1.0 / D)
        dxhat = dy * gg
        dh_ref[...] = r * (dxhat - xhat * jnp.mean(dxhat * xhat, axis=-1, keepdims=True))
        dg_part = jnp.sum(_rowsum8(dy * xhat), axis=0, keepdims=True)
        per_row = jnp.mean(err * err, axis=-1, keepdims=True)
        loss_part = jnp.broadcast_to(0.5 * jnp.sum(per_row, axis=0, keepdims=True), (1, 128))

        @pl.when(i == 0)
        def _():
            dg_ref[...] = dg_part
            loss_ref[...] = loss_part

        @pl.when(i > 0)
        def _():
            dg_ref[...] += dg_part
            loss_ref[...] += loss_part

    tile = pl.BlockSpec((tl, D), lambda i: (i, 0))
    row = pl.BlockSpec((1, D), lambda i: (0, 0))
    return pl.pallas_call(
        body, name=name, grid=(nt,), in_specs=[tile, row, tile],
        out_specs=(tile, row, pl.BlockSpec((1, 128), lambda i: (0, 0))),
        out_shape=(jax.ShapeDtypeStruct((L, D), F32), jax.ShapeDtypeStruct((1, D), F32),
                   jax.ShapeDtypeStruct((1, 128), F32)),
        compiler_params=_params(("arbitrary",)),
    )(h, g, tgt)


def _pool_fwd_block(pwin, pw_ref, row0, rb, g, gd, w, t0):
    wv = pwin[pl.ds(row0 + HALO - POOL_PAD, rb + POOL_PAD), g * gd:(g + 1) * gd]
    s = wv
    sh = 1
    while sh < w:
        s = s + pltpu.roll(s, sh, axis=0)
        sh *= 2
    win = s[POOL_PAD:POOL_PAD + rb]
    pt = wv[POOL_PAD:POOL_PAD + rb]
    tg = t0 + lax.broadcasted_iota(jnp.int32, (rb, 1), 0)
    cnt = jnp.minimum(tg + 1, w).astype(F32)
    return win / cnt - pt


def _fill_windows(i, zp_ref, zc_ref, u0w, pwin, tl, cc):
    keep = i > 0
    zp = zp_ref[...]
    u0w[0:HALO, :] = jnp.where(keep, zp[:, :cc] * _sigmoid(zp[:, cc:2 * cc]), 0.0)
    pwin[0:HALO, :] = jnp.where(keep, zp[:, 2 * cc:], 0.0)

    def fill(c, carry):
        b = pl.multiple_of(c * ROW_CHUNK, SUBLANES)
        zc = zc_ref[pl.ds(b, ROW_CHUNK), :]
        u0w[pl.ds(HALO + b, ROW_CHUNK), :] = zc[:, :cc] * _sigmoid(zc[:, cc:2 * cc])
        pwin[pl.ds(HALO + b, ROW_CHUNK), :] = zc[:, 2 * cc:]
        return carry

    lax.fori_loop(0, tl // ROW_CHUNK, fill, 0)


def _mixer_fwd(z, ck, cb, lg, lb, pw, ps, am, name, xchg=None):
    L, ci = z.shape
    kw, cc = ck.shape
    cp = ci - 2 * cc
    ng, gd = pw.shape[0], pw.shape[1]
    tl = _token_tile(L)
    nt = L // tl
    hb = tl // HALO
    rb = _stat_rows(tl)
    tap0 = CONV_PAD - (kw - 1)

    def body(zp_ref, zc_ref, ck_ref, cb_ref, lg_ref, lb_ref, pw_ref, ps_ref, am_ref, y_ref, u1_ref, u0w, pwin):
        i = pl.program_id(0)
        _fill_windows(i, zp_ref, zc_ref, u0w, pwin, tl, cc)

        def conv(c, carry):
            b = pl.multiple_of(c * ROW_CHUNK, SUBLANES)
            w = u0w[pl.ds(b + HALO - CONV_PAD, ROW_CHUNK + CONV_PAD), :]
            acc = jnp.broadcast_to(cb_ref[...], (ROW_CHUNK, cc))
            for j in range(kw):
                acc = acc + ck_ref[j:j + 1, :] * w[tap0 + j:tap0 + j + ROW_CHUNK]
            u1_ref[pl.ds(b, ROW_CHUNK), :] = acc
            return carry

        lax.fori_loop(0, tl // ROW_CHUNK, conv, 0)

        def blocks(k, carry):
            b = pl.multiple_of(k * rb, SUBLANES)
            u1 = u1_ref[pl.ds(b, rb), :]
            xc = u1 - _head_mean(u1, am_ref)
            var = _head_mean(xc * xc, am_ref)
            u2 = (xc * lax.rsqrt(var + EPS)) * lg_ref[...] + lb_ref[...]
            y_ref[pl.ds(b, rb), 0:cc] = (u2 * _sigmoid(u2)).astype(y_ref.dtype)
            for g in range(ng):
                d = _pool_fwd_block(pwin, pw_ref, b, rb, g, gd, POOL_WINDOWS[g], i * tl + b)
                yp = jnp.dot(d.astype(BF16), pw_ref[g].astype(BF16), preferred_element_type=F32)
                yp = yp * ps_ref[:, g * gd:(g + 1) * gd]
                y_ref[pl.ds(b, rb), cc + g * gd:cc + (g + 1) * gd] = yp.astype(y_ref.dtype)
            return carry

        lax.fori_loop(0, tl // rb, blocks, 0)

    def full(a):
        nd = a.ndim
        return pl.BlockSpec(a.shape, lambda i: (0,) * nd)

    out, xo = _call(
        body, name=name, grid=(nt,),
        in_specs=[pl.BlockSpec((HALO, ci), lambda i: (jnp.maximum(i * hb - 1, 0), 0)),
                  pl.BlockSpec((tl, ci), lambda i: (i, 0)),
                  full(ck), full(cb), full(lg), full(lb), full(pw), full(ps), full(am)],
        out_specs=(pl.BlockSpec((tl, cc + cp), lambda i: (i, 0)), pl.BlockSpec((tl, cc), lambda i: (i, 0))),
        out_shape=(jax.ShapeDtypeStruct((L, cc + cp), BF16), jax.ShapeDtypeStruct((L, cc), F32)),
        scratch_shapes=[pltpu.VMEM((HALO + tl, cc), F32), pltpu.VMEM((HALO + tl, cp), F32)],
        sem=("parallel",), args=(z, z, ck, cb, lg, lb, pw, ps, am), xchg=xchg)
    return out if xchg is None else (out, xo)


def _mixer_bwd(z, u1, dy, ck, lg, lb, pw, ps, am, name, xchg=None):
    L, ci = z.shape
    kw, cc = ck.shape
    cp = ci - 2 * cc
    ng, gd = pw.shape[0], pw.shape[1]
    tl = _token_tile(L)
    nt = L // tl
    hb = tl // HALO
    rb = _stat_rows(tl)
    tap0 = CONV_PAD - (kw - 1)

    def body(zp_ref, zc_ref, u1c_ref, u1n_ref, dyc_ref, dyn_ref, ck_ref, lg_ref, lb_ref, pw_ref, ps_ref, am_ref,
             dz_ref, dck_ref, dcb_ref, dlg_ref, dlb_ref, dpw_ref, dps_ref,
             u0w, pwin, du1w, ddw, ew, dkacc, dcb8, dlg8, dlb8, dps8):
        i = pl.program_id(0)
        has_next = i < nt - 1

        @pl.when(i == 0)
        def _():
            dck_ref[...] = jnp.zeros_like(dck_ref)
            dcb_ref[...] = jnp.zeros_like(dcb_ref)
            dlg_ref[...] = jnp.zeros_like(dlg_ref)
            dlb_ref[...] = jnp.zeros_like(dlb_ref)
            dpw_ref[...] = jnp.zeros_like(dpw_ref)
            dps_ref[...] = jnp.zeros_like(dps_ref)

        dkacc[...] = jnp.zeros_like(dkacc)
        dcb8[...] = jnp.zeros_like(dcb8)
        dlg8[...] = jnp.zeros_like(dlg8)
        dlb8[...] = jnp.zeros_like(dlb8)
        dps8[...] = jnp.zeros_like(dps8)

        _fill_windows(i, zp_ref, zc_ref, u0w, pwin, tl, cc)

        def conv_side(u1, dyc, own):
            xc = u1 - _head_mean(u1, am_ref)
            rstd = lax.rsqrt(_head_mean(xc * xc, am_ref) + EPS)
            uh = xc * rstd
            lgv = lg_ref[...]
            u2 = uh * lgv + lb_ref[...]
            sg = _sigmoid(u2)
            du2 = dyc * (sg * (1.0 + u2 * (1.0 - sg)))
            if own:
                dlg8[...] += _rowsum8(du2 * uh)
                dlb8[...] += _rowsum8(du2)
            duh = du2 * lgv
            return rstd * (duh - _head_mean(duh, am_ref) - uh * _head_mean(duh * uh, am_ref))

        def pool_side(dyp, t0, rows):
            dds, es = [], []
            tg = t0 + lax.broadcasted_iota(jnp.int32, (rows, 1), 0)
            for g in range(ng):
                dypre = dyp[:, g * gd:(g + 1) * gd] * ps_ref[:, g * gd:(g + 1) * gd]
                dd = lax.dot_general(dypre.astype(BF16), pw_ref[g].astype(BF16), (((1,), (1,)), ((), ())),
                                     preferred_element_type=F32)
                cnt = jnp.minimum(tg + 1, POOL_WINDOWS[g]).astype(F32)
                dds.append(dd)
                es.append(dd / cnt)
            return jnp.concatenate(dds, axis=-1), jnp.concatenate(es, axis=-1)

        def blocks(k, carry):
            b = pl.multiple_of(k * rb, SUBLANES)
            dyb = dyc_ref[pl.ds(b, rb), :]
            du1 = conv_side(u1c_ref[pl.ds(b, rb), :], dyb[:, :cc], True)
            du1w[pl.ds(b, rb), :] = du1
            dcb8[...] += _rowsum8(du1)
            dyp = dyb[:, cc:]
            dd, e = pool_side(dyp, i * tl + b, rb)
            ddw[pl.ds(b, rb), :] = dd
            ew[pl.ds(b, rb), :] = e
            for g in range(ng):
                d = _pool_fwd_block(pwin, pw_ref, b, rb, g, gd, POOL_WINDOWS[g], i * tl + b)
                db16 = d.astype(BF16)
                dypg = dyp[:, g * gd:(g + 1) * gd]
                ypre = jnp.dot(db16, pw_ref[g].astype(BF16), preferred_element_type=F32)
                dps8[:, g * gd:(g + 1) * gd] += _rowsum8(dypg * ypre)
                dypre = (dypg * ps_ref[:, g * gd:(g + 1) * gd]).astype(BF16)
                dpw_ref[g] += lax.dot_general(db16, dypre, (((0,), (0,)), ((), ())), preferred_element_type=F32)
            return carry

        lax.fori_loop(0, tl // rb, blocks, 0)

        dyn = dyn_ref[...]
        du1n = conv_side(u1n_ref[...], dyn[:, :cc], False)
        du1w[tl:tl + HALO, :] = jnp.where(has_next, du1n, 0.0)
        ddn, en = pool_side(dyn[:, cc:], (i + 1) * tl, HALO)
        ew[tl:tl + HALO, :] = jnp.where(has_next, en, 0.0)

        def taps(c, carry):
            b = pl.multiple_of(c * ROW_CHUNK, SUBLANES)
            w = du1w[pl.ds(b, ROW_CHUNK + CONV_PAD), :]
            acc = jnp.zeros((ROW_CHUNK, cc), F32)
            for j in range(kw):
                o = kw - 1 - j
                acc = acc + ck_ref[j:j + 1, :] * w[o:o + ROW_CHUNK]
            zc = zc_ref[pl.ds(b, ROW_CHUNK), :]
            a = zc[:, :cc]
            sg = _sigmoid(zc[:, cc:2 * cc])
            dz_ref[pl.ds(b, ROW_CHUNK), 0:cc] = (acc * sg).astype(dz_ref.dtype)
            dz_ref[pl.ds(b, ROW_CHUNK), cc:2 * cc] = (acc * a * sg * (1.0 - sg)).astype(dz_ref.dtype)
            du1c = w[0:ROW_CHUNK]
            uw = u0w[pl.ds(b + HALO - CONV_PAD, ROW_CHUNK + CONV_PAD), :]
            for j in range(kw):
                dkacc[j] += _rowsum8(du1c * uw[tap0 + j:tap0 + j + ROW_CHUNK])
            return carry

        lax.fori_loop(0, tl // ROW_CHUNK, taps, 0)

        def pool_back(k, carry):
            b = pl.multiple_of(k * rb, SUBLANES)
            n = rb + POOL_PAD
            for g in range(ng):
                s = ew[pl.ds(b, n), g * gd:(g + 1) * gd]
                sh = 1
                while sh < POOL_WINDOWS[g]:
                    s = s + pltpu.roll(s, n - sh, axis=0)
                    sh *= 2
                dp = s[0:rb] - ddw[pl.ds(b, rb), g * gd:(g + 1) * gd]
                dz_ref[pl.ds(b, rb), 2 * cc + g * gd:2 * cc + (g + 1) * gd] = dp.astype(dz_ref.dtype)
            return carry

        lax.fori_loop(0, tl // rb, pool_back, 0)

        dck_ref[...] += jnp.sum(dkacc[...], axis=1)
        dcb_ref[...] += jnp.sum(dcb8[...], axis=0, keepdims=True)
        dlg_ref[...] += jnp.sum(dlg8[...], axis=0, keepdims=True)
        dlb_ref[...] += jnp.sum(dlb8[...], axis=0, keepdims=True)
        dps_ref[...] += jnp.sum(dps8[...], axis=0, keepdims=True)

    def full(a):
        nd = a.ndim
        return pl.BlockSpec(a.shape, lambda i: (0,) * nd)

    nhb = L // HALO

    def prev_map(i):
        return (jnp.maximum(i * hb - 1, 0), 0)

    def next_map(i):
        return (jnp.minimum((i + 1) * hb, nhb - 1), 0)

    dcc = cc + cp
    row_cc = jax.ShapeDtypeStruct((1, cc), F32)
    out_shape = (jax.ShapeDtypeStruct((L, ci), BF16), jax.ShapeDtypeStruct((kw, cc), F32), row_cc, row_cc, row_cc,
                 jax.ShapeDtypeStruct((ng, gd, gd), F32), jax.ShapeDtypeStruct((1, cp), F32))
    acc_spec = [pl.BlockSpec((kw, cc), lambda i: (0, 0))] + [pl.BlockSpec((1, cc), lambda i: (0, 0))] * 3 + [
        pl.BlockSpec((ng, gd, gd), lambda i: (0, 0, 0)), pl.BlockSpec((1, cp), lambda i: (0, 0))]
    out, xo = _call(
        body, name=name, grid=(nt,),
        in_specs=[pl.BlockSpec((HALO, ci), prev_map), pl.BlockSpec((tl, ci), lambda i: (i, 0)),
                  pl.BlockSpec((tl, cc), lambda i: (i, 0)), pl.BlockSpec((HALO, cc), next_map),
                  pl.BlockSpec((tl, dcc), lambda i: (i, 0)), pl.BlockSpec((HALO, dcc), next_map),
                  full(ck), full(lg), full(lb), full(pw), full(ps), full(am)],
        out_specs=tuple([pl.BlockSpec((tl, ci), lambda i: (i, 0))] + acc_spec),
        out_shape=out_shape,
        scratch_shapes=[pltpu.VMEM((HALO + tl, cc), F32), pltpu.VMEM((HALO + tl, cp), F32),
                        pltpu.VMEM((tl + HALO, cc), F32), pltpu.VMEM((tl, cp), F32), pltpu.VMEM((tl + HALO, cp), F32),
                        pltpu.VMEM((kw, SUBLANES, cc), F32), pltpu.VMEM((SUBLANES, cc), F32),
                        pltpu.VMEM((SUBLANES, cc), F32), pltpu.VMEM((SUBLANES, cc), F32), pltpu.VMEM((SUBLANES, cp), F32)],
        sem=("arbitrary",), args=(z, z, u1, u1, dy, dy, ck, lg, lb, pw, ps, am), xchg=xchg)
    return out if xchg is None else (out, xo)


def _ffn_tiles(L, f):
    return _token_tile(L), _divisor(f, 256, 128)


def _ffn_conv(win, k_ref, kw, rows):
    o = FFN_PAD - (kw - 1)
    acc = k_ref[0:1, :] * win[o:o + rows]
    for j in range(1, kw):
        acc = acc + k_ref[j:j + 1, :] * win[o + j:o + j + rows]
    return acc


def _ffn_fwd(ug0, kf, name, xchg=None):
    L, f2 = ug0.shape
    f = f2 // 2
    kw = kf.shape[0]
    tl, tc = _ffn_tiles(L, f)
    nj = f // tc
    hb = tl // FFN_PAD

    def body(gp_ref, gc_ref, vp_ref, vc_ref, kg_ref, kv_ref, o_ref):
        i = pl.program_id(1)
        keep = i > 0

        def block(b, gwin, vwin):
            gate = _ffn_conv(gwin, kg_ref, kw, ROW_CHUNK)
            val = _ffn_conv(vwin, kv_ref, kw, ROW_CHUNK)
            o_ref[pl.ds(b, ROW_CHUNK), :] = ((gate * _sigmoid(gate)) * val).astype(o_ref.dtype)

        block(0, jnp.concatenate([jnp.where(keep, gp_ref[...], 0.0), gc_ref[0:ROW_CHUNK]], axis=0),
              jnp.concatenate([jnp.where(keep, vp_ref[...], 0.0), vc_ref[0:ROW_CHUNK]], axis=0))

        def chunk(c, carry):
            b = pl.multiple_of(c * ROW_CHUNK, SUBLANES)
            block(b, gc_ref[pl.ds(b - FFN_PAD, ROW_CHUNK + FFN_PAD), :], vc_ref[pl.ds(b - FFN_PAD, ROW_CHUNK + FFN_PAD), :])
            return carry

        lax.fori_loop(1, tl // ROW_CHUNK, chunk, 0)

    def prev_g(j, i):
        return (jnp.maximum(i * hb - 1, 0), j)

    def prev_v(j, i):
        return (jnp.maximum(i * hb - 1, 0), j + nj)

    out, xo = _call(
        body, name=name, grid=(nj, L // tl),
        in_specs=[pl.BlockSpec((FFN_PAD, tc), prev_g), pl.BlockSpec((tl, tc), lambda j, i: (i, j)),
                  pl.BlockSpec((FFN_PAD, tc), prev_v), pl.BlockSpec((tl, tc), lambda j, i: (i, j + nj)),
                  pl.BlockSpec((kw, tc), lambda j, i: (0, j)), pl.BlockSpec((kw, tc), lambda j, i: (0, j + nj))],
        out_specs=pl.BlockSpec((tl, tc), lambda j, i: (i, j)),
        out_shape=jax.ShapeDtypeStruct((L, f), BF16),
        sem=("parallel", "parallel"), args=(ug0, ug0, ug0, ug0, kf, kf), xchg=xchg)
    return out if xchg is None else (out, xo)


def _ffn_bwd(ug0, dact, kf, name, xchg=None):
    L, f2 = ug0.shape
    f = f2 // 2
    kw = kf.shape[0]
    tl, tc = _ffn_tiles(L, f)
    nj = f // tc
    nt = L // tl
    hb = tl // FFN_PAD
    nhb = L // FFN_PAD

    def body(gp_ref, gc_ref, gn_ref, vp_ref, vc_ref, vn_ref, dc_ref, dn_ref, kg_ref, kv_ref, du_ref, dk_ref,
             gwin, vwin, dgw, dvw, dk8):
        i = pl.program_id(1)
        keep = i > 0
        has_next = i < nt - 1

        @pl.when(i == 0)
        def _():
            dk_ref[...] = jnp.zeros_like(dk_ref)

        dk8[...] = jnp.zeros_like(dk8)

        gwin[0:FFN_PAD, :] = jnp.where(keep, gp_ref[...], 0.0)
        vwin[0:FFN_PAD, :] = jnp.where(keep, vp_ref[...], 0.0)
        gwin[FFN_PAD + tl:FFN_PAD + tl + FFN_PAD, :] = gn_ref[...]
        vwin[FFN_PAD + tl:FFN_PAD + tl + FFN_PAD, :] = vn_ref[...]

        def copy(c, carry):
            b = pl.multiple_of(c * ROW_CHUNK, SUBLANES)
            gwin[pl.ds(FFN_PAD + b, ROW_CHUNK), :] = gc_ref[pl.ds(b, ROW_CHUNK), :]
            vwin[pl.ds(FFN_PAD + b, ROW_CHUNK), :] = vc_ref[pl.ds(b, ROW_CHUNK), :]
            return carry

        lax.fori_loop(0, tl // ROW_CHUNK, copy, 0)

        def grads(b, rows, dact):
            gate = _ffn_conv(gwin[pl.ds(b, rows + FFN_PAD), :], kg_ref, kw, rows)
            val = _ffn_conv(vwin[pl.ds(b, rows + FFN_PAD), :], kv_ref, kw, rows)
            sg = _sigmoid(gate)
            return dact * val * (sg * (1.0 + gate * (1.0 - sg))), dact * (gate * sg)

        def first(c, carry):
            b = pl.multiple_of(c * ROW_CHUNK, SUBLANES)
            dg, dv = grads(b, ROW_CHUNK, dc_ref[pl.ds(b, ROW_CHUNK), :])
            dgw[pl.ds(b, ROW_CHUNK), :] = dg
            dvw[pl.ds(b, ROW_CHUNK), :] = dv
            return carry

        lax.fori_loop(0, tl // ROW_CHUNK, first, 0)
        dgn, dvn = grads(tl, FFN_PAD, dn_ref[...])
        dgw[tl:tl + FFN_PAD, :] = jnp.where(has_next, dgn, 0.0)
        dvw[tl:tl + FFN_PAD, :] = jnp.where(has_next, dvn, 0.0)

        def second(c, carry):
            b = pl.multiple_of(c * ROW_CHUNK, SUBLANES)
            for h, (dw, xwin, k_ref) in enumerate(((dgw, gwin, kg_ref), (dvw, vwin, kv_ref))):
                w = dw[pl.ds(b, ROW_CHUNK + FFN_PAD), :]
                acc = k_ref[kw - 1:kw, :] * w[0:ROW_CHUNK]
                for j in range(kw - 1):
                    o = kw - 1 - j
                    acc = acc + k_ref[j:j + 1, :] * w[o:o + ROW_CHUNK]
                du_ref[h, pl.ds(b, ROW_CHUNK), :] = acc.astype(du_ref.dtype)
                xw = xwin[pl.ds(b, ROW_CHUNK + FFN_PAD), :]
                o0 = FFN_PAD - (kw - 1)
                for j in range(kw):
                    dk8[h, j] += _rowsum8(w[0:ROW_CHUNK] * xw[o0 + j:o0 + j + ROW_CHUNK])
            return carry

        lax.fori_loop(0, tl // ROW_CHUNK, second, 0)
        dk_ref[...] += jnp.sum(dk8[...], axis=2)

    def prev_g(j, i):
        return (jnp.maximum(i * hb - 1, 0), j)

    def prev_v(j, i):
        return (jnp.maximum(i * hb - 1, 0), j + nj)

    def next_g(j, i):
        return (jnp.minimum((i + 1) * hb, nhb - 1), j)

    def next_v(j, i):
        return (jnp.minimum((i + 1) * hb, nhb - 1), j + nj)

    out, xo = _call(
        body, name=name, grid=(nj, nt),
        in_specs=[pl.BlockSpec((FFN_PAD, tc), prev_g), pl.BlockSpec((tl, tc), lambda j, i: (i, j)),
                  pl.BlockSpec((FFN_PAD, tc), next_g),
                  pl.BlockSpec((FFN_PAD, tc), prev_v), pl.BlockSpec((tl, tc), lambda j, i: (i, j + nj)),
                  pl.BlockSpec((FFN_PAD, tc), next_v),
                  pl.BlockSpec((tl, tc), lambda j, i: (i, j)), pl.BlockSpec((FFN_PAD, tc), next_g),
                  pl.BlockSpec((kw, tc), lambda j, i: (0, j)), pl.BlockSpec((kw, tc), lambda j, i: (0, j + nj))],
        out_specs=(pl.BlockSpec((2, tl, tc), lambda j, i: (0, i, j)), pl.BlockSpec((2, kw, tc), lambda j, i: (0, 0, j))),
        out_shape=(jax.ShapeDtypeStruct((2, L, f), BF16), jax.ShapeDtypeStruct((2, kw, f), F32)),
        scratch_shapes=[pltpu.VMEM((tl + 2 * FFN_PAD, tc), F32), pltpu.VMEM((tl + 2 * FFN_PAD, tc), F32),
                        pltpu.VMEM((tl + FFN_PAD, tc), F32), pltpu.VMEM((tl + FFN_PAD, tc), F32),
                        pltpu.VMEM((2, kw, SUBLANES, tc), F32)],
        sem=("parallel", "arbitrary"), args=(ug0, ug0, ug0, ug0, ug0, ug0, dact, dact, kf, kf), xchg=xchg)
    return out if xchg is None else (out, xo)


def _adamw_math(w, g, m, v):
    m = ADAM_B1 * m + (1.0 - ADAM_B1) * g
    v = ADAM_B2 * v + (1.0 - ADAM_B2) * (g * g)
    m_hat = m / (1.0 - ADAM_B1 ** ADAM_STEP)
    v_hat = v / (1.0 - ADAM_B2 ** ADAM_STEP)
    delta = -ADAM_LR * (m_hat / (jnp.sqrt(v_hat) + ADAM_EPS) + ADAM_WD * w)
    return delta, m, v


def _sum_parts(parts_ref, idx):
    g = parts_ref[(0,) + idx].astype(F32)
    for q in range(1, N_DEV):
        g = g + parts_ref[(q,) + idx].astype(F32)
    return g


def _adamw_big(parts, w, m, v, name):
    nl, R, C = w.shape
    tr = _divisor(R, 256, 2 * SUBLANES)

    def body(*refs):
        p_refs = refs[:nl]
        w_ref, m_ref, v_ref, g_ref, d_ref, nm_ref, nv_ref = refs[nl:]
        layer = pl.program_id(0)
        for k in range(nl):
            @pl.when(layer == k)
            def _(k=k):
                g = _sum_parts(p_refs[k], ())
                d, nm, nv = _adamw_math(w_ref[0], g, m_ref[0], v_ref[0])
                g_ref[0] = g
                d_ref[0] = d
                nm_ref[0] = nm
                nv_ref[0] = nv

    def part_spec(k):
        return pl.BlockSpec((N_DEV, tr, C), lambda l, r: (0, jnp.where(l == k, r, 0), 0))

    blk = pl.BlockSpec((1, tr, C), lambda l, r: (l, r, 0))
    shp = jax.ShapeDtypeStruct((nl, R, C), F32)
    return pl.pallas_call(
        body, name=name, grid=(nl, R // tr),
        in_specs=[part_spec(k) for k in range(nl)] + [blk, blk, blk],
        out_specs=(blk, blk, blk, blk), out_shape=(shp, shp, shp, shp),
        compiler_params=_params(("arbitrary", "arbitrary")),
    )(*parts, w, m, v)


def _adamw_small(entries, name):
    n = len(entries)
    uniq = []
    for e in entries:
        if not any(e[0] is u for u in uniq):
            uniq.append(e[0])
    pidx = [next(k for k, u in enumerate(uniq) if u is e[0]) for e in entries]
    npart = len(uniq)

    def body(*refs):
        p_refs = refs[:npart]
        wmv = refs[npart:npart + 3 * n]
        outs = refs[npart + 3 * n:]
        for t, e in enumerate(entries):
            lo, w = e[1], e[2]
            rows = w.shape[0]
            pr = p_refs[pidx[t]]
            g = pr[0, lo:lo + rows].astype(F32)
            for q in range(1, N_DEV):
                g = g + pr[q, lo:lo + rows].astype(F32)
            d, nm, nv = _adamw_math(wmv[3 * t][...], g, wmv[3 * t + 1][...], wmv[3 * t + 2][...])
            outs[4 * t][...] = g
            outs[4 * t + 1][...] = d
            outs[4 * t + 2][...] = nm
            outs[4 * t + 3][...] = nv

    vm = pl.BlockSpec(memory_space=pltpu.VMEM)
    args = list(uniq)
    out_shape = []
    for e in entries:
        args += [e[2], e[3], e[4]]
        out_shape += [jax.ShapeDtypeStruct(e[2].shape, F32)] * 4
    res = pl.pallas_call(
        body, name=name, in_specs=[vm] * len(args), out_specs=tuple([vm] * len(out_shape)),
        out_shape=tuple(out_shape), compiler_params=_params(),
    )(*args)
    return [tuple(res[4 * t:4 * t + 4]) for t in range(n)]


def _head_matrix(cc):
    bw = min(256, cc)
    r = lax.broadcasted_iota(jnp.int32, (bw, bw), 0) // HEAD_DIM
    c = lax.broadcasted_iota(jnp.int32, (bw, bw), 1) // HEAD_DIM
    return jnp.where(r == c, 1.0 / HEAD_DIM, 0.0).astype(BF16)


def _cols_from_shards(g):
    nd = g.ndim
    perm = tuple(range(1, nd - 1)) + (0, nd - 1)
    t = jnp.transpose(g, perm)
    return t.reshape(t.shape[:-2] + (t.shape[-2] * t.shape[-1],))


def _cols_to_shards(a):
    nd = a.ndim
    t = a.reshape(a.shape[:-1] + (N_DEV, a.shape[-1] // N_DEV))
    perm = (nd - 1,) + tuple(range(nd - 1)) + (nd,)
    return jnp.transpose(t, perm)


def kernel(x, meta_tokens, norm1_g, w_in, conv_dw_k, conv_dw_b, conv_ln_g, conv_ln_b, pool_w, pool_scale, w_out, norm2_g, w_up, ffn_dw_k, w_down, final_g, loss_target, m_meta_tokens, m_norm1_g, m_w_in, m_conv_dw_k, m_conv_dw_b, m_conv_ln_g, m_conv_ln_b, m_pool_w, m_pool_scale, m_w_out, m_norm2_g, m_w_up, m_ffn_dw_k, m_w_down, m_final_g, v_meta_tokens, v_norm1_g, v_w_in, v_conv_dw_k, v_conv_dw_b, v_conv_ln_g, v_conv_ln_b, v_pool_w, v_pool_scale, v_w_out, v_norm2_g, v_w_up, v_ffn_dw_k, v_w_down, v_final_g):
    depth, D = norm1_g.shape
    n_meta = meta_tokens.shape[0]
    seq = x.shape[1]
    L = n_meta + seq
    cc = conv_dw_b.shape[1]
    ng, gd = pool_w.shape[1], pool_w.shape[2]
    f = w_down.shape[1] * N_DEV

    def cols(g):
        w = _cols_from_shards(g)
        return w, w.T

    def rows(g):
        w = g.reshape(-1, g.shape[-1])
        return w, w.T

    b16 = lambda a: a.astype(BF16)
    (g_in0, g_ck, g_kf, g_meta) = _exchange([b16(w_in[0]), conv_dw_k, ffn_dw_k, meta_tokens], ["gather"] * 4,
                                            "gather_first")
    ck_full = _cols_from_shards(g_ck)
    kf_full = _cols_from_shards(g_kf)
    meta_full = _cols_from_shards(g_meta)
    am = _head_matrix(cc)
    win, wout, wup, wdown = [None] * depth, [None] * depth, [None] * depth, [None] * depth
    win[0] = cols(g_in0)

    h = jnp.concatenate([meta_full, x[0]], axis=0)
    saved = []
    for l in range(depth):
        hn1 = _rms_fwd(h, norm1_g[l:l + 1], f"rms1_fwd_{l}")
        z, (g_out,) = _mm(hn1, win[l][0], f"in_proj_{l}", tn_cap=768, xchg=([b16(w_out[l])], ["gather"]))
        wout[l] = rows(g_out)
        (ymix, u1), (g_up,) = _mixer_fwd(z, ck_full[l], conv_dw_b[l:l + 1], conv_ln_g[l:l + 1], conv_ln_b[l:l + 1], pool_w[l],
                                         pool_scale[l:l + 1], am, f"mixer_fwd_{l}", xchg=([b16(w_up[l])], ["gather"]))
        wup[l] = cols(g_up)
        h_mid = _mm(ymix, wout[l][0], f"out_proj_{l}", res=h, tn_cap=512)
        hn2 = _rms_fwd(h_mid, norm2_g[l:l + 1], f"rms2_fwd_{l}")
        nxt = [b16(w_down[l])] + ([b16(w_in[l + 1])] if l + 1 < depth else [])
        ug0, got = _mm(hn2, wup[l][0], f"up_proj_{l}", xchg=(nxt, ["gather"] * len(nxt)))
        wdown[l] = rows(got[0])
        if l + 1 < depth:
            win[l + 1] = cols(got[1])
        act = _ffn_fwd(ug0, kf_full[l], f"ffn_fwd_{l}")
        h_out = _mm(act, wdown[l][0], f"down_proj_{l}", res=h_mid, tn_cap=512)
        saved.append((h, hn1, z, u1, ymix, h_mid, hn2, ug0, act))
        h = h_out

    tgt = jnp.concatenate([jnp.zeros((n_meta, D), F32), loss_target[0]], axis=0)
    dh, d_final_g, loss_part = _loss_head(h, final_g.reshape(1, D), tgt, n_meta, "loss_head")

    def row_shards(gm):
        return b16(gm.reshape(N_DEV, -1, gm.shape[-1]))

    def col_shards(gm):
        return b16(_cols_to_shards(gm))

    gw = {k: [None] * depth for k in ("ck", "cb", "lg", "lb", "pw", "ps", "kf", "n1", "n2")}
    parts = {k: [None] * depth for k in ("in", "out", "up", "down")}
    for l in reversed(range(depth)):
        h_in, hn1, z, u1, ymix, h_mid, hn2, ug0, act = saved[l]
        dact = _mm(dh, wdown[l][1], f"down_proj_bwd_{l}")
        g_down = _mm_tn(act, dh, f"down_proj_wgrad_{l}", tq_cap=512)
        (dug0, dkf), (parts["down"][l],) = _ffn_bwd(ug0, dact, kf_full[l], f"ffn_bwd_{l}", xchg=([row_shards(g_down)], ["a2a"]))
        gw["kf"][l] = jnp.concatenate([dkf[0], dkf[1]], axis=-1)
        g_up = _mm_tn(hn2, dug0, f"up_proj_wgrad_{l}", halves=2)
        dhn2, (parts["up"][l],) = _mm(dug0, wup[l][1], f"up_proj_bwd_{l}", halves=2, tn_cap=512,
                                      xchg=([col_shards(g_up)], ["a2a"]))
        dh_mid, gw["n2"][l] = _rms_bwd(h_mid, norm2_g[l:l + 1], dhn2, dh, f"rms2_bwd_{l}")
        dymix = _mm(dh_mid, wout[l][1], f"out_proj_bwd_{l}", tn_cap=512)
        g_out = _mm_tn(ymix, dh_mid, f"out_proj_wgrad_{l}", tq_cap=512)
        (dz, gw["ck"][l], gw["cb"][l], gw["lg"][l], gw["lb"][l], gw["pw"][l], gw["ps"][l]), (parts["out"][l],) = _mixer_bwd(
            z, u1, dymix, ck_full[l], conv_ln_g[l:l + 1], conv_ln_b[l:l + 1], pool_w[l], pool_scale[l:l + 1], am,
            f"mixer_bwd_{l}", xchg=([row_shards(g_out)], ["a2a"]))
        g_in = _mm_tn(hn1, dz, f"in_proj_wgrad_{l}", tq_cap=768)
        dhn1, (parts["in"][l],) = _mm(dz, win[l][1], f"in_proj_bwd_{l}", tn_cap=512, xchg=([col_shards(g_in)], ["a2a"]))
        dh, gw["n1"][l] = _rms_bwd(h_in, norm1_g[l:l + 1], dhn1, dh_mid, f"rms1_bwd_{l}")
    grad_x = dh[n_meta:][None]
    d_meta = dh[:n_meta]

    zero_row = jnp.zeros((1, D), F32)
    pack_d = jnp.concatenate(gw["n1"] + gw["n2"] + [d_final_g, jnp.broadcast_to(loss_part[:, :1], (1, D)), zero_row, zero_row], axis=0)
    pack_c = jnp.concatenate(gw["cb"] + gw["lg"] + gw["lb"] + gw["ps"], axis=0)
    pack_pw = jnp.stack(gw["pw"]).reshape(depth * ng * gd, gd)
    src = [_cols_to_shards(jnp.stack(gw["ck"])), _cols_to_shards(jnp.stack(gw["kf"])), _cols_to_shards(d_meta),
           pack_d, pack_c, pack_pw]
    r_ck, r_kf, r_meta, r_d, r_c, r_pw = _exchange(src, ["a2a"] * 3 + ["gather"] * 3, "exchange_small_grads")

    big = {
        "w_in": _adamw_big(parts["in"], w_in, m_w_in, v_w_in, "adamw_w_in"),
        "w_out": _adamw_big(parts["out"], w_out, m_w_out, v_w_out, "adamw_w_out"),
        "w_up": _adamw_big(parts["up"], w_up, m_w_up, v_w_up, "adamw_w_up"),
        "w_down": _adamw_big(parts["down"], w_down, m_w_down, v_w_down, "adamw_w_down"),
    }
    kwid = conv_dw_k.shape[1]
    fkw = ffn_dw_k.shape[1]
    row = lambda a: a.reshape(1, -1)
    entries = [
        (r_d, 0, norm1_g, m_norm1_g, v_norm1_g),
        (r_d, depth, norm2_g, m_norm2_g, v_norm2_g),
        (r_d, 2 * depth, row(final_g), row(m_final_g), row(v_final_g)),
        (r_c, 0, conv_dw_b, m_conv_dw_b, v_conv_dw_b),
        (r_c, depth, conv_ln_g, m_conv_ln_g, v_conv_ln_g),
        (r_c, 2 * depth, conv_ln_b, m_conv_ln_b, v_conv_ln_b),
        (r_c, 3 * depth, pool_scale, m_pool_scale, v_pool_scale),
        (r_pw, 0, pool_w.reshape(-1, gd), m_pool_w.reshape(-1, gd), v_pool_w.reshape(-1, gd)),
        (r_ck.reshape(N_DEV, depth * kwid, -1), 0, conv_dw_k.reshape(depth * kwid, -1),
         m_conv_dw_k.reshape(depth * kwid, -1), v_conv_dw_k.reshape(depth * kwid, -1)),
        (r_kf.reshape(N_DEV, depth * fkw, -1), 0, ffn_dw_k.reshape(depth * fkw, -1),
         m_ffn_dw_k.reshape(depth * fkw, -1), v_ffn_dw_k.reshape(depth * fkw, -1)),
        (r_meta, 0, meta_tokens, m_meta_tokens, v_meta_tokens),
        (r_d, 2 * depth + 1, zero_row, zero_row, zero_row),
    ]
    small = _adamw_small(entries, "adamw_small")
    names = ["norm1_g", "norm2_g", "final_g", "conv_dw_b", "conv_ln_g", "conv_ln_b", "pool_scale", "pool_w",
             "conv_dw_k", "ffn_dw_k", "meta_tokens"]
    shapes = {"final_g": final_g.shape, "pool_w": pool_w.shape, "conv_dw_k": conv_dw_k.shape, "ffn_dw_k": ffn_dw_k.shape}
    res = dict(big)
    for nme, quad in zip(names, small[:-1]):
        res[nme] = tuple(a.reshape(shapes[nme]) if nme in shapes else a for a in quad)
    loss = small[-1][0][0, 0]

    order = ["meta_tokens", "norm1_g", "w_in", "conv_dw_k", "conv_dw_b", "conv_ln_g", "conv_ln_b", "pool_w", "pool_scale",
             "w_out", "norm2_g", "w_up", "ffn_dw_k", "w_down", "final_g"]
    return (loss, grad_x, *[res[k][0] for k in order], *[res[k][1] for k in order], *[res[k][2] for k in order],
            *[res[k][3] for k in order])
```

```python
import functools

import jax
import jax.numpy as jnp
from jax import lax
from jax.experimental import pallas as pl
from jax.experimental.pallas import tpu as pltpu

F32 = jnp.float32
BF16 = jnp.bfloat16

EPS = 1e-6
HEAD_DIM = 64
POOL_WINDOWS = (2, 4, 8, 16)
ADAM_LR = 0.001
ADAM_B1 = 0.9
ADAM_B2 = 0.999
ADAM_EPS = 1e-08
ADAM_WD = 0.01
ADAM_STEP = 10

N_DEV = 8
SUBLANES = 8
HALO = 48
CONV_PAD = 32
POOL_PAD = 16
FFN_PAD = 8
ROW_CHUNK = 24
MAX_TILE_ROWS = 1024
VMEM_LIMIT = 52 * 1024 * 1024


def _divisor(n, cap, mult):
    best = None
    for d in range(mult, min(n, cap) + 1, mult):
        if n % d == 0:
            best = d
    return n if best is None else best


def _token_tile(L):
    return _divisor(L, MAX_TILE_ROWS, HALO)


def _row_tile(L):
    return _divisor(L, 320, 2 * SUBLANES)


def _stat_rows(tl):
    return _divisor(tl, 256, SUBLANES)


def _params(sem=None):
    return pltpu.CompilerParams(dimension_semantics=sem, vmem_limit_bytes=VMEM_LIMIT)


def _rowsum8(x):
    acc = x[0:SUBLANES]
    for k in range(1, x.shape[0] // SUBLANES):
        acc = acc + x[k * SUBLANES:(k + 1) * SUBLANES]
    return acc


def _sigmoid(x):
    return jax.nn.sigmoid(x)


def _head_mean(x, am_ref):
    bw = am_ref.shape[0]
    am = am_ref[...]
    outs = []
    for blk in range(x.shape[1] // bw):
        xb = x[:, blk * bw:(blk + 1) * bw]
        hi = xb.astype(BF16)
        lo = (xb - hi.astype(F32)).astype(BF16)
        outs.append(jnp.dot(hi, am, preferred_element_type=F32) + jnp.dot(lo, am, preferred_element_type=F32))
    return outs[0] if len(outs) == 1 else jnp.concatenate(outs, axis=-1)


def _xchg_out_shapes(srcs, modes):
    out = []
    for s, m in zip(srcs, modes):
        shp = ((N_DEV,) + tuple(s.shape)) if m == "gather" else tuple(s.shape)
        out.append(jax.ShapeDtypeStruct(shp, s.dtype))
    return out


def _xchg_sems(n):
    return [pltpu.SemaphoreType.DMA((n, N_DEV - 1)), pltpu.SemaphoreType.DMA((n, N_DEV - 1)), pltpu.SemaphoreType.DMA((n,))]


def _xchg_ops(src_refs, out_refs, sems, modes):
    n = len(src_refs)
    send_sems, recv_sems, local_sems = sems
    x, y, c = lax.axis_index("x"), lax.axis_index("y"), lax.axis_index("c")
    me = 4 * x + 2 * y + c

    def peer(d):
        return (x ^ ((d >> 2) & 1), y ^ ((d >> 1) & 1), c ^ (d & 1))

    def peer_id(d):
        px, py, pc = peer(d)
        return 4 * px + 2 * py + pc

    def remote(t, d):
        src = src_refs[t] if modes[t] == "gather" else src_refs[t].at[peer_id(d)]
        return pltpu.make_async_remote_copy(
            src_ref=src, dst_ref=out_refs[t].at[me], send_sem=send_sems.at[t, d - 1], recv_sem=recv_sems.at[t, d - 1],
            device_id=peer(d), device_id_type=pl.DeviceIdType.MESH)

    def arrival(t, d):
        src = src_refs[t] if modes[t] == "gather" else src_refs[t].at[me]
        return pltpu.make_async_remote_copy(
            src_ref=src, dst_ref=out_refs[t].at[peer_id(d)], send_sem=send_sems.at[t, d - 1],
            recv_sem=recv_sems.at[t, d - 1], device_id=peer(d), device_id_type=pl.DeviceIdType.MESH)

    def local(t):
        src = src_refs[t] if modes[t] == "gather" else src_refs[t].at[me]
        return pltpu.make_async_copy(src, out_refs[t].at[me], local_sems.at[t])

    def start():
        for t in range(n):
            local(t).start()
        for t in range(n):
            for d in range(1, N_DEV):
                remote(t, d).start()

    def wait():
        for t in range(n):
            for d in range(1, N_DEV):
                arrival(t, d).wait_recv()
        for t in range(n):
            for d in range(1, N_DEV):
                remote(t, d).wait_send()
        for t in range(n):
            local(t).wait()

    return start, wait


def _exchange(srcs, modes, name):
    n = len(srcs)

    def body(*refs):
        start, wait = _xchg_ops(refs[:n], refs[n:2 * n], refs[2 * n:], modes)
        start()
        wait()

    any_spec = pl.BlockSpec(memory_space=pl.ANY)
    return pl.pallas_call(
        body, name=name, out_shape=tuple(_xchg_out_shapes(srcs, modes)),
        in_specs=[any_spec] * n, out_specs=tuple([any_spec] * n),
        scratch_shapes=_xchg_sems(n),
        compiler_params=pltpu.CompilerParams(has_side_effects=True),
    )(*srcs)


def _call(body, *, name, grid, in_specs, out_specs, out_shape, args, scratch_shapes=(), sem=None, xchg=None):
    single = not isinstance(out_shape, (tuple, list))
    outs_shape = [out_shape] if single else list(out_shape)
    outs_spec = [out_specs] if single else list(out_specs)
    if xchg is None:
        res = pl.pallas_call(
            body, name=name, grid=grid, in_specs=list(in_specs), out_specs=out_specs, out_shape=out_shape,
            scratch_shapes=list(scratch_shapes), compiler_params=_params(sem))(*args)
        return res, ()
    srcs, modes = xchg
    n_in, n_out, n_scr, nx = len(in_specs), len(outs_shape), len(scratch_shapes), len(srcs)

    def wrapped(*refs):
        ins = refs[:n_in]
        xs = refs[n_in:n_in + nx]
        o0 = n_in + nx
        outs = refs[o0:o0 + n_out]
        xo = refs[o0 + n_out:o0 + n_out + nx]
        s0 = o0 + n_out + nx
        scr = refs[s0:s0 + n_scr]
        start, wait = _xchg_ops(xs, xo, refs[s0 + n_scr:], modes)
        first = functools.reduce(jnp.logical_and, [pl.program_id(a) == 0 for a in range(len(grid))])
        last = functools.reduce(jnp.logical_and, [pl.program_id(a) == grid[a] - 1 for a in range(len(grid))])

        @pl.when(first)
        def _():
            start()

        body(*ins, *outs, *scr)

        @pl.when(last)
        def _():
            wait()

    any_spec = pl.BlockSpec(memory_space=pl.ANY)
    res = pl.pallas_call(
        wrapped, name=name, grid=grid, in_specs=list(in_specs) + [any_spec] * nx,
        out_specs=tuple(outs_spec + [any_spec] * nx), out_shape=tuple(outs_shape + _xchg_out_shapes(srcs, modes)),
        scratch_shapes=list(scratch_shapes) + _xchg_sems(nx),
        compiler_params=_params(("arbitrary",) * len(grid)))(*args, *srcs)
    comp = res[:n_out]
    return (comp[0] if single else tuple(comp)), tuple(res[n_out:])


def _rms_fwd(h, g, name):
    L, D = h.shape
    tl = _row_tile(L)

    def body(h_ref, g_ref, o_ref):
        x = h_ref[...]
        r = lax.rsqrt(jnp.mean(x * x, axis=-1, keepdims=True) + EPS)
        o_ref[...] = ((x * r) * g_ref[...]).astype(o_ref.dtype)

    return pl.pallas_call(
        body, name=name, grid=(L // tl,),
        in_specs=[pl.BlockSpec((tl, D), lambda i: (i, 0)), pl.BlockSpec((1, D), lambda i: (0, 0))],
        out_specs=pl.BlockSpec((tl, D), lambda i: (i, 0)),
        out_shape=jax.ShapeDtypeStruct((L, D), BF16),
        compiler_params=_params(("parallel",)),
    )(h, g)


def _mm(a, b, name, *, res=None, out_dtype=F32, tn_cap=1408, halves=1, xchg=None):
    if halves > 1:
        _, M, kh = a.shape
        K = kh * halves
    else:
        M, K = a.shape
        kh = K
    N = b.shape[1]
    tm = _token_tile(M)
    tn = _divisor(N, tn_cap, 128)
    tk = kh if kh <= 2816 else _divisor(kh, 2816, 128)
    kper = kh // tk
    nk = halves * kper
    grid = (M // tm, N // tn, nk)

    def body(*refs):
        if res is None:
            a_ref, b_ref, o_ref = refs[:3]
            r_ref = None
            scratch = refs[3:]
        else:
            a_ref, b_ref, r_ref, o_ref = refs[:4]
            scratch = refs[4:]
        av = a_ref[0] if halves > 1 else a_ref[...]
        prod = jnp.dot(av.astype(BF16), b_ref[...], preferred_element_type=F32)
        if nk == 1:
            if r_ref is not None:
                prod = prod + r_ref[...]
            o_ref[...] = prod.astype(o_ref.dtype)
        else:
            acc = scratch[0]
            k = pl.program_id(2)

            @pl.when(k == 0)
            def _():
                acc[...] = prod

            @pl.when(k > 0)
            def _():
                acc[...] += prod

            @pl.when(k == nk - 1)
            def _():
                tot = acc[...]
                if r_ref is not None:
                    tot = tot + r_ref[...]
                o_ref[...] = tot.astype(o_ref.dtype)

    if halves > 1:
        a_spec = pl.BlockSpec((1, tm, tk), lambda i, j, k: (k // kper, i, k % kper))
    else:
        a_spec = pl.BlockSpec((tm, tk), lambda i, j, k: (i, k))
    in_specs = [a_spec, pl.BlockSpec((tk, tn), lambda i, j, k: (k, j))]
    args = [a, b]
    if res is not None:
        in_specs.append(pl.BlockSpec((tm, tn), lambda i, j, k: (i, j)))
        args.append(res)
    out, xo = _call(
        body, name=name, grid=grid, in_specs=in_specs,
        out_specs=pl.BlockSpec((tm, tn), lambda i, j, k: (i, j)),
        out_shape=jax.ShapeDtypeStruct((M, N), out_dtype),
        scratch_shapes=[pltpu.VMEM((tm, tn), F32)] if nk > 1 else [],
        sem=("parallel", "parallel", "arbitrary"), args=args, xchg=xchg)
    return out if xchg is None else (out, xo)


def _mm_tn(a, b, name, *, halves=1, tq_cap=1408):
    L, P = a.shape
    if halves > 1:
        qh = b.shape[2]
        Q = qh * halves
    else:
        Q = b.shape[1]
        qh = Q
    tl = _token_tile(L)
    tp = _divisor(P, 1408, 128)
    tq = _divisor(qh, tq_cap, 128)
    qper = qh // tq
    grid = (P // tp, Q // tq, L // tl)

    def body(a_ref, b_ref, o_ref):
        bv = b_ref[0] if halves > 1 else b_ref[...]
        prod = lax.dot_general(a_ref[...].astype(BF16), bv.astype(BF16), (((0,), (0,)), ((), ())),
                               preferred_element_type=F32)
        l = pl.program_id(2)

        @pl.when(l == 0)
        def _():
            o_ref[...] = prod

        @pl.when(l > 0)
        def _():
            o_ref[...] += prod

    if halves > 1:
        b_spec = pl.BlockSpec((1, tl, tq), lambda p, q, l: (q // qper, l, q % qper))
    else:
        b_spec = pl.BlockSpec((tl, tq), lambda p, q, l: (l, q))
    return pl.pallas_call(
        body, name=name, grid=grid,
        in_specs=[pl.BlockSpec((tl, tp), lambda p, q, l: (l, p)), b_spec],
        out_specs=pl.BlockSpec((tp, tq), lambda p, q, l: (p, q)),
        out_shape=jax.ShapeDtypeStruct((P, Q), F32),
        compiler_params=_params(("parallel", "parallel", "arbitrary")),
    )(a, b)


def _rms_bwd(h, g, dhn, dres, name):
    L, D = h.shape
    tl = _row_tile(L)
    nt = L // tl

    def body(h_ref, g_ref, dhn_ref, dres_ref, dh_ref, dg_ref):
        i = pl.program_id(0)
        x = h_ref[...]
        r = lax.rsqrt(jnp.mean(x * x, axis=-1, keepdims=True) + EPS)
        xhat = x * r
        dhn = dhn_ref[...]
        dxhat = dhn * g_ref[...]
        dh_ref[...] = dres_ref[...] + r * (dxhat - xhat * jnp.mean(dxhat * xhat, axis=-1, keepdims=True))
        part = jnp.sum(_rowsum8(dhn * xhat), axis=0, keepdims=True)

        @pl.when(i == 0)
        def _():
            dg_ref[...] = part

        @pl.when(i > 0)
        def _():
            dg_ref[...] += part

    tile = pl.BlockSpec((tl, D), lambda i: (i, 0))
    row = pl.BlockSpec((1, D), lambda i: (0, 0))
    return pl.pallas_call(
        body, name=name, grid=(nt,), in_specs=[tile, row, tile, tile], out_specs=(tile, row),
        out_shape=(jax.ShapeDtypeStruct((L, D), F32), jax.ShapeDtypeStruct((1, D), F32)),
        compiler_params=_params(("arbitrary",)),
    )(h, g, dhn, dres)


def _loss_head(h, g, tgt, n_meta, name):
    L, D = h.shape
    tl = _row_tile(L)
    nt = L // tl

    def body(h_ref, g_ref, t_ref, dh_ref, dg_ref, loss_ref):
        i = pl.program_id(0)
        x = h_ref[...]
        r = lax.rsqrt(jnp.mean(x * x, axis=-1, keepdims=True) + EPS)
        xhat = x * r
        gg = g_ref[...]
        y = xhat * gg
        rows = i * tl + lax.broadcasted_iota(jnp.int32, (tl, 1), 0)
        err = jnp.where(rows >= n_meta, y - t_ref[...], 0.0)
        dy = err * (1.0 / D)
        dxhat = dy * gg
        dh_ref[...] = r * (dxhat - xhat * jnp.mean(dxhat * xhat, axis=-1, keepdims=True))
        dg_part = jnp.sum(_rowsum8(dy * xhat), axis=0, keepdims=True)
        per_row = jnp.mean(err * err, axis=-1, keepdims=True)
        loss_part = jnp.broadcast_to(0.5 * jnp.sum(per_row, axis=0, keepdims=True), (1, 128))

        @pl.when(i == 0)
        def _():
            dg_ref[...] = dg_part
            loss_ref[...] = loss_part

        @pl.when(i > 0)
        def _():
            dg_ref[...] += dg_part
            loss_ref[...] += loss_part

    tile = pl.BlockSpec((tl, D), lambda i: (i, 0))
    row = pl.BlockSpec((1, D), lambda i: (0, 0))
    return pl.pallas_call(
        body, name=name, grid=(nt,), in_specs=[tile, row, tile],
        out_specs=(tile, row, pl.BlockSpec((1, 128), lambda i: (0, 0))),
        out_shape=(jax.ShapeDtypeStruct((L, D), F32), jax.ShapeDtypeStruct((1, D), F32),
                   jax.ShapeDtypeStruct((1, 128), F32)),
        compiler_params=_params(("arbitrary",)),
    )(h, g, tgt)


def _pool_fwd_block(pwin, pw_ref, row0, rb, g, gd, w, t0):
    wv = pwin[pl.ds(row0 + HALO - POOL_PAD, rb + POOL_PAD), g * gd:(g + 1) * gd]
    s = wv
    sh = 1
    while sh < w:
        s = s + pltpu.roll(s, sh, axis=0)
        sh *= 2
    win = s[POOL_PAD:POOL_PAD + rb]
    pt = wv[POOL_PAD:POOL_PAD + rb]
    tg = t0 + lax.broadcasted_iota(jnp.int32, (rb, 1), 0)
    cnt = jnp.minimum(tg + 1, w).astype(F32)
    return win / cnt - pt


def _fill_windows(i, zp_ref, zc_ref, u0w, pwin, tl, cc):
    keep = i > 0
    zp = zp_ref[...]
    u0w[0:HALO, :] = jnp.where(keep, zp[:, :cc] * _sigmoid(zp[:, cc:2 * cc]), 0.0)
    pwin[0:HALO, :] = jnp.where(keep, zp[:, 2 * cc:], 0.0)

    def fill(c, carry):
        b = pl.multiple_of(c * ROW_CHUNK, SUBLANES)
        zc = zc_ref[pl.ds(b, ROW_CHUNK), :]
        u0w[pl.ds(HALO + b, ROW_CHUNK), :] = zc[:, :cc] * _sigmoid(zc[:, cc:2 * cc])
        pwin[pl.ds(HALO + b, ROW_CHUNK), :] = zc[:, 2 * cc:]
        return carry

    lax.fori_loop(0, tl // ROW_CHUNK, fill, 0)


def _mixer_fwd(z, ck, cb, lg, lb, pw, ps, am, name, xchg=None):
    L, ci = z.shape
    kw, cc = ck.shape
    cp = ci - 2 * cc
    ng, gd = pw.shape[0], pw.shape[1]
    tl = _token_tile(L)
    nt = L // tl
    hb = tl // HALO
    rb = _stat_rows(tl)
    tap0 = CONV_PAD - (kw - 1)

    def body(zp_ref, zc_ref, ck_ref, cb_ref, lg_ref, lb_ref, pw_ref, ps_ref, am_ref, y_ref, u1_ref, u0w, pwin):
        i = pl.program_id(0)
        _fill_windows(i, zp_ref, zc_ref, u0w, pwin, tl, cc)

        def conv(c, carry):
            b = pl.multiple_of(c * ROW_CHUNK, SUBLANES)
            w = u0w[pl.ds(b + HALO - CONV_PAD, ROW_CHUNK + CONV_PAD), :]
            acc = jnp.broadcast_to(cb_ref[...], (ROW_CHUNK, cc))
            for j in range(kw):
                acc = acc + ck_ref[j:j + 1, :] * w[tap0 + j:tap0 + j + ROW_CHUNK]
            u1_ref[pl.ds(b, ROW_CHUNK), :] = acc
            return carry

        lax.fori_loop(0, tl // ROW_CHUNK, conv, 0)

        def blocks(k, carry):
            b = pl.multiple_of(k * rb, SUBLANES)
            u1 = u1_ref[pl.ds(b, rb), :]
            xc = u1 - _head_mean(u1, am_ref)
            var = _head_mean(xc * xc, am_ref)
            u2 = (xc * lax.rsqrt(var + EPS)) * lg_ref[...] + lb_ref[...]
            y_ref[pl.ds(b, rb), 0:cc] = (u2 * _sigmoid(u2)).astype(y_ref.dtype)
            for g in range(ng):
                d = _pool_fwd_block(pwin, pw_ref, b, rb, g, gd, POOL_WINDOWS[g], i * tl + b)
                yp = jnp.dot(d.astype(BF16), pw_ref[g].astype(BF16), preferred_element_type=F32)
                yp = yp * ps_ref[:, g * gd:(g + 1) * gd]
                y_ref[pl.ds(b, rb), cc + g * gd:cc + (g + 1) * gd] = yp.astype(y_ref.dtype)
            return carry

        lax.fori_loop(0, tl // rb, blocks, 0)

    def full(a):
        nd = a.ndim
        return pl.BlockSpec(a.shape, lambda i: (0,) * nd)

    out, xo = _call(
        body, name=name, grid=(nt,),
        in_specs=[pl.BlockSpec((HALO, ci), lambda i: (jnp.maximum(i * hb - 1, 0), 0)),
                  pl.BlockSpec((tl, ci), lambda i: (i, 0)),
                  full(ck), full(cb), full(lg), full(lb), full(pw), full(ps), full(am)],
        out_specs=(pl.BlockSpec((tl, cc + cp), lambda i: (i, 0)), pl.BlockSpec((tl, cc), lambda i: (i, 0))),
        out_shape=(jax.ShapeDtypeStruct((L, cc + cp), BF16), jax.ShapeDtypeStruct((L, cc), F32)),
        scratch_shapes=[pltpu.VMEM((HALO + tl, cc), F32), pltpu.VMEM((HALO + tl, cp), F32)],
        sem=("parallel",), args=(z, z, ck, cb, lg, lb, pw, ps, am), xchg=xchg)
    return out if xchg is None else (out, xo)


def _mixer_bwd(z, u1, dy, ck, lg, lb, pw, ps, am, name, xchg=None):
    L, ci = z.shape
    kw, cc = ck.shape
    cp = ci - 2 * cc
    ng, gd = pw.shape[0], pw.shape[1]
    tl = _token_tile(L)
    nt = L // tl
    hb = tl // HALO
    rb = _stat_rows(tl)
    tap0 = CONV_PAD - (kw - 1)

    def body(zp_ref, zc_ref, u1c_ref, u1n_ref, dyc_ref, dyn_ref, ck_ref, lg_ref, lb_ref, pw_ref, ps_ref, am_ref,
             dz_ref, dck_ref, dcb_ref, dlg_ref, dlb_ref, dpw_ref, dps_ref,
             u0w, pwin, du1w, ddw, ew, dkacc, dcb8, dlg8, dlb8, dps8):
        i = pl.program_id(0)
        has_next = i < nt - 1

        @pl.when(i == 0)
        def _():
            dck_ref[...] = jnp.zeros_like(dck_ref)
            dcb_ref[...] = jnp.zeros_like(dcb_ref)
            dlg_ref[...] = jnp.zeros_like(dlg_ref)
            dlb_ref[...] = jnp.zeros_like(dlb_ref)
            dpw_ref[...] = jnp.zeros_like(dpw_ref)
            dps_ref[...] = jnp.zeros_like(dps_ref)

        dkacc[...] = jnp.zeros_like(dkacc)
        dcb8[...] = jnp.zeros_like(dcb8)
        dlg8[...] = jnp.zeros_like(dlg8)
        dlb8[...] = jnp.zeros_like(dlb8)
        dps8[...] = jnp.zeros_like(dps8)

        _fill_windows(i, zp_ref, zc_ref, u0w, pwin, tl, cc)

        def conv_side(u1, dyc, own):
            xc = u1 - _head_mean(u1, am_ref)
            rstd = lax.rsqrt(_head_mean(xc * xc, am_ref) + EPS)
            uh = xc * rstd
            lgv = lg_ref[...]
            u2 = uh * lgv + lb_ref[...]
            sg = _sigmoid(u2)
            du2 = dyc * (sg * (1.0 + u2 * (1.0 - sg)))
            if own:
                dlg8[...] += _rowsum8(du2 * uh)
                dlb8[...] += _rowsum8(du2)
            duh = du2 * lgv
            return rstd * (duh - _head_mean(duh, am_ref) - uh * _head_mean(duh * uh, am_ref))

        def pool_side(dyp, t0, rows):
            dds, es = [], []
            tg = t0 + lax.broadcasted_iota(jnp.int32, (rows, 1), 0)
            for g in range(ng):
                dypre = dyp[:, g * gd:(g + 1) * gd] * ps_ref[:, g * gd:(g + 1) * gd]
                dd = lax.dot_general(dypre.astype(BF16), pw_ref[g].astype(BF16), (((1,), (1,)), ((), ())),
                                     preferred_element_type=F32)
                cnt = jnp.minimum(tg + 1, POOL_WINDOWS[g]).astype(F32)
                dds.append(dd)
                es.append(dd / cnt)
            return jnp.concatenate(dds, axis=-1), jnp.concatenate(es, axis=-1)

        def blocks(k, carry):
            b = pl.multiple_of(k * rb, SUBLANES)
            dyb = dyc_ref[pl.ds(b, rb), :]
            du1 = conv_side(u1c_ref[pl.ds(b, rb), :], dyb[:, :cc], True)
            du1w[pl.ds(b, rb), :] = du1
            dcb8[...] += _rowsum8(du1)
            dyp = dyb[:, cc:]
            dd, e = pool_side(dyp, i * tl + b, rb)
            ddw[pl.ds(b, rb), :] = dd
            ew[pl.ds(b, rb), :] = e
            for g in range(ng):
                d = _pool_fwd_block(pwin, pw_ref, b, rb, g, gd, POOL_WINDOWS[g], i * tl + b)
                db16 = d.astype(BF16)
                dypg = dyp[:, g * gd:(g + 1) * gd]
                ypre = jnp.dot(db16, pw_ref[g].astype(BF16), preferred_element_type=F32)
                dps8[:, g * gd:(g + 1) * gd] += _rowsum8(dypg * ypre)
                dypre = (dypg * ps_ref[:, g * gd:(g + 1) * gd]).astype(BF16)
                dpw_ref[g] += lax.dot_general(db16, dypre, (((0,), (0,)), ((), ())), preferred_element_type=F32)
            return carry

        lax.fori_loop(0, tl // rb, blocks, 0)

        dyn = dyn_ref[...]
        du1n = conv_side(u1n_ref[...], dyn[:, :cc], False)
        du1w[tl:tl + HALO, :] = jnp.where(has_next, du1n, 0.0)
        ddn, en = pool_side(dyn[:, cc:], (i + 1) * tl, HALO)
        ew[tl:tl + HALO, :] = jnp.where(has_next, en, 0.0)

        def taps(c, carry):
            b = pl.multiple_of(c * ROW_CHUNK, SUBLANES)
            w = du1w[pl.ds(b, ROW_CHUNK + CONV_PAD), :]
            u0c = u0w[pl.ds(HALO + b, ROW_CHUNK), :]
            acc = jnp.zeros((ROW_CHUNK, cc), F32)
            for j in range(kw):
                o = kw - 1 - j
                sh = w[o:o + ROW_CHUNK]
                acc = acc + ck_ref[j:j + 1, :] * sh
                dkacc[j] += _rowsum8(u0c * sh)
            zc = zc_ref[pl.ds(b, ROW_CHUNK), :]
            a = zc[:, :cc]
            sg = _sigmoid(zc[:, cc:2 * cc])
            dz_ref[pl.ds(b, ROW_CHUNK), 0:cc] = (acc * sg).astype(dz_ref.dtype)
            dz_ref[pl.ds(b, ROW_CHUNK), cc:2 * cc] = (acc * a * sg * (1.0 - sg)).astype(dz_ref.dtype)
            return carry

        lax.fori_loop(0, tl // ROW_CHUNK, taps, 0)

        def pool_back(k, carry):
            b = pl.multiple_of(k * rb, SUBLANES)
            n = rb + POOL_PAD
            for g in range(ng):
                s = ew[pl.ds(b, n), g * gd:(g + 1) * gd]
                sh = 1
                while sh < POOL_WINDOWS[g]:
                    s = s + pltpu.roll(s, n - sh, axis=0)
                    sh *= 2
                dp = s[0:rb] - ddw[pl.ds(b, rb), g * gd:(g + 1) * gd]
                dz_ref[pl.ds(b, rb), 2 * cc + g * gd:2 * cc + (g + 1) * gd] = dp.astype(dz_ref.dtype)
            return carry

        lax.fori_loop(0, tl // rb, pool_back, 0)

        dck_ref[...] += jnp.sum(dkacc[...], axis=1)
        dcb_ref[...] += jnp.sum(dcb8[...], axis=0, keepdims=True)
        dlg_ref[...] += jnp.sum(dlg8[...], axis=0, keepdims=True)
        dlb_ref[...] += jnp.sum(dlb8[...], axis=0, keepdims=True)
        dps_ref[...] += jnp.sum(dps8[...], axis=0, keepdims=True)

    def full(a):
        nd = a.ndim
        return pl.BlockSpec(a.shape, lambda i: (0,) * nd)

    nhb = L // HALO

    def prev_map(i):
        return (jnp.maximum(i * hb - 1, 0), 0)

    def next_map(i):
        return (jnp.minimum((i + 1) * hb, nhb - 1), 0)

    dcc = cc + cp
    row_cc = jax.ShapeDtypeStruct((1, cc), F32)
    out_shape = (jax.ShapeDtypeStruct((L, ci), BF16), jax.ShapeDtypeStruct((kw, cc), F32), row_cc, row_cc, row_cc,
                 jax.ShapeDtypeStruct((ng, gd, gd), F32), jax.ShapeDtypeStruct((1, cp), F32))
    acc_spec = [pl.BlockSpec((kw, cc), lambda i: (0, 0))] + [pl.BlockSpec((1, cc), lambda i: (0, 0))] * 3 + [
        pl.BlockSpec((ng, gd, gd), lambda i: (0, 0, 0)), pl.BlockSpec((1, cp), lambda i: (0, 0))]
    out, xo = _call(
        body, name=name, grid=(nt,),
        in_specs=[pl.BlockSpec((HALO, ci), prev_map), pl.BlockSpec((tl, ci), lambda i: (i, 0)),
                  pl.BlockSpec((tl, cc), lambda i: (i, 0)), pl.BlockSpec((HALO, cc), next_map),
                  pl.BlockSpec((tl, dcc), lambda i: (i, 0)), pl.BlockSpec((HALO, dcc), next_map),
                  full(ck), full(lg), full(lb), full(pw), full(ps), full(am)],
        out_specs=tuple([pl.BlockSpec((tl, ci), lambda i: (i, 0))] + acc_spec),
        out_shape=out_shape,
        scratch_shapes=[pltpu.VMEM((HALO + tl, cc), F32), pltpu.VMEM((HALO + tl, cp), F32),
                        pltpu.VMEM((tl + HALO, cc), F32), pltpu.VMEM((tl, cp), F32), pltpu.VMEM((tl + HALO, cp), F32),
                        pltpu.VMEM((kw, SUBLANES, cc), F32), pltpu.VMEM((SUBLANES, cc), F32),
                        pltpu.VMEM((SUBLANES, cc), F32), pltpu.VMEM((SUBLANES, cc), F32), pltpu.VMEM((SUBLANES, cp), F32)],
        sem=("arbitrary",), args=(z, z, u1, u1, dy, dy, ck, lg, lb, pw, ps, am), xchg=xchg)
    return out if xchg is None else (out, xo)


def _ffn_tiles(L, f):
    return _token_tile(L), _divisor(f, 256, 128)


def _ffn_conv(win, k_ref, kw, rows):
    o = FFN_PAD - (kw - 1)
    acc = k_ref[0:1, :] * win[o:o + rows]
    for j in range(1, kw):
        acc = acc + k_ref[j:j + 1, :] * win[o + j:o + j + rows]
    return acc


def _ffn_block_fwd(h_mid, g, wup, kf, wdown, name, xchg=None):
    L, D = h_mid.shape
    f = wdown.shape[0]
    kw = kf.shape[0]
    tl = _token_tile(L)
    tc = _divisor(f, 256, 128)
    nj = f // tc
    nt = L // tl
    pad = 2 * SUBLANES
    hb = tl // pad
    tap0 = pad - (kw - 1)

    def body(hp_ref, hc_ref, g_ref, wg_ref, wv_ref, kg_ref, kv_ref, wd_ref, out_ref, hn_ref, ug_ref, act_ref, hn_ext, acc):
        i = pl.program_id(0)
        kb = pl.program_id(1)

        @pl.when(kb == 0)
        def _():
            gg = g_ref[...]

            def norm(x):
                r = lax.rsqrt(jnp.mean(x * x, axis=-1, keepdims=True) + EPS)
                return ((x * r) * gg).astype(BF16)

            hn_ext[0:pad, :] = jnp.where(i > 0, norm(hp_ref[...]), jnp.zeros((pad, D), BF16))
            hn = norm(hc_ref[...])
            hn_ext[pad:pad + tl, :] = hn
            hn_ref[...] = hn

        hn = hn_ext[...]
        ugg = jnp.dot(hn, wg_ref[...], preferred_element_type=F32)
        ugv = jnp.dot(hn, wv_ref[...], preferred_element_type=F32)
        ug_ref[0] = ugg[pad:]
        ug_ref[1] = ugv[pad:]
        gate = kg_ref[0:1, :] * ugg[tap0:tap0 + tl]
        val = kv_ref[0:1, :] * ugv[tap0:tap0 + tl]
        for j in range(1, kw):
            gate = gate + kg_ref[j:j + 1, :] * ugg[tap0 + j:tap0 + j + tl]
            val = val + kv_ref[j:j + 1, :] * ugv[tap0 + j:tap0 + j + tl]
        act = ((gate * _sigmoid(gate)) * val).astype(BF16)
        act_ref[...] = act
        prod = jnp.dot(act, wd_ref[...], preferred_element_type=F32)

        @pl.when(kb == 0)
        def _():
            acc[...] = prod

        @pl.when(kb > 0)
        def _():
            acc[...] += prod

        @pl.when(kb == nj - 1)
        def _():
            out_ref[...] = acc[...] + hc_ref[...]

    out, xo = _call(
        body, name=name, grid=(nt, nj),
        in_specs=[pl.BlockSpec((pad, D), lambda i, k: (jnp.maximum(i * hb - 1, 0), 0)),
                  pl.BlockSpec((tl, D), lambda i, k: (i, 0)),
                  pl.BlockSpec((1, D), lambda i, k: (0, 0)),
                  pl.BlockSpec((D, tc), lambda i, k: (0, k)), pl.BlockSpec((D, tc), lambda i, k: (0, k + nj)),
                  pl.BlockSpec((kw, tc), lambda i, k: (0, k)), pl.BlockSpec((kw, tc), lambda i, k: (0, k + nj)),
                  pl.BlockSpec((tc, D), lambda i, k: (k, 0))],
        out_specs=(pl.BlockSpec((tl, D), lambda i, k: (i, 0)), pl.BlockSpec((tl, D), lambda i, k: (i, 0)),
                   pl.BlockSpec((2, tl, tc), lambda i, k: (0, i, k)), pl.BlockSpec((tl, tc), lambda i, k: (i, k))),
        out_shape=(jax.ShapeDtypeStruct((L, D), F32), jax.ShapeDtypeStruct((L, D), BF16),
                   jax.ShapeDtypeStruct((2, L, f), F32), jax.ShapeDtypeStruct((L, f), BF16)),
        scratch_shapes=[pltpu.VMEM((pad + tl, D), BF16), pltpu.VMEM((tl, D), F32)],
        sem=("parallel", "arbitrary"), args=(h_mid, h_mid, g, wup, wup, kf, kf, wdown), xchg=xchg)
    return out if xchg is None else (out, xo)


def _ffn_bwd(ug0, dact, kf, name, xchg=None):
    _, L, f = ug0.shape
    kw = kf.shape[0]
    tl, tc = _ffn_tiles(L, f)
    nj = f // tc
    nt = L // tl
    hb = tl // FFN_PAD
    nhb = L // FFN_PAD

    def body(gp_ref, gc_ref, gn_ref, vp_ref, vc_ref, vn_ref, dc_ref, dn_ref, kg_ref, kv_ref, du_ref, dk_ref,
             gwin, vwin, dgw, dvw, dk8):
        i = pl.program_id(1)
        keep = i > 0
        has_next = i < nt - 1

        @pl.when(i == 0)
        def _():
            dk_ref[...] = jnp.zeros_like(dk_ref)

        dk8[...] = jnp.zeros_like(dk8)

        gwin[0:FFN_PAD, :] = jnp.where(keep, gp_ref[...], 0.0)
        vwin[0:FFN_PAD, :] = jnp.where(keep, vp_ref[...], 0.0)
        gwin[FFN_PAD + tl:FFN_PAD + tl + FFN_PAD, :] = gn_ref[...]
        vwin[FFN_PAD + tl:FFN_PAD + tl + FFN_PAD, :] = vn_ref[...]

        def copy(c, carry):
            b = pl.multiple_of(c * ROW_CHUNK, SUBLANES)
            gwin[pl.ds(FFN_PAD + b, ROW_CHUNK), :] = gc_ref[pl.ds(b, ROW_CHUNK), :]
            vwin[pl.ds(FFN_PAD + b, ROW_CHUNK), :] = vc_ref[pl.ds(b, ROW_CHUNK), :]
            return carry

        lax.fori_loop(0, tl // ROW_CHUNK, copy, 0)

        def grads(b, rows, dact):
            gate = _ffn_conv(gwin[pl.ds(b, rows + FFN_PAD), :], kg_ref, kw, rows)
            val = _ffn_conv(vwin[pl.ds(b, rows + FFN_PAD), :], kv_ref, kw, rows)
            sg = _sigmoid(gate)
            return dact * val * (sg * (1.0 + gate * (1.0 - sg))), dact * (gate * sg)

        def first(c, carry):
            b = pl.multiple_of(c * ROW_CHUNK, SUBLANES)
            dg, dv = grads(b, ROW_CHUNK, dc_ref[pl.ds(b, ROW_CHUNK), :])
            dgw[pl.ds(b, ROW_CHUNK), :] = dg
            dvw[pl.ds(b, ROW_CHUNK), :] = dv
            return carry

        lax.fori_loop(0, tl // ROW_CHUNK, first, 0)
        dgn, dvn = grads(tl, FFN_PAD, dn_ref[...])
        dgw[tl:tl + FFN_PAD, :] = jnp.where(has_next, dgn, 0.0)
        dvw[tl:tl + FFN_PAD, :] = jnp.where(has_next, dvn, 0.0)

        def second(c, carry):
            b = pl.multiple_of(c * ROW_CHUNK, SUBLANES)
            for h, (dw, xwin, k_ref) in enumerate(((dgw, gwin, kg_ref), (dvw, vwin, kv_ref))):
                w = dw[pl.ds(b, ROW_CHUNK + FFN_PAD), :]
                xc = xwin[pl.ds(FFN_PAD + b, ROW_CHUNK), :]
                acc = None
                for j in range(kw):
                    o = kw - 1 - j
                    sh = w[o:o + ROW_CHUNK]
                    term = k_ref[j:j + 1, :] * sh
                    acc = term if acc is None else acc + term
                    dk8[h, j] += _rowsum8(xc * sh)
                du_ref[h, pl.ds(b, ROW_CHUNK), :] = acc.astype(du_ref.dtype)
            return carry

        lax.fori_loop(0, tl // ROW_CHUNK, second, 0)
        dk_ref[...] += jnp.sum(dk8[...], axis=2)

    def prev_g(j, i):
        return (jnp.maximum(i * hb - 1, 0), j)

    def next_g(j, i):
        return (jnp.minimum((i + 1) * hb, nhb - 1), j)

    def half(h, rows, idx):
        return pl.BlockSpec((None, rows, tc), lambda j, i: (h,) + idx(j, i))

    def tile(j, i):
        return (i, j)

    out, xo = _call(
        body, name=name, grid=(nj, nt),
        in_specs=[half(0, FFN_PAD, prev_g), half(0, tl, tile), half(0, FFN_PAD, next_g),
                  half(1, FFN_PAD, prev_g), half(1, tl, tile), half(1, FFN_PAD, next_g),
                  pl.BlockSpec((tl, tc), lambda j, i: (i, j)), pl.BlockSpec((FFN_PAD, tc), next_g),
                  pl.BlockSpec((kw, tc), lambda j, i: (0, j)), pl.BlockSpec((kw, tc), lambda j, i: (0, j + nj))],
        out_specs=(pl.BlockSpec((2, tl, tc), lambda j, i: (0, i, j)), pl.BlockSpec((2, kw, tc), lambda j, i: (0, 0, j))),
        out_shape=(jax.ShapeDtypeStruct((2, L, f), BF16), jax.ShapeDtypeStruct((2, kw, f), F32)),
        scratch_shapes=[pltpu.VMEM((tl + 2 * FFN_PAD, tc), F32), pltpu.VMEM((tl + 2 * FFN_PAD, tc), F32),
                        pltpu.VMEM((tl + FFN_PAD, tc), F32), pltpu.VMEM((tl + FFN_PAD, tc), F32),
                        pltpu.VMEM((2, kw, SUBLANES, tc), F32)],
        sem=("parallel", "arbitrary"), args=(ug0, ug0, ug0, ug0, ug0, ug0, dact, dact, kf, kf), xchg=xchg)
    return out if xchg is None else (out, xo)


def _adamw_math(w, g, m, v):
    m = ADAM_B1 * m + (1.0 - ADAM_B1) * g
    v = ADAM_B2 * v + (1.0 - ADAM_B2) * (g * g)
    m_hat = m / (1.0 - ADAM_B1 ** ADAM_STEP)
    v_hat = v / (1.0 - ADAM_B2 ** ADAM_STEP)
    delta = -ADAM_LR * (m_hat / (jnp.sqrt(v_hat) + ADAM_EPS) + ADAM_WD * w)
    return delta, m, v


def _sum_parts(parts_ref, idx):
    g = parts_ref[(0,) + idx].astype(F32)
    for q in range(1, N_DEV):
        g = g + parts_ref[(q,) + idx].astype(F32)
    return g


def _adamw_big(parts, w, m, v, name):
    nl, R, C = w.shape
    tr = _divisor(R, 256, 2 * SUBLANES)

    def body(*refs):
        p_refs = refs[:nl]
        w_ref, m_ref, v_ref, g_ref, d_ref, nm_ref, nv_ref = refs[nl:]
        layer = pl.program_id(0)
        for k in range(nl):
            @pl.when(layer == k)
            def _(k=k):
                g = _sum_parts(p_refs[k], ())
                d, nm, nv = _adamw_math(w_ref[0], g, m_ref[0], v_ref[0])
                g_ref[0] = g
                d_ref[0] = d
                nm_ref[0] = nm
                nv_ref[0] = nv

    def part_spec(k):
        return pl.BlockSpec((N_DEV, tr, C), lambda l, r: (0, jnp.where(l == k, r, 0), 0))

    blk = pl.BlockSpec((1, tr, C), lambda l, r: (l, r, 0))
    shp = jax.ShapeDtypeStruct((nl, R, C), F32)
    return pl.pallas_call(
        body, name=name, grid=(nl, R // tr),
        in_specs=[part_spec(k) for k in range(nl)] + [blk, blk, blk],
        out_specs=(blk, blk, blk, blk), out_shape=(shp, shp, shp, shp),
        compiler_params=_params(("arbitrary", "arbitrary")),
    )(*parts, w, m, v)


def _adamw_small(entries, name):
    n = len(entries)
    uniq = []
    for e in entries:
        if not any(e[0] is u for u in uniq):
            uniq.append(e[0])
    pidx = [next(k for k, u in enumerate(uniq) if u is e[0]) for e in entries]
    npart = len(uniq)

    def body(*refs):
        p_refs = refs[:npart]
        wmv = refs[npart:npart + 3 * n]
        outs = refs[npart + 3 * n:]
        for t, e in enumerate(entries):
            lo, w = e[1], e[2]
            rows = w.shape[0]
            pr = p_refs[pidx[t]]
            g = pr[0, lo:lo + rows].astype(F32)
            for q in range(1, N_DEV):
                g = g + pr[q, lo:lo + rows].astype(F32)
            d, nm, nv = _adamw_math(wmv[3 * t][...], g, wmv[3 * t + 1][...], wmv[3 * t + 2][...])
            outs[4 * t][...] = g
            outs[4 * t + 1][...] = d
            outs[4 * t + 2][...] = nm
            outs[4 * t + 3][...] = nv

    vm = pl.BlockSpec(memory_space=pltpu.VMEM)
    args = list(uniq)
    out_shape = []
    for e in entries:
        args += [e[2], e[3], e[4]]
        out_shape += [jax.ShapeDtypeStruct(e[2].shape, F32)] * 4
    res = pl.pallas_call(
        body, name=name, in_specs=[vm] * len(args), out_specs=tuple([vm] * len(out_shape)),
        out_shape=tuple(out_shape), compiler_params=_params(),
    )(*args)
    return [tuple(res[4 * t:4 * t + 4]) for t in range(n)]


def _head_matrix(cc):
    bw = min(256, cc)
    r = lax.broadcasted_iota(jnp.int32, (bw, bw), 0) // HEAD_DIM
    c = lax.broadcasted_iota(jnp.int32, (bw, bw), 1) // HEAD_DIM
    return jnp.where(r == c, 1.0 / HEAD_DIM, 0.0).astype(BF16)


def _cols_from_shards(g):
    nd = g.ndim
    perm = tuple(range(1, nd - 1)) + (0, nd - 1)
    t = jnp.transpose(g, perm)
    return t.reshape(t.shape[:-2] + (t.shape[-2] * t.shape[-1],))


def _cols_to_shards(a):
    nd = a.ndim
    t = a.reshape(a.shape[:-1] + (N_DEV, a.shape[-1] // N_DEV))
    perm = (nd - 1,) + tuple(range(nd - 1)) + (nd,)
    return jnp.transpose(t, perm)


def kernel(x, meta_tokens, norm1_g, w_in, conv_dw_k, conv_dw_b, conv_ln_g, conv_ln_b, pool_w, pool_scale, w_out, norm2_g, w_up, ffn_dw_k, w_down, final_g, loss_target, m_meta_tokens, m_norm1_g, m_w_in, m_conv_dw_k, m_conv_dw_b, m_conv_ln_g, m_conv_ln_b, m_pool_w, m_pool_scale, m_w_out, m_norm2_g, m_w_up, m_ffn_dw_k, m_w_down, m_final_g, v_meta_tokens, v_norm1_g, v_w_in, v_conv_dw_k, v_conv_dw_b, v_conv_ln_g, v_conv_ln_b, v_pool_w, v_pool_scale, v_w_out, v_norm2_g, v_w_up, v_ffn_dw_k, v_w_down, v_final_g):
    depth, D = norm1_g.shape
    n_meta = meta_tokens.shape[0]
    seq = x.shape[1]
    L = n_meta + seq
    cc = conv_dw_b.shape[1]
    ng, gd = pool_w.shape[1], pool_w.shape[2]
    f = w_down.shape[1] * N_DEV

    def cols(g):
        w = _cols_from_shards(g)
        return w, w.T

    def rows(g):
        w = g.reshape(-1, g.shape[-1])
        return w, w.T

    b16 = lambda a: a.astype(BF16)
    (g_in0, g_out0, g_ck, g_kf, g_meta) = _exchange([b16(w_in[0]), b16(w_out[0]), conv_dw_k, ffn_dw_k, meta_tokens],
                                                    ["gather"] * 5, "gather_first")
    ck_full = _cols_from_shards(g_ck)
    kf_full = _cols_from_shards(g_kf)
    meta_full = _cols_from_shards(g_meta)
    am = _head_matrix(cc)
    win, wout, wup, wdown = [None] * depth, [None] * depth, [None] * depth, [None] * depth
    win[0] = cols(g_in0)
    wout[0] = rows(g_out0)

    h = jnp.concatenate([meta_full, x[0]], axis=0)
    saved = []
    for l in range(depth):
        more = l + 1 < depth
        hn1 = _rms_fwd(h, norm1_g[l:l + 1], f"rms1_fwd_{l}")
        if l == 0:
            z, (g_down,) = _mm(hn1, win[l][0], f"in_proj_{l}", tn_cap=768, xchg=([b16(w_down[l])], ["gather"]))
            wdown[l] = rows(g_down)
            (ymix, u1), (g_up,) = _mixer_fwd(z, ck_full[l], conv_dw_b[l:l + 1], conv_ln_g[l:l + 1], conv_ln_b[l:l + 1],
                                             pool_w[l], pool_scale[l:l + 1], am, f"mixer_fwd_{l}",
                                             xchg=([b16(w_up[l])], ["gather"]))
            wup[l] = cols(g_up)
        else:
            z = _mm(hn1, win[l][0], f"in_proj_{l}", tn_cap=768)
            ymix, u1 = _mixer_fwd(z, ck_full[l], conv_dw_b[l:l + 1], conv_ln_g[l:l + 1], conv_ln_b[l:l + 1], pool_w[l],
                                  pool_scale[l:l + 1], am, f"mixer_fwd_{l}")
        if more:
            h_mid, (g_in,) = _mm(ymix, wout[l][0], f"out_proj_{l}", res=h, tn_cap=512, xchg=([b16(w_in[l + 1])], ["gather"]))
            win[l + 1] = cols(g_in)
            nxt = [b16(w_out[l + 1]), b16(w_up[l + 1]), b16(w_down[l + 1])]
            (h_out, hn2, ug0, act), got = _ffn_block_fwd(h_mid, norm2_g[l:l + 1], wup[l][0], kf_full[l], wdown[l][0],
                                                         f"ffn_fwd_{l}", xchg=(nxt, ["gather"] * 3))
            wout[l + 1], wup[l + 1], wdown[l + 1] = rows(got[0]), cols(got[1]), rows(got[2])
        else:
            h_mid = _mm(ymix, wout[l][0], f"out_proj_{l}", res=h, tn_cap=512)
            h_out, hn2, ug0, act = _ffn_block_fwd(h_mid, norm2_g[l:l + 1], wup[l][0], kf_full[l], wdown[l][0], f"ffn_fwd_{l}")
        saved.append((h, hn1, z, u1, ymix, h_mid, hn2, ug0, act))
        h = h_out

    tgt = jnp.concatenate([jnp.zeros((n_meta, D), F32), loss_target[0]], axis=0)
    dh, d_final_g, loss_part = _loss_head(h, final_g.reshape(1, D), tgt, n_meta, "loss_head")

    def row_shards(gm):
        return b16(gm.reshape(N_DEV, -1, gm.shape[-1]))

    def col_shards(gm):
        return b16(_cols_to_shards(gm))

    gw = {k: [None] * depth for k in ("ck", "cb", "lg", "lb", "pw", "ps", "kf", "n1", "n2")}
    parts = {k: [None] * depth for k in ("in", "out", "up", "down")}
    for l in reversed(range(depth)):
        h_in, hn1, z, u1, ymix, h_mid, hn2, ug0, act = saved[l]
        dact = _mm(dh, wdown[l][1], f"down_proj_bwd_{l}")
        g_down = _mm_tn(act, dh, f"down_proj_wgrad_{l}", tq_cap=512)
        (dug0, dkf), (parts["down"][l],) = _ffn_bwd(ug0, dact, kf_full[l], f"ffn_bwd_{l}", xchg=([row_shards(g_down)], ["a2a"]))
        gw["kf"][l] = jnp.concatenate([dkf[0], dkf[1]], axis=-1)
        g_up = _mm_tn(hn2, dug0, f"up_proj_wgrad_{l}", halves=2)
        dhn2, (parts["up"][l],) = _mm(dug0, wup[l][1], f"up_proj_bwd_{l}", halves=2, tn_cap=512,
                                      xchg=([col_shards(g_up)], ["a2a"]))
        dh_mid, gw["n2"][l] = _rms_bwd(h_mid, norm2_g[l:l + 1], dhn2, dh, f"rms2_bwd_{l}")
        dymix = _mm(dh_mid, wout[l][1], f"out_proj_bwd_{l}", tn_cap=512)
        g_out = _mm_tn(ymix, dh_mid, f"out_proj_wgrad_{l}", tq_cap=512)
        (dz, gw["ck"][l], gw["cb"][l], gw["lg"][l], gw["lb"][l], gw["pw"][l], gw["ps"][l]), (parts["out"][l],) = _mixer_bwd(
            z, u1, dymix, ck_full[l], conv_ln_g[l:l + 1], conv_ln_b[l:l + 1], pool_w[l], pool_scale[l:l + 1], am,
            f"mixer_bwd_{l}", xchg=([row_shards(g_out)], ["a2a"]))
        g_in = _mm_tn(hn1, dz, f"in_proj_wgrad_{l}", tq_cap=768)
        dhn1, (parts["in"][l],) = _mm(dz, win[l][1], f"in_proj_bwd_{l}", tn_cap=512, xchg=([col_shards(g_in)], ["a2a"]))
        dh, gw["n1"][l] = _rms_bwd(h_in, norm1_g[l:l + 1], dhn1, dh_mid, f"rms1_bwd_{l}")
    grad_x = dh[n_meta:][None]
    d_meta = dh[:n_meta]

    zero_row = jnp.zeros((1, D), F32)
    pack_d = jnp.concatenate(gw["n1"] + gw["n2"] + [d_final_g, jnp.broadcast_to(loss_part[:, :1], (1, D)), zero_row, zero_row], axis=0)
    pack_c = jnp.concatenate(gw["cb"] + gw["lg"] + gw["lb"] + gw["ps"], axis=0)
    pack_pw = jnp.stack(gw["pw"]).reshape(depth * ng * gd, gd)
    src = [_cols_to_shards(jnp.stack(gw["ck"])), _cols_to_shards(jnp.stack(gw["kf"])), _cols_to_shards(d_meta),
           pack_d, pack_c, pack_pw]
    r_ck, r_kf, r_meta, r_d, r_c, r_pw = _exchange(src, ["a2a"] * 3 + ["gather"] * 3, "exchange_small_grads")

    big = {
        "w_in": _adamw_big(parts["in"], w_in, m_w_in, v_w_in, "adamw_w_in"),
        "w_out": _adamw_big(parts["out"], w_out, m_w_out, v_w_out, "adamw_w_out"),
        "w_up": _adamw_big(parts["up"], w_up, m_w_up, v_w_up, "adamw_w_up"),
        "w_down": _adamw_big(parts["down"], w_down, m_w_down, v_w_down, "adamw_w_down"),
    }
    kwid = conv_dw_k.shape[1]
    fkw = ffn_dw_k.shape[1]
    row = lambda a: a.reshape(1, -1)
    entries = [
        (r_d, 0, norm1_g, m_norm1_g, v_norm1_g),
        (r_d, depth, norm2_g, m_norm2_g, v_norm2_g),
        (r_d, 2 * depth, row(final_g), row(m_final_g), row(v_final_g)),
        (r_c, 0, conv_dw_b, m_conv_dw_b, v_conv_dw_b),
        (r_c, depth, conv_ln_g, m_conv_ln_g, v_conv_ln_g),
        (r_c, 2 * depth, conv_ln_b, m_conv_ln_b, v_conv_ln_b),
        (r_c, 3 * depth, pool_scale, m_pool_scale, v_pool_scale),
        (r_pw, 0, pool_w.reshape(-1, gd), m_pool_w.reshape(-1, gd), v_pool_w.reshape(-1, gd)),
        (r_ck.reshape(N_DEV, depth * kwid, -1), 0, conv_dw_k.reshape(depth * kwid, -1),
         m_conv_dw_k.reshape(depth * kwid, -1), v_conv_dw_k.reshape(depth * kwid, -1)),
        (r_kf.reshape(N_DEV, depth * fkw, -1), 0, ffn_dw_k.reshape(depth * fkw, -1),
         m_ffn_dw_k.reshape(depth * fkw, -1), v_ffn_dw_k.reshape(depth * fkw, -1)),
        (r_meta, 0, meta_tokens, m_meta_tokens, v_meta_tokens),
        (r_d, 2 * depth + 1, zero_row, zero_row, zero_row),
    ]
    small = _adamw_small(entries, "adamw_small")
    names = ["norm1_g", "norm2_g", "final_g", "conv_dw_b", "conv_ln_g", "conv_ln_b", "pool_scale", "pool_w",
             "conv_dw_k", "ffn_dw_k", "meta_tokens"]
    shapes = {"final_g": final_g.shape, "pool_w": pool_w.shape, "conv_dw_k": conv_dw_k.shape, "ffn_dw_k": ffn_dw_k.shape}
    res = dict(big)
    for nme, quad in zip(names, small[:-1]):
        res[nme] = tuple(a.reshape(shapes[nme]) if nme in shapes else a for a in quad)
    loss = small[-1][0][0, 0]

    order = ["meta_tokens", "norm1_g", "w_in", "conv_dw_k", "conv_dw_b", "conv_ln_g", "conv_ln_b", "pool_w", "pool_scale",
             "w_out", "norm2_g", "w_up", "ffn_dw_k", "w_down", "final_g"]
    return (loss, grad_x, *[res[k][0] for k in order], *[res[k][1] for k in order], *[res[k][2] for k in order],
            *[res[k][3] for k in order])
```

```python
import functools

import jax
import jax.numpy as jnp
from jax import lax
from jax.experimental import pallas as pl
from jax.experimental.pallas import tpu as pltpu

F32 = jnp.float32
BF16 = jnp.bfloat16

EPS = 1e-6
HEAD_DIM = 64
POOL_WINDOWS = (2, 4, 8, 16)
ADAM_LR = 0.001
ADAM_B1 = 0.9
ADAM_B2 = 0.999
ADAM_EPS = 1e-08
ADAM_WD = 0.01
ADAM_STEP = 10

N_DEV = 8
SUBLANES = 8
HALO = 48
CONV_PAD = 32
POOL_PAD = 16
FFN_PAD = 8
ROW_CHUNK = 24
CONV3_ROWS = 48
MAX_TILE_ROWS = 1024
VMEM_LIMIT = 52 * 1024 * 1024


def _divisor(n, cap, mult):
    best = None
    for d in range(mult, min(n, cap) + 1, mult):
        if n % d == 0:
            best = d
    return n if best is None else best


def _token_tile(L):
    return _divisor(L, MAX_TILE_ROWS, HALO)


def _row_tile(L):
    return _divisor(L, 320, 2 * SUBLANES)


def _stat_rows(tl):
    return _divisor(tl, 256, SUBLANES)


def _params(sem=None):
    return pltpu.CompilerParams(dimension_semantics=sem, vmem_limit_bytes=VMEM_LIMIT)


def _rowsum8(x):
    acc = x[0:SUBLANES]
    for k in range(1, x.shape[0] // SUBLANES):
        acc = acc + x[k * SUBLANES:(k + 1) * SUBLANES]
    return acc


def _sigmoid(x):
    return jax.nn.sigmoid(x)


def _head_mean(x, am_ref):
    bw = am_ref.shape[0]
    am = am_ref[...]
    outs = []
    for blk in range(x.shape[1] // bw):
        xb = x[:, blk * bw:(blk + 1) * bw]
        hi = xb.astype(BF16)
        lo = (xb - hi.astype(F32)).astype(BF16)
        outs.append(jnp.dot(hi, am, preferred_element_type=F32) + jnp.dot(lo, am, preferred_element_type=F32))
    return outs[0] if len(outs) == 1 else jnp.concatenate(outs, axis=-1)


def _xchg_out_shapes(srcs, modes):
    out = []
    for s, m in zip(srcs, modes):
        shp = ((N_DEV,) + tuple(s.shape)) if m == "gather" else tuple(s.shape)
        out.append(jax.ShapeDtypeStruct(shp, s.dtype))
    return out


def _xchg_sems(n):
    return [pltpu.SemaphoreType.DMA((n, N_DEV - 1)), pltpu.SemaphoreType.DMA((n, N_DEV - 1)), pltpu.SemaphoreType.DMA((n,))]


def _xchg_ops(src_refs, out_refs, sems, modes):
    n = len(src_refs)
    send_sems, recv_sems, local_sems = sems
    x, y, c = lax.axis_index("x"), lax.axis_index("y"), lax.axis_index("c")
    me = 4 * x + 2 * y + c

    def peer(d):
        return (x ^ ((d >> 2) & 1), y ^ ((d >> 1) & 1), c ^ (d & 1))

    def peer_id(d):
        px, py, pc = peer(d)
        return 4 * px + 2 * py + pc

    def remote(t, d):
        src = src_refs[t] if modes[t] == "gather" else src_refs[t].at[peer_id(d)]
        return pltpu.make_async_remote_copy(
            src_ref=src, dst_ref=out_refs[t].at[me], send_sem=send_sems.at[t, d - 1], recv_sem=recv_sems.at[t, d - 1],
            device_id=peer(d), device_id_type=pl.DeviceIdType.MESH)

    def arrival(t, d):
        src = src_refs[t] if modes[t] == "gather" else src_refs[t].at[me]
        return pltpu.make_async_remote_copy(
            src_ref=src, dst_ref=out_refs[t].at[peer_id(d)], send_sem=send_sems.at[t, d - 1],
            recv_sem=recv_sems.at[t, d - 1], device_id=peer(d), device_id_type=pl.DeviceIdType.MESH)

    def local(t):
        src = src_refs[t] if modes[t] == "gather" else src_refs[t].at[me]
        return pltpu.make_async_copy(src, out_refs[t].at[me], local_sems.at[t])

    def start():
        for t in range(n):
            local(t).start()
        for t in range(n):
            for d in range(1, N_DEV):
                remote(t, d).start()

    def wait():
        for t in range(n):
            for d in range(1, N_DEV):
                arrival(t, d).wait_recv()
        for t in range(n):
            for d in range(1, N_DEV):
                remote(t, d).wait_send()
        for t in range(n):
            local(t).wait()

    return start, wait


def _exchange(srcs, modes, name):
    n = len(srcs)

    def body(*refs):
        start, wait = _xchg_ops(refs[:n], refs[n:2 * n], refs[2 * n:], modes)
        start()
        wait()

    any_spec = pl.BlockSpec(memory_space=pl.ANY)
    return pl.pallas_call(
        body, name=name, out_shape=tuple(_xchg_out_shapes(srcs, modes)),
        in_specs=[any_spec] * n, out_specs=tuple([any_spec] * n),
        scratch_shapes=_xchg_sems(n),
        compiler_params=pltpu.CompilerParams(has_side_effects=True),
    )(*srcs)


def _call(body, *, name, grid, in_specs, out_specs, out_shape, args, scratch_shapes=(), sem=None, xchg=None):
    single = not isinstance(out_shape, (tuple, list))
    outs_shape = [out_shape] if single else list(out_shape)
    outs_spec = [out_specs] if single else list(out_specs)
    if xchg is None:
        res = pl.pallas_call(
            body, name=name, grid=grid, in_specs=list(in_specs), out_specs=out_specs, out_shape=out_shape,
            scratch_shapes=list(scratch_shapes), compiler_params=_params(sem))(*args)
        return res, ()
    srcs, modes = xchg
    n_in, n_out, n_scr, nx = len(in_specs), len(outs_shape), len(scratch_shapes), len(srcs)

    def wrapped(*refs):
        ins = refs[:n_in]
        xs = refs[n_in:n_in + nx]
        o0 = n_in + nx
        outs = refs[o0:o0 + n_out]
        xo = refs[o0 + n_out:o0 + n_out + nx]
        s0 = o0 + n_out + nx
        scr = refs[s0:s0 + n_scr]
        start, wait = _xchg_ops(xs, xo, refs[s0 + n_scr:], modes)
        first = functools.reduce(jnp.logical_and, [pl.program_id(a) == 0 for a in range(len(grid))])
        last = functools.reduce(jnp.logical_and, [pl.program_id(a) == grid[a] - 1 for a in range(len(grid))])

        @pl.when(first)
        def _():
            start()

        body(*ins, *outs, *scr)

        @pl.when(last)
        def _():
            wait()

    any_spec = pl.BlockSpec(memory_space=pl.ANY)
    res = pl.pallas_call(
        wrapped, name=name, grid=grid, in_specs=list(in_specs) + [any_spec] * nx,
        out_specs=tuple(outs_spec + [any_spec] * nx), out_shape=tuple(outs_shape + _xchg_out_shapes(srcs, modes)),
        scratch_shapes=list(scratch_shapes) + _xchg_sems(nx),
        compiler_params=_params(("arbitrary",) * len(grid)))(*args, *srcs)
    comp = res[:n_out]
    return (comp[0] if single else tuple(comp)), tuple(res[n_out:])


def _rms_fwd(h, g, name):
    L, D = h.shape
    tl = _row_tile(L)

    def body(h_ref, g_ref, o_ref):
        x = h_ref[...]
        r = lax.rsqrt(jnp.mean(x * x, axis=-1, keepdims=True) + EPS)
        o_ref[...] = ((x * r) * g_ref[...]).astype(o_ref.dtype)

    return pl.pallas_call(
        body, name=name, grid=(L // tl,),
        in_specs=[pl.BlockSpec((tl, D), lambda i: (i, 0)), pl.BlockSpec((1, D), lambda i: (0, 0))],
        out_specs=pl.BlockSpec((tl, D), lambda i: (i, 0)),
        out_shape=jax.ShapeDtypeStruct((L, D), BF16),
        compiler_params=_params(("parallel",)),
    )(h, g)


def _mm(a, b, name, *, res=None, out_dtype=F32, tn_cap=1408, halves=1, xchg=None):
    if halves > 1:
        _, M, kh = a.shape
        K = kh * halves
    else:
        M, K = a.shape
        kh = K
    N = b.shape[1]
    tm = _token_tile(M)
    tn = _divisor(N, tn_cap, 128)
    tk = kh if kh <= 2816 else _divisor(kh, 2816, 128)
    kper = kh // tk
    nk = halves * kper
    grid = (M // tm, N // tn, nk)

    def body(*refs):
        if res is None:
            a_ref, b_ref, o_ref = refs[:3]
            r_ref = None
            scratch = refs[3:]
        else:
            a_ref, b_ref, r_ref, o_ref = refs[:4]
            scratch = refs[4:]
        av = a_ref[0] if halves > 1 else a_ref[...]
        prod = jnp.dot(av.astype(BF16), b_ref[...], preferred_element_type=F32)
        if nk == 1:
            if r_ref is not None:
                prod = prod + r_ref[...]
            o_ref[...] = prod.astype(o_ref.dtype)
        else:
            acc = scratch[0]
            k = pl.program_id(2)

            @pl.when(k == 0)
            def _():
                acc[...] = prod

            @pl.when(k > 0)
            def _():
                acc[...] += prod

            @pl.when(k == nk - 1)
            def _():
                tot = acc[...]
                if r_ref is not None:
                    tot = tot + r_ref[...]
                o_ref[...] = tot.astype(o_ref.dtype)

    if halves > 1:
        a_spec = pl.BlockSpec((1, tm, tk), lambda i, j, k: (k // kper, i, k % kper))
    else:
        a_spec = pl.BlockSpec((tm, tk), lambda i, j, k: (i, k))
    in_specs = [a_spec, pl.BlockSpec((tk, tn), lambda i, j, k: (k, j))]
    args = [a, b]
    if res is not None:
        in_specs.append(pl.BlockSpec((tm, tn), lambda i, j, k: (i, j)))
        args.append(res)
    out, xo = _call(
        body, name=name, grid=grid, in_specs=in_specs,
        out_specs=pl.BlockSpec((tm, tn), lambda i, j, k: (i, j)),
        out_shape=jax.ShapeDtypeStruct((M, N), out_dtype),
        scratch_shapes=[pltpu.VMEM((tm, tn), F32)] if nk > 1 else [],
        sem=("parallel", "parallel", "arbitrary"), args=args, xchg=xchg)
    return out if xchg is None else (out, xo)


def _mm_tn(a, b, name, *, halves=1, tq_cap=1408):
    L, P = a.shape
    if halves > 1:
        qh = b.shape[2]
        Q = qh * halves
    else:
        Q = b.shape[1]
        qh = Q
    tl = _token_tile(L)
    tp = _divisor(P, 1408, 128)
    tq = _divisor(qh, tq_cap, 128)
    qper = qh // tq
    grid = (P // tp, Q // tq, L // tl)

    def body(a_ref, b_ref, o_ref):
        bv = b_ref[0] if halves > 1 else b_ref[...]
        prod = lax.dot_general(a_ref[...].astype(BF16), bv.astype(BF16), (((0,), (0,)), ((), ())),
                               preferred_element_type=F32)
        l = pl.program_id(2)

        @pl.when(l == 0)
        def _():
            o_ref[...] = prod

        @pl.when(l > 0)
        def _():
            o_ref[...] += prod

    if halves > 1:
        b_spec = pl.BlockSpec((1, tl, tq), lambda p, q, l: (q // qper, l, q % qper))
    else:
        b_spec = pl.BlockSpec((tl, tq), lambda p, q, l: (l, q))
    return pl.pallas_call(
        body, name=name, grid=grid,
        in_specs=[pl.BlockSpec((tl, tp), lambda p, q, l: (l, p)), b_spec],
        out_specs=pl.BlockSpec((tp, tq), lambda p, q, l: (p, q)),
        out_shape=jax.ShapeDtypeStruct((P, Q), F32),
        compiler_params=_params(("parallel", "parallel", "arbitrary")),
    )(a, b)


def _rms_bwd(h, g, dhn, dres, name):
    L, D = h.shape
    tl = _row_tile(L)
    nt = L // tl

    def body(h_ref, g_ref, dhn_ref, dres_ref, dh_ref, dg_ref):
        i = pl.program_id(0)
        x = h_ref[...]
        r = lax.rsqrt(jnp.mean(x * x, axis=-1, keepdims=True) + EPS)
        xhat = x * r
        dhn = dhn_ref[...]
        dxhat = dhn * g_ref[...]
        dh_ref[...] = dres_ref[...] + r * (dxhat - xhat * jnp.mean(dxhat * xhat, axis=-1, keepdims=True))
        part = jnp.sum(_rowsum8(dhn * xhat), axis=0, keepdims=True)

        @pl.when(i == 0)
        def _():
            dg_ref[...] = part

        @pl.when(i > 0)
        def _():
            dg_ref[...] += part

    tile = pl.BlockSpec((tl, D), lambda i: (i, 0))
    row = pl.BlockSpec((1, D), lambda i: (0, 0))
    return pl.pallas_call(
        body, name=name, grid=(nt,), in_specs=[tile, row, tile, tile], out_specs=(tile, row),
        out_shape=(jax.ShapeDtypeStruct((L, D), F32), jax.ShapeDtypeStruct((1, D), F32)),
        compiler_params=_params(("arbitrary",)),
    )(h, g, dhn, dres)


def _loss_head(h, g, tgt, n_meta, name):
    L, D = h.shape
    tl = _row_tile(L)
    nt = L // tl

    def body(h_ref, g_ref, t_ref, dh_ref, dg_ref, loss_ref):
        i = pl.program_id(0)
        x = h_ref[...]
        r = lax.rsqrt(jnp.mean(x * x, axis=-1, keepdims=True) + EPS)
        xhat = x * r
        gg = g_ref[...]
        y = xhat * gg
        rows = i * tl + lax.broadcasted_iota(jnp.int32, (tl, 1), 0)
        err = jnp.where(rows >= n_meta, y - t_ref[...], 0.0)
        dy = err * (1.0 / D)
        dxhat = dy * gg
        dh_ref[...] = r * (dxhat - xhat * jnp.mean(dxhat * xhat, axis=-1, keepdims=True))
        dg_part = jnp.sum(_rowsum8(dy * xhat), axis=0, keepdims=True)
        per_row = jnp.mean(err * err, axis=-1, keepdims=True)
        loss_part = jnp.broadcast_to(0.5 * jnp.sum(per_row, axis=0, keepdims=True), (1, 128))

        @pl.when(i == 0)
        def _():
            dg_ref[...] = dg_part
            loss_ref[...] = loss_part

        @pl.when(i > 0)
        def _():
            dg_ref[...] += dg_part
            loss_ref[...] += loss_part

    tile = pl.BlockSpec((tl, D), lambda i: (i, 0))
    row = pl.BlockSpec((1, D), lambda i: (0, 0))
    return pl.pallas_call(
        body, name=name, grid=(nt,), in_specs=[tile, row, tile],
        out_specs=(tile, row, pl.BlockSpec((1, 128), lambda i: (0, 0))),
        out_shape=(jax.ShapeDtypeStruct((L, D), F32), jax.ShapeDtypeStruct((1, D), F32),
                   jax.ShapeDtypeStruct((1, 128), F32)),
        compiler_params=_params(("arbitrary",)),
    )(h, g, tgt)


def _pool_fwd_block(pwin, pw_ref, row0, rb, g, gd, w, t0):
    wv = pwin[pl.ds(row0 + HALO - POOL_PAD, rb + POOL_PAD), g * gd:(g + 1) * gd]
    s = wv
    sh = 1
    while sh < w:
        s = s + pltpu.roll(s, sh, axis=0)
        sh *= 2
    win = s[POOL_PAD:POOL_PAD + rb]
    pt = wv[POOL_PAD:POOL_PAD + rb]
    tg = t0 + lax.broadcasted_iota(jnp.int32, (rb, 1), 0)
    cnt = jnp.minimum(tg + 1, w).astype(F32)
    return win / cnt - pt


def _fill_windows(i, zp_ref, zc_ref, u0w, pwin, tl, cc):
    keep = i > 0
    zp = zp_ref[...]
    u0w[0:HALO, :] = jnp.where(keep, zp[:, :cc] * _sigmoid(zp[:, cc:2 * cc]), 0.0)
    pwin[0:HALO, :] = jnp.where(keep, zp[:, 2 * cc:], 0.0)

    def fill(c, carry):
        b = pl.multiple_of(c * ROW_CHUNK, SUBLANES)
        zc = zc_ref[pl.ds(b, ROW_CHUNK), :]
        u0w[pl.ds(HALO + b, ROW_CHUNK), :] = zc[:, :cc] * _sigmoid(zc[:, cc:2 * cc])
        pwin[pl.ds(HALO + b, ROW_CHUNK), :] = zc[:, 2 * cc:]
        return carry

    lax.fori_loop(0, tl // ROW_CHUNK, fill, 0)


def _mixer_fwd(z, ck, cb, lg, lb, pw, ps, am, name, xchg=None):
    L, ci = z.shape
    kw, _, cc = ck.shape
    cp = ci - 2 * cc
    ng, gd = pw.shape[0], pw.shape[1]
    tl = _token_tile(L)
    nt = L // tl
    hb = tl // HALO
    rb = _stat_rows(tl)
    tap0 = CONV_PAD - (kw - 1)

    def body(zp_ref, zc_ref, ck_ref, cb_ref, lg_ref, lb_ref, pw_ref, ps_ref, am_ref, y_ref, u1_ref, u0w, pwin):
        i = pl.program_id(0)
        _fill_windows(i, zp_ref, zc_ref, u0w, pwin, tl, cc)

        def conv(c, carry):
            b = pl.multiple_of(c * ROW_CHUNK, SUBLANES)
            w = u0w[pl.ds(b + HALO - CONV_PAD, ROW_CHUNK + CONV_PAD), :]
            acc = jnp.broadcast_to(cb_ref[...], (ROW_CHUNK, cc))
            for j in range(kw):
                acc = acc + _rows_of(ck_ref[j], ROW_CHUNK) * w[tap0 + j:tap0 + j + ROW_CHUNK]
            u1_ref[pl.ds(b, ROW_CHUNK), :] = acc
            return carry

        lax.fori_loop(0, tl // ROW_CHUNK, conv, 0)

        def blocks(k, carry):
            b = pl.multiple_of(k * rb, SUBLANES)
            u1 = u1_ref[pl.ds(b, rb), :]
            xc = u1 - _head_mean(u1, am_ref)
            var = _head_mean(xc * xc, am_ref)
            u2 = (xc * lax.rsqrt(var + EPS)) * lg_ref[...] + lb_ref[...]
            y_ref[pl.ds(b, rb), 0:cc] = (u2 * _sigmoid(u2)).astype(y_ref.dtype)
            for g in range(ng):
                d = _pool_fwd_block(pwin, pw_ref, b, rb, g, gd, POOL_WINDOWS[g], i * tl + b)
                yp = jnp.dot(d.astype(BF16), pw_ref[g].astype(BF16), preferred_element_type=F32)
                yp = yp * ps_ref[:, g * gd:(g + 1) * gd]
                y_ref[pl.ds(b, rb), cc + g * gd:cc + (g + 1) * gd] = yp.astype(y_ref.dtype)
            return carry

        lax.fori_loop(0, tl // rb, blocks, 0)

    def full(a):
        nd = a.ndim
        return pl.BlockSpec(a.shape, lambda i: (0,) * nd)

    out, xo = _call(
        body, name=name, grid=(nt,),
        in_specs=[pl.BlockSpec((HALO, ci), lambda i: (jnp.maximum(i * hb - 1, 0), 0)),
                  pl.BlockSpec((tl, ci), lambda i: (i, 0)),
                  full(ck), full(cb), full(lg), full(lb), full(pw), full(ps), full(am)],
        out_specs=(pl.BlockSpec((tl, cc + cp), lambda i: (i, 0)), pl.BlockSpec((tl, cc), lambda i: (i, 0))),
        out_shape=(jax.ShapeDtypeStruct((L, cc + cp), BF16), jax.ShapeDtypeStruct((L, cc), F32)),
        scratch_shapes=[pltpu.VMEM((HALO + tl, cc), F32), pltpu.VMEM((HALO + tl, cp), F32)],
        sem=("parallel",), args=(z, z, ck, cb, lg, lb, pw, ps, am), xchg=xchg)
    return out if xchg is None else (out, xo)


def _mixer_bwd(z, u1, dy, ck, lg, lb, pw, ps, am, name, xchg=None):
    L, ci = z.shape
    kw, _, cc = ck.shape
    cp = ci - 2 * cc
    ng, gd = pw.shape[0], pw.shape[1]
    tl = _token_tile(L)
    nt = L // tl
    hb = tl // HALO
    rb = _stat_rows(tl)

    def body(zp_ref, zc_ref, u1c_ref, u1n_ref, dyc_ref, dyn_ref, ck_ref, lg_ref, lb_ref, pw_ref, ps_ref, am_ref,
             dz_ref, dck_ref, dcb_ref, dlg_ref, dlb_ref, dpw_ref, dps_ref,
             u0w, pwin, du1w, ddw, ew, dkacc, dcb8, dlg8, dlb8, dps8):
        i = pl.program_id(0)
        has_next = i < nt - 1

        @pl.when(i == 0)
        def _():
            dck_ref[...] = jnp.zeros_like(dck_ref)
            dcb_ref[...] = jnp.zeros_like(dcb_ref)
            dlg_ref[...] = jnp.zeros_like(dlg_ref)
            dlb_ref[...] = jnp.zeros_like(dlb_ref)
            dpw_ref[...] = jnp.zeros_like(dpw_ref)
            dps_ref[...] = jnp.zeros_like(dps_ref)

        dkacc[...] = jnp.zeros_like(dkacc)
        dcb8[...] = jnp.zeros_like(dcb8)
        dlg8[...] = jnp.zeros_like(dlg8)
        dlb8[...] = jnp.zeros_like(dlb8)
        dps8[...] = jnp.zeros_like(dps8)

        _fill_windows(i, zp_ref, zc_ref, u0w, pwin, tl, cc)

        def conv_side(u1, dyc, own):
            xc = u1 - _head_mean(u1, am_ref)
            rstd = lax.rsqrt(_head_mean(xc * xc, am_ref) + EPS)
            uh = xc * rstd
            lgv = lg_ref[...]
            u2 = uh * lgv + lb_ref[...]
            sg = _sigmoid(u2)
            du2 = dyc * (sg * (1.0 + u2 * (1.0 - sg)))
            if own:
                dlg8[...] += _rowsum8(du2 * uh)
                dlb8[...] += _rowsum8(du2)
            duh = du2 * lgv
            return rstd * (duh - _head_mean(duh, am_ref) - uh * _head_mean(duh * uh, am_ref))

        def pool_side(dyp, t0, rows):
            dds, es = [], []
            tg = t0 + lax.broadcasted_iota(jnp.int32, (rows, 1), 0)
            for g in range(ng):
                dypre = dyp[:, g * gd:(g + 1) * gd] * ps_ref[:, g * gd:(g + 1) * gd]
                dd = lax.dot_general(dypre.astype(BF16), pw_ref[g].astype(BF16), (((1,), (1,)), ((), ())),
                                     preferred_element_type=F32)
                cnt = jnp.minimum(tg + 1, POOL_WINDOWS[g]).astype(F32)
                dds.append(dd)
                es.append(dd / cnt)
            return jnp.concatenate(dds, axis=-1), jnp.concatenate(es, axis=-1)

        def blocks(k, carry):
            b = pl.multiple_of(k * rb, SUBLANES)
            dyb = dyc_ref[pl.ds(b, rb), :]
            du1 = conv_side(u1c_ref[pl.ds(b, rb), :], dyb[:, :cc], True)
            du1w[pl.ds(b, rb), :] = du1
            dcb8[...] += _rowsum8(du1)
            dyp = dyb[:, cc:]
            dd, e = pool_side(dyp, i * tl + b, rb)
            ddw[pl.ds(b, rb), :] = dd
            ew[pl.ds(b, rb), :] = e
            for g in range(ng):
                d = _pool_fwd_block(pwin, pw_ref, b, rb, g, gd, POOL_WINDOWS[g], i * tl + b)
                db16 = d.astype(BF16)
                dypg = dyp[:, g * gd:(g + 1) * gd]
                ypre = jnp.dot(db16, pw_ref[g].astype(BF16), preferred_element_type=F32)
                dps8[:, g * gd:(g + 1) * gd] += _rowsum8(dypg * ypre)
                dypre = (dypg * ps_ref[:, g * gd:(g + 1) * gd]).astype(BF16)
                dpw_ref[g] += lax.dot_general(db16, dypre, (((0,), (0,)), ((), ())), preferred_element_type=F32)
            return carry

        lax.fori_loop(0, tl // rb, blocks, 0)

        dyn = dyn_ref[...]
        du1n = conv_side(u1n_ref[...], dyn[:, :cc], False)
        du1w[tl:tl + HALO, :] = jnp.where(has_next, du1n, 0.0)
        ddn, en = pool_side(dyn[:, cc:], (i + 1) * tl, HALO)
        ew[tl:tl + HALO, :] = jnp.where(has_next, en, 0.0)

        def taps(c, carry):
            b = pl.multiple_of(c * ROW_CHUNK, SUBLANES)
            w = du1w[pl.ds(b, ROW_CHUNK + CONV_PAD), :]
            u0c = u0w[pl.ds(HALO + b, ROW_CHUNK), :]
            acc = jnp.zeros((ROW_CHUNK, cc), F32)
            for j in range(kw):
                o = kw - 1 - j
                sh = w[o:o + ROW_CHUNK]
                acc = acc + _rows_of(ck_ref[j], ROW_CHUNK) * sh
                dkacc[j] += _rowsum8(u0c * sh)
            zc = zc_ref[pl.ds(b, ROW_CHUNK), :]
            a = zc[:, :cc]
            sg = _sigmoid(zc[:, cc:2 * cc])
            dz_ref[pl.ds(b, ROW_CHUNK), 0:cc] = (acc * sg).astype(dz_ref.dtype)
            dz_ref[pl.ds(b, ROW_CHUNK), cc:2 * cc] = (acc * a * sg * (1.0 - sg)).astype(dz_ref.dtype)
            return carry

        lax.fori_loop(0, tl // ROW_CHUNK, taps, 0)

        def pool_back(k, carry):
            b = pl.multiple_of(k * rb, SUBLANES)
            n = rb + POOL_PAD
            for g in range(ng):
                s = ew[pl.ds(b, n), g * gd:(g + 1) * gd]
                sh = 1
                while sh < POOL_WINDOWS[g]:
                    s = s + pltpu.roll(s, n - sh, axis=0)
                    sh *= 2
                dp = s[0:rb] - ddw[pl.ds(b, rb), g * gd:(g + 1) * gd]
                dz_ref[pl.ds(b, rb), 2 * cc + g * gd:2 * cc + (g + 1) * gd] = dp.astype(dz_ref.dtype)
            return carry

        lax.fori_loop(0, tl // rb, pool_back, 0)

        dck_ref[...] += jnp.sum(dkacc[...], axis=1)
        dcb_ref[...] += jnp.sum(dcb8[...], axis=0, keepdims=True)
        dlg_ref[...] += jnp.sum(dlg8[...], axis=0, keepdims=True)
        dlb_ref[...] += jnp.sum(dlb8[...], axis=0, keepdims=True)
        dps_ref[...] += jnp.sum(dps8[...], axis=0, keepdims=True)

    def full(a):
        nd = a.ndim
        return pl.BlockSpec(a.shape, lambda i: (0,) * nd)

    nhb = L // HALO

    def prev_map(i):
        return (jnp.maximum(i * hb - 1, 0), 0)

    def next_map(i):
        return (jnp.minimum((i + 1) * hb, nhb - 1), 0)

    dcc = cc + cp
    row_cc = jax.ShapeDtypeStruct((1, cc), F32)
    out_shape = (jax.ShapeDtypeStruct((L, ci), BF16), jax.ShapeDtypeStruct((kw, cc), F32), row_cc, row_cc, row_cc,
                 jax.ShapeDtypeStruct((ng, gd, gd), F32), jax.ShapeDtypeStruct((1, cp), F32))
    acc_spec = [pl.BlockSpec((kw, cc), lambda i: (0, 0))] + [pl.BlockSpec((1, cc), lambda i: (0, 0))] * 3 + [
        pl.BlockSpec((ng, gd, gd), lambda i: (0, 0, 0)), pl.BlockSpec((1, cp), lambda i: (0, 0))]
    out, xo = _call(
        body, name=name, grid=(nt,),
        in_specs=[pl.BlockSpec((HALO, ci), prev_map), pl.BlockSpec((tl, ci), lambda i: (i, 0)),
                  pl.BlockSpec((tl, cc), lambda i: (i, 0)), pl.BlockSpec((HALO, cc), next_map),
                  pl.BlockSpec((tl, dcc), lambda i: (i, 0)), pl.BlockSpec((HALO, dcc), next_map),
                  full(ck), full(lg), full(lb), full(pw), full(ps), full(am)],
        out_specs=tuple([pl.BlockSpec((tl, ci), lambda i: (i, 0))] + acc_spec),
        out_shape=out_shape,
        scratch_shapes=[pltpu.VMEM((HALO + tl, cc), F32), pltpu.VMEM((HALO + tl, cp), F32),
                        pltpu.VMEM((tl + HALO, cc), F32), pltpu.VMEM((tl, cp), F32), pltpu.VMEM((tl + HALO, cp), F32),
                        pltpu.VMEM((kw, SUBLANES, cc), F32), pltpu.VMEM((SUBLANES, cc), F32),
                        pltpu.VMEM((SUBLANES, cc), F32), pltpu.VMEM((SUBLANES, cc), F32), pltpu.VMEM((SUBLANES, cp), F32)],
        sem=("arbitrary",), args=(z, z, u1, u1, dy, dy, ck, lg, lb, pw, ps, am), xchg=xchg)
    return out if xchg is None else (out, xo)


def _tap_rows(k_ref):
    return [jnp.broadcast_to(k_ref[j:j + 1, :], (SUBLANES, k_ref.shape[1])) for j in range(k_ref.shape[0])]


def _rows_of(tap, n):
    return tap if n == SUBLANES else jnp.concatenate([tap] * (n // SUBLANES), axis=0)


def _ffn_conv(win, taps, rows):
    kw = len(taps)
    o = FFN_PAD - (kw - 1)
    acc = _rows_of(taps[0], rows) * win[o:o + rows]
    for j in range(1, kw):
        acc = acc + _rows_of(taps[j], rows) * win[o + j:o + j + rows]
    return acc


def _ffn_block_fwd(h_mid, g, wup, kf, wdown, name, xchg=None):
    L, D = h_mid.shape
    f = wdown.shape[0]
    kw = kf.shape[0]
    tl = _token_tile(L)
    tc = _divisor(f, 256, 128)
    nj = f // tc
    nt = L // tl
    pad = 2 * SUBLANES
    hb = tl // pad

    def body(hp_ref, hc_ref, g_ref, wg_ref, wv_ref, kg_ref, kv_ref, wd_ref, out_ref, hn_ref, ug_ref, act_ref, hn_halo, halo, acc):
        i = pl.program_id(0)
        kb = pl.program_id(1)

        @pl.when(kb == 0)
        def _():
            gg = g_ref[...]

            def norm(x):
                r = lax.rsqrt(jnp.mean(x * x, axis=-1, keepdims=True) + EPS)
                return ((x * r) * gg).astype(BF16)

            hn_halo[...] = jnp.where(i > 0, norm(hp_ref[...]), jnp.zeros((pad, D), BF16))
            hn_ref[...] = norm(hc_ref[...])

        hn = hn_ref[...]
        hh = hn_halo[...]
        for h, (w_ref, k_ref) in enumerate(((wg_ref, kg_ref), (wv_ref, kv_ref))):
            ug_ref[h] = jnp.dot(hn, w_ref[...], preferred_element_type=F32)
            halo[h] = jnp.dot(hh, w_ref[...], preferred_element_type=F32)[pad - FFN_PAD:]

        taps = (_tap_rows(kg_ref), _tap_rows(kv_ref))
        for c in range(tl // CONV3_ROWS):
            r0 = c * CONV3_ROWS
            convd = []
            for h in range(2):
                if c == 0:
                    win = jnp.concatenate([halo[h], ug_ref[h, 0:CONV3_ROWS]], axis=0)
                else:
                    win = ug_ref[h, r0 - FFN_PAD:r0 + CONV3_ROWS]
                convd.append(_ffn_conv(win, taps[h], CONV3_ROWS))
            gate, val = convd
            act_ref[r0:r0 + CONV3_ROWS, :] = ((gate * _sigmoid(gate)) * val).astype(BF16)
        prod = jnp.dot(act_ref[...], wd_ref[...], preferred_element_type=F32)

        @pl.when(kb == 0)
        def _():
            acc[...] = prod

        @pl.when(kb > 0)
        def _():
            acc[...] += prod

        @pl.when(kb == nj - 1)
        def _():
            out_ref[...] = acc[...] + hc_ref[...]

    out, xo = _call(
        body, name=name, grid=(nt, nj),
        in_specs=[pl.BlockSpec((pad, D), lambda i, k: (jnp.maximum(i * hb - 1, 0), 0)),
                  pl.BlockSpec((tl, D), lambda i, k: (i, 0)),
                  pl.BlockSpec((1, D), lambda i, k: (0, 0)),
                  pl.BlockSpec((D, tc), lambda i, k: (0, k)), pl.BlockSpec((D, tc), lambda i, k: (0, k + nj)),
                  pl.BlockSpec((kw, tc), lambda i, k: (0, k)), pl.BlockSpec((kw, tc), lambda i, k: (0, k + nj)),
                  pl.BlockSpec((tc, D), lambda i, k: (k, 0))],
        out_specs=(pl.BlockSpec((tl, D), lambda i, k: (i, 0)), pl.BlockSpec((tl, D), lambda i, k: (i, 0)),
                   pl.BlockSpec((2, tl, tc), lambda i, k: (0, i, k)), pl.BlockSpec((tl, tc), lambda i, k: (i, k))),
        out_shape=(jax.ShapeDtypeStruct((L, D), F32), jax.ShapeDtypeStruct((L, D), BF16),
                   jax.ShapeDtypeStruct((2, L, f), F32), jax.ShapeDtypeStruct((L, f), BF16)),
        scratch_shapes=[pltpu.VMEM((pad, D), BF16), pltpu.VMEM((2, FFN_PAD, tc), F32), pltpu.VMEM((tl, D), F32)],
        sem=("parallel", "arbitrary"), args=(h_mid, h_mid, g, wup, wup, kf, kf, wdown), xchg=xchg)
    return out if xchg is None else (out, xo)


def _ffn_block_bwd(dh, h_mid, g, ug0, kf, wdown_t, wup_t, name, xchg=None):
    L, D = dh.shape
    f = ug0.shape[2]
    kw = kf.shape[0]
    tl = _token_tile(L)
    tc = _divisor(f, 256, 128)
    nj = f // tc
    nt = L // tl
    pad = 2 * SUBLANES
    hb, nhb = tl // FFN_PAD, L // FFN_PAD
    rc = CONV3_ROWS
    nc = tl // rc

    def body(dhc_ref, dhn_ref, hm_ref, g_ref, gp_ref, gc_ref, gn_ref, vp_ref, vc_ref, vn_ref, kg_ref, kv_ref,
             wd_ref, wg_ref, wv_ref, dhm_ref, dg_ref, du_ref, dk_ref, dh_ext, dact_s, acc):
        i = pl.program_id(0)
        kb = pl.program_id(1)

        @pl.when(kb == 0)
        def _():
            dh_ext[0:tl, :] = dhc_ref[...].astype(BF16)
            dh_ext[tl:tl + pad, :] = dhn_ref[...].astype(BF16)

        @pl.when(jnp.logical_and(i == 0, kb == 0))
        def _():
            dg_ref[...] = jnp.zeros_like(dg_ref)
            dk_ref[...] = jnp.zeros_like(dk_ref)

        dact_s[...] = jnp.dot(dh_ext[...], wd_ref[...], preferred_element_type=F32)
        prev = (jnp.where(i > 0, gp_ref[...], 0.0), jnp.where(i > 0, vp_ref[...], 0.0))
        x_refs, nxt = (gc_ref, vc_ref), (gn_ref, vn_ref)
        taps = (_tap_rows(kg_ref), _tap_rows(kv_ref))
        dk = [[jnp.zeros((SUBLANES, tc), F32) for _ in range(kw)] for _ in range(2)]

        for c in range(nc):
            r0 = c * rc
            n = rc + FFN_PAD
            xs = []
            for h in range(2):
                parts = [prev[h] if c == 0 else x_refs[h][r0 - FFN_PAD:r0]]
                if c == nc - 1:
                    parts += [x_refs[h][r0:r0 + rc], nxt[h][...]]
                else:
                    parts += [x_refs[h][r0:r0 + n]]
                xs.append(jnp.concatenate(parts, axis=0))
            gate = _ffn_conv(xs[0], taps[0], n)
            val = _ffn_conv(xs[1], taps[1], n)
            dact = dact_s[r0:r0 + n, :]
            sg = _sigmoid(gate)
            dcs = [dact * val * (sg * (1.0 + gate * (1.0 - sg))), dact * (gate * sg)]
            if c == nc - 1:
                live = jnp.logical_or(lax.broadcasted_iota(jnp.int32, (n, 1), 0) < rc, i < nt - 1)
                dcs = [jnp.where(live, d, 0.0) for d in dcs]
            for h in range(2):
                xc = xs[h][FFN_PAD:FFN_PAD + rc]
                dx = None
                for j in range(kw):
                    o = kw - 1 - j
                    sh = dcs[h][o:o + rc]
                    term = _rows_of(taps[h][j], rc) * sh
                    dx = term if dx is None else dx + term
                    dk[h][j] = dk[h][j] + _rowsum8(xc * sh)
                du_ref[h, r0:r0 + rc, :] = dx.astype(BF16)
        for h in range(2):
            for j in range(kw):
                dk_ref[kb, h, j:j + 1, :] += jnp.sum(dk[h][j], axis=0, keepdims=True)
        prod = (jnp.dot(du_ref[0], wg_ref[...], preferred_element_type=F32)
                + jnp.dot(du_ref[1], wv_ref[...], preferred_element_type=F32))

        @pl.when(kb == 0)
        def _():
            acc[...] = prod

        @pl.when(kb > 0)
        def _():
            acc[...] += prod

        @pl.when(kb == nj - 1)
        def _():
            x = hm_ref[...]
            r = lax.rsqrt(jnp.mean(x * x, axis=-1, keepdims=True) + EPS)
            xhat = x * r
            dhn = acc[...]
            dxhat = dhn * g_ref[...]
            dhm_ref[...] = dhc_ref[...] + r * (dxhat - xhat * jnp.mean(dxhat * xhat, axis=-1, keepdims=True))
            dg_ref[...] += jnp.sum(_rowsum8(dhn * xhat), axis=0, keepdims=True)

    def prev8(i, k):
        return (jnp.maximum(i * hb - 1, 0), k)

    def next8(i, k):
        return (jnp.minimum((i + 1) * hb, nhb - 1), k)

    def half(h, rows, idx):
        return pl.BlockSpec((None, rows, tc), lambda i, k: (h,) + idx(i, k))

    def tile(i, k):
        return (i, k)

    out, xo = _call(
        body, name=name, grid=(nt, nj),
        in_specs=[pl.BlockSpec((tl, D), lambda i, k: (i, 0)),
                  pl.BlockSpec((pad, D), lambda i, k: (jnp.minimum((i + 1) * (tl // pad), L // pad - 1), 0)),
                  pl.BlockSpec((tl, D), lambda i, k: (i, 0)), pl.BlockSpec((1, D), lambda i, k: (0, 0)),
                  half(0, FFN_PAD, prev8), half(0, tl, tile), half(0, FFN_PAD, next8),
                  half(1, FFN_PAD, prev8), half(1, tl, tile), half(1, FFN_PAD, next8),
                  pl.BlockSpec((kw, tc), lambda i, k: (0, k)), pl.BlockSpec((kw, tc), lambda i, k: (0, k + nj)),
                  pl.BlockSpec((D, tc), lambda i, k: (0, k)),
                  pl.BlockSpec((tc, D), lambda i, k: (k, 0)), pl.BlockSpec((tc, D), lambda i, k: (k + nj, 0))],
        out_specs=(pl.BlockSpec((tl, D), lambda i, k: (i, 0)), pl.BlockSpec((1, D), lambda i, k: (0, 0)),
                   pl.BlockSpec((2, tl, tc), lambda i, k: (0, i, k)),
                   pl.BlockSpec((nj, 2, kw, tc), lambda i, k: (0, 0, 0, 0))),
        out_shape=(jax.ShapeDtypeStruct((L, D), F32), jax.ShapeDtypeStruct((1, D), F32),
                   jax.ShapeDtypeStruct((2, L, f), BF16), jax.ShapeDtypeStruct((nj, 2, kw, tc), F32)),
        scratch_shapes=[pltpu.VMEM((tl + pad, D), BF16), pltpu.VMEM((tl + pad, tc), F32), pltpu.VMEM((tl, D), F32)],
        sem=("arbitrary", "arbitrary"), args=(dh, dh, h_mid, g, ug0, ug0, ug0, ug0, ug0, ug0, kf, kf, wdown_t, wup_t, wup_t),
        xchg=xchg)
    return out if xchg is None else (out, xo)


def _adamw_math(w, g, m, v):
    m = ADAM_B1 * m + (1.0 - ADAM_B1) * g
    v = ADAM_B2 * v + (1.0 - ADAM_B2) * (g * g)
    m_hat = m / (1.0 - ADAM_B1 ** ADAM_STEP)
    v_hat = v / (1.0 - ADAM_B2 ** ADAM_STEP)
    delta = -ADAM_LR * (m_hat / (jnp.sqrt(v_hat) + ADAM_EPS) + ADAM_WD * w)
    return delta, m, v


def _sum_parts(parts_ref, idx):
    g = parts_ref[(0,) + idx].astype(F32)
    for q in range(1, N_DEV):
        g = g + parts_ref[(q,) + idx].astype(F32)
    return g


def _adamw_big(parts, w, m, v, name):
    nl, R, C = w.shape
    tr = _divisor(R, 256, 2 * SUBLANES)

    def body(*refs):
        p_refs = refs[:nl]
        w_ref, m_ref, v_ref, g_ref, d_ref, nm_ref, nv_ref = refs[nl:]
        layer = pl.program_id(0)
        for k in range(nl):
            @pl.when(layer == k)
            def _(k=k):
                g = _sum_parts(p_refs[k], ())
                d, nm, nv = _adamw_math(w_ref[0], g, m_ref[0], v_ref[0])
                g_ref[0] = g
                d_ref[0] = d
                nm_ref[0] = nm
                nv_ref[0] = nv

    def part_spec(k):
        return pl.BlockSpec((N_DEV, tr, C), lambda l, r: (0, jnp.where(l == k, r, 0), 0))

    blk = pl.BlockSpec((1, tr, C), lambda l, r: (l, r, 0))
    shp = jax.ShapeDtypeStruct((nl, R, C), F32)
    return pl.pallas_call(
        body, name=name, grid=(nl, R // tr),
        in_specs=[part_spec(k) for k in range(nl)] + [blk, blk, blk],
        out_specs=(blk, blk, blk, blk), out_shape=(shp, shp, shp, shp),
        compiler_params=_params(("arbitrary", "arbitrary")),
    )(*parts, w, m, v)


def _adamw_small(entries, name):
    n = len(entries)
    uniq = []
    for e in entries:
        if not any(e[0] is u for u in uniq):
            uniq.append(e[0])
    pidx = [next(k for k, u in enumerate(uniq) if u is e[0]) for e in entries]
    npart = len(uniq)

    def body(*refs):
        p_refs = refs[:npart]
        wmv = refs[npart:npart + 3 * n]
        outs = refs[npart + 3 * n:]
        for t, e in enumerate(entries):
            lo, w = e[1], e[2]
            rows = w.shape[0]
            pr = p_refs[pidx[t]]
            g = pr[0, lo:lo + rows].astype(F32)
            for q in range(1, N_DEV):
                g = g + pr[q, lo:lo + rows].astype(F32)
            d, nm, nv = _adamw_math(wmv[3 * t][...], g, wmv[3 * t + 1][...], wmv[3 * t + 2][...])
            outs[4 * t][...] = g
            outs[4 * t + 1][...] = d
            outs[4 * t + 2][...] = nm
            outs[4 * t + 3][...] = nv

    vm = pl.BlockSpec(memory_space=pltpu.VMEM)
    args = list(uniq)
    out_shape = []
    for e in entries:
        args += [e[2], e[3], e[4]]
        out_shape += [jax.ShapeDtypeStruct(e[2].shape, F32)] * 4
    res = pl.pallas_call(
        body, name=name, in_specs=[vm] * len(args), out_specs=tuple([vm] * len(out_shape)),
        out_shape=tuple(out_shape), compiler_params=_params(),
    )(*args)
    return [tuple(res[4 * t:4 * t + 4]) for t in range(n)]


def _head_matrix(cc):
    bw = min(256, cc)
    r = lax.broadcasted_iota(jnp.int32, (bw, bw), 0) // HEAD_DIM
    c = lax.broadcasted_iota(jnp.int32, (bw, bw), 1) // HEAD_DIM
    return jnp.where(r == c, 1.0 / HEAD_DIM, 0.0).astype(BF16)


def _cols_from_shards(g):
    nd = g.ndim
    perm = tuple(range(1, nd - 1)) + (0, nd - 1)
    t = jnp.transpose(g, perm)
    return t.reshape(t.shape[:-2] + (t.shape[-2] * t.shape[-1],))


def _cols_to_shards(a):
    nd = a.ndim
    t = a.reshape(a.shape[:-1] + (N_DEV, a.shape[-1] // N_DEV))
    perm = (nd - 1,) + tuple(range(nd - 1)) + (nd,)
    return jnp.transpose(t, perm)


def kernel(x, meta_tokens, norm1_g, w_in, conv_dw_k, conv_dw_b, conv_ln_g, conv_ln_b, pool_w, pool_scale, w_out, norm2_g, w_up, ffn_dw_k, w_down, final_g, loss_target, m_meta_tokens, m_norm1_g, m_w_in, m_conv_dw_k, m_conv_dw_b, m_conv_ln_g, m_conv_ln_b, m_pool_w, m_pool_scale, m_w_out, m_norm2_g, m_w_up, m_ffn_dw_k, m_w_down, m_final_g, v_meta_tokens, v_norm1_g, v_w_in, v_conv_dw_k, v_conv_dw_b, v_conv_ln_g, v_conv_ln_b, v_pool_w, v_pool_scale, v_w_out, v_norm2_g, v_w_up, v_ffn_dw_k, v_w_down, v_final_g):
    depth, D = norm1_g.shape
    n_meta = meta_tokens.shape[0]
    seq = x.shape[1]
    L = n_meta + seq
    cc = conv_dw_b.shape[1]
    ng, gd = pool_w.shape[1], pool_w.shape[2]
    f = w_down.shape[1] * N_DEV

    def cols(g):
        w = _cols_from_shards(g)
        return w, w.T

    def rows(g):
        w = g.reshape(-1, g.shape[-1])
        return w, w.T

    b16 = lambda a: a.astype(BF16)
    (g_in0, g_out0, g_ck, g_kf, g_meta) = _exchange([b16(w_in[0]), b16(w_out[0]), conv_dw_k, ffn_dw_k, meta_tokens],
                                                    ["gather"] * 5, "gather_first")
    ck_full = _cols_from_shards(g_ck)
    ck_rows = jnp.broadcast_to(ck_full[:, :, None, :], ck_full.shape[:2] + (SUBLANES, cc))
    kf_full = _cols_from_shards(g_kf)
    meta_full = _cols_from_shards(g_meta)
    am = _head_matrix(cc)
    win, wout, wup, wdown = [None] * depth, [None] * depth, [None] * depth, [None] * depth
    win[0] = cols(g_in0)
    wout[0] = rows(g_out0)

    h = jnp.concatenate([meta_full, x[0]], axis=0)
    saved = []
    for l in range(depth):
        more = l + 1 < depth
        hn1 = _rms_fwd(h, norm1_g[l:l + 1], f"rms1_fwd_{l}")
        if l == 0:
            z, (g_down,) = _mm(hn1, win[l][0], f"in_proj_{l}", tn_cap=768, xchg=([b16(w_down[l])], ["gather"]))
            wdown[l] = rows(g_down)
            (ymix, u1), (g_up,) = _mixer_fwd(z, ck_rows[l], conv_dw_b[l:l + 1], conv_ln_g[l:l + 1], conv_ln_b[l:l + 1],
                                             pool_w[l], pool_scale[l:l + 1], am, f"mixer_fwd_{l}",
                                             xchg=([b16(w_up[l])], ["gather"]))
            wup[l] = cols(g_up)
        else:
            z = _mm(hn1, win[l][0], f"in_proj_{l}", tn_cap=768)
            ymix, u1 = _mixer_fwd(z, ck_rows[l], conv_dw_b[l:l + 1], conv_ln_g[l:l + 1], conv_ln_b[l:l + 1], pool_w[l],
                                  pool_scale[l:l + 1], am, f"mixer_fwd_{l}")
        if more:
            h_mid, (g_in,) = _mm(ymix, wout[l][0], f"out_proj_{l}", res=h, tn_cap=512, xchg=([b16(w_in[l + 1])], ["gather"]))
            win[l + 1] = cols(g_in)
            nxt = [b16(w_out[l + 1]), b16(w_up[l + 1]), b16(w_down[l + 1])]
            (h_out, hn2, ug0, act), got = _ffn_block_fwd(h_mid, norm2_g[l:l + 1], wup[l][0], kf_full[l], wdown[l][0],
                                                         f"ffn_fwd_{l}", xchg=(nxt, ["gather"] * 3))
            wout[l + 1], wup[l + 1], wdown[l + 1] = rows(got[0]), cols(got[1]), rows(got[2])
        else:
            h_mid = _mm(ymix, wout[l][0], f"out_proj_{l}", res=h, tn_cap=512)
            h_out, hn2, ug0, act = _ffn_block_fwd(h_mid, norm2_g[l:l + 1], wup[l][0], kf_full[l], wdown[l][0], f"ffn_fwd_{l}")
        saved.append((h, hn1, z, u1, ymix, h_mid, hn2, ug0, act))
        h = h_out

    tgt = jnp.concatenate([jnp.zeros((n_meta, D), F32), loss_target[0]], axis=0)
    dh, d_final_g, loss_part = _loss_head(h, final_g.reshape(1, D), tgt, n_meta, "loss_head")

    def row_shards(gm):
        return b16(gm.reshape(N_DEV, -1, gm.shape[-1]))

    def col_shards(gm):
        return b16(_cols_to_shards(gm))

    gw = {k: [None] * depth for k in ("ck", "cb", "lg", "lb", "pw", "ps", "kf", "n1", "n2")}
    parts = {k: [None] * depth for k in ("in", "out", "up", "down")}
    for l in reversed(range(depth)):
        h_in, hn1, z, u1, ymix, h_mid, hn2, ug0, act = saved[l]
        g_down = _mm_tn(act, dh, f"down_proj_wgrad_{l}", tq_cap=512)
        (dh_mid, gw["n2"][l], dug0, dkf), (parts["down"][l],) = _ffn_block_bwd(
            dh, h_mid, norm2_g[l:l + 1], ug0, kf_full[l], wdown[l][1], wup[l][1], f"ffn_bwd_{l}",
            xchg=([row_shards(g_down)], ["a2a"]))
        gw["kf"][l] = jnp.transpose(dkf, (2, 1, 0, 3)).reshape(dkf.shape[2], -1)
        g_up = _mm_tn(hn2, dug0, f"up_proj_wgrad_{l}", halves=2)
        dymix = _mm(dh_mid, wout[l][1], f"out_proj_bwd_{l}", tn_cap=512)
        g_out = _mm_tn(ymix, dh_mid, f"out_proj_wgrad_{l}", tq_cap=512)
        ((dz, gw["ck"][l], gw["cb"][l], gw["lg"][l], gw["lb"][l], gw["pw"][l], gw["ps"][l]),
         (parts["up"][l], parts["out"][l])) = _mixer_bwd(
            z, u1, dymix, ck_rows[l], conv_ln_g[l:l + 1], conv_ln_b[l:l + 1], pool_w[l], pool_scale[l:l + 1], am,
            f"mixer_bwd_{l}", xchg=([col_shards(g_up), row_shards(g_out)], ["a2a", "a2a"]))
        g_in = _mm_tn(hn1, dz, f"in_proj_wgrad_{l}", tq_cap=768)
        dhn1, (parts["in"][l],) = _mm(dz, win[l][1], f"in_proj_bwd_{l}", tn_cap=512, xchg=([col_shards(g_in)], ["a2a"]))
        dh, gw["n1"][l] = _rms_bwd(h_in, norm1_g[l:l + 1], dhn1, dh_mid, f"rms1_bwd_{l}")
    grad_x = dh[n_meta:][None]
    d_meta = dh[:n_meta]

    zero_row = jnp.zeros((1, D), F32)
    pack_d = jnp.concatenate(gw["n1"] + gw["n2"] + [d_final_g, jnp.broadcast_to(loss_part[:, :1], (1, D)), zero_row, zero_row], axis=0)
    pack_c = jnp.concatenate(gw["cb"] + gw["lg"] + gw["lb"] + gw["ps"], axis=0)
    pack_pw = jnp.stack(gw["pw"]).reshape(depth * ng * gd, gd)
    src = [_cols_to_shards(jnp.stack(gw["ck"])), _cols_to_shards(jnp.stack(gw["kf"])), _cols_to_shards(d_meta),
           pack_d, pack_c, pack_pw]
    r_ck, r_kf, r_meta, r_d, r_c, r_pw = _exchange(src, ["a2a"] * 3 + ["gather"] * 3, "exchange_small_grads")

    big = {
        "w_in": _adamw_big(parts["in"], w_in, m_w_in, v_w_in, "adamw_w_in"),
        "w_out": _adamw_big(parts["out"], w_out, m_w_out, v_w_out, "adamw_w_out"),
        "w_up": _adamw_big(parts["up"], w_up, m_w_up, v_w_up, "adamw_w_up"),
        "w_down": _adamw_big(parts["down"], w_down, m_w_down, v_w_down, "adamw_w_down"),
    }
    kwid = conv_dw_k.shape[1]
    fkw = ffn_dw_k.shape[1]
    row = lambda a: a.reshape(1, -1)
    entries = [
        (r_d, 0, norm1_g, m_norm1_g, v_norm1_g),
        (r_d, depth, norm2_g, m_norm2_g, v_norm2_g),
        (r_d, 2 * depth, row(final_g), row(m_final_g), row(v_final_g)),
        (r_c, 0, conv_dw_b, m_conv_dw_b, v_conv_dw_b),
        (r_c, depth, conv_ln_g, m_conv_ln_g, v_conv_ln_g),
        (r_c, 2 * depth, conv_ln_b, m_conv_ln_b, v_conv_ln_b),
        (r_c, 3 * depth, pool_scale, m_pool_scale, v_pool_scale),
        (r_pw, 0, pool_w.reshape(-1, gd), m_pool_w.reshape(-1, gd), v_pool_w.reshape(-1, gd)),
        (r_ck.reshape(N_DEV, depth * kwid, -1), 0, conv_dw_k.reshape(depth * kwid, -1),
         m_conv_dw_k.reshape(depth * kwid, -1), v_conv_dw_k.reshape(depth * kwid, -1)),
        (r_kf.reshape(N_DEV, depth * fkw, -1), 0, ffn_dw_k.reshape(depth * fkw, -1),
         m_ffn_dw_k.reshape(depth * fkw, -1), v_ffn_dw_k.reshape(depth * fkw, -1)),
        (r_meta, 0, meta_tokens, m_meta_tokens, v_meta_tokens),
        (r_d, 2 * depth + 1, zero_row, zero_row, zero_row),
    ]
    small = _adamw_small(entries, "adamw_small")
    names = ["norm1_g", "norm2_g", "final_g", "conv_dw_b", "conv_ln_g", "conv_ln_b", "pool_scale", "pool_w",
             "conv_dw_k", "ffn_dw_k", "meta_tokens"]
    shapes = {"final_g": final_g.shape, "pool_w": pool_w.shape, "conv_dw_k": conv_dw_k.shape, "ffn_dw_k": ffn_dw_k.shape}
    res = dict(big)
    for nme, quad in zip(names, small[:-1]):
        res[nme] = tuple(a.reshape(shapes[nme]) if nme in shapes else a for a in quad)
    loss = small[-1][0][0, 0]

    order = ["meta_tokens", "norm1_g", "w_in", "conv_dw_k", "conv_dw_b", "conv_ln_g", "conv_ln_b", "pool_w", "pool_scale",
             "w_out", "norm2_g", "w_up", "ffn_dw_k", "w_down", "final_g"]
    return (loss, grad_x, *[res[k][0] for k in order], *[res[k][1] for k in order], *[res[k][2] for k in order],
            *[res[k][3] for k in order])
```

```python
import functools

import jax
import jax.numpy as jnp
from jax import lax
from jax.experimental import pallas as pl
from jax.experimental.pallas import tpu as pltpu

F32 = jnp.float32
BF16 = jnp.bfloat16

EPS = 1e-6
HEAD_DIM = 64
POOL_WINDOWS = (2, 4, 8, 16)
ADAM_LR = 0.001
ADAM_B1 = 0.9
ADAM_B2 = 0.999
ADAM_EPS = 1e-08
ADAM_WD = 0.01
ADAM_STEP = 10

N_DEV = 8
SUBLANES = 8
HALO = 48
CONV_PAD = 32
POOL_PAD = 16
FFN_PAD = 8
ROW_CHUNK = 24
CONV3_ROWS = 48
MAX_TILE_ROWS = 1024
VMEM_LIMIT = 52 * 1024 * 1024


def _divisor(n, cap, mult):
    best = None
    for d in range(mult, min(n, cap) + 1, mult):
        if n % d == 0:
            best = d
    return n if best is None else best


def _token_tile(L):
    return _divisor(L, MAX_TILE_ROWS, HALO)


def _row_tile(L):
    return _divisor(L, 320, 2 * SUBLANES)


def _stat_rows(tl):
    return _divisor(tl, 256, SUBLANES)


def _params(sem=None):
    return pltpu.CompilerParams(dimension_semantics=sem, vmem_limit_bytes=VMEM_LIMIT)


def _rowsum8(x):
    acc = x[0:SUBLANES]
    for k in range(1, x.shape[0] // SUBLANES):
        acc = acc + x[k * SUBLANES:(k + 1) * SUBLANES]
    return acc


def _sigmoid(x):
    return jax.nn.sigmoid(x)


def _head_mean(x, am_ref):
    bw = am_ref.shape[0]
    am = am_ref[...]
    outs = []
    for blk in range(x.shape[1] // bw):
        xb = x[:, blk * bw:(blk + 1) * bw]
        hi = xb.astype(BF16)
        lo = (xb - hi.astype(F32)).astype(BF16)
        outs.append(jnp.dot(hi, am, preferred_element_type=F32) + jnp.dot(lo, am, preferred_element_type=F32))
    return outs[0] if len(outs) == 1 else jnp.concatenate(outs, axis=-1)


def _xchg_out_shapes(srcs, modes):
    out = []
    for s, m in zip(srcs, modes):
        shp = ((N_DEV,) + tuple(s.shape)) if m == "gather" else tuple(s.shape)
        out.append(jax.ShapeDtypeStruct(shp, s.dtype))
    return out


def _xchg_sems(n):
    return [pltpu.SemaphoreType.DMA((n, N_DEV - 1)), pltpu.SemaphoreType.DMA((n, N_DEV - 1)), pltpu.SemaphoreType.DMA((n,))]


def _xchg_ops(src_refs, out_refs, sems, modes):
    n = len(src_refs)
    send_sems, recv_sems, local_sems = sems
    x, y, c = lax.axis_index("x"), lax.axis_index("y"), lax.axis_index("c")
    me = 4 * x + 2 * y + c

    def peer(d):
        return (x ^ ((d >> 2) & 1), y ^ ((d >> 1) & 1), c ^ (d & 1))

    def peer_id(d):
        px, py, pc = peer(d)
        return 4 * px + 2 * py + pc

    def remote(t, d):
        src = src_refs[t] if modes[t] == "gather" else src_refs[t].at[peer_id(d)]
        return pltpu.make_async_remote_copy(
            src_ref=src, dst_ref=out_refs[t].at[me], send_sem=send_sems.at[t, d - 1], recv_sem=recv_sems.at[t, d - 1],
            device_id=peer(d), device_id_type=pl.DeviceIdType.MESH)

    def arrival(t, d):
        src = src_refs[t] if modes[t] == "gather" else src_refs[t].at[me]
        return pltpu.make_async_remote_copy(
            src_ref=src, dst_ref=out_refs[t].at[peer_id(d)], send_sem=send_sems.at[t, d - 1],
            recv_sem=recv_sems.at[t, d - 1], device_id=peer(d), device_id_type=pl.DeviceIdType.MESH)

    def local(t):
        src = src_refs[t] if modes[t] == "gather" else src_refs[t].at[me]
        return pltpu.make_async_copy(src, out_refs[t].at[me], local_sems.at[t])

    def start():
        for t in range(n):
            local(t).start()
        for t in range(n):
            for d in range(1, N_DEV):
                remote(t, d).start()

    def wait():
        for t in range(n):
            for d in range(1, N_DEV):
                arrival(t, d).wait_recv()
        for t in range(n):
            for d in range(1, N_DEV):
                remote(t, d).wait_send()
        for t in range(n):
            local(t).wait()

    return start, wait


def _exchange(srcs, modes, name):
    n = len(srcs)

    def body(*refs):
        start, wait = _xchg_ops(refs[:n], refs[n:2 * n], refs[2 * n:], modes)
        start()
        wait()

    any_spec = pl.BlockSpec(memory_space=pl.ANY)
    return pl.pallas_call(
        body, name=name, out_shape=tuple(_xchg_out_shapes(srcs, modes)),
        in_specs=[any_spec] * n, out_specs=tuple([any_spec] * n),
        scratch_shapes=_xchg_sems(n),
        compiler_params=pltpu.CompilerParams(has_side_effects=True),
    )(*srcs)


def _call(body, *, name, grid, in_specs, out_specs, out_shape, args, scratch_shapes=(), sem=None, xchg=None):
    single = not isinstance(out_shape, (tuple, list))
    outs_shape = [out_shape] if single else list(out_shape)
    outs_spec = [out_specs] if single else list(out_specs)
    if xchg is None:
        res = pl.pallas_call(
            body, name=name, grid=grid, in_specs=list(in_specs), out_specs=out_specs, out_shape=out_shape,
            scratch_shapes=list(scratch_shapes), compiler_params=_params(sem))(*args)
        return res, ()
    srcs, modes = xchg
    n_in, n_out, n_scr, nx = len(in_specs), len(outs_shape), len(scratch_shapes), len(srcs)

    def wrapped(*refs):
        ins = refs[:n_in]
        xs = refs[n_in:n_in + nx]
        o0 = n_in + nx
        outs = refs[o0:o0 + n_out]
        xo = refs[o0 + n_out:o0 + n_out + nx]
        s0 = o0 + n_out + nx
        scr = refs[s0:s0 + n_scr]
        start, wait = _xchg_ops(xs, xo, refs[s0 + n_scr:], modes)
        first = functools.reduce(jnp.logical_and, [pl.program_id(a) == 0 for a in range(len(grid))])
        last = functools.reduce(jnp.logical_and, [pl.program_id(a) == grid[a] - 1 for a in range(len(grid))])

        @pl.when(first)
        def _():
            start()

        body(*ins, *outs, *scr)

        @pl.when(last)
        def _():
            wait()

    any_spec = pl.BlockSpec(memory_space=pl.ANY)
    res = pl.pallas_call(
        wrapped, name=name, grid=grid, in_specs=list(in_specs) + [any_spec] * nx,
        out_specs=tuple(outs_spec + [any_spec] * nx), out_shape=tuple(outs_shape + _xchg_out_shapes(srcs, modes)),
        scratch_shapes=list(scratch_shapes) + _xchg_sems(nx),
        compiler_params=_params(("arbitrary",) * len(grid)))(*args, *srcs)
    comp = res[:n_out]
    return (comp[0] if single else tuple(comp)), tuple(res[n_out:])


def _rms_fwd(h, g, name):
    L, D = h.shape
    tl = _row_tile(L)

    def body(h_ref, g_ref, o_ref):
        x = h_ref[...]
        r = lax.rsqrt(jnp.mean(x * x, axis=-1, keepdims=True) + EPS)
        o_ref[...] = ((x * r) * g_ref[...]).astype(o_ref.dtype)

    return pl.pallas_call(
        body, name=name, grid=(L // tl,),
        in_specs=[pl.BlockSpec((tl, D), lambda i: (i, 0)), pl.BlockSpec((1, D), lambda i: (0, 0))],
        out_specs=pl.BlockSpec((tl, D), lambda i: (i, 0)),
        out_shape=jax.ShapeDtypeStruct((L, D), BF16),
        compiler_params=_params(("parallel",)),
    )(h, g)


def _mm(a, b, name, *, res=None, out_dtype=F32, tn_cap=1408, halves=1, xchg=None):
    if halves > 1:
        _, M, kh = a.shape
        K = kh * halves
    else:
        M, K = a.shape
        kh = K
    N = b.shape[1]
    tm = _token_tile(M)
    tn = _divisor(N, tn_cap, 128)
    tk = kh if kh <= 2816 else _divisor(kh, 2816, 128)
    kper = kh // tk
    nk = halves * kper
    grid = (M // tm, N // tn, nk)

    def body(*refs):
        if res is None:
            a_ref, b_ref, o_ref = refs[:3]
            r_ref = None
            scratch = refs[3:]
        else:
            a_ref, b_ref, r_ref, o_ref = refs[:4]
            scratch = refs[4:]
        av = a_ref[0] if halves > 1 else a_ref[...]
        prod = jnp.dot(av.astype(BF16), b_ref[...], preferred_element_type=F32)
        if nk == 1:
            if r_ref is not None:
                prod = prod + r_ref[...]
            o_ref[...] = prod.astype(o_ref.dtype)
        else:
            acc = scratch[0]
            k = pl.program_id(2)

            @pl.when(k == 0)
            def _():
                acc[...] = prod

            @pl.when(k > 0)
            def _():
                acc[...] += prod

            @pl.when(k == nk - 1)
            def _():
                tot = acc[...]
                if r_ref is not None:
                    tot = tot + r_ref[...]
                o_ref[...] = tot.astype(o_ref.dtype)

    if halves > 1:
        a_spec = pl.BlockSpec((1, tm, tk), lambda i, j, k: (k // kper, i, k % kper))
    else:
        a_spec = pl.BlockSpec((tm, tk), lambda i, j, k: (i, k))
    in_specs = [a_spec, pl.BlockSpec((tk, tn), lambda i, j, k: (k, j))]
    args = [a, b]
    if res is not None:
        in_specs.append(pl.BlockSpec((tm, tn), lambda i, j, k: (i, j)))
        args.append(res)
    out, xo = _call(
        body, name=name, grid=grid, in_specs=in_specs,
        out_specs=pl.BlockSpec((tm, tn), lambda i, j, k: (i, j)),
        out_shape=jax.ShapeDtypeStruct((M, N), out_dtype),
        scratch_shapes=[pltpu.VMEM((tm, tn), F32)] if nk > 1 else [],
        sem=("parallel", "parallel", "arbitrary"), args=args, xchg=xchg)
    return out if xchg is None else (out, xo)


def _mm_tn(a, b, name, *, halves=1, tq_cap=1408):
    L, P = a.shape
    if halves > 1:
        qh = b.shape[2]
        Q = qh * halves
    else:
        Q = b.shape[1]
        qh = Q
    tl = _token_tile(L)
    tp = _divisor(P, 1408, 128)
    tq = _divisor(qh, tq_cap, 128)
    qper = qh // tq
    grid = (P // tp, Q // tq, L // tl)

    def body(a_ref, b_ref, o_ref):
        bv = b_ref[0] if halves > 1 else b_ref[...]
        prod = lax.dot_general(a_ref[...].astype(BF16), bv.astype(BF16), (((0,), (0,)), ((), ())),
                               preferred_element_type=F32)
        l = pl.program_id(2)

        @pl.when(l == 0)
        def _():
            o_ref[...] = prod

        @pl.when(l > 0)
        def _():
            o_ref[...] += prod

    if halves > 1:
        b_spec = pl.BlockSpec((1, tl, tq), lambda p, q, l: (q // qper, l, q % qper))
    else:
        b_spec = pl.BlockSpec((tl, tq), lambda p, q, l: (l, q))
    return pl.pallas_call(
        body, name=name, grid=grid,
        in_specs=[pl.BlockSpec((tl, tp), lambda p, q, l: (l, p)), b_spec],
        out_specs=pl.BlockSpec((tp, tq), lambda p, q, l: (p, q)),
        out_shape=jax.ShapeDtypeStruct((P, Q), F32),
        compiler_params=_params(("parallel", "parallel", "arbitrary")),
    )(a, b)


def _rms_bwd(h, g, dhn, dres, name):
    L, D = h.shape
    tl = _row_tile(L)
    nt = L // tl

    def body(h_ref, g_ref, dhn_ref, dres_ref, dh_ref, dg_ref):
        i = pl.program_id(0)
        x = h_ref[...]
        r = lax.rsqrt(jnp.mean(x * x, axis=-1, keepdims=True) + EPS)
        xhat = x * r
        dhn = dhn_ref[...]
        dxhat = dhn * g_ref[...]
        dh_ref[...] = dres_ref[...] + r * (dxhat - xhat * jnp.mean(dxhat * xhat, axis=-1, keepdims=True))
        part = jnp.sum(_rowsum8(dhn * xhat), axis=0, keepdims=True)

        @pl.when(i == 0)
        def _():
            dg_ref[...] = part

        @pl.when(i > 0)
        def _():
            dg_ref[...] += part

    tile = pl.BlockSpec((tl, D), lambda i: (i, 0))
    row = pl.BlockSpec((1, D), lambda i: (0, 0))
    return pl.pallas_call(
        body, name=name, grid=(nt,), in_specs=[tile, row, tile, tile], out_specs=(tile, row),
        out_shape=(jax.ShapeDtypeStruct((L, D), F32), jax.ShapeDtypeStruct((1, D), F32)),
        compiler_params=_params(("arbitrary",)),
    )(h, g, dhn, dres)


def _loss_head(h, g, tgt, n_meta, name):
    L, D = h.shape
    tl = _row_tile(L)
    nt = L // tl

    def body(h_ref, g_ref, t_ref, dh_ref, dg_ref, loss_ref):
        i = pl.program_id(0)
        x = h_ref[...]
        r = lax.rsqrt(jnp.mean(x * x, axis=-1, keepdims=True) + EPS)
        xhat = x * r
        gg = g_ref[...]
        y = xhat * gg
        rows = i * tl + lax.broadcasted_iota(jnp.int32, (tl, 1), 0)
        err = jnp.where(rows >= n_meta, y - t_ref[...], 0.0)
        dy = err * (1.0 / D)
        dxhat = dy * gg
        dh_ref[...] = r * (dxhat - xhat * jnp.mean(dxhat * xhat, axis=-1, keepdims=True))
        dg_part = jnp.sum(_rowsum8(dy * xhat), axis=0, keepdims=True)
        per_row = jnp.mean(err * err, axis=-1, keepdims=True)
        loss_part = jnp.broadcast_to(0.5 * jnp.sum(per_row, axis=0, keepdims=True), (1, 128))

        @pl.when(i == 0)
        def _():
            dg_ref[...] = dg_part
            loss_ref[...] = loss_part

        @pl.when(i > 0)
        def _():
            dg_ref[...] += dg_part
            loss_ref[...] += loss_part

    tile = pl.BlockSpec((tl, D), lambda i: (i, 0))
    row = pl.BlockSpec((1, D), lambda i: (0, 0))
    return pl.pallas_call(
        body, name=name, grid=(nt,), in_specs=[tile, row, tile],
        out_specs=(tile, row, pl.BlockSpec((1, 128), lambda i: (0, 0))),
        out_shape=(jax.ShapeDtypeStruct((L, D), F32), jax.ShapeDtypeStruct((1, D), F32),
                   jax.ShapeDtypeStruct((1, 128), F32)),
        compiler_params=_params(("arbitrary",)),
    )(h, g, tgt)


def _pool_fwd_block(pwin, pw_ref, row0, rb, g, gd, w, t0):
    wv = pwin[pl.ds(row0 + HALO - POOL_PAD, rb + POOL_PAD), g * gd:(g + 1) * gd]
    s = wv
    sh = 1
    while sh < w:
        s = s + pltpu.roll(s, sh, axis=0)
        sh *= 2
    win = s[POOL_PAD:POOL_PAD + rb]
    pt = wv[POOL_PAD:POOL_PAD + rb]
    tg = t0 + lax.broadcasted_iota(jnp.int32, (rb, 1), 0)
    cnt = jnp.minimum(tg + 1, w).astype(F32)
    return win / cnt - pt


def _fill_windows(i, zp_ref, zc_ref, u0w, pwin, tl, cc):
    keep = i > 0
    zp = zp_ref[...]
    u0w[0:HALO, :] = jnp.where(keep, zp[:, :cc] * _sigmoid(zp[:, cc:2 * cc]), 0.0)
    pwin[0:HALO, :] = jnp.where(keep, zp[:, 2 * cc:], 0.0)

    def fill(c, carry):
        b = pl.multiple_of(c * ROW_CHUNK, SUBLANES)
        zc = zc_ref[pl.ds(b, ROW_CHUNK), :]
        u0w[pl.ds(HALO + b, ROW_CHUNK), :] = zc[:, :cc] * _sigmoid(zc[:, cc:2 * cc])
        pwin[pl.ds(HALO + b, ROW_CHUNK), :] = zc[:, 2 * cc:]
        return carry

    lax.fori_loop(0, tl // ROW_CHUNK, fill, 0)


def _mixer_fwd(z, ck, cb, lg, lb, pw, ps, am, name, xchg=None):
    L, ci = z.shape
    kw, _, cc = ck.shape
    cp = ci - 2 * cc
    ng, gd = pw.shape[0], pw.shape[1]
    tl = _token_tile(L)
    nt = L // tl
    hb = tl // HALO
    rb = _stat_rows(tl)
    tap0 = CONV_PAD - (kw - 1)

    def body(zp_ref, zc_ref, ck_ref, cb_ref, lg_ref, lb_ref, pw_ref, ps_ref, am_ref, y_ref, u1_ref, u0w, pwin):
        i = pl.program_id(0)
        _fill_windows(i, zp_ref, zc_ref, u0w, pwin, tl, cc)

        def conv(c, carry):
            b = pl.multiple_of(c * ROW_CHUNK, SUBLANES)
            w = u0w[pl.ds(b + HALO - CONV_PAD, ROW_CHUNK + CONV_PAD), :]
            acc = jnp.broadcast_to(cb_ref[...], (ROW_CHUNK, cc))
            for j in range(kw):
                acc = acc + _rows_of(ck_ref[j], ROW_CHUNK) * w[tap0 + j:tap0 + j + ROW_CHUNK]
            u1_ref[pl.ds(b, ROW_CHUNK), :] = acc
            return carry

        lax.fori_loop(0, tl // ROW_CHUNK, conv, 0)

        def blocks(k, carry):
            b = pl.multiple_of(k * rb, SUBLANES)
            u1 = u1_ref[pl.ds(b, rb), :]
            xc = u1 - _head_mean(u1, am_ref)
            var = _head_mean(xc * xc, am_ref)
            u2 = (xc * lax.rsqrt(var + EPS)) * lg_ref[...] + lb_ref[...]
            y_ref[pl.ds(b, rb), 0:cc] = (u2 * _sigmoid(u2)).astype(y_ref.dtype)
            for g in range(ng):
                d = _pool_fwd_block(pwin, pw_ref, b, rb, g, gd, POOL_WINDOWS[g], i * tl + b)
                yp = jnp.dot(d.astype(BF16), pw_ref[g].astype(BF16), preferred_element_type=F32)
                yp = yp * ps_ref[:, g * gd:(g + 1) * gd]
                y_ref[pl.ds(b, rb), cc + g * gd:cc + (g + 1) * gd] = yp.astype(y_ref.dtype)
            return carry

        lax.fori_loop(0, tl // rb, blocks, 0)

    def full(a):
        nd = a.ndim
        return pl.BlockSpec(a.shape, lambda i: (0,) * nd)

    out, xo = _call(
        body, name=name, grid=(nt,),
        in_specs=[pl.BlockSpec((HALO, ci), lambda i: (jnp.maximum(i * hb - 1, 0), 0)),
                  pl.BlockSpec((tl, ci), lambda i: (i, 0)),
                  full(ck), full(cb), full(lg), full(lb), full(pw), full(ps), full(am)],
        out_specs=(pl.BlockSpec((tl, cc + cp), lambda i: (i, 0)), pl.BlockSpec((tl, cc), lambda i: (i, 0))),
        out_shape=(jax.ShapeDtypeStruct((L, cc + cp), BF16), jax.ShapeDtypeStruct((L, cc), F32)),
        scratch_shapes=[pltpu.VMEM((HALO + tl, cc), F32), pltpu.VMEM((HALO + tl, cp), F32)],
        sem=("parallel",), args=(z, z, ck, cb, lg, lb, pw, ps, am), xchg=xchg)
    return out if xchg is None else (out, xo)


def _mixer_bwd(z, u1, dy, ck, lg, lb, pw, ps, am, name, xchg=None):
    L, ci = z.shape
    kw, _, cc = ck.shape
    cp = ci - 2 * cc
    ng, gd = pw.shape[0], pw.shape[1]
    tl = _token_tile(L)
    nt = L // tl
    hb = tl // HALO
    rb = _stat_rows(tl)

    def body(zp_ref, zc_ref, u1c_ref, u1n_ref, dyc_ref, dyn_ref, ck_ref, lg_ref, lb_ref, pw_ref, ps_ref, am_ref,
             dz_ref, dck_ref, dcb_ref, dlg_ref, dlb_ref, dpw_ref, dps_ref,
             u0w, pwin, du1w, ddw, ew, dkacc, dcb8, dlg8, dlb8, dps8):
        i = pl.program_id(0)
        has_next = i < nt - 1

        @pl.when(i == 0)
        def _():
            dck_ref[...] = jnp.zeros_like(dck_ref)
            dcb_ref[...] = jnp.zeros_like(dcb_ref)
            dlg_ref[...] = jnp.zeros_like(dlg_ref)
            dlb_ref[...] = jnp.zeros_like(dlb_ref)
            dpw_ref[...] = jnp.zeros_like(dpw_ref)
            dps_ref[...] = jnp.zeros_like(dps_ref)

        dkacc[...] = jnp.zeros_like(dkacc)
        dcb8[...] = jnp.zeros_like(dcb8)
        dlg8[...] = jnp.zeros_like(dlg8)
        dlb8[...] = jnp.zeros_like(dlb8)
        dps8[...] = jnp.zeros_like(dps8)

        _fill_windows(i, zp_ref, zc_ref, u0w, pwin, tl, cc)

        def conv_side(u1, dyc, own):
            xc = u1 - _head_mean(u1, am_ref)
            rstd = lax.rsqrt(_head_mean(xc * xc, am_ref) + EPS)
            uh = xc * rstd
            lgv = lg_ref[...]
            u2 = uh * lgv + lb_ref[...]
            sg = _sigmoid(u2)
            du2 = dyc * (sg * (1.0 + u2 * (1.0 - sg)))
            if own:
                dlg8[...] += _rowsum8(du2 * uh)
                dlb8[...] += _rowsum8(du2)
            duh = du2 * lgv
            return rstd * (duh - _head_mean(duh, am_ref) - uh * _head_mean(duh * uh, am_ref))

        def pool_side(dyp, t0, rows):
            dds, es = [], []
            tg = t0 + lax.broadcasted_iota(jnp.int32, (rows, 1), 0)
            for g in range(ng):
                dypre = dyp[:, g * gd:(g + 1) * gd] * ps_ref[:, g * gd:(g + 1) * gd]
                dd = lax.dot_general(dypre.astype(BF16), pw_ref[g].astype(BF16), (((1,), (1,)), ((), ())),
                                     preferred_element_type=F32)
                cnt = jnp.minimum(tg + 1, POOL_WINDOWS[g]).astype(F32)
                dds.append(dd)
                es.append(dd / cnt)
            return jnp.concatenate(dds, axis=-1), jnp.concatenate(es, axis=-1)

        def blocks(k, carry):
            b = pl.multiple_of(k * rb, SUBLANES)
            dyb = dyc_ref[pl.ds(b, rb), :]
            du1 = conv_side(u1c_ref[pl.ds(b, rb), :], dyb[:, :cc], True)
            du1w[pl.ds(b, rb), :] = du1
            dcb8[...] += _rowsum8(du1)
            dyp = dyb[:, cc:]
            dd, e = pool_side(dyp, i * tl + b, rb)
            ddw[pl.ds(b, rb), :] = dd
            ew[pl.ds(b, rb), :] = e
            for g in range(ng):
                d = _pool_fwd_block(pwin, pw_ref, b, rb, g, gd, POOL_WINDOWS[g], i * tl + b)
                db16 = d.astype(BF16)
                dypg = dyp[:, g * gd:(g + 1) * gd]
                ypre = jnp.dot(db16, pw_ref[g].astype(BF16), preferred_element_type=F32)
                dps8[:, g * gd:(g + 1) * gd] += _rowsum8(dypg * ypre)
                dypre = (dypg * ps_ref[:, g * gd:(g + 1) * gd]).astype(BF16)
                dpw_ref[g] += lax.dot_general(db16, dypre, (((0,), (0,)), ((), ())), preferred_element_type=F32)
            return carry

        lax.fori_loop(0, tl // rb, blocks, 0)

        dyn = dyn_ref[...]
        du1n = conv_side(u1n_ref[...], dyn[:, :cc], False)
        du1w[tl:tl + HALO, :] = jnp.where(has_next, du1n, 0.0)
        ddn, en = pool_side(dyn[:, cc:], (i + 1) * tl, HALO)
        ew[tl:tl + HALO, :] = jnp.where(has_next, en, 0.0)

        def taps(c, carry):
            b = pl.multiple_of(c * ROW_CHUNK, SUBLANES)
            w = du1w[pl.ds(b, ROW_CHUNK + CONV_PAD), :]
            u0c = u0w[pl.ds(HALO + b, ROW_CHUNK), :]
            acc = jnp.zeros((ROW_CHUNK, cc), F32)
            for j in range(kw):
                o = kw - 1 - j
                sh = w[o:o + ROW_CHUNK]
                acc = acc + _rows_of(ck_ref[j], ROW_CHUNK) * sh
                dkacc[j] += _rowsum8(u0c * sh)
            zc = zc_ref[pl.ds(b, ROW_CHUNK), :]
            a = zc[:, :cc]
            sg = _sigmoid(zc[:, cc:2 * cc])
            dz_ref[pl.ds(b, ROW_CHUNK), 0:cc] = (acc * sg).astype(dz_ref.dtype)
            dz_ref[pl.ds(b, ROW_CHUNK), cc:2 * cc] = (acc * a * sg * (1.0 - sg)).astype(dz_ref.dtype)
            return carry

        lax.fori_loop(0, tl // ROW_CHUNK, taps, 0)

        def pool_back(k, carry):
            b = pl.multiple_of(k * rb, SUBLANES)
            n = rb + POOL_PAD
            for g in range(ng):
                s = ew[pl.ds(b, n), g * gd:(g + 1) * gd]
                sh = 1
                while sh < POOL_WINDOWS[g]:
                    s = s + pltpu.roll(s, n - sh, axis=0)
                    sh *= 2
                dp = s[0:rb] - ddw[pl.ds(b, rb), g * gd:(g + 1) * gd]
                dz_ref[pl.ds(b, rb), 2 * cc + g * gd:2 * cc + (g + 1) * gd] = dp.astype(dz_ref.dtype)
            return carry

        lax.fori_loop(0, tl // rb, pool_back, 0)

        dck_ref[...] += jnp.sum(dkacc[...], axis=1)
        dcb_ref[...] += jnp.sum(dcb8[...], axis=0, keepdims=True)
        dlg_ref[...] += jnp.sum(dlg8[...], axis=0, keepdims=True)
        dlb_ref[...] += jnp.sum(dlb8[...], axis=0, keepdims=True)
        dps_ref[...] += jnp.sum(dps8[...], axis=0, keepdims=True)

    def full(a):
        nd = a.ndim
        return pl.BlockSpec(a.shape, lambda i: (0,) * nd)

    nhb = L // HALO

    def prev_map(i):
        return (jnp.maximum(i * hb - 1, 0), 0)

    def next_map(i):
        return (jnp.minimum((i + 1) * hb, nhb - 1), 0)

    dcc = cc + cp
    row_cc = jax.ShapeDtypeStruct((1, cc), F32)
    out_shape = (jax.ShapeDtypeStruct((L, ci), BF16), jax.ShapeDtypeStruct((kw, cc), F32), row_cc, row_cc, row_cc,
                 jax.ShapeDtypeStruct((ng, gd, gd), F32), jax.ShapeDtypeStruct((1, cp), F32))
    acc_spec = [pl.BlockSpec((kw, cc), lambda i: (0, 0))] + [pl.BlockSpec((1, cc), lambda i: (0, 0))] * 3 + [
        pl.BlockSpec((ng, gd, gd), lambda i: (0, 0, 0)), pl.BlockSpec((1, cp), lambda i: (0, 0))]
    out, xo = _call(
        body, name=name, grid=(nt,),
        in_specs=[pl.BlockSpec((HALO, ci), prev_map), pl.BlockSpec((tl, ci), lambda i: (i, 0)),
                  pl.BlockSpec((tl, cc), lambda i: (i, 0)), pl.BlockSpec((HALO, cc), next_map),
                  pl.BlockSpec((tl, dcc), lambda i: (i, 0)), pl.BlockSpec((HALO, dcc), next_map),
                  full(ck), full(lg), full(lb), full(pw), full(ps), full(am)],
        out_specs=tuple([pl.BlockSpec((tl, ci), lambda i: (i, 0))] + acc_spec),
        out_shape=out_shape,
        scratch_shapes=[pltpu.VMEM((HALO + tl, cc), F32), pltpu.VMEM((HALO + tl, cp), F32),
                        pltpu.VMEM((tl + HALO, cc), F32), pltpu.VMEM((tl, cp), F32), pltpu.VMEM((tl + HALO, cp), F32),
                        pltpu.VMEM((kw, SUBLANES, cc), F32), pltpu.VMEM((SUBLANES, cc), F32),
                        pltpu.VMEM((SUBLANES, cc), F32), pltpu.VMEM((SUBLANES, cc), F32), pltpu.VMEM((SUBLANES, cp), F32)],
        sem=("arbitrary",), args=(z, z, u1, u1, dy, dy, ck, lg, lb, pw, ps, am), xchg=xchg)
    return out if xchg is None else (out, xo)


def _row_parts(nc, n=3):
    n = min(n, nc)
    cuts = [round(k * nc / n) for k in range(n + 1)]
    return [(cuts[k], cuts[k + 1]) for k in range(n)]


def _tap_rows(k_ref):
    return [jnp.broadcast_to(k_ref[j:j + 1, :], (SUBLANES, k_ref.shape[1])) for j in range(k_ref.shape[0])]


def _rows_of(tap, n):
    return tap if n == SUBLANES else jnp.concatenate([tap] * (n // SUBLANES), axis=0)


def _ffn_conv(win, taps, rows):
    kw = len(taps)
    o = FFN_PAD - (kw - 1)
    acc = _rows_of(taps[0], rows) * win[o:o + rows]
    for j in range(1, kw):
        acc = acc + _rows_of(taps[j], rows) * win[o + j:o + j + rows]
    return acc


def _ffn_block_fwd(h_mid, g, wup, kf, wdown, name, xchg=None):
    L, D = h_mid.shape
    f = wdown.shape[0]
    kw = kf.shape[0]
    tl = _token_tile(L)
    tc = _divisor(f, 256, 128)
    nj = f // tc
    nt = L // tl
    pad = 2 * SUBLANES
    hb = tl // pad
    rc = CONV3_ROWS
    parts = _row_parts(tl // rc)

    def body(hp_ref, hc_ref, g_ref, wg_ref, wv_ref, kg_ref, kv_ref, wd_ref, out_ref, hn_ref, ug_ref, act_ref, hn_halo, halo, acc):
        i = pl.program_id(0)
        kb = pl.program_id(1)

        @pl.when(kb == 0)
        def _():
            gg = g_ref[...]

            def norm(x):
                r = lax.rsqrt(jnp.mean(x * x, axis=-1, keepdims=True) + EPS)
                return ((x * r) * gg).astype(BF16)

            hn_halo[...] = jnp.where(i > 0, norm(hp_ref[...]), jnp.zeros((pad, D), BF16))
            hn_ref[...] = norm(hc_ref[...])
            acc[...] = jnp.zeros_like(acc)

        w_refs = (wg_ref, wv_ref)
        taps = (_tap_rows(kg_ref), _tap_rows(kv_ref))
        hh = hn_halo[...]
        for h in range(2):
            halo[h] = jnp.dot(hh, w_refs[h][...], preferred_element_type=F32)[pad - FFN_PAD:]

        def up_part(lo, hi):
            a, b = lo * rc, hi * rc
            for h in range(2):
                ug_ref[h, a:b, :] = jnp.dot(hn_ref[a:b, :], w_refs[h][...], preferred_element_type=F32)

        def down_part(lo, hi):
            a, b = lo * rc, hi * rc
            acc[a:b, :] += jnp.dot(act_ref[a:b, :], wd_ref[...], preferred_element_type=F32)

        def chunk_rows(lo, hi):
            for c in range(lo, hi):
                r0 = c * rc
                convd = []
                for h in range(2):
                    if c == 0:
                        win = jnp.concatenate([halo[h], ug_ref[h, 0:rc]], axis=0)
                    else:
                        win = ug_ref[h, r0 - FFN_PAD:r0 + rc]
                    convd.append(_ffn_conv(win, taps[h], rc))
                gate, val = convd
                act_ref[r0:r0 + rc, :] = ((gate * _sigmoid(gate)) * val).astype(BF16)

        for p, (lo, hi) in enumerate(parts):
            if p == 0:
                up_part(lo, hi)
            if p + 1 < len(parts):
                up_part(*parts[p + 1])
            if p > 0:
                down_part(*parts[p - 1])
            chunk_rows(lo, hi)
        down_part(*parts[-1])

        @pl.when(kb == nj - 1)
        def _():
            out_ref[...] = acc[...] + hc_ref[...]

    out, xo = _call(
        body, name=name, grid=(nt, nj),
        in_specs=[pl.BlockSpec((pad, D), lambda i, k: (jnp.maximum(i * hb - 1, 0), 0)),
                  pl.BlockSpec((tl, D), lambda i, k: (i, 0)),
                  pl.BlockSpec((1, D), lambda i, k: (0, 0)),
                  pl.BlockSpec((D, tc), lambda i, k: (0, k)), pl.BlockSpec((D, tc), lambda i, k: (0, k + nj)),
                  pl.BlockSpec((kw, tc), lambda i, k: (0, k)), pl.BlockSpec((kw, tc), lambda i, k: (0, k + nj)),
                  pl.BlockSpec((tc, D), lambda i, k: (k, 0))],
        out_specs=(pl.BlockSpec((tl, D), lambda i, k: (i, 0)), pl.BlockSpec((tl, D), lambda i, k: (i, 0)),
                   pl.BlockSpec((2, tl, tc), lambda i, k: (0, i, k)), pl.BlockSpec((tl, tc), lambda i, k: (i, k))),
        out_shape=(jax.ShapeDtypeStruct((L, D), F32), jax.ShapeDtypeStruct((L, D), BF16),
                   jax.ShapeDtypeStruct((2, L, f), F32), jax.ShapeDtypeStruct((L, f), BF16)),
        scratch_shapes=[pltpu.VMEM((pad, D), BF16), pltpu.VMEM((2, FFN_PAD, tc), F32), pltpu.VMEM((tl, D), F32)],
        sem=("parallel", "arbitrary"), args=(h_mid, h_mid, g, wup, wup, kf, kf, wdown), xchg=xchg)
    return out if xchg is None else (out, xo)


def _ffn_block_bwd(dh, h_mid, g, ug0, kf, wdown_t, wup_t, name, xchg=None):
    L, D = dh.shape
    f = ug0.shape[2]
    kw = kf.shape[0]
    tl = _token_tile(L)
    tc = _divisor(f, 256, 128)
    nj = f // tc
    nt = L // tl
    pad = 2 * SUBLANES
    hb, nhb = tl // FFN_PAD, L // FFN_PAD
    rc = CONV3_ROWS
    nc = tl // rc
    parts = _row_parts(nc)

    def body(dhc_ref, dhn_ref, hm_ref, g_ref, gp_ref, gc_ref, gn_ref, vp_ref, vc_ref, vn_ref, kg_ref, kv_ref,
             wd_ref, wg_ref, wv_ref, dhm_ref, dg_ref, du_ref, dk_ref, dh_ext, dact_s, acc):
        i = pl.program_id(0)
        kb = pl.program_id(1)

        @pl.when(kb == 0)
        def _():
            dh_ext[0:tl, :] = dhc_ref[...].astype(BF16)
            dh_ext[tl:tl + pad, :] = dhn_ref[...].astype(BF16)
            acc[...] = jnp.zeros_like(acc)

        @pl.when(jnp.logical_and(i == 0, kb == 0))
        def _():
            dg_ref[...] = jnp.zeros_like(dg_ref)
            dk_ref[...] = jnp.zeros_like(dk_ref)

        prev = (jnp.where(i > 0, gp_ref[...], 0.0), jnp.where(i > 0, vp_ref[...], 0.0))
        x_refs, nxt = (gc_ref, vc_ref), (gn_ref, vn_ref)
        taps = (_tap_rows(kg_ref), _tap_rows(kv_ref))
        dk = [[jnp.zeros((SUBLANES, tc), F32) for _ in range(kw)] for _ in range(2)]

        def dact_part(lo, hi):
            a, b = lo * rc, hi * rc + pad
            dact_s[a:b, :] = jnp.dot(dh_ext[a:b, :], wd_ref[...], preferred_element_type=F32)

        def dhn_part(lo, hi):
            a, b = lo * rc, hi * rc
            acc[a:b, :] += (jnp.dot(du_ref[0, a:b, :], wg_ref[...], preferred_element_type=F32)
                            + jnp.dot(du_ref[1, a:b, :], wv_ref[...], preferred_element_type=F32))

        for p, (lo, hi) in enumerate(parts):
            if p == 0:
                dact_part(lo, hi)
            if p + 1 < len(parts):
                dact_part(*parts[p + 1])
            if p > 0:
                dhn_part(*parts[p - 1])
            chunk_rows(lo, hi, prev, x_refs, nxt, taps, dk, i, dact_s, du_ref)
        dhn_part(*parts[-1])
        for h in range(2):
            for j in range(kw):
                dk_ref[kb, h, j:j + 1, :] += jnp.sum(dk[h][j], axis=0, keepdims=True)

        @pl.when(kb == nj - 1)
        def _():
            x = hm_ref[...]
            r = lax.rsqrt(jnp.mean(x * x, axis=-1, keepdims=True) + EPS)
            xhat = x * r
            dhn = acc[...]
            dxhat = dhn * g_ref[...]
            dhm_ref[...] = dhc_ref[...] + r * (dxhat - xhat * jnp.mean(dxhat * xhat, axis=-1, keepdims=True))
            dg_ref[...] += jnp.sum(_rowsum8(dhn * xhat), axis=0, keepdims=True)

    def chunk_rows(lo, hi, prev, x_refs, nxt, taps, dk, i, dact_s, du_ref):
        for c in range(lo, hi):
            r0 = c * rc
            n = rc + FFN_PAD
            xs = []
            for h in range(2):
                parts = [prev[h] if c == 0 else x_refs[h][r0 - FFN_PAD:r0]]
                if c == nc - 1:
                    parts += [x_refs[h][r0:r0 + rc], nxt[h][...]]
                else:
                    parts += [x_refs[h][r0:r0 + n]]
                xs.append(jnp.concatenate(parts, axis=0))
            gate = _ffn_conv(xs[0], taps[0], n)
            val = _ffn_conv(xs[1], taps[1], n)
            dact = dact_s[r0:r0 + n, :]
            sg = _sigmoid(gate)
            dcs = [dact * val * (sg * (1.0 + gate * (1.0 - sg))), dact * (gate * sg)]
            if c == nc - 1:
                live = jnp.logical_or(lax.broadcasted_iota(jnp.int32, (n, 1), 0) < rc, i < nt - 1)
                dcs = [jnp.where(live, d, 0.0) for d in dcs]
            for h in range(2):
                xc = xs[h][FFN_PAD:FFN_PAD + rc]
                dx = None
                for j in range(kw):
                    o = kw - 1 - j
                    sh = dcs[h][o:o + rc]
                    term = _rows_of(taps[h][j], rc) * sh
                    dx = term if dx is None else dx + term
                    dk[h][j] = dk[h][j] + _rowsum8(xc * sh)
                du_ref[h, r0:r0 + rc, :] = dx.astype(BF16)

    def prev8(i, k):
        return (jnp.maximum(i * hb - 1, 0), k)

    def next8(i, k):
        return (jnp.minimum((i + 1) * hb, nhb - 1), k)

    def half(h, rows, idx):
        return pl.BlockSpec((None, rows, tc), lambda i, k: (h,) + idx(i, k))

    def tile(i, k):
        return (i, k)

    out, xo = _call(
        body, name=name, grid=(nt, nj),
        in_specs=[pl.BlockSpec((tl, D), lambda i, k: (i, 0)),
                  pl.BlockSpec((pad, D), lambda i, k: (jnp.minimum((i + 1) * (tl // pad), L // pad - 1), 0)),
                  pl.BlockSpec((tl, D), lambda i, k: (i, 0)), pl.BlockSpec((1, D), lambda i, k: (0, 0)),
                  half(0, FFN_PAD, prev8), half(0, tl, tile), half(0, FFN_PAD, next8),
                  half(1, FFN_PAD, prev8), half(1, tl, tile), half(1, FFN_PAD, next8),
                  pl.BlockSpec((kw, tc), lambda i, k: (0, k)), pl.BlockSpec((kw, tc), lambda i, k: (0, k + nj)),
                  pl.BlockSpec((D, tc), lambda i, k: (0, k)),
                  pl.BlockSpec((tc, D), lambda i, k: (k, 0)), pl.BlockSpec((tc, D), lambda i, k: (k + nj, 0))],
        out_specs=(pl.BlockSpec((tl, D), lambda i, k: (i, 0)), pl.BlockSpec((1, D), lambda i, k: (0, 0)),
                   pl.BlockSpec((2, tl, tc), lambda i, k: (0, i, k)),
                   pl.BlockSpec((nj, 2, kw, tc), lambda i, k: (0, 0, 0, 0))),
        out_shape=(jax.ShapeDtypeStruct((L, D), F32), jax.ShapeDtypeStruct((1, D), F32),
                   jax.ShapeDtypeStruct((2, L, f), BF16), jax.ShapeDtypeStruct((nj, 2, kw, tc), F32)),
        scratch_shapes=[pltpu.VMEM((tl + pad, D), BF16), pltpu.VMEM((tl + pad, tc), F32), pltpu.VMEM((tl, D), F32)],
        sem=("arbitrary", "arbitrary"), args=(dh, dh, h_mid, g, ug0, ug0, ug0, ug0, ug0, ug0, kf, kf, wdown_t, wup_t, wup_t),
        xchg=xchg)
    return out if xchg is None else (out, xo)


def _adamw_math(w, g, m, v):
    m = ADAM_B1 * m + (1.0 - ADAM_B1) * g
    v = ADAM_B2 * v + (1.0 - ADAM_B2) * (g * g)
    m_hat = m / (1.0 - ADAM_B1 ** ADAM_STEP)
    v_hat = v / (1.0 - ADAM_B2 ** ADAM_STEP)
    delta = -ADAM_LR * (m_hat / (jnp.sqrt(v_hat) + ADAM_EPS) + ADAM_WD * w)
    return delta, m, v


def _sum_parts(parts_ref, idx):
    g = parts_ref[(0,) + idx].astype(F32)
    for q in range(1, N_DEV):
        g = g + parts_ref[(q,) + idx].astype(F32)
    return g


def _adamw_big(parts, w, m, v, name):
    nl, R, C = w.shape
    tr = _divisor(R, 256, 2 * SUBLANES)

    def body(*refs):
        p_refs = refs[:nl]
        w_ref, m_ref, v_ref, g_ref, d_ref, nm_ref, nv_ref = refs[nl:]
        layer = pl.program_id(0)
        for k in range(nl):
            @pl.when(layer == k)
            def _(k=k):
                g = _sum_parts(p_refs[k], ())
                d, nm, nv = _adamw_math(w_ref[0], g, m_ref[0], v_ref[0])
                g_ref[0] = g
                d_ref[0] = d
                nm_ref[0] = nm
                nv_ref[0] = nv

    def part_spec(k):
        return pl.BlockSpec((N_DEV, tr, C), lambda l, r: (0, jnp.where(l == k, r, 0), 0))

    blk = pl.BlockSpec((1, tr, C), lambda l, r: (l, r, 0))
    shp = jax.ShapeDtypeStruct((nl, R, C), F32)
    return pl.pallas_call(
        body, name=name, grid=(nl, R // tr),
        in_specs=[part_spec(k) for k in range(nl)] + [blk, blk, blk],
        out_specs=(blk, blk, blk, blk), out_shape=(shp, shp, shp, shp),
        compiler_params=_params(("arbitrary", "arbitrary")),
    )(*parts, w, m, v)


def _adamw_small(entries, name):
    n = len(entries)
    uniq = []
    for e in entries:
        if not any(e[0] is u for u in uniq):
            uniq.append(e[0])
    pidx = [next(k for k, u in enumerate(uniq) if u is e[0]) for e in entries]
    npart = len(uniq)

    def body(*refs):
        p_refs = refs[:npart]
        wmv = refs[npart:npart + 3 * n]
        outs = refs[npart + 3 * n:]
        for t, e in enumerate(entries):
            lo, w = e[1], e[2]
            rows = w.shape[0]
            pr = p_refs[pidx[t]]
            g = pr[0, lo:lo + rows].astype(F32)
            for q in range(1, N_DEV):
                g = g + pr[q, lo:lo + rows].astype(F32)
            d, nm, nv = _adamw_math(wmv[3 * t][...], g, wmv[3 * t + 1][...], wmv[3 * t + 2][...])
            outs[4 * t][...] = g
            outs[4 * t + 1][...] = d
            outs[4 * t + 2][...] = nm
            outs[4 * t + 3][...] = nv

    vm = pl.BlockSpec(memory_space=pltpu.VMEM)
    args = list(uniq)
    out_shape = []
    for e in entries:
        args += [e[2], e[3], e[4]]
        out_shape += [jax.ShapeDtypeStruct(e[2].shape, F32)] * 4
    res = pl.pallas_call(
        body, name=name, in_specs=[vm] * len(args), out_specs=tuple([vm] * len(out_shape)),
        out_shape=tuple(out_shape), compiler_params=_params(),
    )(*args)
    return [tuple(res[4 * t:4 * t + 4]) for t in range(n)]


def _head_matrix(cc):
    bw = min(256, cc)
    r = lax.broadcasted_iota(jnp.int32, (bw, bw), 0) // HEAD_DIM
    c = lax.broadcasted_iota(jnp.int32, (bw, bw), 1) // HEAD_DIM
    return jnp.where(r == c, 1.0 / HEAD_DIM, 0.0).astype(BF16)


def _cols_from_shards(g):
    nd = g.ndim
    perm = tuple(range(1, nd - 1)) + (0, nd - 1)
    t = jnp.transpose(g, perm)
    return t.reshape(t.shape[:-2] + (t.shape[-2] * t.shape[-1],))


def _cols_to_shards(a):
    nd = a.ndim
    t = a.reshape(a.shape[:-1] + (N_DEV, a.shape[-1] // N_DEV))
    perm = (nd - 1,) + tuple(range(nd - 1)) + (nd,)
    return jnp.transpose(t, perm)


def kernel(x, meta_tokens, norm1_g, w_in, conv_dw_k, conv_dw_b, conv_ln_g, conv_ln_b, pool_w, pool_scale, w_out, norm2_g, w_up, ffn_dw_k, w_down, final_g, loss_target, m_meta_tokens, m_norm1_g, m_w_in, m_conv_dw_k, m_conv_dw_b, m_conv_ln_g, m_conv_ln_b, m_pool_w, m_pool_scale, m_w_out, m_norm2_g, m_w_up, m_ffn_dw_k, m_w_down, m_final_g, v_meta_tokens, v_norm1_g, v_w_in, v_conv_dw_k, v_conv_dw_b, v_conv_ln_g, v_conv_ln_b, v_pool_w, v_pool_scale, v_w_out, v_norm2_g, v_w_up, v_ffn_dw_k, v_w_down, v_final_g):
    depth, D = norm1_g.shape
    n_meta = meta_tokens.shape[0]
    seq = x.shape[1]
    L = n_meta + seq
    cc = conv_dw_b.shape[1]
    ng, gd = pool_w.shape[1], pool_w.shape[2]
    f = w_down.shape[1] * N_DEV

    def cols(g):
        w = _cols_from_shards(g)
        return w, w.T

    def rows(g):
        w = g.reshape(-1, g.shape[-1])
        return w, w.T

    b16 = lambda a: a.astype(BF16)
    (g_in0, g_out0, g_ck, g_kf, g_meta) = _exchange([b16(w_in[0]), b16(w_out[0]), conv_dw_k, ffn_dw_k, meta_tokens],
                                                    ["gather"] * 5, "gather_first")
    ck_full = _cols_from_shards(g_ck)
    ck_rows = jnp.broadcast_to(ck_full[:, :, None, :], ck_full.shape[:2] + (SUBLANES, cc))
    kf_full = _cols_from_shards(g_kf)
    meta_full = _cols_from_shards(g_meta)
    am = _head_matrix(cc)
    win, wout, wup, wdown = [None] * depth, [None] * depth, [None] * depth, [None] * depth
    win[0] = cols(g_in0)
    wout[0] = rows(g_out0)

    h = jnp.concatenate([meta_full, x[0]], axis=0)
    saved = []
    for l in range(depth):
        more = l + 1 < depth
        hn1 = _rms_fwd(h, norm1_g[l:l + 1], f"rms1_fwd_{l}")
        if l == 0:
            z, (g_down,) = _mm(hn1, win[l][0], f"in_proj_{l}", tn_cap=768, xchg=([b16(w_down[l])], ["gather"]))
            wdown[l] = rows(g_down)
            (ymix, u1), (g_up,) = _mixer_fwd(z, ck_rows[l], conv_dw_b[l:l + 1], conv_ln_g[l:l + 1], conv_ln_b[l:l + 1],
                                             pool_w[l], pool_scale[l:l + 1], am, f"mixer_fwd_{l}",
                                             xchg=([b16(w_up[l])], ["gather"]))
            wup[l] = cols(g_up)
        else:
            z = _mm(hn1, win[l][0], f"in_proj_{l}", tn_cap=768)
            ymix, u1 = _mixer_fwd(z, ck_rows[l], conv_dw_b[l:l + 1], conv_ln_g[l:l + 1], conv_ln_b[l:l + 1], pool_w[l],
                                  pool_scale[l:l + 1], am, f"mixer_fwd_{l}")
        if more:
            h_mid, (g_in,) = _mm(ymix, wout[l][0], f"out_proj_{l}", res=h, tn_cap=512, xchg=([b16(w_in[l + 1])], ["gather"]))
            win[l + 1] = cols(g_in)
            nxt = [b16(w_out[l + 1]), b16(w_up[l + 1]), b16(w_down[l + 1])]
            (h_out, hn2, ug0, act), got = _ffn_block_fwd(h_mid, norm2_g[l:l + 1], wup[l][0], kf_full[l], wdown[l][0],
                                                         f"ffn_fwd_{l}", xchg=(nxt, ["gather"] * 3))
            wout[l + 1], wup[l + 1], wdown[l + 1] = rows(got[0]), cols(got[1]), rows(got[2])
        else:
            h_mid = _mm(ymix, wout[l][0], f"out_proj_{l}", res=h, tn_cap=512)
            h_out, hn2, ug0, act = _ffn_block_fwd(h_mid, norm2_g[l:l + 1], wup[l][0], kf_full[l], wdown[l][0], f"ffn_fwd_{l}")
        saved.append((h, hn1, z, u1, ymix, h_mid, hn2, ug0, act))
        h = h_out

    tgt = jnp.concatenate([jnp.zeros((n_meta, D), F32), loss_target[0]], axis=0)
    dh, d_final_g, loss_part = _loss_head(h, final_g.reshape(1, D), tgt, n_meta, "loss_head")

    def row_shards(gm):
        return b16(gm.reshape(N_DEV, -1, gm.shape[-1]))

    def col_shards(gm):
        return b16(_cols_to_shards(gm))

    gw = {k: [None] * depth for k in ("ck", "cb", "lg", "lb", "pw", "ps", "kf", "n1", "n2")}
    parts = {k: [None] * depth for k in ("in", "out", "up", "down")}
    for l in reversed(range(depth)):
        h_in, hn1, z, u1, ymix, h_mid, hn2, ug0, act = saved[l]
        g_down = _mm_tn(act, dh, f"down_proj_wgrad_{l}", tq_cap=512)
        (dh_mid, gw["n2"][l], dug0, dkf), (parts["down"][l],) = _ffn_block_bwd(
            dh, h_mid, norm2_g[l:l + 1], ug0, kf_full[l], wdown[l][1], wup[l][1], f"ffn_bwd_{l}",
            xchg=([row_shards(g_down)], ["a2a"]))
        gw["kf"][l] = jnp.transpose(dkf, (2, 1, 0, 3)).reshape(dkf.shape[2], -1)
        g_up = _mm_tn(hn2, dug0, f"up_proj_wgrad_{l}", halves=2)
        dymix = _mm(dh_mid, wout[l][1], f"out_proj_bwd_{l}", tn_cap=512)
        g_out = _mm_tn(ymix, dh_mid, f"out_proj_wgrad_{l}", tq_cap=512)
        ((dz, gw["ck"][l], gw["cb"][l], gw["lg"][l], gw["lb"][l], gw["pw"][l], gw["ps"][l]),
         (parts["up"][l], parts["out"][l])) = _mixer_bwd(
            z, u1, dymix, ck_rows[l], conv_ln_g[l:l + 1], conv_ln_b[l:l + 1], pool_w[l], pool_scale[l:l + 1], am,
            f"mixer_bwd_{l}", xchg=([col_shards(g_up), row_shards(g_out)], ["a2a", "a2a"]))
        g_in = _mm_tn(hn1, dz, f"in_proj_wgrad_{l}", tq_cap=768)
        dhn1, (parts["in"][l],) = _mm(dz, win[l][1], f"in_proj_bwd_{l}", tn_cap=512, xchg=([col_shards(g_in)], ["a2a"]))
        dh, gw["n1"][l] = _rms_bwd(h_in, norm1_g[l:l + 1], dhn1, dh_mid, f"rms1_bwd_{l}")
    grad_x = dh[n_meta:][None]
    d_meta = dh[:n_meta]

    zero_row = jnp.zeros((1, D), F32)
    pack_d = jnp.concatenate(gw["n1"] + gw["n2"] + [d_final_g, jnp.broadcast_to(loss_part[:, :1], (1, D)), zero_row, zero_row], axis=0)
    pack_c = jnp.concatenate(gw["cb"] + gw["lg"] + gw["lb"] + gw["ps"], axis=0)
    pack_pw = jnp.stack(gw["pw"]).reshape(depth * ng * gd, gd)
    src = [_cols_to_shards(jnp.stack(gw["ck"])), _cols_to_shards(jnp.stack(gw["kf"])), _cols_to_shards(d_meta),
           pack_d, pack_c, pack_pw]
    r_ck, r_kf, r_meta, r_d, r_c, r_pw = _exchange(src, ["a2a"] * 3 + ["gather"] * 3, "exchange_small_grads")

    big = {
        "w_in": _adamw_big(parts["in"], w_in, m_w_in, v_w_in, "adamw_w_in"),
        "w_out": _adamw_big(parts["out"], w_out, m_w_out, v_w_out, "adamw_w_out"),
        "w_up": _adamw_big(parts["up"], w_up, m_w_up, v_w_up, "adamw_w_up"),
        "w_down": _adamw_big(parts["down"], w_down, m_w_down, v_w_down, "adamw_w_down"),
    }
    kwid = conv_dw_k.shape[1]
    fkw = ffn_dw_k.shape[1]
    row = lambda a: a.reshape(1, -1)
    entries = [
        (r_d, 0, norm1_g, m_norm1_g, v_norm1_g),
        (r_d, depth, norm2_g, m_norm2_g, v_norm2_g),
        (r_d, 2 * depth, row(final_g), row(m_final_g), row(v_final_g)),
        (r_c, 0, conv_dw_b, m_conv_dw_b, v_conv_dw_b),
        (r_c, depth, conv_ln_g, m_conv_ln_g, v_conv_ln_g),
        (r_c, 2 * depth, conv_ln_b, m_conv_ln_b, v_conv_ln_b),
        (r_c, 3 * depth, pool_scale, m_pool_scale, v_pool_scale),
        (r_pw, 0, pool_w.reshape(-1, gd), m_pool_w.reshape(-1, gd), v_pool_w.reshape(-1, gd)),
        (r_ck.reshape(N_DEV, depth * kwid, -1), 0, conv_dw_k.reshape(depth * kwid, -1),
         m_conv_dw_k.reshape(depth * kwid, -1), v_conv_dw_k.reshape(depth * kwid, -1)),
        (r_kf.reshape(N_DEV, depth * fkw, -1), 0, ffn_dw_k.reshape(depth * fkw, -1),
         m_ffn_dw_k.reshape(depth * fkw, -1), v_ffn_dw_k.reshape(depth * fkw, -1)),
        (r_meta, 0, meta_tokens, m_meta_tokens, v_meta_tokens),
        (r_d, 2 * depth + 1, zero_row, zero_row, zero_row),
    ]
    small = _adamw_small(entries, "adamw_small")
    names = ["norm1_g", "norm2_g", "final_g", "conv_dw_b", "conv_ln_g", "conv_ln_b", "pool_scale", "pool_w",
             "conv_dw_k", "ffn_dw_k", "meta_tokens"]
    shapes = {"final_g": final_g.shape, "pool_w": pool_w.shape, "conv_dw_k": conv_dw_k.shape, "ffn_dw_k": ffn_dw_k.shape}
    res = dict(big)
    for nme, quad in zip(names, small[:-1]):
        res[nme] = tuple(a.reshape(shapes[nme]) if nme in shapes else a for a in quad)
    loss = small[-1][0][0, 0]

    order = ["meta_tokens", "norm1_g", "w_in", "conv_dw_k", "conv_dw_b", "conv_ln_g", "conv_ln_b", "pool_w", "pool_scale",
             "w_out", "norm2_g", "w_up", "ffn_dw_k", "w_down", "final_g"]
    return (loss, grad_x, *[res[k][0] for k in order], *[res[k][1] for k in order], *[res[k][2] for k in order],
            *[res[k][3] for k in order])
```

```python
import functools

import jax
import jax.numpy as jnp
from jax import lax
from jax.experimental import pallas as pl
from jax.experimental.pallas import tpu as pltpu

F32 = jnp.float32
BF16 = jnp.bfloat16

EPS = 1e-6
HEAD_DIM = 64
POOL_WINDOWS = (2, 4, 8, 16)
ADAM_LR = 0.001
ADAM_B1 = 0.9
ADAM_B2 = 0.999
ADAM_EPS = 1e-08
ADAM_WD = 0.01
ADAM_STEP = 10

N_DEV = 8
SUBLANES = 8
HALO = 48
CONV_PAD = 32
POOL_PAD = 16
FFN_PAD = 8
ROW_CHUNK = 24
CONV3_ROWS = 48
MAX_TILE_ROWS = 1024
WGRAD_TILE_ROWS = 2816
VMEM_LIMIT = 52 * 1024 * 1024


def _divisor(n, cap, mult):
    best = None
    for d in range(mult, min(n, cap) + 1, mult):
        if n % d == 0:
            best = d
    return n if best is None else best


def _token_tile(L):
    return _divisor(L, MAX_TILE_ROWS, HALO)


def _row_tile(L):
    return _divisor(L, 320, 2 * SUBLANES)


def _stat_rows(tl):
    return _divisor(tl, 256, SUBLANES)


def _params(sem=None):
    return pltpu.CompilerParams(dimension_semantics=sem, vmem_limit_bytes=VMEM_LIMIT)


def _rowsum8(x):
    acc = x[0:SUBLANES]
    for k in range(1, x.shape[0] // SUBLANES):
        acc = acc + x[k * SUBLANES:(k + 1) * SUBLANES]
    return acc


def _sigmoid(x):
    return jax.nn.sigmoid(x)


def _head_mean(x, am_ref):
    bw = am_ref.shape[0]
    am = am_ref[...]
    outs = []
    for blk in range(x.shape[1] // bw):
        xb = x[:, blk * bw:(blk + 1) * bw]
        hi = xb.astype(BF16)
        lo = (xb - hi.astype(F32)).astype(BF16)
        outs.append(jnp.dot(hi, am, preferred_element_type=F32) + jnp.dot(lo, am, preferred_element_type=F32))
    return outs[0] if len(outs) == 1 else jnp.concatenate(outs, axis=-1)


def _xchg_out_shapes(srcs, modes):
    out = []
    for s, m in zip(srcs, modes):
        shp = ((N_DEV,) + tuple(s.shape)) if m == "gather" else tuple(s.shape)
        out.append(jax.ShapeDtypeStruct(shp, s.dtype))
    return out


def _xchg_sems(n):
    return [pltpu.SemaphoreType.DMA((n, N_DEV - 1)), pltpu.SemaphoreType.DMA((n, N_DEV - 1)), pltpu.SemaphoreType.DMA((n,))]


def _xchg_ops(src_refs, out_refs, sems, modes):
    n = len(src_refs)
    send_sems, recv_sems, local_sems = sems
    x, y, c = lax.axis_index("x"), lax.axis_index("y"), lax.axis_index("c")
    me = 4 * x + 2 * y + c

    def peer(d):
        return (x ^ ((d >> 2) & 1), y ^ ((d >> 1) & 1), c ^ (d & 1))

    def peer_id(d):
        px, py, pc = peer(d)
        return 4 * px + 2 * py + pc

    def remote(t, d):
        src = src_refs[t] if modes[t] == "gather" else src_refs[t].at[peer_id(d)]
        return pltpu.make_async_remote_copy(
            src_ref=src, dst_ref=out_refs[t].at[me], send_sem=send_sems.at[t, d - 1], recv_sem=recv_sems.at[t, d - 1],
            device_id=peer(d), device_id_type=pl.DeviceIdType.MESH)

    def arrival(t, d):
        src = src_refs[t] if modes[t] == "gather" else src_refs[t].at[me]
        return pltpu.make_async_remote_copy(
            src_ref=src, dst_ref=out_refs[t].at[peer_id(d)], send_sem=send_sems.at[t, d - 1],
            recv_sem=recv_sems.at[t, d - 1], device_id=peer(d), device_id_type=pl.DeviceIdType.MESH)

    def local(t):
        src = src_refs[t] if modes[t] == "gather" else src_refs[t].at[me]
        return pltpu.make_async_copy(src, out_refs[t].at[me], local_sems.at[t])

    def start():
        for t in range(n):
            local(t).start()
        for t in range(n):
            for d in range(1, N_DEV):
                remote(t, d).start()

    def wait():
        for t in range(n):
            for d in range(1, N_DEV):
                arrival(t, d).wait_recv()
        for t in range(n):
            for d in range(1, N_DEV):
                remote(t, d).wait_send()
        for t in range(n):
            local(t).wait()

    return start, wait


def _exchange(srcs, modes, name):
    n = len(srcs)

    def body(*refs):
        start, wait = _xchg_ops(refs[:n], refs[n:2 * n], refs[2 * n:], modes)
        start()
        wait()

    any_spec = pl.BlockSpec(memory_space=pl.ANY)
    return pl.pallas_call(
        body, name=name, out_shape=tuple(_xchg_out_shapes(srcs, modes)),
        in_specs=[any_spec] * n, out_specs=tuple([any_spec] * n),
        scratch_shapes=_xchg_sems(n),
        compiler_params=pltpu.CompilerParams(has_side_effects=True),
    )(*srcs)


def _call(body, *, name, grid, in_specs, out_specs, out_shape, args, scratch_shapes=(), sem=None, xchg=None):
    single = not isinstance(out_shape, (tuple, list))
    outs_shape = [out_shape] if single else list(out_shape)
    outs_spec = [out_specs] if single else list(out_specs)
    if xchg is None:
        res = pl.pallas_call(
            body, name=name, grid=grid, in_specs=list(in_specs), out_specs=out_specs, out_shape=out_shape,
            scratch_shapes=list(scratch_shapes), compiler_params=_params(sem))(*args)
        return res, ()
    srcs, modes = xchg
    n_in, n_out, n_scr, nx = len(in_specs), len(outs_shape), len(scratch_shapes), len(srcs)

    def wrapped(*refs):
        ins = refs[:n_in]
        xs = refs[n_in:n_in + nx]
        o0 = n_in + nx
        outs = refs[o0:o0 + n_out]
        xo = refs[o0 + n_out:o0 + n_out + nx]
        s0 = o0 + n_out + nx
        scr = refs[s0:s0 + n_scr]
        start, wait = _xchg_ops(xs, xo, refs[s0 + n_scr:], modes)
        first = functools.reduce(jnp.logical_and, [pl.program_id(a) == 0 for a in range(len(grid))])
        last = functools.reduce(jnp.logical_and, [pl.program_id(a) == grid[a] - 1 for a in range(len(grid))])

        @pl.when(first)
        def _():
            start()

        body(*ins, *outs, *scr)

        @pl.when(last)
        def _():
            wait()

    any_spec = pl.BlockSpec(memory_space=pl.ANY)
    res = pl.pallas_call(
        wrapped, name=name, grid=grid, in_specs=list(in_specs) + [any_spec] * nx,
        out_specs=tuple(outs_spec + [any_spec] * nx), out_shape=tuple(outs_shape + _xchg_out_shapes(srcs, modes)),
        scratch_shapes=list(scratch_shapes) + _xchg_sems(nx),
        compiler_params=_params(("arbitrary",) * len(grid)))(*args, *srcs)
    comp = res[:n_out]
    return (comp[0] if single else tuple(comp)), tuple(res[n_out:])


def _norm_proj(h, g, w, name, *, tn_cap, xchg=None):
    L, D = h.shape
    N = w.shape[1]
    tm = _token_tile(L)
    tn = _divisor(N, tn_cap, 128)

    def body(h_ref, g_ref, w_ref, z_ref, hn_ref):
        @pl.when(pl.program_id(1) == 0)
        def _():
            x = h_ref[...]
            r = lax.rsqrt(jnp.mean(x * x, axis=-1, keepdims=True) + EPS)
            hn_ref[...] = ((x * r) * g_ref[...]).astype(BF16)

        z_ref[...] = jnp.dot(hn_ref[...], w_ref[...], preferred_element_type=F32)

    out, xo = _call(
        body, name=name, grid=(L // tm, N // tn),
        in_specs=[pl.BlockSpec((tm, D), lambda i, j: (i, 0)), pl.BlockSpec((1, D), lambda i, j: (0, 0)),
                  pl.BlockSpec((D, tn), lambda i, j: (0, j))],
        out_specs=(pl.BlockSpec((tm, tn), lambda i, j: (i, j)), pl.BlockSpec((tm, D), lambda i, j: (i, 0))),
        out_shape=(jax.ShapeDtypeStruct((L, N), F32), jax.ShapeDtypeStruct((L, D), BF16)),
        sem=("parallel", "arbitrary"), args=(h, g, w), xchg=xchg)
    return out if xchg is None else (out, xo)


def _proj_bwd_norm(a, b, h, g, dres, name, xchg=None):
    L, K = a.shape
    D = b.shape[1]
    tm = _token_tile(L)

    def body(a_ref, b_ref, h_ref, g_ref, dres_ref, dh_ref, dg_ref):
        i = pl.program_id(0)
        dhn = jnp.dot(a_ref[...], b_ref[...], preferred_element_type=F32)
        x = h_ref[...]
        r = lax.rsqrt(jnp.mean(x * x, axis=-1, keepdims=True) + EPS)
        xhat = x * r
        dxhat = dhn * g_ref[...]
        dh_ref[...] = dres_ref[...] + r * (dxhat - xhat * jnp.mean(dxhat * xhat, axis=-1, keepdims=True))
        part = jnp.sum(_rowsum8(dhn * xhat), axis=0, keepdims=True)

        @pl.when(i == 0)
        def _():
            dg_ref[...] = part

        @pl.when(i > 0)
        def _():
            dg_ref[...] += part

    tile = pl.BlockSpec((tm, D), lambda i: (i, 0))
    row = pl.BlockSpec((1, D), lambda i: (0, 0))
    out, xo = _call(
        body, name=name, grid=(L // tm,),
        in_specs=[pl.BlockSpec((tm, K), lambda i: (i, 0)), pl.BlockSpec((K, D), lambda i: (0, 0)), tile, row, tile],
        out_specs=(tile, row), out_shape=(jax.ShapeDtypeStruct((L, D), F32), jax.ShapeDtypeStruct((1, D), F32)),
        sem=("arbitrary",), args=(a, b, h, g, dres), xchg=xchg)
    return out if xchg is None else (out, xo)


def _mm(a, b, name, *, res=None, out_dtype=F32, tn_cap=1408, halves=1, xchg=None):
    if halves > 1:
        _, M, kh = a.shape
        K = kh * halves
    else:
        M, K = a.shape
        kh = K
    N = b.shape[1]
    tm = _token_tile(M)
    tn = _divisor(N, tn_cap, 128)
    tk = kh if kh <= 2816 else _divisor(kh, 2816, 128)
    kper = kh // tk
    nk = halves * kper
    grid = (M // tm, N // tn, nk)

    def body(*refs):
        if res is None:
            a_ref, b_ref, o_ref = refs[:3]
            r_ref = None
            scratch = refs[3:]
        else:
            a_ref, b_ref, r_ref, o_ref = refs[:4]
            scratch = refs[4:]
        av = a_ref[0] if halves > 1 else a_ref[...]
        prod = jnp.dot(av.astype(BF16), b_ref[...], preferred_element_type=F32)
        if nk == 1:
            if r_ref is not None:
                prod = prod + r_ref[...]
            o_ref[...] = prod.astype(o_ref.dtype)
        else:
            acc = scratch[0]
            k = pl.program_id(2)

            @pl.when(k == 0)
            def _():
                acc[...] = prod

            @pl.when(k > 0)
            def _():
                acc[...] += prod

            @pl.when(k == nk - 1)
            def _():
                tot = acc[...]
                if r_ref is not None:
                    tot = tot + r_ref[...]
                o_ref[...] = tot.astype(o_ref.dtype)

    if halves > 1:
        a_spec = pl.BlockSpec((1, tm, tk), lambda i, j, k: (k // kper, i, k % kper))
    else:
        a_spec = pl.BlockSpec((tm, tk), lambda i, j, k: (i, k))
    in_specs = [a_spec, pl.BlockSpec((tk, tn), lambda i, j, k: (k, j))]
    args = [a, b]
    if res is not None:
        in_specs.append(pl.BlockSpec((tm, tn), lambda i, j, k: (i, j)))
        args.append(res)
    out, xo = _call(
        body, name=name, grid=grid, in_specs=in_specs,
        out_specs=pl.BlockSpec((tm, tn), lambda i, j, k: (i, j)),
        out_shape=jax.ShapeDtypeStruct((M, N), out_dtype),
        scratch_shapes=[pltpu.VMEM((tm, tn), F32)] if nk > 1 else [],
        sem=("parallel", "parallel", "arbitrary"), args=args, xchg=xchg)
    return out if xchg is None else (out, xo)


def _mm_tn(a, b, name, *, halves=1, tq_cap=1408):
    L, P = a.shape
    if halves > 1:
        qh = b.shape[2]
        Q = qh * halves
    else:
        Q = b.shape[1]
        qh = Q
    tl = _divisor(L, WGRAD_TILE_ROWS, HALO)
    tp = _divisor(P, 1408, 128)
    tq = _divisor(qh, tq_cap, 128)
    qper = qh // tq
    grid = (P // tp, Q // tq, L // tl)

    def body(a_ref, b_ref, o_ref):
        bv = b_ref[0] if halves > 1 else b_ref[...]
        prod = lax.dot_general(a_ref[...].astype(BF16), bv.astype(BF16), (((0,), (0,)), ((), ())),
                               preferred_element_type=F32)
        l = pl.program_id(2)

        @pl.when(l == 0)
        def _():
            o_ref[...] = prod

        @pl.when(l > 0)
        def _():
            o_ref[...] += prod

    if halves > 1:
        b_spec = pl.BlockSpec((1, tl, tq), lambda p, q, l: (q // qper, l, q % qper))
    else:
        b_spec = pl.BlockSpec((tl, tq), lambda p, q, l: (l, q))
    return pl.pallas_call(
        body, name=name, grid=grid,
        in_specs=[pl.BlockSpec((tl, tp), lambda p, q, l: (l, p)), b_spec],
        out_specs=pl.BlockSpec((tp, tq), lambda p, q, l: (p, q)),
        out_shape=jax.ShapeDtypeStruct((P, Q), F32),
        compiler_params=_params(("parallel", "parallel", "arbitrary")),
    )(a, b)


def _loss_head(h, g, tgt, n_meta, name):
    L, D = h.shape
    tl = _row_tile(L)
    nt = L // tl

    def body(h_ref, g_ref, t_ref, dh_ref, dg_ref, loss_ref):
        i = pl.program_id(0)
        x = h_ref[...]
        r = lax.rsqrt(jnp.mean(x * x, axis=-1, keepdims=True) + EPS)
        xhat = x * r
        gg = g_ref[...]
        y = xhat * gg
        rows = i * tl + lax.broadcasted_iota(jnp.int32, (tl, 1), 0)
        err = jnp.where(rows >= n_meta, y - t_ref[...], 0.0)
        dy = err * (1.0 / D)
        dxhat = dy * gg
        dh_ref[...] = r * (dxhat - xhat * jnp.mean(dxhat * xhat, axis=-1, keepdims=True))
        dg_part = jnp.sum(_rowsum8(dy * xhat), axis=0, keepdims=True)
        per_row = jnp.mean(err * err, axis=-1, keepdims=True)
        loss_part = jnp.broadcast_to(0.5 * jnp.sum(per_row, axis=0, keepdims=True), (1, 128))

        @pl.when(i == 0)
        def _():
            dg_ref[...] = dg_part
            loss_ref[...] = loss_part

        @pl.when(i > 0)
        def _():
            dg_ref[...] += dg_part
            loss_ref[...] += loss_part

    tile = pl.BlockSpec((tl, D), lambda i: (i, 0))
    row = pl.BlockSpec((1, D), lambda i: (0, 0))
    return pl.pallas_call(
        body, name=name, grid=(nt,), in_specs=[tile, row, tile],
        out_specs=(tile, row, pl.BlockSpec((1, 128), lambda i: (0, 0))),
        out_shape=(jax.ShapeDtypeStruct((L, D), F32), jax.ShapeDtypeStruct((1, D), F32),
                   jax.ShapeDtypeStruct((1, 128), F32)),
        compiler_params=_params(("arbitrary",)),
    )(h, g, tgt)


def _pool_fwd_block(pwin, pw_ref, row0, rb, g, gd, w, t0):
    wv = pwin[pl.ds(row0 + HALO - POOL_PAD, rb + POOL_PAD), g * gd:(g + 1) * gd]
    s = wv
    sh = 1
    while sh < w:
        s = s + pltpu.roll(s, sh, axis=0)
        sh *= 2
    win = s[POOL_PAD:POOL_PAD + rb]
    pt = wv[POOL_PAD:POOL_PAD + rb]
    tg = t0 + lax.broadcasted_iota(jnp.int32, (rb, 1), 0)
    cnt = jnp.minimum(tg + 1, w).astype(F32)
    return win / cnt - pt


def _fill_windows(i, zp_ref, zc_ref, u0w, pwin, tl, cc):
    keep = i > 0
    zp = zp_ref[...]
    u0w[0:HALO, :] = jnp.where(keep, zp[:, :cc] * _sigmoid(zp[:, cc:2 * cc]), 0.0)
    pwin[0:HALO, :] = jnp.where(keep, zp[:, 2 * cc:], 0.0)

    def fill(c, carry):
        b = pl.multiple_of(c * ROW_CHUNK, SUBLANES)
        zc = zc_ref[pl.ds(b, ROW_CHUNK), :]
        u0w[pl.ds(HALO + b, ROW_CHUNK), :] = zc[:, :cc] * _sigmoid(zc[:, cc:2 * cc])
        pwin[pl.ds(HALO + b, ROW_CHUNK), :] = zc[:, 2 * cc:]
        return carry

    lax.fori_loop(0, tl // ROW_CHUNK, fill, 0)


def _mixer_fwd(z, ck, cb, lg, lb, pw, ps, am, name, xchg=None):
    L, ci = z.shape
    kw, _, cc = ck.shape
    cp = ci - 2 * cc
    ng, gd = pw.shape[0], pw.shape[1]
    tl = _token_tile(L)
    nt = L // tl
    hb = tl // HALO
    rb = _stat_rows(tl)
    tap0 = CONV_PAD - (kw - 1)

    def body(zp_ref, zc_ref, ck_ref, cb_ref, lg_ref, lb_ref, pw_ref, ps_ref, am_ref, y_ref, u1_ref, u0w, pwin):
        i = pl.program_id(0)
        _fill_windows(i, zp_ref, zc_ref, u0w, pwin, tl, cc)

        def conv(c, carry):
            b = pl.multiple_of(c * ROW_CHUNK, SUBLANES)
            w = u0w[pl.ds(b + HALO - CONV_PAD, ROW_CHUNK + CONV_PAD), :]
            acc = jnp.broadcast_to(cb_ref[...], (ROW_CHUNK, cc))
            for j in range(kw):
                acc = acc + _rows_of(ck_ref[j], ROW_CHUNK) * w[tap0 + j:tap0 + j + ROW_CHUNK]
            u1_ref[pl.ds(b, ROW_CHUNK), :] = acc
            return carry

        lax.fori_loop(0, tl // ROW_CHUNK, conv, 0)

        def blocks(k, carry):
            b = pl.multiple_of(k * rb, SUBLANES)
            u1 = u1_ref[pl.ds(b, rb), :]
            xc = u1 - _head_mean(u1, am_ref)
            var = _head_mean(xc * xc, am_ref)
            u2 = (xc * lax.rsqrt(var + EPS)) * lg_ref[...] + lb_ref[...]
            y_ref[pl.ds(b, rb), 0:cc] = (u2 * _sigmoid(u2)).astype(y_ref.dtype)
            for g in range(ng):
                d = _pool_fwd_block(pwin, pw_ref, b, rb, g, gd, POOL_WINDOWS[g], i * tl + b)
                yp = jnp.dot(d.astype(BF16), pw_ref[g].astype(BF16), preferred_element_type=F32)
                yp = yp * ps_ref[:, g * gd:(g + 1) * gd]
                y_ref[pl.ds(b, rb), cc + g * gd:cc + (g + 1) * gd] = yp.astype(y_ref.dtype)
            return carry

        lax.fori_loop(0, tl // rb, blocks, 0)

    def full(a):
        nd = a.ndim
        return pl.BlockSpec(a.shape, lambda i: (0,) * nd)

    out, xo = _call(
        body, name=name, grid=(nt,),
        in_specs=[pl.BlockSpec((HALO, ci), lambda i: (jnp.maximum(i * hb - 1, 0), 0)),
                  pl.BlockSpec((tl, ci), lambda i: (i, 0)),
                  full(ck), full(cb), full(lg), full(lb), full(pw), full(ps), full(am)],
        out_specs=(pl.BlockSpec((tl, cc + cp), lambda i: (i, 0)), pl.BlockSpec((tl, cc), lambda i: (i, 0))),
        out_shape=(jax.ShapeDtypeStruct((L, cc + cp), BF16), jax.ShapeDtypeStruct((L, cc), F32)),
        scratch_shapes=[pltpu.VMEM((HALO + tl, cc), F32), pltpu.VMEM((HALO + tl, cp), F32)],
        sem=("parallel",), args=(z, z, ck, cb, lg, lb, pw, ps, am), xchg=xchg)
    return out if xchg is None else (out, xo)


def _mixer_bwd(z, u1, dy, ck, lg, lb, pw, ps, am, name, xchg=None):
    L, ci = z.shape
    kw, _, cc = ck.shape
    cp = ci - 2 * cc
    ng, gd = pw.shape[0], pw.shape[1]
    tl = _token_tile(L)
    nt = L // tl
    hb = tl // HALO
    rb = _stat_rows(tl)

    def body(zp_ref, zc_ref, u1c_ref, u1n_ref, dyc_ref, dyn_ref, ck_ref, lg_ref, lb_ref, pw_ref, ps_ref, am_ref,
             dz_ref, dck_ref, dcb_ref, dlg_ref, dlb_ref, dpw_ref, dps_ref,
             u0w, pwin, du1w, ddw, ew, dkacc, dcb8, dlg8, dlb8, dps8):
        i = pl.program_id(0)
        has_next = i < nt - 1

        @pl.when(i == 0)
        def _():
            dck_ref[...] = jnp.zeros_like(dck_ref)
            dcb_ref[...] = jnp.zeros_like(dcb_ref)
            dlg_ref[...] = jnp.zeros_like(dlg_ref)
            dlb_ref[...] = jnp.zeros_like(dlb_ref)
            dpw_ref[...] = jnp.zeros_like(dpw_ref)
            dps_ref[...] = jnp.zeros_like(dps_ref)

        dkacc[...] = jnp.zeros_like(dkacc)
        dcb8[...] = jnp.zeros_like(dcb8)
        dlg8[...] = jnp.zeros_like(dlg8)
        dlb8[...] = jnp.zeros_like(dlb8)
        dps8[...] = jnp.zeros_like(dps8)

        _fill_windows(i, zp_ref, zc_ref, u0w, pwin, tl, cc)

        def conv_side(u1, dyc, own):
            xc = u1 - _head_mean(u1, am_ref)
            rstd = lax.rsqrt(_head_mean(xc * xc, am_ref) + EPS)
            uh = xc * rstd
            lgv = lg_ref[...]
            u2 = uh * lgv + lb_ref[...]
            sg = _sigmoid(u2)
            du2 = dyc * (sg * (1.0 + u2 * (1.0 - sg)))
            if own:
                dlg8[...] += _rowsum8(du2 * uh)
                dlb8[...] += _rowsum8(du2)
            duh = du2 * lgv
            return rstd * (duh - _head_mean(duh, am_ref) - uh * _head_mean(duh * uh, am_ref))

        def pool_side(dyp, t0, rows):
            dds, es = [], []
            tg = t0 + lax.broadcasted_iota(jnp.int32, (rows, 1), 0)
            for g in range(ng):
                dypre = dyp[:, g * gd:(g + 1) * gd] * ps_ref[:, g * gd:(g + 1) * gd]
                dd = lax.dot_general(dypre.astype(BF16), pw_ref[g].astype(BF16), (((1,), (1,)), ((), ())),
                                     preferred_element_type=F32)
                cnt = jnp.minimum(tg + 1, POOL_WINDOWS[g]).astype(F32)
                dds.append(dd)
                es.append(dd / cnt)
            return jnp.concatenate(dds, axis=-1), jnp.concatenate(es, axis=-1)

        def blocks(k, carry):
            b = pl.multiple_of(k * rb, SUBLANES)
            dyb = dyc_ref[pl.ds(b, rb), :]
            du1 = conv_side(u1c_ref[pl.ds(b, rb), :], dyb[:, :cc], True)
            du1w[pl.ds(b, rb), :] = du1
            dcb8[...] += _rowsum8(du1)
            dyp = dyb[:, cc:]
            dd, e = pool_side(dyp, i * tl + b, rb)
            ddw[pl.ds(b, rb), :] = dd
            ew[pl.ds(b, rb), :] = e
            for g in range(ng):
                d = _pool_fwd_block(pwin, pw_ref, b, rb, g, gd, POOL_WINDOWS[g], i * tl + b)
                db16 = d.astype(BF16)
                dypg = dyp[:, g * gd:(g + 1) * gd]
                ypre = jnp.dot(db16, pw_ref[g].astype(BF16), preferred_element_type=F32)
                dps8[:, g * gd:(g + 1) * gd] += _rowsum8(dypg * ypre)
                dypre = (dypg * ps_ref[:, g * gd:(g + 1) * gd]).astype(BF16)
                dpw_ref[g] += lax.dot_general(db16, dypre, (((0,), (0,)), ((), ())), preferred_element_type=F32)
            return carry

        lax.fori_loop(0, tl // rb, blocks, 0)

        dyn = dyn_ref[...]
        du1n = conv_side(u1n_ref[...], dyn[:, :cc], False)
        du1w[tl:tl + HALO, :] = jnp.where(has_next, du1n, 0.0)
        ddn, en = pool_side(dyn[:, cc:], (i + 1) * tl, HALO)
        ew[tl:tl + HALO, :] = jnp.where(has_next, en, 0.0)

        def taps(c, carry):
            b = pl.multiple_of(c * ROW_CHUNK, SUBLANES)
            w = du1w[pl.ds(b, ROW_CHUNK + CONV_PAD), :]
            u0c = u0w[pl.ds(HALO + b, ROW_CHUNK), :]
            acc = jnp.zeros((ROW_CHUNK, cc), F32)
            for j in range(kw):
                o = kw - 1 - j
                sh = w[o:o + ROW_CHUNK]
                acc = acc + _rows_of(ck_ref[j], ROW_CHUNK) * sh
                dkacc[j] += _rowsum8(u0c * sh)
            zc = zc_ref[pl.ds(b, ROW_CHUNK), :]
            a = zc[:, :cc]
            sg = _sigmoid(zc[:, cc:2 * cc])
            dz_ref[pl.ds(b, ROW_CHUNK), 0:cc] = (acc * sg).astype(dz_ref.dtype)
            dz_ref[pl.ds(b, ROW_CHUNK), cc:2 * cc] = (acc * a * sg * (1.0 - sg)).astype(dz_ref.dtype)
            return carry

        lax.fori_loop(0, tl // ROW_CHUNK, taps, 0)

        def pool_back(k, carry):
            b = pl.multiple_of(k * rb, SUBLANES)
            n = rb + POOL_PAD
            for g in range(ng):
                s = ew[pl.ds(b, n), g * gd:(g + 1) * gd]
                sh = 1
                while sh < POOL_WINDOWS[g]:
                    s = s + pltpu.roll(s, n - sh, axis=0)
                    sh *= 2
                dp = s[0:rb] - ddw[pl.ds(b, rb), g * gd:(g + 1) * gd]
                dz_ref[pl.ds(b, rb), 2 * cc + g * gd:2 * cc + (g + 1) * gd] = dp.astype(dz_ref.dtype)
            return carry

        lax.fori_loop(0, tl // rb, pool_back, 0)

        dck_ref[...] += jnp.sum(dkacc[...], axis=1)
        dcb_ref[...] += jnp.sum(dcb8[...], axis=0, keepdims=True)
        dlg_ref[...] += jnp.sum(dlg8[...], axis=0, keepdims=True)
        dlb_ref[...] += jnp.sum(dlb8[...], axis=0, keepdims=True)
        dps_ref[...] += jnp.sum(dps8[...], axis=0, keepdims=True)

    def full(a):
        nd = a.ndim
        return pl.BlockSpec(a.shape, lambda i: (0,) * nd)

    nhb = L // HALO

    def prev_map(i):
        return (jnp.maximum(i * hb - 1, 0), 0)

    def next_map(i):
        return (jnp.minimum((i + 1) * hb, nhb - 1), 0)

    dcc = cc + cp
    row_cc = jax.ShapeDtypeStruct((1, cc), F32)
    out_shape = (jax.ShapeDtypeStruct((L, ci), BF16), jax.ShapeDtypeStruct((kw, cc), F32), row_cc, row_cc, row_cc,
                 jax.ShapeDtypeStruct((ng, gd, gd), F32), jax.ShapeDtypeStruct((1, cp), F32))
    acc_spec = [pl.BlockSpec((kw, cc), lambda i: (0, 0))] + [pl.BlockSpec((1, cc), lambda i: (0, 0))] * 3 + [
        pl.BlockSpec((ng, gd, gd), lambda i: (0, 0, 0)), pl.BlockSpec((1, cp), lambda i: (0, 0))]
    out, xo = _call(
        body, name=name, grid=(nt,),
        in_specs=[pl.BlockSpec((HALO, ci), prev_map), pl.BlockSpec((tl, ci), lambda i: (i, 0)),
                  pl.BlockSpec((tl, cc), lambda i: (i, 0)), pl.BlockSpec((HALO, cc), next_map),
                  pl.BlockSpec((tl, dcc), lambda i: (i, 0)), pl.BlockSpec((HALO, dcc), next_map),
                  full(ck), full(lg), full(lb), full(pw), full(ps), full(am)],
        out_specs=tuple([pl.BlockSpec((tl, ci), lambda i: (i, 0))] + acc_spec),
        out_shape=out_shape,
        scratch_shapes=[pltpu.VMEM((HALO + tl, cc), F32), pltpu.VMEM((HALO + tl, cp), F32),
                        pltpu.VMEM((tl + HALO, cc), F32), pltpu.VMEM((tl, cp), F32), pltpu.VMEM((tl + HALO, cp), F32),
                        pltpu.VMEM((kw, SUBLANES, cc), F32), pltpu.VMEM((SUBLANES, cc), F32),
                        pltpu.VMEM((SUBLANES, cc), F32), pltpu.VMEM((SUBLANES, cc), F32), pltpu.VMEM((SUBLANES, cp), F32)],
        sem=("arbitrary",), args=(z, z, u1, u1, dy, dy, ck, lg, lb, pw, ps, am), xchg=xchg)
    return out if xchg is None else (out, xo)


def _row_parts(nc, n=3):
    n = min(n, nc)
    cuts = [round(k * nc / n) for k in range(n + 1)]
    return [(cuts[k], cuts[k + 1]) for k in range(n)]


def _tap_rows(k_ref):
    return [jnp.broadcast_to(k_ref[j:j + 1, :], (SUBLANES, k_ref.shape[1])) for j in range(k_ref.shape[0])]


def _rows_of(tap, n):
    return tap if n == SUBLANES else jnp.concatenate([tap] * (n // SUBLANES), axis=0)


def _ffn_conv(win, taps, rows):
    kw = len(taps)
    o = FFN_PAD - (kw - 1)
    acc = _rows_of(taps[0], rows) * win[o:o + rows]
    for j in range(1, kw):
        acc = acc + _rows_of(taps[j], rows) * win[o + j:o + j + rows]
    return acc


def _ffn_block_fwd(h_mid, g, wup, kf, wdown, name, xchg=None):
    L, D = h_mid.shape
    f = wdown.shape[0]
    kw = kf.shape[0]
    tl = _token_tile(L)
    tc = _divisor(f, 256, 128)
    nj = f // tc
    nt = L // tl
    pad = 2 * SUBLANES
    hb = tl // pad
    rc = CONV3_ROWS
    parts = _row_parts(tl // rc)

    def body(hp_ref, hc_ref, g_ref, wg_ref, wv_ref, kg_ref, kv_ref, wd_ref, out_ref, hn_ref, ug_ref, act_ref, hn_halo, halo, acc):
        i = pl.program_id(0)
        kb = pl.program_id(1)

        @pl.when(kb == 0)
        def _():
            gg = g_ref[...]

            def norm(x):
                r = lax.rsqrt(jnp.mean(x * x, axis=-1, keepdims=True) + EPS)
                return ((x * r) * gg).astype(BF16)

            hn_halo[...] = jnp.where(i > 0, norm(hp_ref[...]), jnp.zeros((pad, D), BF16))
            hn_ref[...] = norm(hc_ref[...])
            acc[...] = jnp.zeros_like(acc)

        w_refs = (wg_ref, wv_ref)
        taps = (_tap_rows(kg_ref), _tap_rows(kv_ref))
        hh = hn_halo[...]
        for h in range(2):
            halo[h] = jnp.dot(hh, w_refs[h][...], preferred_element_type=F32)[pad - FFN_PAD:]

        def up_part(lo, hi):
            a, b = lo * rc, hi * rc
            for h in range(2):
                ug_ref[h, a:b, :] = jnp.dot(hn_ref[a:b, :], w_refs[h][...], preferred_element_type=F32)

        def down_part(lo, hi):
            a, b = lo * rc, hi * rc
            acc[a:b, :] += jnp.dot(act_ref[a:b, :], wd_ref[...], preferred_element_type=F32)

        def chunk_rows(lo, hi):
            for c in range(lo, hi):
                r0 = c * rc
                convd = []
                for h in range(2):
                    if c == 0:
                        win = jnp.concatenate([halo[h], ug_ref[h, 0:rc]], axis=0)
                    else:
                        win = ug_ref[h, r0 - FFN_PAD:r0 + rc]
                    convd.append(_ffn_conv(win, taps[h], rc))
                gate, val = convd
                act_ref[r0:r0 + rc, :] = ((gate * _sigmoid(gate)) * val).astype(BF16)

        for p, (lo, hi) in enumerate(parts):
            if p == 0:
                up_part(lo, hi)
            if p + 1 < len(parts):
                up_part(*parts[p + 1])
            if p > 0:
                down_part(*parts[p - 1])
            chunk_rows(lo, hi)
        down_part(*parts[-1])

        @pl.when(kb == nj - 1)
        def _():
            out_ref[...] = acc[...] + hc_ref[...]

    out, xo = _call(
        body, name=name, grid=(nt, nj),
        in_specs=[pl.BlockSpec((pad, D), lambda i, k: (jnp.maximum(i * hb - 1, 0), 0)),
                  pl.BlockSpec((tl, D), lambda i, k: (i, 0)),
                  pl.BlockSpec((1, D), lambda i, k: (0, 0)),
                  pl.BlockSpec((D, tc), lambda i, k: (0, k)), pl.BlockSpec((D, tc), lambda i, k: (0, k + nj)),
                  pl.BlockSpec((kw, tc), lambda i, k: (0, k)), pl.BlockSpec((kw, tc), lambda i, k: (0, k + nj)),
                  pl.BlockSpec((tc, D), lambda i, k: (k, 0))],
        out_specs=(pl.BlockSpec((tl, D), lambda i, k: (i, 0)), pl.BlockSpec((tl, D), lambda i, k: (i, 0)),
                   pl.BlockSpec((2, tl, tc), lambda i, k: (0, i, k)), pl.BlockSpec((tl, tc), lambda i, k: (i, k))),
        out_shape=(jax.ShapeDtypeStruct((L, D), F32), jax.ShapeDtypeStruct((L, D), BF16),
                   jax.ShapeDtypeStruct((2, L, f), F32), jax.ShapeDtypeStruct((L, f), BF16)),
        scratch_shapes=[pltpu.VMEM((pad, D), BF16), pltpu.VMEM((2, FFN_PAD, tc), F32), pltpu.VMEM((tl, D), F32)],
        sem=("parallel", "arbitrary"), args=(h_mid, h_mid, g, wup, wup, kf, kf, wdown), xchg=xchg)
    return out if xchg is None else (out, xo)


def _ffn_block_bwd(dh, h_mid, g, ug0, kf, wdown_t, wup_t, name, xchg=None):
    L, D = dh.shape
    f = ug0.shape[2]
    kw = kf.shape[0]
    tl = _token_tile(L)
    tc = _divisor(f, 256, 128)
    nj = f // tc
    nt = L // tl
    pad = 2 * SUBLANES
    hb, nhb = tl // FFN_PAD, L // FFN_PAD
    rc = CONV3_ROWS
    nc = tl // rc
    parts = _row_parts(nc)

    def body(dhc_ref, dhn_ref, hm_ref, g_ref, gp_ref, gc_ref, gn_ref, vp_ref, vc_ref, vn_ref, kg_ref, kv_ref,
             wd_ref, wg_ref, wv_ref, dhm_ref, dg_ref, du_ref, dk_ref, dh_ext, dact_s, acc):
        i = pl.program_id(0)
        kb = pl.program_id(1)

        @pl.when(kb == 0)
        def _():
            dh_ext[0:tl, :] = dhc_ref[...].astype(BF16)
            dh_ext[tl:tl + pad, :] = dhn_ref[...].astype(BF16)
            acc[...] = jnp.zeros_like(acc)

        @pl.when(jnp.logical_and(i == 0, kb == 0))
        def _():
            dg_ref[...] = jnp.zeros_like(dg_ref)
            dk_ref[...] = jnp.zeros_like(dk_ref)

        prev = (jnp.where(i > 0, gp_ref[...], 0.0), jnp.where(i > 0, vp_ref[...], 0.0))
        x_refs, nxt = (gc_ref, vc_ref), (gn_ref, vn_ref)
        taps = (_tap_rows(kg_ref), _tap_rows(kv_ref))
        dk = [[jnp.zeros((SUBLANES, tc), F32) for _ in range(kw)] for _ in range(2)]

        def dact_part(lo, hi):
            a, b = lo * rc, hi * rc + pad
            dact_s[a:b, :] = jnp.dot(dh_ext[a:b, :], wd_ref[...], preferred_element_type=F32)

        def dhn_part(lo, hi):
            a, b = lo * rc, hi * rc
            acc[a:b, :] += (jnp.dot(du_ref[0, a:b, :], wg_ref[...], preferred_element_type=F32)
                            + jnp.dot(du_ref[1, a:b, :], wv_ref[...], preferred_element_type=F32))

        for p, (lo, hi) in enumerate(parts):
            if p == 0:
                dact_part(lo, hi)
            if p + 1 < len(parts):
                dact_part(*parts[p + 1])
            if p > 0:
                dhn_part(*parts[p - 1])
            chunk_rows(lo, hi, prev, x_refs, nxt, taps, dk, i, dact_s, du_ref)
        dhn_part(*parts[-1])
        for h in range(2):
            for j in range(kw):
                dk_ref[kb, h, j:j + 1, :] += jnp.sum(dk[h][j], axis=0, keepdims=True)

        @pl.when(kb == nj - 1)
        def _():
            x = hm_ref[...]
            r = lax.rsqrt(jnp.mean(x * x, axis=-1, keepdims=True) + EPS)
            xhat = x * r
            dhn = acc[...]
            dxhat = dhn * g_ref[...]
            dhm_ref[...] = dhc_ref[...] + r * (dxhat - xhat * jnp.mean(dxhat * xhat, axis=-1, keepdims=True))
            dg_ref[...] += jnp.sum(_rowsum8(dhn * xhat), axis=0, keepdims=True)

    def chunk_rows(lo, hi, prev, x_refs, nxt, taps, dk, i, dact_s, du_ref):
        for c in range(lo, hi):
            r0 = c * rc
            n = rc + FFN_PAD
            xs = []
            for h in range(2):
                parts = [prev[h] if c == 0 else x_refs[h][r0 - FFN_PAD:r0]]
                if c == nc - 1:
                    parts += [x_refs[h][r0:r0 + rc], nxt[h][...]]
                else:
                    parts += [x_refs[h][r0:r0 + n]]
                xs.append(jnp.concatenate(parts, axis=0))
            gate = _ffn_conv(xs[0], taps[0], n)
            val = _ffn_conv(xs[1], taps[1], n)
            dact = dact_s[r0:r0 + n, :]
            sg = _sigmoid(gate)
            dcs = [dact * val * (sg * (1.0 + gate * (1.0 - sg))), dact * (gate * sg)]
            if c == nc - 1:
                live = jnp.logical_or(lax.broadcasted_iota(jnp.int32, (n, 1), 0) < rc, i < nt - 1)
                dcs = [jnp.where(live, d, 0.0) for d in dcs]
            for h in range(2):
                xc = xs[h][FFN_PAD:FFN_PAD + rc]
                dx = None
                for j in range(kw):
                    o = kw - 1 - j
                    sh = dcs[h][o:o + rc]
                    term = _rows_of(taps[h][j], rc) * sh
                    dx = term if dx is None else dx + term
                    dk[h][j] = dk[h][j] + _rowsum8(xc * sh)
                du_ref[h, r0:r0 + rc, :] = dx.astype(BF16)

    def prev8(i, k):
        return (jnp.maximum(i * hb - 1, 0), k)

    def next8(i, k):
        return (jnp.minimum((i + 1) * hb, nhb - 1), k)

    def half(h, rows, idx):
        return pl.BlockSpec((None, rows, tc), lambda i, k: (h,) + idx(i, k))

    def tile(i, k):
        return (i, k)

    out, xo = _call(
        body, name=name, grid=(nt, nj),
        in_specs=[pl.BlockSpec((tl, D), lambda i, k: (i, 0)),
                  pl.BlockSpec((pad, D), lambda i, k: (jnp.minimum((i + 1) * (tl // pad), L // pad - 1), 0)),
                  pl.BlockSpec((tl, D), lambda i, k: (i, 0)), pl.BlockSpec((1, D), lambda i, k: (0, 0)),
                  half(0, FFN_PAD, prev8), half(0, tl, tile), half(0, FFN_PAD, next8),
                  half(1, FFN_PAD, prev8), half(1, tl, tile), half(1, FFN_PAD, next8),
                  pl.BlockSpec((kw, tc), lambda i, k: (0, k)), pl.BlockSpec((kw, tc), lambda i, k: (0, k + nj)),
                  pl.BlockSpec((D, tc), lambda i, k: (0, k)),
                  pl.BlockSpec((tc, D), lambda i, k: (k, 0)), pl.BlockSpec((tc, D), lambda i, k: (k + nj, 0))],
        out_specs=(pl.BlockSpec((tl, D), lambda i, k: (i, 0)), pl.BlockSpec((1, D), lambda i, k: (0, 0)),
                   pl.BlockSpec((2, tl, tc), lambda i, k: (0, i, k)),
                   pl.BlockSpec((nj, 2, kw, tc), lambda i, k: (0, 0, 0, 0))),
        out_shape=(jax.ShapeDtypeStruct((L, D), F32), jax.ShapeDtypeStruct((1, D), F32),
                   jax.ShapeDtypeStruct((2, L, f), BF16), jax.ShapeDtypeStruct((nj, 2, kw, tc), F32)),
        scratch_shapes=[pltpu.VMEM((tl + pad, D), BF16), pltpu.VMEM((tl + pad, tc), F32), pltpu.VMEM((tl, D), F32)],
        sem=("arbitrary", "arbitrary"), args=(dh, dh, h_mid, g, ug0, ug0, ug0, ug0, ug0, ug0, kf, kf, wdown_t, wup_t, wup_t),
        xchg=xchg)
    return out if xchg is None else (out, xo)


def _adamw_math(w, g, m, v):
    m = ADAM_B1 * m + (1.0 - ADAM_B1) * g
    v = ADAM_B2 * v + (1.0 - ADAM_B2) * (g * g)
    m_hat = m / (1.0 - ADAM_B1 ** ADAM_STEP)
    v_hat = v / (1.0 - ADAM_B2 ** ADAM_STEP)
    delta = -ADAM_LR * (m_hat / (jnp.sqrt(v_hat) + ADAM_EPS) + ADAM_WD * w)
    return delta, m, v


def _sum_parts(parts_ref, idx):
    g = parts_ref[(0,) + idx].astype(F32)
    for q in range(1, N_DEV):
        g = g + parts_ref[(q,) + idx].astype(F32)
    return g


def _adamw_big(parts, w, m, v, name):
    nl, R, C = w.shape
    tr = _divisor(R, 256, 2 * SUBLANES)

    def body(*refs):
        p_refs = refs[:nl]
        w_ref, m_ref, v_ref, g_ref, d_ref, nm_ref, nv_ref = refs[nl:]
        layer = pl.program_id(0)
        for k in range(nl):
            @pl.when(layer == k)
            def _(k=k):
                g = _sum_parts(p_refs[k], ())
                d, nm, nv = _adamw_math(w_ref[0], g, m_ref[0], v_ref[0])
                g_ref[0] = g
                d_ref[0] = d
                nm_ref[0] = nm
                nv_ref[0] = nv

    def part_spec(k):
        return pl.BlockSpec((N_DEV, tr, C), lambda l, r: (0, jnp.where(l == k, r, 0), 0))

    blk = pl.BlockSpec((1, tr, C), lambda l, r: (l, r, 0))
    shp = jax.ShapeDtypeStruct((nl, R, C), F32)
    return pl.pallas_call(
        body, name=name, grid=(nl, R // tr),
        in_specs=[part_spec(k) for k in range(nl)] + [blk, blk, blk],
        out_specs=(blk, blk, blk, blk), out_shape=(shp, shp, shp, shp),
        compiler_params=_params(("arbitrary", "arbitrary")),
    )(*parts, w, m, v)


def _adamw_small(entries, name):
    n = len(entries)
    uniq = []
    for e in entries:
        if not any(e[0] is u for u in uniq):
            uniq.append(e[0])
    pidx = [next(k for k, u in enumerate(uniq) if u is e[0]) for e in entries]
    npart = len(uniq)

    def body(*refs):
        p_refs = refs[:npart]
        wmv = refs[npart:npart + 3 * n]
        outs = refs[npart + 3 * n:]
        for t, e in enumerate(entries):
            lo, w = e[1], e[2]
            rows = w.shape[0]
            pr = p_refs[pidx[t]]
            g = pr[0, lo:lo + rows].astype(F32)
            for q in range(1, N_DEV):
                g = g + pr[q, lo:lo + rows].astype(F32)
            d, nm, nv = _adamw_math(wmv[3 * t][...], g, wmv[3 * t + 1][...], wmv[3 * t + 2][...])
            outs[4 * t][...] = g
            outs[4 * t + 1][...] = d
            outs[4 * t + 2][...] = nm
            outs[4 * t + 3][...] = nv

    vm = pl.BlockSpec(memory_space=pltpu.VMEM)
    args = list(uniq)
    out_shape = []
    for e in entries:
        args += [e[2], e[3], e[4]]
        out_shape += [jax.ShapeDtypeStruct(e[2].shape, F32)] * 4
    res = pl.pallas_call(
        body, name=name, in_specs=[vm] * len(args), out_specs=tuple([vm] * len(out_shape)),
        out_shape=tuple(out_shape), compiler_params=_params(),
    )(*args)
    return [tuple(res[4 * t:4 * t + 4]) for t in range(n)]


def _head_matrix(cc):
    bw = min(256, cc)
    r = lax.broadcasted_iota(jnp.int32, (bw, bw), 0) // HEAD_DIM
    c = lax.broadcasted_iota(jnp.int32, (bw, bw), 1) // HEAD_DIM
    return jnp.where(r == c, 1.0 / HEAD_DIM, 0.0).astype(BF16)


def _cols_from_shards(g):
    nd = g.ndim
    perm = tuple(range(1, nd - 1)) + (0, nd - 1)
    t = jnp.transpose(g, perm)
    return t.reshape(t.shape[:-2] + (t.shape[-2] * t.shape[-1],))


def _cols_to_shards(a):
    nd = a.ndim
    t = a.reshape(a.shape[:-1] + (N_DEV, a.shape[-1] // N_DEV))
    perm = (nd - 1,) + tuple(range(nd - 1)) + (nd,)
    return jnp.transpose(t, perm)


def kernel(x, meta_tokens, norm1_g, w_in, conv_dw_k, conv_dw_b, conv_ln_g, conv_ln_b, pool_w, pool_scale, w_out, norm2_g, w_up, ffn_dw_k, w_down, final_g, loss_target, m_meta_tokens, m_norm1_g, m_w_in, m_conv_dw_k, m_conv_dw_b, m_conv_ln_g, m_conv_ln_b, m_pool_w, m_pool_scale, m_w_out, m_norm2_g, m_w_up, m_ffn_dw_k, m_w_down, m_final_g, v_meta_tokens, v_norm1_g, v_w_in, v_conv_dw_k, v_conv_dw_b, v_conv_ln_g, v_conv_ln_b, v_pool_w, v_pool_scale, v_w_out, v_norm2_g, v_w_up, v_ffn_dw_k, v_w_down, v_final_g):
    depth, D = norm1_g.shape
    n_meta = meta_tokens.shape[0]
    seq = x.shape[1]
    L = n_meta + seq
    cc = conv_dw_b.shape[1]
    ng, gd = pool_w.shape[1], pool_w.shape[2]
    f = w_down.shape[1] * N_DEV

    def cols(g):
        w = _cols_from_shards(g)
        return w, w.T

    def rows(g):
        w = g.reshape(-1, g.shape[-1])
        return w, w.T

    b16 = lambda a: a.astype(BF16)
    (g_in0, g_out0, g_ck, g_kf, g_meta) = _exchange([b16(w_in[0]), b16(w_out[0]), conv_dw_k, ffn_dw_k, meta_tokens],
                                                    ["gather"] * 5, "gather_first")
    ck_full = _cols_from_shards(g_ck)
    ck_rows = jnp.broadcast_to(ck_full[:, :, None, :], ck_full.shape[:2] + (SUBLANES, cc))
    kf_full = _cols_from_shards(g_kf)
    meta_full = _cols_from_shards(g_meta)
    am = _head_matrix(cc)
    win, wout, wup, wdown = [None] * depth, [None] * depth, [None] * depth, [None] * depth
    win[0] = cols(g_in0)
    wout[0] = rows(g_out0)

    h = jnp.concatenate([meta_full, x[0]], axis=0)
    saved = []
    for l in range(depth):
        more = l + 1 < depth
        if l == 0:
            (z, hn1), (g_down,) = _norm_proj(h, norm1_g[l:l + 1], win[l][0], f"in_proj_{l}", tn_cap=768,
                                             xchg=([b16(w_down[l])], ["gather"]))
            wdown[l] = rows(g_down)
            (ymix, u1), (g_up,) = _mixer_fwd(z, ck_rows[l], conv_dw_b[l:l + 1], conv_ln_g[l:l + 1], conv_ln_b[l:l + 1],
                                             pool_w[l], pool_scale[l:l + 1], am, f"mixer_fwd_{l}",
                                             xchg=([b16(w_up[l])], ["gather"]))
            wup[l] = cols(g_up)
        else:
            z, hn1 = _norm_proj(h, norm1_g[l:l + 1], win[l][0], f"in_proj_{l}", tn_cap=768)
            ymix, u1 = _mixer_fwd(z, ck_rows[l], conv_dw_b[l:l + 1], conv_ln_g[l:l + 1], conv_ln_b[l:l + 1], pool_w[l],
                                  pool_scale[l:l + 1], am, f"mixer_fwd_{l}")
        if more:
            h_mid, (g_in,) = _mm(ymix, wout[l][0], f"out_proj_{l}", res=h, tn_cap=512, xchg=([b16(w_in[l + 1])], ["gather"]))
            win[l + 1] = cols(g_in)
            nxt = [b16(w_out[l + 1]), b16(w_up[l + 1]), b16(w_down[l + 1])]
            (h_out, hn2, ug0, act), got = _ffn_block_fwd(h_mid, norm2_g[l:l + 1], wup[l][0], kf_full[l], wdown[l][0],
                                                         f"ffn_fwd_{l}", xchg=(nxt, ["gather"] * 3))
            wout[l + 1], wup[l + 1], wdown[l + 1] = rows(got[0]), cols(got[1]), rows(got[2])
        else:
            h_mid = _mm(ymix, wout[l][0], f"out_proj_{l}", res=h, tn_cap=512)
            h_out, hn2, ug0, act = _ffn_block_fwd(h_mid, norm2_g[l:l + 1], wup[l][0], kf_full[l], wdown[l][0], f"ffn_fwd_{l}")
        saved.append((h, hn1, z, u1, ymix, h_mid, hn2, ug0, act))
        h = h_out

    tgt = jnp.concatenate([jnp.zeros((n_meta, D), F32), loss_target[0]], axis=0)
    dh, d_final_g, loss_part = _loss_head(h, final_g.reshape(1, D), tgt, n_meta, "loss_head")

    def row_shards(gm):
        return b16(gm.reshape(N_DEV, -1, gm.shape[-1]))

    def col_shards(gm):
        return b16(_cols_to_shards(gm))

    gw = {k: [None] * depth for k in ("ck", "cb", "lg", "lb", "pw", "ps", "kf", "n1", "n2")}
    parts = {k: [None] * depth for k in ("in", "out", "up", "down")}
    for l in reversed(range(depth)):
        h_in, hn1, z, u1, ymix, h_mid, hn2, ug0, act = saved[l]
        g_down = _mm_tn(act, dh, f"down_proj_wgrad_{l}", tq_cap=512)
        (dh_mid, gw["n2"][l], dug0, dkf), (parts["down"][l],) = _ffn_block_bwd(
            dh, h_mid, norm2_g[l:l + 1], ug0, kf_full[l], wdown[l][1], wup[l][1], f"ffn_bwd_{l}",
            xchg=([row_shards(g_down)], ["a2a"]))
        gw["kf"][l] = jnp.transpose(dkf, (2, 1, 0, 3)).reshape(dkf.shape[2], -1)
        g_up = _mm_tn(hn2, dug0, f"up_proj_wgrad_{l}", halves=2)
        dymix = _mm(dh_mid, wout[l][1], f"out_proj_bwd_{l}", tn_cap=512)
        g_out = _mm_tn(ymix, dh_mid, f"out_proj_wgrad_{l}", tq_cap=512)
        ((dz, gw["ck"][l], gw["cb"][l], gw["lg"][l], gw["lb"][l], gw["pw"][l], gw["ps"][l]),
         (parts["up"][l], parts["out"][l])) = _mixer_bwd(
            z, u1, dymix, ck_rows[l], conv_ln_g[l:l + 1], conv_ln_b[l:l + 1], pool_w[l], pool_scale[l:l + 1], am,
            f"mixer_bwd_{l}", xchg=([col_shards(g_up), row_shards(g_out)], ["a2a", "a2a"]))
        g_in = _mm_tn(hn1, dz, f"in_proj_wgrad_{l}", tq_cap=768)
        (dh, gw["n1"][l]), (parts["in"][l],) = _proj_bwd_norm(dz, win[l][1], h_in, norm1_g[l:l + 1], dh_mid,
                                                              f"in_proj_bwd_{l}", xchg=([col_shards(g_in)], ["a2a"]))
    grad_x = dh[n_meta:][None]
    d_meta = dh[:n_meta]

    zero_row = jnp.zeros((1, D), F32)
    pack_d = jnp.concatenate(gw["n1"] + gw["n2"] + [d_final_g, jnp.broadcast_to(loss_part[:, :1], (1, D)), zero_row, zero_row], axis=0)
    pack_c = jnp.concatenate(gw["cb"] + gw["lg"] + gw["lb"] + gw["ps"], axis=0)
    pack_pw = jnp.stack(gw["pw"]).reshape(depth * ng * gd, gd)
    src = [_cols_to_shards(jnp.stack(gw["ck"])), _cols_to_shards(jnp.stack(gw["kf"])), _cols_to_shards(d_meta),
           pack_d, pack_c, pack_pw]
    r_ck, r_kf, r_meta, r_d, r_c, r_pw = _exchange(src, ["a2a"] * 3 + ["gather"] * 3, "exchange_small_grads")

    big = {
        "w_in": _adamw_big(parts["in"], w_in, m_w_in, v_w_in, "adamw_w_in"),
        "w_out": _adamw_big(parts["out"], w_out, m_w_out, v_w_out, "adamw_w_out"),
        "w_up": _adamw_big(parts["up"], w_up, m_w_up, v_w_up, "adamw_w_up"),
        "w_down": _adamw_big(parts["down"], w_down, m_w_down, v_w_down, "adamw_w_down"),
    }
    kwid = conv_dw_k.shape[1]
    fkw = ffn_dw_k.shape[1]
    row = lambda a: a.reshape(1, -1)
    entries = [
        (r_d, 0, norm1_g, m_norm1_g, v_norm1_g),
        (r_d, depth, norm2_g, m_norm2_g, v_norm2_g),
        (r_d, 2 * depth, row(final_g), row(m_final_g), row(v_final_g)),
        (r_c, 0, conv_dw_b, m_conv_dw_b, v_conv_dw_b),
        (r_c, depth, conv_ln_g, m_conv_ln_g, v_conv_ln_g),
        (r_c, 2 * depth, conv_ln_b, m_conv_ln_b, v_conv_ln_b),
        (r_c, 3 * depth, pool_scale, m_pool_scale, v_pool_scale),
        (r_pw, 0, pool_w.reshape(-1, gd), m_pool_w.reshape(-1, gd), v_pool_w.reshape(-1, gd)),
        (r_ck.reshape(N_DEV, depth * kwid, -1), 0, conv_dw_k.reshape(depth * kwid, -1),
         m_conv_dw_k.reshape(depth * kwid, -1), v_conv_dw_k.reshape(depth * kwid, -1)),
        (r_kf.reshape(N_DEV, depth * fkw, -1), 0, ffn_dw_k.reshape(depth * fkw, -1),
         m_ffn_dw_k.reshape(depth * fkw, -1), v_ffn_dw_k.reshape(depth * fkw, -1)),
        (r_meta, 0, meta_tokens, m_meta_tokens, v_meta_tokens),
        (r_d, 2 * depth + 1, zero_row, zero_row, zero_row),
    ]
    small = _adamw_small(entries, "adamw_small")
    names = ["norm1_g", "norm2_g", "final_g", "conv_dw_b", "conv_ln_g", "conv_ln_b", "pool_scale", "pool_w",
             "conv_dw_k", "ffn_dw_k", "meta_tokens"]
    shapes = {"final_g": final_g.shape, "pool_w": pool_w.shape, "conv_dw_k": conv_dw_k.shape, "ffn_dw_k": ffn_dw_k.shape}
    res = dict(big)
    for nme, quad in zip(names, small[:-1]):
        res[nme] = tuple(a.reshape(shapes[nme]) if nme in shapes else a for a in quad)
    loss = small[-1][0][0, 0]

    order = ["meta_tokens", "norm1_g", "w_in", "conv_dw_k", "conv_dw_b", "conv_ln_g", "conv_ln_b", "pool_w", "pool_scale",
             "w_out", "norm2_g", "w_up", "ffn_dw_k", "w_down", "final_g"]
    return (loss, grad_x, *[res[k][0] for k in order], *[res[k][1] for k in order], *[res[k][2] for k in order],
            *[res[k][3] for k in order])
```

```python
import functools

import jax
import jax.numpy as jnp
from jax import lax
from jax.experimental import pallas as pl
from jax.experimental.pallas import tpu as pltpu

F32 = jnp.float32
BF16 = jnp.bfloat16

EPS = 1e-6
HEAD_DIM = 64
POOL_WINDOWS = (2, 4, 8, 16)
ADAM_LR = 0.001
ADAM_B1 = 0.9
ADAM_B2 = 0.999
ADAM_EPS = 1e-08
ADAM_WD = 0.01
ADAM_STEP = 10

N_DEV = 8
OTHER_CHIPS = (2, 4, 6)
SUBLANES = 8
HALO = 48
CONV_PAD = 32
POOL_PAD = 16
FFN_PAD = 8
ROW_CHUNK = 24
CONV3_ROWS = 48
MAX_TILE_ROWS = 1024
WGRAD_TILE_ROWS = 2816
VMEM_LIMIT = 52 * 1024 * 1024


def _divisor(n, cap, mult):
    best = None
    for d in range(mult, min(n, cap) + 1, mult):
        if n % d == 0:
            best = d
    return n if best is None else best


def _token_tile(L):
    return _divisor(L, MAX_TILE_ROWS, HALO)


def _row_tile(L):
    return _divisor(L, 320, 2 * SUBLANES)


def _stat_rows(tl):
    return _divisor(tl, 256, SUBLANES)


def _params(sem=None):
    return pltpu.CompilerParams(dimension_semantics=sem, vmem_limit_bytes=VMEM_LIMIT)


def _rowsum8(x):
    acc = x[0:SUBLANES]
    for k in range(1, x.shape[0] // SUBLANES):
        acc = acc + x[k * SUBLANES:(k + 1) * SUBLANES]
    return acc


def _sigmoid(x):
    return jax.nn.sigmoid(x)


def _dot_nt(a, b):
    return lax.dot_general(a, b, (((1,), (1,)), ((), ())), preferred_element_type=F32)


def _head_mean(x, am_ref):
    bw = am_ref.shape[0]
    am = am_ref[...]
    outs = []
    for blk in range(x.shape[1] // bw):
        xb = x[:, blk * bw:(blk + 1) * bw]
        hi = xb.astype(BF16)
        lo = (xb - hi.astype(F32)).astype(BF16)
        outs.append(jnp.dot(hi, am, preferred_element_type=F32) + jnp.dot(lo, am, preferred_element_type=F32))
    return outs[0] if len(outs) == 1 else jnp.concatenate(outs, axis=-1)


def _xchg_out_shapes(srcs, modes):
    out = []
    for s, m in zip(srcs, modes):
        shp = ((N_DEV,) + tuple(s.shape)) if m == "gather" else tuple(s.shape)
        out.append(jax.ShapeDtypeStruct(shp, s.dtype))
    return out


def _xchg_sems(n):
    return [pltpu.SemaphoreType.DMA((n, N_DEV - 1)), pltpu.SemaphoreType.DMA((n, N_DEV - 1)), pltpu.SemaphoreType.DMA((n,))]


def _xchg_ops(src_refs, out_refs, sems, modes):
    n = len(src_refs)
    send_sems, recv_sems, local_sems = sems
    x, y, c = lax.axis_index("x"), lax.axis_index("y"), lax.axis_index("c")
    me = 4 * x + 2 * y + c

    def peer(d):
        return (x ^ ((d >> 2) & 1), y ^ ((d >> 1) & 1), c ^ (d & 1))

    def peer_id(d):
        px, py, pc = peer(d)
        return 4 * px + 2 * py + pc

    def remote(t, d):
        src = src_refs[t] if modes[t] == "gather" else src_refs[t].at[peer_id(d)]
        return pltpu.make_async_remote_copy(
            src_ref=src, dst_ref=out_refs[t].at[me], send_sem=send_sems.at[t, d - 1], recv_sem=recv_sems.at[t, d - 1],
            device_id=peer(d), device_id_type=pl.DeviceIdType.MESH)

    def arrival(t, d):
        src = src_refs[t] if modes[t] == "gather" else src_refs[t].at[me]
        return pltpu.make_async_remote_copy(
            src_ref=src, dst_ref=out_refs[t].at[peer_id(d)], send_sem=send_sems.at[t, d - 1],
            recv_sem=recv_sems.at[t, d - 1], device_id=peer(d), device_id_type=pl.DeviceIdType.MESH)

    def passed_on(t, d):
        blk = out_refs[t].at[peer_id(d)]
        return pltpu.make_async_remote_copy(
            src_ref=blk, dst_ref=blk, send_sem=send_sems.at[t, d], recv_sem=recv_sems.at[t, d],
            device_id=peer(1), device_id_type=pl.DeviceIdType.MESH)

    def local(t):
        src = src_refs[t] if modes[t] == "gather" else src_refs[t].at[me]
        return pltpu.make_async_copy(src, out_refs[t].at[me], local_sems.at[t])

    def sent_first(t):
        return OTHER_CHIPS + (1,) if modes[t] == "gather" else tuple(range(1, N_DEV))

    def start():
        for t in range(n):
            local(t).start()
        for t in range(n):
            for d in sent_first(t):
                remote(t, d).start()

    def wait():
        gathered = [t for t in range(n) if modes[t] == "gather"]
        for t in gathered:
            for d in OTHER_CHIPS:
                arrival(t, d).wait_recv()
                passed_on(t, d).start()
        for t in range(n):
            for d in range(1, N_DEV):
                if not (modes[t] == "gather" and d in OTHER_CHIPS):
                    arrival(t, d).wait_recv()
        for t in range(n):
            for d in sent_first(t):
                remote(t, d).wait_send()
        for t in gathered:
            for d in OTHER_CHIPS:
                passed_on(t, d).wait_send()
        for t in range(n):
            local(t).wait()

    return start, wait


def _exchange(srcs, modes, name):
    n = len(srcs)

    def body(*refs):
        start, wait = _xchg_ops(refs[:n], refs[n:2 * n], refs[2 * n:], modes)
        start()
        wait()

    any_spec = pl.BlockSpec(memory_space=pl.ANY)
    return pl.pallas_call(
        body, name=name, out_shape=tuple(_xchg_out_shapes(srcs, modes)),
        in_specs=[any_spec] * n, out_specs=tuple([any_spec] * n),
        scratch_shapes=_xchg_sems(n),
        compiler_params=pltpu.CompilerParams(has_side_effects=True),
    )(*srcs)


def _call(body, *, name, grid, in_specs, out_specs, out_shape, args, scratch_shapes=(), sem=None, xchg=None):
    single = not isinstance(out_shape, (tuple, list))
    outs_shape = [out_shape] if single else list(out_shape)
    outs_spec = [out_specs] if single else list(out_specs)
    if xchg is None:
        res = pl.pallas_call(
            body, name=name, grid=grid, in_specs=list(in_specs), out_specs=out_specs, out_shape=out_shape,
            scratch_shapes=list(scratch_shapes), compiler_params=_params(sem))(*args)
        return res, ()
    srcs, modes = xchg
    n_in, n_out, n_scr, nx = len(in_specs), len(outs_shape), len(scratch_shapes), len(srcs)

    def wrapped(*refs):
        ins = refs[:n_in]
        xs = refs[n_in:n_in + nx]
        o0 = n_in + nx
        outs = refs[o0:o0 + n_out]
        xo = refs[o0 + n_out:o0 + n_out + nx]
        s0 = o0 + n_out + nx
        scr = refs[s0:s0 + n_scr]
        start, wait = _xchg_ops(xs, xo, refs[s0 + n_scr:], modes)
        first = functools.reduce(jnp.logical_and, [pl.program_id(a) == 0 for a in range(len(grid))])
        last = functools.reduce(jnp.logical_and, [pl.program_id(a) == grid[a] - 1 for a in range(len(grid))])

        @pl.when(first)
        def _():
            start()

        body(*ins, *outs, *scr)

        @pl.when(last)
        def _():
            wait()

    any_spec = pl.BlockSpec(memory_space=pl.ANY)
    res = pl.pallas_call(
        wrapped, name=name, grid=grid, in_specs=list(in_specs) + [any_spec] * nx,
        out_specs=tuple(outs_spec + [any_spec] * nx), out_shape=tuple(outs_shape + _xchg_out_shapes(srcs, modes)),
        scratch_shapes=list(scratch_shapes) + _xchg_sems(nx),
        compiler_params=_params(("arbitrary",) * len(grid)))(*args, *srcs)
    comp = res[:n_out]
    return (comp[0] if single else tuple(comp)), tuple(res[n_out:])


def _norm_proj(h, g, w, name, *, tn_cap, xchg=None):
    L, D = h.shape
    N = w.shape[1]
    tm = _token_tile(L)
    tn = _divisor(N, tn_cap, 128)

    def body(h_ref, g_ref, w_ref, z_ref, hn_ref):
        @pl.when(pl.program_id(1) == 0)
        def _():
            x = h_ref[...]
            r = lax.rsqrt(jnp.mean(x * x, axis=-1, keepdims=True) + EPS)
            hn_ref[...] = ((x * r) * g_ref[...]).astype(BF16)

        z_ref[...] = jnp.dot(hn_ref[...], w_ref[...], preferred_element_type=F32)

    out, xo = _call(
        body, name=name, grid=(L // tm, N // tn),
        in_specs=[pl.BlockSpec((tm, D), lambda i, j: (i, 0)), pl.BlockSpec((1, D), lambda i, j: (0, 0)),
                  pl.BlockSpec((D, tn), lambda i, j: (0, j))],
        out_specs=(pl.BlockSpec((tm, tn), lambda i, j: (i, j)), pl.BlockSpec((tm, D), lambda i, j: (i, 0))),
        out_shape=(jax.ShapeDtypeStruct((L, N), F32), jax.ShapeDtypeStruct((L, D), BF16)),
        sem=("parallel", "arbitrary"), args=(h, g, w), xchg=xchg)
    return out if xchg is None else (out, xo)


def _proj_bwd_norm(a, b, h, g, dres, name, xchg=None):
    L, K = a.shape
    D = b.shape[0]
    tm = _token_tile(L)

    def body(a_ref, b_ref, h_ref, g_ref, dres_ref, dh_ref, dg_ref):
        i = pl.program_id(0)
        dhn = _dot_nt(a_ref[...], b_ref[...])
        x = h_ref[...]
        r = lax.rsqrt(jnp.mean(x * x, axis=-1, keepdims=True) + EPS)
        xhat = x * r
        dxhat = dhn * g_ref[...]
        dh_ref[...] = dres_ref[...] + r * (dxhat - xhat * jnp.mean(dxhat * xhat, axis=-1, keepdims=True))
        part = jnp.sum(_rowsum8(dhn * xhat), axis=0, keepdims=True)

        @pl.when(i == 0)
        def _():
            dg_ref[...] = part

        @pl.when(i > 0)
        def _():
            dg_ref[...] += part

    tile = pl.BlockSpec((tm, D), lambda i: (i, 0))
    row = pl.BlockSpec((1, D), lambda i: (0, 0))
    out, xo = _call(
        body, name=name, grid=(L // tm,),
        in_specs=[pl.BlockSpec((tm, K), lambda i: (i, 0)), pl.BlockSpec((D, K), lambda i: (0, 0)), tile, row, tile],
        out_specs=(tile, row), out_shape=(jax.ShapeDtypeStruct((L, D), F32), jax.ShapeDtypeStruct((1, D), F32)),
        sem=("arbitrary",), args=(a, b, h, g, dres), xchg=xchg)
    return out if xchg is None else (out, xo)


def _mm(a, b, name, *, res=None, b_t=False, tn_cap=1408, xchg=None):
    M, K = a.shape
    N = b.shape[0] if b_t else b.shape[1]
    tm = _token_tile(M)
    tn = _divisor(N, tn_cap, 128)

    def body(*refs):
        a_ref, b_ref = refs[:2]
        r_ref, o_ref = (None, refs[2]) if res is None else (refs[2], refs[3])
        av = a_ref[...].astype(BF16)
        prod = _dot_nt(av, b_ref[...]) if b_t else jnp.dot(av, b_ref[...], preferred_element_type=F32)
        o_ref[...] = prod if r_ref is None else prod + r_ref[...]

    b_spec = pl.BlockSpec((tn, K), lambda i, j: (j, 0)) if b_t else pl.BlockSpec((K, tn), lambda i, j: (0, j))
    in_specs = [pl.BlockSpec((tm, K), lambda i, j: (i, 0)), b_spec]
    args = [a, b]
    if res is not None:
        in_specs.append(pl.BlockSpec((tm, tn), lambda i, j: (i, j)))
        args.append(res)
    out, xo = _call(
        body, name=name, grid=(M // tm, N // tn), in_specs=in_specs,
        out_specs=pl.BlockSpec((tm, tn), lambda i, j: (i, j)), out_shape=jax.ShapeDtypeStruct((M, N), F32),
        sem=("parallel", "parallel"), args=args, xchg=xchg)
    return out if xchg is None else (out, xo)


def _mm_tn(a, b, name, *, halves=1, tq_cap=1408):
    L, P = a.shape
    if halves > 1:
        qh = b.shape[2]
        Q = qh * halves
    else:
        Q = b.shape[1]
        qh = Q
    tl = _divisor(L, WGRAD_TILE_ROWS, HALO)
    tp = _divisor(P, 1408, 128)
    tq = _divisor(qh, tq_cap, 128)
    qper = qh // tq
    grid = (P // tp, Q // tq, L // tl)

    def body(a_ref, b_ref, o_ref):
        bv = b_ref[0] if halves > 1 else b_ref[...]
        prod = lax.dot_general(a_ref[...].astype(BF16), bv.astype(BF16), (((0,), (0,)), ((), ())),
                               preferred_element_type=F32)
        l = pl.program_id(2)

        @pl.when(l == 0)
        def _():
            o_ref[...] = prod

        @pl.when(l > 0)
        def _():
            o_ref[...] += prod

    if halves > 1:
        b_spec = pl.BlockSpec((1, tl, tq), lambda p, q, l: (q // qper, l, q % qper))
    else:
        b_spec = pl.BlockSpec((tl, tq), lambda p, q, l: (l, q))
    return pl.pallas_call(
        body, name=name, grid=grid,
        in_specs=[pl.BlockSpec((tl, tp), lambda p, q, l: (l, p)), b_spec],
        out_specs=pl.BlockSpec((tp, tq), lambda p, q, l: (p, q)),
        out_shape=jax.ShapeDtypeStruct((P, Q), F32),
        compiler_params=_params(("parallel", "parallel", "arbitrary")),
    )(a, b)


def _loss_head(h, g, tgt, n_meta, name):
    L, D = h.shape
    tl = _row_tile(L)
    nt = L // tl

    def body(h_ref, g_ref, t_ref, dh_ref, dg_ref, loss_ref):
        i = pl.program_id(0)
        x = h_ref[...]
        r = lax.rsqrt(jnp.mean(x * x, axis=-1, keepdims=True) + EPS)
        xhat = x * r
        gg = g_ref[...]
        y = xhat * gg
        rows = i * tl + lax.broadcasted_iota(jnp.int32, (tl, 1), 0)
        err = jnp.where(rows >= n_meta, y - t_ref[...], 0.0)
        dy = err * (1.0 / D)
        dxhat = dy * gg
        dh_ref[...] = r * (dxhat - xhat * jnp.mean(dxhat * xhat, axis=-1, keepdims=True))
        dg_part = jnp.sum(_rowsum8(dy * xhat), axis=0, keepdims=True)
        per_row = jnp.mean(err * err, axis=-1, keepdims=True)
        loss_part = jnp.broadcast_to(0.5 * jnp.sum(per_row, axis=0, keepdims=True), (1, 128))

        @pl.when(i == 0)
        def _():
            dg_ref[...] = dg_part
            loss_ref[...] = loss_part

        @pl.when(i > 0)
        def _():
            dg_ref[...] += dg_part
            loss_ref[...] += loss_part

    tile = pl.BlockSpec((tl, D), lambda i: (i, 0))
    row = pl.BlockSpec((1, D), lambda i: (0, 0))
    return pl.pallas_call(
        body, name=name, grid=(nt,), in_specs=[tile, row, tile],
        out_specs=(tile, row, pl.BlockSpec((1, 128), lambda i: (0, 0))),
        out_shape=(jax.ShapeDtypeStruct((L, D), F32), jax.ShapeDtypeStruct((1, D), F32),
                   jax.ShapeDtypeStruct((1, 128), F32)),
        compiler_params=_params(("arbitrary",)),
    )(h, g, tgt)


def _pool_fwd_block(pwin, pw_ref, row0, rb, g, gd, w, t0):
    wv = pwin[pl.ds(row0 + HALO - POOL_PAD, rb + POOL_PAD), g * gd:(g + 1) * gd]
    s = wv
    sh = 1
    while sh < w:
        s = s + pltpu.roll(s, sh, axis=0)
        sh *= 2
    win = s[POOL_PAD:POOL_PAD + rb]
    pt = wv[POOL_PAD:POOL_PAD + rb]
    tg = t0 + lax.broadcasted_iota(jnp.int32, (rb, 1), 0)
    cnt = jnp.minimum(tg + 1, w).astype(F32)
    return win / cnt - pt


def _fill_windows(i, zp_ref, zc_ref, u0w, pwin, tl, cc):
    keep = i > 0
    zp = zp_ref[...]
    u0w[0:HALO, :] = jnp.where(keep, zp[:, :cc] * _sigmoid(zp[:, cc:2 * cc]), 0.0)
    pwin[0:HALO, :] = jnp.where(keep, zp[:, 2 * cc:], 0.0)

    def fill(c, carry):
        b = pl.multiple_of(c * ROW_CHUNK, SUBLANES)
        zc = zc_ref[pl.ds(b, ROW_CHUNK), :]
        u0w[pl.ds(HALO + b, ROW_CHUNK), :] = zc[:, :cc] * _sigmoid(zc[:, cc:2 * cc])
        pwin[pl.ds(HALO + b, ROW_CHUNK), :] = zc[:, 2 * cc:]
        return carry

    lax.fori_loop(0, tl // ROW_CHUNK, fill, 0)


def _mixer_fwd(z, ck, cb, lg, lb, pw, ps, am, name, xchg=None):
    L, ci = z.shape
    kw, _, cc = ck.shape
    cp = ci - 2 * cc
    ng, gd = pw.shape[0], pw.shape[1]
    tl = _token_tile(L)
    nt = L // tl
    hb = tl // HALO
    rb = _stat_rows(tl)
    tap0 = CONV_PAD - (kw - 1)

    def body(zp_ref, zc_ref, ck_ref, cb_ref, lg_ref, lb_ref, pw_ref, ps_ref, am_ref, y_ref, u1_ref, u0w, pwin):
        i = pl.program_id(0)
        _fill_windows(i, zp_ref, zc_ref, u0w, pwin, tl, cc)

        def conv(c, carry):
            b = pl.multiple_of(c * ROW_CHUNK, SUBLANES)
            w = u0w[pl.ds(b + HALO - CONV_PAD, ROW_CHUNK + CONV_PAD), :]
            acc = jnp.broadcast_to(cb_ref[...], (ROW_CHUNK, cc))
            for j in range(kw):
                acc = acc + _rows_of(ck_ref[j], ROW_CHUNK) * w[tap0 + j:tap0 + j + ROW_CHUNK]
            u1_ref[pl.ds(b, ROW_CHUNK), :] = acc
            return carry

        lax.fori_loop(0, tl // ROW_CHUNK, conv, 0)

        def blocks(k, carry):
            b = pl.multiple_of(k * rb, SUBLANES)
            u1 = u1_ref[pl.ds(b, rb), :]
            xc = u1 - _head_mean(u1, am_ref)
            var = _head_mean(xc * xc, am_ref)
            u2 = (xc * lax.rsqrt(var + EPS)) * lg_ref[...] + lb_ref[...]
            y_ref[pl.ds(b, rb), 0:cc] = (u2 * _sigmoid(u2)).astype(y_ref.dtype)
            for g in range(ng):
                d = _pool_fwd_block(pwin, pw_ref, b, rb, g, gd, POOL_WINDOWS[g], i * tl + b)
                yp = jnp.dot(d.astype(BF16), pw_ref[g].astype(BF16), preferred_element_type=F32)
                yp = yp * ps_ref[:, g * gd:(g + 1) * gd]
                y_ref[pl.ds(b, rb), cc + g * gd:cc + (g + 1) * gd] = yp.astype(y_ref.dtype)
            return carry

        lax.fori_loop(0, tl // rb, blocks, 0)

    def full(a):
        nd = a.ndim
        return pl.BlockSpec(a.shape, lambda i: (0,) * nd)

    out, xo = _call(
        body, name=name, grid=(nt,),
        in_specs=[pl.BlockSpec((HALO, ci), lambda i: (jnp.maximum(i * hb - 1, 0), 0)),
                  pl.BlockSpec((tl, ci), lambda i: (i, 0)),
                  full(ck), full(cb), full(lg), full(lb), full(pw), full(ps), full(am)],
        out_specs=(pl.BlockSpec((tl, cc + cp), lambda i: (i, 0)), pl.BlockSpec((tl, cc), lambda i: (i, 0))),
        out_shape=(jax.ShapeDtypeStruct((L, cc + cp), BF16), jax.ShapeDtypeStruct((L, cc), F32)),
        scratch_shapes=[pltpu.VMEM((HALO + tl, cc), F32), pltpu.VMEM((HALO + tl, cp), F32)],
        sem=("parallel",), args=(z, z, ck, cb, lg, lb, pw, ps, am), xchg=xchg)
    return out if xchg is None else (out, xo)


def _mixer_bwd(z, u1, dy, ck, lg, lb, pw, ps, am, name, xchg=None):
    L, ci = z.shape
    kw, _, cc = ck.shape
    cp = ci - 2 * cc
    ng, gd = pw.shape[0], pw.shape[1]
    tl = _token_tile(L)
    nt = L // tl
    hb = tl // HALO
    rb = _stat_rows(tl)

    def body(zp_ref, zc_ref, u1c_ref, u1n_ref, dyc_ref, dyn_ref, ck_ref, lg_ref, lb_ref, pw_ref, ps_ref, am_ref,
             dz_ref, dck_ref, dcb_ref, dlg_ref, dlb_ref, dpw_ref, dps_ref,
             u0w, pwin, du1w, ddw, ew, dkacc, dcb8, dlg8, dlb8, dps8):
        i = pl.program_id(0)
        has_next = i < nt - 1

        @pl.when(i == 0)
        def _():
            dck_ref[...] = jnp.zeros_like(dck_ref)
            dcb_ref[...] = jnp.zeros_like(dcb_ref)
            dlg_ref[...] = jnp.zeros_like(dlg_ref)
            dlb_ref[...] = jnp.zeros_like(dlb_ref)
            dpw_ref[...] = jnp.zeros_like(dpw_ref)
            dps_ref[...] = jnp.zeros_like(dps_ref)

        dkacc[...] = jnp.zeros_like(dkacc)
        dcb8[...] = jnp.zeros_like(dcb8)
        dlg8[...] = jnp.zeros_like(dlg8)
        dlb8[...] = jnp.zeros_like(dlb8)
        dps8[...] = jnp.zeros_like(dps8)

        _fill_windows(i, zp_ref, zc_ref, u0w, pwin, tl, cc)

        def conv_side(u1, dyc, own):
            xc = u1 - _head_mean(u1, am_ref)
            rstd = lax.rsqrt(_head_mean(xc * xc, am_ref) + EPS)
            uh = xc * rstd
            lgv = lg_ref[...]
            u2 = uh * lgv + lb_ref[...]
            sg = _sigmoid(u2)
            du2 = dyc * (sg * (1.0 + u2 * (1.0 - sg)))
            if own:
                dlg8[...] += _rowsum8(du2 * uh)
                dlb8[...] += _rowsum8(du2)
            duh = du2 * lgv
            return rstd * (duh - _head_mean(duh, am_ref) - uh * _head_mean(duh * uh, am_ref))

        def pool_side(dyp, t0, rows):
            dds, es = [], []
            tg = t0 + lax.broadcasted_iota(jnp.int32, (rows, 1), 0)
            for g in range(ng):
                dypre = dyp[:, g * gd:(g + 1) * gd] * ps_ref[:, g * gd:(g + 1) * gd]
                dd = lax.dot_general(dypre.astype(BF16), pw_ref[g].astype(BF16), (((1,), (1,)), ((), ())),
                                     preferred_element_type=F32)
                cnt = jnp.minimum(tg + 1, POOL_WINDOWS[g]).astype(F32)
                dds.append(dd)
                es.append(dd / cnt)
            return jnp.concatenate(dds, axis=-1), jnp.concatenate(es, axis=-1)

        def blocks(k, carry):
            b = pl.multiple_of(k * rb, SUBLANES)
            dyb = dyc_ref[pl.ds(b, rb), :]
            du1 = conv_side(u1c_ref[pl.ds(b, rb), :], dyb[:, :cc], True)
            du1w[pl.ds(b, rb), :] = du1
            dcb8[...] += _rowsum8(du1)
            dyp = dyb[:, cc:]
            dd, e = pool_side(dyp, i * tl + b, rb)
            ddw[pl.ds(b, rb), :] = dd
            ew[pl.ds(b, rb), :] = e
            for g in range(ng):
                d = _pool_fwd_block(pwin, pw_ref, b, rb, g, gd, POOL_WINDOWS[g], i * tl + b)
                db16 = d.astype(BF16)
                dypg = dyp[:, g * gd:(g + 1) * gd]
                ypre = jnp.dot(db16, pw_ref[g].astype(BF16), preferred_element_type=F32)
                dps8[:, g * gd:(g + 1) * gd] += _rowsum8(dypg * ypre)
                dypre = (dypg * ps_ref[:, g * gd:(g + 1) * gd]).astype(BF16)
                dpw_ref[g] += lax.dot_general(db16, dypre, (((0,), (0,)), ((), ())), preferred_element_type=F32)
            return carry

        lax.fori_loop(0, tl // rb, blocks, 0)

        dyn = dyn_ref[...]
        du1n = conv_side(u1n_ref[...], dyn[:, :cc], False)
        du1w[tl:tl + HALO, :] = jnp.where(has_next, du1n, 0.0)
        ddn, en = pool_side(dyn[:, cc:], (i + 1) * tl, HALO)
        ew[tl:tl + HALO, :] = jnp.where(has_next, en, 0.0)

        def taps(c, carry):
            b = pl.multiple_of(c * ROW_CHUNK, SUBLANES)
            w = du1w[pl.ds(b, ROW_CHUNK + CONV_PAD), :]
            u0c = u0w[pl.ds(HALO + b, ROW_CHUNK), :]
            acc = jnp.zeros((ROW_CHUNK, cc), F32)
            for j in range(kw):
                o = kw - 1 - j
                sh = w[o:o + ROW_CHUNK]
                acc = acc + _rows_of(ck_ref[j], ROW_CHUNK) * sh
                dkacc[j] += _rowsum8(u0c * sh)
            zc = zc_ref[pl.ds(b, ROW_CHUNK), :]
            a = zc[:, :cc]
            sg = _sigmoid(zc[:, cc:2 * cc])
            dz_ref[pl.ds(b, ROW_CHUNK), 0:cc] = (acc * sg).astype(dz_ref.dtype)
            dz_ref[pl.ds(b, ROW_CHUNK), cc:2 * cc] = (acc * a * sg * (1.0 - sg)).astype(dz_ref.dtype)
            return carry

        lax.fori_loop(0, tl // ROW_CHUNK, taps, 0)

        def pool_back(k, carry):
            b = pl.multiple_of(k * rb, SUBLANES)
            n = rb + POOL_PAD
            for g in range(ng):
                s = ew[pl.ds(b, n), g * gd:(g + 1) * gd]
                sh = 1
                while sh < POOL_WINDOWS[g]:
                    s = s + pltpu.roll(s, n - sh, axis=0)
                    sh *= 2
                dp = s[0:rb] - ddw[pl.ds(b, rb), g * gd:(g + 1) * gd]
                dz_ref[pl.ds(b, rb), 2 * cc + g * gd:2 * cc + (g + 1) * gd] = dp.astype(dz_ref.dtype)
            return carry

        lax.fori_loop(0, tl // rb, pool_back, 0)

        dck_ref[...] += jnp.sum(dkacc[...], axis=1)
        dcb_ref[...] += jnp.sum(dcb8[...], axis=0, keepdims=True)
        dlg_ref[...] += jnp.sum(dlg8[...], axis=0, keepdims=True)
        dlb_ref[...] += jnp.sum(dlb8[...], axis=0, keepdims=True)
        dps_ref[...] += jnp.sum(dps8[...], axis=0, keepdims=True)

    def full(a):
        nd = a.ndim
        return pl.BlockSpec(a.shape, lambda i: (0,) * nd)

    nhb = L // HALO

    def prev_map(i):
        return (jnp.maximum(i * hb - 1, 0), 0)

    def next_map(i):
        return (jnp.minimum((i + 1) * hb, nhb - 1), 0)

    dcc = cc + cp
    row_cc = jax.ShapeDtypeStruct((1, cc), F32)
    out_shape = (jax.ShapeDtypeStruct((L, ci), BF16), jax.ShapeDtypeStruct((kw, cc), F32), row_cc, row_cc, row_cc,
                 jax.ShapeDtypeStruct((ng, gd, gd), F32), jax.ShapeDtypeStruct((1, cp), F32))
    acc_spec = [pl.BlockSpec((kw, cc), lambda i: (0, 0))] + [pl.BlockSpec((1, cc), lambda i: (0, 0))] * 3 + [
        pl.BlockSpec((ng, gd, gd), lambda i: (0, 0, 0)), pl.BlockSpec((1, cp), lambda i: (0, 0))]
    out, xo = _call(
        body, name=name, grid=(nt,),
        in_specs=[pl.BlockSpec((HALO, ci), prev_map), pl.BlockSpec((tl, ci), lambda i: (i, 0)),
                  pl.BlockSpec((tl, cc), lambda i: (i, 0)), pl.BlockSpec((HALO, cc), next_map),
                  pl.BlockSpec((tl, dcc), lambda i: (i, 0)), pl.BlockSpec((HALO, dcc), next_map),
                  full(ck), full(lg), full(lb), full(pw), full(ps), full(am)],
        out_specs=tuple([pl.BlockSpec((tl, ci), lambda i: (i, 0))] + acc_spec),
        out_shape=out_shape,
        scratch_shapes=[pltpu.VMEM((HALO + tl, cc), F32), pltpu.VMEM((HALO + tl, cp), F32),
                        pltpu.VMEM((tl + HALO, cc), F32), pltpu.VMEM((tl, cp), F32), pltpu.VMEM((tl + HALO, cp), F32),
                        pltpu.VMEM((kw, SUBLANES, cc), F32), pltpu.VMEM((SUBLANES, cc), F32),
                        pltpu.VMEM((SUBLANES, cc), F32), pltpu.VMEM((SUBLANES, cc), F32), pltpu.VMEM((SUBLANES, cp), F32)],
        sem=("arbitrary",), args=(z, z, u1, u1, dy, dy, ck, lg, lb, pw, ps, am), xchg=xchg)
    return out if xchg is None else (out, xo)


def _row_parts(nc, n=3):
    n = min(n, nc)
    cuts = [round(k * nc / n) for k in range(n + 1)]
    return [(cuts[k], cuts[k + 1]) for k in range(n)]


def _tap_rows(k_ref):
    return [jnp.broadcast_to(k_ref[j:j + 1, :], (SUBLANES, k_ref.shape[1])) for j in range(k_ref.shape[0])]


def _rows_of(tap, n):
    return tap if n == SUBLANES else jnp.concatenate([tap] * (n // SUBLANES), axis=0)


def _ffn_conv(win, taps, rows):
    kw = len(taps)
    o = FFN_PAD - (kw - 1)
    acc = _rows_of(taps[0], rows) * win[o:o + rows]
    for j in range(1, kw):
        acc = acc + _rows_of(taps[j], rows) * win[o + j:o + j + rows]
    return acc


def _ffn_block_fwd(h_mid, g, wup, kf, wdown, name, xchg=None):
    L, D = h_mid.shape
    f = wdown.shape[0]
    kw = kf.shape[0]
    tl = _token_tile(L)
    tc = _divisor(f, 256, 128)
    nj = f // tc
    nt = L // tl
    pad = 2 * SUBLANES
    hb = tl // pad
    rc = CONV3_ROWS
    parts = _row_parts(tl // rc)

    def body(hp_ref, hc_ref, g_ref, wg_ref, wv_ref, kg_ref, kv_ref, wd_ref, out_ref, hn_ref, ug_ref, act_ref, hn_halo, halo, acc):
        i = pl.program_id(0)
        kb = pl.program_id(1)

        @pl.when(kb == 0)
        def _():
            gg = g_ref[...]

            def norm(x):
                r = lax.rsqrt(jnp.mean(x * x, axis=-1, keepdims=True) + EPS)
                return ((x * r) * gg).astype(BF16)

            hn_halo[...] = jnp.where(i > 0, norm(hp_ref[...]), jnp.zeros((pad, D), BF16))
            hn_ref[...] = norm(hc_ref[...])
            acc[...] = jnp.zeros_like(acc)

        w_refs = (wg_ref, wv_ref)
        taps = (_tap_rows(kg_ref), _tap_rows(kv_ref))
        hh = hn_halo[...]
        for h in range(2):
            halo[h] = jnp.dot(hh, w_refs[h][...], preferred_element_type=F32)[pad - FFN_PAD:]

        def up_part(lo, hi):
            a, b = lo * rc, hi * rc
            for h in range(2):
                ug_ref[h, a:b, :] = jnp.dot(hn_ref[a:b, :], w_refs[h][...], preferred_element_type=F32)

        def down_part(lo, hi):
            a, b = lo * rc, hi * rc
            acc[a:b, :] += jnp.dot(act_ref[a:b, :], wd_ref[...], preferred_element_type=F32)

        def chunk_rows(lo, hi):
            for c in range(lo, hi):
                r0 = c * rc
                convd = []
                for h in range(2):
                    if c == 0:
                        win = jnp.concatenate([halo[h], ug_ref[h, 0:rc]], axis=0)
                    else:
                        win = ug_ref[h, r0 - FFN_PAD:r0 + rc]
                    convd.append(_ffn_conv(win, taps[h], rc))
                gate, val = convd
                act_ref[r0:r0 + rc, :] = ((gate * _sigmoid(gate)) * val).astype(BF16)

        for p, (lo, hi) in enumerate(parts):
            if p == 0:
                up_part(lo, hi)
            if p + 1 < len(parts):
                up_part(*parts[p + 1])
            if p > 0:
                down_part(*parts[p - 1])
            chunk_rows(lo, hi)
        down_part(*parts[-1])

        @pl.when(kb == nj - 1)
        def _():
            out_ref[...] = acc[...] + hc_ref[...]

    out, xo = _call(
        body, name=name, grid=(nt, nj),
        in_specs=[pl.BlockSpec((pad, D), lambda i, k: (jnp.maximum(i * hb - 1, 0), 0)),
                  pl.BlockSpec((tl, D), lambda i, k: (i, 0)),
                  pl.BlockSpec((1, D), lambda i, k: (0, 0)),
                  pl.BlockSpec((D, tc), lambda i, k: (0, k)), pl.BlockSpec((D, tc), lambda i, k: (0, k + nj)),
                  pl.BlockSpec((kw, tc), lambda i, k: (0, k)), pl.BlockSpec((kw, tc), lambda i, k: (0, k + nj)),
                  pl.BlockSpec((tc, D), lambda i, k: (k, 0))],
        out_specs=(pl.BlockSpec((tl, D), lambda i, k: (i, 0)), pl.BlockSpec((tl, D), lambda i, k: (i, 0)),
                   pl.BlockSpec((2, tl, tc), lambda i, k: (0, i, k)), pl.BlockSpec((tl, tc), lambda i, k: (i, k))),
        out_shape=(jax.ShapeDtypeStruct((L, D), F32), jax.ShapeDtypeStruct((L, D), BF16),
                   jax.ShapeDtypeStruct((2, L, f), F32), jax.ShapeDtypeStruct((L, f), BF16)),
        scratch_shapes=[pltpu.VMEM((pad, D), BF16), pltpu.VMEM((2, FFN_PAD, tc), F32), pltpu.VMEM((tl, D), F32)],
        sem=("parallel", "arbitrary"), args=(h_mid, h_mid, g, wup, wup, kf, kf, wdown), xchg=xchg)
    return out if xchg is None else (out, xo)


def _ffn_block_bwd(dh, h_mid, g, ug0, kf, wdown, wup, name, xchg=None):
    L, D = dh.shape
    f = ug0.shape[2]
    kw = kf.shape[0]
    tl = _token_tile(L)
    tc = _divisor(f, 256, 128)
    nj = f // tc
    nt = L // tl
    pad = 2 * SUBLANES
    hb, nhb = tl // FFN_PAD, L // FFN_PAD
    rc = CONV3_ROWS
    nc = tl // rc
    parts = _row_parts(nc)

    def body(dhc_ref, dhn_ref, hm_ref, g_ref, gp_ref, gc_ref, gn_ref, vp_ref, vc_ref, vn_ref, kg_ref, kv_ref,
             wd_ref, wg_ref, wv_ref, dhm_ref, dg_ref, du_ref, dk_ref, dh_ext, dact_s, acc):
        i = pl.program_id(0)
        kb = pl.program_id(1)

        @pl.when(kb == 0)
        def _():
            dh_ext[0:tl, :] = dhc_ref[...].astype(BF16)
            dh_ext[tl:tl + pad, :] = dhn_ref[...].astype(BF16)
            acc[...] = jnp.zeros_like(acc)

        @pl.when(jnp.logical_and(i == 0, kb == 0))
        def _():
            dg_ref[...] = jnp.zeros_like(dg_ref)
            dk_ref[...] = jnp.zeros_like(dk_ref)

        prev = (jnp.where(i > 0, gp_ref[...], 0.0), jnp.where(i > 0, vp_ref[...], 0.0))
        x_refs, nxt = (gc_ref, vc_ref), (gn_ref, vn_ref)
        taps = (_tap_rows(kg_ref), _tap_rows(kv_ref))
        dk = [[jnp.zeros((SUBLANES, tc), F32) for _ in range(kw)] for _ in range(2)]

        def dact_part(lo, hi):
            a, b = lo * rc, hi * rc + pad
            dact_s[a:b, :] = _dot_nt(dh_ext[a:b, :], wd_ref[...])

        def dhn_part(lo, hi):
            a, b = lo * rc, hi * rc
            acc[a:b, :] += _dot_nt(du_ref[0, a:b, :], wg_ref[...]) + _dot_nt(du_ref[1, a:b, :], wv_ref[...])

        for p, (lo, hi) in enumerate(parts):
            if p == 0:
                dact_part(lo, hi)
            if p + 1 < len(parts):
                dact_part(*parts[p + 1])
            if p > 0:
                dhn_part(*parts[p - 1])
            chunk_rows(lo, hi, prev, x_refs, nxt, taps, dk, i, dact_s, du_ref)
        dhn_part(*parts[-1])
        for h in range(2):
            for j in range(kw):
                dk_ref[kb, h, j:j + 1, :] += jnp.sum(dk[h][j], axis=0, keepdims=True)

        @pl.when(kb == nj - 1)
        def _():
            x = hm_ref[...]
            r = lax.rsqrt(jnp.mean(x * x, axis=-1, keepdims=True) + EPS)
            xhat = x * r
            dhn = acc[...]
            dxhat = dhn * g_ref[...]
            dhm_ref[...] = dhc_ref[...] + r * (dxhat - xhat * jnp.mean(dxhat * xhat, axis=-1, keepdims=True))
            dg_ref[...] += jnp.sum(_rowsum8(dhn * xhat), axis=0, keepdims=True)

    def chunk_rows(lo, hi, prev, x_refs, nxt, taps, dk, i, dact_s, du_ref):
        for c in range(lo, hi):
            r0 = c * rc
            n = rc + FFN_PAD
            xs = []
            for h in range(2):
                parts = [prev[h] if c == 0 else x_refs[h][r0 - FFN_PAD:r0]]
                if c == nc - 1:
                    parts += [x_refs[h][r0:r0 + rc], nxt[h][...]]
                else:
                    parts += [x_refs[h][r0:r0 + n]]
                xs.append(jnp.concatenate(parts, axis=0))
            gate = _ffn_conv(xs[0], taps[0], n)
            val = _ffn_conv(xs[1], taps[1], n)
            dact = dact_s[r0:r0 + n, :]
            sg = _sigmoid(gate)
            dcs = [dact * val * (sg * (1.0 + gate * (1.0 - sg))), dact * (gate * sg)]
            if c == nc - 1:
                live = jnp.logical_or(lax.broadcasted_iota(jnp.int32, (n, 1), 0) < rc, i < nt - 1)
                dcs = [jnp.where(live, d, 0.0) for d in dcs]
            for h in range(2):
                xc = xs[h][FFN_PAD:FFN_PAD + rc]
                dx = None
                for j in range(kw):
                    o = kw - 1 - j
                    sh = dcs[h][o:o + rc]
                    term = _rows_of(taps[h][j], rc) * sh
                    dx = term if dx is None else dx + term
                    dk[h][j] = dk[h][j] + _rowsum8(xc * sh)
                du_ref[h, r0:r0 + rc, :] = dx.astype(BF16)

    def prev8(i, k):
        return (jnp.maximum(i * hb - 1, 0), k)

    def next8(i, k):
        return (jnp.minimum((i + 1) * hb, nhb - 1), k)

    def half(h, rows, idx):
        return pl.BlockSpec((None, rows, tc), lambda i, k: (h,) + idx(i, k))

    def tile(i, k):
        return (i, k)

    out, xo = _call(
        body, name=name, grid=(nt, nj),
        in_specs=[pl.BlockSpec((tl, D), lambda i, k: (i, 0)),
                  pl.BlockSpec((pad, D), lambda i, k: (jnp.minimum((i + 1) * (tl // pad), L // pad - 1), 0)),
                  pl.BlockSpec((tl, D), lambda i, k: (i, 0)), pl.BlockSpec((1, D), lambda i, k: (0, 0)),
                  half(0, FFN_PAD, prev8), half(0, tl, tile), half(0, FFN_PAD, next8),
                  half(1, FFN_PAD, prev8), half(1, tl, tile), half(1, FFN_PAD, next8),
                  pl.BlockSpec((kw, tc), lambda i, k: (0, k)), pl.BlockSpec((kw, tc), lambda i, k: (0, k + nj)),
                  pl.BlockSpec((tc, D), lambda i, k: (k, 0)),
                  pl.BlockSpec((D, tc), lambda i, k: (0, k)), pl.BlockSpec((D, tc), lambda i, k: (0, k + nj))],
        out_specs=(pl.BlockSpec((tl, D), lambda i, k: (i, 0)), pl.BlockSpec((1, D), lambda i, k: (0, 0)),
                   pl.BlockSpec((2, tl, tc), lambda i, k: (0, i, k)),
                   pl.BlockSpec((nj, 2, kw, tc), lambda i, k: (0, 0, 0, 0))),
        out_shape=(jax.ShapeDtypeStruct((L, D), F32), jax.ShapeDtypeStruct((1, D), F32),
                   jax.ShapeDtypeStruct((2, L, f), BF16), jax.ShapeDtypeStruct((nj, 2, kw, tc), F32)),
        scratch_shapes=[pltpu.VMEM((tl + pad, D), BF16), pltpu.VMEM((tl + pad, tc), F32), pltpu.VMEM((tl, D), F32)],
        sem=("arbitrary", "arbitrary"), args=(dh, dh, h_mid, g, ug0, ug0, ug0, ug0, ug0, ug0, kf, kf, wdown, wup, wup),
        xchg=xchg)
    return out if xchg is None else (out, xo)


def _adamw_math(w, g, m, v):
    m = ADAM_B1 * m + (1.0 - ADAM_B1) * g
    v = ADAM_B2 * v + (1.0 - ADAM_B2) * (g * g)
    m_hat = m / (1.0 - ADAM_B1 ** ADAM_STEP)
    v_hat = v / (1.0 - ADAM_B2 ** ADAM_STEP)
    delta = -ADAM_LR * (m_hat / (jnp.sqrt(v_hat) + ADAM_EPS) + ADAM_WD * w)
    return delta, m, v


def _sum_parts(parts_ref, idx):
    g = parts_ref[(0,) + idx].astype(F32)
    for q in range(1, N_DEV):
        g = g + parts_ref[(q,) + idx].astype(F32)
    return g


def _adamw_big(parts, w, m, v, name):
    nl, R, C = w.shape
    tr = _divisor(R, 256, 2 * SUBLANES)

    def body(*refs):
        p_refs = refs[:nl]
        w_ref, m_ref, v_ref, g_ref, d_ref, nm_ref, nv_ref = refs[nl:]
        layer = pl.program_id(0)
        for k in range(nl):
            @pl.when(layer == k)
            def _(k=k):
                g = _sum_parts(p_refs[k], ())
                d, nm, nv = _adamw_math(w_ref[0], g, m_ref[0], v_ref[0])
                g_ref[0] = g
                d_ref[0] = d
                nm_ref[0] = nm
                nv_ref[0] = nv

    def part_spec(k):
        return pl.BlockSpec((N_DEV, tr, C), lambda l, r: (0, jnp.where(l == k, r, 0), 0))

    blk = pl.BlockSpec((1, tr, C), lambda l, r: (l, r, 0))
    shp = jax.ShapeDtypeStruct((nl, R, C), F32)
    return pl.pallas_call(
        body, name=name, grid=(nl, R // tr),
        in_specs=[part_spec(k) for k in range(nl)] + [blk, blk, blk],
        out_specs=(blk, blk, blk, blk), out_shape=(shp, shp, shp, shp),
        compiler_params=_params(("arbitrary", "arbitrary")),
    )(*parts, w, m, v)


def _adamw_small(entries, name):
    n = len(entries)
    uniq = []
    for e in entries:
        if not any(e[0] is u for u in uniq):
            uniq.append(e[0])
    pidx = [next(k for k, u in enumerate(uniq) if u is e[0]) for e in entries]
    npart = len(uniq)

    def body(*refs):
        p_refs = refs[:npart]
        wmv = refs[npart:npart + 3 * n]
        outs = refs[npart + 3 * n:]
        for t, e in enumerate(entries):
            lo, w = e[1], e[2]
            rows = w.shape[0]
            pr = p_refs[pidx[t]]
            g = pr[0, lo:lo + rows].astype(F32)
            for q in range(1, N_DEV):
                g = g + pr[q, lo:lo + rows].astype(F32)
            d, nm, nv = _adamw_math(wmv[3 * t][...], g, wmv[3 * t + 1][...], wmv[3 * t + 2][...])
            outs[4 * t][...] = g
            outs[4 * t + 1][...] = d
            outs[4 * t + 2][...] = nm
            outs[4 * t + 3][...] = nv

    vm = pl.BlockSpec(memory_space=pltpu.VMEM)
    args = list(uniq)
    out_shape = []
    for e in entries:
        args += [e[2], e[3], e[4]]
        out_shape += [jax.ShapeDtypeStruct(e[2].shape, F32)] * 4
    res = pl.pallas_call(
        body, name=name, in_specs=[vm] * len(args), out_specs=tuple([vm] * len(out_shape)),
        out_shape=tuple(out_shape), compiler_params=_params(),
    )(*args)
    return [tuple(res[4 * t:4 * t + 4]) for t in range(n)]


def _head_matrix(cc):
    bw = min(256, cc)
    r = lax.broadcasted_iota(jnp.int32, (bw, bw), 0) // HEAD_DIM
    c = lax.broadcasted_iota(jnp.int32, (bw, bw), 1) // HEAD_DIM
    return jnp.where(r == c, 1.0 / HEAD_DIM, 0.0).astype(BF16)


def _cols_from_shards(g):
    nd = g.ndim
    perm = tuple(range(1, nd - 1)) + (0, nd - 1)
    t = jnp.transpose(g, perm)
    return t.reshape(t.shape[:-2] + (t.shape[-2] * t.shape[-1],))


def _cols_to_shards(a):
    nd = a.ndim
    t = a.reshape(a.shape[:-1] + (N_DEV, a.shape[-1] // N_DEV))
    perm = (nd - 1,) + tuple(range(nd - 1)) + (nd,)
    return jnp.transpose(t, perm)


def kernel(x, meta_tokens, norm1_g, w_in, conv_dw_k, conv_dw_b, conv_ln_g, conv_ln_b, pool_w, pool_scale, w_out, norm2_g, w_up, ffn_dw_k, w_down, final_g, loss_target, m_meta_tokens, m_norm1_g, m_w_in, m_conv_dw_k, m_conv_dw_b, m_conv_ln_g, m_conv_ln_b, m_pool_w, m_pool_scale, m_w_out, m_norm2_g, m_w_up, m_ffn_dw_k, m_w_down, m_final_g, v_meta_tokens, v_norm1_g, v_w_in, v_conv_dw_k, v_conv_dw_b, v_conv_ln_g, v_conv_ln_b, v_pool_w, v_pool_scale, v_w_out, v_norm2_g, v_w_up, v_ffn_dw_k, v_w_down, v_final_g):
    depth, D = norm1_g.shape
    n_meta = meta_tokens.shape[0]
    seq = x.shape[1]
    L = n_meta + seq
    cc = conv_dw_b.shape[1]
    ng, gd = pool_w.shape[1], pool_w.shape[2]
    f = w_down.shape[1] * N_DEV

    cols = _cols_from_shards

    def rows(g):
        return g.reshape(-1, g.shape[-1])

    b16 = lambda a: a.astype(BF16)
    (g_in0, g_out0, g_ck, g_kf, g_meta) = _exchange([b16(w_in[0]), b16(w_out[0]), conv_dw_k, ffn_dw_k, meta_tokens],
                                                    ["gather"] * 5, "gather_first")
    ck_full = _cols_from_shards(g_ck)
    ck_rows = jnp.broadcast_to(ck_full[:, :, None, :], ck_full.shape[:2] + (SUBLANES, cc))
    kf_full = _cols_from_shards(g_kf)
    meta_full = _cols_from_shards(g_meta)
    am = _head_matrix(cc)
    win, wout, wup, wdown = [None] * depth, [None] * depth, [None] * depth, [None] * depth
    win[0] = cols(g_in0)
    wout[0] = rows(g_out0)

    h = jnp.concatenate([meta_full, x[0]], axis=0)
    saved = []
    for l in range(depth):
        more = l + 1 < depth
        if l == 0:
            (z, hn1), (g_down,) = _norm_proj(h, norm1_g[l:l + 1], win[l], f"in_proj_{l}", tn_cap=768,
                                             xchg=([b16(w_down[l])], ["gather"]))
            wdown[l] = rows(g_down)
            (ymix, u1), (g_up,) = _mixer_fwd(z, ck_rows[l], conv_dw_b[l:l + 1], conv_ln_g[l:l + 1], conv_ln_b[l:l + 1],
                                             pool_w[l], pool_scale[l:l + 1], am, f"mixer_fwd_{l}",
                                             xchg=([b16(w_up[l])], ["gather"]))
            wup[l] = cols(g_up)
        else:
            z, hn1 = _norm_proj(h, norm1_g[l:l + 1], win[l], f"in_proj_{l}", tn_cap=768)
            ymix, u1 = _mixer_fwd(z, ck_rows[l], conv_dw_b[l:l + 1], conv_ln_g[l:l + 1], conv_ln_b[l:l + 1], pool_w[l],
                                  pool_scale[l:l + 1], am, f"mixer_fwd_{l}")
        if more:
            h_mid, (g_in,) = _mm(ymix, wout[l], f"out_proj_{l}", res=h, tn_cap=512, xchg=([b16(w_in[l + 1])], ["gather"]))
            win[l + 1] = cols(g_in)
            nxt = [b16(w_out[l + 1]), b16(w_up[l + 1]), b16(w_down[l + 1])]
            (h_out, hn2, ug0, act), got = _ffn_block_fwd(h_mid, norm2_g[l:l + 1], wup[l], kf_full[l], wdown[l],
                                                         f"ffn_fwd_{l}", xchg=(nxt, ["gather"] * 3))
            wout[l + 1], wup[l + 1], wdown[l + 1] = rows(got[0]), cols(got[1]), rows(got[2])
        else:
            h_mid = _mm(ymix, wout[l], f"out_proj_{l}", res=h, tn_cap=512)
            h_out, hn2, ug0, act = _ffn_block_fwd(h_mid, norm2_g[l:l + 1], wup[l], kf_full[l], wdown[l], f"ffn_fwd_{l}")
        saved.append((h, hn1, z, u1, ymix, h_mid, hn2, ug0, act))
        h = h_out

    tgt = jnp.concatenate([jnp.zeros((n_meta, D), F32), loss_target[0]], axis=0)
    dh, d_final_g, loss_part = _loss_head(h, final_g.reshape(1, D), tgt, n_meta, "loss_head")

    def row_shards(gm):
        return b16(gm.reshape(N_DEV, -1, gm.shape[-1]))

    def col_shards(gm):
        return b16(_cols_to_shards(gm))

    gw = {k: [None] * depth for k in ("ck", "cb", "lg", "lb", "pw", "ps", "kf", "n1", "n2")}
    parts = {k: [None] * depth for k in ("in", "out", "up", "down")}
    for l in reversed(range(depth)):
        h_in, hn1, z, u1, ymix, h_mid, hn2, ug0, act = saved[l]
        g_down = _mm_tn(act, dh, f"down_proj_wgrad_{l}", tq_cap=512)
        (dh_mid, gw["n2"][l], dug0, dkf), (parts["down"][l],) = _ffn_block_bwd(
            dh, h_mid, norm2_g[l:l + 1], ug0, kf_full[l], wdown[l], wup[l], f"ffn_bwd_{l}",
            xchg=([row_shards(g_down)], ["a2a"]))
        gw["kf"][l] = jnp.transpose(dkf, (2, 1, 0, 3)).reshape(dkf.shape[2], -1)
        g_up = _mm_tn(hn2, dug0, f"up_proj_wgrad_{l}", halves=2)
        dymix = _mm(dh_mid, wout[l], f"out_proj_bwd_{l}", b_t=True, tn_cap=512)
        g_out = _mm_tn(ymix, dh_mid, f"out_proj_wgrad_{l}", tq_cap=512)
        ((dz, gw["ck"][l], gw["cb"][l], gw["lg"][l], gw["lb"][l], gw["pw"][l], gw["ps"][l]),
         (parts["up"][l], parts["out"][l])) = _mixer_bwd(
            z, u1, dymix, ck_rows[l], conv_ln_g[l:l + 1], conv_ln_b[l:l + 1], pool_w[l], pool_scale[l:l + 1], am,
            f"mixer_bwd_{l}", xchg=([col_shards(g_up), row_shards(g_out)], ["a2a", "a2a"]))
        g_in = _mm_tn(hn1, dz, f"in_proj_wgrad_{l}", tq_cap=768)
        (dh, gw["n1"][l]), (parts["in"][l],) = _proj_bwd_norm(dz, win[l], h_in, norm1_g[l:l + 1], dh_mid,
                                                              f"in_proj_bwd_{l}", xchg=([col_shards(g_in)], ["a2a"]))
    grad_x = dh[n_meta:][None]
    d_meta = dh[:n_meta]

    zero_row = jnp.zeros((1, D), F32)
    pack_d = jnp.concatenate(gw["n1"] + gw["n2"] + [d_final_g, jnp.broadcast_to(loss_part[:, :1], (1, D)), zero_row, zero_row], axis=0)
    pack_c = jnp.concatenate(gw["cb"] + gw["lg"] + gw["lb"] + gw["ps"], axis=0)
    pack_pw = jnp.stack(gw["pw"]).reshape(depth * ng * gd, gd)
    src = [_cols_to_shards(jnp.stack(gw["ck"])), _cols_to_shards(jnp.stack(gw["kf"])), _cols_to_shards(d_meta),
           pack_d, pack_c, pack_pw]
    r_ck, r_kf, r_meta, r_d, r_c, r_pw = _exchange(src, ["a2a"] * 3 + ["gather"] * 3, "exchange_small_grads")

    big = {
        "w_in": _adamw_big(parts["in"], w_in, m_w_in, v_w_in, "adamw_w_in"),
        "w_out": _adamw_big(parts["out"], w_out, m_w_out, v_w_out, "adamw_w_out"),
        "w_up": _adamw_big(parts["up"], w_up, m_w_up, v_w_up, "adamw_w_up"),
        "w_down": _adamw_big(parts["down"], w_down, m_w_down, v_w_down, "adamw_w_down"),
    }
    kwid = conv_dw_k.shape[1]
    fkw = ffn_dw_k.shape[1]
    row = lambda a: a.reshape(1, -1)
    entries = [
        (r_d, 0, norm1_g, m_norm1_g, v_norm1_g),
        (r_d, depth, norm2_g, m_norm2_g, v_norm2_g),
        (r_d, 2 * depth, row(final_g), row(m_final_g), row(v_final_g)),
        (r_c, 0, conv_dw_b, m_conv_dw_b, v_conv_dw_b),
        (r_c, depth, conv_ln_g, m_conv_ln_g, v_conv_ln_g),
        (r_c, 2 * depth, conv_ln_b, m_conv_ln_b, v_conv_ln_b),
        (r_c, 3 * depth, pool_scale, m_pool_scale, v_pool_scale),
        (r_pw, 0, pool_w.reshape(-1, gd), m_pool_w.reshape(-1, gd), v_pool_w.reshape(-1, gd)),
        (r_ck.reshape(N_DEV, depth * kwid, -1), 0, conv_dw_k.reshape(depth * kwid, -1),
         m_conv_dw_k.reshape(depth * kwid, -1), v_conv_dw_k.reshape(depth * kwid, -1)),
        (r_kf.reshape(N_DEV, depth * fkw, -1), 0, ffn_dw_k.reshape(depth * fkw, -1),
         m_ffn_dw_k.reshape(depth * fkw, -1), v_ffn_dw_k.reshape(depth * fkw, -1)),
        (r_meta, 0, meta_tokens, m_meta_tokens, v_meta_tokens),
        (r_d, 2 * depth + 1, zero_row, zero_row, zero_row),
    ]
    small = _adamw_small(entries, "adamw_small")
    names = ["norm1_g", "norm2_g", "final_g", "conv_dw_b", "conv_ln_g", "conv_ln_b", "pool_scale", "pool_w",
             "conv_dw_k", "ffn_dw_k", "meta_tokens"]
    shapes = {"final_g": final_g.shape, "pool_w": pool_w.shape, "conv_dw_k": conv_dw_k.shape, "ffn_dw_k": ffn_dw_k.shape}
    res = dict(big)
    for nme, quad in zip(names, small[:-1]):
        res[nme] = tuple(a.reshape(shapes[nme]) if nme in shapes else a for a in quad)
    loss = small[-1][0][0, 0]

    order = ["meta_tokens", "norm1_g", "w_in", "conv_dw_k", "conv_dw_b", "conv_ln_g", "conv_ln_b", "pool_w", "pool_scale",
             "w_out", "norm2_g", "w_up", "ffn_dw_k", "w_down", "final_g"]
    return (loss, grad_x, *[res[k][0] for k in order], *[res[k][1] for k in order], *[res[k][2] for k in order],
            *[res[k][3] for k in order])
```

```python
import functools

import jax
import jax.numpy as jnp
from jax import lax
from jax.experimental import pallas as pl
from jax.experimental.pallas import tpu as pltpu

F32 = jnp.float32
BF16 = jnp.bfloat16

EPS = 1e-6
HEAD_DIM = 64
POOL_WINDOWS = (2, 4, 8, 16)
ADAM_LR = 0.001
ADAM_B1 = 0.9
ADAM_B2 = 0.999
ADAM_EPS = 1e-08
ADAM_WD = 0.01
ADAM_STEP = 10

N_DEV = 8
OTHER_CHIPS = (2, 4, 6)
SUBLANES = 8
HALO = 48
CONV_PAD = 32
POOL_PAD = 16
FFN_PAD = 8
ROW_CHUNK = 24
CONV3_ROWS = 48
MAX_TILE_ROWS = 1024
WGRAD_TILE_ROWS = 2816
VMEM_LIMIT = 52 * 1024 * 1024


def _divisor(n, cap, mult):
    best = None
    for d in range(mult, min(n, cap) + 1, mult):
        if n % d == 0:
            best = d
    return n if best is None else best


def _token_tile(L):
    return _divisor(L, MAX_TILE_ROWS, HALO)


def _row_tile(L):
    return _divisor(L, 320, 2 * SUBLANES)


def _stat_rows(tl):
    return _divisor(tl, 256, SUBLANES)


def _params(sem=None):
    return pltpu.CompilerParams(dimension_semantics=sem, vmem_limit_bytes=VMEM_LIMIT)


def _rowsum8(x):
    acc = x[0:SUBLANES]
    for k in range(1, x.shape[0] // SUBLANES):
        acc = acc + x[k * SUBLANES:(k + 1) * SUBLANES]
    return acc


def _sigmoid(x):
    return jax.nn.sigmoid(x)


def _dot_nt(a, b):
    return lax.dot_general(a, b, (((1,), (1,)), ((), ())), preferred_element_type=F32)


def _head_mean(x, am_ref):
    bw = am_ref.shape[0]
    am = am_ref[...]
    outs = []
    for blk in range(x.shape[1] // bw):
        xb = x[:, blk * bw:(blk + 1) * bw]
        hi = xb.astype(BF16)
        lo = (xb - hi.astype(F32)).astype(BF16)
        outs.append(jnp.dot(hi, am, preferred_element_type=F32) + jnp.dot(lo, am, preferred_element_type=F32))
    return outs[0] if len(outs) == 1 else jnp.concatenate(outs, axis=-1)


def _xchg_out_shapes(srcs, modes):
    out = []
    for s, m in zip(srcs, modes):
        shp = ((N_DEV,) + tuple(s.shape)) if m == "gather" else tuple(s.shape)
        out.append(jax.ShapeDtypeStruct(shp, s.dtype))
    return out


def _xchg_sems(n):
    return [pltpu.SemaphoreType.DMA((n, N_DEV - 1)), pltpu.SemaphoreType.DMA((n, N_DEV - 1)), pltpu.SemaphoreType.DMA((n,))]


def _xchg_ops(src_refs, out_refs, sems, modes):
    n = len(src_refs)
    send_sems, recv_sems, local_sems = sems
    x, y, c = lax.axis_index("x"), lax.axis_index("y"), lax.axis_index("c")
    me = 4 * x + 2 * y + c

    def peer(d):
        return (x ^ ((d >> 2) & 1), y ^ ((d >> 1) & 1), c ^ (d & 1))

    def peer_id(d):
        px, py, pc = peer(d)
        return 4 * px + 2 * py + pc

    def remote(t, d):
        src = src_refs[t] if modes[t] == "gather" else src_refs[t].at[peer_id(d)]
        return pltpu.make_async_remote_copy(
            src_ref=src, dst_ref=out_refs[t].at[me], send_sem=send_sems.at[t, d - 1], recv_sem=recv_sems.at[t, d - 1],
            device_id=peer(d), device_id_type=pl.DeviceIdType.MESH)

    def arrival(t, d):
        src = src_refs[t] if modes[t] == "gather" else src_refs[t].at[me]
        return pltpu.make_async_remote_copy(
            src_ref=src, dst_ref=out_refs[t].at[peer_id(d)], send_sem=send_sems.at[t, d - 1],
            recv_sem=recv_sems.at[t, d - 1], device_id=peer(d), device_id_type=pl.DeviceIdType.MESH)

    def passed_on(t, d):
        blk = out_refs[t].at[peer_id(d)]
        return pltpu.make_async_remote_copy(
            src_ref=blk, dst_ref=blk, send_sem=send_sems.at[t, d], recv_sem=recv_sems.at[t, d],
            device_id=peer(1), device_id_type=pl.DeviceIdType.MESH)

    def local(t):
        src = src_refs[t] if modes[t] == "gather" else src_refs[t].at[me]
        return pltpu.make_async_copy(src, out_refs[t].at[me], local_sems.at[t])

    def sent_first(t):
        return OTHER_CHIPS + (1,) if modes[t] == "gather" else tuple(range(1, N_DEV))

    def start():
        for t in range(n):
            local(t).start()
        for t in range(n):
            for d in sent_first(t):
                remote(t, d).start()

    def wait():
        gathered = [t for t in range(n) if modes[t] == "gather"]
        for t in gathered:
            for d in OTHER_CHIPS:
                arrival(t, d).wait_recv()
                passed_on(t, d).start()
        for t in range(n):
            for d in range(1, N_DEV):
                if not (modes[t] == "gather" and d in OTHER_CHIPS):
                    arrival(t, d).wait_recv()
        for t in range(n):
            for d in sent_first(t):
                remote(t, d).wait_send()
        for t in gathered:
            for d in OTHER_CHIPS:
                passed_on(t, d).wait_send()
        for t in range(n):
            local(t).wait()

    return start, wait


def _exchange(srcs, modes, name):
    n = len(srcs)

    def body(*refs):
        start, wait = _xchg_ops(refs[:n], refs[n:2 * n], refs[2 * n:], modes)
        start()
        wait()

    any_spec = pl.BlockSpec(memory_space=pl.ANY)
    return pl.pallas_call(
        body, name=name, out_shape=tuple(_xchg_out_shapes(srcs, modes)),
        in_specs=[any_spec] * n, out_specs=tuple([any_spec] * n),
        scratch_shapes=_xchg_sems(n),
        compiler_params=pltpu.CompilerParams(has_side_effects=True),
    )(*srcs)


def _call(body, *, name, grid, in_specs, out_specs, out_shape, args, scratch_shapes=(), sem=None, xchg=None):
    single = not isinstance(out_shape, (tuple, list))
    outs_shape = [out_shape] if single else list(out_shape)
    outs_spec = [out_specs] if single else list(out_specs)
    if xchg is None:
        res = pl.pallas_call(
            body, name=name, grid=grid, in_specs=list(in_specs), out_specs=out_specs, out_shape=out_shape,
            scratch_shapes=list(scratch_shapes), compiler_params=_params(sem))(*args)
        return res, ()
    srcs, modes = xchg
    n_in, n_out, n_scr, nx = len(in_specs), len(outs_shape), len(scratch_shapes), len(srcs)

    def wrapped(*refs):
        ins = refs[:n_in]
        xs = refs[n_in:n_in + nx]
        o0 = n_in + nx
        outs = refs[o0:o0 + n_out]
        xo = refs[o0 + n_out:o0 + n_out + nx]
        s0 = o0 + n_out + nx
        scr = refs[s0:s0 + n_scr]
        start, wait = _xchg_ops(xs, xo, refs[s0 + n_scr:], modes)
        first = functools.reduce(jnp.logical_and, [pl.program_id(a) == 0 for a in range(len(grid))])
        last = functools.reduce(jnp.logical_and, [pl.program_id(a) == grid[a] - 1 for a in range(len(grid))])

        @pl.when(first)
        def _():
            start()

        body(*ins, *outs, *scr)

        @pl.when(last)
        def _():
            wait()

    any_spec = pl.BlockSpec(memory_space=pl.ANY)
    res = pl.pallas_call(
        wrapped, name=name, grid=grid, in_specs=list(in_specs) + [any_spec] * nx,
        out_specs=tuple(outs_spec + [any_spec] * nx), out_shape=tuple(outs_shape + _xchg_out_shapes(srcs, modes)),
        scratch_shapes=list(scratch_shapes) + _xchg_sems(nx),
        compiler_params=_params(("arbitrary",) * len(grid)))(*args, *srcs)
    comp = res[:n_out]
    return (comp[0] if single else tuple(comp)), tuple(res[n_out:])


def _norm_proj(h, g, w, name, *, tn_cap, xchg=None):
    L, D = h.shape
    N = w.shape[0]
    tm = _token_tile(L)
    tn = _divisor(N, tn_cap, 128)

    def body(h_ref, g_ref, w_ref, z_ref, hn_ref):
        @pl.when(pl.program_id(1) == 0)
        def _():
            x = h_ref[...]
            r = lax.rsqrt(jnp.mean(x * x, axis=-1, keepdims=True) + EPS)
            hn_ref[...] = ((x * r) * g_ref[...]).astype(BF16)

        z_ref[...] = _dot_nt(hn_ref[...], w_ref[...])

    out, xo = _call(
        body, name=name, grid=(L // tm, N // tn),
        in_specs=[pl.BlockSpec((tm, D), lambda i, j: (i, 0)), pl.BlockSpec((1, D), lambda i, j: (0, 0)),
                  pl.BlockSpec((tn, D), lambda i, j: (j, 0))],
        out_specs=(pl.BlockSpec((tm, tn), lambda i, j: (i, j)), pl.BlockSpec((tm, D), lambda i, j: (i, 0))),
        out_shape=(jax.ShapeDtypeStruct((L, N), F32), jax.ShapeDtypeStruct((L, D), BF16)),
        sem=("parallel", "arbitrary"), args=(h, g, w), xchg=xchg)
    return out if xchg is None else (out, xo)


def _proj_bwd_norm(a, b, h, g, dres, name, xchg=None):
    L, K = a.shape
    D = b.shape[1]
    tm = _token_tile(L)

    def body(a_ref, b_ref, h_ref, g_ref, dres_ref, dh_ref, dg_ref):
        i = pl.program_id(0)
        dhn = jnp.dot(a_ref[...], b_ref[...], preferred_element_type=F32)
        x = h_ref[...]
        r = lax.rsqrt(jnp.mean(x * x, axis=-1, keepdims=True) + EPS)
        xhat = x * r
        dxhat = dhn * g_ref[...]
        dh_ref[...] = dres_ref[...] + r * (dxhat - xhat * jnp.mean(dxhat * xhat, axis=-1, keepdims=True))
        part = jnp.sum(_rowsum8(dhn * xhat), axis=0, keepdims=True)

        @pl.when(i == 0)
        def _():
            dg_ref[...] = part

        @pl.when(i > 0)
        def _():
            dg_ref[...] += part

    tile = pl.BlockSpec((tm, D), lambda i: (i, 0))
    row = pl.BlockSpec((1, D), lambda i: (0, 0))
    out, xo = _call(
        body, name=name, grid=(L // tm,),
        in_specs=[pl.BlockSpec((tm, K), lambda i: (i, 0)), pl.BlockSpec((K, D), lambda i: (0, 0)), tile, row, tile],
        out_specs=(tile, row), out_shape=(jax.ShapeDtypeStruct((L, D), F32), jax.ShapeDtypeStruct((1, D), F32)),
        sem=("arbitrary",), args=(a, b, h, g, dres), xchg=xchg)
    return out if xchg is None else (out, xo)


def _mm(a, b, name, *, res=None, b_t=False, tn_cap=1408, xchg=None):
    M, K = a.shape
    N = b.shape[0] if b_t else b.shape[1]
    tm = _token_tile(M)
    tn = _divisor(N, tn_cap, 128)

    def body(*refs):
        a_ref, b_ref = refs[:2]
        r_ref, o_ref = (None, refs[2]) if res is None else (refs[2], refs[3])
        av = a_ref[...].astype(BF16)
        prod = _dot_nt(av, b_ref[...]) if b_t else jnp.dot(av, b_ref[...], preferred_element_type=F32)
        o_ref[...] = prod if r_ref is None else prod + r_ref[...]

    b_spec = pl.BlockSpec((tn, K), lambda i, j: (j, 0)) if b_t else pl.BlockSpec((K, tn), lambda i, j: (0, j))
    in_specs = [pl.BlockSpec((tm, K), lambda i, j: (i, 0)), b_spec]
    args = [a, b]
    if res is not None:
        in_specs.append(pl.BlockSpec((tm, tn), lambda i, j: (i, j)))
        args.append(res)
    out, xo = _call(
        body, name=name, grid=(M // tm, N // tn), in_specs=in_specs,
        out_specs=pl.BlockSpec((tm, tn), lambda i, j: (i, j)), out_shape=jax.ShapeDtypeStruct((M, N), F32),
        sem=("parallel", "parallel"), args=args, xchg=xchg)
    return out if xchg is None else (out, xo)


def _mm_tn(a, b, name, *, halves=1, tq_cap=1408):
    L, Q = b.shape
    ph = a.shape[-1]
    P = ph * halves
    tl = _divisor(L, WGRAD_TILE_ROWS, HALO)
    tp = _divisor(ph, 1408, 128)
    tq = _divisor(Q, tq_cap, 128)
    pper = ph // tp
    grid = (P // tp, Q // tq, L // tl)

    def body(a_ref, b_ref, o_ref):
        prod = lax.dot_general(a_ref[...].astype(BF16), b_ref[...].astype(BF16), (((0,), (0,)), ((), ())),
                               preferred_element_type=F32)
        l = pl.program_id(2)

        @pl.when(l == 0)
        def _():
            o_ref[...] = prod

        @pl.when(l > 0)
        def _():
            o_ref[...] += prod

    if halves > 1:
        a_spec = pl.BlockSpec((None, tl, tp), lambda p, q, l: (p // pper, l, p % pper))
    else:
        a_spec = pl.BlockSpec((tl, tp), lambda p, q, l: (l, p))
    return pl.pallas_call(
        body, name=name, grid=grid,
        in_specs=[a_spec, pl.BlockSpec((tl, tq), lambda p, q, l: (l, q))],
        out_specs=pl.BlockSpec((tp, tq), lambda p, q, l: (p, q)),
        out_shape=jax.ShapeDtypeStruct((P, Q), F32),
        compiler_params=_params(("parallel", "parallel", "arbitrary")),
    )(a, b)


def _loss_head(h, g, tgt, n_meta, name):
    L, D = h.shape
    tl = _row_tile(L)
    nt = L // tl

    def body(h_ref, g_ref, t_ref, dh_ref, dg_ref, loss_ref):
        i = pl.program_id(0)
        x = h_ref[...]
        r = lax.rsqrt(jnp.mean(x * x, axis=-1, keepdims=True) + EPS)
        xhat = x * r
        gg = g_ref[...]
        y = xhat * gg
        rows = i * tl + lax.broadcasted_iota(jnp.int32, (tl, 1), 0)
        t = t_ref[...]
        t = jnp.where(i == 0, pltpu.roll(t, n_meta, axis=0), t)
        err = jnp.where(rows >= n_meta, y - t, 0.0)
        dy = err * (1.0 / D)
        dxhat = dy * gg
        dh_ref[...] = r * (dxhat - xhat * jnp.mean(dxhat * xhat, axis=-1, keepdims=True))
        dg_part = jnp.sum(_rowsum8(dy * xhat), axis=0, keepdims=True)
        per_row = jnp.mean(err * err, axis=-1, keepdims=True)
        loss_part = jnp.broadcast_to(0.5 * jnp.sum(per_row, axis=0, keepdims=True), (1, 128))

        @pl.when(i == 0)
        def _():
            dg_ref[...] = dg_part
            loss_ref[...] = loss_part

        @pl.when(i > 0)
        def _():
            dg_ref[...] += dg_part
            loss_ref[...] += loss_part

    tile = pl.BlockSpec((tl, D), lambda i: (i, 0))
    row = pl.BlockSpec((1, D), lambda i: (0, 0))
    window = pl.BlockSpec((pl.Element(tl), pl.Element(D)),
                          lambda i: (pl.multiple_of(jnp.maximum(i * tl - n_meta, 0), SUBLANES), 0))
    return pl.pallas_call(
        body, name=name, grid=(nt,), in_specs=[tile, row, window],
        out_specs=(tile, row, pl.BlockSpec((1, 128), lambda i: (0, 0))),
        out_shape=(jax.ShapeDtypeStruct((L, D), F32), jax.ShapeDtypeStruct((1, D), F32),
                   jax.ShapeDtypeStruct((1, 128), F32)),
        compiler_params=_params(("arbitrary",)),
    )(h, g, tgt)


def _pool_fwd_block(pwin, pw_ref, row0, rb, g, gd, w, t0):
    wv = pwin[pl.ds(row0 + HALO - POOL_PAD, rb + POOL_PAD), g * gd:(g + 1) * gd]
    s = wv
    sh = 1
    while sh < w:
        s = s + pltpu.roll(s, sh, axis=0)
        sh *= 2
    win = s[POOL_PAD:POOL_PAD + rb]
    pt = wv[POOL_PAD:POOL_PAD + rb]
    tg = t0 + lax.broadcasted_iota(jnp.int32, (rb, 1), 0)
    cnt = jnp.minimum(tg + 1, w).astype(F32)
    return win / cnt - pt


def _fill_windows(i, zp_ref, zc_ref, u0w, pwin, tl, cc):
    keep = i > 0
    zp = zp_ref[...]
    u0w[0:HALO, :] = jnp.where(keep, zp[:, :cc] * _sigmoid(zp[:, cc:2 * cc]), 0.0)
    pwin[0:HALO, :] = jnp.where(keep, zp[:, 2 * cc:], 0.0)

    def fill(c, carry):
        b = pl.multiple_of(c * ROW_CHUNK, SUBLANES)
        zc = zc_ref[pl.ds(b, ROW_CHUNK), :]
        u0w[pl.ds(HALO + b, ROW_CHUNK), :] = zc[:, :cc] * _sigmoid(zc[:, cc:2 * cc])
        pwin[pl.ds(HALO + b, ROW_CHUNK), :] = zc[:, 2 * cc:]
        return carry

    lax.fori_loop(0, tl // ROW_CHUNK, fill, 0)


def _mixer_fwd(z, ck, cb, lg, lb, pw, ps, am, name, xchg=None):
    L, ci = z.shape
    kw, _, cc = ck.shape
    cp = ci - 2 * cc
    ng, gd = pw.shape[0], pw.shape[1]
    tl = _token_tile(L)
    nt = L // tl
    hb = tl // HALO
    rb = _stat_rows(tl)
    tap0 = CONV_PAD - (kw - 1)

    def body(zp_ref, zc_ref, ck_ref, cb_ref, lg_ref, lb_ref, pw_ref, ps_ref, am_ref, y_ref, u1_ref, u0w, pwin):
        i = pl.program_id(0)
        _fill_windows(i, zp_ref, zc_ref, u0w, pwin, tl, cc)

        def conv(c, carry):
            b = pl.multiple_of(c * ROW_CHUNK, SUBLANES)
            w = u0w[pl.ds(b + HALO - CONV_PAD, ROW_CHUNK + CONV_PAD), :]
            acc = jnp.broadcast_to(cb_ref[...], (ROW_CHUNK, cc))
            for j in range(kw):
                acc = acc + _rows_of(ck_ref[j], ROW_CHUNK) * w[tap0 + j:tap0 + j + ROW_CHUNK]
            u1_ref[pl.ds(b, ROW_CHUNK), :] = acc
            return carry

        lax.fori_loop(0, tl // ROW_CHUNK, conv, 0)

        def blocks(k, carry):
            b = pl.multiple_of(k * rb, SUBLANES)
            u1 = u1_ref[pl.ds(b, rb), :]
            xc = u1 - _head_mean(u1, am_ref)
            var = _head_mean(xc * xc, am_ref)
            u2 = (xc * lax.rsqrt(var + EPS)) * lg_ref[...] + lb_ref[...]
            y_ref[pl.ds(b, rb), 0:cc] = (u2 * _sigmoid(u2)).astype(y_ref.dtype)
            for g in range(ng):
                d = _pool_fwd_block(pwin, pw_ref, b, rb, g, gd, POOL_WINDOWS[g], i * tl + b)
                yp = jnp.dot(d.astype(BF16), pw_ref[g].astype(BF16), preferred_element_type=F32)
                yp = yp * ps_ref[:, g * gd:(g + 1) * gd]
                y_ref[pl.ds(b, rb), cc + g * gd:cc + (g + 1) * gd] = yp.astype(y_ref.dtype)
            return carry

        lax.fori_loop(0, tl // rb, blocks, 0)

    def full(a):
        nd = a.ndim
        return pl.BlockSpec(a.shape, lambda i: (0,) * nd)

    out, xo = _call(
        body, name=name, grid=(nt,),
        in_specs=[pl.BlockSpec((HALO, ci), lambda i: (jnp.maximum(i * hb - 1, 0), 0)),
                  pl.BlockSpec((tl, ci), lambda i: (i, 0)),
                  full(ck), full(cb), full(lg), full(lb), full(pw), full(ps), full(am)],
        out_specs=(pl.BlockSpec((tl, cc + cp), lambda i: (i, 0)), pl.BlockSpec((tl, cc), lambda i: (i, 0))),
        out_shape=(jax.ShapeDtypeStruct((L, cc + cp), BF16), jax.ShapeDtypeStruct((L, cc), F32)),
        scratch_shapes=[pltpu.VMEM((HALO + tl, cc), F32), pltpu.VMEM((HALO + tl, cp), F32)],
        sem=("parallel",), args=(z, z, ck, cb, lg, lb, pw, ps, am), xchg=xchg)
    return out if xchg is None else (out, xo)


def _mixer_bwd(z, u1, dy, ck, lg, lb, pw, ps, am, name, xchg=None):
    L, ci = z.shape
    kw, _, cc = ck.shape
    cp = ci - 2 * cc
    ng, gd = pw.shape[0], pw.shape[1]
    tl = _token_tile(L)
    nt = L // tl
    hb = tl // HALO
    rb = _stat_rows(tl)

    def body(zp_ref, zc_ref, u1c_ref, u1n_ref, dyc_ref, dyn_ref, ck_ref, lg_ref, lb_ref, pw_ref, ps_ref, am_ref,
             dz_ref, dck_ref, dcb_ref, dlg_ref, dlb_ref, dpw_ref, dps_ref,
             u0w, pwin, du1w, ddw, ew, dkacc, dcb8, dlg8, dlb8, dps8):
        i = pl.program_id(0)
        has_next = i < nt - 1

        @pl.when(i == 0)
        def _():
            dck_ref[...] = jnp.zeros_like(dck_ref)
            dcb_ref[...] = jnp.zeros_like(dcb_ref)
            dlg_ref[...] = jnp.zeros_like(dlg_ref)
            dlb_ref[...] = jnp.zeros_like(dlb_ref)
            dpw_ref[...] = jnp.zeros_like(dpw_ref)
            dps_ref[...] = jnp.zeros_like(dps_ref)

        dkacc[...] = jnp.zeros_like(dkacc)
        dcb8[...] = jnp.zeros_like(dcb8)
        dlg8[...] = jnp.zeros_like(dlg8)
        dlb8[...] = jnp.zeros_like(dlb8)
        dps8[...] = jnp.zeros_like(dps8)

        _fill_windows(i, zp_ref, zc_ref, u0w, pwin, tl, cc)

        def conv_side(u1, dyc, own):
            xc = u1 - _head_mean(u1, am_ref)
            rstd = lax.rsqrt(_head_mean(xc * xc, am_ref) + EPS)
            uh = xc * rstd
            lgv = lg_ref[...]
            u2 = uh * lgv + lb_ref[...]
            sg = _sigmoid(u2)
            du2 = dyc * (sg * (1.0 + u2 * (1.0 - sg)))
            if own:
                dlg8[...] += _rowsum8(du2 * uh)
                dlb8[...] += _rowsum8(du2)
            duh = du2 * lgv
            return rstd * (duh - _head_mean(duh, am_ref) - uh * _head_mean(duh * uh, am_ref))

        def pool_side(dyp, t0, rows):
            dds, es = [], []
            tg = t0 + lax.broadcasted_iota(jnp.int32, (rows, 1), 0)
            for g in range(ng):
                dypre = dyp[:, g * gd:(g + 1) * gd] * ps_ref[:, g * gd:(g + 1) * gd]
                dd = lax.dot_general(dypre.astype(BF16), pw_ref[g].astype(BF16), (((1,), (1,)), ((), ())),
                                     preferred_element_type=F32)
                cnt = jnp.minimum(tg + 1, POOL_WINDOWS[g]).astype(F32)
                dds.append(dd)
                es.append(dd / cnt)
            return jnp.concatenate(dds, axis=-1), jnp.concatenate(es, axis=-1)

        def blocks(k, carry):
            b = pl.multiple_of(k * rb, SUBLANES)
            dyb = dyc_ref[pl.ds(b, rb), :]
            du1 = conv_side(u1c_ref[pl.ds(b, rb), :], dyb[:, :cc], True)
            du1w[pl.ds(b, rb), :] = du1
            dcb8[...] += _rowsum8(du1)
            dyp = dyb[:, cc:]
            dd, e = pool_side(dyp, i * tl + b, rb)
            ddw[pl.ds(b, rb), :] = dd
            ew[pl.ds(b, rb), :] = e
            for g in range(ng):
                d = _pool_fwd_block(pwin, pw_ref, b, rb, g, gd, POOL_WINDOWS[g], i * tl + b)
                db16 = d.astype(BF16)
                dypg = dyp[:, g * gd:(g + 1) * gd]
                ypre = jnp.dot(db16, pw_ref[g].astype(BF16), preferred_element_type=F32)
                dps8[:, g * gd:(g + 1) * gd] += _rowsum8(dypg * ypre)
                dypre = (dypg * ps_ref[:, g * gd:(g + 1) * gd]).astype(BF16)
                dpw_ref[g] += lax.dot_general(db16, dypre, (((0,), (0,)), ((), ())), preferred_element_type=F32)
            return carry

        lax.fori_loop(0, tl // rb, blocks, 0)

        dyn = dyn_ref[...]
        du1n = conv_side(u1n_ref[...], dyn[:, :cc], False)
        du1w[tl:tl + HALO, :] = jnp.where(has_next, du1n, 0.0)
        ddn, en = pool_side(dyn[:, cc:], (i + 1) * tl, HALO)
        ew[tl:tl + HALO, :] = jnp.where(has_next, en, 0.0)

        def taps(c, carry):
            b = pl.multiple_of(c * ROW_CHUNK, SUBLANES)
            w = du1w[pl.ds(b, ROW_CHUNK + CONV_PAD), :]
            u0c = u0w[pl.ds(HALO + b, ROW_CHUNK), :]
            acc = jnp.zeros((ROW_CHUNK, cc), F32)
            for j in range(kw):
                o = kw - 1 - j
                sh = w[o:o + ROW_CHUNK]
                acc = acc + _rows_of(ck_ref[j], ROW_CHUNK) * sh
                dkacc[j] += _rowsum8(u0c * sh)
            zc = zc_ref[pl.ds(b, ROW_CHUNK), :]
            a = zc[:, :cc]
            sg = _sigmoid(zc[:, cc:2 * cc])
            dz_ref[pl.ds(b, ROW_CHUNK), 0:cc] = (acc * sg).astype(dz_ref.dtype)
            dz_ref[pl.ds(b, ROW_CHUNK), cc:2 * cc] = (acc * a * sg * (1.0 - sg)).astype(dz_ref.dtype)
            return carry

        lax.fori_loop(0, tl // ROW_CHUNK, taps, 0)

        def pool_back(k, carry):
            b = pl.multiple_of(k * rb, SUBLANES)
            n = rb + POOL_PAD
            for g in range(ng):
                s = ew[pl.ds(b, n), g * gd:(g + 1) * gd]
                sh = 1
                while sh < POOL_WINDOWS[g]:
                    s = s + pltpu.roll(s, n - sh, axis=0)
                    sh *= 2
                dp = s[0:rb] - ddw[pl.ds(b, rb), g * gd:(g + 1) * gd]
                dz_ref[pl.ds(b, rb), 2 * cc + g * gd:2 * cc + (g + 1) * gd] = dp.astype(dz_ref.dtype)
            return carry

        lax.fori_loop(0, tl // rb, pool_back, 0)

        dck_ref[...] += jnp.sum(dkacc[...], axis=1)
        dcb_ref[...] += jnp.sum(dcb8[...], axis=0, keepdims=True)
        dlg_ref[...] += jnp.sum(dlg8[...], axis=0, keepdims=True)
        dlb_ref[...] += jnp.sum(dlb8[...], axis=0, keepdims=True)
        dps_ref[...] += jnp.sum(dps8[...], axis=0, keepdims=True)

    def full(a):
        nd = a.ndim
        return pl.BlockSpec(a.shape, lambda i: (0,) * nd)

    nhb = L // HALO

    def prev_map(i):
        return (jnp.maximum(i * hb - 1, 0), 0)

    def next_map(i):
        return (jnp.minimum((i + 1) * hb, nhb - 1), 0)

    dcc = cc + cp
    row_cc = jax.ShapeDtypeStruct((1, cc), F32)
    out_shape = (jax.ShapeDtypeStruct((L, ci), BF16), jax.ShapeDtypeStruct((kw, cc), F32), row_cc, row_cc, row_cc,
                 jax.ShapeDtypeStruct((ng, gd, gd), F32), jax.ShapeDtypeStruct((1, cp), F32))
    acc_spec = [pl.BlockSpec((kw, cc), lambda i: (0, 0))] + [pl.BlockSpec((1, cc), lambda i: (0, 0))] * 3 + [
        pl.BlockSpec((ng, gd, gd), lambda i: (0, 0, 0)), pl.BlockSpec((1, cp), lambda i: (0, 0))]
    out, xo = _call(
        body, name=name, grid=(nt,),
        in_specs=[pl.BlockSpec((HALO, ci), prev_map), pl.BlockSpec((tl, ci), lambda i: (i, 0)),
                  pl.BlockSpec((tl, cc), lambda i: (i, 0)), pl.BlockSpec((HALO, cc), next_map),
                  pl.BlockSpec((tl, dcc), lambda i: (i, 0)), pl.BlockSpec((HALO, dcc), next_map),
                  full(ck), full(lg), full(lb), full(pw), full(ps), full(am)],
        out_specs=tuple([pl.BlockSpec((tl, ci), lambda i: (i, 0))] + acc_spec),
        out_shape=out_shape,
        scratch_shapes=[pltpu.VMEM((HALO + tl, cc), F32), pltpu.VMEM((HALO + tl, cp), F32),
                        pltpu.VMEM((tl + HALO, cc), F32), pltpu.VMEM((tl, cp), F32), pltpu.VMEM((tl + HALO, cp), F32),
                        pltpu.VMEM((kw, SUBLANES, cc), F32), pltpu.VMEM((SUBLANES, cc), F32),
                        pltpu.VMEM((SUBLANES, cc), F32), pltpu.VMEM((SUBLANES, cc), F32), pltpu.VMEM((SUBLANES, cp), F32)],
        sem=("arbitrary",), args=(z, z, u1, u1, dy, dy, ck, lg, lb, pw, ps, am), xchg=xchg)
    return out if xchg is None else (out, xo)


def _row_parts(nc, n=3):
    n = min(n, nc)
    cuts = [round(k * nc / n) for k in range(n + 1)]
    return [(cuts[k], cuts[k + 1]) for k in range(n)]


def _tap_rows(k_ref):
    return [jnp.broadcast_to(k_ref[j:j + 1, :], (SUBLANES, k_ref.shape[1])) for j in range(k_ref.shape[0])]


def _rows_of(tap, n):
    return tap if n == SUBLANES else jnp.concatenate([tap] * (n // SUBLANES), axis=0)


def _ffn_conv(win, taps, rows):
    kw = len(taps)
    o = FFN_PAD - (kw - 1)
    acc = _rows_of(taps[0], rows) * win[o:o + rows]
    for j in range(1, kw):
        acc = acc + _rows_of(taps[j], rows) * win[o + j:o + j + rows]
    return acc


def _ffn_block_fwd(h_mid, g, wup_t, kf, wdown, name, xchg=None):
    L, D = h_mid.shape
    f = wdown.shape[0]
    kw = kf.shape[0]
    tl = _token_tile(L)
    tc = _divisor(f, 256, 128)
    nj = f // tc
    nt = L // tl
    pad = 2 * SUBLANES
    hb = tl // pad
    rc = CONV3_ROWS
    parts = _row_parts(tl // rc)

    def body(hp_ref, hc_ref, g_ref, wg_ref, wv_ref, kg_ref, kv_ref, wd_ref, out_ref, hn_ref, ug_ref, act_ref, hn_halo, halo, acc):
        i = pl.program_id(0)
        kb = pl.program_id(1)

        @pl.when(kb == 0)
        def _():
            gg = g_ref[...]

            def norm(x):
                r = lax.rsqrt(jnp.mean(x * x, axis=-1, keepdims=True) + EPS)
                return ((x * r) * gg).astype(BF16)

            hn_halo[...] = jnp.where(i > 0, norm(hp_ref[...]), jnp.zeros((pad, D), BF16))
            hn_ref[...] = norm(hc_ref[...])
            acc[...] = jnp.zeros_like(acc)

        w_refs = (wg_ref, wv_ref)
        taps = (_tap_rows(kg_ref), _tap_rows(kv_ref))
        hh = hn_halo[...]
        for h in range(2):
            halo[h] = _dot_nt(hh, w_refs[h][...])[pad - FFN_PAD:]

        def up_part(lo, hi):
            a, b = lo * rc, hi * rc
            for h in range(2):
                ug_ref[h, a:b, :] = _dot_nt(hn_ref[a:b, :], w_refs[h][...])

        def down_part(lo, hi):
            a, b = lo * rc, hi * rc
            acc[a:b, :] += jnp.dot(act_ref[a:b, :], wd_ref[...], preferred_element_type=F32)

        def chunk_rows(lo, hi):
            for c in range(lo, hi):
                r0 = c * rc
                convd = []
                for h in range(2):
                    if c == 0:
                        win = jnp.concatenate([halo[h], ug_ref[h, 0:rc]], axis=0)
                    else:
                        win = ug_ref[h, r0 - FFN_PAD:r0 + rc]
                    convd.append(_ffn_conv(win, taps[h], rc))
                gate, val = convd
                act_ref[r0:r0 + rc, :] = ((gate * _sigmoid(gate)) * val).astype(BF16)

        for p, (lo, hi) in enumerate(parts):
            if p == 0:
                up_part(lo, hi)
            if p + 1 < len(parts):
                up_part(*parts[p + 1])
            if p > 0:
                down_part(*parts[p - 1])
            chunk_rows(lo, hi)
        down_part(*parts[-1])

        @pl.when(kb == nj - 1)
        def _():
            out_ref[...] = acc[...] + hc_ref[...]

    out, xo = _call(
        body, name=name, grid=(nt, nj),
        in_specs=[pl.BlockSpec((pad, D), lambda i, k: (jnp.maximum(i * hb - 1, 0), 0)),
                  pl.BlockSpec((tl, D), lambda i, k: (i, 0)),
                  pl.BlockSpec((1, D), lambda i, k: (0, 0)),
                  pl.BlockSpec((tc, D), lambda i, k: (k, 0)), pl.BlockSpec((tc, D), lambda i, k: (k + nj, 0)),
                  pl.BlockSpec((kw, tc), lambda i, k: (0, k)), pl.BlockSpec((kw, tc), lambda i, k: (0, k + nj)),
                  pl.BlockSpec((tc, D), lambda i, k: (k, 0))],
        out_specs=(pl.BlockSpec((tl, D), lambda i, k: (i, 0)), pl.BlockSpec((tl, D), lambda i, k: (i, 0)),
                   pl.BlockSpec((2, tl, tc), lambda i, k: (0, i, k)), pl.BlockSpec((tl, tc), lambda i, k: (i, k))),
        out_shape=(jax.ShapeDtypeStruct((L, D), F32), jax.ShapeDtypeStruct((L, D), BF16),
                   jax.ShapeDtypeStruct((2, L, f), F32), jax.ShapeDtypeStruct((L, f), BF16)),
        scratch_shapes=[pltpu.VMEM((pad, D), BF16), pltpu.VMEM((2, FFN_PAD, tc), F32), pltpu.VMEM((tl, D), F32)],
        sem=("parallel", "arbitrary"), args=(h_mid, h_mid, g, wup_t, wup_t, kf, kf, wdown), xchg=xchg)
    return out if xchg is None else (out, xo)


def _ffn_block_bwd(dh, h_mid, g, ug0, kf, wdown, wup_t, name, xchg=None):
    L, D = dh.shape
    f = ug0.shape[2]
    kw = kf.shape[0]
    tl = _token_tile(L)
    tc = _divisor(f, 256, 128)
    nj = f // tc
    nt = L // tl
    pad = 2 * SUBLANES
    hb, nhb = tl // FFN_PAD, L // FFN_PAD
    rc = CONV3_ROWS
    nc = tl // rc
    parts = _row_parts(nc)

    def body(dhc_ref, dhn_ref, hm_ref, g_ref, gp_ref, gc_ref, gn_ref, vp_ref, vc_ref, vn_ref, kg_ref, kv_ref,
             wd_ref, wg_ref, wv_ref, dhm_ref, dg_ref, du_ref, dk_ref, dh_ext, dact_s, acc):
        i = pl.program_id(0)
        kb = pl.program_id(1)

        @pl.when(kb == 0)
        def _():
            dh_ext[0:tl, :] = dhc_ref[...].astype(BF16)
            dh_ext[tl:tl + pad, :] = dhn_ref[...].astype(BF16)
            acc[...] = jnp.zeros_like(acc)

        @pl.when(jnp.logical_and(i == 0, kb == 0))
        def _():
            dg_ref[...] = jnp.zeros_like(dg_ref)
            dk_ref[...] = jnp.zeros_like(dk_ref)

        prev = (jnp.where(i > 0, gp_ref[...], 0.0), jnp.where(i > 0, vp_ref[...], 0.0))
        x_refs, nxt = (gc_ref, vc_ref), (gn_ref, vn_ref)
        taps = (_tap_rows(kg_ref), _tap_rows(kv_ref))
        dk = [[jnp.zeros((SUBLANES, tc), F32) for _ in range(kw)] for _ in range(2)]

        def dact_part(lo, hi):
            a, b = lo * rc, hi * rc + pad
            dact_s[a:b, :] = _dot_nt(dh_ext[a:b, :], wd_ref[...])

        def dhn_part(lo, hi):
            a, b = lo * rc, hi * rc
            acc[a:b, :] += (jnp.dot(du_ref[0, a:b, :], wg_ref[...], preferred_element_type=F32)
                            + jnp.dot(du_ref[1, a:b, :], wv_ref[...], preferred_element_type=F32))

        for p, (lo, hi) in enumerate(parts):
            if p == 0:
                dact_part(lo, hi)
            if p + 1 < len(parts):
                dact_part(*parts[p + 1])
            if p > 0:
                dhn_part(*parts[p - 1])
            chunk_rows(lo, hi, prev, x_refs, nxt, taps, dk, i, dact_s, du_ref)
        dhn_part(*parts[-1])
        for h in range(2):
            for j in range(kw):
                dk_ref[kb, h, j:j + 1, :] += jnp.sum(dk[h][j], axis=0, keepdims=True)

        @pl.when(kb == nj - 1)
        def _():
            x = hm_ref[...]
            r = lax.rsqrt(jnp.mean(x * x, axis=-1, keepdims=True) + EPS)
            xhat = x * r
            dhn = acc[...]
            dxhat = dhn * g_ref[...]
            dhm_ref[...] = dhc_ref[...] + r * (dxhat - xhat * jnp.mean(dxhat * xhat, axis=-1, keepdims=True))
            dg_ref[...] += jnp.sum(_rowsum8(dhn * xhat), axis=0, keepdims=True)

    def chunk_rows(lo, hi, prev, x_refs, nxt, taps, dk, i, dact_s, du_ref):
        for c in range(lo, hi):
            r0 = c * rc
            n = rc + FFN_PAD
            xs = []
            for h in range(2):
                parts = [prev[h] if c == 0 else x_refs[h][r0 - FFN_PAD:r0]]
                if c == nc - 1:
                    parts += [x_refs[h][r0:r0 + rc], nxt[h][...]]
                else:
                    parts += [x_refs[h][r0:r0 + n]]
                xs.append(jnp.concatenate(parts, axis=0))
            gate = _ffn_conv(xs[0], taps[0], n)
            val = _ffn_conv(xs[1], taps[1], n)
            dact = dact_s[r0:r0 + n, :]
            sg = _sigmoid(gate)
            dcs = [dact * val * (sg * (1.0 + gate * (1.0 - sg))), dact * (gate * sg)]
            if c == nc - 1:
                live = jnp.logical_or(lax.broadcasted_iota(jnp.int32, (n, 1), 0) < rc, i < nt - 1)
                dcs = [jnp.where(live, d, 0.0) for d in dcs]
            for h in range(2):
                xc = xs[h][FFN_PAD:FFN_PAD + rc]
                dx = None
                for j in range(kw):
                    o = kw - 1 - j
                    sh = dcs[h][o:o + rc]
                    term = _rows_of(taps[h][j], rc) * sh
                    dx = term if dx is None else dx + term
                    dk[h][j] = dk[h][j] + _rowsum8(xc * sh)
                du_ref[h, r0:r0 + rc, :] = dx.astype(BF16)

    def prev8(i, k):
        return (jnp.maximum(i * hb - 1, 0), k)

    def next8(i, k):
        return (jnp.minimum((i + 1) * hb, nhb - 1), k)

    def half(h, rows, idx):
        return pl.BlockSpec((None, rows, tc), lambda i, k: (h,) + idx(i, k))

    def tile(i, k):
        return (i, k)

    out, xo = _call(
        body, name=name, grid=(nt, nj),
        in_specs=[pl.BlockSpec((tl, D), lambda i, k: (i, 0)),
                  pl.BlockSpec((pad, D), lambda i, k: (jnp.minimum((i + 1) * (tl // pad), L // pad - 1), 0)),
                  pl.BlockSpec((tl, D), lambda i, k: (i, 0)), pl.BlockSpec((1, D), lambda i, k: (0, 0)),
                  half(0, FFN_PAD, prev8), half(0, tl, tile), half(0, FFN_PAD, next8),
                  half(1, FFN_PAD, prev8), half(1, tl, tile), half(1, FFN_PAD, next8),
                  pl.BlockSpec((kw, tc), lambda i, k: (0, k)), pl.BlockSpec((kw, tc), lambda i, k: (0, k + nj)),
                  pl.BlockSpec((tc, D), lambda i, k: (k, 0)),
                  pl.BlockSpec((tc, D), lambda i, k: (k, 0)), pl.BlockSpec((tc, D), lambda i, k: (k + nj, 0))],
        out_specs=(pl.BlockSpec((tl, D), lambda i, k: (i, 0)), pl.BlockSpec((1, D), lambda i, k: (0, 0)),
                   pl.BlockSpec((2, tl, tc), lambda i, k: (0, i, k)),
                   pl.BlockSpec((nj, 2, kw, tc), lambda i, k: (0, 0, 0, 0))),
        out_shape=(jax.ShapeDtypeStruct((L, D), F32), jax.ShapeDtypeStruct((1, D), F32),
                   jax.ShapeDtypeStruct((2, L, f), BF16), jax.ShapeDtypeStruct((nj, 2, kw, tc), F32)),
        scratch_shapes=[pltpu.VMEM((tl + pad, D), BF16), pltpu.VMEM((tl + pad, tc), F32), pltpu.VMEM((tl, D), F32)],
        sem=("arbitrary", "arbitrary"), args=(dh, dh, h_mid, g, ug0, ug0, ug0, ug0, ug0, ug0, kf, kf, wdown, wup_t, wup_t),
        xchg=xchg)
    return out if xchg is None else (out, xo)


def _adamw_math(w, g, m, v):
    m = ADAM_B1 * m + (1.0 - ADAM_B1) * g
    v = ADAM_B2 * v + (1.0 - ADAM_B2) * (g * g)
    m_hat = m / (1.0 - ADAM_B1 ** ADAM_STEP)
    v_hat = v / (1.0 - ADAM_B2 ** ADAM_STEP)
    delta = -ADAM_LR * (m_hat / (jnp.sqrt(v_hat) + ADAM_EPS) + ADAM_WD * w)
    return delta, m, v


def _sum_parts(parts_ref, idx):
    g = parts_ref[(0,) + idx].astype(F32)
    for q in range(1, N_DEV):
        g = g + parts_ref[(q,) + idx].astype(F32)
    return g


def _adamw_big(parts, w, m, v, name):
    nl, R, C = w.shape
    tr = _divisor(R, 256, 2 * SUBLANES)

    def body(*refs):
        p_refs = refs[:nl]
        w_ref, m_ref, v_ref, g_ref, d_ref, nm_ref, nv_ref = refs[nl:]
        layer = pl.program_id(0)
        for k in range(nl):
            @pl.when(layer == k)
            def _(k=k):
                g = _sum_parts(p_refs[k], ())
                d, nm, nv = _adamw_math(w_ref[0], g, m_ref[0], v_ref[0])
                g_ref[0] = g
                d_ref[0] = d
                nm_ref[0] = nm
                nv_ref[0] = nv

    def part_spec(k):
        return pl.BlockSpec((N_DEV, tr, C), lambda l, r: (0, jnp.where(l == k, r, 0), 0))

    blk = pl.BlockSpec((1, tr, C), lambda l, r: (l, r, 0))
    shp = jax.ShapeDtypeStruct((nl, R, C), F32)
    return pl.pallas_call(
        body, name=name, grid=(nl, R // tr),
        in_specs=[part_spec(k) for k in range(nl)] + [blk, blk, blk],
        out_specs=(blk, blk, blk, blk), out_shape=(shp, shp, shp, shp),
        compiler_params=_params(("arbitrary", "arbitrary")),
    )(*parts, w, m, v)


def _adamw_small(entries, name):
    n = len(entries)
    uniq = []
    for e in entries:
        if not any(e[0] is u for u in uniq):
            uniq.append(e[0])
    pidx = [next(k for k, u in enumerate(uniq) if u is e[0]) for e in entries]
    npart = len(uniq)

    def body(*refs):
        p_refs = refs[:npart]
        wmv = refs[npart:npart + 3 * n]
        outs = refs[npart + 3 * n:]
        for t, e in enumerate(entries):
            lo, w = e[1], e[2]
            rows = w.shape[0]
            pr = p_refs[pidx[t]]
            g = pr[0, lo:lo + rows].astype(F32)
            for q in range(1, N_DEV):
                g = g + pr[q, lo:lo + rows].astype(F32)
            d, nm, nv = _adamw_math(wmv[3 * t][...], g, wmv[3 * t + 1][...], wmv[3 * t + 2][...])
            outs[4 * t][...] = g
            outs[4 * t + 1][...] = d
            outs[4 * t + 2][...] = nm
            outs[4 * t + 3][...] = nv

    vm = pl.BlockSpec(memory_space=pltpu.VMEM)
    args = list(uniq)
    out_shape = []
    for e in entries:
        args += [e[2], e[3], e[4]]
        out_shape += [jax.ShapeDtypeStruct(e[2].shape, F32)] * 4
    res = pl.pallas_call(
        body, name=name, in_specs=[vm] * len(args), out_specs=tuple([vm] * len(out_shape)),
        out_shape=tuple(out_shape), compiler_params=_params(),
    )(*args)
    return [tuple(res[4 * t:4 * t + 4]) for t in range(n)]


def _head_matrix(cc):
    bw = min(256, cc)
    r = lax.broadcasted_iota(jnp.int32, (bw, bw), 0) // HEAD_DIM
    c = lax.broadcasted_iota(jnp.int32, (bw, bw), 1) // HEAD_DIM
    return jnp.where(r == c, 1.0 / HEAD_DIM, 0.0).astype(BF16)


def _cols_from_shards(g):
    nd = g.ndim
    perm = tuple(range(1, nd - 1)) + (0, nd - 1)
    t = jnp.transpose(g, perm)
    return t.reshape(t.shape[:-2] + (t.shape[-2] * t.shape[-1],))


def _cols_to_shards(a):
    nd = a.ndim
    t = a.reshape(a.shape[:-1] + (N_DEV, a.shape[-1] // N_DEV))
    perm = (nd - 1,) + tuple(range(nd - 1)) + (nd,)
    return jnp.transpose(t, perm)


def kernel(x, meta_tokens, norm1_g, w_in, conv_dw_k, conv_dw_b, conv_ln_g, conv_ln_b, pool_w, pool_scale, w_out, norm2_g, w_up, ffn_dw_k, w_down, final_g, loss_target, m_meta_tokens, m_norm1_g, m_w_in, m_conv_dw_k, m_conv_dw_b, m_conv_ln_g, m_conv_ln_b, m_pool_w, m_pool_scale, m_w_out, m_norm2_g, m_w_up, m_ffn_dw_k, m_w_down, m_final_g, v_meta_tokens, v_norm1_g, v_w_in, v_conv_dw_k, v_conv_dw_b, v_conv_ln_g, v_conv_ln_b, v_pool_w, v_pool_scale, v_w_out, v_norm2_g, v_w_up, v_ffn_dw_k, v_w_down, v_final_g):
    depth, D = norm1_g.shape
    n_meta = meta_tokens.shape[0]
    seq = x.shape[1]
    L = n_meta + seq
    cc = conv_dw_b.shape[1]
    ng, gd = pool_w.shape[1], pool_w.shape[2]
    f = w_down.shape[1] * N_DEV

    def rows(g):
        return g.reshape(-1, g.shape[-1])

    b16 = lambda a: a.astype(BF16)
    tr = lambda a: jnp.swapaxes(a, -1, -2)
    w_in_t, m_w_in_t, v_w_in_t = tr(w_in), tr(m_w_in), tr(v_w_in)
    w_up_t, m_w_up_t, v_w_up_t = tr(w_up), tr(m_w_up), tr(v_w_up)
    (g_in0, g_out0, g_ck, g_kf, g_meta) = _exchange([b16(w_in_t[0]), b16(w_out[0]), conv_dw_k, ffn_dw_k, meta_tokens],
                                                    ["gather"] * 5, "gather_first")
    ck_full = _cols_from_shards(g_ck)
    ck_rows = jnp.broadcast_to(ck_full[:, :, None, :], ck_full.shape[:2] + (SUBLANES, cc))
    kf_full = _cols_from_shards(g_kf)
    meta_full = _cols_from_shards(g_meta)
    am = _head_matrix(cc)
    win, wout, wup, wdown = [None] * depth, [None] * depth, [None] * depth, [None] * depth
    win[0] = rows(g_in0)
    wout[0] = rows(g_out0)

    h = jnp.concatenate([meta_full, x[0]], axis=0)
    saved = []
    for l in range(depth):
        more = l + 1 < depth
        if l == 0:
            (z, hn1), (g_down,) = _norm_proj(h, norm1_g[l:l + 1], win[l], f"in_proj_{l}", tn_cap=768,
                                             xchg=([b16(w_down[l])], ["gather"]))
            wdown[l] = rows(g_down)
            (ymix, u1), (g_up,) = _mixer_fwd(z, ck_rows[l], conv_dw_b[l:l + 1], conv_ln_g[l:l + 1], conv_ln_b[l:l + 1],
                                             pool_w[l], pool_scale[l:l + 1], am, f"mixer_fwd_{l}",
                                             xchg=([b16(w_up_t[l])], ["gather"]))
            wup[l] = rows(g_up)
        else:
            z, hn1 = _norm_proj(h, norm1_g[l:l + 1], win[l], f"in_proj_{l}", tn_cap=768)
            ymix, u1 = _mixer_fwd(z, ck_rows[l], conv_dw_b[l:l + 1], conv_ln_g[l:l + 1], conv_ln_b[l:l + 1], pool_w[l],
                                  pool_scale[l:l + 1], am, f"mixer_fwd_{l}")
        if more:
            h_mid, (g_in,) = _mm(ymix, wout[l], f"out_proj_{l}", res=h, tn_cap=512, xchg=([b16(w_in_t[l + 1])], ["gather"]))
            win[l + 1] = rows(g_in)
            nxt = [b16(w_out[l + 1]), b16(w_up_t[l + 1]), b16(w_down[l + 1])]
            (h_out, hn2, ug0, act), got = _ffn_block_fwd(h_mid, norm2_g[l:l + 1], wup[l], kf_full[l], wdown[l],
                                                         f"ffn_fwd_{l}", xchg=(nxt, ["gather"] * 3))
            wout[l + 1], wup[l + 1], wdown[l + 1] = rows(got[0]), rows(got[1]), rows(got[2])
        else:
            h_mid = _mm(ymix, wout[l], f"out_proj_{l}", res=h, tn_cap=512)
            h_out, hn2, ug0, act = _ffn_block_fwd(h_mid, norm2_g[l:l + 1], wup[l], kf_full[l], wdown[l], f"ffn_fwd_{l}")
        saved.append((h, hn1, z, u1, ymix, h_mid, hn2, ug0, act))
        h = h_out

    dh, d_final_g, loss_part = _loss_head(h, final_g.reshape(1, D), loss_target[0], n_meta, "loss_head")

    def row_shards(gm):
        return b16(gm.reshape(N_DEV, -1, gm.shape[-1]))

    gw = {k: [None] * depth for k in ("ck", "cb", "lg", "lb", "pw", "ps", "kf", "n1", "n2")}
    parts = {k: [None] * depth for k in ("in", "out", "up", "down")}
    for l in reversed(range(depth)):
        h_in, hn1, z, u1, ymix, h_mid, hn2, ug0, act = saved[l]
        g_down = _mm_tn(act, dh, f"down_proj_wgrad_{l}", tq_cap=512)
        (dh_mid, gw["n2"][l], dug0, dkf), (parts["down"][l],) = _ffn_block_bwd(
            dh, h_mid, norm2_g[l:l + 1], ug0, kf_full[l], wdown[l], wup[l], f"ffn_bwd_{l}",
            xchg=([row_shards(g_down)], ["a2a"]))
        gw["kf"][l] = jnp.transpose(dkf, (2, 1, 0, 3)).reshape(dkf.shape[2], -1)
        g_up_t = _mm_tn(dug0, hn2, f"up_proj_wgrad_{l}", halves=2, tq_cap=1024)
        dymix = _mm(dh_mid, wout[l], f"out_proj_bwd_{l}", b_t=True, tn_cap=512)
        g_out = _mm_tn(ymix, dh_mid, f"out_proj_wgrad_{l}", tq_cap=512)
        ((dz, gw["ck"][l], gw["cb"][l], gw["lg"][l], gw["lb"][l], gw["pw"][l], gw["ps"][l]),
         (parts["up"][l], parts["out"][l])) = _mixer_bwd(
            z, u1, dymix, ck_rows[l], conv_ln_g[l:l + 1], conv_ln_b[l:l + 1], pool_w[l], pool_scale[l:l + 1], am,
            f"mixer_bwd_{l}", xchg=([row_shards(g_up_t), row_shards(g_out)], ["a2a", "a2a"]))
        g_in_t = _mm_tn(dz, hn1, f"in_proj_wgrad_{l}", tq_cap=1024)
        (dh, gw["n1"][l]), (parts["in"][l],) = _proj_bwd_norm(dz, win[l], h_in, norm1_g[l:l + 1], dh_mid,
                                                              f"in_proj_bwd_{l}", xchg=([row_shards(g_in_t)], ["a2a"]))
    grad_x = dh[n_meta:][None]
    d_meta = dh[:n_meta]

    zero_row = jnp.zeros((1, D), F32)
    pack_d = jnp.concatenate(gw["n1"] + gw["n2"] + [d_final_g, jnp.broadcast_to(loss_part[:, :1], (1, D)), zero_row, zero_row], axis=0)
    pack_c = jnp.concatenate(gw["cb"] + gw["lg"] + gw["lb"] + gw["ps"], axis=0)
    pack_pw = jnp.stack(gw["pw"]).reshape(depth * ng * gd, gd)
    src = [_cols_to_shards(jnp.stack(gw["ck"])), _cols_to_shards(jnp.stack(gw["kf"])), _cols_to_shards(d_meta),
           pack_d, pack_c, pack_pw]
    r_ck, r_kf, r_meta, r_d, r_c, r_pw = _exchange(src, ["a2a"] * 3 + ["gather"] * 3, "exchange_small_grads")

    big = {
        "w_in": tuple(tr(a) for a in _adamw_big(parts["in"], w_in_t, m_w_in_t, v_w_in_t, "adamw_w_in")),
        "w_out": _adamw_big(parts["out"], w_out, m_w_out, v_w_out, "adamw_w_out"),
        "w_up": tuple(tr(a) for a in _adamw_big(parts["up"], w_up_t, m_w_up_t, v_w_up_t, "adamw_w_up")),
        "w_down": _adamw_big(parts["down"], w_down, m_w_down, v_w_down, "adamw_w_down"),
    }
    kwid = conv_dw_k.shape[1]
    fkw = ffn_dw_k.shape[1]
    row = lambda a: a.reshape(1, -1)
    entries = [
        (r_d, 0, norm1_g, m_norm1_g, v_norm1_g),
        (r_d, depth, norm2_g, m_norm2_g, v_norm2_g),
        (r_d, 2 * depth, row(final_g), row(m_final_g), row(v_final_g)),
        (r_c, 0, conv_dw_b, m_conv_dw_b, v_conv_dw_b),
        (r_c, depth, conv_ln_g, m_conv_ln_g, v_conv_ln_g),
        (r_c, 2 * depth, conv_ln_b, m_conv_ln_b, v_conv_ln_b),
        (r_c, 3 * depth, pool_scale, m_pool_scale, v_pool_scale),
        (r_pw, 0, pool_w.reshape(-1, gd), m_pool_w.reshape(-1, gd), v_pool_w.reshape(-1, gd)),
        (r_ck.reshape(N_DEV, depth * kwid, -1), 0, conv_dw_k.reshape(depth * kwid, -1),
         m_conv_dw_k.reshape(depth * kwid, -1), v_conv_dw_k.reshape(depth * kwid, -1)),
        (r_kf.reshape(N_DEV, depth * fkw, -1), 0, ffn_dw_k.reshape(depth * fkw, -1),
         m_ffn_dw_k.reshape(depth * fkw, -1), v_ffn_dw_k.reshape(depth * fkw, -1)),
        (r_meta, 0, meta_tokens, m_meta_tokens, v_meta_tokens),
        (r_d, 2 * depth + 1, zero_row, zero_row, zero_row),
    ]
    small = _adamw_small(entries, "adamw_small")
    names = ["norm1_g", "norm2_g", "final_g", "conv_dw_b", "conv_ln_g", "conv_ln_b", "pool_scale", "pool_w",
             "conv_dw_k", "ffn_dw_k", "meta_tokens"]
    shapes = {"final_g": final_g.shape, "pool_w": pool_w.shape, "conv_dw_k": conv_dw_k.shape, "ffn_dw_k": ffn_dw_k.shape}
    res = dict(big)
    for nme, quad in zip(names, small[:-1]):
        res[nme] = tuple(a.reshape(shapes[nme]) if nme in shapes else a for a in quad)
    loss = small[-1][0][0, 0]

    order = ["meta_tokens", "norm1_g", "w_in", "conv_dw_k", "conv_dw_b", "conv_ln_g", "conv_ln_b", "pool_w", "pool_scale",
             "w_out", "norm2_g", "w_up", "ffn_dw_k", "w_down", "final_g"]
    return (loss, grad_x, *[res[k][0] for k in order], *[res[k][1] for k in order], *[res[k][2] for k in order],
            *[res[k][3] for k in order])
```

```python
import functools

import jax
import jax.numpy as jnp
from jax import lax
from jax.experimental import pallas as pl
from jax.experimental.pallas import tpu as pltpu

F32 = jnp.float32
BF16 = jnp.bfloat16

EPS = 1e-6
HEAD_DIM = 64
POOL_WINDOWS = (2, 4, 8, 16)
ADAM_LR = 0.001
ADAM_B1 = 0.9
ADAM_B2 = 0.999
ADAM_EPS = 1e-08
ADAM_WD = 0.01
ADAM_STEP = 10

N_DEV = 8
OTHER_CHIPS = (2, 4, 6)
SUBLANES = 8
HALO = 48
CONV_PAD = 32
POOL_PAD = 16
FFN_PAD = 8
ROW_CHUNK = 24
CONV3_ROWS = 48
MAX_TILE_ROWS = 1024
WGRAD_TILE_ROWS = 2816
VMEM_LIMIT = 52 * 1024 * 1024


def _divisor(n, cap, mult):
    best = None
    for d in range(mult, min(n, cap) + 1, mult):
        if n % d == 0:
            best = d
    return n if best is None else best


def _token_tile(L):
    return _divisor(L, MAX_TILE_ROWS, HALO)


def _row_tile(L):
    return _divisor(L, 320, 2 * SUBLANES)


def _stat_rows(tl):
    return _divisor(tl, 256, SUBLANES)


def _params(sem=None):
    return pltpu.CompilerParams(dimension_semantics=sem, vmem_limit_bytes=VMEM_LIMIT)


def _rowsum8(x):
    acc = x[0:SUBLANES]
    for k in range(1, x.shape[0] // SUBLANES):
        acc = acc + x[k * SUBLANES:(k + 1) * SUBLANES]
    return acc


def _sigmoid(x):
    return jax.nn.sigmoid(x)


def _dot_nt(a, b):
    return lax.dot_general(a, b, (((1,), (1,)), ((), ())), preferred_element_type=F32)


def _head_mean(x, am_ref):
    bw = am_ref.shape[0]
    am = am_ref[...]
    outs = []
    for blk in range(x.shape[1] // bw):
        xb = x[:, blk * bw:(blk + 1) * bw]
        hi = xb.astype(BF16)
        lo = (xb - hi.astype(F32)).astype(BF16)
        outs.append(jnp.dot(hi, am, preferred_element_type=F32) + jnp.dot(lo, am, preferred_element_type=F32))
    return outs[0] if len(outs) == 1 else jnp.concatenate(outs, axis=-1)


def _xchg_out_shapes(srcs, modes):
    out = []
    for s, m in zip(srcs, modes):
        shp = ((N_DEV,) + tuple(s.shape)) if m == "gather" else tuple(s.shape)
        out.append(jax.ShapeDtypeStruct(shp, s.dtype))
    return out


def _xchg_sems(n):
    return [pltpu.SemaphoreType.DMA((n, N_DEV - 1)), pltpu.SemaphoreType.DMA((n, N_DEV - 1)), pltpu.SemaphoreType.DMA((n,))]


def _xchg_ops(src_refs, out_refs, sems, modes):
    n = len(src_refs)
    send_sems, recv_sems, local_sems = sems
    x, y, c = lax.axis_index("x"), lax.axis_index("y"), lax.axis_index("c")
    me = 4 * x + 2 * y + c

    def peer(d):
        return (x ^ ((d >> 2) & 1), y ^ ((d >> 1) & 1), c ^ (d & 1))

    def peer_id(d):
        px, py, pc = peer(d)
        return 4 * px + 2 * py + pc

    def remote(t, d):
        src = src_refs[t] if modes[t] == "gather" else src_refs[t].at[peer_id(d)]
        return pltpu.make_async_remote_copy(
            src_ref=src, dst_ref=out_refs[t].at[me], send_sem=send_sems.at[t, d - 1], recv_sem=recv_sems.at[t, d - 1],
            device_id=peer(d), device_id_type=pl.DeviceIdType.MESH)

    def arrival(t, d):
        src = src_refs[t] if modes[t] == "gather" else src_refs[t].at[me]
        return pltpu.make_async_remote_copy(
            src_ref=src, dst_ref=out_refs[t].at[peer_id(d)], send_sem=send_sems.at[t, d - 1],
            recv_sem=recv_sems.at[t, d - 1], device_id=peer(d), device_id_type=pl.DeviceIdType.MESH)

    def passed_on(t, d):
        blk = out_refs[t].at[peer_id(d)]
        return pltpu.make_async_remote_copy(
            src_ref=blk, dst_ref=blk, send_sem=send_sems.at[t, d], recv_sem=recv_sems.at[t, d],
            device_id=peer(1), device_id_type=pl.DeviceIdType.MESH)

    def local(t):
        src = src_refs[t] if modes[t] == "gather" else src_refs[t].at[me]
        return pltpu.make_async_copy(src, out_refs[t].at[me], local_sems.at[t])

    def sent_first(t):
        return OTHER_CHIPS + (1,) if modes[t] == "gather" else tuple(range(1, N_DEV))

    def start():
        for t in range(n):
            local(t).start()
        for t in range(n):
            for d in sent_first(t):
                remote(t, d).start()

    def wait():
        gathered = [t for t in range(n) if modes[t] == "gather"]
        for t in gathered:
            for d in OTHER_CHIPS:
                arrival(t, d).wait_recv()
                passed_on(t, d).start()
        for t in range(n):
            for d in range(1, N_DEV):
                if not (modes[t] == "gather" and d in OTHER_CHIPS):
                    arrival(t, d).wait_recv()
        for t in range(n):
            for d in sent_first(t):
                remote(t, d).wait_send()
        for t in gathered:
            for d in OTHER_CHIPS:
                passed_on(t, d).wait_send()
        for t in range(n):
            local(t).wait()

    return start, wait


def _exchange(srcs, modes, name):
    n = len(srcs)

    def body(*refs):
        start, wait = _xchg_ops(refs[:n], refs[n:2 * n], refs[2 * n:], modes)
        start()
        wait()

    any_spec = pl.BlockSpec(memory_space=pl.ANY)
    return pl.pallas_call(
        body, name=name, out_shape=tuple(_xchg_out_shapes(srcs, modes)),
        in_specs=[any_spec] * n, out_specs=tuple([any_spec] * n),
        scratch_shapes=_xchg_sems(n),
        compiler_params=pltpu.CompilerParams(has_side_effects=True),
    )(*srcs)


def _call(body, *, name, grid, in_specs, out_specs, out_shape, args, scratch_shapes=(), sem=None, xchg=None):
    single = not isinstance(out_shape, (tuple, list))
    outs_shape = [out_shape] if single else list(out_shape)
    outs_spec = [out_specs] if single else list(out_specs)
    if xchg is None:
        res = pl.pallas_call(
            body, name=name, grid=grid, in_specs=list(in_specs), out_specs=out_specs, out_shape=out_shape,
            scratch_shapes=list(scratch_shapes), compiler_params=_params(sem))(*args)
        return res, ()
    srcs, modes = xchg
    n_in, n_out, n_scr, nx = len(in_specs), len(outs_shape), len(scratch_shapes), len(srcs)

    def wrapped(*refs):
        ins = refs[:n_in]
        xs = refs[n_in:n_in + nx]
        o0 = n_in + nx
        outs = refs[o0:o0 + n_out]
        xo = refs[o0 + n_out:o0 + n_out + nx]
        s0 = o0 + n_out + nx
        scr = refs[s0:s0 + n_scr]
        start, wait = _xchg_ops(xs, xo, refs[s0 + n_scr:], modes)
        first = functools.reduce(jnp.logical_and, [pl.program_id(a) == 0 for a in range(len(grid))])
        last = functools.reduce(jnp.logical_and, [pl.program_id(a) == grid[a] - 1 for a in range(len(grid))])

        @pl.when(first)
        def _():
            start()

        body(*ins, *outs, *scr)

        @pl.when(last)
        def _():
            wait()

    any_spec = pl.BlockSpec(memory_space=pl.ANY)
    res = pl.pallas_call(
        wrapped, name=name, grid=grid, in_specs=list(in_specs) + [any_spec] * nx,
        out_specs=tuple(outs_spec + [any_spec] * nx), out_shape=tuple(outs_shape + _xchg_out_shapes(srcs, modes)),
        scratch_shapes=list(scratch_shapes) + _xchg_sems(nx),
        compiler_params=_params(("arbitrary",) * len(grid)))(*args, *srcs)
    comp = res[:n_out]
    return (comp[0] if single else tuple(comp)), tuple(res[n_out:])


def _norm_proj(h, g, w, name, *, tn_cap, xchg=None):
    L, D = h.shape
    N = w.shape[0]
    tm = _token_tile(L)
    tn = _divisor(N, tn_cap, 128)

    def body(h_ref, g_ref, w_ref, z_ref, hn_ref):
        @pl.when(pl.program_id(1) == 0)
        def _():
            x = h_ref[...]
            r = lax.rsqrt(jnp.mean(x * x, axis=-1, keepdims=True) + EPS)
            hn_ref[...] = ((x * r) * g_ref[...]).astype(BF16)

        z_ref[...] = _dot_nt(hn_ref[...], w_ref[...])

    out, xo = _call(
        body, name=name, grid=(L // tm, N // tn),
        in_specs=[pl.BlockSpec((tm, D), lambda i, j: (i, 0)), pl.BlockSpec((1, D), lambda i, j: (0, 0)),
                  pl.BlockSpec((tn, D), lambda i, j: (j, 0))],
        out_specs=(pl.BlockSpec((tm, tn), lambda i, j: (i, j)), pl.BlockSpec((tm, D), lambda i, j: (i, 0))),
        out_shape=(jax.ShapeDtypeStruct((L, N), F32), jax.ShapeDtypeStruct((L, D), BF16)),
        sem=("parallel", "arbitrary"), args=(h, g, w), xchg=xchg)
    return out if xchg is None else (out, xo)


def _proj_bwd_norm(a, b, h, g, dres, dg0, name, skip=0, xchg=None):
    L, K = a.shape
    D = b.shape[1]
    rows = L - skip
    tm = _divisor(rows, MAX_TILE_ROWS, 2 * SUBLANES) if skip else _token_tile(L)

    def body(a_ref, b_ref, h_ref, g_ref, dres_ref, dg0_ref, dh_ref, dg_ref):
        i = pl.program_id(0)
        dhn = jnp.dot(a_ref[...], b_ref[...], preferred_element_type=F32)
        x = h_ref[...]
        r = lax.rsqrt(jnp.mean(x * x, axis=-1, keepdims=True) + EPS)
        xhat = x * r
        dxhat = dhn * g_ref[...]
        dh_ref[...] = dres_ref[...] + r * (dxhat - xhat * jnp.mean(dxhat * xhat, axis=-1, keepdims=True))
        part = jnp.sum(_rowsum8(dhn * xhat), axis=0, keepdims=True)

        @pl.when(i == 0)
        def _():
            dg_ref[...] = dg0_ref[...] + part

        @pl.when(i > 0)
        def _():
            dg_ref[...] += part

    def rows_of(cols):
        if not skip:
            return pl.BlockSpec((tm, cols), lambda i: (i, 0))
        return pl.BlockSpec((pl.Element(tm), pl.Element(cols)), lambda i: (pl.multiple_of(skip + i * tm, SUBLANES), 0))

    row = pl.BlockSpec((1, D), lambda i: (0, 0))
    out, xo = _call(
        body, name=name, grid=(rows // tm,),
        in_specs=[rows_of(K), pl.BlockSpec((K, D), lambda i: (0, 0)), rows_of(D), row, rows_of(D), row],
        out_specs=(pl.BlockSpec((tm, D), lambda i: (i, 0)), row),
        out_shape=(jax.ShapeDtypeStruct((rows, D), F32), jax.ShapeDtypeStruct((1, D), F32)),
        sem=("arbitrary",), args=(a, b, h, g, dres, dg0), xchg=xchg)
    return out if xchg is None else (out, xo)


def _mm(a, b, name, *, res=None, b_t=False, tn_cap=1408, xchg=None):
    M, K = a.shape
    N = b.shape[0] if b_t else b.shape[1]
    tm = _token_tile(M)
    tn = _divisor(N, tn_cap, 128)

    def body(*refs):
        a_ref, b_ref = refs[:2]
        r_ref, o_ref = (None, refs[2]) if res is None else (refs[2], refs[3])
        av = a_ref[...].astype(BF16)
        prod = _dot_nt(av, b_ref[...]) if b_t else jnp.dot(av, b_ref[...], preferred_element_type=F32)
        o_ref[...] = prod if r_ref is None else prod + r_ref[...]

    b_spec = pl.BlockSpec((tn, K), lambda i, j: (j, 0)) if b_t else pl.BlockSpec((K, tn), lambda i, j: (0, j))
    in_specs = [pl.BlockSpec((tm, K), lambda i, j: (i, 0)), b_spec]
    args = [a, b]
    if res is not None:
        in_specs.append(pl.BlockSpec((tm, tn), lambda i, j: (i, j)))
        args.append(res)
    out, xo = _call(
        body, name=name, grid=(M // tm, N // tn), in_specs=in_specs,
        out_specs=pl.BlockSpec((tm, tn), lambda i, j: (i, j)), out_shape=jax.ShapeDtypeStruct((M, N), F32),
        sem=("parallel", "parallel"), args=args, xchg=xchg)
    return out if xchg is None else (out, xo)


def _mm_tn(a, b, name, *, halves=1, tq_cap=1408):
    L, Q = b.shape
    ph = a.shape[-1]
    P = ph * halves
    tl = _divisor(L, WGRAD_TILE_ROWS, HALO)
    tp = _divisor(ph, 1408, 128)
    tq = _divisor(Q, tq_cap, 128)
    pper = ph // tp
    nl = L // tl
    grid = (P // tp, Q // tq, nl)

    def body(a_ref, b_ref, o_ref, acc):
        prod = lax.dot_general(a_ref[...].astype(BF16), b_ref[...].astype(BF16), (((0,), (0,)), ((), ())),
                               preferred_element_type=F32)
        l = pl.program_id(2)
        if nl == 1:
            o_ref[...] = prod.astype(BF16)
            return

        @pl.when(l == 0)
        def _():
            acc[...] = prod

        @pl.when(jnp.logical_and(l > 0, l < nl - 1))
        def _():
            acc[...] += prod

        @pl.when(l == nl - 1)
        def _():
            o_ref[...] = (acc[...] + prod).astype(BF16)

    if halves > 1:
        a_spec = pl.BlockSpec((None, tl, tp), lambda p, q, l: (p // pper, l, p % pper))
    else:
        a_spec = pl.BlockSpec((tl, tp), lambda p, q, l: (l, p))
    return pl.pallas_call(
        body, name=name, grid=grid,
        in_specs=[a_spec, pl.BlockSpec((tl, tq), lambda p, q, l: (l, q))],
        out_specs=pl.BlockSpec((tp, tq), lambda p, q, l: (p, q)),
        out_shape=jax.ShapeDtypeStruct((P, Q), BF16),
        scratch_shapes=[pltpu.VMEM((tp, tq), F32)],
        compiler_params=_params(("parallel", "parallel", "arbitrary")),
    )(a, b)


def _loss_head(h, g, tgt, n_meta, name):
    L, D = h.shape
    tl = _row_tile(L)
    nt = L // tl

    def body(h_ref, g_ref, t_ref, dh_ref, dg_ref, loss_ref):
        i = pl.program_id(0)
        x = h_ref[...]
        r = lax.rsqrt(jnp.mean(x * x, axis=-1, keepdims=True) + EPS)
        xhat = x * r
        gg = g_ref[...]
        y = xhat * gg
        rows = i * tl + lax.broadcasted_iota(jnp.int32, (tl, 1), 0)
        t = t_ref[...]
        t = jnp.where(i == 0, pltpu.roll(t, n_meta, axis=0), t)
        err = jnp.where(rows >= n_meta, y - t, 0.0)
        dy = err * (1.0 / D)
        dxhat = dy * gg
        dh_ref[...] = r * (dxhat - xhat * jnp.mean(dxhat * xhat, axis=-1, keepdims=True))
        dg_part = jnp.sum(_rowsum8(dy * xhat), axis=0, keepdims=True)
        per_row = jnp.mean(err * err, axis=-1, keepdims=True)
        loss_part = jnp.broadcast_to(0.5 * jnp.sum(per_row, axis=0, keepdims=True), (1, 128))

        @pl.when(i == 0)
        def _():
            dg_ref[...] = dg_part
            loss_ref[...] = loss_part

        @pl.when(i > 0)
        def _():
            dg_ref[...] += dg_part
            loss_ref[...] += loss_part

    tile = pl.BlockSpec((tl, D), lambda i: (i, 0))
    row = pl.BlockSpec((1, D), lambda i: (0, 0))
    window = pl.BlockSpec((pl.Element(tl), pl.Element(D)),
                          lambda i: (pl.multiple_of(jnp.maximum(i * tl - n_meta, 0), SUBLANES), 0))
    return pl.pallas_call(
        body, name=name, grid=(nt,), in_specs=[tile, row, window],
        out_specs=(tile, row, pl.BlockSpec((1, 128), lambda i: (0, 0))),
        out_shape=(jax.ShapeDtypeStruct((L, D), F32), jax.ShapeDtypeStruct((1, D), F32),
                   jax.ShapeDtypeStruct((1, 128), F32)),
        compiler_params=_params(("arbitrary",)),
    )(h, g, tgt)


def _pool_fwd_block(pwin, pw_ref, row0, rb, g, gd, w, t0):
    wv = pwin[pl.ds(row0 + HALO - POOL_PAD, rb + POOL_PAD), g * gd:(g + 1) * gd]
    s = wv
    sh = 1
    while sh < w:
        s = s + pltpu.roll(s, sh, axis=0)
        sh *= 2
    win = s[POOL_PAD:POOL_PAD + rb]
    pt = wv[POOL_PAD:POOL_PAD + rb]
    tg = t0 + lax.broadcasted_iota(jnp.int32, (rb, 1), 0)
    cnt = jnp.minimum(tg + 1, w).astype(F32)
    return win / cnt - pt


def _fill_windows(i, zp_ref, zc_ref, u0w, pwin, tl, cc):
    keep = i > 0
    zp = zp_ref[...]
    u0w[0:HALO, :] = jnp.where(keep, zp[:, :cc] * _sigmoid(zp[:, cc:2 * cc]), 0.0)
    pwin[0:HALO, :] = jnp.where(keep, zp[:, 2 * cc:], 0.0)

    def fill(c, carry):
        b = pl.multiple_of(c * ROW_CHUNK, SUBLANES)
        zc = zc_ref[pl.ds(b, ROW_CHUNK), :]
        u0w[pl.ds(HALO + b, ROW_CHUNK), :] = zc[:, :cc] * _sigmoid(zc[:, cc:2 * cc])
        pwin[pl.ds(HALO + b, ROW_CHUNK), :] = zc[:, 2 * cc:]
        return carry

    lax.fori_loop(0, tl // ROW_CHUNK, fill, 0)


def _mixer_fwd(z, ck, cb, lg, lb, pw, ps, am, name, xchg=None):
    L, ci = z.shape
    kw, _, cc = ck.shape
    cp = ci - 2 * cc
    ng, gd = pw.shape[0], pw.shape[1]
    tl = _token_tile(L)
    nt = L // tl
    hb = tl // HALO
    rb = _stat_rows(tl)
    tap0 = CONV_PAD - (kw - 1)

    def body(zp_ref, zc_ref, ck_ref, cb_ref, lg_ref, lb_ref, pw_ref, ps_ref, am_ref, y_ref, u1_ref, u0w, pwin):
        i = pl.program_id(0)
        _fill_windows(i, zp_ref, zc_ref, u0w, pwin, tl, cc)

        def conv(c, carry):
            b = pl.multiple_of(c * ROW_CHUNK, SUBLANES)
            w = u0w[pl.ds(b + HALO - CONV_PAD, ROW_CHUNK + CONV_PAD), :]
            acc = jnp.broadcast_to(cb_ref[...], (ROW_CHUNK, cc))
            for j in range(kw):
                acc = acc + _rows_of(ck_ref[j], ROW_CHUNK) * w[tap0 + j:tap0 + j + ROW_CHUNK]
            u1_ref[pl.ds(b, ROW_CHUNK), :] = acc
            return carry

        lax.fori_loop(0, tl // ROW_CHUNK, conv, 0)

        def blocks(k, carry):
            b = pl.multiple_of(k * rb, SUBLANES)
            u1 = u1_ref[pl.ds(b, rb), :]
            xc = u1 - _head_mean(u1, am_ref)
            var = _head_mean(xc * xc, am_ref)
            u2 = (xc * lax.rsqrt(var + EPS)) * lg_ref[...] + lb_ref[...]
            y_ref[pl.ds(b, rb), 0:cc] = (u2 * _sigmoid(u2)).astype(y_ref.dtype)
            for g in range(ng):
                d = _pool_fwd_block(pwin, pw_ref, b, rb, g, gd, POOL_WINDOWS[g], i * tl + b)
                yp = jnp.dot(d.astype(BF16), pw_ref[g].astype(BF16), preferred_element_type=F32)
                yp = yp * ps_ref[:, g * gd:(g + 1) * gd]
                y_ref[pl.ds(b, rb), cc + g * gd:cc + (g + 1) * gd] = yp.astype(y_ref.dtype)
            return carry

        lax.fori_loop(0, tl // rb, blocks, 0)

    def full(a):
        nd = a.ndim
        return pl.BlockSpec(a.shape, lambda i: (0,) * nd)

    out, xo = _call(
        body, name=name, grid=(nt,),
        in_specs=[pl.BlockSpec((HALO, ci), lambda i: (jnp.maximum(i * hb - 1, 0), 0)),
                  pl.BlockSpec((tl, ci), lambda i: (i, 0)),
                  full(ck), full(cb), full(lg), full(lb), full(pw), full(ps), full(am)],
        out_specs=(pl.BlockSpec((tl, cc + cp), lambda i: (i, 0)), pl.BlockSpec((tl, cc), lambda i: (i, 0))),
        out_shape=(jax.ShapeDtypeStruct((L, cc + cp), BF16), jax.ShapeDtypeStruct((L, cc), F32)),
        scratch_shapes=[pltpu.VMEM((HALO + tl, cc), F32), pltpu.VMEM((HALO + tl, cp), F32)],
        sem=("parallel",), args=(z, z, ck, cb, lg, lb, pw, ps, am), xchg=xchg)
    return out if xchg is None else (out, xo)


def _mixer_bwd(z, u1, dy, ck, lg, lb, pw, ps, am, name, xchg=None):
    L, ci = z.shape
    kw, _, cc = ck.shape
    cp = ci - 2 * cc
    ng, gd = pw.shape[0], pw.shape[1]
    tl = _token_tile(L)
    nt = L // tl
    hb = tl // HALO
    rb = _stat_rows(tl)

    def body(zp_ref, zc_ref, u1c_ref, u1n_ref, dyc_ref, dyn_ref, ck_ref, lg_ref, lb_ref, pw_ref, ps_ref, am_ref,
             dz_ref, dck_ref, dcb_ref, dlg_ref, dlb_ref, dpw_ref, dps_ref,
             u0w, pwin, du1w, ddw, ew, dkacc, dcb8, dlg8, dlb8, dps8):
        i = pl.program_id(0)
        has_next = i < nt - 1

        @pl.when(i == 0)
        def _():
            dck_ref[...] = jnp.zeros_like(dck_ref)
            dcb_ref[...] = jnp.zeros_like(dcb_ref)
            dlg_ref[...] = jnp.zeros_like(dlg_ref)
            dlb_ref[...] = jnp.zeros_like(dlb_ref)
            dpw_ref[...] = jnp.zeros_like(dpw_ref)
            dps_ref[...] = jnp.zeros_like(dps_ref)

        dkacc[...] = jnp.zeros_like(dkacc)
        dcb8[...] = jnp.zeros_like(dcb8)
        dlg8[...] = jnp.zeros_like(dlg8)
        dlb8[...] = jnp.zeros_like(dlb8)
        dps8[...] = jnp.zeros_like(dps8)

        _fill_windows(i, zp_ref, zc_ref, u0w, pwin, tl, cc)

        def conv_side(u1, dyc, own):
            xc = u1 - _head_mean(u1, am_ref)
            rstd = lax.rsqrt(_head_mean(xc * xc, am_ref) + EPS)
            uh = xc * rstd
            lgv = lg_ref[...]
            u2 = uh * lgv + lb_ref[...]
            sg = _sigmoid(u2)
            du2 = dyc * (sg * (1.0 + u2 * (1.0 - sg)))
            if own:
                dlg8[...] += _rowsum8(du2 * uh)
                dlb8[...] += _rowsum8(du2)
            duh = du2 * lgv
            return rstd * (duh - _head_mean(duh, am_ref) - uh * _head_mean(duh * uh, am_ref))

        def pool_side(dyp, t0, rows):
            dds, es = [], []
            tg = t0 + lax.broadcasted_iota(jnp.int32, (rows, 1), 0)
            for g in range(ng):
                dypre = dyp[:, g * gd:(g + 1) * gd] * ps_ref[:, g * gd:(g + 1) * gd]
                dd = lax.dot_general(dypre.astype(BF16), pw_ref[g].astype(BF16), (((1,), (1,)), ((), ())),
                                     preferred_element_type=F32)
                cnt = jnp.minimum(tg + 1, POOL_WINDOWS[g]).astype(F32)
                dds.append(dd)
                es.append(dd / cnt)
            return jnp.concatenate(dds, axis=-1), jnp.concatenate(es, axis=-1)

        def blocks(k, carry):
            b = pl.multiple_of(k * rb, SUBLANES)
            dyb = dyc_ref[pl.ds(b, rb), :]
            du1 = conv_side(u1c_ref[pl.ds(b, rb), :], dyb[:, :cc], True)
            du1w[pl.ds(b, rb), :] = du1
            dcb8[...] += _rowsum8(du1)
            dyp = dyb[:, cc:]
            dd, e = pool_side(dyp, i * tl + b, rb)
            ddw[pl.ds(b, rb), :] = dd
            ew[pl.ds(b, rb), :] = e
            for g in range(ng):
                d = _pool_fwd_block(pwin, pw_ref, b, rb, g, gd, POOL_WINDOWS[g], i * tl + b)
                db16 = d.astype(BF16)
                dypg = dyp[:, g * gd:(g + 1) * gd]
                ypre = jnp.dot(db16, pw_ref[g].astype(BF16), preferred_element_type=F32)
                dps8[:, g * gd:(g + 1) * gd] += _rowsum8(dypg * ypre)
                dypre = (dypg * ps_ref[:, g * gd:(g + 1) * gd]).astype(BF16)
                dpw_ref[g] += lax.dot_general(db16, dypre, (((0,), (0,)), ((), ())), preferred_element_type=F32)
            return carry

        lax.fori_loop(0, tl // rb, blocks, 0)

        dyn = dyn_ref[...]
        du1n = conv_side(u1n_ref[...], dyn[:, :cc], False)
        du1w[tl:tl + HALO, :] = jnp.where(has_next, du1n, 0.0)
        ddn, en = pool_side(dyn[:, cc:], (i + 1) * tl, HALO)
        ew[tl:tl + HALO, :] = jnp.where(has_next, en, 0.0)

        def taps(c, carry):
            b = pl.multiple_of(c * ROW_CHUNK, SUBLANES)
            w = du1w[pl.ds(b, ROW_CHUNK + CONV_PAD), :]
            u0c = u0w[pl.ds(HALO + b, ROW_CHUNK), :]
            acc = jnp.zeros((ROW_CHUNK, cc), F32)
            for j in range(kw):
                o = kw - 1 - j
                sh = w[o:o + ROW_CHUNK]
                acc = acc + _rows_of(ck_ref[j], ROW_CHUNK) * sh
                dkacc[j] += _rowsum8(u0c * sh)
            zc = zc_ref[pl.ds(b, ROW_CHUNK), :]
            a = zc[:, :cc]
            sg = _sigmoid(zc[:, cc:2 * cc])
            dz_ref[pl.ds(b, ROW_CHUNK), 0:cc] = (acc * sg).astype(dz_ref.dtype)
            dz_ref[pl.ds(b, ROW_CHUNK), cc:2 * cc] = (acc * a * sg * (1.0 - sg)).astype(dz_ref.dtype)
            return carry

        lax.fori_loop(0, tl // ROW_CHUNK, taps, 0)

        def pool_back(k, carry):
            b = pl.multiple_of(k * rb, SUBLANES)
            n = rb + POOL_PAD
            for g in range(ng):
                s = ew[pl.ds(b, n), g * gd:(g + 1) * gd]
                sh = 1
                while sh < POOL_WINDOWS[g]:
                    s = s + pltpu.roll(s, n - sh, axis=0)
                    sh *= 2
                dp = s[0:rb] - ddw[pl.ds(b, rb), g * gd:(g + 1) * gd]
                dz_ref[pl.ds(b, rb), 2 * cc + g * gd:2 * cc + (g + 1) * gd] = dp.astype(dz_ref.dtype)
            return carry

        lax.fori_loop(0, tl // rb, pool_back, 0)

        dck_ref[...] += jnp.sum(dkacc[...], axis=1)
        dcb_ref[...] += jnp.sum(dcb8[...], axis=0, keepdims=True)
        dlg_ref[...] += jnp.sum(dlg8[...], axis=0, keepdims=True)
        dlb_ref[...] += jnp.sum(dlb8[...], axis=0, keepdims=True)
        dps_ref[...] += jnp.sum(dps8[...], axis=0, keepdims=True)

    def full(a):
        nd = a.ndim
        return pl.BlockSpec(a.shape, lambda i: (0,) * nd)

    nhb = L // HALO

    def prev_map(i):
        return (jnp.maximum(i * hb - 1, 0), 0)

    def next_map(i):
        return (jnp.minimum((i + 1) * hb, nhb - 1), 0)

    dcc = cc + cp
    row_cc = jax.ShapeDtypeStruct((1, cc), F32)
    out_shape = (jax.ShapeDtypeStruct((L, ci), BF16), jax.ShapeDtypeStruct((kw, cc), F32), row_cc, row_cc, row_cc,
                 jax.ShapeDtypeStruct((ng, gd, gd), F32), jax.ShapeDtypeStruct((1, cp), F32))
    acc_spec = [pl.BlockSpec((kw, cc), lambda i: (0, 0))] + [pl.BlockSpec((1, cc), lambda i: (0, 0))] * 3 + [
        pl.BlockSpec((ng, gd, gd), lambda i: (0, 0, 0)), pl.BlockSpec((1, cp), lambda i: (0, 0))]
    out, xo = _call(
        body, name=name, grid=(nt,),
        in_specs=[pl.BlockSpec((HALO, ci), prev_map), pl.BlockSpec((tl, ci), lambda i: (i, 0)),
                  pl.BlockSpec((tl, cc), lambda i: (i, 0)), pl.BlockSpec((HALO, cc), next_map),
                  pl.BlockSpec((tl, dcc), lambda i: (i, 0)), pl.BlockSpec((HALO, dcc), next_map),
                  full(ck), full(lg), full(lb), full(pw), full(ps), full(am)],
        out_specs=tuple([pl.BlockSpec((tl, ci), lambda i: (i, 0))] + acc_spec),
        out_shape=out_shape,
        scratch_shapes=[pltpu.VMEM((HALO + tl, cc), F32), pltpu.VMEM((HALO + tl, cp), F32),
                        pltpu.VMEM((tl + HALO, cc), F32), pltpu.VMEM((tl, cp), F32), pltpu.VMEM((tl + HALO, cp), F32),
                        pltpu.VMEM((kw, SUBLANES, cc), F32), pltpu.VMEM((SUBLANES, cc), F32),
                        pltpu.VMEM((SUBLANES, cc), F32), pltpu.VMEM((SUBLANES, cc), F32), pltpu.VMEM((SUBLANES, cp), F32)],
        sem=("arbitrary",), args=(z, z, u1, u1, dy, dy, ck, lg, lb, pw, ps, am), xchg=xchg)
    return out if xchg is None else (out, xo)


def _row_parts(nc, n=3):
    n = min(n, nc)
    cuts = [round(k * nc / n) for k in range(n + 1)]
    return [(cuts[k], cuts[k + 1]) for k in range(n)]


def _tap_rows(k_ref):
    return [jnp.broadcast_to(k_ref[j:j + 1, :], (SUBLANES, k_ref.shape[1])) for j in range(k_ref.shape[0])]


def _rows_of(tap, n):
    return tap if n == SUBLANES else jnp.concatenate([tap] * (n // SUBLANES), axis=0)


def _ffn_conv(win, taps, rows):
    kw = len(taps)
    o = FFN_PAD - (kw - 1)
    acc = _rows_of(taps[0], rows) * win[o:o + rows]
    for j in range(1, kw):
        acc = acc + _rows_of(taps[j], rows) * win[o + j:o + j + rows]
    return acc


def _ffn_block_fwd(h_mid, g, wup_t, kf, wdown, name, xchg=None):
    L, D = h_mid.shape
    f = wdown.shape[0]
    kw = kf.shape[0]
    tl = _token_tile(L)
    tc = _divisor(f, 256, 128)
    nj = f // tc
    nt = L // tl
    pad = 2 * SUBLANES
    hb = tl // pad
    rc = CONV3_ROWS
    parts = _row_parts(tl // rc)

    def body(hp_ref, hc_ref, g_ref, wg_ref, wv_ref, kg_ref, kv_ref, wd_ref, out_ref, hn_ref, ug_ref, act_ref, hn_halo, halo, acc):
        i = pl.program_id(0)
        kb = pl.program_id(1)

        @pl.when(kb == 0)
        def _():
            gg = g_ref[...]

            def norm(x):
                r = lax.rsqrt(jnp.mean(x * x, axis=-1, keepdims=True) + EPS)
                return ((x * r) * gg).astype(BF16)

            hn_halo[...] = jnp.where(i > 0, norm(hp_ref[...]), jnp.zeros((pad, D), BF16))
            hn_ref[...] = norm(hc_ref[...])
            acc[...] = jnp.zeros_like(acc)

        w_refs = (wg_ref, wv_ref)
        taps = (_tap_rows(kg_ref), _tap_rows(kv_ref))
        hh = hn_halo[...]
        for h in range(2):
            halo[h] = _dot_nt(hh, w_refs[h][...])[pad - FFN_PAD:]

        def up_part(lo, hi):
            a, b = lo * rc, hi * rc
            for h in range(2):
                ug_ref[h, a:b, :] = _dot_nt(hn_ref[a:b, :], w_refs[h][...])

        def down_part(lo, hi):
            a, b = lo * rc, hi * rc
            acc[a:b, :] += jnp.dot(act_ref[a:b, :], wd_ref[...], preferred_element_type=F32)

        def chunk_rows(lo, hi):
            for c in range(lo, hi):
                r0 = c * rc
                convd = []
                for h in range(2):
                    if c == 0:
                        win = jnp.concatenate([halo[h], ug_ref[h, 0:rc]], axis=0)
                    else:
                        win = ug_ref[h, r0 - FFN_PAD:r0 + rc]
                    convd.append(_ffn_conv(win, taps[h], rc))
                gate, val = convd
                act_ref[r0:r0 + rc, :] = ((gate * _sigmoid(gate)) * val).astype(BF16)

        for p, (lo, hi) in enumerate(parts):
            if p == 0:
                up_part(lo, hi)
            if p + 1 < len(parts):
                up_part(*parts[p + 1])
            if p > 0:
                down_part(*parts[p - 1])
            chunk_rows(lo, hi)
        down_part(*parts[-1])

        @pl.when(kb == nj - 1)
        def _():
            out_ref[...] = acc[...] + hc_ref[...]

    out, xo = _call(
        body, name=name, grid=(nt, nj),
        in_specs=[pl.BlockSpec((pad, D), lambda i, k: (jnp.maximum(i * hb - 1, 0), 0)),
                  pl.BlockSpec((tl, D), lambda i, k: (i, 0)),
                  pl.BlockSpec((1, D), lambda i, k: (0, 0)),
                  pl.BlockSpec((tc, D), lambda i, k: (k, 0)), pl.BlockSpec((tc, D), lambda i, k: (k + nj, 0)),
                  pl.BlockSpec((kw, tc), lambda i, k: (0, k)), pl.BlockSpec((kw, tc), lambda i, k: (0, k + nj)),
                  pl.BlockSpec((tc, D), lambda i, k: (k, 0))],
        out_specs=(pl.BlockSpec((tl, D), lambda i, k: (i, 0)), pl.BlockSpec((tl, D), lambda i, k: (i, 0)),
                   pl.BlockSpec((2, tl, tc), lambda i, k: (0, i, k)), pl.BlockSpec((tl, tc), lambda i, k: (i, k))),
        out_shape=(jax.ShapeDtypeStruct((L, D), F32), jax.ShapeDtypeStruct((L, D), BF16),
                   jax.ShapeDtypeStruct((2, L, f), F32), jax.ShapeDtypeStruct((L, f), BF16)),
        scratch_shapes=[pltpu.VMEM((pad, D), BF16), pltpu.VMEM((2, FFN_PAD, tc), F32), pltpu.VMEM((tl, D), F32)],
        sem=("parallel", "arbitrary"), args=(h_mid, h_mid, g, wup_t, wup_t, kf, kf, wdown), xchg=xchg)
    return out if xchg is None else (out, xo)


def _ffn_block_bwd(dh, h_mid, g, ug0, kf, wdown, wup_t, name, xchg=None):
    L, D = dh.shape
    f = ug0.shape[2]
    kw = kf.shape[0]
    tl = _token_tile(L)
    tc = _divisor(f, 256, 128)
    nj = f // tc
    nt = L // tl
    pad = 2 * SUBLANES
    hb, nhb = tl // FFN_PAD, L // FFN_PAD
    rc = CONV3_ROWS
    nc = tl // rc
    parts = _row_parts(nc)

    def body(dhc_ref, dhn_ref, hm_ref, g_ref, gp_ref, gc_ref, gn_ref, vp_ref, vc_ref, vn_ref, kg_ref, kv_ref,
             wd_ref, wg_ref, wv_ref, dhm_ref, dg_ref, du_ref, dk_ref, dh_ext, dact_s, acc):
        i = pl.program_id(0)
        kb = pl.program_id(1)

        @pl.when(kb == 0)
        def _():
            dh_ext[0:tl, :] = dhc_ref[...].astype(BF16)
            dh_ext[tl:tl + pad, :] = dhn_ref[...].astype(BF16)
            acc[...] = jnp.zeros_like(acc)

        @pl.when(jnp.logical_and(i == 0, kb == 0))
        def _():
            dg_ref[...] = jnp.zeros_like(dg_ref)
            dk_ref[...] = jnp.zeros_like(dk_ref)

        prev = (jnp.where(i > 0, gp_ref[...], 0.0), jnp.where(i > 0, vp_ref[...], 0.0))
        x_refs, nxt = (gc_ref, vc_ref), (gn_ref, vn_ref)
        taps = (_tap_rows(kg_ref), _tap_rows(kv_ref))
        dk = [[jnp.zeros((SUBLANES, tc), F32) for _ in range(kw)] for _ in range(2)]

        def dact_part(lo, hi):
            a, b = lo * rc, hi * rc + pad
            dact_s[a:b, :] = _dot_nt(dh_ext[a:b, :], wd_ref[...])

        def dhn_part(lo, hi):
            a, b = lo * rc, hi * rc
            acc[a:b, :] += (jnp.dot(du_ref[0, a:b, :], wg_ref[...], preferred_element_type=F32)
                            + jnp.dot(du_ref[1, a:b, :], wv_ref[...], preferred_element_type=F32))

        for p, (lo, hi) in enumerate(parts):
            if p == 0:
                dact_part(lo, hi)
            if p + 1 < len(parts):
                dact_part(*parts[p + 1])
            if p > 0:
                dhn_part(*parts[p - 1])
            chunk_rows(lo, hi, prev, x_refs, nxt, taps, dk, i, dact_s, du_ref)
        dhn_part(*parts[-1])
        for h in range(2):
            for j in range(kw):
                dk_ref[kb, h, j:j + 1, :] += jnp.sum(dk[h][j], axis=0, keepdims=True)

        @pl.when(kb == nj - 1)
        def _():
            x = hm_ref[...]
            r = lax.rsqrt(jnp.mean(x * x, axis=-1, keepdims=True) + EPS)
            xhat = x * r
            dhn = acc[...]
            dxhat = dhn * g_ref[...]
            dhm_ref[...] = dhc_ref[...] + r * (dxhat - xhat * jnp.mean(dxhat * xhat, axis=-1, keepdims=True))
            dg_ref[...] += jnp.sum(_rowsum8(dhn * xhat), axis=0, keepdims=True)

    def chunk_rows(lo, hi, prev, x_refs, nxt, taps, dk, i, dact_s, du_ref):
        for c in range(lo, hi):
            r0 = c * rc
            n = rc + FFN_PAD
            xs = []
            for h in range(2):
                parts = [prev[h] if c == 0 else x_refs[h][r0 - FFN_PAD:r0]]
                if c == nc - 1:
                    parts += [x_refs[h][r0:r0 + rc], nxt[h][...]]
                else:
                    parts += [x_refs[h][r0:r0 + n]]
                xs.append(jnp.concatenate(parts, axis=0))
            gate = _ffn_conv(xs[0], taps[0], n)
            val = _ffn_conv(xs[1], taps[1], n)
            dact = dact_s[r0:r0 + n, :]
            sg = _sigmoid(gate)
            dcs = [dact * val * (sg * (1.0 + gate * (1.0 - sg))), dact * (gate * sg)]
            if c == nc - 1:
                live = jnp.logical_or(lax.broadcasted_iota(jnp.int32, (n, 1), 0) < rc, i < nt - 1)
                dcs = [jnp.where(live, d, 0.0) for d in dcs]
            for h in range(2):
                xc = xs[h][FFN_PAD:FFN_PAD + rc]
                dx = None
                for j in range(kw):
                    o = kw - 1 - j
                    sh = dcs[h][o:o + rc]
                    term = _rows_of(taps[h][j], rc) * sh
                    dx = term if dx is None else dx + term
                    dk[h][j] = dk[h][j] + _rowsum8(xc * sh)
                du_ref[h, r0:r0 + rc, :] = dx.astype(BF16)

    def prev8(i, k):
        return (jnp.maximum(i * hb - 1, 0), k)

    def next8(i, k):
        return (jnp.minimum((i + 1) * hb, nhb - 1), k)

    def half(h, rows, idx):
        return pl.BlockSpec((None, rows, tc), lambda i, k: (h,) + idx(i, k))

    def tile(i, k):
        return (i, k)

    out, xo = _call(
        body, name=name, grid=(nt, nj),
        in_specs=[pl.BlockSpec((tl, D), lambda i, k: (i, 0)),
                  pl.BlockSpec((pad, D), lambda i, k: (jnp.minimum((i + 1) * (tl // pad), L // pad - 1), 0)),
                  pl.BlockSpec((tl, D), lambda i, k: (i, 0)), pl.BlockSpec((1, D), lambda i, k: (0, 0)),
                  half(0, FFN_PAD, prev8), half(0, tl, tile), half(0, FFN_PAD, next8),
                  half(1, FFN_PAD, prev8), half(1, tl, tile), half(1, FFN_PAD, next8),
                  pl.BlockSpec((kw, tc), lambda i, k: (0, k)), pl.BlockSpec((kw, tc), lambda i, k: (0, k + nj)),
                  pl.BlockSpec((tc, D), lambda i, k: (k, 0)),
                  pl.BlockSpec((tc, D), lambda i, k: (k, 0)), pl.BlockSpec((tc, D), lambda i, k: (k + nj, 0))],
        out_specs=(pl.BlockSpec((tl, D), lambda i, k: (i, 0)), pl.BlockSpec((1, D), lambda i, k: (0, 0)),
                   pl.BlockSpec((2, tl, tc), lambda i, k: (0, i, k)),
                   pl.BlockSpec((nj, 2, kw, tc), lambda i, k: (0, 0, 0, 0))),
        out_shape=(jax.ShapeDtypeStruct((L, D), F32), jax.ShapeDtypeStruct((1, D), F32),
                   jax.ShapeDtypeStruct((2, L, f), BF16), jax.ShapeDtypeStruct((nj, 2, kw, tc), F32)),
        scratch_shapes=[pltpu.VMEM((tl + pad, D), BF16), pltpu.VMEM((tl + pad, tc), F32), pltpu.VMEM((tl, D), F32)],
        sem=("arbitrary", "arbitrary"), args=(dh, dh, h_mid, g, ug0, ug0, ug0, ug0, ug0, ug0, kf, kf, wdown, wup_t, wup_t),
        xchg=xchg)
    return out if xchg is None else (out, xo)


def _adamw_math(w, g, m, v):
    m = ADAM_B1 * m + (1.0 - ADAM_B1) * g
    v = ADAM_B2 * v + (1.0 - ADAM_B2) * (g * g)
    m_hat = m / (1.0 - ADAM_B1 ** ADAM_STEP)
    v_hat = v / (1.0 - ADAM_B2 ** ADAM_STEP)
    delta = -ADAM_LR * (m_hat / (jnp.sqrt(v_hat) + ADAM_EPS) + ADAM_WD * w)
    return delta, m, v


def _sum_parts(parts_ref, idx):
    g = parts_ref[(0,) + idx].astype(F32)
    for q in range(1, N_DEV):
        g = g + parts_ref[(q,) + idx].astype(F32)
    return g


def _adamw_big(parts, w, m, v, name):
    nl, R, C = w.shape
    tr = _divisor(R, 256, 2 * SUBLANES)

    def body(*refs):
        p_refs = refs[:nl]
        w_ref, m_ref, v_ref, g_ref, d_ref, nm_ref, nv_ref = refs[nl:]
        layer = pl.program_id(0)
        for k in range(nl):
            @pl.when(layer == k)
            def _(k=k):
                g = _sum_parts(p_refs[k], ())
                d, nm, nv = _adamw_math(w_ref[0], g, m_ref[0], v_ref[0])
                g_ref[0] = g
                d_ref[0] = d
                nm_ref[0] = nm
                nv_ref[0] = nv

    def part_spec(k):
        return pl.BlockSpec((N_DEV, tr, C), lambda l, r: (0, jnp.where(l == k, r, 0), 0))

    blk = pl.BlockSpec((1, tr, C), lambda l, r: (l, r, 0))
    shp = jax.ShapeDtypeStruct((nl, R, C), F32)
    return pl.pallas_call(
        body, name=name, grid=(nl, R // tr),
        in_specs=[part_spec(k) for k in range(nl)] + [blk, blk, blk],
        out_specs=(blk, blk, blk, blk), out_shape=(shp, shp, shp, shp),
        compiler_params=_params(("arbitrary", "arbitrary")),
    )(*parts, w, m, v)


def _adamw_small(entries, name):
    n = len(entries)
    uniq = []
    for e in entries:
        if not any(e[0] is u for u in uniq):
            uniq.append(e[0])
    pidx = [next(k for k, u in enumerate(uniq) if u is e[0]) for e in entries]
    npart = len(uniq)

    def body(*refs):
        p_refs = refs[:npart]
        wmv = refs[npart:npart + 3 * n]
        outs = refs[npart + 3 * n:]
        for t, e in enumerate(entries):
            lo, w = e[1], e[2]
            rows = w.shape[0]
            pr = p_refs[pidx[t]]
            g = pr[0, lo:lo + rows].astype(F32)
            for q in range(1, N_DEV):
                g = g + pr[q, lo:lo + rows].astype(F32)
            d, nm, nv = _adamw_math(wmv[3 * t][...], g, wmv[3 * t + 1][...], wmv[3 * t + 2][...])
            outs[4 * t][...] = g
            outs[4 * t + 1][...] = d
            outs[4 * t + 2][...] = nm
            outs[4 * t + 3][...] = nv

    vm = pl.BlockSpec(memory_space=pltpu.VMEM)
    args = list(uniq)
    out_shape = []
    for e in entries:
        args += [e[2], e[3], e[4]]
        out_shape += [jax.ShapeDtypeStruct(e[2].shape, F32)] * 4
    res = pl.pallas_call(
        body, name=name, in_specs=[vm] * len(args), out_specs=tuple([vm] * len(out_shape)),
        out_shape=tuple(out_shape), compiler_params=_params(),
    )(*args)
    return [tuple(res[4 * t:4 * t + 4]) for t in range(n)]


def _head_matrix(cc):
    bw = min(256, cc)
    r = lax.broadcasted_iota(jnp.int32, (bw, bw), 0) // HEAD_DIM
    c = lax.broadcasted_iota(jnp.int32, (bw, bw), 1) // HEAD_DIM
    return jnp.where(r == c, 1.0 / HEAD_DIM, 0.0).astype(BF16)


def _cols_from_shards(g):
    nd = g.ndim
    perm = tuple(range(1, nd - 1)) + (0, nd - 1)
    t = jnp.transpose(g, perm)
    return t.reshape(t.shape[:-2] + (t.shape[-2] * t.shape[-1],))


def _cols_to_shards(a):
    nd = a.ndim
    t = a.reshape(a.shape[:-1] + (N_DEV, a.shape[-1] // N_DEV))
    perm = (nd - 1,) + tuple(range(nd - 1)) + (nd,)
    return jnp.transpose(t, perm)


def kernel(x, meta_tokens, norm1_g, w_in, conv_dw_k, conv_dw_b, conv_ln_g, conv_ln_b, pool_w, pool_scale, w_out, norm2_g, w_up, ffn_dw_k, w_down, final_g, loss_target, m_meta_tokens, m_norm1_g, m_w_in, m_conv_dw_k, m_conv_dw_b, m_conv_ln_g, m_conv_ln_b, m_pool_w, m_pool_scale, m_w_out, m_norm2_g, m_w_up, m_ffn_dw_k, m_w_down, m_final_g, v_meta_tokens, v_norm1_g, v_w_in, v_conv_dw_k, v_conv_dw_b, v_conv_ln_g, v_conv_ln_b, v_pool_w, v_pool_scale, v_w_out, v_norm2_g, v_w_up, v_ffn_dw_k, v_w_down, v_final_g):
    depth, D = norm1_g.shape
    n_meta = meta_tokens.shape[0]
    seq = x.shape[1]
    L = n_meta + seq
    cc = conv_dw_b.shape[1]
    ng, gd = pool_w.shape[1], pool_w.shape[2]
    f = w_down.shape[1] * N_DEV

    def rows(g):
        return g.reshape(-1, g.shape[-1])

    b16 = lambda a: a.astype(BF16)
    tr = lambda a: jnp.swapaxes(a, -1, -2)
    w_in_t, m_w_in_t, v_w_in_t = tr(w_in), tr(m_w_in), tr(v_w_in)
    w_up_t, m_w_up_t, v_w_up_t = tr(w_up), tr(m_w_up), tr(v_w_up)
    (g_in0, g_out0, g_ck, g_kf, g_meta) = _exchange([b16(w_in_t[0]), b16(w_out[0]), conv_dw_k, ffn_dw_k, meta_tokens],
                                                    ["gather"] * 5, "gather_first")
    ck_full = _cols_from_shards(g_ck)
    ck_rows = jnp.broadcast_to(ck_full[:, :, None, :], ck_full.shape[:2] + (SUBLANES, cc))
    kf_full = _cols_from_shards(g_kf)
    meta_full = _cols_from_shards(g_meta)
    am = _head_matrix(cc)
    win, wout, wup, wdown = [None] * depth, [None] * depth, [None] * depth, [None] * depth
    win[0] = rows(g_in0)
    wout[0] = rows(g_out0)

    h = jnp.concatenate([meta_full, x[0]], axis=0)
    saved = []
    for l in range(depth):
        more = l + 1 < depth
        if l == 0:
            (z, hn1), (g_down,) = _norm_proj(h, norm1_g[l:l + 1], win[l], f"in_proj_{l}", tn_cap=768,
                                             xchg=([b16(w_down[l])], ["gather"]))
            wdown[l] = rows(g_down)
            (ymix, u1), (g_up,) = _mixer_fwd(z, ck_rows[l], conv_dw_b[l:l + 1], conv_ln_g[l:l + 1], conv_ln_b[l:l + 1],
                                             pool_w[l], pool_scale[l:l + 1], am, f"mixer_fwd_{l}",
                                             xchg=([b16(w_up_t[l])], ["gather"]))
            wup[l] = rows(g_up)
        else:
            z, hn1 = _norm_proj(h, norm1_g[l:l + 1], win[l], f"in_proj_{l}", tn_cap=768)
            ymix, u1 = _mixer_fwd(z, ck_rows[l], conv_dw_b[l:l + 1], conv_ln_g[l:l + 1], conv_ln_b[l:l + 1], pool_w[l],
                                  pool_scale[l:l + 1], am, f"mixer_fwd_{l}")
        if more:
            h_mid, (g_in,) = _mm(ymix, wout[l], f"out_proj_{l}", res=h, tn_cap=512, xchg=([b16(w_in_t[l + 1])], ["gather"]))
            win[l + 1] = rows(g_in)
            nxt = [b16(w_out[l + 1]), b16(w_up_t[l + 1]), b16(w_down[l + 1])]
            (h_out, hn2, ug0, act), got = _ffn_block_fwd(h_mid, norm2_g[l:l + 1], wup[l], kf_full[l], wdown[l],
                                                         f"ffn_fwd_{l}", xchg=(nxt, ["gather"] * 3))
            wout[l + 1], wup[l + 1], wdown[l + 1] = rows(got[0]), rows(got[1]), rows(got[2])
        else:
            h_mid = _mm(ymix, wout[l], f"out_proj_{l}", res=h, tn_cap=512)
            h_out, hn2, ug0, act = _ffn_block_fwd(h_mid, norm2_g[l:l + 1], wup[l], kf_full[l], wdown[l], f"ffn_fwd_{l}")
        saved.append((h, hn1, z, u1, ymix, h_mid, hn2, ug0, act))
        h = h_out

    dh, d_final_g, loss_part = _loss_head(h, final_g.reshape(1, D), loss_target[0], n_meta, "loss_head")

    def row_shards(gm):
        return gm.reshape(N_DEV, -1, gm.shape[-1])

    zero_row = jnp.zeros((1, D), F32)
    gw = {k: [None] * depth for k in ("ck", "cb", "lg", "lb", "pw", "ps", "kf", "n1", "n2")}
    parts = {k: [None] * depth for k in ("in", "out", "up", "down")}
    for l in reversed(range(depth)):
        h_in, hn1, z, u1, ymix, h_mid, hn2, ug0, act = saved[l]
        g_down = _mm_tn(act, dh, f"down_proj_wgrad_{l}", tq_cap=512)
        (dh_mid, gw["n2"][l], dug0, dkf), (parts["down"][l],) = _ffn_block_bwd(
            dh, h_mid, norm2_g[l:l + 1], ug0, kf_full[l], wdown[l], wup[l], f"ffn_bwd_{l}",
            xchg=([row_shards(g_down)], ["a2a"]))
        gw["kf"][l] = jnp.transpose(dkf, (2, 1, 0, 3)).reshape(dkf.shape[2], -1)
        g_up_t = _mm_tn(dug0, hn2, f"up_proj_wgrad_{l}", halves=2, tq_cap=1024)
        dymix = _mm(dh_mid, wout[l], f"out_proj_bwd_{l}", b_t=True, tn_cap=512)
        g_out = _mm_tn(ymix, dh_mid, f"out_proj_wgrad_{l}", tq_cap=512)
        ((dz, gw["ck"][l], gw["cb"][l], gw["lg"][l], gw["lb"][l], gw["pw"][l], gw["ps"][l]),
         (parts["up"][l], parts["out"][l])) = _mixer_bwd(
            z, u1, dymix, ck_rows[l], conv_ln_g[l:l + 1], conv_ln_b[l:l + 1], pool_w[l], pool_scale[l:l + 1], am,
            f"mixer_bwd_{l}", xchg=([row_shards(g_up_t), row_shards(g_out)], ["a2a", "a2a"]))
        g_in_t = _mm_tn(dz, hn1, f"in_proj_wgrad_{l}", tq_cap=1024)
        if l > 0:
            (dh, gw["n1"][l]), (parts["in"][l],) = _proj_bwd_norm(dz, win[l], h_in, norm1_g[l:l + 1], dh_mid, zero_row,
                                                                  f"in_proj_bwd_{l}", xchg=([row_shards(g_in_t)], ["a2a"]))
        else:
            (grad_x, dg_x), (parts["in"][l],) = _proj_bwd_norm(dz, win[l], h_in, norm1_g[l:l + 1], dh_mid, zero_row,
                                                               f"in_proj_bwd_{l}", skip=n_meta,
                                                               xchg=([row_shards(g_in_t)], ["a2a"]))
            d_meta, gw["n1"][l] = _proj_bwd_norm(dz[:n_meta], win[l], h_in[:n_meta], norm1_g[l:l + 1], dh_mid[:n_meta],
                                                 dg_x, f"in_proj_bwd_meta_{l}")
    grad_x = grad_x[None]

    pack_d = jnp.concatenate(gw["n1"] + gw["n2"] + [d_final_g, jnp.broadcast_to(loss_part[:, :1], (1, D)), zero_row, zero_row], axis=0)
    pack_c = jnp.concatenate(gw["cb"] + gw["lg"] + gw["lb"] + gw["ps"], axis=0)
    pack_pw = jnp.stack(gw["pw"]).reshape(depth * ng * gd, gd)
    src = [_cols_to_shards(jnp.stack(gw["ck"])), _cols_to_shards(jnp.stack(gw["kf"])), _cols_to_shards(d_meta),
           pack_d, pack_c, pack_pw]
    r_ck, r_kf, r_meta, r_d, r_c, r_pw = _exchange(src, ["a2a"] * 3 + ["gather"] * 3, "exchange_small_grads")

    big = {
        "w_in": tuple(tr(a) for a in _adamw_big(parts["in"], w_in_t, m_w_in_t, v_w_in_t, "adamw_w_in")),
        "w_out": _adamw_big(parts["out"], w_out, m_w_out, v_w_out, "adamw_w_out"),
        "w_up": tuple(tr(a) for a in _adamw_big(parts["up"], w_up_t, m_w_up_t, v_w_up_t, "adamw_w_up")),
        "w_down": _adamw_big(parts["down"], w_down, m_w_down, v_w_down, "adamw_w_down"),
    }
    kwid = conv_dw_k.shape[1]
    fkw = ffn_dw_k.shape[1]
    row = lambda a: a.reshape(1, -1)
    entries = [
        (r_d, 0, norm1_g, m_norm1_g, v_norm1_g),
        (r_d, depth, norm2_g, m_norm2_g, v_norm2_g),
        (r_d, 2 * depth, row(final_g), row(m_final_g), row(v_final_g)),
        (r_c, 0, conv_dw_b, m_conv_dw_b, v_conv_dw_b),
        (r_c, depth, conv_ln_g, m_conv_ln_g, v_conv_ln_g),
        (r_c, 2 * depth, conv_ln_b, m_conv_ln_b, v_conv_ln_b),
        (r_c, 3 * depth, pool_scale, m_pool_scale, v_pool_scale),
        (r_pw, 0, pool_w.reshape(-1, gd), m_pool_w.reshape(-1, gd), v_pool_w.reshape(-1, gd)),
        (r_ck.reshape(N_DEV, depth * kwid, -1), 0, conv_dw_k.reshape(depth * kwid, -1),
         m_conv_dw_k.reshape(depth * kwid, -1), v_conv_dw_k.reshape(depth * kwid, -1)),
        (r_kf.reshape(N_DEV, depth * fkw, -1), 0, ffn_dw_k.reshape(depth * fkw, -1),
         m_ffn_dw_k.reshape(depth * fkw, -1), v_ffn_dw_k.reshape(depth * fkw, -1)),
        (r_meta, 0, meta_tokens, m_meta_tokens, v_meta_tokens),
        (r_d, 2 * depth + 1, zero_row, zero_row, zero_row),
    ]
    small = _adamw_small(entries, "adamw_small")
    names = ["norm1_g", "norm2_g", "final_g", "conv_dw_b", "conv_ln_g", "conv_ln_b", "pool_scale", "pool_w",
             "conv_dw_k", "ffn_dw_k", "meta_tokens"]
    shapes = {"final_g": final_g.shape, "pool_w": pool_w.shape, "conv_dw_k": conv_dw_k.shape, "ffn_dw_k": ffn_dw_k.shape}
    res = dict(big)
    for nme, quad in zip(names, small[:-1]):
        res[nme] = tuple(a.reshape(shapes[nme]) if nme in shapes else a for a in quad)
    loss = small[-1][0][0, 0]

    order = ["meta_tokens", "norm1_g", "w_in", "conv_dw_k", "conv_dw_b", "conv_ln_g", "conv_ln_b", "pool_w", "pool_scale",
             "w_out", "norm2_g", "w_up", "ffn_dw_k", "w_down", "final_g"]
    return (loss, grad_x, *[res[k][0] for k in order], *[res[k][1] for k in order], *[res[k][2] for k in order],
            *[res[k][3] for k in order])
```

```python
import functools

import jax
import jax.numpy as jnp
from jax import lax
from jax.experimental import pallas as pl
from jax.experimental.pallas import tpu as pltpu

F32 = jnp.float32
BF16 = jnp.bfloat16

EPS = 1e-6
HEAD_DIM = 64
POOL_WINDOWS = (2, 4, 8, 16)
ADAM_LR = 0.001
ADAM_B1 = 0.9
ADAM_B2 = 0.999
ADAM_EPS = 1e-08
ADAM_WD = 0.01
ADAM_STEP = 10

N_DEV = 8
OTHER_CHIPS = (2, 4, 6)
SUBLANES = 8
HALO = 48
CONV_PAD = 32
POOL_PAD = 16
FFN_PAD = 8
ROW_CHUNK = 24
CONV3_ROWS = 48
MAX_TILE_ROWS = 1024
WGRAD_TILE_ROWS = 2816
VMEM_LIMIT = 52 * 1024 * 1024


def _divisor(n, cap, mult):
    best = None
    for d in range(mult, min(n, cap) + 1, mult):
        if n % d == 0:
            best = d
    return n if best is None else best


def _token_tile(L):
    return _divisor(L, MAX_TILE_ROWS, HALO)


def _row_tile(L):
    return _divisor(L, 320, 2 * SUBLANES)


def _stat_rows(tl):
    return _divisor(tl, 256, SUBLANES)


def _params(sem=None):
    return pltpu.CompilerParams(dimension_semantics=sem, vmem_limit_bytes=VMEM_LIMIT)


def _rowsum8(x):
    acc = x[0:SUBLANES]
    for k in range(1, x.shape[0] // SUBLANES):
        acc = acc + x[k * SUBLANES:(k + 1) * SUBLANES]
    return acc


def _sigmoid(x):
    return jax.nn.sigmoid(x)


def _dot_nt(a, b):
    return lax.dot_general(a, b, (((1,), (1,)), ((), ())), preferred_element_type=F32)


def _head_mean(x, am_ref):
    bw = am_ref.shape[0]
    am = am_ref[...]
    outs = []
    for blk in range(x.shape[1] // bw):
        xb = x[:, blk * bw:(blk + 1) * bw]
        hi = xb.astype(BF16)
        lo = (xb - hi.astype(F32)).astype(BF16)
        outs.append(jnp.dot(hi, am, preferred_element_type=F32) + jnp.dot(lo, am, preferred_element_type=F32))
    return outs[0] if len(outs) == 1 else jnp.concatenate(outs, axis=-1)


def _xchg_out_shapes(srcs, modes):
    out = []
    for s, m in zip(srcs, modes):
        shp = ((N_DEV,) + tuple(s.shape)) if m == "gather" else tuple(s.shape)
        out.append(jax.ShapeDtypeStruct(shp, s.dtype))
    return out


def _xchg_sems(n):
    return [pltpu.SemaphoreType.DMA((n, N_DEV - 1)), pltpu.SemaphoreType.DMA((n, N_DEV - 1)), pltpu.SemaphoreType.DMA((n,))]


def _xchg_ops(src_refs, out_refs, sems, modes):
    n = len(src_refs)
    send_sems, recv_sems, local_sems = sems
    x, y, c = lax.axis_index("x"), lax.axis_index("y"), lax.axis_index("c")
    me = 4 * x + 2 * y + c

    def peer(d):
        return (x ^ ((d >> 2) & 1), y ^ ((d >> 1) & 1), c ^ (d & 1))

    def peer_id(d):
        px, py, pc = peer(d)
        return 4 * px + 2 * py + pc

    def remote(t, d):
        src = src_refs[t] if modes[t] == "gather" else src_refs[t].at[peer_id(d)]
        return pltpu.make_async_remote_copy(
            src_ref=src, dst_ref=out_refs[t].at[me], send_sem=send_sems.at[t, d - 1], recv_sem=recv_sems.at[t, d - 1],
            device_id=peer(d), device_id_type=pl.DeviceIdType.MESH)

    def arrival(t, d):
        src = src_refs[t] if modes[t] == "gather" else src_refs[t].at[me]
        return pltpu.make_async_remote_copy(
            src_ref=src, dst_ref=out_refs[t].at[peer_id(d)], send_sem=send_sems.at[t, d - 1],
            recv_sem=recv_sems.at[t, d - 1], device_id=peer(d), device_id_type=pl.DeviceIdType.MESH)

    def passed_on(t, d):
        blk = out_refs[t].at[peer_id(d)]
        return pltpu.make_async_remote_copy(
            src_ref=blk, dst_ref=blk, send_sem=send_sems.at[t, d], recv_sem=recv_sems.at[t, d],
            device_id=peer(1), device_id_type=pl.DeviceIdType.MESH)

    def local(t):
        src = src_refs[t] if modes[t] == "gather" else src_refs[t].at[me]
        return pltpu.make_async_copy(src, out_refs[t].at[me], local_sems.at[t])

    def sent_first(t):
        return OTHER_CHIPS + (1,) if modes[t] == "gather" else tuple(range(1, N_DEV))

    def start():
        for t in range(n):
            local(t).start()
        for t in range(n):
            for d in sent_first(t):
                remote(t, d).start()

    def wait():
        gathered = [t for t in range(n) if modes[t] == "gather"]
        for t in gathered:
            for d in OTHER_CHIPS:
                arrival(t, d).wait_recv()
                passed_on(t, d).start()
        for t in range(n):
            for d in range(1, N_DEV):
                if not (modes[t] == "gather" and d in OTHER_CHIPS):
                    arrival(t, d).wait_recv()
        for t in range(n):
            for d in sent_first(t):
                remote(t, d).wait_send()
        for t in gathered:
            for d in OTHER_CHIPS:
                passed_on(t, d).wait_send()
        for t in range(n):
            local(t).wait()

    return start, wait


def _exchange(srcs, modes, name):
    n = len(srcs)

    def body(*refs):
        start, wait = _xchg_ops(refs[:n], refs[n:2 * n], refs[2 * n:], modes)
        start()
        wait()

    any_spec = pl.BlockSpec(memory_space=pl.ANY)
    return pl.pallas_call(
        body, name=name, out_shape=tuple(_xchg_out_shapes(srcs, modes)),
        in_specs=[any_spec] * n, out_specs=tuple([any_spec] * n),
        scratch_shapes=_xchg_sems(n),
        compiler_params=pltpu.CompilerParams(has_side_effects=True),
    )(*srcs)


def _call(body, *, name, grid, in_specs, out_specs, out_shape, args, scratch_shapes=(), sem=None, xchg=None):
    single = not isinstance(out_shape, (tuple, list))
    outs_shape = [out_shape] if single else list(out_shape)
    outs_spec = [out_specs] if single else list(out_specs)
    if xchg is None:
        res = pl.pallas_call(
            body, name=name, grid=grid, in_specs=list(in_specs), out_specs=out_specs, out_shape=out_shape,
            scratch_shapes=list(scratch_shapes), compiler_params=_params(sem))(*args)
        return res, ()
    srcs, modes = xchg
    n_in, n_out, n_scr, nx = len(in_specs), len(outs_shape), len(scratch_shapes), len(srcs)

    def wrapped(*refs):
        ins = refs[:n_in]
        xs = refs[n_in:n_in + nx]
        o0 = n_in + nx
        outs = refs[o0:o0 + n_out]
        xo = refs[o0 + n_out:o0 + n_out + nx]
        s0 = o0 + n_out + nx
        scr = refs[s0:s0 + n_scr]
        start, wait = _xchg_ops(xs, xo, refs[s0 + n_scr:], modes)
        first = functools.reduce(jnp.logical_and, [pl.program_id(a) == 0 for a in range(len(grid))])
        last = functools.reduce(jnp.logical_and, [pl.program_id(a) == grid[a] - 1 for a in range(len(grid))])

        @pl.when(first)
        def _():
            start()

        body(*ins, *outs, *scr)

        @pl.when(last)
        def _():
            wait()

    any_spec = pl.BlockSpec(memory_space=pl.ANY)
    res = pl.pallas_call(
        wrapped, name=name, grid=grid, in_specs=list(in_specs) + [any_spec] * nx,
        out_specs=tuple(outs_spec + [any_spec] * nx), out_shape=tuple(outs_shape + _xchg_out_shapes(srcs, modes)),
        scratch_shapes=list(scratch_shapes) + _xchg_sems(nx),
        compiler_params=_params(("arbitrary",) * len(grid)))(*args, *srcs)
    comp = res[:n_out]
    return (comp[0] if single else tuple(comp)), tuple(res[n_out:])


def _norm_proj(h, g, w, name, *, tn_cap, xchg=None):
    L, D = h.shape
    N = w.shape[0]
    tm = _token_tile(L)
    tn = _divisor(N, tn_cap, 128)

    def body(h_ref, g_ref, w_ref, z_ref, hn_ref):
        @pl.when(pl.program_id(1) == 0)
        def _():
            x = h_ref[...]
            r = lax.rsqrt(jnp.mean(x * x, axis=-1, keepdims=True) + EPS)
            hn_ref[...] = ((x * r) * g_ref[...]).astype(BF16)

        z_ref[...] = _dot_nt(hn_ref[...], w_ref[...])

    out, xo = _call(
        body, name=name, grid=(L // tm, N // tn),
        in_specs=[pl.BlockSpec((tm, D), lambda i, j: (i, 0)), pl.BlockSpec((1, D), lambda i, j: (0, 0)),
                  pl.BlockSpec((tn, D), lambda i, j: (j, 0))],
        out_specs=(pl.BlockSpec((tm, tn), lambda i, j: (i, j)), pl.BlockSpec((tm, D), lambda i, j: (i, 0))),
        out_shape=(jax.ShapeDtypeStruct((L, N), F32), jax.ShapeDtypeStruct((L, D), BF16)),
        sem=("parallel", "arbitrary"), args=(h, g, w), xchg=xchg)
    return out if xchg is None else (out, xo)


def _proj_bwd_norm(a, b, h, g, dres, dg0, name, skip=0, xchg=None):
    L, K = a.shape
    D = b.shape[1]
    rows = L - skip
    tm = _divisor(rows, MAX_TILE_ROWS, 2 * SUBLANES) if skip else _token_tile(L)

    def body(a_ref, b_ref, h_ref, g_ref, dres_ref, dg0_ref, dh_ref, dg_ref):
        i = pl.program_id(0)
        dhn = jnp.dot(a_ref[...], b_ref[...], preferred_element_type=F32)
        x = h_ref[...]
        r = lax.rsqrt(jnp.mean(x * x, axis=-1, keepdims=True) + EPS)
        xhat = x * r
        dxhat = dhn * g_ref[...]
        dh_ref[...] = dres_ref[...] + r * (dxhat - xhat * jnp.mean(dxhat * xhat, axis=-1, keepdims=True))
        part = jnp.sum(_rowsum8(dhn * xhat), axis=0, keepdims=True)

        @pl.when(i == 0)
        def _():
            dg_ref[...] = dg0_ref[...] + part

        @pl.when(i > 0)
        def _():
            dg_ref[...] += part

    def rows_of(cols):
        if not skip:
            return pl.BlockSpec((tm, cols), lambda i: (i, 0))
        return pl.BlockSpec((pl.Element(tm), pl.Element(cols)), lambda i: (pl.multiple_of(skip + i * tm, SUBLANES), 0))

    row = pl.BlockSpec((1, D), lambda i: (0, 0))
    out, xo = _call(
        body, name=name, grid=(rows // tm,),
        in_specs=[rows_of(K), pl.BlockSpec((K, D), lambda i: (0, 0)), rows_of(D), row, rows_of(D), row],
        out_specs=(pl.BlockSpec((tm, D), lambda i: (i, 0)), row),
        out_shape=(jax.ShapeDtypeStruct((rows, D), F32), jax.ShapeDtypeStruct((1, D), F32)),
        sem=("arbitrary",), args=(a, b, h, g, dres, dg0), xchg=xchg)
    return out if xchg is None else (out, xo)


def _mm(a, b, name, *, res=None, b_t=False, tn_cap=1408, xchg=None):
    M, K = a.shape
    N = b.shape[0] if b_t else b.shape[1]
    tm = _token_tile(M)
    tn = _divisor(N, tn_cap, 128)

    def body(*refs):
        a_ref, b_ref = refs[:2]
        r_ref, o_ref = (None, refs[2]) if res is None else (refs[2], refs[3])
        av = a_ref[...].astype(BF16)
        prod = _dot_nt(av, b_ref[...]) if b_t else jnp.dot(av, b_ref[...], preferred_element_type=F32)
        o_ref[...] = prod if r_ref is None else prod + r_ref[...]

    b_spec = pl.BlockSpec((tn, K), lambda i, j: (j, 0)) if b_t else pl.BlockSpec((K, tn), lambda i, j: (0, j))
    in_specs = [pl.BlockSpec((tm, K), lambda i, j: (i, 0)), b_spec]
    args = [a, b]
    if res is not None:
        in_specs.append(pl.BlockSpec((tm, tn), lambda i, j: (i, j)))
        args.append(res)
    out, xo = _call(
        body, name=name, grid=(M // tm, N // tn), in_specs=in_specs,
        out_specs=pl.BlockSpec((tm, tn), lambda i, j: (i, j)), out_shape=jax.ShapeDtypeStruct((M, N), F32),
        sem=("parallel", "parallel"), args=args, xchg=xchg)
    return out if xchg is None else (out, xo)


def _mm_tn(a, b, name, *, halves=1, tq_cap=1408):
    L, Q = b.shape
    ph = a.shape[-1]
    P = ph * halves
    tl = _divisor(L, WGRAD_TILE_ROWS, HALO)
    tp = _divisor(ph, 1408, 128)
    tq = _divisor(Q, tq_cap, 128)
    pper = ph // tp
    nl = L // tl
    grid = (P // tp, Q // tq, nl)

    def body(a_ref, b_ref, o_ref, acc):
        prod = lax.dot_general(a_ref[...].astype(BF16), b_ref[...].astype(BF16), (((0,), (0,)), ((), ())),
                               preferred_element_type=F32)
        l = pl.program_id(2)
        if nl == 1:
            o_ref[...] = prod.astype(BF16)
            return

        @pl.when(l == 0)
        def _():
            acc[...] = prod

        @pl.when(jnp.logical_and(l > 0, l < nl - 1))
        def _():
            acc[...] += prod

        @pl.when(l == nl - 1)
        def _():
            o_ref[...] = (acc[...] + prod).astype(BF16)

    if halves > 1:
        a_spec = pl.BlockSpec((None, tl, tp), lambda p, q, l: (p // pper, l, p % pper))
    else:
        a_spec = pl.BlockSpec((tl, tp), lambda p, q, l: (l, p))
    return pl.pallas_call(
        body, name=name, grid=grid,
        in_specs=[a_spec, pl.BlockSpec((tl, tq), lambda p, q, l: (l, q))],
        out_specs=pl.BlockSpec((tp, tq), lambda p, q, l: (p, q)),
        out_shape=jax.ShapeDtypeStruct((P, Q), BF16),
        scratch_shapes=[pltpu.VMEM((tp, tq), F32)],
        compiler_params=_params(("parallel", "parallel", "arbitrary")),
    )(a, b)


def _loss_head(h, g, tgt, n_meta, name):
    L, D = h.shape
    tl = _row_tile(L)
    nt = L // tl

    def body(h_ref, g_ref, t_ref, dh_ref, dg_ref, loss_ref):
        i = pl.program_id(0)
        x = h_ref[...]
        r = lax.rsqrt(jnp.mean(x * x, axis=-1, keepdims=True) + EPS)
        xhat = x * r
        gg = g_ref[...]
        y = xhat * gg
        rows = i * tl + lax.broadcasted_iota(jnp.int32, (tl, 1), 0)
        t = t_ref[...]
        t = jnp.where(i == 0, pltpu.roll(t, n_meta, axis=0), t)
        err = jnp.where(rows >= n_meta, y - t, 0.0)
        dy = err * (1.0 / D)
        dxhat = dy * gg
        dh_ref[...] = r * (dxhat - xhat * jnp.mean(dxhat * xhat, axis=-1, keepdims=True))
        dg_part = jnp.sum(_rowsum8(dy * xhat), axis=0, keepdims=True)
        per_row = jnp.mean(err * err, axis=-1, keepdims=True)
        loss_part = jnp.broadcast_to(0.5 * jnp.sum(per_row, axis=0, keepdims=True), (1, 128))

        @pl.when(i == 0)
        def _():
            dg_ref[...] = dg_part
            loss_ref[...] = loss_part

        @pl.when(i > 0)
        def _():
            dg_ref[...] += dg_part
            loss_ref[...] += loss_part

    tile = pl.BlockSpec((tl, D), lambda i: (i, 0))
    row = pl.BlockSpec((1, D), lambda i: (0, 0))
    window = pl.BlockSpec((pl.Element(tl), pl.Element(D)),
                          lambda i: (pl.multiple_of(jnp.maximum(i * tl - n_meta, 0), SUBLANES), 0))
    return pl.pallas_call(
        body, name=name, grid=(nt,), in_specs=[tile, row, window],
        out_specs=(tile, row, pl.BlockSpec((1, 128), lambda i: (0, 0))),
        out_shape=(jax.ShapeDtypeStruct((L, D), F32), jax.ShapeDtypeStruct((1, D), F32),
                   jax.ShapeDtypeStruct((1, 128), F32)),
        compiler_params=_params(("arbitrary",)),
    )(h, g, tgt)


def _pool_fwd_block(pwin, pw_ref, row0, rb, g, gd, w, t0):
    wv = pwin[pl.ds(row0 + HALO - POOL_PAD, rb + POOL_PAD), g * gd:(g + 1) * gd]
    s = wv
    sh = 1
    while sh < w:
        s = s + pltpu.roll(s, sh, axis=0)
        sh *= 2
    win = s[POOL_PAD:POOL_PAD + rb]
    pt = wv[POOL_PAD:POOL_PAD + rb]
    tg = t0 + lax.broadcasted_iota(jnp.int32, (rb, 1), 0)
    cnt = jnp.minimum(tg + 1, w).astype(F32)
    return win / cnt - pt


def _fill_windows(i, zp_ref, zc_ref, u0w, pwin, tl, cc):
    keep = i > 0
    zp = zp_ref[...]
    u0w[0:HALO, :] = jnp.where(keep, zp[:, :cc] * _sigmoid(zp[:, cc:2 * cc]), 0.0)
    pwin[0:HALO, :] = jnp.where(keep, zp[:, 2 * cc:], 0.0)

    def fill(c, carry):
        b = pl.multiple_of(c * ROW_CHUNK, SUBLANES)
        zc = zc_ref[pl.ds(b, ROW_CHUNK), :]
        u0w[pl.ds(HALO + b, ROW_CHUNK), :] = zc[:, :cc] * _sigmoid(zc[:, cc:2 * cc])
        pwin[pl.ds(HALO + b, ROW_CHUNK), :] = zc[:, 2 * cc:]
        return carry

    lax.fori_loop(0, tl // ROW_CHUNK, fill, 0)


def _mixer_fwd(z, ck, cb, lg, lb, pw, ps, am, name, xchg=None):
    L, ci = z.shape
    kw, _, cc = ck.shape
    cp = ci - 2 * cc
    ng, gd = pw.shape[0], pw.shape[1]
    tl = _token_tile(L)
    nt = L // tl
    hb = tl // HALO
    rb = _stat_rows(tl)
    tap0 = CONV_PAD - (kw - 1)

    def body(zp_ref, zc_ref, ck_ref, cb_ref, lg_ref, lb_ref, pw_ref, ps_ref, am_ref, y_ref, u1_ref, u0w, pwin):
        i = pl.program_id(0)
        _fill_windows(i, zp_ref, zc_ref, u0w, pwin, tl, cc)

        def conv(c, carry):
            b = pl.multiple_of(c * ROW_CHUNK, SUBLANES)
            w = u0w[pl.ds(b + HALO - CONV_PAD, ROW_CHUNK + CONV_PAD), :]
            acc = jnp.broadcast_to(cb_ref[...], (ROW_CHUNK, cc))
            for j in range(kw):
                acc = acc + _rows_of(ck_ref[j], ROW_CHUNK) * w[tap0 + j:tap0 + j + ROW_CHUNK]
            u1_ref[pl.ds(b, ROW_CHUNK), :] = acc
            return carry

        lax.fori_loop(0, tl // ROW_CHUNK, conv, 0)

        def blocks(k, carry):
            b = pl.multiple_of(k * rb, SUBLANES)
            u1 = u1_ref[pl.ds(b, rb), :]
            xc = u1 - _head_mean(u1, am_ref)
            var = _head_mean(xc * xc, am_ref)
            u2 = (xc * lax.rsqrt(var + EPS)) * lg_ref[...] + lb_ref[...]
            y_ref[pl.ds(b, rb), 0:cc] = (u2 * _sigmoid(u2)).astype(y_ref.dtype)
            for g in range(ng):
                d = _pool_fwd_block(pwin, pw_ref, b, rb, g, gd, POOL_WINDOWS[g], i * tl + b)
                yp = jnp.dot(d.astype(BF16), pw_ref[g].astype(BF16), preferred_element_type=F32)
                yp = yp * ps_ref[:, g * gd:(g + 1) * gd]
                y_ref[pl.ds(b, rb), cc + g * gd:cc + (g + 1) * gd] = yp.astype(y_ref.dtype)
            return carry

        lax.fori_loop(0, tl // rb, blocks, 0)

    def full(a):
        nd = a.ndim
        return pl.BlockSpec(a.shape, lambda i: (0,) * nd)

    out, xo = _call(
        body, name=name, grid=(nt,),
        in_specs=[pl.BlockSpec((HALO, ci), lambda i: (jnp.maximum(i * hb - 1, 0), 0)),
                  pl.BlockSpec((tl, ci), lambda i: (i, 0)),
                  full(ck), full(cb), full(lg), full(lb), full(pw), full(ps), full(am)],
        out_specs=(pl.BlockSpec((tl, cc + cp), lambda i: (i, 0)), pl.BlockSpec((tl, cc), lambda i: (i, 0))),
        out_shape=(jax.ShapeDtypeStruct((L, cc + cp), BF16), jax.ShapeDtypeStruct((L, cc), F32)),
        scratch_shapes=[pltpu.VMEM((HALO + tl, cc), F32), pltpu.VMEM((HALO + tl, cp), F32)],
        sem=("parallel",), args=(z, z, ck, cb, lg, lb, pw, ps, am), xchg=xchg)
    return out if xchg is None else (out, xo)


def _mixer_bwd(z, u1, dy, ck, lg, lb, pw, ps, am, name, xchg=None):
    L, ci = z.shape
    kw, _, cc = ck.shape
    cp = ci - 2 * cc
    ng, gd = pw.shape[0], pw.shape[1]
    tl = _token_tile(L)
    nt = L // tl
    hb = tl // HALO
    rb = _stat_rows(tl)

    def body(zp_ref, zc_ref, u1c_ref, u1n_ref, dyc_ref, dyn_ref, ck_ref, lg_ref, lb_ref, pw_ref, ps_ref, am_ref,
             dz_ref, dck_ref, dcb_ref, dlg_ref, dlb_ref, dpw_ref, dps_ref,
             u0w, pwin, du1w, ddw, ew, dkacc, dcb8, dlg8, dlb8, dps8):
        i = pl.program_id(0)
        has_next = i < nt - 1

        @pl.when(i == 0)
        def _():
            dck_ref[...] = jnp.zeros_like(dck_ref)
            dcb_ref[...] = jnp.zeros_like(dcb_ref)
            dlg_ref[...] = jnp.zeros_like(dlg_ref)
            dlb_ref[...] = jnp.zeros_like(dlb_ref)
            dpw_ref[...] = jnp.zeros_like(dpw_ref)
            dps_ref[...] = jnp.zeros_like(dps_ref)

        dkacc[...] = jnp.zeros_like(dkacc)
        dcb8[...] = jnp.zeros_like(dcb8)
        dlg8[...] = jnp.zeros_like(dlg8)
        dlb8[...] = jnp.zeros_like(dlb8)
        dps8[...] = jnp.zeros_like(dps8)

        _fill_windows(i, zp_ref, zc_ref, u0w, pwin, tl, cc)

        def conv_side(u1, dyc, own):
            xc = u1 - _head_mean(u1, am_ref)
            rstd = lax.rsqrt(_head_mean(xc * xc, am_ref) + EPS)
            uh = xc * rstd
            lgv = lg_ref[...]
            u2 = uh * lgv + lb_ref[...]
            sg = _sigmoid(u2)
            du2 = dyc * (sg * (1.0 + u2 * (1.0 - sg)))
            if own:
                dlg8[...] += _rowsum8(du2 * uh)
                dlb8[...] += _rowsum8(du2)
            duh = du2 * lgv
            return rstd * (duh - _head_mean(duh, am_ref) - uh * _head_mean(duh * uh, am_ref))

        def pool_side(dyp, t0, rows):
            dds, es = [], []
            tg = t0 + lax.broadcasted_iota(jnp.int32, (rows, 1), 0)
            for g in range(ng):
                dypre = dyp[:, g * gd:(g + 1) * gd] * ps_ref[:, g * gd:(g + 1) * gd]
                dd = lax.dot_general(dypre.astype(BF16), pw_ref[g].astype(BF16), (((1,), (1,)), ((), ())),
                                     preferred_element_type=F32)
                cnt = jnp.minimum(tg + 1, POOL_WINDOWS[g]).astype(F32)
                dds.append(dd)
                es.append(dd / cnt)
            return jnp.concatenate(dds, axis=-1), jnp.concatenate(es, axis=-1)

        def blocks(k, carry):
            b = pl.multiple_of(k * rb, SUBLANES)
            dyb = dyc_ref[pl.ds(b, rb), :]
            du1 = conv_side(u1c_ref[pl.ds(b, rb), :], dyb[:, :cc], True)
            du1w[pl.ds(b, rb), :] = du1
            dcb8[...] += _rowsum8(du1)
            dyp = dyb[:, cc:]
            dd, e = pool_side(dyp, i * tl + b, rb)
            ddw[pl.ds(b, rb), :] = dd
            ew[pl.ds(b, rb), :] = e
            for g in range(ng):
                d = _pool_fwd_block(pwin, pw_ref, b, rb, g, gd, POOL_WINDOWS[g], i * tl + b)
                db16 = d.astype(BF16)
                dypg = dyp[:, g * gd:(g + 1) * gd]
                ypre = jnp.dot(db16, pw_ref[g].astype(BF16), preferred_element_type=F32)
                dps8[:, g * gd:(g + 1) * gd] += _rowsum8(dypg * ypre)
                dypre = (dypg * ps_ref[:, g * gd:(g + 1) * gd]).astype(BF16)
                dpw_ref[g] += lax.dot_general(db16, dypre, (((0,), (0,)), ((), ())), preferred_element_type=F32)
            return carry

        lax.fori_loop(0, tl // rb, blocks, 0)

        dyn = dyn_ref[...]
        du1n = conv_side(u1n_ref[...], dyn[:, :cc], False)
        du1w[tl:tl + HALO, :] = jnp.where(has_next, du1n, 0.0)
        ddn, en = pool_side(dyn[:, cc:], (i + 1) * tl, HALO)
        ew[tl:tl + HALO, :] = jnp.where(has_next, en, 0.0)

        def taps(c, carry):
            b = pl.multiple_of(c * ROW_CHUNK, SUBLANES)
            w = du1w[pl.ds(b, ROW_CHUNK + CONV_PAD), :]
            u0c = u0w[pl.ds(HALO + b, ROW_CHUNK), :]
            acc = jnp.zeros((ROW_CHUNK, cc), F32)
            for j in range(kw):
                o = kw - 1 - j
                sh = w[o:o + ROW_CHUNK]
                acc = acc + _rows_of(ck_ref[j], ROW_CHUNK) * sh
                dkacc[j] += _rowsum8(u0c * sh)
            zc = zc_ref[pl.ds(b, ROW_CHUNK), :]
            a = zc[:, :cc]
            sg = _sigmoid(zc[:, cc:2 * cc])
            dz_ref[pl.ds(b, ROW_CHUNK), 0:cc] = (acc * sg).astype(dz_ref.dtype)
            dz_ref[pl.ds(b, ROW_CHUNK), cc:2 * cc] = (acc * a * sg * (1.0 - sg)).astype(dz_ref.dtype)
            return carry

        lax.fori_loop(0, tl // ROW_CHUNK, taps, 0)

        def pool_back(k, carry):
            b = pl.multiple_of(k * rb, SUBLANES)
            n = rb + POOL_PAD
            for g in range(ng):
                s = ew[pl.ds(b, n), g * gd:(g + 1) * gd]
                sh = 1
                while sh < POOL_WINDOWS[g]:
                    s = s + pltpu.roll(s, n - sh, axis=0)
                    sh *= 2
                dp = s[0:rb] - ddw[pl.ds(b, rb), g * gd:(g + 1) * gd]
                dz_ref[pl.ds(b, rb), 2 * cc + g * gd:2 * cc + (g + 1) * gd] = dp.astype(dz_ref.dtype)
            return carry

        lax.fori_loop(0, tl // rb, pool_back, 0)

        dck_ref[...] += jnp.sum(dkacc[...], axis=1)
        dcb_ref[...] += jnp.sum(dcb8[...], axis=0, keepdims=True)
        dlg_ref[...] += jnp.sum(dlg8[...], axis=0, keepdims=True)
        dlb_ref[...] += jnp.sum(dlb8[...], axis=0, keepdims=True)
        dps_ref[...] += jnp.sum(dps8[...], axis=0, keepdims=True)

    def full(a):
        nd = a.ndim
        return pl.BlockSpec(a.shape, lambda i: (0,) * nd)

    nhb = L // HALO

    def prev_map(i):
        return (jnp.maximum(i * hb - 1, 0), 0)

    def next_map(i):
        return (jnp.minimum((i + 1) * hb, nhb - 1), 0)

    dcc = cc + cp
    row_cc = jax.ShapeDtypeStruct((1, cc), F32)
    out_shape = (jax.ShapeDtypeStruct((L, ci), BF16), jax.ShapeDtypeStruct((kw, cc), F32), row_cc, row_cc, row_cc,
                 jax.ShapeDtypeStruct((ng, gd, gd), F32), jax.ShapeDtypeStruct((1, cp), F32))
    acc_spec = [pl.BlockSpec((kw, cc), lambda i: (0, 0))] + [pl.BlockSpec((1, cc), lambda i: (0, 0))] * 3 + [
        pl.BlockSpec((ng, gd, gd), lambda i: (0, 0, 0)), pl.BlockSpec((1, cp), lambda i: (0, 0))]
    out, xo = _call(
        body, name=name, grid=(nt,),
        in_specs=[pl.BlockSpec((HALO, ci), prev_map), pl.BlockSpec((tl, ci), lambda i: (i, 0)),
                  pl.BlockSpec((tl, cc), lambda i: (i, 0)), pl.BlockSpec((HALO, cc), next_map),
                  pl.BlockSpec((tl, dcc), lambda i: (i, 0)), pl.BlockSpec((HALO, dcc), next_map),
                  full(ck), full(lg), full(lb), full(pw), full(ps), full(am)],
        out_specs=tuple([pl.BlockSpec((tl, ci), lambda i: (i, 0))] + acc_spec),
        out_shape=out_shape,
        scratch_shapes=[pltpu.VMEM((HALO + tl, cc), F32), pltpu.VMEM((HALO + tl, cp), F32),
                        pltpu.VMEM((tl + HALO, cc), F32), pltpu.VMEM((tl, cp), F32), pltpu.VMEM((tl + HALO, cp), F32),
                        pltpu.VMEM((kw, SUBLANES, cc), F32), pltpu.VMEM((SUBLANES, cc), F32),
                        pltpu.VMEM((SUBLANES, cc), F32), pltpu.VMEM((SUBLANES, cc), F32), pltpu.VMEM((SUBLANES, cp), F32)],
        sem=("arbitrary",), args=(z, z, u1, u1, dy, dy, ck, lg, lb, pw, ps, am), xchg=xchg)
    return out if xchg is None else (out, xo)


def _row_parts(nc, n=3):
    n = min(n, nc)
    cuts = [round(k * nc / n) for k in range(n + 1)]
    return [(cuts[k], cuts[k + 1]) for k in range(n)]


def _tap_rows(k_ref):
    return [jnp.broadcast_to(k_ref[j:j + 1, :], (SUBLANES, k_ref.shape[1])) for j in range(k_ref.shape[0])]


def _rows_of(tap, n):
    return tap if n == SUBLANES else jnp.concatenate([tap] * (n // SUBLANES), axis=0)


def _ffn_conv(win, taps, rows):
    kw = len(taps)
    o = FFN_PAD - (kw - 1)
    acc = _rows_of(taps[0], rows) * win[o:o + rows]
    for j in range(1, kw):
        acc = acc + _rows_of(taps[j], rows) * win[o + j:o + j + rows]
    return acc


def _ffn_block_fwd(h_mid, g, wup_t, kf, wdown, name, xchg=None):
    L, D = h_mid.shape
    f = wdown.shape[0]
    kw = kf.shape[0]
    tl = _token_tile(L)
    tc = _divisor(f, 256, 128)
    nj = f // tc
    nt = L // tl
    pad = 2 * SUBLANES
    hb = tl // pad
    rc = CONV3_ROWS
    parts = _row_parts(tl // rc)

    def body(hp_ref, hc_ref, g_ref, wg_ref, wv_ref, kg_ref, kv_ref, wd_ref, out_ref, hn_ref, act_ref, ux_ref, uc_ref,
             hn_halo, halo, ug_ref, acc):
        i = pl.program_id(0)
        kb = pl.program_id(1)

        @pl.when(kb == 0)
        def _():
            gg = g_ref[...]

            def norm(x):
                r = lax.rsqrt(jnp.mean(x * x, axis=-1, keepdims=True) + EPS)
                return ((x * r) * gg).astype(BF16)

            hn_halo[...] = jnp.where(i > 0, norm(hp_ref[...]), jnp.zeros((pad, D), BF16))
            hn_ref[...] = norm(hc_ref[...])
            acc[...] = jnp.zeros_like(acc)

        w_refs = (wg_ref, wv_ref)
        taps = (_tap_rows(kg_ref), _tap_rows(kv_ref))
        hh = hn_halo[...]
        for h in range(2):
            halo[h] = _dot_nt(hh, w_refs[h][...])[pad - FFN_PAD:]

        def up_part(lo, hi):
            a, b = lo * rc, hi * rc
            for h in range(2):
                ug_ref[h, a:b, :] = _dot_nt(hn_ref[a:b, :], w_refs[h][...])

        def down_part(lo, hi):
            a, b = lo * rc, hi * rc
            acc[a:b, :] += jnp.dot(act_ref[a:b, :], wd_ref[...], preferred_element_type=F32)

        def chunk_rows(lo, hi):
            for c in range(lo, hi):
                r0 = c * rc
                convd = []
                for h in range(2):
                    if c == 0:
                        win = jnp.concatenate([halo[h], ug_ref[h, 0:rc]], axis=0)
                    else:
                        win = ug_ref[h, r0 - FFN_PAD:r0 + rc]
                    convd.append(_ffn_conv(win, taps[h], rc))
                    ux_ref[h, r0:r0 + rc, :] = win[FFN_PAD:].astype(BF16)
                    uc_ref[h, r0:r0 + rc, :] = convd[h].astype(BF16)
                gate, val = convd
                act_ref[r0:r0 + rc, :] = ((gate * _sigmoid(gate)) * val).astype(BF16)

        for p, (lo, hi) in enumerate(parts):
            if p == 0:
                up_part(lo, hi)
            if p + 1 < len(parts):
                up_part(*parts[p + 1])
            if p > 0:
                down_part(*parts[p - 1])
            chunk_rows(lo, hi)
        down_part(*parts[-1])

        @pl.when(kb == nj - 1)
        def _():
            out_ref[...] = acc[...] + hc_ref[...]

    out, xo = _call(
        body, name=name, grid=(nt, nj),
        in_specs=[pl.BlockSpec((pad, D), lambda i, k: (jnp.maximum(i * hb - 1, 0), 0)),
                  pl.BlockSpec((tl, D), lambda i, k: (i, 0)),
                  pl.BlockSpec((1, D), lambda i, k: (0, 0)),
                  pl.BlockSpec((tc, D), lambda i, k: (k, 0)), pl.BlockSpec((tc, D), lambda i, k: (k + nj, 0)),
                  pl.BlockSpec((kw, tc), lambda i, k: (0, k)), pl.BlockSpec((kw, tc), lambda i, k: (0, k + nj)),
                  pl.BlockSpec((tc, D), lambda i, k: (k, 0))],
        out_specs=(pl.BlockSpec((tl, D), lambda i, k: (i, 0)), pl.BlockSpec((tl, D), lambda i, k: (i, 0)),
                   pl.BlockSpec((tl, tc), lambda i, k: (i, k)),
                   pl.BlockSpec((2, tl, tc), lambda i, k: (0, i, k)), pl.BlockSpec((2, tl, tc), lambda i, k: (0, i, k))),
        out_shape=(jax.ShapeDtypeStruct((L, D), F32), jax.ShapeDtypeStruct((L, D), BF16),
                   jax.ShapeDtypeStruct((L, f), BF16),
                   jax.ShapeDtypeStruct((2, L, f), BF16), jax.ShapeDtypeStruct((2, L, f), BF16)),
        scratch_shapes=[pltpu.VMEM((pad, D), BF16), pltpu.VMEM((2, FFN_PAD, tc), F32), pltpu.VMEM((2, tl, tc), F32),
                        pltpu.VMEM((tl, D), F32)],
        sem=("parallel", "arbitrary"), args=(h_mid, h_mid, g, wup_t, wup_t, kf, kf, wdown), xchg=xchg)
    return out if xchg is None else (out, xo)


def _ffn_block_bwd(dh, h_mid, g, ux, uc, kf, wdown, wup_t, name, xchg=None):
    L, D = dh.shape
    f = ux.shape[2]
    kw = kf.shape[0]
    tl = _token_tile(L)
    tc = _divisor(f, 256, 128)
    nj = f // tc
    nt = L // tl
    pad = 2 * SUBLANES
    rc = CONV3_ROWS
    nc = tl // rc
    parts = _row_parts(nc)

    def body(dhc_ref, dhn_ref, hm_ref, g_ref, xg_ref, xv_ref, cg_ref, cgn_ref, cv_ref, cvn_ref, kg_ref, kv_ref,
             wd_ref, wg_ref, wv_ref, dhm_ref, dg_ref, du_ref, dk_ref, dh_ext, dact_s, acc):
        i = pl.program_id(0)
        kb = pl.program_id(1)

        @pl.when(kb == 0)
        def _():
            dh_ext[0:tl, :] = dhc_ref[...].astype(BF16)
            dh_ext[tl:tl + pad, :] = dhn_ref[...].astype(BF16)
            acc[...] = jnp.zeros_like(acc)

        @pl.when(jnp.logical_and(i == 0, kb == 0))
        def _():
            dg_ref[...] = jnp.zeros_like(dg_ref)
            dk_ref[...] = jnp.zeros_like(dk_ref)

        x_refs, c_refs, nxt = (xg_ref, xv_ref), (cg_ref, cv_ref), (cgn_ref, cvn_ref)
        taps = (_tap_rows(kg_ref), _tap_rows(kv_ref))
        dk = [[jnp.zeros((SUBLANES, tc), F32) for _ in range(kw)] for _ in range(2)]

        def dact_part(lo, hi):
            a, b = lo * rc, hi * rc + pad
            dact_s[a:b, :] = _dot_nt(dh_ext[a:b, :], wd_ref[...])

        def dhn_part(lo, hi):
            a, b = lo * rc, hi * rc
            acc[a:b, :] += (jnp.dot(du_ref[0, a:b, :], wg_ref[...], preferred_element_type=F32)
                            + jnp.dot(du_ref[1, a:b, :], wv_ref[...], preferred_element_type=F32))

        for p, (lo, hi) in enumerate(parts):
            if p == 0:
                dact_part(lo, hi)
            if p + 1 < len(parts):
                dact_part(*parts[p + 1])
            if p > 0:
                dhn_part(*parts[p - 1])
            chunk_rows(lo, hi, x_refs, c_refs, nxt, taps, dk, i, dact_s, du_ref)
        dhn_part(*parts[-1])
        for h in range(2):
            for j in range(kw):
                dk_ref[kb, h, j:j + 1, :] += jnp.sum(dk[h][j], axis=0, keepdims=True)

        @pl.when(kb == nj - 1)
        def _():
            x = hm_ref[...]
            r = lax.rsqrt(jnp.mean(x * x, axis=-1, keepdims=True) + EPS)
            xhat = x * r
            dhn = acc[...]
            dxhat = dhn * g_ref[...]
            dhm_ref[...] = dhc_ref[...] + r * (dxhat - xhat * jnp.mean(dxhat * xhat, axis=-1, keepdims=True))
            dg_ref[...] += jnp.sum(_rowsum8(dhn * xhat), axis=0, keepdims=True)

    def chunk_rows(lo, hi, x_refs, c_refs, nxt, taps, dk, i, dact_s, du_ref):
        for c in range(lo, hi):
            r0 = c * rc
            n = rc + FFN_PAD
            convd = []
            for h in range(2):
                if c == nc - 1:
                    rows = jnp.concatenate([c_refs[h][r0:r0 + rc, :], nxt[h][...]], axis=0)
                else:
                    rows = c_refs[h][r0:r0 + rc + pad, :]
                convd.append(rows.astype(F32)[0:n])
            gate, val = convd
            xs = [x_refs[h][r0:r0 + rc, :].astype(F32) for h in range(2)]
            dact = dact_s[r0:r0 + n, :]
            sg = _sigmoid(gate)
            dcs = [dact * val * (sg * (1.0 + gate * (1.0 - sg))), dact * (gate * sg)]
            if c == nc - 1:
                live = jnp.logical_or(lax.broadcasted_iota(jnp.int32, (n, 1), 0) < rc, i < nt - 1)
                dcs = [jnp.where(live, d, 0.0) for d in dcs]
            for h in range(2):
                xc = xs[h]
                dx = None
                for j in range(kw):
                    o = kw - 1 - j
                    sh = dcs[h][o:o + rc]
                    term = _rows_of(taps[h][j], rc) * sh
                    dx = term if dx is None else dx + term
                    dk[h][j] = dk[h][j] + _rowsum8(xc * sh)
                du_ref[h, r0:r0 + rc, :] = dx.astype(BF16)

    def after(i):
        return jnp.minimum((i + 1) * (tl // pad), L // pad - 1)

    def half(h, rows, idx):
        return pl.BlockSpec((None, rows, tc), lambda i, k: (h,) + idx(i, k))

    def tile(i, k):
        return (i, k)

    def behind(i, k):
        return (after(i), k)

    out, xo = _call(
        body, name=name, grid=(nt, nj),
        in_specs=[pl.BlockSpec((tl, D), lambda i, k: (i, 0)),
                  pl.BlockSpec((pad, D), lambda i, k: (after(i), 0)),
                  pl.BlockSpec((tl, D), lambda i, k: (i, 0)), pl.BlockSpec((1, D), lambda i, k: (0, 0)),
                  half(0, tl, tile), half(1, tl, tile),
                  half(0, tl, tile), half(0, pad, behind), half(1, tl, tile), half(1, pad, behind),
                  pl.BlockSpec((kw, tc), lambda i, k: (0, k)), pl.BlockSpec((kw, tc), lambda i, k: (0, k + nj)),
                  pl.BlockSpec((tc, D), lambda i, k: (k, 0)),
                  pl.BlockSpec((tc, D), lambda i, k: (k, 0)), pl.BlockSpec((tc, D), lambda i, k: (k + nj, 0))],
        out_specs=(pl.BlockSpec((tl, D), lambda i, k: (i, 0)), pl.BlockSpec((1, D), lambda i, k: (0, 0)),
                   pl.BlockSpec((2, tl, tc), lambda i, k: (0, i, k)),
                   pl.BlockSpec((nj, 2, kw, tc), lambda i, k: (0, 0, 0, 0))),
        out_shape=(jax.ShapeDtypeStruct((L, D), F32), jax.ShapeDtypeStruct((1, D), F32),
                   jax.ShapeDtypeStruct((2, L, f), BF16), jax.ShapeDtypeStruct((nj, 2, kw, tc), F32)),
        scratch_shapes=[pltpu.VMEM((tl + pad, D), BF16), pltpu.VMEM((tl + pad, tc), F32), pltpu.VMEM((tl, D), F32)],
        sem=("arbitrary", "arbitrary"), args=(dh, dh, h_mid, g, ux, ux, uc, uc, uc, uc, kf, kf, wdown, wup_t, wup_t),
        xchg=xchg)
    return out if xchg is None else (out, xo)


def _adamw_math(w, g, m, v):
    m = ADAM_B1 * m + (1.0 - ADAM_B1) * g
    v = ADAM_B2 * v + (1.0 - ADAM_B2) * (g * g)
    m_hat = m / (1.0 - ADAM_B1 ** ADAM_STEP)
    v_hat = v / (1.0 - ADAM_B2 ** ADAM_STEP)
    delta = -ADAM_LR * (m_hat / (jnp.sqrt(v_hat) + ADAM_EPS) + ADAM_WD * w)
    return delta, m, v


def _sum_parts(parts_ref, idx):
    g = parts_ref[(0,) + idx].astype(F32)
    for q in range(1, N_DEV):
        g = g + parts_ref[(q,) + idx].astype(F32)
    return g


def _adamw_big(parts, w, m, v, name):
    nl, R, C = w.shape
    tr = _divisor(R, 256, 2 * SUBLANES)

    def body(*refs):
        p_refs = refs[:nl]
        w_ref, m_ref, v_ref, g_ref, d_ref, nm_ref, nv_ref = refs[nl:]
        layer = pl.program_id(0)
        for k in range(nl):
            @pl.when(layer == k)
            def _(k=k):
                g = _sum_parts(p_refs[k], ())
                d, nm, nv = _adamw_math(w_ref[0], g, m_ref[0], v_ref[0])
                g_ref[0] = g
                d_ref[0] = d
                nm_ref[0] = nm
                nv_ref[0] = nv

    def part_spec(k):
        return pl.BlockSpec((N_DEV, tr, C), lambda l, r: (0, jnp.where(l == k, r, 0), 0))

    blk = pl.BlockSpec((1, tr, C), lambda l, r: (l, r, 0))
    shp = jax.ShapeDtypeStruct((nl, R, C), F32)
    return pl.pallas_call(
        body, name=name, grid=(nl, R // tr),
        in_specs=[part_spec(k) for k in range(nl)] + [blk, blk, blk],
        out_specs=(blk, blk, blk, blk), out_shape=(shp, shp, shp, shp),
        compiler_params=_params(("arbitrary", "arbitrary")),
    )(*parts, w, m, v)


def _adamw_small(entries, name):
    n = len(entries)
    uniq = []
    for e in entries:
        if not any(e[0] is u for u in uniq):
            uniq.append(e[0])
    pidx = [next(k for k, u in enumerate(uniq) if u is e[0]) for e in entries]
    npart = len(uniq)

    def body(*refs):
        p_refs = refs[:npart]
        wmv = refs[npart:npart + 3 * n]
        outs = refs[npart + 3 * n:]
        for t, e in enumerate(entries):
            lo, w = e[1], e[2]
            rows = w.shape[0]
            pr = p_refs[pidx[t]]
            g = pr[0, lo:lo + rows].astype(F32)
            for q in range(1, N_DEV):
                g = g + pr[q, lo:lo + rows].astype(F32)
            d, nm, nv = _adamw_math(wmv[3 * t][...], g, wmv[3 * t + 1][...], wmv[3 * t + 2][...])
            outs[4 * t][...] = g
            outs[4 * t + 1][...] = d
            outs[4 * t + 2][...] = nm
            outs[4 * t + 3][...] = nv

    vm = pl.BlockSpec(memory_space=pltpu.VMEM)
    args = list(uniq)
    out_shape = []
    for e in entries:
        args += [e[2], e[3], e[4]]
        out_shape += [jax.ShapeDtypeStruct(e[2].shape, F32)] * 4
    res = pl.pallas_call(
        body, name=name, in_specs=[vm] * len(args), out_specs=tuple([vm] * len(out_shape)),
        out_shape=tuple(out_shape), compiler_params=_params(),
    )(*args)
    return [tuple(res[4 * t:4 * t + 4]) for t in range(n)]


def _head_matrix(cc):
    bw = min(256, cc)
    r = lax.broadcasted_iota(jnp.int32, (bw, bw), 0) // HEAD_DIM
    c = lax.broadcasted_iota(jnp.int32, (bw, bw), 1) // HEAD_DIM
    return jnp.where(r == c, 1.0 / HEAD_DIM, 0.0).astype(BF16)


def _cols_from_shards(g):
    nd = g.ndim
    perm = tuple(range(1, nd - 1)) + (0, nd - 1)
    t = jnp.transpose(g, perm)
    return t.reshape(t.shape[:-2] + (t.shape[-2] * t.shape[-1],))


def _cols_to_shards(a):
    nd = a.ndim
    t = a.reshape(a.shape[:-1] + (N_DEV, a.shape[-1] // N_DEV))
    perm = (nd - 1,) + tuple(range(nd - 1)) + (nd,)
    return jnp.transpose(t, perm)


def kernel(x, meta_tokens, norm1_g, w_in, conv_dw_k, conv_dw_b, conv_ln_g, conv_ln_b, pool_w, pool_scale, w_out, norm2_g, w_up, ffn_dw_k, w_down, final_g, loss_target, m_meta_tokens, m_norm1_g, m_w_in, m_conv_dw_k, m_conv_dw_b, m_conv_ln_g, m_conv_ln_b, m_pool_w, m_pool_scale, m_w_out, m_norm2_g, m_w_up, m_ffn_dw_k, m_w_down, m_final_g, v_meta_tokens, v_norm1_g, v_w_in, v_conv_dw_k, v_conv_dw_b, v_conv_ln_g, v_conv_ln_b, v_pool_w, v_pool_scale, v_w_out, v_norm2_g, v_w_up, v_ffn_dw_k, v_w_down, v_final_g):
    depth, D = norm1_g.shape
    n_meta = meta_tokens.shape[0]
    seq = x.shape[1]
    L = n_meta + seq
    cc = conv_dw_b.shape[1]
    ng, gd = pool_w.shape[1], pool_w.shape[2]
    f = w_down.shape[1] * N_DEV

    def rows(g):
        return g.reshape(-1, g.shape[-1])

    b16 = lambda a: a.astype(BF16)
    tr = lambda a: jnp.swapaxes(a, -1, -2)
    w_in_t, m_w_in_t, v_w_in_t = tr(w_in), tr(m_w_in), tr(v_w_in)
    w_up_t, m_w_up_t, v_w_up_t = tr(w_up), tr(m_w_up), tr(v_w_up)
    (g_in0, g_out0, g_ck, g_kf, g_meta) = _exchange([b16(w_in_t[0]), b16(w_out[0]), conv_dw_k, ffn_dw_k, meta_tokens],
                                                    ["gather"] * 5, "gather_first")
    ck_full = _cols_from_shards(g_ck)
    ck_rows = jnp.broadcast_to(ck_full[:, :, None, :], ck_full.shape[:2] + (SUBLANES, cc))
    kf_full = _cols_from_shards(g_kf)
    meta_full = _cols_from_shards(g_meta)
    am = _head_matrix(cc)
    win, wout, wup, wdown = [None] * depth, [None] * depth, [None] * depth, [None] * depth
    win[0] = rows(g_in0)
    wout[0] = rows(g_out0)

    h = jnp.concatenate([meta_full, x[0]], axis=0)
    saved = []
    for l in range(depth):
        more = l + 1 < depth
        if l == 0:
            (z, hn1), (g_down,) = _norm_proj(h, norm1_g[l:l + 1], win[l], f"in_proj_{l}", tn_cap=1536,
                                             xchg=([b16(w_down[l])], ["gather"]))
            wdown[l] = rows(g_down)
            (ymix, u1), (g_up,) = _mixer_fwd(z, ck_rows[l], conv_dw_b[l:l + 1], conv_ln_g[l:l + 1], conv_ln_b[l:l + 1],
                                             pool_w[l], pool_scale[l:l + 1], am, f"mixer_fwd_{l}",
                                             xchg=([b16(w_up_t[l])], ["gather"]))
            wup[l] = rows(g_up)
        else:
            z, hn1 = _norm_proj(h, norm1_g[l:l + 1], win[l], f"in_proj_{l}", tn_cap=1536)
            ymix, u1 = _mixer_fwd(z, ck_rows[l], conv_dw_b[l:l + 1], conv_ln_g[l:l + 1], conv_ln_b[l:l + 1], pool_w[l],
                                  pool_scale[l:l + 1], am, f"mixer_fwd_{l}")
        if more:
            h_mid, (g_in,) = _mm(ymix, wout[l], f"out_proj_{l}", res=h, tn_cap=1024, xchg=([b16(w_in_t[l + 1])], ["gather"]))
            win[l + 1] = rows(g_in)
            nxt = [b16(w_out[l + 1]), b16(w_up_t[l + 1]), b16(w_down[l + 1])]
            (h_out, hn2, act, ux, uc), got = _ffn_block_fwd(h_mid, norm2_g[l:l + 1], wup[l], kf_full[l], wdown[l],
                                                            f"ffn_fwd_{l}", xchg=(nxt, ["gather"] * 3))
            wout[l + 1], wup[l + 1], wdown[l + 1] = rows(got[0]), rows(got[1]), rows(got[2])
        else:
            h_mid = _mm(ymix, wout[l], f"out_proj_{l}", res=h, tn_cap=1024)
            h_out, hn2, act, ux, uc = _ffn_block_fwd(h_mid, norm2_g[l:l + 1], wup[l], kf_full[l], wdown[l], f"ffn_fwd_{l}")
        saved.append((h, hn1, z, u1, ymix, h_mid, hn2, ux, uc, act))
        h = h_out

    dh, d_final_g, loss_part = _loss_head(h, final_g.reshape(1, D), loss_target[0], n_meta, "loss_head")

    def row_shards(gm):
        return gm.reshape(N_DEV, -1, gm.shape[-1])

    zero_row = jnp.zeros((1, D), F32)
    gw = {k: [None] * depth for k in ("ck", "cb", "lg", "lb", "pw", "ps", "kf", "n1", "n2")}
    parts = {k: [None] * depth for k in ("in", "out", "up", "down")}
    for l in reversed(range(depth)):
        h_in, hn1, z, u1, ymix, h_mid, hn2, ux, uc, act = saved[l]
        g_down = _mm_tn(act, dh, f"down_proj_wgrad_{l}", tq_cap=512)
        (dh_mid, gw["n2"][l], dug0, dkf), (parts["down"][l],) = _ffn_block_bwd(
            dh, h_mid, norm2_g[l:l + 1], ux, uc, kf_full[l], wdown[l], wup[l], f"ffn_bwd_{l}",
            xchg=([row_shards(g_down)], ["a2a"]))
        gw["kf"][l] = jnp.transpose(dkf, (2, 1, 0, 3)).reshape(dkf.shape[2], -1)
        g_up_t = _mm_tn(dug0, hn2, f"up_proj_wgrad_{l}", halves=2, tq_cap=1024)
        dymix = _mm(dh_mid, wout[l], f"out_proj_bwd_{l}", b_t=True, tn_cap=1024)
        g_out = _mm_tn(ymix, dh_mid, f"out_proj_wgrad_{l}", tq_cap=512)
        ((dz, gw["ck"][l], gw["cb"][l], gw["lg"][l], gw["lb"][l], gw["pw"][l], gw["ps"][l]),
         (parts["up"][l], parts["out"][l])) = _mixer_bwd(
            z, u1, dymix, ck_rows[l], conv_ln_g[l:l + 1], conv_ln_b[l:l + 1], pool_w[l], pool_scale[l:l + 1], am,
            f"mixer_bwd_{l}", xchg=([row_shards(g_up_t), row_shards(g_out)], ["a2a", "a2a"]))
        g_in_t = _mm_tn(dz, hn1, f"in_proj_wgrad_{l}", tq_cap=1024)
        if l > 0:
            (dh, gw["n1"][l]), (parts["in"][l],) = _proj_bwd_norm(dz, win[l], h_in, norm1_g[l:l + 1], dh_mid, zero_row,
                                                                  f"in_proj_bwd_{l}", xchg=([row_shards(g_in_t)], ["a2a"]))
        else:
            (grad_x, dg_x), (parts["in"][l],) = _proj_bwd_norm(dz, win[l], h_in, norm1_g[l:l + 1], dh_mid, zero_row,
                                                               f"in_proj_bwd_{l}", skip=n_meta,
                                                               xchg=([row_shards(g_in_t)], ["a2a"]))
            d_meta, gw["n1"][l] = _proj_bwd_norm(dz[:n_meta], win[l], h_in[:n_meta], norm1_g[l:l + 1], dh_mid[:n_meta],
                                                 dg_x, f"in_proj_bwd_meta_{l}")
    grad_x = grad_x[None]

    pack_d = jnp.concatenate(gw["n1"] + gw["n2"] + [d_final_g, jnp.broadcast_to(loss_part[:, :1], (1, D)), zero_row, zero_row], axis=0)
    pack_c = jnp.concatenate(gw["cb"] + gw["lg"] + gw["lb"] + gw["ps"], axis=0)
    pack_pw = jnp.stack(gw["pw"]).reshape(depth * ng * gd, gd)
    src = [_cols_to_shards(jnp.stack(gw["ck"])), _cols_to_shards(jnp.stack(gw["kf"])), _cols_to_shards(d_meta),
           pack_d, pack_c, pack_pw]
    r_ck, r_kf, r_meta, r_d, r_c, r_pw = _exchange(src, ["a2a"] * 3 + ["gather"] * 3, "exchange_small_grads")

    big = {
        "w_in": tuple(tr(a) for a in _adamw_big(parts["in"], w_in_t, m_w_in_t, v_w_in_t, "adamw_w_in")),
        "w_out": _adamw_big(parts["out"], w_out, m_w_out, v_w_out, "adamw_w_out"),
        "w_up": tuple(tr(a) for a in _adamw_big(parts["up"], w_up_t, m_w_up_t, v_w_up_t, "adamw_w_up")),
        "w_down": _adamw_big(parts["down"], w_down, m_w_down, v_w_down, "adamw_w_down"),
    }
    kwid = conv_dw_k.shape[1]
    fkw = ffn_dw_k.shape[1]
    row = lambda a: a.reshape(1, -1)
    entries = [
        (r_d, 0, norm1_g, m_norm1_g, v_norm1_g),
        (r_d, depth, norm2_g, m_norm2_g, v_norm2_g),
        (r_d, 2 * depth, row(final_g), row(m_final_g), row(v_final_g)),
        (r_c, 0, conv_dw_b, m_conv_dw_b, v_conv_dw_b),
        (r_c, depth, conv_ln_g, m_conv_ln_g, v_conv_ln_g),
        (r_c, 2 * depth, conv_ln_b, m_conv_ln_b, v_conv_ln_b),
        (r_c, 3 * depth, pool_scale, m_pool_scale, v_pool_scale),
        (r_pw, 0, pool_w.reshape(-1, gd), m_pool_w.reshape(-1, gd), v_pool_w.reshape(-1, gd)),
        (r_ck.reshape(N_DEV, depth * kwid, -1), 0, conv_dw_k.reshape(depth * kwid, -1),
         m_conv_dw_k.reshape(depth * kwid, -1), v_conv_dw_k.reshape(depth * kwid, -1)),
        (r_kf.reshape(N_DEV, depth * fkw, -1), 0, ffn_dw_k.reshape(depth * fkw, -1),
         m_ffn_dw_k.reshape(depth * fkw, -1), v_ffn_dw_k.reshape(depth * fkw, -1)),
        (r_meta, 0, meta_tokens, m_meta_tokens, v_meta_tokens),
        (r_d, 2 * depth + 1, zero_row, zero_row, zero_row),
    ]
    small = _adamw_small(entries, "adamw_small")
    names = ["norm1_g", "norm2_g", "final_g", "conv_dw_b", "conv_ln_g", "conv_ln_b", "pool_scale", "pool_w",
             "conv_dw_k", "ffn_dw_k", "meta_tokens"]
    shapes = {"final_g": final_g.shape, "pool_w": pool_w.shape, "conv_dw_k": conv_dw_k.shape, "ffn_dw_k": ffn_dw_k.shape}
    res = dict(big)
    for nme, quad in zip(names, small[:-1]):
        res[nme] = tuple(a.reshape(shapes[nme]) if nme in shapes else a for a in quad)
    loss = small[-1][0][0, 0]

    order = ["meta_tokens", "norm1_g", "w_in", "conv_dw_k", "conv_dw_b", "conv_ln_g", "conv_ln_b", "pool_w", "pool_scale",
             "w_out", "norm2_g", "w_up", "ffn_dw_k", "w_down", "final_g"]
    return (loss, grad_x, *[res[k][0] for k in order], *[res[k][1] for k in order], *[res[k][2] for k in order],
            *[res[k][3] for k in order])
```

```python
import functools

import jax
import jax.numpy as jnp
from jax import lax
from jax.experimental import pallas as pl
from jax.experimental.pallas import tpu as pltpu

F32 = jnp.float32
BF16 = jnp.bfloat16

EPS = 1e-6
HEAD_DIM = 64
POOL_WINDOWS = (2, 4, 8, 16)
ADAM_LR = 0.001
ADAM_B1 = 0.9
ADAM_B2 = 0.999
ADAM_EPS = 1e-08
ADAM_WD = 0.01
ADAM_STEP = 10

N_DEV = 8
OTHER_CHIPS = (2, 4, 6)
SUBLANES = 8
HALO = 48
CONV_PAD = 32
POOL_PAD = 16
FFN_PAD = 8
ROW_CHUNK = 24
CONV3_ROWS = 48
MAX_TILE_ROWS = 1024
WGRAD_TILE_ROWS = 2816
VMEM_LIMIT = 52 * 1024 * 1024


def _divisor(n, cap, mult):
    best = None
    for d in range(mult, min(n, cap) + 1, mult):
        if n % d == 0:
            best = d
    return n if best is None else best


def _token_tile(L):
    return _divisor(L, MAX_TILE_ROWS, HALO)


def _stat_rows(tl):
    return _divisor(tl, 512, SUBLANES)


def _params(sem=None):
    return pltpu.CompilerParams(dimension_semantics=sem, vmem_limit_bytes=VMEM_LIMIT)


def _rowsum8(x):
    acc = x[0:SUBLANES]
    for k in range(1, x.shape[0] // SUBLANES):
        acc = acc + x[k * SUBLANES:(k + 1) * SUBLANES]
    return acc


def _sigmoid(x):
    return jax.nn.sigmoid(x)


def _dot_nt(a, b):
    return lax.dot_general(a, b, (((1,), (1,)), ((), ())), preferred_element_type=F32)


def _head_mean(x, am_ref):
    bw = am_ref.shape[0]
    am = am_ref[...]
    outs = []
    for blk in range(x.shape[1] // bw):
        xb = x[:, blk * bw:(blk + 1) * bw]
        hi = xb.astype(BF16)
        lo = (xb - hi.astype(F32)).astype(BF16)
        outs.append(jnp.dot(hi, am, preferred_element_type=F32) + jnp.dot(lo, am, preferred_element_type=F32))
    return outs[0] if len(outs) == 1 else jnp.concatenate(outs, axis=-1)


def _xchg_out_shapes(srcs, modes):
    out = []
    for s, m in zip(srcs, modes):
        shp = ((N_DEV,) + tuple(s.shape)) if m == "gather" else tuple(s.shape)
        out.append(jax.ShapeDtypeStruct(shp, s.dtype))
    return out


def _xchg_sems(n):
    return [pltpu.SemaphoreType.DMA((n, N_DEV - 1)), pltpu.SemaphoreType.DMA((n, N_DEV - 1)), pltpu.SemaphoreType.DMA((n,))]


def _xchg_ops(src_refs, out_refs, sems, modes):
    n = len(src_refs)
    send_sems, recv_sems, local_sems = sems
    x, y, c = lax.axis_index("x"), lax.axis_index("y"), lax.axis_index("c")
    me = 4 * x + 2 * y + c

    def peer(d):
        return (x ^ ((d >> 2) & 1), y ^ ((d >> 1) & 1), c ^ (d & 1))

    def peer_id(d):
        px, py, pc = peer(d)
        return 4 * px + 2 * py + pc

    def remote(t, d):
        src = src_refs[t] if modes[t] == "gather" else src_refs[t].at[peer_id(d)]
        return pltpu.make_async_remote_copy(
            src_ref=src, dst_ref=out_refs[t].at[me], send_sem=send_sems.at[t, d - 1], recv_sem=recv_sems.at[t, d - 1],
            device_id=peer(d), device_id_type=pl.DeviceIdType.MESH)

    def arrival(t, d):
        src = src_refs[t] if modes[t] == "gather" else src_refs[t].at[me]
        return pltpu.make_async_remote_copy(
            src_ref=src, dst_ref=out_refs[t].at[peer_id(d)], send_sem=send_sems.at[t, d - 1],
            recv_sem=recv_sems.at[t, d - 1], device_id=peer(d), device_id_type=pl.DeviceIdType.MESH)

    def passed_on(t, d):
        blk = out_refs[t].at[peer_id(d)]
        return pltpu.make_async_remote_copy(
            src_ref=blk, dst_ref=blk, send_sem=send_sems.at[t, d], recv_sem=recv_sems.at[t, d],
            device_id=peer(1), device_id_type=pl.DeviceIdType.MESH)

    def local(t):
        src = src_refs[t] if modes[t] == "gather" else src_refs[t].at[me]
        return pltpu.make_async_copy(src, out_refs[t].at[me], local_sems.at[t])

    def sent_first(t):
        return OTHER_CHIPS + (1,) if modes[t] == "gather" else tuple(range(1, N_DEV))

    def start():
        for t in range(n):
            local(t).start()
        for t in range(n):
            for d in sent_first(t):
                remote(t, d).start()

    def wait():
        gathered = [t for t in range(n) if modes[t] == "gather"]
        for t in gathered:
            for d in OTHER_CHIPS:
                arrival(t, d).wait_recv()
                passed_on(t, d).start()
        for t in range(n):
            for d in range(1, N_DEV):
                if not (modes[t] == "gather" and d in OTHER_CHIPS):
                    arrival(t, d).wait_recv()
        for t in range(n):
            for d in sent_first(t):
                remote(t, d).wait_send()
        for t in gathered:
            for d in OTHER_CHIPS:
                passed_on(t, d).wait_send()
        for t in range(n):
            local(t).wait()

    return start, wait


def _exchange(srcs, modes, name):
    n = len(srcs)

    def body(*refs):
        start, wait = _xchg_ops(refs[:n], refs[n:2 * n], refs[2 * n:], modes)
        start()
        wait()

    any_spec = pl.BlockSpec(memory_space=pl.ANY)
    return pl.pallas_call(
        body, name=name, out_shape=tuple(_xchg_out_shapes(srcs, modes)),
        in_specs=[any_spec] * n, out_specs=tuple([any_spec] * n),
        scratch_shapes=_xchg_sems(n),
        compiler_params=pltpu.CompilerParams(has_side_effects=True),
    )(*srcs)


def _call(body, *, name, grid, in_specs, out_specs, out_shape, args, scratch_shapes=(), sem=None, xchg=None):
    single = not isinstance(out_shape, (tuple, list))
    outs_shape = [out_shape] if single else list(out_shape)
    outs_spec = [out_specs] if single else list(out_specs)
    if xchg is None:
        res = pl.pallas_call(
            body, name=name, grid=grid, in_specs=list(in_specs), out_specs=out_specs, out_shape=out_shape,
            scratch_shapes=list(scratch_shapes), compiler_params=_params(sem))(*args)
        return res, ()
    srcs, modes = xchg
    n_in, n_out, n_scr, nx = len(in_specs), len(outs_shape), len(scratch_shapes), len(srcs)

    def wrapped(*refs):
        ins = refs[:n_in]
        xs = refs[n_in:n_in + nx]
        o0 = n_in + nx
        outs = refs[o0:o0 + n_out]
        xo = refs[o0 + n_out:o0 + n_out + nx]
        s0 = o0 + n_out + nx
        scr = refs[s0:s0 + n_scr]
        start, wait = _xchg_ops(xs, xo, refs[s0 + n_scr:], modes)
        first = functools.reduce(jnp.logical_and, [pl.program_id(a) == 0 for a in range(len(grid))])
        last = functools.reduce(jnp.logical_and, [pl.program_id(a) == grid[a] - 1 for a in range(len(grid))])

        @pl.when(first)
        def _():
            start()

        body(*ins, *outs, *scr)

        @pl.when(last)
        def _():
            wait()

    any_spec = pl.BlockSpec(memory_space=pl.ANY)
    res = pl.pallas_call(
        wrapped, name=name, grid=grid, in_specs=list(in_specs) + [any_spec] * nx,
        out_specs=tuple(outs_spec + [any_spec] * nx), out_shape=tuple(outs_shape + _xchg_out_shapes(srcs, modes)),
        scratch_shapes=list(scratch_shapes) + _xchg_sems(nx),
        compiler_params=_params(("arbitrary",) * len(grid)))(*args, *srcs)
    comp = res[:n_out]
    return (comp[0] if single else tuple(comp)), tuple(res[n_out:])


def _norm_proj(h, g, w, name, *, tn_cap, xchg=None):
    L, D = h.shape
    N = w.shape[0]
    tm = _token_tile(L)
    tn = _divisor(N, tn_cap, 128)

    def body(h_ref, g_ref, w_ref, z_ref, hn_ref):
        @pl.when(pl.program_id(1) == 0)
        def _():
            x = h_ref[...]
            r = lax.rsqrt(jnp.mean(x * x, axis=-1, keepdims=True) + EPS)
            hn_ref[...] = ((x * r) * g_ref[...]).astype(BF16)

        z_ref[...] = _dot_nt(hn_ref[...], w_ref[...])

    out, xo = _call(
        body, name=name, grid=(L // tm, N // tn),
        in_specs=[pl.BlockSpec((tm, D), lambda i, j: (i, 0)), pl.BlockSpec((1, D), lambda i, j: (0, 0)),
                  pl.BlockSpec((tn, D), lambda i, j: (j, 0))],
        out_specs=(pl.BlockSpec((tm, tn), lambda i, j: (i, j)), pl.BlockSpec((tm, D), lambda i, j: (i, 0))),
        out_shape=(jax.ShapeDtypeStruct((L, N), F32), jax.ShapeDtypeStruct((L, D), BF16)),
        sem=("parallel", "arbitrary"), args=(h, g, w), xchg=xchg)
    return out if xchg is None else (out, xo)


def _proj_bwd_norm(a, b, h, g, dres, dg0, name, skip=0, xchg=None):
    L, K = a.shape
    D = b.shape[1]
    rows = L - skip
    tm = _divisor(rows, MAX_TILE_ROWS, 2 * SUBLANES) if skip else _token_tile(L)

    def body(a_ref, b_ref, h_ref, g_ref, dres_ref, dg0_ref, dh_ref, dg_ref):
        i = pl.program_id(0)
        dhn = jnp.dot(a_ref[...], b_ref[...], preferred_element_type=F32)
        x = h_ref[...]
        r = lax.rsqrt(jnp.mean(x * x, axis=-1, keepdims=True) + EPS)
        xhat = x * r
        dxhat = dhn * g_ref[...]
        dh_ref[...] = dres_ref[...] + r * (dxhat - xhat * jnp.mean(dxhat * xhat, axis=-1, keepdims=True))
        part = jnp.sum(_rowsum8(dhn * xhat), axis=0, keepdims=True)

        @pl.when(i == 0)
        def _():
            dg_ref[...] = dg0_ref[...] + part

        @pl.when(i > 0)
        def _():
            dg_ref[...] += part

    def rows_of(cols):
        if not skip:
            return pl.BlockSpec((tm, cols), lambda i: (i, 0))
        return pl.BlockSpec((pl.Element(tm), pl.Element(cols)), lambda i: (pl.multiple_of(skip + i * tm, SUBLANES), 0))

    row = pl.BlockSpec((1, D), lambda i: (0, 0))
    out, xo = _call(
        body, name=name, grid=(rows // tm,),
        in_specs=[rows_of(K), pl.BlockSpec((K, D), lambda i: (0, 0)), rows_of(D), row, rows_of(D), row],
        out_specs=(pl.BlockSpec((tm, D), lambda i: (i, 0)), row),
        out_shape=(jax.ShapeDtypeStruct((rows, D), F32), jax.ShapeDtypeStruct((1, D), F32)),
        sem=("arbitrary",), args=(a, b, h, g, dres, dg0), xchg=xchg)
    return out if xchg is None else (out, xo)


def _mm(a, b, name, *, res=None, b_t=False, tn_cap=1408, xchg=None):
    M, K = a.shape
    N = b.shape[0] if b_t else b.shape[1]
    tm = _token_tile(M)
    tn = _divisor(N, tn_cap, 128)

    def body(*refs):
        a_ref, b_ref = refs[:2]
        r_ref, o_ref = (None, refs[2]) if res is None else (refs[2], refs[3])
        av = a_ref[...].astype(BF16)
        prod = _dot_nt(av, b_ref[...]) if b_t else jnp.dot(av, b_ref[...], preferred_element_type=F32)
        o_ref[...] = prod if r_ref is None else prod + r_ref[...]

    b_spec = pl.BlockSpec((tn, K), lambda i, j: (j, 0)) if b_t else pl.BlockSpec((K, tn), lambda i, j: (0, j))
    in_specs = [pl.BlockSpec((tm, K), lambda i, j: (i, 0)), b_spec]
    args = [a, b]
    if res is not None:
        in_specs.append(pl.BlockSpec((tm, tn), lambda i, j: (i, j)))
        args.append(res)
    out, xo = _call(
        body, name=name, grid=(M // tm, N // tn), in_specs=in_specs,
        out_specs=pl.BlockSpec((tm, tn), lambda i, j: (i, j)), out_shape=jax.ShapeDtypeStruct((M, N), F32),
        sem=("parallel", "parallel"), args=args, xchg=xchg)
    return out if xchg is None else (out, xo)


def _mm_tn(a, b, name, *, halves=1, tq_cap=1408):
    L, Q = b.shape
    ph = a.shape[-1]
    P = ph * halves
    tl = _divisor(L, WGRAD_TILE_ROWS, HALO)
    tp = _divisor(ph, 1408, 128)
    tq = _divisor(Q, tq_cap, 128)
    pper = ph // tp
    nl = L // tl
    grid = (P // tp, Q // tq, nl)

    def body(a_ref, b_ref, o_ref, acc):
        prod = lax.dot_general(a_ref[...].astype(BF16), b_ref[...].astype(BF16), (((0,), (0,)), ((), ())),
                               preferred_element_type=F32)
        l = pl.program_id(2)
        if nl == 1:
            o_ref[...] = prod.astype(BF16)
            return

        @pl.when(l == 0)
        def _():
            acc[...] = prod

        @pl.when(jnp.logical_and(l > 0, l < nl - 1))
        def _():
            acc[...] += prod

        @pl.when(l == nl - 1)
        def _():
            o_ref[...] = (acc[...] + prod).astype(BF16)

    if halves > 1:
        a_spec = pl.BlockSpec((None, tl, tp), lambda p, q, l: (p // pper, l, p % pper))
    else:
        a_spec = pl.BlockSpec((tl, tp), lambda p, q, l: (l, p))
    return pl.pallas_call(
        body, name=name, grid=grid,
        in_specs=[a_spec, pl.BlockSpec((tl, tq), lambda p, q, l: (l, q))],
        out_specs=pl.BlockSpec((tp, tq), lambda p, q, l: (p, q)),
        out_shape=jax.ShapeDtypeStruct((P, Q), BF16),
        scratch_shapes=[pltpu.VMEM((tp, tq), F32)],
        compiler_params=_params(("parallel", "parallel", "arbitrary")),
    )(a, b)


def _loss_head(h, g, tgt, n_meta, name):
    L, D = h.shape
    tl = _token_tile(L)
    nt = L // tl

    def body(h_ref, g_ref, t_ref, dh_ref, dg_ref, loss_ref):
        i = pl.program_id(0)
        x = h_ref[...]
        r = lax.rsqrt(jnp.mean(x * x, axis=-1, keepdims=True) + EPS)
        xhat = x * r
        gg = g_ref[...]
        y = xhat * gg
        rows = i * tl + lax.broadcasted_iota(jnp.int32, (tl, 1), 0)
        t = t_ref[...]
        t = jnp.where(i == 0, pltpu.roll(t, n_meta, axis=0), t)
        err = jnp.where(rows >= n_meta, y - t, 0.0)
        dy = err * (1.0 / D)
        dxhat = dy * gg
        dh_ref[...] = r * (dxhat - xhat * jnp.mean(dxhat * xhat, axis=-1, keepdims=True))
        dg_part = jnp.sum(_rowsum8(dy * xhat), axis=0, keepdims=True)
        per_row = jnp.mean(err * err, axis=-1, keepdims=True)
        loss_part = jnp.broadcast_to(0.5 * jnp.sum(per_row, axis=0, keepdims=True), (1, 128))

        @pl.when(i == 0)
        def _():
            dg_ref[...] = dg_part
            loss_ref[...] = loss_part

        @pl.when(i > 0)
        def _():
            dg_ref[...] += dg_part
            loss_ref[...] += loss_part

    tile = pl.BlockSpec((tl, D), lambda i: (i, 0))
    row = pl.BlockSpec((1, D), lambda i: (0, 0))
    window = pl.BlockSpec((pl.Element(tl), pl.Element(D)),
                          lambda i: (pl.multiple_of(jnp.maximum(i * tl - n_meta, 0), SUBLANES), 0))
    return pl.pallas_call(
        body, name=name, grid=(nt,), in_specs=[tile, row, window],
        out_specs=(tile, row, pl.BlockSpec((1, 128), lambda i: (0, 0))),
        out_shape=(jax.ShapeDtypeStruct((L, D), F32), jax.ShapeDtypeStruct((1, D), F32),
                   jax.ShapeDtypeStruct((1, 128), F32)),
        compiler_params=_params(("arbitrary",)),
    )(h, g, tgt)


def _pool_fwd_block(pwin, pw_ref, row0, rb, g, gd, w, t0):
    wv = pwin[pl.ds(row0 + HALO - POOL_PAD, rb + POOL_PAD), g * gd:(g + 1) * gd]
    s = wv
    sh = 1
    while sh < w:
        s = s + pltpu.roll(s, sh, axis=0)
        sh *= 2
    win = s[POOL_PAD:POOL_PAD + rb]
    pt = wv[POOL_PAD:POOL_PAD + rb]
    tg = t0 + lax.broadcasted_iota(jnp.int32, (rb, 1), 0)
    cnt = jnp.minimum(tg + 1, w).astype(F32)
    return win / cnt - pt


def _fill_windows(i, zp_ref, zc_ref, u0w, pwin, tl, cc):
    keep = i > 0
    zp = zp_ref[...]
    u0w[0:HALO, :] = jnp.where(keep, zp[:, :cc] * _sigmoid(zp[:, cc:2 * cc]), 0.0)
    pwin[0:HALO, :] = jnp.where(keep, zp[:, 2 * cc:], 0.0)

    def fill(c, carry):
        b = pl.multiple_of(c * ROW_CHUNK, SUBLANES)
        zc = zc_ref[pl.ds(b, ROW_CHUNK), :]
        u0w[pl.ds(HALO + b, ROW_CHUNK), :] = zc[:, :cc] * _sigmoid(zc[:, cc:2 * cc])
        pwin[pl.ds(HALO + b, ROW_CHUNK), :] = zc[:, 2 * cc:]
        return carry

    lax.fori_loop(0, tl // ROW_CHUNK, fill, 0)


def _mixer_fwd(z, ck, cb, lg, lb, pw, ps, am, name, xchg=None):
    L, ci = z.shape
    kw, _, cc = ck.shape
    cp = ci - 2 * cc
    ng, gd = pw.shape[0], pw.shape[1]
    tl = _token_tile(L)
    nt = L // tl
    hb = tl // HALO
    rb = _stat_rows(tl)
    tap0 = CONV_PAD - (kw - 1)

    def body(zp_ref, zc_ref, ck_ref, cb_ref, lg_ref, lb_ref, pw_ref, ps_ref, am_ref, y_ref, u1_ref, u0w, pwin):
        i = pl.program_id(0)
        _fill_windows(i, zp_ref, zc_ref, u0w, pwin, tl, cc)

        def conv(c, carry):
            b = pl.multiple_of(c * ROW_CHUNK, SUBLANES)
            w = u0w[pl.ds(b + HALO - CONV_PAD, ROW_CHUNK + CONV_PAD), :]
            acc = jnp.broadcast_to(cb_ref[...], (ROW_CHUNK, cc))
            for j in range(kw):
                acc = acc + _rows_of(ck_ref[j], ROW_CHUNK) * w[tap0 + j:tap0 + j + ROW_CHUNK]
            u1_ref[pl.ds(b, ROW_CHUNK), :] = acc
            return carry

        lax.fori_loop(0, tl // ROW_CHUNK, conv, 0)

        def blocks(k, carry):
            b = pl.multiple_of(k * rb, SUBLANES)
            u1 = u1_ref[pl.ds(b, rb), :]
            xc = u1 - _head_mean(u1, am_ref)
            var = _head_mean(xc * xc, am_ref)
            u2 = (xc * lax.rsqrt(var + EPS)) * lg_ref[...] + lb_ref[...]
            y_ref[pl.ds(b, rb), 0:cc] = (u2 * _sigmoid(u2)).astype(y_ref.dtype)
            for g in range(ng):
                d = _pool_fwd_block(pwin, pw_ref, b, rb, g, gd, POOL_WINDOWS[g], i * tl + b)
                yp = jnp.dot(d.astype(BF16), pw_ref[g].astype(BF16), preferred_element_type=F32)
                yp = yp * ps_ref[:, g * gd:(g + 1) * gd]
                y_ref[pl.ds(b, rb), cc + g * gd:cc + (g + 1) * gd] = yp.astype(y_ref.dtype)
            return carry

        lax.fori_loop(0, tl // rb, blocks, 0)

    def full(a):
        nd = a.ndim
        return pl.BlockSpec(a.shape, lambda i: (0,) * nd)

    out, xo = _call(
        body, name=name, grid=(nt,),
        in_specs=[pl.BlockSpec((HALO, ci), lambda i: (jnp.maximum(i * hb - 1, 0), 0)),
                  pl.BlockSpec((tl, ci), lambda i: (i, 0)),
                  full(ck), full(cb), full(lg), full(lb), full(pw), full(ps), full(am)],
        out_specs=(pl.BlockSpec((tl, cc + cp), lambda i: (i, 0)), pl.BlockSpec((tl, cc), lambda i: (i, 0))),
        out_shape=(jax.ShapeDtypeStruct((L, cc + cp), BF16), jax.ShapeDtypeStruct((L, cc), F32)),
        scratch_shapes=[pltpu.VMEM((HALO + tl, cc), F32), pltpu.VMEM((HALO + tl, cp), F32)],
        sem=("parallel",), args=(z, z, ck, cb, lg, lb, pw, ps, am), xchg=xchg)
    return out if xchg is None else (out, xo)


def _mixer_bwd(z, u1, dy, ck, lg, lb, pw, ps, am, name, xchg=None):
    L, ci = z.shape
    kw, _, cc = ck.shape
    cp = ci - 2 * cc
    ng, gd = pw.shape[0], pw.shape[1]
    tl = _token_tile(L)
    nt = L // tl
    hb = tl // HALO
    rb = _stat_rows(tl)

    def body(zp_ref, zc_ref, u1c_ref, u1n_ref, dyc_ref, dyn_ref, ck_ref, lg_ref, lb_ref, pw_ref, ps_ref, am_ref,
             dz_ref, dck_ref, dcb_ref, dlg_ref, dlb_ref, dpw_ref, dps_ref,
             u0w, pwin, du1w, ddw, ew, dkacc, dcb8, dlg8, dlb8, dps8):
        i = pl.program_id(0)
        has_next = i < nt - 1

        @pl.when(i == 0)
        def _():
            dck_ref[...] = jnp.zeros_like(dck_ref)
            dcb_ref[...] = jnp.zeros_like(dcb_ref)
            dlg_ref[...] = jnp.zeros_like(dlg_ref)
            dlb_ref[...] = jnp.zeros_like(dlb_ref)
            dpw_ref[...] = jnp.zeros_like(dpw_ref)
            dps_ref[...] = jnp.zeros_like(dps_ref)

        dkacc[...] = jnp.zeros_like(dkacc)
        dcb8[...] = jnp.zeros_like(dcb8)
        dlg8[...] = jnp.zeros_like(dlg8)
        dlb8[...] = jnp.zeros_like(dlb8)
        dps8[...] = jnp.zeros_like(dps8)

        _fill_windows(i, zp_ref, zc_ref, u0w, pwin, tl, cc)

        def conv_side(u1, dyc, own):
            xc = u1 - _head_mean(u1, am_ref)
            rstd = lax.rsqrt(_head_mean(xc * xc, am_ref) + EPS)
            uh = xc * rstd
            lgv = lg_ref[...]
            u2 = uh * lgv + lb_ref[...]
            sg = _sigmoid(u2)
            du2 = dyc * (sg * (1.0 + u2 * (1.0 - sg)))
            if own:
                dlg8[...] += _rowsum8(du2 * uh)
                dlb8[...] += _rowsum8(du2)
            duh = du2 * lgv
            return rstd * (duh - _head_mean(duh, am_ref) - uh * _head_mean(duh * uh, am_ref))

        def pool_side(dyp, t0, rows):
            dds, es = [], []
            tg = t0 + lax.broadcasted_iota(jnp.int32, (rows, 1), 0)
            for g in range(ng):
                dypre = dyp[:, g * gd:(g + 1) * gd] * ps_ref[:, g * gd:(g + 1) * gd]
                dd = lax.dot_general(dypre.astype(BF16), pw_ref[g].astype(BF16), (((1,), (1,)), ((), ())),
                                     preferred_element_type=F32)
                cnt = jnp.minimum(tg + 1, POOL_WINDOWS[g]).astype(F32)
                dds.append(dd)
                es.append(dd / cnt)
            return jnp.concatenate(dds, axis=-1), jnp.concatenate(es, axis=-1)

        def blocks(k, carry):
            b = pl.multiple_of(k * rb, SUBLANES)
            dyb = dyc_ref[pl.ds(b, rb), :]
            du1 = conv_side(u1c_ref[pl.ds(b, rb), :], dyb[:, :cc], True)
            du1w[pl.ds(b, rb), :] = du1
            dcb8[...] += _rowsum8(du1)
            dyp = dyb[:, cc:]
            dd, e = pool_side(dyp, i * tl + b, rb)
            ddw[pl.ds(b, rb), :] = dd
            ew[pl.ds(b, rb), :] = e
            for g in range(ng):
                d = _pool_fwd_block(pwin, pw_ref, b, rb, g, gd, POOL_WINDOWS[g], i * tl + b)
                db16 = d.astype(BF16)
                dypg = dyp[:, g * gd:(g + 1) * gd]
                ypre = jnp.dot(db16, pw_ref[g].astype(BF16), preferred_element_type=F32)
                dps8[:, g * gd:(g + 1) * gd] += _rowsum8(dypg * ypre)
                dypre = (dypg * ps_ref[:, g * gd:(g + 1) * gd]).astype(BF16)
                dpw_ref[g] += lax.dot_general(db16, dypre, (((0,), (0,)), ((), ())), preferred_element_type=F32)
            return carry

        lax.fori_loop(0, tl // rb, blocks, 0)

        dyn = dyn_ref[...]
        du1n = conv_side(u1n_ref[...], dyn[:, :cc], False)
        du1w[tl:tl + HALO, :] = jnp.where(has_next, du1n, 0.0)
        ddn, en = pool_side(dyn[:, cc:], (i + 1) * tl, HALO)
        ew[tl:tl + HALO, :] = jnp.where(has_next, en, 0.0)

        def taps(c, carry):
            b = pl.multiple_of(c * ROW_CHUNK, SUBLANES)
            w = du1w[pl.ds(b, ROW_CHUNK + CONV_PAD), :]
            u0c = u0w[pl.ds(HALO + b, ROW_CHUNK), :]
            acc = jnp.zeros((ROW_CHUNK, cc), F32)
            for j in range(kw):
                o = kw - 1 - j
                sh = w[o:o + ROW_CHUNK]
                acc = acc + _rows_of(ck_ref[j], ROW_CHUNK) * sh
                dkacc[j] += _rowsum8(u0c * sh)
            zc = zc_ref[pl.ds(b, ROW_CHUNK), :]
            a = zc[:, :cc]
            sg = _sigmoid(zc[:, cc:2 * cc])
            dz_ref[pl.ds(b, ROW_CHUNK), 0:cc] = (acc * sg).astype(dz_ref.dtype)
            dz_ref[pl.ds(b, ROW_CHUNK), cc:2 * cc] = (acc * a * sg * (1.0 - sg)).astype(dz_ref.dtype)
            return carry

        lax.fori_loop(0, tl // ROW_CHUNK, taps, 0)

        def pool_back(k, carry):
            b = pl.multiple_of(k * rb, SUBLANES)
            n = rb + POOL_PAD
            for g in range(ng):
                s = ew[pl.ds(b, n), g * gd:(g + 1) * gd]
                sh = 1
                while sh < POOL_WINDOWS[g]:
                    s = s + pltpu.roll(s, n - sh, axis=0)
                    sh *= 2
                dp = s[0:rb] - ddw[pl.ds(b, rb), g * gd:(g + 1) * gd]
                dz_ref[pl.ds(b, rb), 2 * cc + g * gd:2 * cc + (g + 1) * gd] = dp.astype(dz_ref.dtype)
            return carry

        lax.fori_loop(0, tl // rb, pool_back, 0)

        dck_ref[...] += jnp.sum(dkacc[...], axis=1)
        dcb_ref[...] += jnp.sum(dcb8[...], axis=0, keepdims=True)
        dlg_ref[...] += jnp.sum(dlg8[...], axis=0, keepdims=True)
        dlb_ref[...] += jnp.sum(dlb8[...], axis=0, keepdims=True)
        dps_ref[...] += jnp.sum(dps8[...], axis=0, keepdims=True)

    def full(a):
        nd = a.ndim
        return pl.BlockSpec(a.shape, lambda i: (0,) * nd)

    nhb = L // HALO

    def prev_map(i):
        return (jnp.maximum(i * hb - 1, 0), 0)

    def next_map(i):
        return (jnp.minimum((i + 1) * hb, nhb - 1), 0)

    dcc = cc + cp
    row_cc = jax.ShapeDtypeStruct((1, cc), F32)
    out_shape = (jax.ShapeDtypeStruct((L, ci), BF16), jax.ShapeDtypeStruct((kw, cc), F32), row_cc, row_cc, row_cc,
                 jax.ShapeDtypeStruct((ng, gd, gd), F32), jax.ShapeDtypeStruct((1, cp), F32))
    acc_spec = [pl.BlockSpec((kw, cc), lambda i: (0, 0))] + [pl.BlockSpec((1, cc), lambda i: (0, 0))] * 3 + [
        pl.BlockSpec((ng, gd, gd), lambda i: (0, 0, 0)), pl.BlockSpec((1, cp), lambda i: (0, 0))]
    out, xo = _call(
        body, name=name, grid=(nt,),
        in_specs=[pl.BlockSpec((HALO, ci), prev_map), pl.BlockSpec((tl, ci), lambda i: (i, 0)),
                  pl.BlockSpec((tl, cc), lambda i: (i, 0)), pl.BlockSpec((HALO, cc), next_map),
                  pl.BlockSpec((tl, dcc), lambda i: (i, 0)), pl.BlockSpec((HALO, dcc), next_map),
                  full(ck), full(lg), full(lb), full(pw), full(ps), full(am)],
        out_specs=tuple([pl.BlockSpec((tl, ci), lambda i: (i, 0))] + acc_spec),
        out_shape=out_shape,
        scratch_shapes=[pltpu.VMEM((HALO + tl, cc), F32), pltpu.VMEM((HALO + tl, cp), F32),
                        pltpu.VMEM((tl + HALO, cc), F32), pltpu.VMEM((tl, cp), F32), pltpu.VMEM((tl + HALO, cp), F32),
                        pltpu.VMEM((kw, SUBLANES, cc), F32), pltpu.VMEM((SUBLANES, cc), F32),
                        pltpu.VMEM((SUBLANES, cc), F32), pltpu.VMEM((SUBLANES, cc), F32), pltpu.VMEM((SUBLANES, cp), F32)],
        sem=("arbitrary",), args=(z, z, u1, u1, dy, dy, ck, lg, lb, pw, ps, am), xchg=xchg)
    return out if xchg is None else (out, xo)


def _row_parts(nc, n=3):
    n = min(n, nc)
    cuts = [round(k * nc / n) for k in range(n + 1)]
    return [(cuts[k], cuts[k + 1]) for k in range(n)]


def _tap_rows(k_ref):
    return [jnp.broadcast_to(k_ref[j:j + 1, :], (SUBLANES, k_ref.shape[1])) for j in range(k_ref.shape[0])]


def _rows_of(tap, n):
    return tap if n == SUBLANES else jnp.concatenate([tap] * (n // SUBLANES), axis=0)


def _ffn_conv(win, taps, rows):
    kw = len(taps)
    o = FFN_PAD - (kw - 1)
    acc = _rows_of(taps[0], rows) * win[o:o + rows]
    for j in range(1, kw):
        acc = acc + _rows_of(taps[j], rows) * win[o + j:o + j + rows]
    return acc


def _ffn_block_fwd(h_mid, g, wup_t, kf, wdown, name, xchg=None):
    L, D = h_mid.shape
    f = wdown.shape[0]
    kw = kf.shape[0]
    tl = _token_tile(L)
    tc = _divisor(f, 256, 128)
    nj = f // tc
    nt = L // tl
    pad = 2 * SUBLANES
    hb = tl // pad
    rc = CONV3_ROWS
    parts = _row_parts(tl // rc)

    def body(hp_ref, hc_ref, g_ref, wg_ref, wv_ref, kg_ref, kv_ref, wd_ref, out_ref, hn_ref, act_ref, ux_ref, uc_ref,
             hn_halo, halo, ug_ref, acc):
        i = pl.program_id(0)
        kb = pl.program_id(1)

        @pl.when(kb == 0)
        def _():
            gg = g_ref[...]

            def norm(x):
                r = lax.rsqrt(jnp.mean(x * x, axis=-1, keepdims=True) + EPS)
                return ((x * r) * gg).astype(BF16)

            hn_halo[...] = jnp.where(i > 0, norm(hp_ref[...]), jnp.zeros((pad, D), BF16))
            hn_ref[...] = norm(hc_ref[...])
            acc[...] = jnp.zeros_like(acc)

        w_refs = (wg_ref, wv_ref)
        taps = (_tap_rows(kg_ref), _tap_rows(kv_ref))
        hh = hn_halo[...]
        for h in range(2):
            halo[h] = _dot_nt(hh, w_refs[h][...])[pad - FFN_PAD:]

        def up_part(lo, hi):
            a, b = lo * rc, hi * rc
            for h in range(2):
                ug_ref[h, a:b, :] = _dot_nt(hn_ref[a:b, :], w_refs[h][...])

        def down_part(lo, hi):
            a, b = lo * rc, hi * rc
            acc[a:b, :] += jnp.dot(act_ref[a:b, :], wd_ref[...], preferred_element_type=F32)

        def chunk_rows(lo, hi):
            for c in range(lo, hi):
                r0 = c * rc
                convd = []
                for h in range(2):
                    if c == 0:
                        win = jnp.concatenate([halo[h], ug_ref[h, 0:rc]], axis=0)
                    else:
                        win = ug_ref[h, r0 - FFN_PAD:r0 + rc]
                    convd.append(_ffn_conv(win, taps[h], rc))
                    ux_ref[h, r0:r0 + rc, :] = win[FFN_PAD:].astype(BF16)
                    uc_ref[h, r0:r0 + rc, :] = convd[h].astype(BF16)
                gate, val = convd
                act_ref[r0:r0 + rc, :] = ((gate * _sigmoid(gate)) * val).astype(BF16)

        for p, (lo, hi) in enumerate(parts):
            if p == 0:
                up_part(lo, hi)
            if p + 1 < len(parts):
                up_part(*parts[p + 1])
            if p > 0:
                down_part(*parts[p - 1])
            chunk_rows(lo, hi)
        down_part(*parts[-1])

        @pl.when(kb == nj - 1)
        def _():
            out_ref[...] = acc[...] + hc_ref[...]

    out, xo = _call(
        body, name=name, grid=(nt, nj),
        in_specs=[pl.BlockSpec((pad, D), lambda i, k: (jnp.maximum(i * hb - 1, 0), 0)),
                  pl.BlockSpec((tl, D), lambda i, k: (i, 0)),
                  pl.BlockSpec((1, D), lambda i, k: (0, 0)),
                  pl.BlockSpec((tc, D), lambda i, k: (k, 0)), pl.BlockSpec((tc, D), lambda i, k: (k + nj, 0)),
                  pl.BlockSpec((kw, tc), lambda i, k: (0, k)), pl.BlockSpec((kw, tc), lambda i, k: (0, k + nj)),
                  pl.BlockSpec((tc, D), lambda i, k: (k, 0))],
        out_specs=(pl.BlockSpec((tl, D), lambda i, k: (i, 0)), pl.BlockSpec((tl, D), lambda i, k: (i, 0)),
                   pl.BlockSpec((tl, tc), lambda i, k: (i, k)),
                   pl.BlockSpec((2, tl, tc), lambda i, k: (0, i, k)), pl.BlockSpec((2, tl, tc), lambda i, k: (0, i, k))),
        out_shape=(jax.ShapeDtypeStruct((L, D), F32), jax.ShapeDtypeStruct((L, D), BF16),
                   jax.ShapeDtypeStruct((L, f), BF16),
                   jax.ShapeDtypeStruct((2, L, f), BF16), jax.ShapeDtypeStruct((2, L, f), BF16)),
        scratch_shapes=[pltpu.VMEM((pad, D), BF16), pltpu.VMEM((2, FFN_PAD, tc), F32), pltpu.VMEM((2, tl, tc), F32),
                        pltpu.VMEM((tl, D), F32)],
        sem=("parallel", "arbitrary"), args=(h_mid, h_mid, g, wup_t, wup_t, kf, kf, wdown), xchg=xchg)
    return out if xchg is None else (out, xo)


def _ffn_block_bwd(dh, h_mid, g, ux, uc, kf, wdown, wup_t, name, xchg=None):
    L, D = dh.shape
    f = ux.shape[2]
    kw = kf.shape[0]
    tl = _token_tile(L)
    tc = _divisor(f, 256, 128)
    nj = f // tc
    nt = L // tl
    pad = 2 * SUBLANES
    rc = CONV3_ROWS
    nc = tl // rc
    parts = _row_parts(nc)

    def body(dhc_ref, dhn_ref, hm_ref, g_ref, xg_ref, xv_ref, cg_ref, cgn_ref, cv_ref, cvn_ref, kg_ref, kv_ref,
             wd_ref, wg_ref, wv_ref, dhm_ref, dg_ref, du_ref, dk_ref, dh_ext, dact_s, acc):
        i = pl.program_id(0)
        kb = pl.program_id(1)

        @pl.when(kb == 0)
        def _():
            dh_ext[0:tl, :] = dhc_ref[...].astype(BF16)
            dh_ext[tl:tl + pad, :] = dhn_ref[...].astype(BF16)
            acc[...] = jnp.zeros_like(acc)

        @pl.when(jnp.logical_and(i == 0, kb == 0))
        def _():
            dg_ref[...] = jnp.zeros_like(dg_ref)
            dk_ref[...] = jnp.zeros_like(dk_ref)

        x_refs, c_refs, nxt = (xg_ref, xv_ref), (cg_ref, cv_ref), (cgn_ref, cvn_ref)
        taps = (_tap_rows(kg_ref), _tap_rows(kv_ref))
        dk = [[jnp.zeros((SUBLANES, tc), F32) for _ in range(kw)] for _ in range(2)]

        def dact_part(lo, hi):
            a, b = lo * rc, hi * rc + pad
            dact_s[a:b, :] = _dot_nt(dh_ext[a:b, :], wd_ref[...])

        def dhn_part(lo, hi):
            a, b = lo * rc, hi * rc
            acc[a:b, :] += (jnp.dot(du_ref[0, a:b, :], wg_ref[...], preferred_element_type=F32)
                            + jnp.dot(du_ref[1, a:b, :], wv_ref[...], preferred_element_type=F32))

        for p, (lo, hi) in enumerate(parts):
            if p == 0:
                dact_part(lo, hi)
            if p + 1 < len(parts):
                dact_part(*parts[p + 1])
            if p > 0:
                dhn_part(*parts[p - 1])
            chunk_rows(lo, hi, x_refs, c_refs, nxt, taps, dk, i, dact_s, du_ref)
        dhn_part(*parts[-1])
        for h in range(2):
            for j in range(kw):
                dk_ref[kb, h, j:j + 1, :] += jnp.sum(dk[h][j], axis=0, keepdims=True)

        @pl.when(kb == nj - 1)
        def _():
            x = hm_ref[...]
            r = lax.rsqrt(jnp.mean(x * x, axis=-1, keepdims=True) + EPS)
            xhat = x * r
            dhn = acc[...]
            dxhat = dhn * g_ref[...]
            dhm_ref[...] = dhc_ref[...] + r * (dxhat - xhat * jnp.mean(dxhat * xhat, axis=-1, keepdims=True))
            dg_ref[...] += jnp.sum(_rowsum8(dhn * xhat), axis=0, keepdims=True)

    def chunk_rows(lo, hi, x_refs, c_refs, nxt, taps, dk, i, dact_s, du_ref):
        for c in range(lo, hi):
            r0 = c * rc
            n = rc + FFN_PAD
            convd = []
            for h in range(2):
                if c == nc - 1:
                    rows = jnp.concatenate([c_refs[h][r0:r0 + rc, :], nxt[h][...]], axis=0)
                else:
                    rows = c_refs[h][r0:r0 + rc + pad, :]
                convd.append(rows.astype(F32)[0:n])
            gate, val = convd
            xs = [x_refs[h][r0:r0 + rc, :].astype(F32) for h in range(2)]
            dact = dact_s[r0:r0 + n, :]
            sg = _sigmoid(gate)
            dcs = [dact * val * (sg * (1.0 + gate * (1.0 - sg))), dact * (gate * sg)]
            if c == nc - 1:
                live = jnp.logical_or(lax.broadcasted_iota(jnp.int32, (n, 1), 0) < rc, i < nt - 1)
                dcs = [jnp.where(live, d, 0.0) for d in dcs]
            for h in range(2):
                xc = xs[h]
                dx = None
                for j in range(kw):
                    o = kw - 1 - j
                    sh = dcs[h][o:o + rc]
                    term = _rows_of(taps[h][j], rc) * sh
                    dx = term if dx is None else dx + term
                    dk[h][j] = dk[h][j] + _rowsum8(xc * sh)
                du_ref[h, r0:r0 + rc, :] = dx.astype(BF16)

    def after(i):
        return jnp.minimum((i + 1) * (tl // pad), L // pad - 1)

    def half(h, rows, idx):
        return pl.BlockSpec((None, rows, tc), lambda i, k: (h,) + idx(i, k))

    def tile(i, k):
        return (i, k)

    def behind(i, k):
        return (after(i), k)

    out, xo = _call(
        body, name=name, grid=(nt, nj),
        in_specs=[pl.BlockSpec((tl, D), lambda i, k: (i, 0)),
                  pl.BlockSpec((pad, D), lambda i, k: (after(i), 0)),
                  pl.BlockSpec((tl, D), lambda i, k: (i, 0)), pl.BlockSpec((1, D), lambda i, k: (0, 0)),
                  half(0, tl, tile), half(1, tl, tile),
                  half(0, tl, tile), half(0, pad, behind), half(1, tl, tile), half(1, pad, behind),
                  pl.BlockSpec((kw, tc), lambda i, k: (0, k)), pl.BlockSpec((kw, tc), lambda i, k: (0, k + nj)),
                  pl.BlockSpec((tc, D), lambda i, k: (k, 0)),
                  pl.BlockSpec((tc, D), lambda i, k: (k, 0)), pl.BlockSpec((tc, D), lambda i, k: (k + nj, 0))],
        out_specs=(pl.BlockSpec((tl, D), lambda i, k: (i, 0)), pl.BlockSpec((1, D), lambda i, k: (0, 0)),
                   pl.BlockSpec((2, tl, tc), lambda i, k: (0, i, k)),
                   pl.BlockSpec((nj, 2, kw, tc), lambda i, k: (0, 0, 0, 0))),
        out_shape=(jax.ShapeDtypeStruct((L, D), F32), jax.ShapeDtypeStruct((1, D), F32),
                   jax.ShapeDtypeStruct((2, L, f), BF16), jax.ShapeDtypeStruct((nj, 2, kw, tc), F32)),
        scratch_shapes=[pltpu.VMEM((tl + pad, D), BF16), pltpu.VMEM((tl + pad, tc), F32), pltpu.VMEM((tl, D), F32)],
        sem=("arbitrary", "arbitrary"), args=(dh, dh, h_mid, g, ux, ux, uc, uc, uc, uc, kf, kf, wdown, wup_t, wup_t),
        xchg=xchg)
    return out if xchg is None else (out, xo)


def _adamw_math(w, g, m, v):
    m = ADAM_B1 * m + (1.0 - ADAM_B1) * g
    v = ADAM_B2 * v + (1.0 - ADAM_B2) * (g * g)
    m_hat = m / (1.0 - ADAM_B1 ** ADAM_STEP)
    v_hat = v / (1.0 - ADAM_B2 ** ADAM_STEP)
    delta = -ADAM_LR * (m_hat / (jnp.sqrt(v_hat) + ADAM_EPS) + ADAM_WD * w)
    return delta, m, v


def _sum_parts(parts_ref, idx):
    g = parts_ref[(0,) + idx].astype(F32)
    for q in range(1, N_DEV):
        g = g + parts_ref[(q,) + idx].astype(F32)
    return g


def _adamw_big(parts, w, m, v, name):
    nl, R, C = w.shape
    tr = _divisor(R, 256, 2 * SUBLANES)

    def body(*refs):
        p_refs = refs[:nl]
        w_ref, m_ref, v_ref, g_ref, d_ref, nm_ref, nv_ref = refs[nl:]
        layer = pl.program_id(0)
        for k in range(nl):
            @pl.when(layer == k)
            def _(k=k):
                g = _sum_parts(p_refs[k], ())
                d, nm, nv = _adamw_math(w_ref[0], g, m_ref[0], v_ref[0])
                g_ref[0] = g
                d_ref[0] = d
                nm_ref[0] = nm
                nv_ref[0] = nv

    def part_spec(k):
        return pl.BlockSpec((N_DEV, tr, C), lambda l, r: (0, jnp.where(l == k, r, 0), 0))

    blk = pl.BlockSpec((1, tr, C), lambda l, r: (l, r, 0))
    shp = jax.ShapeDtypeStruct((nl, R, C), F32)
    return pl.pallas_call(
        body, name=name, grid=(nl, R // tr),
        in_specs=[part_spec(k) for k in range(nl)] + [blk, blk, blk],
        out_specs=(blk, blk, blk, blk), out_shape=(shp, shp, shp, shp),
        compiler_params=_params(("arbitrary", "arbitrary")),
    )(*parts, w, m, v)


def _adamw_small(entries, name):
    n = len(entries)
    uniq = []
    for e in entries:
        if not any(e[0] is u for u in uniq):
            uniq.append(e[0])
    pidx = [next(k for k, u in enumerate(uniq) if u is e[0]) for e in entries]
    npart = len(uniq)

    def body(*refs):
        p_refs = refs[:npart]
        wmv = refs[npart:npart + 3 * n]
        outs = refs[npart + 3 * n:]
        for t, e in enumerate(entries):
            lo, w = e[1], e[2]
            rows = w.shape[0]
            pr = p_refs[pidx[t]]
            g = pr[0, lo:lo + rows].astype(F32)
            for q in range(1, N_DEV):
                g = g + pr[q, lo:lo + rows].astype(F32)
            d, nm, nv = _adamw_math(wmv[3 * t][...], g, wmv[3 * t + 1][...], wmv[3 * t + 2][...])
            outs[4 * t][...] = g
            outs[4 * t + 1][...] = d
            outs[4 * t + 2][...] = nm
            outs[4 * t + 3][...] = nv

    vm = pl.BlockSpec(memory_space=pltpu.VMEM)
    args = list(uniq)
    out_shape = []
    for e in entries:
        args += [e[2], e[3], e[4]]
        out_shape += [jax.ShapeDtypeStruct(e[2].shape, F32)] * 4
    res = pl.pallas_call(
        body, name=name, in_specs=[vm] * len(args), out_specs=tuple([vm] * len(out_shape)),
        out_shape=tuple(out_shape), compiler_params=_params(),
    )(*args)
    return [tuple(res[4 * t:4 * t + 4]) for t in range(n)]


def _head_matrix(cc):
    bw = min(256, cc)
    r = lax.broadcasted_iota(jnp.int32, (bw, bw), 0) // HEAD_DIM
    c = lax.broadcasted_iota(jnp.int32, (bw, bw), 1) // HEAD_DIM
    return jnp.where(r == c, 1.0 / HEAD_DIM, 0.0).astype(BF16)


def _cols_from_shards(g):
    nd = g.ndim
    perm = tuple(range(1, nd - 1)) + (0, nd - 1)
    t = jnp.transpose(g, perm)
    return t.reshape(t.shape[:-2] + (t.shape[-2] * t.shape[-1],))


def _cols_to_shards(a):
    nd = a.ndim
    t = a.reshape(a.shape[:-1] + (N_DEV, a.shape[-1] // N_DEV))
    perm = (nd - 1,) + tuple(range(nd - 1)) + (nd,)
    return jnp.transpose(t, perm)


def kernel(x, meta_tokens, norm1_g, w_in, conv_dw_k, conv_dw_b, conv_ln_g, conv_ln_b, pool_w, pool_scale, w_out, norm2_g, w_up, ffn_dw_k, w_down, final_g, loss_target, m_meta_tokens, m_norm1_g, m_w_in, m_conv_dw_k, m_conv_dw_b, m_conv_ln_g, m_conv_ln_b, m_pool_w, m_pool_scale, m_w_out, m_norm2_g, m_w_up, m_ffn_dw_k, m_w_down, m_final_g, v_meta_tokens, v_norm1_g, v_w_in, v_conv_dw_k, v_conv_dw_b, v_conv_ln_g, v_conv_ln_b, v_pool_w, v_pool_scale, v_w_out, v_norm2_g, v_w_up, v_ffn_dw_k, v_w_down, v_final_g):
    depth, D = norm1_g.shape
    n_meta = meta_tokens.shape[0]
    seq = x.shape[1]
    L = n_meta + seq
    cc = conv_dw_b.shape[1]
    ng, gd = pool_w.shape[1], pool_w.shape[2]
    f = w_down.shape[1] * N_DEV

    def rows(g):
        return g.reshape(-1, g.shape[-1])

    b16 = lambda a: a.astype(BF16)
    tr = lambda a: jnp.swapaxes(a, -1, -2)
    w_in_t, m_w_in_t, v_w_in_t = tr(w_in), tr(m_w_in), tr(v_w_in)
    w_up_t, m_w_up_t, v_w_up_t = tr(w_up), tr(m_w_up), tr(v_w_up)
    (g_in0, g_ck, g_kf, g_meta) = _exchange([b16(w_in_t[0]), conv_dw_k, ffn_dw_k, meta_tokens], ["gather"] * 4,
                                            "gather_first")
    ck_full = _cols_from_shards(g_ck)
    ck_rows = jnp.broadcast_to(ck_full[:, :, None, :], ck_full.shape[:2] + (SUBLANES, cc))
    kf_full = _cols_from_shards(g_kf)
    meta_full = _cols_from_shards(g_meta)
    am = _head_matrix(cc)
    win, wout, wup, wdown = [None] * depth, [None] * depth, [None] * depth, [None] * depth
    win[0] = rows(g_in0)

    h = jnp.concatenate([meta_full, x[0]], axis=0)
    saved = []
    for l in range(depth):
        more = l + 1 < depth
        if l == 0:
            (z, hn1), (g_out, g_down) = _norm_proj(h, norm1_g[l:l + 1], win[l], f"in_proj_{l}", tn_cap=1536,
                                                   xchg=([b16(w_out[l]), b16(w_down[l])], ["gather"] * 2))
            wout[l], wdown[l] = rows(g_out), rows(g_down)
            (ymix, u1), (g_up,) = _mixer_fwd(z, ck_rows[l], conv_dw_b[l:l + 1], conv_ln_g[l:l + 1], conv_ln_b[l:l + 1],
                                             pool_w[l], pool_scale[l:l + 1], am, f"mixer_fwd_{l}",
                                             xchg=([b16(w_up_t[l])], ["gather"]))
            wup[l] = rows(g_up)
        else:
            z, hn1 = _norm_proj(h, norm1_g[l:l + 1], win[l], f"in_proj_{l}", tn_cap=1536)
            ymix, u1 = _mixer_fwd(z, ck_rows[l], conv_dw_b[l:l + 1], conv_ln_g[l:l + 1], conv_ln_b[l:l + 1], pool_w[l],
                                  pool_scale[l:l + 1], am, f"mixer_fwd_{l}")
        if more:
            h_mid, (g_in,) = _mm(ymix, wout[l], f"out_proj_{l}", res=h, tn_cap=1024, xchg=([b16(w_in_t[l + 1])], ["gather"]))
            win[l + 1] = rows(g_in)
            nxt = [b16(w_out[l + 1]), b16(w_up_t[l + 1]), b16(w_down[l + 1])]
            (h_out, hn2, act, ux, uc), got = _ffn_block_fwd(h_mid, norm2_g[l:l + 1], wup[l], kf_full[l], wdown[l],
                                                            f"ffn_fwd_{l}", xchg=(nxt, ["gather"] * 3))
            wout[l + 1], wup[l + 1], wdown[l + 1] = rows(got[0]), rows(got[1]), rows(got[2])
        else:
            h_mid = _mm(ymix, wout[l], f"out_proj_{l}", res=h, tn_cap=1024)
            h_out, hn2, act, ux, uc = _ffn_block_fwd(h_mid, norm2_g[l:l + 1], wup[l], kf_full[l], wdown[l], f"ffn_fwd_{l}")
        saved.append((h, hn1, z, u1, ymix, h_mid, hn2, ux, uc, act))
        h = h_out

    dh, d_final_g, loss_part = _loss_head(h, final_g.reshape(1, D), loss_target[0], n_meta, "loss_head")

    def row_shards(gm):
        return gm.reshape(N_DEV, -1, gm.shape[-1])

    zero_row = jnp.zeros((1, D), F32)
    gw = {k: [None] * depth for k in ("ck", "cb", "lg", "lb", "pw", "ps", "kf", "n1", "n2")}
    parts = {k: [None] * depth for k in ("in", "out", "up", "down")}
    for l in reversed(range(depth)):
        h_in, hn1, z, u1, ymix, h_mid, hn2, ux, uc, act = saved[l]
        g_down = _mm_tn(act, dh, f"down_proj_wgrad_{l}", tq_cap=512)
        (dh_mid, gw["n2"][l], dug0, dkf), (parts["down"][l],) = _ffn_block_bwd(
            dh, h_mid, norm2_g[l:l + 1], ux, uc, kf_full[l], wdown[l], wup[l], f"ffn_bwd_{l}",
            xchg=([row_shards(g_down)], ["a2a"]))
        gw["kf"][l] = jnp.transpose(dkf, (2, 1, 0, 3)).reshape(dkf.shape[2], -1)
        g_up_t = _mm_tn(dug0, hn2, f"up_proj_wgrad_{l}", halves=2, tq_cap=1024)
        dymix = _mm(dh_mid, wout[l], f"out_proj_bwd_{l}", b_t=True, tn_cap=1024)
        g_out = _mm_tn(ymix, dh_mid, f"out_proj_wgrad_{l}", tq_cap=512)
        ((dz, gw["ck"][l], gw["cb"][l], gw["lg"][l], gw["lb"][l], gw["pw"][l], gw["ps"][l]),
         (parts["up"][l], parts["out"][l])) = _mixer_bwd(
            z, u1, dymix, ck_rows[l], conv_ln_g[l:l + 1], conv_ln_b[l:l + 1], pool_w[l], pool_scale[l:l + 1], am,
            f"mixer_bwd_{l}", xchg=([row_shards(g_up_t), row_shards(g_out)], ["a2a", "a2a"]))
        g_in_t = _mm_tn(dz, hn1, f"in_proj_wgrad_{l}", tq_cap=1024)
        if l > 0:
            (dh, gw["n1"][l]), (parts["in"][l],) = _proj_bwd_norm(dz, win[l], h_in, norm1_g[l:l + 1], dh_mid, zero_row,
                                                                  f"in_proj_bwd_{l}", xchg=([row_shards(g_in_t)], ["a2a"]))
        else:
            (grad_x, dg_x), (parts["in"][l],) = _proj_bwd_norm(dz, win[l], h_in, norm1_g[l:l + 1], dh_mid, zero_row,
                                                               f"in_proj_bwd_{l}", skip=n_meta,
                                                               xchg=([row_shards(g_in_t)], ["a2a"]))
            d_meta, gw["n1"][l] = _proj_bwd_norm(dz[:n_meta], win[l], h_in[:n_meta], norm1_g[l:l + 1], dh_mid[:n_meta],
                                                 dg_x, f"in_proj_bwd_meta_{l}")
    grad_x = grad_x[None]

    pack_d = jnp.concatenate(gw["n1"] + gw["n2"] + [d_final_g, jnp.broadcast_to(loss_part[:, :1], (1, D)), zero_row, zero_row], axis=0)
    pack_c = jnp.concatenate(gw["cb"] + gw["lg"] + gw["lb"] + gw["ps"], axis=0)
    pack_pw = jnp.stack(gw["pw"]).reshape(depth * ng * gd, gd)
    src = [_cols_to_shards(jnp.stack(gw["ck"])), _cols_to_shards(jnp.stack(gw["kf"])), _cols_to_shards(d_meta),
           pack_d, pack_c, pack_pw]
    r_ck, r_kf, r_meta, r_d, r_c, r_pw = _exchange(src, ["a2a"] * 3 + ["gather"] * 3, "exchange_small_grads")

    big = {
        "w_in": tuple(tr(a) for a in _adamw_big(parts["in"], w_in_t, m_w_in_t, v_w_in_t, "adamw_w_in")),
        "w_out": _adamw_big(parts["out"], w_out, m_w_out, v_w_out, "adamw_w_out"),
        "w_up": tuple(tr(a) for a in _adamw_big(parts["up"], w_up_t, m_w_up_t, v_w_up_t, "adamw_w_up")),
        "w_down": _adamw_big(parts["down"], w_down, m_w_down, v_w_down, "adamw_w_down"),
    }
    kwid = conv_dw_k.shape[1]
    fkw = ffn_dw_k.shape[1]
    row = lambda a: a.reshape(1, -1)
    entries = [
        (r_d, 0, norm1_g, m_norm1_g, v_norm1_g),
        (r_d, depth, norm2_g, m_norm2_g, v_norm2_g),
        (r_d, 2 * depth, row(final_g), row(m_final_g), row(v_final_g)),
        (r_c, 0, conv_dw_b, m_conv_dw_b, v_conv_dw_b),
        (r_c, depth, conv_ln_g, m_conv_ln_g, v_conv_ln_g),
        (r_c, 2 * depth, conv_ln_b, m_conv_ln_b, v_conv_ln_b),
        (r_c, 3 * depth, pool_scale, m_pool_scale, v_pool_scale),
        (r_pw, 0, pool_w.reshape(-1, gd), m_pool_w.reshape(-1, gd), v_pool_w.reshape(-1, gd)),
        (r_ck.reshape(N_DEV, depth * kwid, -1), 0, conv_dw_k.reshape(depth * kwid, -1),
         m_conv_dw_k.reshape(depth * kwid, -1), v_conv_dw_k.reshape(depth * kwid, -1)),
        (r_kf.reshape(N_DEV, depth * fkw, -1), 0, ffn_dw_k.reshape(depth * fkw, -1),
         m_ffn_dw_k.reshape(depth * fkw, -1), v_ffn_dw_k.reshape(depth * fkw, -1)),
        (r_meta, 0, meta_tokens, m_meta_tokens, v_meta_tokens),
        (r_d, 2 * depth + 1, zero_row, zero_row, zero_row),
    ]
    small = _adamw_small(entries, "adamw_small")
    names = ["norm1_g", "norm2_g", "final_g", "conv_dw_b", "conv_ln_g", "conv_ln_b", "pool_scale", "pool_w",
             "conv_dw_k", "ffn_dw_k", "meta_tokens"]
    shapes = {"final_g": final_g.shape, "pool_w": pool_w.shape, "conv_dw_k": conv_dw_k.shape, "ffn_dw_k": ffn_dw_k.shape}
    res = dict(big)
    for nme, quad in zip(names, small[:-1]):
        res[nme] = tuple(a.reshape(shapes[nme]) if nme in shapes else a for a in quad)
    loss = small[-1][0][0, 0]

    order = ["meta_tokens", "norm1_g", "w_in", "conv_dw_k", "conv_dw_b", "conv_ln_g", "conv_ln_b", "pool_w", "pool_scale",
             "w_out", "norm2_g", "w_up", "ffn_dw_k", "w_down", "final_g"]
    return (loss, grad_x, *[res[k][0] for k in order], *[res[k][1] for k in order], *[res[k][2] for k in order],
            *[res[k][3] for k in order])
```

```python
import functools

import jax
import jax.numpy as jnp
from jax import lax
from jax.experimental import pallas as pl
from jax.experimental.pallas import tpu as pltpu

F32 = jnp.float32
BF16 = jnp.bfloat16

EPS = 1e-6
HEAD_DIM = 64
POOL_WINDOWS = (2, 4, 8, 16)
ADAM_LR = 0.001
ADAM_B1 = 0.9
ADAM_B2 = 0.999
ADAM_EPS = 1e-08
ADAM_WD = 0.01
ADAM_STEP = 10

N_DEV = 8
OTHER_CHIPS = (2, 4, 6)
SUBLANES = 8
HALO = 48
CONV_PAD = 32
POOL_PAD = 16
FFN_PAD = 8
ROW_CHUNK = 24
CONV3_ROWS = 48
MAX_TILE_ROWS = 1024
WGRAD_TILE_ROWS = 2816
VMEM_LIMIT = 52 * 1024 * 1024


def _divisor(n, cap, mult):
    best = None
    for d in range(mult, min(n, cap) + 1, mult):
        if n % d == 0:
            best = d
    return n if best is None else best


def _token_tile(L):
    return _divisor(L, MAX_TILE_ROWS, HALO)


def _stat_rows(tl):
    return _divisor(tl, 512, SUBLANES)


def _params(sem=None):
    return pltpu.CompilerParams(dimension_semantics=sem, vmem_limit_bytes=VMEM_LIMIT)


def _rowsum8(x):
    acc = x[0:SUBLANES]
    for k in range(1, x.shape[0] // SUBLANES):
        acc = acc + x[k * SUBLANES:(k + 1) * SUBLANES]
    return acc


def _sigmoid(x):
    return jax.nn.sigmoid(x)


def _dot_nt(a, b):
    return lax.dot_general(a, b, (((1,), (1,)), ((), ())), preferred_element_type=F32)


def _head_mean(x, am_ref):
    bw = am_ref.shape[0]
    am = am_ref[...]
    outs = []
    for blk in range(x.shape[1] // bw):
        xb = x[:, blk * bw:(blk + 1) * bw]
        hi = xb.astype(BF16)
        lo = (xb - hi.astype(F32)).astype(BF16)
        outs.append(jnp.dot(hi, am, preferred_element_type=F32) + jnp.dot(lo, am, preferred_element_type=F32))
    return outs[0] if len(outs) == 1 else jnp.concatenate(outs, axis=-1)


def _xchg_out_shapes(srcs, modes):
    out = []
    for s, m in zip(srcs, modes):
        shp = ((N_DEV,) + tuple(s.shape)) if m == "gather" else tuple(s.shape)
        out.append(jax.ShapeDtypeStruct(shp, s.dtype))
    return out


def _xchg_sems(n):
    return [pltpu.SemaphoreType.DMA((n, N_DEV - 1)), pltpu.SemaphoreType.DMA((n, N_DEV - 1)), pltpu.SemaphoreType.DMA((n,))]


def _xchg_ops(src_refs, out_refs, sems, modes):
    n = len(src_refs)
    send_sems, recv_sems, local_sems = sems
    x, y, c = lax.axis_index("x"), lax.axis_index("y"), lax.axis_index("c")
    me = 4 * x + 2 * y + c

    def peer(d):
        return (x ^ ((d >> 2) & 1), y ^ ((d >> 1) & 1), c ^ (d & 1))

    def peer_id(d):
        px, py, pc = peer(d)
        return 4 * px + 2 * py + pc

    def remote(t, d):
        src = src_refs[t] if modes[t] == "gather" else src_refs[t].at[peer_id(d)]
        return pltpu.make_async_remote_copy(
            src_ref=src, dst_ref=out_refs[t].at[me], send_sem=send_sems.at[t, d - 1], recv_sem=recv_sems.at[t, d - 1],
            device_id=peer(d), device_id_type=pl.DeviceIdType.MESH)

    def arrival(t, d):
        src = src_refs[t] if modes[t] == "gather" else src_refs[t].at[me]
        return pltpu.make_async_remote_copy(
            src_ref=src, dst_ref=out_refs[t].at[peer_id(d)], send_sem=send_sems.at[t, d - 1],
            recv_sem=recv_sems.at[t, d - 1], device_id=peer(d), device_id_type=pl.DeviceIdType.MESH)

    def passed_on(t, d):
        blk = out_refs[t].at[peer_id(d)]
        return pltpu.make_async_remote_copy(
            src_ref=blk, dst_ref=blk, send_sem=send_sems.at[t, d], recv_sem=recv_sems.at[t, d],
            device_id=peer(1), device_id_type=pl.DeviceIdType.MESH)

    def local(t):
        src = src_refs[t] if modes[t] == "gather" else src_refs[t].at[me]
        return pltpu.make_async_copy(src, out_refs[t].at[me], local_sems.at[t])

    def sent_first(t):
        return OTHER_CHIPS + (1,) if modes[t] == "gather" else tuple(range(1, N_DEV))

    def start():
        for t in range(n):
            local(t).start()
        for t in range(n):
            for d in sent_first(t):
                remote(t, d).start()

    gathered = [t for t in range(n) if modes[t] == "gather"]

    def relay():
        for t in gathered:
            for d in OTHER_CHIPS:
                arrival(t, d).wait_recv()
                passed_on(t, d).start()

    def wait():
        for t in range(n):
            for d in range(1, N_DEV):
                if not (modes[t] == "gather" and d in OTHER_CHIPS):
                    arrival(t, d).wait_recv()
        for t in range(n):
            for d in sent_first(t):
                remote(t, d).wait_send()
        for t in gathered:
            for d in OTHER_CHIPS:
                passed_on(t, d).wait_send()
        for t in range(n):
            local(t).wait()

    return start, relay, wait


def _exchange(srcs, modes, name):
    n = len(srcs)

    def body(*refs):
        start, relay, wait = _xchg_ops(refs[:n], refs[n:2 * n], refs[2 * n:], modes)
        start()
        relay()
        wait()

    any_spec = pl.BlockSpec(memory_space=pl.ANY)
    return pl.pallas_call(
        body, name=name, out_shape=tuple(_xchg_out_shapes(srcs, modes)),
        in_specs=[any_spec] * n, out_specs=tuple([any_spec] * n),
        scratch_shapes=_xchg_sems(n),
        compiler_params=pltpu.CompilerParams(has_side_effects=True),
    )(*srcs)


def _call(body, *, name, grid, in_specs, out_specs, out_shape, args, scratch_shapes=(), sem=None, xchg=None):
    single = not isinstance(out_shape, (tuple, list))
    outs_shape = [out_shape] if single else list(out_shape)
    outs_spec = [out_specs] if single else list(out_specs)
    if xchg is None:
        res = pl.pallas_call(
            body, name=name, grid=grid, in_specs=list(in_specs), out_specs=out_specs, out_shape=out_shape,
            scratch_shapes=list(scratch_shapes), compiler_params=_params(sem))(*args)
        return res, ()
    srcs, modes = xchg
    n_in, n_out, n_scr, nx = len(in_specs), len(outs_shape), len(scratch_shapes), len(srcs)
    n_steps = functools.reduce(lambda a, b: a * b, grid, 1)
    relay_step = (3 * n_steps) // 4 if n_steps > 1 else 0

    def wrapped(*refs):
        ins = refs[:n_in]
        xs = refs[n_in:n_in + nx]
        o0 = n_in + nx
        outs = refs[o0:o0 + n_out]
        xo = refs[o0 + n_out:o0 + n_out + nx]
        s0 = o0 + n_out + nx
        scr = refs[s0:s0 + n_scr]
        start, relay, wait = _xchg_ops(xs, xo, refs[s0 + n_scr:], modes)
        step = functools.reduce(lambda acc, a: acc * grid[a] + pl.program_id(a), range(len(grid)), 0)

        @pl.when(step == 0)
        def _():
            start()

        body(*ins, *outs, *scr)

        @pl.when(step == relay_step)
        def _():
            relay()

        @pl.when(step == n_steps - 1)
        def _():
            wait()

    any_spec = pl.BlockSpec(memory_space=pl.ANY)
    res = pl.pallas_call(
        wrapped, name=name, grid=grid, in_specs=list(in_specs) + [any_spec] * nx,
        out_specs=tuple(outs_spec + [any_spec] * nx), out_shape=tuple(outs_shape + _xchg_out_shapes(srcs, modes)),
        scratch_shapes=list(scratch_shapes) + _xchg_sems(nx),
        compiler_params=_params(("arbitrary",) * len(grid)))(*args, *srcs)
    comp = res[:n_out]
    return (comp[0] if single else tuple(comp)), tuple(res[n_out:])


def _seq_rows(h):
    if isinstance(h, tuple):
        return h[0].shape[0] + h[1].shape[0], h[1].shape[1]
    return h.shape


def _seq_tiles(h, tm):
    if not isinstance(h, tuple):
        return [pl.BlockSpec((tm, h.shape[1]), lambda i, *_: (i, 0))], [h], lambda refs, i: refs[0][...]
    meta, x = h
    n, D = meta.shape

    def read(refs, i):
        t = refs[1][...]
        first = jnp.concatenate([refs[0][...], pltpu.roll(t, n, axis=0)[n:]], axis=0)
        return jnp.where(i == 0, first, t)

    window = pl.BlockSpec((pl.Element(tm), pl.Element(D)),
                          lambda i, *_: (pl.multiple_of(jnp.maximum(i * tm - n, 0), SUBLANES), 0))
    return [pl.BlockSpec((n, D), lambda *_: (0, 0)), window], [meta, x], read


def _norm_proj(h, g, w, name, *, tn_cap, xchg=None):
    L, D = _seq_rows(h)
    N = w.shape[0]
    tm = _token_tile(L)
    tn = _divisor(N, tn_cap, 128)
    h_specs, h_args, read_h = _seq_tiles(h, tm)
    nh = len(h_specs)

    def body(*refs):
        g_ref, w_ref, z_ref, hn_ref = refs[nh:]

        @pl.when(pl.program_id(1) == 0)
        def _():
            x = read_h(refs[:nh], pl.program_id(0))
            r = lax.rsqrt(jnp.mean(x * x, axis=-1, keepdims=True) + EPS)
            hn_ref[...] = ((x * r) * g_ref[...]).astype(BF16)

        z_ref[...] = _dot_nt(hn_ref[...], w_ref[...])

    out, xo = _call(
        body, name=name, grid=(L // tm, N // tn),
        in_specs=h_specs + [pl.BlockSpec((1, D), lambda i, j: (0, 0)), pl.BlockSpec((tn, D), lambda i, j: (j, 0))],
        out_specs=(pl.BlockSpec((tm, tn), lambda i, j: (i, j)), pl.BlockSpec((tm, D), lambda i, j: (i, 0))),
        out_shape=(jax.ShapeDtypeStruct((L, N), F32), jax.ShapeDtypeStruct((L, D), BF16)),
        sem=("parallel", "arbitrary"), args=(*h_args, g, w), xchg=xchg)
    return out if xchg is None else (out, xo)


def _proj_bwd_norm(a, b, h, g, dres, dg0, name, skip=0, xchg=None):
    L, K = a.shape
    D = b.shape[1]
    rows = L - skip
    tm = _divisor(rows, MAX_TILE_ROWS, 2 * SUBLANES) if skip else _token_tile(L)

    def body(a_ref, b_ref, h_ref, g_ref, dres_ref, dg0_ref, dh_ref, dg_ref):
        i = pl.program_id(0)
        dhn = jnp.dot(a_ref[...], b_ref[...], preferred_element_type=F32)
        x = h_ref[...]
        r = lax.rsqrt(jnp.mean(x * x, axis=-1, keepdims=True) + EPS)
        xhat = x * r
        dxhat = dhn * g_ref[...]
        dh_ref[...] = dres_ref[...] + r * (dxhat - xhat * jnp.mean(dxhat * xhat, axis=-1, keepdims=True))
        part = jnp.sum(_rowsum8(dhn * xhat), axis=0, keepdims=True)

        @pl.when(i == 0)
        def _():
            dg_ref[...] = dg0_ref[...] + part

        @pl.when(i > 0)
        def _():
            dg_ref[...] += part

    def rows_of(cols, first=skip):
        if not first:
            return pl.BlockSpec((tm, cols), lambda i: (i, 0))
        return pl.BlockSpec((pl.Element(tm), pl.Element(cols)), lambda i: (pl.multiple_of(first + i * tm, SUBLANES), 0))

    row = pl.BlockSpec((1, D), lambda i: (0, 0))
    h_first = skip if h.shape[0] == L else 0
    out, xo = _call(
        body, name=name, grid=(rows // tm,),
        in_specs=[rows_of(K), pl.BlockSpec((K, D), lambda i: (0, 0)), rows_of(D, h_first), row, rows_of(D), row],
        out_specs=(pl.BlockSpec((tm, D), lambda i: (i, 0)), row),
        out_shape=(jax.ShapeDtypeStruct((rows, D), F32), jax.ShapeDtypeStruct((1, D), F32)),
        sem=("arbitrary",), args=(a, b, h, g, dres, dg0), xchg=xchg)
    return out if xchg is None else (out, xo)


def _mm(a, b, name, *, res=None, b_t=False, tn_cap=1408, xchg=None):
    M, K = a.shape
    N = b.shape[0] if b_t else b.shape[1]
    tm = _token_tile(M)
    tn = _divisor(N, tn_cap, 128)
    if isinstance(res, tuple):
        assert tn == N
        r_specs, r_args, read_r = _seq_tiles(res, tm)
    elif res is not None:
        r_specs, r_args, read_r = [pl.BlockSpec((tm, tn), lambda i, j: (i, j))], [res], lambda refs, i: refs[0][...]
    else:
        r_specs, r_args, read_r = [], [], None

    def body(*refs):
        a_ref, b_ref, o_ref = refs[0], refs[1], refs[-1]
        av = a_ref[...].astype(BF16)
        prod = _dot_nt(av, b_ref[...]) if b_t else jnp.dot(av, b_ref[...], preferred_element_type=F32)
        o_ref[...] = prod if read_r is None else prod + read_r(refs[2:-1], pl.program_id(0))

    b_spec = pl.BlockSpec((tn, K), lambda i, j: (j, 0)) if b_t else pl.BlockSpec((K, tn), lambda i, j: (0, j))
    in_specs = [pl.BlockSpec((tm, K), lambda i, j: (i, 0)), b_spec] + r_specs
    args = [a, b] + r_args
    out, xo = _call(
        body, name=name, grid=(M // tm, N // tn), in_specs=in_specs,
        out_specs=pl.BlockSpec((tm, tn), lambda i, j: (i, j)), out_shape=jax.ShapeDtypeStruct((M, N), F32),
        sem=("parallel", "parallel"), args=args, xchg=xchg)
    return out if xchg is None else (out, xo)


def _mm_tn(a, b, name, *, halves=1, tq_cap=1408):
    L, Q = b.shape
    ph = a.shape[-1]
    P = ph * halves
    tl = _divisor(L, WGRAD_TILE_ROWS, HALO)
    tp = _divisor(ph, 1408, 128)
    tq = _divisor(Q, tq_cap, 128)
    pper = ph // tp
    nl = L // tl
    grid = (P // tp, Q // tq, nl)

    def body(a_ref, b_ref, o_ref, acc):
        prod = lax.dot_general(a_ref[...].astype(BF16), b_ref[...].astype(BF16), (((0,), (0,)), ((), ())),
                               preferred_element_type=F32)
        l = pl.program_id(2)
        if nl == 1:
            o_ref[...] = prod.astype(BF16)
            return

        @pl.when(l == 0)
        def _():
            acc[...] = prod

        @pl.when(jnp.logical_and(l > 0, l < nl - 1))
        def _():
            acc[...] += prod

        @pl.when(l == nl - 1)
        def _():
            o_ref[...] = (acc[...] + prod).astype(BF16)

    if halves > 1:
        a_spec = pl.BlockSpec((None, tl, tp), lambda p, q, l: (p // pper, l, p % pper))
    else:
        a_spec = pl.BlockSpec((tl, tp), lambda p, q, l: (l, p))
    return pl.pallas_call(
        body, name=name, grid=grid,
        in_specs=[a_spec, pl.BlockSpec((tl, tq), lambda p, q, l: (l, q))],
        out_specs=pl.BlockSpec((tp, tq), lambda p, q, l: (p, q)),
        out_shape=jax.ShapeDtypeStruct((P, Q), BF16),
        scratch_shapes=[pltpu.VMEM((tp, tq), F32)],
        compiler_params=_params(("parallel", "parallel", "arbitrary")),
    )(a, b)


def _loss_head(h, g, tgt, n_meta, name):
    L, D = h.shape
    tl = _token_tile(L)
    nt = L // tl

    def body(h_ref, g_ref, t_ref, dh_ref, dg_ref, loss_ref):
        i = pl.program_id(0)
        x = h_ref[...]
        r = lax.rsqrt(jnp.mean(x * x, axis=-1, keepdims=True) + EPS)
        xhat = x * r
        gg = g_ref[...]
        y = xhat * gg
        rows = i * tl + lax.broadcasted_iota(jnp.int32, (tl, 1), 0)
        t = t_ref[...]
        t = jnp.where(i == 0, pltpu.roll(t, n_meta, axis=0), t)
        err = jnp.where(rows >= n_meta, y - t, 0.0)
        dy = err * (1.0 / D)
        dxhat = dy * gg
        dh_ref[...] = r * (dxhat - xhat * jnp.mean(dxhat * xhat, axis=-1, keepdims=True))
        dg_part = jnp.sum(_rowsum8(dy * xhat), axis=0, keepdims=True)
        per_row = jnp.mean(err * err, axis=-1, keepdims=True)
        loss_part = jnp.broadcast_to(0.5 * jnp.sum(per_row, axis=0, keepdims=True), (1, 128))

        @pl.when(i == 0)
        def _():
            dg_ref[...] = dg_part
            loss_ref[...] = loss_part

        @pl.when(i > 0)
        def _():
            dg_ref[...] += dg_part
            loss_ref[...] += loss_part

    tile = pl.BlockSpec((tl, D), lambda i: (i, 0))
    row = pl.BlockSpec((1, D), lambda i: (0, 0))
    window = pl.BlockSpec((pl.Element(tl), pl.Element(D)),
                          lambda i: (pl.multiple_of(jnp.maximum(i * tl - n_meta, 0), SUBLANES), 0))
    return pl.pallas_call(
        body, name=name, grid=(nt,), in_specs=[tile, row, window],
        out_specs=(tile, row, pl.BlockSpec((1, 128), lambda i: (0, 0))),
        out_shape=(jax.ShapeDtypeStruct((L, D), F32), jax.ShapeDtypeStruct((1, D), F32),
                   jax.ShapeDtypeStruct((1, 128), F32)),
        compiler_params=_params(("arbitrary",)),
    )(h, g, tgt)


def _pool_fwd_block(pwin, pw_ref, row0, rb, g, gd, w, t0):
    wv = pwin[pl.ds(row0 + HALO - POOL_PAD, rb + POOL_PAD), g * gd:(g + 1) * gd]
    s = wv
    sh = 1
    while sh < w:
        s = s + pltpu.roll(s, sh, axis=0)
        sh *= 2
    win = s[POOL_PAD:POOL_PAD + rb]
    pt = wv[POOL_PAD:POOL_PAD + rb]
    tg = t0 + lax.broadcasted_iota(jnp.int32, (rb, 1), 0)
    cnt = jnp.minimum(tg + 1, w).astype(F32)
    return win / cnt - pt


def _fill_windows(i, zp_ref, zc_ref, u0w, pwin, tl, cc):
    keep = i > 0
    zp = zp_ref[...]
    u0w[0:HALO, :] = jnp.where(keep, zp[:, :cc] * _sigmoid(zp[:, cc:2 * cc]), 0.0)
    pwin[0:HALO, :] = jnp.where(keep, zp[:, 2 * cc:], 0.0)

    def fill(c, carry):
        b = pl.multiple_of(c * ROW_CHUNK, SUBLANES)
        zc = zc_ref[pl.ds(b, ROW_CHUNK), :]
        u0w[pl.ds(HALO + b, ROW_CHUNK), :] = zc[:, :cc] * _sigmoid(zc[:, cc:2 * cc])
        pwin[pl.ds(HALO + b, ROW_CHUNK), :] = zc[:, 2 * cc:]
        return carry

    lax.fori_loop(0, tl // ROW_CHUNK, fill, 0)


def _mixer_fwd(z, ck, cb, lg, lb, pw, ps, am, name, xchg=None):
    L, ci = z.shape
    kw, _, cc = ck.shape
    cp = ci - 2 * cc
    ng, gd = pw.shape[0], pw.shape[1]
    tl = _token_tile(L)
    nt = L // tl
    hb = tl // HALO
    rb = _stat_rows(tl)
    tap0 = CONV_PAD - (kw - 1)

    def body(zp_ref, zc_ref, ck_ref, cb_ref, lg_ref, lb_ref, pw_ref, ps_ref, am_ref, y_ref, u1_ref, u0w, pwin):
        i = pl.program_id(0)
        _fill_windows(i, zp_ref, zc_ref, u0w, pwin, tl, cc)

        def conv(c, carry):
            b = pl.multiple_of(c * ROW_CHUNK, SUBLANES)
            w = u0w[pl.ds(b + HALO - CONV_PAD, ROW_CHUNK + CONV_PAD), :]
            acc = jnp.broadcast_to(cb_ref[...], (ROW_CHUNK, cc))
            for j in range(kw):
                acc = acc + _rows_of(ck_ref[j], ROW_CHUNK) * w[tap0 + j:tap0 + j + ROW_CHUNK]
            u1_ref[pl.ds(b, ROW_CHUNK), :] = acc
            return carry

        lax.fori_loop(0, tl // ROW_CHUNK, conv, 0)

        def blocks(k, carry):
            b = pl.multiple_of(k * rb, SUBLANES)
            u1 = u1_ref[pl.ds(b, rb), :]
            xc = u1 - _head_mean(u1, am_ref)
            var = _head_mean(xc * xc, am_ref)
            u2 = (xc * lax.rsqrt(var + EPS)) * lg_ref[...] + lb_ref[...]
            y_ref[pl.ds(b, rb), 0:cc] = (u2 * _sigmoid(u2)).astype(y_ref.dtype)
            for g in range(ng):
                d = _pool_fwd_block(pwin, pw_ref, b, rb, g, gd, POOL_WINDOWS[g], i * tl + b)
                yp = jnp.dot(d.astype(BF16), pw_ref[g].astype(BF16), preferred_element_type=F32)
                yp = yp * ps_ref[:, g * gd:(g + 1) * gd]
                y_ref[pl.ds(b, rb), cc + g * gd:cc + (g + 1) * gd] = yp.astype(y_ref.dtype)
            return carry

        lax.fori_loop(0, tl // rb, blocks, 0)

    def full(a):
        nd = a.ndim
        return pl.BlockSpec(a.shape, lambda i: (0,) * nd)

    out, xo = _call(
        body, name=name, grid=(nt,),
        in_specs=[pl.BlockSpec((HALO, ci), lambda i: (jnp.maximum(i * hb - 1, 0), 0)),
                  pl.BlockSpec((tl, ci), lambda i: (i, 0)),
                  full(ck), full(cb), full(lg), full(lb), full(pw), full(ps), full(am)],
        out_specs=(pl.BlockSpec((tl, cc + cp), lambda i: (i, 0)), pl.BlockSpec((tl, cc), lambda i: (i, 0))),
        out_shape=(jax.ShapeDtypeStruct((L, cc + cp), BF16), jax.ShapeDtypeStruct((L, cc), F32)),
        scratch_shapes=[pltpu.VMEM((HALO + tl, cc), F32), pltpu.VMEM((HALO + tl, cp), F32)],
        sem=("parallel",), args=(z, z, ck, cb, lg, lb, pw, ps, am), xchg=xchg)
    return out if xchg is None else (out, xo)


def _mixer_bwd(z, u1, dy, ck, lg, lb, pw, ps, am, name, xchg=None):
    L, ci = z.shape
    kw, _, cc = ck.shape
    cp = ci - 2 * cc
    ng, gd = pw.shape[0], pw.shape[1]
    tl = _token_tile(L)
    nt = L // tl
    hb = tl // HALO
    rb = _stat_rows(tl)

    def body(zp_ref, zc_ref, u1c_ref, u1n_ref, dyc_ref, dyn_ref, ck_ref, lg_ref, lb_ref, pw_ref, ps_ref, am_ref,
             dz_ref, dck_ref, dcb_ref, dlg_ref, dlb_ref, dpw_ref, dps_ref,
             u0w, pwin, du1w, ddw, ew, dkacc, dcb8, dlg8, dlb8, dps8):
        i = pl.program_id(0)
        has_next = i < nt - 1

        @pl.when(i == 0)
        def _():
            dck_ref[...] = jnp.zeros_like(dck_ref)
            dcb_ref[...] = jnp.zeros_like(dcb_ref)
            dlg_ref[...] = jnp.zeros_like(dlg_ref)
            dlb_ref[...] = jnp.zeros_like(dlb_ref)
            dpw_ref[...] = jnp.zeros_like(dpw_ref)
            dps_ref[...] = jnp.zeros_like(dps_ref)

        dkacc[...] = jnp.zeros_like(dkacc)
        dcb8[...] = jnp.zeros_like(dcb8)
        dlg8[...] = jnp.zeros_like(dlg8)
        dlb8[...] = jnp.zeros_like(dlb8)
        dps8[...] = jnp.zeros_like(dps8)

        _fill_windows(i, zp_ref, zc_ref, u0w, pwin, tl, cc)

        def conv_side(u1, dyc, own):
            xc = u1 - _head_mean(u1, am_ref)
            rstd = lax.rsqrt(_head_mean(xc * xc, am_ref) + EPS)
            uh = xc * rstd
            lgv = lg_ref[...]
            u2 = uh * lgv + lb_ref[...]
            sg = _sigmoid(u2)
            du2 = dyc * (sg * (1.0 + u2 * (1.0 - sg)))
            if own:
                dlg8[...] += _rowsum8(du2 * uh)
                dlb8[...] += _rowsum8(du2)
            duh = du2 * lgv
            return rstd * (duh - _head_mean(duh, am_ref) - uh * _head_mean(duh * uh, am_ref))

        def pool_side(dyp, t0, rows):
            dds, es = [], []
            tg = t0 + lax.broadcasted_iota(jnp.int32, (rows, 1), 0)
            for g in range(ng):
                dypre = dyp[:, g * gd:(g + 1) * gd] * ps_ref[:, g * gd:(g + 1) * gd]
                dd = lax.dot_general(dypre.astype(BF16), pw_ref[g].astype(BF16), (((1,), (1,)), ((), ())),
                                     preferred_element_type=F32)
                cnt = jnp.minimum(tg + 1, POOL_WINDOWS[g]).astype(F32)
                dds.append(dd)
                es.append(dd / cnt)
            return jnp.concatenate(dds, axis=-1), jnp.concatenate(es, axis=-1)

        def blocks(k, carry):
            b = pl.multiple_of(k * rb, SUBLANES)
            dyb = dyc_ref[pl.ds(b, rb), :]
            du1 = conv_side(u1c_ref[pl.ds(b, rb), :], dyb[:, :cc], True)
            du1w[pl.ds(b, rb), :] = du1
            dcb8[...] += _rowsum8(du1)
            dyp = dyb[:, cc:]
            dd, e = pool_side(dyp, i * tl + b, rb)
            ddw[pl.ds(b, rb), :] = dd
            ew[pl.ds(b, rb), :] = e
            for g in range(ng):
                d = _pool_fwd_block(pwin, pw_ref, b, rb, g, gd, POOL_WINDOWS[g], i * tl + b)
                db16 = d.astype(BF16)
                dypg = dyp[:, g * gd:(g + 1) * gd]
                ypre = jnp.dot(db16, pw_ref[g].astype(BF16), preferred_element_type=F32)
                dps8[:, g * gd:(g + 1) * gd] += _rowsum8(dypg * ypre)
                dypre = (dypg * ps_ref[:, g * gd:(g + 1) * gd]).astype(BF16)
                dpw_ref[g] += lax.dot_general(db16, dypre, (((0,), (0,)), ((), ())), preferred_element_type=F32)
            return carry

        lax.fori_loop(0, tl // rb, blocks, 0)

        dyn = dyn_ref[...]
        du1n = conv_side(u1n_ref[...], dyn[:, :cc], False)
        du1w[tl:tl + HALO, :] = jnp.where(has_next, du1n, 0.0)
        ddn, en = pool_side(dyn[:, cc:], (i + 1) * tl, HALO)
        ew[tl:tl + HALO, :] = jnp.where(has_next, en, 0.0)

        def taps(c, carry):
            b = pl.multiple_of(c * ROW_CHUNK, SUBLANES)
            w = du1w[pl.ds(b, ROW_CHUNK + CONV_PAD), :]
            u0c = u0w[pl.ds(HALO + b, ROW_CHUNK), :]
            acc = jnp.zeros((ROW_CHUNK, cc), F32)
            for j in range(kw):
                o = kw - 1 - j
                sh = w[o:o + ROW_CHUNK]
                acc = acc + _rows_of(ck_ref[j], ROW_CHUNK) * sh
                dkacc[j] += _rowsum8(u0c * sh)
            zc = zc_ref[pl.ds(b, ROW_CHUNK), :]
            a = zc[:, :cc]
            sg = _sigmoid(zc[:, cc:2 * cc])
            dz_ref[pl.ds(b, ROW_CHUNK), 0:cc] = (acc * sg).astype(dz_ref.dtype)
            dz_ref[pl.ds(b, ROW_CHUNK), cc:2 * cc] = (acc * a * sg * (1.0 - sg)).astype(dz_ref.dtype)
            return carry

        lax.fori_loop(0, tl // ROW_CHUNK, taps, 0)

        def pool_back(k, carry):
            b = pl.multiple_of(k * rb, SUBLANES)
            n = rb + POOL_PAD
            for g in range(ng):
                s = ew[pl.ds(b, n), g * gd:(g + 1) * gd]
                sh = 1
                while sh < POOL_WINDOWS[g]:
                    s = s + pltpu.roll(s, n - sh, axis=0)
                    sh *= 2
                dp = s[0:rb] - ddw[pl.ds(b, rb), g * gd:(g + 1) * gd]
                dz_ref[pl.ds(b, rb), 2 * cc + g * gd:2 * cc + (g + 1) * gd] = dp.astype(dz_ref.dtype)
            return carry

        lax.fori_loop(0, tl // rb, pool_back, 0)

        dck_ref[...] += jnp.sum(dkacc[...], axis=1)
        dcb_ref[...] += jnp.sum(dcb8[...], axis=0, keepdims=True)
        dlg_ref[...] += jnp.sum(dlg8[...], axis=0, keepdims=True)
        dlb_ref[...] += jnp.sum(dlb8[...], axis=0, keepdims=True)
        dps_ref[...] += jnp.sum(dps8[...], axis=0, keepdims=True)

    def full(a):
        nd = a.ndim
        return pl.BlockSpec(a.shape, lambda i: (0,) * nd)

    nhb = L // HALO

    def prev_map(i):
        return (jnp.maximum(i * hb - 1, 0), 0)

    def next_map(i):
        return (jnp.minimum((i + 1) * hb, nhb - 1), 0)

    dcc = cc + cp
    row_cc = jax.ShapeDtypeStruct((1, cc), F32)
    out_shape = (jax.ShapeDtypeStruct((L, ci), BF16), jax.ShapeDtypeStruct((kw, cc), F32), row_cc, row_cc, row_cc,
                 jax.ShapeDtypeStruct((ng, gd, gd), F32), jax.ShapeDtypeStruct((1, cp), F32))
    acc_spec = [pl.BlockSpec((kw, cc), lambda i: (0, 0))] + [pl.BlockSpec((1, cc), lambda i: (0, 0))] * 3 + [
        pl.BlockSpec((ng, gd, gd), lambda i: (0, 0, 0)), pl.BlockSpec((1, cp), lambda i: (0, 0))]
    out, xo = _call(
        body, name=name, grid=(nt,),
        in_specs=[pl.BlockSpec((HALO, ci), prev_map), pl.BlockSpec((tl, ci), lambda i: (i, 0)),
                  pl.BlockSpec((tl, cc), lambda i: (i, 0)), pl.BlockSpec((HALO, cc), next_map),
                  pl.BlockSpec((tl, dcc), lambda i: (i, 0)), pl.BlockSpec((HALO, dcc), next_map),
                  full(ck), full(lg), full(lb), full(pw), full(ps), full(am)],
        out_specs=tuple([pl.BlockSpec((tl, ci), lambda i: (i, 0))] + acc_spec),
        out_shape=out_shape,
        scratch_shapes=[pltpu.VMEM((HALO + tl, cc), F32), pltpu.VMEM((HALO + tl, cp), F32),
                        pltpu.VMEM((tl + HALO, cc), F32), pltpu.VMEM((tl, cp), F32), pltpu.VMEM((tl + HALO, cp), F32),
                        pltpu.VMEM((kw, SUBLANES, cc), F32), pltpu.VMEM((SUBLANES, cc), F32),
                        pltpu.VMEM((SUBLANES, cc), F32), pltpu.VMEM((SUBLANES, cc), F32), pltpu.VMEM((SUBLANES, cp), F32)],
        sem=("arbitrary",), args=(z, z, u1, u1, dy, dy, ck, lg, lb, pw, ps, am), xchg=xchg)
    return out if xchg is None else (out, xo)


def _row_parts(nc, n=3):
    n = min(n, nc)
    cuts = [round(k * nc / n) for k in range(n + 1)]
    return [(cuts[k], cuts[k + 1]) for k in range(n)]


def _tap_rows(k_ref):
    return [jnp.broadcast_to(k_ref[j:j + 1, :], (SUBLANES, k_ref.shape[1])) for j in range(k_ref.shape[0])]


def _rows_of(tap, n):
    return tap if n == SUBLANES else jnp.concatenate([tap] * (n // SUBLANES), axis=0)


def _ffn_conv(win, taps, rows):
    kw = len(taps)
    o = FFN_PAD - (kw - 1)
    acc = _rows_of(taps[0], rows) * win[o:o + rows]
    for j in range(1, kw):
        acc = acc + _rows_of(taps[j], rows) * win[o + j:o + j + rows]
    return acc


def _ffn_block_fwd(h_mid, g, wup_t, kf, wdown, name, xchg=None):
    L, D = h_mid.shape
    f = wdown.shape[0]
    kw = kf.shape[0]
    tl = _token_tile(L)
    tc = _divisor(f, 256, 128)
    nj = f // tc
    nt = L // tl
    pad = 2 * SUBLANES
    hb = tl // pad
    rc = CONV3_ROWS
    parts = _row_parts(tl // rc)

    def body(hp_ref, hc_ref, g_ref, wg_ref, wv_ref, kg_ref, kv_ref, wd_ref, out_ref, hn_ref, act_ref, ux_ref, uc_ref,
             hn_halo, halo, ug_ref, acc):
        i = pl.program_id(0)
        kb = pl.program_id(1)

        @pl.when(kb == 0)
        def _():
            gg = g_ref[...]

            def norm(x):
                r = lax.rsqrt(jnp.mean(x * x, axis=-1, keepdims=True) + EPS)
                return ((x * r) * gg).astype(BF16)

            hn_halo[...] = jnp.where(i > 0, norm(hp_ref[...]), jnp.zeros((pad, D), BF16))
            hn_ref[...] = norm(hc_ref[...])
            acc[...] = jnp.zeros_like(acc)

        w_refs = (wg_ref, wv_ref)
        taps = (_tap_rows(kg_ref), _tap_rows(kv_ref))
        hh = hn_halo[...]
        for h in range(2):
            halo[h] = _dot_nt(hh, w_refs[h][...])[pad - FFN_PAD:]

        def up_part(lo, hi):
            a, b = lo * rc, hi * rc
            for h in range(2):
                ug_ref[h, a:b, :] = _dot_nt(hn_ref[a:b, :], w_refs[h][...])

        def down_part(lo, hi):
            a, b = lo * rc, hi * rc
            acc[a:b, :] += jnp.dot(act_ref[a:b, :], wd_ref[...], preferred_element_type=F32)

        def chunk_rows(lo, hi):
            for c in range(lo, hi):
                r0 = c * rc
                convd = []
                for h in range(2):
                    if c == 0:
                        win = jnp.concatenate([halo[h], ug_ref[h, 0:rc]], axis=0)
                    else:
                        win = ug_ref[h, r0 - FFN_PAD:r0 + rc]
                    convd.append(_ffn_conv(win, taps[h], rc))
                    ux_ref[h, r0:r0 + rc, :] = win[FFN_PAD:].astype(BF16)
                    uc_ref[h, r0:r0 + rc, :] = convd[h].astype(BF16)
                gate, val = convd
                act_ref[r0:r0 + rc, :] = ((gate * _sigmoid(gate)) * val).astype(BF16)

        for p, (lo, hi) in enumerate(parts):
            if p == 0:
                up_part(lo, hi)
            if p + 1 < len(parts):
                up_part(*parts[p + 1])
            if p > 0:
                down_part(*parts[p - 1])
            chunk_rows(lo, hi)
        down_part(*parts[-1])

        @pl.when(kb == nj - 1)
        def _():
            out_ref[...] = acc[...] + hc_ref[...]

    out, xo = _call(
        body, name=name, grid=(nt, nj),
        in_specs=[pl.BlockSpec((pad, D), lambda i, k: (jnp.maximum(i * hb - 1, 0), 0)),
                  pl.BlockSpec((tl, D), lambda i, k: (i, 0)),
                  pl.BlockSpec((1, D), lambda i, k: (0, 0)),
                  pl.BlockSpec((tc, D), lambda i, k: (k, 0)), pl.BlockSpec((tc, D), lambda i, k: (k + nj, 0)),
                  pl.BlockSpec((kw, tc), lambda i, k: (0, k)), pl.BlockSpec((kw, tc), lambda i, k: (0, k + nj)),
                  pl.BlockSpec((tc, D), lambda i, k: (k, 0))],
        out_specs=(pl.BlockSpec((tl, D), lambda i, k: (i, 0)), pl.BlockSpec((tl, D), lambda i, k: (i, 0)),
                   pl.BlockSpec((tl, tc), lambda i, k: (i, k)),
                   pl.BlockSpec((2, tl, tc), lambda i, k: (0, i, k)), pl.BlockSpec((2, tl, tc), lambda i, k: (0, i, k))),
        out_shape=(jax.ShapeDtypeStruct((L, D), F32), jax.ShapeDtypeStruct((L, D), BF16),
                   jax.ShapeDtypeStruct((L, f), BF16),
                   jax.ShapeDtypeStruct((2, L, f), BF16), jax.ShapeDtypeStruct((2, L, f), BF16)),
        scratch_shapes=[pltpu.VMEM((pad, D), BF16), pltpu.VMEM((2, FFN_PAD, tc), F32), pltpu.VMEM((2, tl, tc), F32),
                        pltpu.VMEM((tl, D), F32)],
        sem=("parallel", "arbitrary"), args=(h_mid, h_mid, g, wup_t, wup_t, kf, kf, wdown), xchg=xchg)
    return out if xchg is None else (out, xo)


def _ffn_block_bwd(dh, h_mid, g, ux, uc, kf, wdown, wup_t, name, xchg=None):
    L, D = dh.shape
    f = ux.shape[2]
    kw = kf.shape[0]
    tl = _token_tile(L)
    tc = _divisor(f, 256, 128)
    nj = f // tc
    nt = L // tl
    pad = 2 * SUBLANES
    rc = CONV3_ROWS
    nc = tl // rc
    parts = _row_parts(nc)

    def body(dhc_ref, dhn_ref, hm_ref, g_ref, xg_ref, xv_ref, cg_ref, cgn_ref, cv_ref, cvn_ref, kg_ref, kv_ref,
             wd_ref, wg_ref, wv_ref, dhm_ref, dg_ref, du_ref, dk_ref, dh_ext, dact_s, acc):
        i = pl.program_id(0)
        kb = pl.program_id(1)

        @pl.when(kb == 0)
        def _():
            dh_ext[0:tl, :] = dhc_ref[...].astype(BF16)
            dh_ext[tl:tl + pad, :] = dhn_ref[...].astype(BF16)
            acc[...] = jnp.zeros_like(acc)

        @pl.when(jnp.logical_and(i == 0, kb == 0))
        def _():
            dg_ref[...] = jnp.zeros_like(dg_ref)
            dk_ref[...] = jnp.zeros_like(dk_ref)

        x_refs, c_refs, nxt = (xg_ref, xv_ref), (cg_ref, cv_ref), (cgn_ref, cvn_ref)
        taps = (_tap_rows(kg_ref), _tap_rows(kv_ref))
        dk = [[jnp.zeros((SUBLANES, tc), F32) for _ in range(kw)] for _ in range(2)]

        def dact_part(lo, hi):
            a, b = lo * rc, hi * rc + pad
            dact_s[a:b, :] = _dot_nt(dh_ext[a:b, :], wd_ref[...])

        def dhn_part(lo, hi):
            a, b = lo * rc, hi * rc
            acc[a:b, :] += (jnp.dot(du_ref[0, a:b, :], wg_ref[...], preferred_element_type=F32)
                            + jnp.dot(du_ref[1, a:b, :], wv_ref[...], preferred_element_type=F32))

        for p, (lo, hi) in enumerate(parts):
            if p == 0:
                dact_part(lo, hi)
            if p + 1 < len(parts):
                dact_part(*parts[p + 1])
            if p > 0:
                dhn_part(*parts[p - 1])
            chunk_rows(lo, hi, x_refs, c_refs, nxt, taps, dk, i, dact_s, du_ref)
        dhn_part(*parts[-1])
        for h in range(2):
            for j in range(kw):
                dk_ref[kb, h, j:j + 1, :] += jnp.sum(dk[h][j], axis=0, keepdims=True)

        @pl.when(kb == nj - 1)
        def _():
            x = hm_ref[...]
            r = lax.rsqrt(jnp.mean(x * x, axis=-1, keepdims=True) + EPS)
            xhat = x * r
            dhn = acc[...]
            dxhat = dhn * g_ref[...]
            dhm_ref[...] = dhc_ref[...] + r * (dxhat - xhat * jnp.mean(dxhat * xhat, axis=-1, keepdims=True))
            dg_ref[...] += jnp.sum(_rowsum8(dhn * xhat), axis=0, keepdims=True)

    def chunk_rows(lo, hi, x_refs, c_refs, nxt, taps, dk, i, dact_s, du_ref):
        for c in range(lo, hi):
            r0 = c * rc
            n = rc + FFN_PAD
            convd = []
            for h in range(2):
                if c == nc - 1:
                    rows = jnp.concatenate([c_refs[h][r0:r0 + rc, :], nxt[h][...]], axis=0)
                else:
                    rows = c_refs[h][r0:r0 + rc + pad, :]
                convd.append(rows.astype(F32)[0:n])
            gate, val = convd
            xs = [x_refs[h][r0:r0 + rc, :].astype(F32) for h in range(2)]
            dact = dact_s[r0:r0 + n, :]
            sg = _sigmoid(gate)
            dcs = [dact * val * (sg * (1.0 + gate * (1.0 - sg))), dact * (gate * sg)]
            if c == nc - 1:
                live = jnp.logical_or(lax.broadcasted_iota(jnp.int32, (n, 1), 0) < rc, i < nt - 1)
                dcs = [jnp.where(live, d, 0.0) for d in dcs]
            for h in range(2):
                xc = xs[h]
                dx = None
                for j in range(kw):
                    o = kw - 1 - j
                    sh = dcs[h][o:o + rc]
                    term = _rows_of(taps[h][j], rc) * sh
                    dx = term if dx is None else dx + term
                    dk[h][j] = dk[h][j] + _rowsum8(xc * sh)
                du_ref[h, r0:r0 + rc, :] = dx.astype(BF16)

    def after(i):
        return jnp.minimum((i + 1) * (tl // pad), L // pad - 1)

    def half(h, rows, idx):
        return pl.BlockSpec((None, rows, tc), lambda i, k: (h,) + idx(i, k))

    def tile(i, k):
        return (i, k)

    def behind(i, k):
        return (after(i), k)

    out, xo = _call(
        body, name=name, grid=(nt, nj),
        in_specs=[pl.BlockSpec((tl, D), lambda i, k: (i, 0)),
                  pl.BlockSpec((pad, D), lambda i, k: (after(i), 0)),
                  pl.BlockSpec((tl, D), lambda i, k: (i, 0)), pl.BlockSpec((1, D), lambda i, k: (0, 0)),
                  half(0, tl, tile), half(1, tl, tile),
                  half(0, tl, tile), half(0, pad, behind), half(1, tl, tile), half(1, pad, behind),
                  pl.BlockSpec((kw, tc), lambda i, k: (0, k)), pl.BlockSpec((kw, tc), lambda i, k: (0, k + nj)),
                  pl.BlockSpec((tc, D), lambda i, k: (k, 0)),
                  pl.BlockSpec((tc, D), lambda i, k: (k, 0)), pl.BlockSpec((tc, D), lambda i, k: (k + nj, 0))],
        out_specs=(pl.BlockSpec((tl, D), lambda i, k: (i, 0)), pl.BlockSpec((1, D), lambda i, k: (0, 0)),
                   pl.BlockSpec((2, tl, tc), lambda i, k: (0, i, k)),
                   pl.BlockSpec((nj, 2, kw, tc), lambda i, k: (0, 0, 0, 0))),
        out_shape=(jax.ShapeDtypeStruct((L, D), F32), jax.ShapeDtypeStruct((1, D), F32),
                   jax.ShapeDtypeStruct((2, L, f), BF16), jax.ShapeDtypeStruct((nj, 2, kw, tc), F32)),
        scratch_shapes=[pltpu.VMEM((tl + pad, D), BF16), pltpu.VMEM((tl + pad, tc), F32), pltpu.VMEM((tl, D), F32)],
        sem=("arbitrary", "arbitrary"), args=(dh, dh, h_mid, g, ux, ux, uc, uc, uc, uc, kf, kf, wdown, wup_t, wup_t),
        xchg=xchg)
    return out if xchg is None else (out, xo)


def _adamw_math(w, g, m, v):
    m = ADAM_B1 * m + (1.0 - ADAM_B1) * g
    v = ADAM_B2 * v + (1.0 - ADAM_B2) * (g * g)
    m_hat = m / (1.0 - ADAM_B1 ** ADAM_STEP)
    v_hat = v / (1.0 - ADAM_B2 ** ADAM_STEP)
    delta = -ADAM_LR * (m_hat / (jnp.sqrt(v_hat) + ADAM_EPS) + ADAM_WD * w)
    return delta, m, v


def _sum_parts(parts_ref, idx):
    g = parts_ref[(0,) + idx].astype(F32)
    for q in range(1, N_DEV):
        g = g + parts_ref[(q,) + idx].astype(F32)
    return g


def _adamw_big(parts, w, m, v, name):
    nl, R, C = w.shape
    tr = _divisor(R, 256, 2 * SUBLANES)

    def body(*refs):
        p_refs = refs[:nl]
        w_ref, m_ref, v_ref, g_ref, d_ref, nm_ref, nv_ref = refs[nl:]
        layer = pl.program_id(0)
        for k in range(nl):
            @pl.when(layer == k)
            def _(k=k):
                g = _sum_parts(p_refs[k], ())
                d, nm, nv = _adamw_math(w_ref[0], g, m_ref[0], v_ref[0])
                g_ref[0] = g
                d_ref[0] = d
                nm_ref[0] = nm
                nv_ref[0] = nv

    def part_spec(k):
        return pl.BlockSpec((N_DEV, tr, C), lambda l, r: (0, jnp.where(l == k, r, 0), 0))

    blk = pl.BlockSpec((1, tr, C), lambda l, r: (l, r, 0))
    shp = jax.ShapeDtypeStruct((nl, R, C), F32)
    return pl.pallas_call(
        body, name=name, grid=(nl, R // tr),
        in_specs=[part_spec(k) for k in range(nl)] + [blk, blk, blk],
        out_specs=(blk, blk, blk, blk), out_shape=(shp, shp, shp, shp),
        compiler_params=_params(("arbitrary", "arbitrary")),
    )(*parts, w, m, v)


def _adamw_small(entries, name):
    n = len(entries)
    uniq = []
    for e in entries:
        if not any(e[0] is u for u in uniq):
            uniq.append(e[0])
    pidx = [next(k for k, u in enumerate(uniq) if u is e[0]) for e in entries]
    npart = len(uniq)

    def body(*refs):
        p_refs = refs[:npart]
        wmv = refs[npart:npart + 3 * n]
        outs = refs[npart + 3 * n:]
        for t, e in enumerate(entries):
            lo, w = e[1], e[2]
            rows = w.shape[0]
            pr = p_refs[pidx[t]]
            g = pr[0, lo:lo + rows].astype(F32)
            for q in range(1, N_DEV):
                g = g + pr[q, lo:lo + rows].astype(F32)
            d, nm, nv = _adamw_math(wmv[3 * t][...], g, wmv[3 * t + 1][...], wmv[3 * t + 2][...])
            outs[4 * t][...] = g
            outs[4 * t + 1][...] = d
            outs[4 * t + 2][...] = nm
            outs[4 * t + 3][...] = nv

    vm = pl.BlockSpec(memory_space=pltpu.VMEM)
    args = list(uniq)
    out_shape = []
    for e in entries:
        args += [e[2], e[3], e[4]]
        out_shape += [jax.ShapeDtypeStruct(e[2].shape, F32)] * 4
    res = pl.pallas_call(
        body, name=name, in_specs=[vm] * len(args), out_specs=tuple([vm] * len(out_shape)),
        out_shape=tuple(out_shape), compiler_params=_params(),
    )(*args)
    return [tuple(res[4 * t:4 * t + 4]) for t in range(n)]


def _head_matrix(cc):
    bw = min(256, cc)
    r = lax.broadcasted_iota(jnp.int32, (bw, bw), 0) // HEAD_DIM
    c = lax.broadcasted_iota(jnp.int32, (bw, bw), 1) // HEAD_DIM
    return jnp.where(r == c, 1.0 / HEAD_DIM, 0.0).astype(BF16)


def _cols_from_shards(g):
    nd = g.ndim
    perm = tuple(range(1, nd - 1)) + (0, nd - 1)
    t = jnp.transpose(g, perm)
    return t.reshape(t.shape[:-2] + (t.shape[-2] * t.shape[-1],))


def _cols_to_shards(a):
    nd = a.ndim
    t = a.reshape(a.shape[:-1] + (N_DEV, a.shape[-1] // N_DEV))
    perm = (nd - 1,) + tuple(range(nd - 1)) + (nd,)
    return jnp.transpose(t, perm)


def kernel(x, meta_tokens, norm1_g, w_in, conv_dw_k, conv_dw_b, conv_ln_g, conv_ln_b, pool_w, pool_scale, w_out, norm2_g, w_up, ffn_dw_k, w_down, final_g, loss_target, m_meta_tokens, m_norm1_g, m_w_in, m_conv_dw_k, m_conv_dw_b, m_conv_ln_g, m_conv_ln_b, m_pool_w, m_pool_scale, m_w_out, m_norm2_g, m_w_up, m_ffn_dw_k, m_w_down, m_final_g, v_meta_tokens, v_norm1_g, v_w_in, v_conv_dw_k, v_conv_dw_b, v_conv_ln_g, v_conv_ln_b, v_pool_w, v_pool_scale, v_w_out, v_norm2_g, v_w_up, v_ffn_dw_k, v_w_down, v_final_g):
    depth, D = norm1_g.shape
    n_meta = meta_tokens.shape[0]
    seq = x.shape[1]
    L = n_meta + seq
    cc = conv_dw_b.shape[1]
    ng, gd = pool_w.shape[1], pool_w.shape[2]
    f = w_down.shape[1] * N_DEV

    def rows(g):
        return g.reshape(-1, g.shape[-1])

    b16 = lambda a: a.astype(BF16)
    tr = lambda a: jnp.swapaxes(a, -1, -2)
    w_in_t, m_w_in_t, v_w_in_t = tr(w_in), tr(m_w_in), tr(v_w_in)
    w_up_t, m_w_up_t, v_w_up_t = tr(w_up), tr(m_w_up), tr(v_w_up)
    (g_in0, g_ck, g_kf, g_meta) = _exchange([b16(w_in_t[0]), conv_dw_k, ffn_dw_k, meta_tokens], ["gather"] * 4,
                                            "gather_first")
    ck_full = _cols_from_shards(g_ck)
    ck_rows = jnp.broadcast_to(ck_full[:, :, None, :], ck_full.shape[:2] + (SUBLANES, cc))
    kf_full = _cols_from_shards(g_kf)
    meta_full = _cols_from_shards(g_meta)
    am = _head_matrix(cc)
    win, wout, wup, wdown = [None] * depth, [None] * depth, [None] * depth, [None] * depth
    win[0] = rows(g_in0)

    h = (meta_full, x[0])
    saved = []
    for l in range(depth):
        more = l + 1 < depth
        if l == 0:
            (z, hn1), (g_out, g_down) = _norm_proj(h, norm1_g[l:l + 1], win[l], f"in_proj_{l}", tn_cap=1536,
                                                   xchg=([b16(w_out[l]), b16(w_down[l])], ["gather"] * 2))
            wout[l], wdown[l] = rows(g_out), rows(g_down)
            (ymix, u1), (g_up,) = _mixer_fwd(z, ck_rows[l], conv_dw_b[l:l + 1], conv_ln_g[l:l + 1], conv_ln_b[l:l + 1],
                                             pool_w[l], pool_scale[l:l + 1], am, f"mixer_fwd_{l}",
                                             xchg=([b16(w_up_t[l])], ["gather"]))
            wup[l] = rows(g_up)
        else:
            z, hn1 = _norm_proj(h, norm1_g[l:l + 1], win[l], f"in_proj_{l}", tn_cap=1536)
            ymix, u1 = _mixer_fwd(z, ck_rows[l], conv_dw_b[l:l + 1], conv_ln_g[l:l + 1], conv_ln_b[l:l + 1], pool_w[l],
                                  pool_scale[l:l + 1], am, f"mixer_fwd_{l}")
        if more:
            h_mid, (g_in,) = _mm(ymix, wout[l], f"out_proj_{l}", res=h, tn_cap=1024, xchg=([b16(w_in_t[l + 1])], ["gather"]))
            win[l + 1] = rows(g_in)
            nxt = [b16(w_out[l + 1]), b16(w_up_t[l + 1]), b16(w_down[l + 1])]
            (h_out, hn2, act, ux, uc), got = _ffn_block_fwd(h_mid, norm2_g[l:l + 1], wup[l], kf_full[l], wdown[l],
                                                            f"ffn_fwd_{l}", xchg=(nxt, ["gather"] * 3))
            wout[l + 1], wup[l + 1], wdown[l + 1] = rows(got[0]), rows(got[1]), rows(got[2])
        else:
            h_mid = _mm(ymix, wout[l], f"out_proj_{l}", res=h, tn_cap=1024)
            h_out, hn2, act, ux, uc = _ffn_block_fwd(h_mid, norm2_g[l:l + 1], wup[l], kf_full[l], wdown[l], f"ffn_fwd_{l}")
        saved.append((h, hn1, z, u1, ymix, h_mid, hn2, ux, uc, act))
        h = h_out

    dh, d_final_g, loss_part = _loss_head(h, final_g.reshape(1, D), loss_target[0], n_meta, "loss_head")

    def row_shards(gm):
        return gm.reshape(N_DEV, -1, gm.shape[-1])

    zero_row = jnp.zeros((1, D), F32)
    gw = {k: [None] * depth for k in ("ck", "cb", "lg", "lb", "pw", "ps", "kf", "n1", "n2")}
    parts = {k: [None] * depth for k in ("in", "out", "up", "down")}
    for l in reversed(range(depth)):
        h_in, hn1, z, u1, ymix, h_mid, hn2, ux, uc, act = saved[l]
        g_down = _mm_tn(act, dh, f"down_proj_wgrad_{l}", tq_cap=512)
        (dh_mid, gw["n2"][l], dug0, dkf), (parts["down"][l],) = _ffn_block_bwd(
            dh, h_mid, norm2_g[l:l + 1], ux, uc, kf_full[l], wdown[l], wup[l], f"ffn_bwd_{l}",
            xchg=([row_shards(g_down)], ["a2a"]))
        gw["kf"][l] = jnp.transpose(dkf, (2, 1, 0, 3)).reshape(dkf.shape[2], -1)
        g_up_t = _mm_tn(dug0, hn2, f"up_proj_wgrad_{l}", halves=2, tq_cap=1024)
        dymix = _mm(dh_mid, wout[l], f"out_proj_bwd_{l}", b_t=True, tn_cap=1024)
        g_out = _mm_tn(ymix, dh_mid, f"out_proj_wgrad_{l}", tq_cap=512)
        ((dz, gw["ck"][l], gw["cb"][l], gw["lg"][l], gw["lb"][l], gw["pw"][l], gw["ps"][l]),
         (parts["up"][l], parts["out"][l])) = _mixer_bwd(
            z, u1, dymix, ck_rows[l], conv_ln_g[l:l + 1], conv_ln_b[l:l + 1], pool_w[l], pool_scale[l:l + 1], am,
            f"mixer_bwd_{l}", xchg=([row_shards(g_up_t), row_shards(g_out)], ["a2a", "a2a"]))
        g_in_t = _mm_tn(dz, hn1, f"in_proj_wgrad_{l}", tq_cap=1024)
        if l > 0:
            (dh, gw["n1"][l]), (parts["in"][l],) = _proj_bwd_norm(dz, win[l], h_in, norm1_g[l:l + 1], dh_mid, zero_row,
                                                                  f"in_proj_bwd_{l}", xchg=([row_shards(g_in_t)], ["a2a"]))
        else:
            meta_rows, x_rows = h_in
            (grad_x, dg_x), (parts["in"][l],) = _proj_bwd_norm(dz, win[l], x_rows, norm1_g[l:l + 1], dh_mid, zero_row,
                                                               f"in_proj_bwd_{l}", skip=n_meta,
                                                               xchg=([row_shards(g_in_t)], ["a2a"]))
            d_meta, gw["n1"][l] = _proj_bwd_norm(dz[:n_meta], win[l], meta_rows, norm1_g[l:l + 1], dh_mid[:n_meta],
                                                 dg_x, f"in_proj_bwd_meta_{l}")
    grad_x = grad_x[None]

    pack_d = jnp.concatenate(gw["n1"] + gw["n2"] + [d_final_g, jnp.broadcast_to(loss_part[:, :1], (1, D)), zero_row, zero_row], axis=0)
    pack_c = jnp.concatenate(gw["cb"] + gw["lg"] + gw["lb"] + gw["ps"], axis=0)
    pack_pw = jnp.stack(gw["pw"]).reshape(depth * ng * gd, gd)
    src = [_cols_to_shards(jnp.stack(gw["ck"])), _cols_to_shards(jnp.stack(gw["kf"])), _cols_to_shards(d_meta),
           pack_d, pack_c, pack_pw]
    r_ck, r_kf, r_meta, r_d, r_c, r_pw = _exchange(src, ["a2a"] * 3 + ["gather"] * 3, "exchange_small_grads")

    big = {
        "w_in": tuple(tr(a) for a in _adamw_big(parts["in"], w_in_t, m_w_in_t, v_w_in_t, "adamw_w_in")),
        "w_out": _adamw_big(parts["out"], w_out, m_w_out, v_w_out, "adamw_w_out"),
        "w_up": tuple(tr(a) for a in _adamw_big(parts["up"], w_up_t, m_w_up_t, v_w_up_t, "adamw_w_up")),
        "w_down": _adamw_big(parts["down"], w_down, m_w_down, v_w_down, "adamw_w_down"),
    }
    kwid = conv_dw_k.shape[1]
    fkw = ffn_dw_k.shape[1]
    row = lambda a: a.reshape(1, -1)
    entries = [
        (r_d, 0, norm1_g, m_norm1_g, v_norm1_g),
        (r_d, depth, norm2_g, m_norm2_g, v_norm2_g),
        (r_d, 2 * depth, row(final_g), row(m_final_g), row(v_final_g)),
        (r_c, 0, conv_dw_b, m_conv_dw_b, v_conv_dw_b),
        (r_c, depth, conv_ln_g, m_conv_ln_g, v_conv_ln_g),
        (r_c, 2 * depth, conv_ln_b, m_conv_ln_b, v_conv_ln_b),
        (r_c, 3 * depth, pool_scale, m_pool_scale, v_pool_scale),
        (r_pw, 0, pool_w.reshape(-1, gd), m_pool_w.reshape(-1, gd), v_pool_w.reshape(-1, gd)),
        (r_ck.reshape(N_DEV, depth * kwid, -1), 0, conv_dw_k.reshape(depth * kwid, -1),
         m_conv_dw_k.reshape(depth * kwid, -1), v_conv_dw_k.reshape(depth * kwid, -1)),
        (r_kf.reshape(N_DEV, depth * fkw, -1), 0, ffn_dw_k.reshape(depth * fkw, -1),
         m_ffn_dw_k.reshape(depth * fkw, -1), v_ffn_dw_k.reshape(depth * fkw, -1)),
        (r_meta, 0, meta_tokens, m_meta_tokens, v_meta_tokens),
        (r_d, 2 * depth + 1, zero_row, zero_row, zero_row),
    ]
    small = _adamw_small(entries, "adamw_small")
    names = ["norm1_g", "norm2_g", "final_g", "conv_dw_b", "conv_ln_g", "conv_ln_b", "pool_scale", "pool_w",
             "conv_dw_k", "ffn_dw_k", "meta_tokens"]
    shapes = {"final_g": final_g.shape, "pool_w": pool_w.shape, "conv_dw_k": conv_dw_k.shape, "ffn_dw_k": ffn_dw_k.shape}
    res = dict(big)
    for nme, quad in zip(names, small[:-1]):
        res[nme] = tuple(a.reshape(shapes[nme]) if nme in shapes else a for a in quad)
    loss = small[-1][0][0, 0]

    order = ["meta_tokens", "norm1_g", "w_in", "conv_dw_k", "conv_dw_b", "conv_ln_g", "conv_ln_b", "pool_w", "pool_scale",
             "w_out", "norm2_g", "w_up", "ffn_dw_k", "w_down", "final_g"]
    return (loss, grad_x, *[res[k][0] for k in order], *[res[k][1] for k in order], *[res[k][2] for k in order],
            *[res[k][3] for k in order])
```

```python
import functools

import jax
import jax.numpy as jnp
from jax import lax
from jax.experimental import pallas as pl
from jax.experimental.pallas import tpu as pltpu

F32 = jnp.float32
BF16 = jnp.bfloat16

EPS = 1e-6
HEAD_DIM = 64
POOL_WINDOWS = (2, 4, 8, 16)
ADAM_LR = 0.001
ADAM_B1 = 0.9
ADAM_B2 = 0.999
ADAM_EPS = 1e-08
ADAM_WD = 0.01
ADAM_STEP = 10

N_DEV = 8
OTHER_CHIPS = (2, 4, 6)
SUBLANES = 8
HALO = 48
CONV_PAD = 32
POOL_PAD = 16
FFN_PAD = 8
ROW_CHUNK = 24
CONV3_ROWS = 48
MAX_TILE_ROWS = 1024
WGRAD_TILE_ROWS = 2816
VMEM_LIMIT = 52 * 1024 * 1024


def _divisor(n, cap, mult):
    best = None
    for d in range(mult, min(n, cap) + 1, mult):
        if n % d == 0:
            best = d
    return n if best is None else best


def _token_tile(L):
    return _divisor(L, MAX_TILE_ROWS, HALO)


def _stat_rows(tl):
    return _divisor(tl, 512, SUBLANES)


def _params(sem=None):
    return pltpu.CompilerParams(dimension_semantics=sem, vmem_limit_bytes=VMEM_LIMIT)


def _rowsum8(x):
    acc = x[0:SUBLANES]
    for k in range(1, x.shape[0] // SUBLANES):
        acc = acc + x[k * SUBLANES:(k + 1) * SUBLANES]
    return acc


def _sigmoid(x):
    return jax.nn.sigmoid(x)


def _dot_nt(a, b):
    return lax.dot_general(a, b, (((1,), (1,)), ((), ())), preferred_element_type=F32)


def _head_mean(x, am_ref):
    bw = am_ref.shape[0]
    am = am_ref[...]
    outs = []
    for blk in range(x.shape[1] // bw):
        xb = x[:, blk * bw:(blk + 1) * bw]
        hi = xb.astype(BF16)
        lo = (xb - hi.astype(F32)).astype(BF16)
        outs.append(jnp.dot(hi, am, preferred_element_type=F32) + jnp.dot(lo, am, preferred_element_type=F32))
    return outs[0] if len(outs) == 1 else jnp.concatenate(outs, axis=-1)


def _xchg_out_shapes(srcs, modes):
    out = []
    for s, m in zip(srcs, modes):
        shp = ((N_DEV,) + tuple(s.shape)) if m == "gather" else tuple(s.shape)
        out.append(jax.ShapeDtypeStruct(shp, s.dtype))
    return out


def _xchg_sems(n):
    return [pltpu.SemaphoreType.DMA((n, N_DEV - 1)), pltpu.SemaphoreType.DMA((n, N_DEV - 1)), pltpu.SemaphoreType.DMA((n,))]


def _xchg_ops(src_refs, out_refs, sems, modes):
    n = len(src_refs)
    send_sems, recv_sems, local_sems = sems
    x, y, c = lax.axis_index("x"), lax.axis_index("y"), lax.axis_index("c")
    me = 4 * x + 2 * y + c

    def peer(d):
        return (x ^ ((d >> 2) & 1), y ^ ((d >> 1) & 1), c ^ (d & 1))

    def peer_id(d):
        px, py, pc = peer(d)
        return 4 * px + 2 * py + pc

    def remote(t, d):
        src = src_refs[t] if modes[t] == "gather" else src_refs[t].at[peer_id(d)]
        return pltpu.make_async_remote_copy(
            src_ref=src, dst_ref=out_refs[t].at[me], send_sem=send_sems.at[t, d - 1], recv_sem=recv_sems.at[t, d - 1],
            device_id=peer(d), device_id_type=pl.DeviceIdType.MESH)

    def arrival(t, d):
        src = src_refs[t] if modes[t] == "gather" else src_refs[t].at[me]
        return pltpu.make_async_remote_copy(
            src_ref=src, dst_ref=out_refs[t].at[peer_id(d)], send_sem=send_sems.at[t, d - 1],
            recv_sem=recv_sems.at[t, d - 1], device_id=peer(d), device_id_type=pl.DeviceIdType.MESH)

    def passed_on(t, d):
        blk = out_refs[t].at[peer_id(d)]
        return pltpu.make_async_remote_copy(
            src_ref=blk, dst_ref=blk, send_sem=send_sems.at[t, d], recv_sem=recv_sems.at[t, d],
            device_id=peer(1), device_id_type=pl.DeviceIdType.MESH)

    def local(t):
        src = src_refs[t] if modes[t] == "gather" else src_refs[t].at[me]
        return pltpu.make_async_copy(src, out_refs[t].at[me], local_sems.at[t])

    def sent_first(t):
        return OTHER_CHIPS + (1,) if modes[t] == "gather" else tuple(range(1, N_DEV))

    def start():
        for t in range(n):
            local(t).start()
        for t in range(n):
            for d in sent_first(t):
                remote(t, d).start()

    gathered = [t for t in range(n) if modes[t] == "gather"]

    def relay():
        for t in gathered:
            for d in OTHER_CHIPS:
                arrival(t, d).wait_recv()
                passed_on(t, d).start()

    def wait():
        for t in range(n):
            for d in range(1, N_DEV):
                if not (modes[t] == "gather" and d in OTHER_CHIPS):
                    arrival(t, d).wait_recv()
        for t in range(n):
            for d in sent_first(t):
                remote(t, d).wait_send()
        for t in gathered:
            for d in OTHER_CHIPS:
                passed_on(t, d).wait_send()
        for t in range(n):
            local(t).wait()

    return start, relay, wait


def _exchange(srcs, modes, name):
    n = len(srcs)

    def body(*refs):
        start, relay, wait = _xchg_ops(refs[:n], refs[n:2 * n], refs[2 * n:], modes)
        start()
        relay()
        wait()

    any_spec = pl.BlockSpec(memory_space=pl.ANY)
    return pl.pallas_call(
        body, name=name, out_shape=tuple(_xchg_out_shapes(srcs, modes)),
        in_specs=[any_spec] * n, out_specs=tuple([any_spec] * n),
        scratch_shapes=_xchg_sems(n),
        compiler_params=pltpu.CompilerParams(has_side_effects=True),
    )(*srcs)


def _call(body, *, name, grid, in_specs, out_specs, out_shape, args, scratch_shapes=(), sem=None, xchg=None):
    single = not isinstance(out_shape, (tuple, list))
    outs_shape = [out_shape] if single else list(out_shape)
    outs_spec = [out_specs] if single else list(out_specs)
    if xchg is None:
        res = pl.pallas_call(
            body, name=name, grid=grid, in_specs=list(in_specs), out_specs=out_specs, out_shape=out_shape,
            scratch_shapes=list(scratch_shapes), compiler_params=_params(sem))(*args)
        return res, ()
    srcs, modes = xchg
    n_in, n_out, n_scr, nx = len(in_specs), len(outs_shape), len(scratch_shapes), len(srcs)
    n_steps = functools.reduce(lambda a, b: a * b, grid, 1)
    relay_step = (3 * n_steps) // 4 if n_steps > 1 else 0

    def wrapped(*refs):
        ins = refs[:n_in]
        xs = refs[n_in:n_in + nx]
        o0 = n_in + nx
        outs = refs[o0:o0 + n_out]
        xo = refs[o0 + n_out:o0 + n_out + nx]
        s0 = o0 + n_out + nx
        scr = refs[s0:s0 + n_scr]
        start, relay, wait = _xchg_ops(xs, xo, refs[s0 + n_scr:], modes)
        step = functools.reduce(lambda acc, a: acc * grid[a] + pl.program_id(a), range(len(grid)), 0)

        @pl.when(step == 0)
        def _():
            start()

        body(*ins, *outs, *scr)

        @pl.when(step == relay_step)
        def _():
            relay()

        @pl.when(step == n_steps - 1)
        def _():
            wait()

    any_spec = pl.BlockSpec(memory_space=pl.ANY)
    res = pl.pallas_call(
        wrapped, name=name, grid=grid, in_specs=list(in_specs) + [any_spec] * nx,
        out_specs=tuple(outs_spec + [any_spec] * nx), out_shape=tuple(outs_shape + _xchg_out_shapes(srcs, modes)),
        scratch_shapes=list(scratch_shapes) + _xchg_sems(nx),
        compiler_params=_params(("arbitrary",) * len(grid)))(*args, *srcs)
    comp = res[:n_out]
    return (comp[0] if single else tuple(comp)), tuple(res[n_out:])


def _seq_rows(h):
    if isinstance(h, tuple):
        return h[0].shape[0] + h[1].shape[0], h[1].shape[1]
    return h.shape


def _seq_tiles(h, tm):
    if not isinstance(h, tuple):
        return [pl.BlockSpec((tm, h.shape[1]), lambda i, *_: (i, 0))], [h], lambda refs, i: refs[0][...]
    meta, x = h
    n, D = meta.shape

    def read(refs, i):
        t = refs[1][...]
        first = jnp.concatenate([refs[0][...], pltpu.roll(t, n, axis=0)[n:]], axis=0)
        return jnp.where(i == 0, first, t)

    window = pl.BlockSpec((pl.Element(tm), pl.Element(D)),
                          lambda i, *_: (pl.multiple_of(jnp.maximum(i * tm - n, 0), SUBLANES), 0))
    return [pl.BlockSpec((n, D), lambda *_: (0, 0)), window], [meta, x], read


def _norm_proj(h, g, w, name, *, tn_cap, xchg=None):
    L, D = _seq_rows(h)
    N = w.shape[0]
    tm = _token_tile(L)
    tn = _divisor(N, tn_cap, 128)
    h_specs, h_args, read_h = _seq_tiles(h, tm)
    nh = len(h_specs)

    def body(*refs):
        g_ref, w_ref, z_ref, hn_ref = refs[nh:]

        @pl.when(pl.program_id(1) == 0)
        def _():
            x = read_h(refs[:nh], pl.program_id(0))
            r = lax.rsqrt(jnp.mean(x * x, axis=-1, keepdims=True) + EPS)
            hn_ref[...] = ((x * r) * g_ref[...]).astype(BF16)

        z_ref[...] = _dot_nt(hn_ref[...], w_ref[...])

    out, xo = _call(
        body, name=name, grid=(L // tm, N // tn),
        in_specs=h_specs + [pl.BlockSpec((1, D), lambda i, j: (0, 0)), pl.BlockSpec((tn, D), lambda i, j: (j, 0))],
        out_specs=(pl.BlockSpec((tm, tn), lambda i, j: (i, j)), pl.BlockSpec((tm, D), lambda i, j: (i, 0))),
        out_shape=(jax.ShapeDtypeStruct((L, N), F32), jax.ShapeDtypeStruct((L, D), BF16)),
        sem=("parallel", "arbitrary"), args=(*h_args, g, w), xchg=xchg)
    return out if xchg is None else (out, xo)


def _proj_bwd_norm(a, b, h, g, dres, dg0, name, skip=0, xchg=None):
    L, K = a.shape
    D = b.shape[1]
    rows = L - skip
    tm = _divisor(rows, MAX_TILE_ROWS, 2 * SUBLANES) if skip else _token_tile(L)

    def body(a_ref, b_ref, h_ref, g_ref, dres_ref, dg0_ref, dh_ref, dg_ref):
        i = pl.program_id(0)
        dhn = jnp.dot(a_ref[...], b_ref[...], preferred_element_type=F32)
        x = h_ref[...]
        r = lax.rsqrt(jnp.mean(x * x, axis=-1, keepdims=True) + EPS)
        xhat = x * r
        dxhat = dhn * g_ref[...]
        dh_ref[...] = dres_ref[...] + r * (dxhat - xhat * jnp.mean(dxhat * xhat, axis=-1, keepdims=True))
        part = jnp.sum(_rowsum8(dhn * xhat), axis=0, keepdims=True)

        @pl.when(i == 0)
        def _():
            dg_ref[...] = dg0_ref[...] + part

        @pl.when(i > 0)
        def _():
            dg_ref[...] += part

    def rows_of(cols, first=skip):
        if not first:
            return pl.BlockSpec((tm, cols), lambda i: (i, 0))
        return pl.BlockSpec((pl.Element(tm), pl.Element(cols)), lambda i: (pl.multiple_of(first + i * tm, SUBLANES), 0))

    row = pl.BlockSpec((1, D), lambda i: (0, 0))
    h_first = skip if h.shape[0] == L else 0
    out, xo = _call(
        body, name=name, grid=(rows // tm,),
        in_specs=[rows_of(K), pl.BlockSpec((K, D), lambda i: (0, 0)), rows_of(D, h_first), row, rows_of(D), row],
        out_specs=(pl.BlockSpec((tm, D), lambda i: (i, 0)), row),
        out_shape=(jax.ShapeDtypeStruct((rows, D), F32), jax.ShapeDtypeStruct((1, D), F32)),
        sem=("arbitrary",), args=(a, b, h, g, dres, dg0), xchg=xchg)
    return out if xchg is None else (out, xo)


def _mm(a, b, name, *, res=None, b_t=False, tn_cap=1408, xchg=None):
    M, K = a.shape
    N = b.shape[0] if b_t else b.shape[1]
    tm = _token_tile(M)
    tn = _divisor(N, tn_cap, 128)
    if isinstance(res, tuple):
        assert tn == N
        r_specs, r_args, read_r = _seq_tiles(res, tm)
    elif res is not None:
        r_specs, r_args, read_r = [pl.BlockSpec((tm, tn), lambda i, j: (i, j))], [res], lambda refs, i: refs[0][...]
    else:
        r_specs, r_args, read_r = [], [], None

    def body(*refs):
        a_ref, b_ref, o_ref = refs[0], refs[1], refs[-1]
        av = a_ref[...].astype(BF16)
        prod = _dot_nt(av, b_ref[...]) if b_t else jnp.dot(av, b_ref[...], preferred_element_type=F32)
        o_ref[...] = prod if read_r is None else prod + read_r(refs[2:-1], pl.program_id(0))

    b_spec = pl.BlockSpec((tn, K), lambda i, j: (j, 0)) if b_t else pl.BlockSpec((K, tn), lambda i, j: (0, j))
    in_specs = [pl.BlockSpec((tm, K), lambda i, j: (i, 0)), b_spec] + r_specs
    args = [a, b] + r_args
    out, xo = _call(
        body, name=name, grid=(M // tm, N // tn), in_specs=in_specs,
        out_specs=pl.BlockSpec((tm, tn), lambda i, j: (i, j)), out_shape=jax.ShapeDtypeStruct((M, N), F32),
        sem=("parallel", "parallel"), args=args, xchg=xchg)
    return out if xchg is None else (out, xo)


def _mm_tn(a, b, name, *, halves=1, tq_cap=1408):
    L, Q = b.shape
    ph = a.shape[-1]
    P = ph * halves
    tl = _divisor(L, WGRAD_TILE_ROWS, HALO)
    tp = _divisor(ph, 1408, 128)
    tq = _divisor(Q, tq_cap, 128)
    pper = ph // tp
    nl = L // tl
    grid = (P // tp, Q // tq, nl)

    def body(a_ref, b_ref, o_ref, acc):
        prod = lax.dot_general(a_ref[...].astype(BF16), b_ref[...].astype(BF16), (((0,), (0,)), ((), ())),
                               preferred_element_type=F32)
        l = pl.program_id(2)
        if nl == 1:
            o_ref[...] = prod.astype(BF16)
            return

        @pl.when(l == 0)
        def _():
            acc[...] = prod

        @pl.when(jnp.logical_and(l > 0, l < nl - 1))
        def _():
            acc[...] += prod

        @pl.when(l == nl - 1)
        def _():
            o_ref[...] = (acc[...] + prod).astype(BF16)

    if halves > 1:
        a_spec = pl.BlockSpec((None, tl, tp), lambda p, q, l: (p // pper, l, p % pper))
    else:
        a_spec = pl.BlockSpec((tl, tp), lambda p, q, l: (l, p))
    return pl.pallas_call(
        body, name=name, grid=grid,
        in_specs=[a_spec, pl.BlockSpec((tl, tq), lambda p, q, l: (l, q))],
        out_specs=pl.BlockSpec((tp, tq), lambda p, q, l: (p, q)),
        out_shape=jax.ShapeDtypeStruct((P, Q), BF16),
        scratch_shapes=[pltpu.VMEM((tp, tq), F32)],
        compiler_params=_params(("parallel", "parallel", "arbitrary")),
    )(a, b)


def _loss_head(h, g, tgt, n_meta, name):
    L, D = h.shape
    tl = _token_tile(L)
    nt = L // tl

    def body(h_ref, g_ref, t_ref, dh_ref, dg_ref, loss_ref):
        i = pl.program_id(0)
        x = h_ref[...]
        r = lax.rsqrt(jnp.mean(x * x, axis=-1, keepdims=True) + EPS)
        xhat = x * r
        gg = g_ref[...]
        y = xhat * gg
        rows = i * tl + lax.broadcasted_iota(jnp.int32, (tl, 1), 0)
        t = t_ref[...]
        t = jnp.where(i == 0, pltpu.roll(t, n_meta, axis=0), t)
        err = jnp.where(rows >= n_meta, y - t, 0.0)
        dy = err * (1.0 / D)
        dxhat = dy * gg
        dh_ref[...] = r * (dxhat - xhat * jnp.mean(dxhat * xhat, axis=-1, keepdims=True))
        dg_part = jnp.sum(_rowsum8(dy * xhat), axis=0, keepdims=True)
        per_row = jnp.mean(err * err, axis=-1, keepdims=True)
        loss_part = jnp.broadcast_to(0.5 * jnp.sum(per_row, axis=0, keepdims=True), (1, 128))

        @pl.when(i == 0)
        def _():
            dg_ref[...] = dg_part
            loss_ref[...] = loss_part

        @pl.when(i > 0)
        def _():
            dg_ref[...] += dg_part
            loss_ref[...] += loss_part

    tile = pl.BlockSpec((tl, D), lambda i: (i, 0))
    row = pl.BlockSpec((1, D), lambda i: (0, 0))
    window = pl.BlockSpec((pl.Element(tl), pl.Element(D)),
                          lambda i: (pl.multiple_of(jnp.maximum(i * tl - n_meta, 0), SUBLANES), 0))
    return pl.pallas_call(
        body, name=name, grid=(nt,), in_specs=[tile, row, window],
        out_specs=(tile, row, pl.BlockSpec((1, 128), lambda i: (0, 0))),
        out_shape=(jax.ShapeDtypeStruct((L, D), F32), jax.ShapeDtypeStruct((1, D), F32),
                   jax.ShapeDtypeStruct((1, 128), F32)),
        compiler_params=_params(("arbitrary",)),
    )(h, g, tgt)


def _pool_fwd_block(pwin, pw_ref, row0, rb, g, gd, w, t0):
    wv = pwin[pl.ds(row0 + HALO - POOL_PAD, rb + POOL_PAD), g * gd:(g + 1) * gd]
    s = wv
    sh = 1
    while sh < w:
        s = s + pltpu.roll(s, sh, axis=0)
        sh *= 2
    win = s[POOL_PAD:POOL_PAD + rb]
    pt = wv[POOL_PAD:POOL_PAD + rb]
    tg = t0 + lax.broadcasted_iota(jnp.int32, (rb, 1), 0)
    cnt = jnp.minimum(tg + 1, w).astype(F32)
    return win / cnt - pt


def _fill_windows(i, zp_ref, zc_ref, u0w, pwin, tl, cc):
    keep = i > 0
    zp = zp_ref[...]
    u0w[0:HALO, :] = jnp.where(keep, zp[:, :cc] * _sigmoid(zp[:, cc:2 * cc]), 0.0)
    pwin[0:HALO, :] = jnp.where(keep, zp[:, 2 * cc:], 0.0)

    def fill(c, carry):
        b = pl.multiple_of(c * ROW_CHUNK, SUBLANES)
        zc = zc_ref[pl.ds(b, ROW_CHUNK), :]
        u0w[pl.ds(HALO + b, ROW_CHUNK), :] = zc[:, :cc] * _sigmoid(zc[:, cc:2 * cc])
        pwin[pl.ds(HALO + b, ROW_CHUNK), :] = zc[:, 2 * cc:]
        return carry

    lax.fori_loop(0, tl // ROW_CHUNK, fill, 0)


def _mixer_fwd(z, ck, cb, lg, lb, pw, ps, am, name, xchg=None):
    L, ci = z.shape
    kw, _, cc = ck.shape
    cp = ci - 2 * cc
    ng, gd = pw.shape[0], pw.shape[1]
    tl = _token_tile(L)
    nt = L // tl
    hb = tl // HALO
    rb = _stat_rows(tl)
    tap0 = CONV_PAD - (kw - 1)

    def body(zp_ref, zc_ref, ck_ref, cb_ref, lg_ref, lb_ref, pw_ref, ps_ref, am_ref, y_ref, u1_ref, u0w, pwin):
        i = pl.program_id(0)
        _fill_windows(i, zp_ref, zc_ref, u0w, pwin, tl, cc)

        def conv(c, carry):
            b = pl.multiple_of(c * ROW_CHUNK, SUBLANES)
            w = u0w[pl.ds(b + HALO - CONV_PAD, ROW_CHUNK + CONV_PAD), :]
            acc = jnp.broadcast_to(cb_ref[...], (ROW_CHUNK, cc))
            for j in range(kw):
                acc = acc + _rows_of(ck_ref[j], ROW_CHUNK) * w[tap0 + j:tap0 + j + ROW_CHUNK]
            u1_ref[pl.ds(b, ROW_CHUNK), :] = acc
            return carry

        lax.fori_loop(0, tl // ROW_CHUNK, conv, 0)

        def blocks(k, carry):
            b = pl.multiple_of(k * rb, SUBLANES)
            u1 = u1_ref[pl.ds(b, rb), :]
            xc = u1 - _head_mean(u1, am_ref)
            var = _head_mean(xc * xc, am_ref)
            u2 = (xc * lax.rsqrt(var + EPS)) * lg_ref[...] + lb_ref[...]
            y_ref[pl.ds(b, rb), 0:cc] = (u2 * _sigmoid(u2)).astype(y_ref.dtype)
            for g in range(ng):
                d = _pool_fwd_block(pwin, pw_ref, b, rb, g, gd, POOL_WINDOWS[g], i * tl + b)
                yp = jnp.dot(d.astype(BF16), pw_ref[g].astype(BF16), preferred_element_type=F32)
                yp = yp * ps_ref[:, g * gd:(g + 1) * gd]
                y_ref[pl.ds(b, rb), cc + g * gd:cc + (g + 1) * gd] = yp.astype(y_ref.dtype)
            return carry

        lax.fori_loop(0, tl // rb, blocks, 0)

    def full(a):
        nd = a.ndim
        return pl.BlockSpec(a.shape, lambda i: (0,) * nd)

    out, xo = _call(
        body, name=name, grid=(nt,),
        in_specs=[pl.BlockSpec((HALO, ci), lambda i: (jnp.maximum(i * hb - 1, 0), 0)),
                  pl.BlockSpec((tl, ci), lambda i: (i, 0)),
                  full(ck), full(cb), full(lg), full(lb), full(pw), full(ps), full(am)],
        out_specs=(pl.BlockSpec((tl, cc + cp), lambda i: (i, 0)), pl.BlockSpec((tl, cc), lambda i: (i, 0))),
        out_shape=(jax.ShapeDtypeStruct((L, cc + cp), BF16), jax.ShapeDtypeStruct((L, cc), F32)),
        scratch_shapes=[pltpu.VMEM((HALO + tl, cc), F32), pltpu.VMEM((HALO + tl, cp), F32)],
        sem=("parallel",), args=(z, z, ck, cb, lg, lb, pw, ps, am), xchg=xchg)
    return out if xchg is None else (out, xo)


def _mixer_bwd(z, u1, dy, ck, lg, lb, pw, ps, am, name, xchg=None):
    L, ci = z.shape
    kw, _, cc = ck.shape
    cp = ci - 2 * cc
    ng, gd = pw.shape[0], pw.shape[1]
    tl = _token_tile(L)
    nt = L // tl
    hb = tl // HALO
    rb = _stat_rows(tl)

    def body(zp_ref, zc_ref, u1c_ref, u1n_ref, dyc_ref, dyn_ref, ck_ref, lg_ref, lb_ref, pw_ref, ps_ref, am_ref,
             dz_ref, dck_ref, dcb_ref, dlg_ref, dlb_ref, dpw_ref, dps_ref,
             u0w, pwin, du1w, ddw, ew, dkacc, dcb8, dlg8, dlb8, dps8):
        i = pl.program_id(0)
        has_next = i < nt - 1

        @pl.when(i == 0)
        def _():
            dck_ref[...] = jnp.zeros_like(dck_ref)
            dcb_ref[...] = jnp.zeros_like(dcb_ref)
            dlg_ref[...] = jnp.zeros_like(dlg_ref)
            dlb_ref[...] = jnp.zeros_like(dlb_ref)
            dpw_ref[...] = jnp.zeros_like(dpw_ref)
            dps_ref[...] = jnp.zeros_like(dps_ref)

        dkacc[...] = jnp.zeros_like(dkacc)
        dcb8[...] = jnp.zeros_like(dcb8)
        dlg8[...] = jnp.zeros_like(dlg8)
        dlb8[...] = jnp.zeros_like(dlb8)
        dps8[...] = jnp.zeros_like(dps8)

        _fill_windows(i, zp_ref, zc_ref, u0w, pwin, tl, cc)

        def conv_side(u1, dyc, own):
            xc = u1 - _head_mean(u1, am_ref)
            rstd = lax.rsqrt(_head_mean(xc * xc, am_ref) + EPS)
            uh = xc * rstd
            lgv = lg_ref[...]
            u2 = uh * lgv + lb_ref[...]
            sg = _sigmoid(u2)
            du2 = dyc * (sg * (1.0 + u2 * (1.0 - sg)))
            if own:
                dlg8[...] += _rowsum8(du2 * uh)
                dlb8[...] += _rowsum8(du2)
            duh = du2 * lgv
            return rstd * (duh - _head_mean(duh, am_ref) - uh * _head_mean(duh * uh, am_ref))

        def pool_side(dyp, t0, rows):
            dds, es = [], []
            tg = t0 + lax.broadcasted_iota(jnp.int32, (rows, 1), 0)
            for g in range(ng):
                dypre = dyp[:, g * gd:(g + 1) * gd] * ps_ref[:, g * gd:(g + 1) * gd]
                dd = lax.dot_general(dypre.astype(BF16), pw_ref[g].astype(BF16), (((1,), (1,)), ((), ())),
                                     preferred_element_type=F32)
                cnt = jnp.minimum(tg + 1, POOL_WINDOWS[g]).astype(F32)
                dds.append(dd)
                es.append(dd / cnt)
            return jnp.concatenate(dds, axis=-1), jnp.concatenate(es, axis=-1)

        def blocks(k, carry):
            b = pl.multiple_of(k * rb, SUBLANES)
            dyb = dyc_ref[pl.ds(b, rb), :]
            du1 = conv_side(u1c_ref[pl.ds(b, rb), :], dyb[:, :cc], True)
            du1w[pl.ds(b, rb), :] = du1
            dcb8[...] += _rowsum8(du1)
            dyp = dyb[:, cc:]
            dd, e = pool_side(dyp, i * tl + b, rb)
            ddw[pl.ds(b, rb), :] = dd
            ew[pl.ds(b, rb), :] = e
            for g in range(ng):
                d = _pool_fwd_block(pwin, pw_ref, b, rb, g, gd, POOL_WINDOWS[g], i * tl + b)
                db16 = d.astype(BF16)
                dypg = dyp[:, g * gd:(g + 1) * gd]
                ypre = jnp.dot(db16, pw_ref[g].astype(BF16), preferred_element_type=F32)
                dps8[:, g * gd:(g + 1) * gd] += _rowsum8(dypg * ypre)
                dypre = (dypg * ps_ref[:, g * gd:(g + 1) * gd]).astype(BF16)
                dpw_ref[g] += lax.dot_general(db16, dypre, (((0,), (0,)), ((), ())), preferred_element_type=F32)
            return carry

        lax.fori_loop(0, tl // rb, blocks, 0)

        dyn = dyn_ref[...]
        du1n = conv_side(u1n_ref[...], dyn[:, :cc], False)
        du1w[tl:tl + HALO, :] = jnp.where(has_next, du1n, 0.0)
        ddn, en = pool_side(dyn[:, cc:], (i + 1) * tl, HALO)
        ew[tl:tl + HALO, :] = jnp.where(has_next, en, 0.0)

        def taps(c, carry):
            b = pl.multiple_of(c * ROW_CHUNK, SUBLANES)
            w = du1w[pl.ds(b, ROW_CHUNK + CONV_PAD), :]
            u0c = u0w[pl.ds(HALO + b, ROW_CHUNK), :]
            acc = jnp.zeros((ROW_CHUNK, cc), F32)
            for j in range(kw):
                o = kw - 1 - j
                sh = w[o:o + ROW_CHUNK]
                acc = acc + _rows_of(ck_ref[j], ROW_CHUNK) * sh
                dkacc[j] += _rowsum8(u0c * sh)
            zc = zc_ref[pl.ds(b, ROW_CHUNK), :]
            a = zc[:, :cc]
            sg = _sigmoid(zc[:, cc:2 * cc])
            dz_ref[pl.ds(b, ROW_CHUNK), 0:cc] = (acc * sg).astype(dz_ref.dtype)
            dz_ref[pl.ds(b, ROW_CHUNK), cc:2 * cc] = (acc * a * sg * (1.0 - sg)).astype(dz_ref.dtype)
            return carry

        lax.fori_loop(0, tl // ROW_CHUNK, taps, 0)

        def pool_back(k, carry):
            b = pl.multiple_of(k * rb, SUBLANES)
            n = rb + POOL_PAD
            for g in range(ng):
                s = ew[pl.ds(b, n), g * gd:(g + 1) * gd]
                sh = 1
                while sh < POOL_WINDOWS[g]:
                    s = s + pltpu.roll(s, n - sh, axis=0)
                    sh *= 2
                dp = s[0:rb] - ddw[pl.ds(b, rb), g * gd:(g + 1) * gd]
                dz_ref[pl.ds(b, rb), 2 * cc + g * gd:2 * cc + (g + 1) * gd] = dp.astype(dz_ref.dtype)
            return carry

        lax.fori_loop(0, tl // rb, pool_back, 0)

        dck_ref[...] += jnp.sum(dkacc[...], axis=1)
        dcb_ref[...] += jnp.sum(dcb8[...], axis=0, keepdims=True)
        dlg_ref[...] += jnp.sum(dlg8[...], axis=0, keepdims=True)
        dlb_ref[...] += jnp.sum(dlb8[...], axis=0, keepdims=True)
        dps_ref[...] += jnp.sum(dps8[...], axis=0, keepdims=True)

    def full(a):
        nd = a.ndim
        return pl.BlockSpec(a.shape, lambda i: (0,) * nd)

    nhb = L // HALO

    def prev_map(i):
        return (jnp.maximum(i * hb - 1, 0), 0)

    def next_map(i):
        return (jnp.minimum((i + 1) * hb, nhb - 1), 0)

    dcc = cc + cp
    row_cc = jax.ShapeDtypeStruct((1, cc), F32)
    out_shape = (jax.ShapeDtypeStruct((L, ci), BF16), jax.ShapeDtypeStruct((kw, cc), F32), row_cc, row_cc, row_cc,
                 jax.ShapeDtypeStruct((ng, gd, gd), F32), jax.ShapeDtypeStruct((1, cp), F32))
    acc_spec = [pl.BlockSpec((kw, cc), lambda i: (0, 0))] + [pl.BlockSpec((1, cc), lambda i: (0, 0))] * 3 + [
        pl.BlockSpec((ng, gd, gd), lambda i: (0, 0, 0)), pl.BlockSpec((1, cp), lambda i: (0, 0))]
    out, xo = _call(
        body, name=name, grid=(nt,),
        in_specs=[pl.BlockSpec((HALO, ci), prev_map), pl.BlockSpec((tl, ci), lambda i: (i, 0)),
                  pl.BlockSpec((tl, cc), lambda i: (i, 0)), pl.BlockSpec((HALO, cc), next_map),
                  pl.BlockSpec((tl, dcc), lambda i: (i, 0)), pl.BlockSpec((HALO, dcc), next_map),
                  full(ck), full(lg), full(lb), full(pw), full(ps), full(am)],
        out_specs=tuple([pl.BlockSpec((tl, ci), lambda i: (i, 0))] + acc_spec),
        out_shape=out_shape,
        scratch_shapes=[pltpu.VMEM((HALO + tl, cc), F32), pltpu.VMEM((HALO + tl, cp), F32),
                        pltpu.VMEM((tl + HALO, cc), F32), pltpu.VMEM((tl, cp), F32), pltpu.VMEM((tl + HALO, cp), F32),
                        pltpu.VMEM((kw, SUBLANES, cc), F32), pltpu.VMEM((SUBLANES, cc), F32),
                        pltpu.VMEM((SUBLANES, cc), F32), pltpu.VMEM((SUBLANES, cc), F32), pltpu.VMEM((SUBLANES, cp), F32)],
        sem=("arbitrary",), args=(z, z, u1, u1, dy, dy, ck, lg, lb, pw, ps, am), xchg=xchg)
    return out if xchg is None else (out, xo)


def _row_parts(nc, n=3):
    n = min(n, nc)
    cuts = [round(k * nc / n) for k in range(n + 1)]
    return [(cuts[k], cuts[k + 1]) for k in range(n)]


def _tap_rows(k_ref):
    return [jnp.broadcast_to(k_ref[j:j + 1, :], (SUBLANES, k_ref.shape[1])) for j in range(k_ref.shape[0])]


def _rows_of(tap, n):
    return tap if n == SUBLANES else jnp.concatenate([tap] * (n // SUBLANES), axis=0)


def _ffn_conv(win, taps, rows):
    kw = len(taps)
    o = FFN_PAD - (kw - 1)
    acc = _rows_of(taps[0], rows) * win[o:o + rows]
    for j in range(1, kw):
        acc = acc + _rows_of(taps[j], rows) * win[o + j:o + j + rows]
    return acc


def _ffn_block_fwd(h_mid, g, wup_t, kf, wdown, name, xchg=None):
    L, D = h_mid.shape
    f = wdown.shape[0]
    kw = kf.shape[0]
    tl = _token_tile(L)
    tc = _divisor(f, 256, 128)
    nj = f // tc
    nt = L // tl
    pad = 2 * SUBLANES
    hb = tl // pad
    rc = CONV3_ROWS
    parts = _row_parts(tl // rc)

    def body(hp_ref, hc_ref, g_ref, wg_ref, wv_ref, kg_ref, kv_ref, wd_ref, out_ref, hn_ref, act_ref, ux_ref, uc_ref,
             hn_halo, halo, ug_ref, acc):
        i = pl.program_id(0)
        kb = pl.program_id(1)

        @pl.when(kb == 0)
        def _():
            gg = g_ref[...]

            def norm(x):
                r = lax.rsqrt(jnp.mean(x * x, axis=-1, keepdims=True) + EPS)
                return ((x * r) * gg).astype(BF16)

            hn_halo[...] = jnp.where(i > 0, norm(hp_ref[...]), jnp.zeros((pad, D), BF16))
            hn_ref[...] = norm(hc_ref[...])
            acc[...] = jnp.zeros_like(acc)

        w_refs = (wg_ref, wv_ref)
        taps = (_tap_rows(kg_ref), _tap_rows(kv_ref))
        hh = hn_halo[...]
        for h in range(2):
            halo[h] = _dot_nt(hh, w_refs[h][...])[pad - FFN_PAD:]

        def up_part(lo, hi):
            a, b = lo * rc, hi * rc
            for h in range(2):
                ug_ref[h, a:b, :] = _dot_nt(hn_ref[a:b, :], w_refs[h][...])

        def down_part(lo, hi):
            a, b = lo * rc, hi * rc
            acc[a:b, :] += jnp.dot(act_ref[a:b, :], wd_ref[...], preferred_element_type=F32)

        def chunk_rows(lo, hi):
            for c in range(lo, hi):
                r0 = c * rc
                convd = []
                for h in range(2):
                    if c == 0:
                        win = jnp.concatenate([halo[h], ug_ref[h, 0:rc]], axis=0)
                    else:
                        win = ug_ref[h, r0 - FFN_PAD:r0 + rc]
                    convd.append(_ffn_conv(win, taps[h], rc))
                    ux_ref[h, r0:r0 + rc, :] = win[FFN_PAD:].astype(BF16)
                    uc_ref[h, r0:r0 + rc, :] = convd[h].astype(BF16)
                gate, val = convd
                act_ref[r0:r0 + rc, :] = ((gate * _sigmoid(gate)) * val).astype(BF16)

        for p, (lo, hi) in enumerate(parts):
            if p == 0:
                up_part(lo, hi)
            if p + 1 < len(parts):
                up_part(*parts[p + 1])
            if p > 0:
                down_part(*parts[p - 1])
            chunk_rows(lo, hi)
        down_part(*parts[-1])

        @pl.when(kb == nj - 1)
        def _():
            out_ref[...] = acc[...] + hc_ref[...]

    out, xo = _call(
        body, name=name, grid=(nt, nj),
        in_specs=[pl.BlockSpec((pad, D), lambda i, k: (jnp.maximum(i * hb - 1, 0), 0)),
                  pl.BlockSpec((tl, D), lambda i, k: (i, 0)),
                  pl.BlockSpec((1, D), lambda i, k: (0, 0)),
                  pl.BlockSpec((tc, D), lambda i, k: (k, 0)), pl.BlockSpec((tc, D), lambda i, k: (k + nj, 0)),
                  pl.BlockSpec((kw, tc), lambda i, k: (0, k)), pl.BlockSpec((kw, tc), lambda i, k: (0, k + nj)),
                  pl.BlockSpec((tc, D), lambda i, k: (k, 0))],
        out_specs=(pl.BlockSpec((tl, D), lambda i, k: (i, 0)), pl.BlockSpec((tl, D), lambda i, k: (i, 0)),
                   pl.BlockSpec((tl, tc), lambda i, k: (i, k)),
                   pl.BlockSpec((2, tl, tc), lambda i, k: (0, i, k)), pl.BlockSpec((2, tl, tc), lambda i, k: (0, i, k))),
        out_shape=(jax.ShapeDtypeStruct((L, D), F32), jax.ShapeDtypeStruct((L, D), BF16),
                   jax.ShapeDtypeStruct((L, f), BF16),
                   jax.ShapeDtypeStruct((2, L, f), BF16), jax.ShapeDtypeStruct((2, L, f), BF16)),
        scratch_shapes=[pltpu.VMEM((pad, D), BF16), pltpu.VMEM((2, FFN_PAD, tc), F32), pltpu.VMEM((2, tl, tc), F32),
                        pltpu.VMEM((tl, D), F32)],
        sem=("parallel", "arbitrary"), args=(h_mid, h_mid, g, wup_t, wup_t, kf, kf, wdown), xchg=xchg)
    return out if xchg is None else (out, xo)


def _ffn_block_bwd(dh, h_mid, g, ux, uc, kf, wdown, wup_t, name, xchg=None):
    L, D = dh.shape
    f = ux.shape[2]
    kw = kf.shape[0]
    tl = _token_tile(L)
    tc = _divisor(f, 256, 128)
    nj = f // tc
    nt = L // tl
    pad = 2 * SUBLANES
    rc = CONV3_ROWS
    nc = tl // rc
    parts = _row_parts(nc)

    def body(dhc_ref, dhn_ref, hm_ref, g_ref, xg_ref, xv_ref, cg_ref, cgn_ref, cv_ref, cvn_ref, kg_ref, kv_ref,
             wd_ref, wg_ref, wv_ref, dhm_ref, dg_ref, du_ref, dk_ref, dh_ext, dact_s, acc):
        i = pl.program_id(0)
        kb = pl.program_id(1)

        @pl.when(kb == 0)
        def _():
            dh_ext[0:tl, :] = dhc_ref[...].astype(BF16)
            dh_ext[tl:tl + pad, :] = dhn_ref[...].astype(BF16)
            acc[...] = jnp.zeros_like(acc)

        @pl.when(jnp.logical_and(i == 0, kb == 0))
        def _():
            dg_ref[...] = jnp.zeros_like(dg_ref)
            dk_ref[...] = jnp.zeros_like(dk_ref)

        x_refs, c_refs, nxt = (xg_ref, xv_ref), (cg_ref, cv_ref), (cgn_ref, cvn_ref)
        taps = (_tap_rows(kg_ref), _tap_rows(kv_ref))
        dk = [[jnp.zeros((SUBLANES, tc), F32) for _ in range(kw)] for _ in range(2)]

        def dact_part(lo, hi):
            a, b = lo * rc, hi * rc + pad
            dact_s[a:b, :] = _dot_nt(dh_ext[a:b, :], wd_ref[...])

        def dhn_part(lo, hi):
            a, b = lo * rc, hi * rc
            acc[a:b, :] += (jnp.dot(du_ref[0, a:b, :], wg_ref[...], preferred_element_type=F32)
                            + jnp.dot(du_ref[1, a:b, :], wv_ref[...], preferred_element_type=F32))

        for p, (lo, hi) in enumerate(parts):
            if p == 0:
                dact_part(lo, hi)
            if p + 1 < len(parts):
                dact_part(*parts[p + 1])
            if p > 0:
                dhn_part(*parts[p - 1])
            chunk_rows(lo, hi, x_refs, c_refs, nxt, taps, dk, i, dact_s, du_ref)
        dhn_part(*parts[-1])
        for h in range(2):
            for j in range(kw):
                dk_ref[kb, h, j:j + 1, :] += jnp.sum(dk[h][j], axis=0, keepdims=True)

        @pl.when(kb == nj - 1)
        def _():
            x = hm_ref[...]
            r = lax.rsqrt(jnp.mean(x * x, axis=-1, keepdims=True) + EPS)
            xhat = x * r
            dhn = acc[...]
            dxhat = dhn * g_ref[...]
            dhm_ref[...] = dhc_ref[...] + r * (dxhat - xhat * jnp.mean(dxhat * xhat, axis=-1, keepdims=True))
            dg_ref[...] += jnp.sum(_rowsum8(dhn * xhat), axis=0, keepdims=True)

    def chunk_rows(lo, hi, x_refs, c_refs, nxt, taps, dk, i, dact_s, du_ref):
        for c in range(lo, hi):
            r0 = c * rc
            n = rc + FFN_PAD
            convd = []
            for h in range(2):
                if c == nc - 1:
                    rows = jnp.concatenate([c_refs[h][r0:r0 + rc, :], nxt[h][...]], axis=0)
                else:
                    rows = c_refs[h][r0:r0 + rc + pad, :]
                convd.append(rows.astype(F32)[0:n])
            gate, val = convd
            xs = [x_refs[h][r0:r0 + rc, :].astype(F32) for h in range(2)]
            dact = dact_s[r0:r0 + n, :]
            sg = _sigmoid(gate)
            dcs = [dact * val * (sg * (1.0 + gate * (1.0 - sg))), dact * (gate * sg)]
            if c == nc - 1:
                live = jnp.logical_or(lax.broadcasted_iota(jnp.int32, (n, 1), 0) < rc, i < nt - 1)
                dcs = [jnp.where(live, d, 0.0) for d in dcs]
            for h in range(2):
                xc = xs[h]
                dx = None
                for j in range(kw):
                    o = kw - 1 - j
                    sh = dcs[h][o:o + rc]
                    term = _rows_of(taps[h][j], rc) * sh
                    dx = term if dx is None else dx + term
                    dk[h][j] = dk[h][j] + _rowsum8(xc * sh)
                du_ref[h, r0:r0 + rc, :] = dx.astype(BF16)

    def after(i):
        return jnp.minimum((i + 1) * (tl // pad), L // pad - 1)

    def half(h, rows, idx):
        return pl.BlockSpec((None, rows, tc), lambda i, k: (h,) + idx(i, k))

    def tile(i, k):
        return (i, k)

    def behind(i, k):
        return (after(i), k)

    out, xo = _call(
        body, name=name, grid=(nt, nj),
        in_specs=[pl.BlockSpec((tl, D), lambda i, k: (i, 0)),
                  pl.BlockSpec((pad, D), lambda i, k: (after(i), 0)),
                  pl.BlockSpec((tl, D), lambda i, k: (i, 0)), pl.BlockSpec((1, D), lambda i, k: (0, 0)),
                  half(0, tl, tile), half(1, tl, tile),
                  half(0, tl, tile), half(0, pad, behind), half(1, tl, tile), half(1, pad, behind),
                  pl.BlockSpec((kw, tc), lambda i, k: (0, k)), pl.BlockSpec((kw, tc), lambda i, k: (0, k + nj)),
                  pl.BlockSpec((tc, D), lambda i, k: (k, 0)),
                  pl.BlockSpec((tc, D), lambda i, k: (k, 0)), pl.BlockSpec((tc, D), lambda i, k: (k + nj, 0))],
        out_specs=(pl.BlockSpec((tl, D), lambda i, k: (i, 0)), pl.BlockSpec((1, D), lambda i, k: (0, 0)),
                   pl.BlockSpec((2, tl, tc), lambda i, k: (0, i, k)),
                   pl.BlockSpec((nj, 2, kw, tc), lambda i, k: (0, 0, 0, 0))),
        out_shape=(jax.ShapeDtypeStruct((L, D), F32), jax.ShapeDtypeStruct((1, D), F32),
                   jax.ShapeDtypeStruct((2, L, f), BF16), jax.ShapeDtypeStruct((nj, 2, kw, tc), F32)),
        scratch_shapes=[pltpu.VMEM((tl + pad, D), BF16), pltpu.VMEM((tl + pad, tc), F32), pltpu.VMEM((tl, D), F32)],
        sem=("arbitrary", "arbitrary"), args=(dh, dh, h_mid, g, ux, ux, uc, uc, uc, uc, kf, kf, wdown, wup_t, wup_t),
        xchg=xchg)
    return out if xchg is None else (out, xo)


def _adamw_math(w, g, m, v):
    m = ADAM_B1 * m + (1.0 - ADAM_B1) * g
    v = ADAM_B2 * v + (1.0 - ADAM_B2) * (g * g)
    m_hat = m / (1.0 - ADAM_B1 ** ADAM_STEP)
    v_hat = v / (1.0 - ADAM_B2 ** ADAM_STEP)
    delta = -ADAM_LR * (m_hat / (jnp.sqrt(v_hat) + ADAM_EPS) + ADAM_WD * w)
    return delta, m, v


def _sum_parts(parts_ref, idx):
    g = parts_ref[(0,) + idx].astype(F32)
    for q in range(1, N_DEV):
        g = g + parts_ref[(q,) + idx].astype(F32)
    return g


def _adamw_big(parts, w, m, v, name):
    nl, R, C = w.shape
    tr = _divisor(R, 256, 2 * SUBLANES)

    def body(*refs):
        p_refs = refs[:nl]
        w_ref, m_ref, v_ref, g_ref, d_ref, nm_ref, nv_ref = refs[nl:]
        layer = pl.program_id(0)
        for k in range(nl):
            @pl.when(layer == k)
            def _(k=k):
                g = _sum_parts(p_refs[k], ())
                d, nm, nv = _adamw_math(w_ref[0], g, m_ref[0], v_ref[0])
                g_ref[0] = g
                d_ref[0] = d
                nm_ref[0] = nm
                nv_ref[0] = nv

    def part_spec(k):
        return pl.BlockSpec((N_DEV, tr, C), lambda l, r: (0, jnp.where(l == k, r, 0), 0))

    blk = pl.BlockSpec((1, tr, C), lambda l, r: (l, r, 0))
    shp = jax.ShapeDtypeStruct((nl, R, C), F32)
    return pl.pallas_call(
        body, name=name, grid=(nl, R // tr),
        in_specs=[part_spec(k) for k in range(nl)] + [blk, blk, blk],
        out_specs=(blk, blk, blk, blk), out_shape=(shp, shp, shp, shp),
        compiler_params=_params(("arbitrary", "arbitrary")),
    )(*parts, w, m, v)


def _adamw_small(entries, name):
    n = len(entries)
    uniq = []
    for e in entries:
        if not any(e[0] is u for u in uniq):
            uniq.append(e[0])
    pidx = [next(k for k, u in enumerate(uniq) if u is e[0]) for e in entries]
    npart = len(uniq)

    def body(*refs):
        p_refs = refs[:npart]
        wmv = refs[npart:npart + 3 * n]
        outs = refs[npart + 3 * n:]
        for t, e in enumerate(entries):
            lo, w = e[1], e[2]
            rows = w.shape[0]
            pr = p_refs[pidx[t]]
            g = pr[0, lo:lo + rows].astype(F32)
            for q in range(1, N_DEV):
                g = g + pr[q, lo:lo + rows].astype(F32)
            d, nm, nv = _adamw_math(wmv[3 * t][...], g, wmv[3 * t + 1][...], wmv[3 * t + 2][...])
            outs[4 * t][...] = g
            outs[4 * t + 1][...] = d
            outs[4 * t + 2][...] = nm
            outs[4 * t + 3][...] = nv

    vm = pl.BlockSpec(memory_space=pltpu.VMEM)
    args = list(uniq)
    out_shape = []
    for e in entries:
        args += [e[2], e[3], e[4]]
        out_shape += [jax.ShapeDtypeStruct(e[2].shape, F32)] * 4
    res = pl.pallas_call(
        body, name=name, in_specs=[vm] * len(args), out_specs=tuple([vm] * len(out_shape)),
        out_shape=tuple(out_shape), compiler_params=_params(),
    )(*args)
    return [tuple(res[4 * t:4 * t + 4]) for t in range(n)]


def _head_matrix(cc):
    bw = min(256, cc)
    r = lax.broadcasted_iota(jnp.int32, (bw, bw), 0) // HEAD_DIM
    c = lax.broadcasted_iota(jnp.int32, (bw, bw), 1) // HEAD_DIM
    return jnp.where(r == c, 1.0 / HEAD_DIM, 0.0).astype(BF16)


def _cols_from_shards(g):
    nd = g.ndim
    perm = tuple(range(1, nd - 1)) + (0, nd - 1)
    t = jnp.transpose(g, perm)
    return t.reshape(t.shape[:-2] + (t.shape[-2] * t.shape[-1],))


def _cols_to_shards(a):
    nd = a.ndim
    t = a.reshape(a.shape[:-1] + (N_DEV, a.shape[-1] // N_DEV))
    perm = (nd - 1,) + tuple(range(nd - 1)) + (nd,)
    return jnp.transpose(t, perm)


def kernel(x, meta_tokens, norm1_g, w_in, conv_dw_k, conv_dw_b, conv_ln_g, conv_ln_b, pool_w, pool_scale, w_out, norm2_g, w_up, ffn_dw_k, w_down, final_g, loss_target, m_meta_tokens, m_norm1_g, m_w_in, m_conv_dw_k, m_conv_dw_b, m_conv_ln_g, m_conv_ln_b, m_pool_w, m_pool_scale, m_w_out, m_norm2_g, m_w_up, m_ffn_dw_k, m_w_down, m_final_g, v_meta_tokens, v_norm1_g, v_w_in, v_conv_dw_k, v_conv_dw_b, v_conv_ln_g, v_conv_ln_b, v_pool_w, v_pool_scale, v_w_out, v_norm2_g, v_w_up, v_ffn_dw_k, v_w_down, v_final_g):
    depth, D = norm1_g.shape
    n_meta = meta_tokens.shape[0]
    seq = x.shape[1]
    L = n_meta + seq
    cc = conv_dw_b.shape[1]
    ng, gd = pool_w.shape[1], pool_w.shape[2]
    f = w_down.shape[1] * N_DEV

    def rows(g):
        return g.reshape(-1, g.shape[-1])

    b16 = lambda a: a.astype(BF16)
    tr = lambda a: jnp.swapaxes(a, -1, -2)
    w_in_t, m_w_in_t, v_w_in_t = tr(w_in), tr(m_w_in), tr(v_w_in)
    w_up_t, m_w_up_t, v_w_up_t = tr(w_up), tr(m_w_up), tr(v_w_up)
    (g_in0, g_ck, g_kf, g_meta) = _exchange([b16(w_in_t[0]), conv_dw_k, ffn_dw_k, meta_tokens], ["gather"] * 4,
                                            "gather_first")
    ck_full = _cols_from_shards(g_ck)
    ck_rows = jnp.broadcast_to(ck_full[:, :, None, :], ck_full.shape[:2] + (SUBLANES, cc))
    kf_full = _cols_from_shards(g_kf)
    meta_full = _cols_from_shards(g_meta)
    am = _head_matrix(cc)
    win, wout, wup, wdown = [None] * depth, [None] * depth, [None] * depth, [None] * depth
    win[0] = rows(g_in0)

    h = (meta_full, x[0])
    saved = []
    for l in range(depth):
        more = l + 1 < depth
        if l == 0:
            (z, hn1), (g_out,) = _norm_proj(h, norm1_g[l:l + 1], win[l], f"in_proj_{l}", tn_cap=1536,
                                            xchg=([b16(w_out[l])], ["gather"]))
            wout[l] = rows(g_out)
            (ymix, u1), (g_up, g_down) = _mixer_fwd(z, ck_rows[l], conv_dw_b[l:l + 1], conv_ln_g[l:l + 1], conv_ln_b[l:l + 1],
                                                    pool_w[l], pool_scale[l:l + 1], am, f"mixer_fwd_{l}",
                                                    xchg=([b16(w_up_t[l]), b16(w_down[l])], ["gather"] * 2))
            wup[l], wdown[l] = rows(g_up), rows(g_down)
        else:
            z, hn1 = _norm_proj(h, norm1_g[l:l + 1], win[l], f"in_proj_{l}", tn_cap=1536)
            ymix, u1 = _mixer_fwd(z, ck_rows[l], conv_dw_b[l:l + 1], conv_ln_g[l:l + 1], conv_ln_b[l:l + 1], pool_w[l],
                                  pool_scale[l:l + 1], am, f"mixer_fwd_{l}")
        if more:
            h_mid, (g_in, g_out) = _mm(ymix, wout[l], f"out_proj_{l}", res=h, tn_cap=1024,
                                       xchg=([b16(w_in_t[l + 1]), b16(w_out[l + 1])], ["gather"] * 2))
            win[l + 1], wout[l + 1] = rows(g_in), rows(g_out)
            nxt = [b16(w_up_t[l + 1]), b16(w_down[l + 1])]
            (h_out, hn2, act, ux, uc), got = _ffn_block_fwd(h_mid, norm2_g[l:l + 1], wup[l], kf_full[l], wdown[l],
                                                            f"ffn_fwd_{l}", xchg=(nxt, ["gather"] * 2))
            wup[l + 1], wdown[l + 1] = rows(got[0]), rows(got[1])
        else:
            h_mid = _mm(ymix, wout[l], f"out_proj_{l}", res=h, tn_cap=1024)
            h_out, hn2, act, ux, uc = _ffn_block_fwd(h_mid, norm2_g[l:l + 1], wup[l], kf_full[l], wdown[l], f"ffn_fwd_{l}")
        saved.append((h, hn1, z, u1, ymix, h_mid, hn2, ux, uc, act))
        h = h_out

    dh, d_final_g, loss_part = _loss_head(h, final_g.reshape(1, D), loss_target[0], n_meta, "loss_head")

    def row_shards(gm):
        return gm.reshape(N_DEV, -1, gm.shape[-1])

    zero_row = jnp.zeros((1, D), F32)
    gw = {k: [None] * depth for k in ("ck", "cb", "lg", "lb", "pw", "ps", "kf", "n1", "n2")}
    parts = {k: [None] * depth for k in ("in", "out", "up", "down")}
    for l in reversed(range(depth)):
        h_in, hn1, z, u1, ymix, h_mid, hn2, ux, uc, act = saved[l]
        g_down = _mm_tn(act, dh, f"down_proj_wgrad_{l}", tq_cap=512)
        (dh_mid, gw["n2"][l], dug0, dkf), (parts["down"][l],) = _ffn_block_bwd(
            dh, h_mid, norm2_g[l:l + 1], ux, uc, kf_full[l], wdown[l], wup[l], f"ffn_bwd_{l}",
            xchg=([row_shards(g_down)], ["a2a"]))
        gw["kf"][l] = jnp.transpose(dkf, (2, 1, 0, 3)).reshape(dkf.shape[2], -1)
        g_up_t = _mm_tn(dug0, hn2, f"up_proj_wgrad_{l}", halves=2, tq_cap=1024)
        dymix = _mm(dh_mid, wout[l], f"out_proj_bwd_{l}", b_t=True, tn_cap=1024)
        g_out = _mm_tn(ymix, dh_mid, f"out_proj_wgrad_{l}", tq_cap=512)
        ((dz, gw["ck"][l], gw["cb"][l], gw["lg"][l], gw["lb"][l], gw["pw"][l], gw["ps"][l]),
         (parts["up"][l], parts["out"][l])) = _mixer_bwd(
            z, u1, dymix, ck_rows[l], conv_ln_g[l:l + 1], conv_ln_b[l:l + 1], pool_w[l], pool_scale[l:l + 1], am,
            f"mixer_bwd_{l}", xchg=([row_shards(g_up_t), row_shards(g_out)], ["a2a", "a2a"]))
        g_in_t = _mm_tn(dz, hn1, f"in_proj_wgrad_{l}", tq_cap=1024)
        if l > 0:
            (dh, gw["n1"][l]), (parts["in"][l],) = _proj_bwd_norm(dz, win[l], h_in, norm1_g[l:l + 1], dh_mid, zero_row,
                                                                  f"in_proj_bwd_{l}", xchg=([row_shards(g_in_t)], ["a2a"]))
        else:
            meta_rows, x_rows = h_in
            (grad_x, dg_x), (parts["in"][l],) = _proj_bwd_norm(dz, win[l], x_rows, norm1_g[l:l + 1], dh_mid, zero_row,
                                                               f"in_proj_bwd_{l}", skip=n_meta,
                                                               xchg=([row_shards(g_in_t)], ["a2a"]))
            d_meta, gw["n1"][l] = _proj_bwd_norm(dz[:n_meta], win[l], meta_rows, norm1_g[l:l + 1], dh_mid[:n_meta],
                                                 dg_x, f"in_proj_bwd_meta_{l}")
    grad_x = grad_x[None]

    pack_d = jnp.concatenate(gw["n1"] + gw["n2"] + [d_final_g, jnp.broadcast_to(loss_part[:, :1], (1, D)), zero_row, zero_row], axis=0)
    pack_c = jnp.concatenate(gw["cb"] + gw["lg"] + gw["lb"] + gw["ps"], axis=0)
    pack_pw = b16(jnp.stack(gw["pw"]).reshape(depth * ng * gd, gd))
    src = [_cols_to_shards(jnp.stack(gw["ck"])), _cols_to_shards(jnp.stack(gw["kf"])), _cols_to_shards(d_meta),
           pack_d, pack_c, pack_pw]
    r_ck, r_kf, r_meta, r_d, r_c, r_pw = _exchange(src, ["a2a"] * 3 + ["gather"] * 3, "exchange_small_grads")

    big = {
        "w_in": tuple(tr(a) for a in _adamw_big(parts["in"], w_in_t, m_w_in_t, v_w_in_t, "adamw_w_in")),
        "w_out": _adamw_big(parts["out"], w_out, m_w_out, v_w_out, "adamw_w_out"),
        "w_up": tuple(tr(a) for a in _adamw_big(parts["up"], w_up_t, m_w_up_t, v_w_up_t, "adamw_w_up")),
        "w_down": _adamw_big(parts["down"], w_down, m_w_down, v_w_down, "adamw_w_down"),
    }
    kwid = conv_dw_k.shape[1]
    fkw = ffn_dw_k.shape[1]
    row = lambda a: a.reshape(1, -1)
    entries = [
        (r_d, 0, norm1_g, m_norm1_g, v_norm1_g),
        (r_d, depth, norm2_g, m_norm2_g, v_norm2_g),
        (r_d, 2 * depth, row(final_g), row(m_final_g), row(v_final_g)),
        (r_c, 0, conv_dw_b, m_conv_dw_b, v_conv_dw_b),
        (r_c, depth, conv_ln_g, m_conv_ln_g, v_conv_ln_g),
        (r_c, 2 * depth, conv_ln_b, m_conv_ln_b, v_conv_ln_b),
        (r_c, 3 * depth, pool_scale, m_pool_scale, v_pool_scale),
        (r_pw, 0, pool_w.reshape(-1, gd), m_pool_w.reshape(-1, gd), v_pool_w.reshape(-1, gd)),
        (r_ck.reshape(N_DEV, depth * kwid, -1), 0, conv_dw_k.reshape(depth * kwid, -1),
         m_conv_dw_k.reshape(depth * kwid, -1), v_conv_dw_k.reshape(depth * kwid, -1)),
        (r_kf.reshape(N_DEV, depth * fkw, -1), 0, ffn_dw_k.reshape(depth * fkw, -1),
         m_ffn_dw_k.reshape(depth * fkw, -1), v_ffn_dw_k.reshape(depth * fkw, -1)),
        (r_meta, 0, meta_tokens, m_meta_tokens, v_meta_tokens),
        (r_d, 2 * depth + 1, zero_row, zero_row, zero_row),
    ]
    small = _adamw_small(entries, "adamw_small")
    names = ["norm1_g", "norm2_g", "final_g", "conv_dw_b", "conv_ln_g", "conv_ln_b", "pool_scale", "pool_w",
             "conv_dw_k", "ffn_dw_k", "meta_tokens"]
    shapes = {"final_g": final_g.shape, "pool_w": pool_w.shape, "conv_dw_k": conv_dw_k.shape, "ffn_dw_k": ffn_dw_k.shape}
    res = dict(big)
    for nme, quad in zip(names, small[:-1]):
        res[nme] = tuple(a.reshape(shapes[nme]) if nme in shapes else a for a in quad)
    loss = small[-1][0][0, 0]

    order = ["meta_tokens", "norm1_g", "w_in", "conv_dw_k", "conv_dw_b", "conv_ln_g", "conv_ln_b", "pool_w", "pool_scale",
             "w_out", "norm2_g", "w_up", "ffn_dw_k", "w_down", "final_g"]
    return (loss, grad_x, *[res[k][0] for k in order], *[res[k][1] for k in order], *[res[k][2] for k in order],
            *[res[k][3] for k in order])
```

```python
import functools

import jax
import jax.numpy as jnp
from jax import lax
from jax.experimental import pallas as pl
from jax.experimental.pallas import tpu as pltpu

F32 = jnp.float32
BF16 = jnp.bfloat16

EPS = 1e-6
HEAD_DIM = 64
POOL_WINDOWS = (2, 4, 8, 16)
ADAM_LR = 0.001
ADAM_B1 = 0.9
ADAM_B2 = 0.999
ADAM_EPS = 1e-08
ADAM_WD = 0.01
ADAM_STEP = 10

N_DEV = 8
OTHER_CHIPS = (2, 4, 6)
SUBLANES = 8
HALO = 48
CONV_PAD = 32
POOL_PAD = 16
FFN_PAD = 8
ROW_CHUNK = 24
CONV3_ROWS = 48
MAX_TILE_ROWS = 1024
WGRAD_TILE_ROWS = 2816
VMEM_LIMIT = 52 * 1024 * 1024


def _divisor(n, cap, mult):
    best = None
    for d in range(mult, min(n, cap) + 1, mult):
        if n % d == 0:
            best = d
    return n if best is None else best


def _token_tile(L):
    return _divisor(L, MAX_TILE_ROWS, HALO)


def _stat_rows(tl):
    return _divisor(tl, 512, SUBLANES)


def _params(sem=None):
    return pltpu.CompilerParams(dimension_semantics=sem, vmem_limit_bytes=VMEM_LIMIT)


def _rowsum8(x):
    acc = x[0:SUBLANES]
    for k in range(1, x.shape[0] // SUBLANES):
        acc = acc + x[k * SUBLANES:(k + 1) * SUBLANES]
    return acc


def _sigmoid(x):
    return jax.nn.sigmoid(x)


def _dot_nt(a, b):
    return lax.dot_general(a, b, (((1,), (1,)), ((), ())), preferred_element_type=F32)


def _head_mean(x, am_ref):
    bw = am_ref.shape[0]
    am = am_ref[...]
    outs = []
    for blk in range(x.shape[1] // bw):
        xb = x[:, blk * bw:(blk + 1) * bw]
        hi = xb.astype(BF16)
        lo = (xb - hi.astype(F32)).astype(BF16)
        outs.append(jnp.dot(hi, am, preferred_element_type=F32) + jnp.dot(lo, am, preferred_element_type=F32))
    return outs[0] if len(outs) == 1 else jnp.concatenate(outs, axis=-1)


def _xchg_out_shapes(srcs, modes):
    out = []
    for s, m in zip(srcs, modes):
        shp = ((N_DEV,) + tuple(s.shape)) if m == "gather" else tuple(s.shape)
        out.append(jax.ShapeDtypeStruct(shp, s.dtype))
    return out


def _xchg_sems(n):
    return [pltpu.SemaphoreType.DMA((n, N_DEV - 1)), pltpu.SemaphoreType.DMA((n, N_DEV - 1)), pltpu.SemaphoreType.DMA((n,))]


def _xchg_ops(src_refs, out_refs, sems, modes):
    n = len(src_refs)
    send_sems, recv_sems, local_sems = sems
    x, y, c = lax.axis_index("x"), lax.axis_index("y"), lax.axis_index("c")
    me = 4 * x + 2 * y + c

    def peer(d):
        return (x ^ ((d >> 2) & 1), y ^ ((d >> 1) & 1), c ^ (d & 1))

    def peer_id(d):
        px, py, pc = peer(d)
        return 4 * px + 2 * py + pc

    def remote(t, d):
        src = src_refs[t] if modes[t] == "gather" else src_refs[t].at[peer_id(d)]
        return pltpu.make_async_remote_copy(
            src_ref=src, dst_ref=out_refs[t].at[me], send_sem=send_sems.at[t, d - 1], recv_sem=recv_sems.at[t, d - 1],
            device_id=peer(d), device_id_type=pl.DeviceIdType.MESH)

    def arrival(t, d):
        src = src_refs[t] if modes[t] == "gather" else src_refs[t].at[me]
        return pltpu.make_async_remote_copy(
            src_ref=src, dst_ref=out_refs[t].at[peer_id(d)], send_sem=send_sems.at[t, d - 1],
            recv_sem=recv_sems.at[t, d - 1], device_id=peer(d), device_id_type=pl.DeviceIdType.MESH)

    def passed_on(t, d):
        blk = out_refs[t].at[peer_id(d)]
        return pltpu.make_async_remote_copy(
            src_ref=blk, dst_ref=blk, send_sem=send_sems.at[t, d], recv_sem=recv_sems.at[t, d],
            device_id=peer(1), device_id_type=pl.DeviceIdType.MESH)

    def local(t):
        src = src_refs[t] if modes[t] == "gather" else src_refs[t].at[me]
        return pltpu.make_async_copy(src, out_refs[t].at[me], local_sems.at[t])

    def sent_first(t):
        return OTHER_CHIPS + (1,) if modes[t] == "gather" else tuple(range(1, N_DEV))

    def start():
        for t in range(n):
            local(t).start()
        for t in range(n):
            for d in sent_first(t):
                remote(t, d).start()

    gathered = [t for t in range(n) if modes[t] == "gather"]

    def relay():
        for t in gathered:
            for d in OTHER_CHIPS:
                arrival(t, d).wait_recv()
                passed_on(t, d).start()

    def wait():
        for t in range(n):
            for d in range(1, N_DEV):
                if not (modes[t] == "gather" and d in OTHER_CHIPS):
                    arrival(t, d).wait_recv()
        for t in range(n):
            for d in sent_first(t):
                remote(t, d).wait_send()
        for t in gathered:
            for d in OTHER_CHIPS:
                passed_on(t, d).wait_send()
        for t in range(n):
            local(t).wait()

    return start, relay, wait


def _exchange(srcs, modes, name):
    n = len(srcs)

    def body(*refs):
        start, relay, wait = _xchg_ops(refs[:n], refs[n:2 * n], refs[2 * n:], modes)
        start()
        relay()
        wait()

    any_spec = pl.BlockSpec(memory_space=pl.ANY)
    return pl.pallas_call(
        body, name=name, out_shape=tuple(_xchg_out_shapes(srcs, modes)),
        in_specs=[any_spec] * n, out_specs=tuple([any_spec] * n),
        scratch_shapes=_xchg_sems(n),
        compiler_params=pltpu.CompilerParams(has_side_effects=True),
    )(*srcs)


def _call(body, *, name, grid, in_specs, out_specs, out_shape, args, scratch_shapes=(), sem=None, xchg=None):
    single = not isinstance(out_shape, (tuple, list))
    outs_shape = [out_shape] if single else list(out_shape)
    outs_spec = [out_specs] if single else list(out_specs)
    if xchg is None:
        res = pl.pallas_call(
            body, name=name, grid=grid, in_specs=list(in_specs), out_specs=out_specs, out_shape=out_shape,
            scratch_shapes=list(scratch_shapes), compiler_params=_params(sem))(*args)
        return res, ()
    srcs, modes = xchg
    n_in, n_out, n_scr, nx = len(in_specs), len(outs_shape), len(scratch_shapes), len(srcs)
    n_steps = functools.reduce(lambda a, b: a * b, grid, 1)
    relay_step = (3 * n_steps) // 4 if n_steps > 1 else 0

    def wrapped(*refs):
        ins = refs[:n_in]
        xs = refs[n_in:n_in + nx]
        o0 = n_in + nx
        outs = refs[o0:o0 + n_out]
        xo = refs[o0 + n_out:o0 + n_out + nx]
        s0 = o0 + n_out + nx
        scr = refs[s0:s0 + n_scr]
        start, relay, wait = _xchg_ops(xs, xo, refs[s0 + n_scr:], modes)
        step = functools.reduce(lambda acc, a: acc * grid[a] + pl.program_id(a), range(len(grid)), 0)

        @pl.when(step == 0)
        def _():
            start()

        body(*ins, *outs, *scr)

        @pl.when(step == relay_step)
        def _():
            relay()

        @pl.when(step == n_steps - 1)
        def _():
            wait()

    any_spec = pl.BlockSpec(memory_space=pl.ANY)
    res = pl.pallas_call(
        wrapped, name=name, grid=grid, in_specs=list(in_specs) + [any_spec] * nx,
        out_specs=tuple(outs_spec + [any_spec] * nx), out_shape=tuple(outs_shape + _xchg_out_shapes(srcs, modes)),
        scratch_shapes=list(scratch_shapes) + _xchg_sems(nx),
        compiler_params=_params(("arbitrary",) * len(grid)))(*args, *srcs)
    comp = res[:n_out]
    return (comp[0] if single else tuple(comp)), tuple(res[n_out:])


def _seq_rows(h):
    if isinstance(h, tuple):
        return h[0].shape[0] + h[1].shape[0], h[1].shape[1]
    return h.shape


def _seq_tiles(h, tm):
    if not isinstance(h, tuple):
        return [pl.BlockSpec((tm, h.shape[1]), lambda i, *_: (i, 0))], [h], lambda refs, i: refs[0][...]
    meta, x = h
    n, D = meta.shape

    def read(refs, i):
        t = refs[1][...]
        first = jnp.concatenate([refs[0][...], pltpu.roll(t, n, axis=0)[n:]], axis=0)
        return jnp.where(i == 0, first, t)

    window = pl.BlockSpec((pl.Element(tm), pl.Element(D)),
                          lambda i, *_: (pl.multiple_of(jnp.maximum(i * tm - n, 0), SUBLANES), 0))
    return [pl.BlockSpec((n, D), lambda *_: (0, 0)), window], [meta, x], read


def _norm_proj(h, g, w, name, *, tn_cap, xchg=None):
    L, D = _seq_rows(h)
    N = w.shape[0]
    tm = _token_tile(L)
    tn = _divisor(N, tn_cap, 128)
    h_specs, h_args, read_h = _seq_tiles(h, tm)
    nh = len(h_specs)

    def body(*refs):
        g_ref, w_ref, z_ref, hn_ref = refs[nh:]

        @pl.when(pl.program_id(1) == 0)
        def _():
            x = read_h(refs[:nh], pl.program_id(0))
            r = lax.rsqrt(jnp.mean(x * x, axis=-1, keepdims=True) + EPS)
            hn_ref[...] = ((x * r) * g_ref[...]).astype(BF16)

        z_ref[...] = _dot_nt(hn_ref[...], w_ref[...])

    out, xo = _call(
        body, name=name, grid=(L // tm, N // tn),
        in_specs=h_specs + [pl.BlockSpec((1, D), lambda i, j: (0, 0)), pl.BlockSpec((tn, D), lambda i, j: (j, 0))],
        out_specs=(pl.BlockSpec((tm, tn), lambda i, j: (i, j)), pl.BlockSpec((tm, D), lambda i, j: (i, 0))),
        out_shape=(jax.ShapeDtypeStruct((L, N), F32), jax.ShapeDtypeStruct((L, D), BF16)),
        sem=("parallel", "arbitrary"), args=(*h_args, g, w), xchg=xchg)
    return out if xchg is None else (out, xo)


def _proj_bwd_norm(a, b, h, g, dres, dg0, name, skip=0, xchg=None):
    L, K = a.shape
    D = b.shape[1]
    rows = L - skip
    tm = _divisor(rows, MAX_TILE_ROWS, 2 * SUBLANES) if skip else _token_tile(L)

    def body(a_ref, b_ref, h_ref, g_ref, dres_ref, dg0_ref, dh_ref, dg_ref):
        i = pl.program_id(0)
        dhn = jnp.dot(a_ref[...], b_ref[...], preferred_element_type=F32)
        x = h_ref[...]
        r = lax.rsqrt(jnp.mean(x * x, axis=-1, keepdims=True) + EPS)
        xhat = x * r
        dxhat = dhn * g_ref[...]
        dh_ref[...] = dres_ref[...] + r * (dxhat - xhat * jnp.mean(dxhat * xhat, axis=-1, keepdims=True))
        part = jnp.sum(_rowsum8(dhn * xhat), axis=0, keepdims=True)

        @pl.when(i == 0)
        def _():
            dg_ref[...] = dg0_ref[...] + part

        @pl.when(i > 0)
        def _():
            dg_ref[...] += part

    def rows_of(cols, first=skip):
        if not first:
            return pl.BlockSpec((tm, cols), lambda i: (i, 0))
        return pl.BlockSpec((pl.Element(tm), pl.Element(cols)), lambda i: (pl.multiple_of(first + i * tm, SUBLANES), 0))

    row = pl.BlockSpec((1, D), lambda i: (0, 0))
    h_first = skip if h.shape[0] == L else 0
    out, xo = _call(
        body, name=name, grid=(rows // tm,),
        in_specs=[rows_of(K), pl.BlockSpec((K, D), lambda i: (0, 0)), rows_of(D, h_first), row, rows_of(D), row],
        out_specs=(pl.BlockSpec((tm, D), lambda i: (i, 0)), row),
        out_shape=(jax.ShapeDtypeStruct((rows, D), F32), jax.ShapeDtypeStruct((1, D), F32)),
        sem=("arbitrary",), args=(a, b, h, g, dres, dg0), xchg=xchg)
    return out if xchg is None else (out, xo)


def _mm(a, b, name, *, res=None, b_t=False, out_dtype=F32, tn_cap=1408, xchg=None):
    M, K = a.shape
    N = b.shape[0] if b_t else b.shape[1]
    tm = _token_tile(M)
    tn = _divisor(N, tn_cap, 128)
    if isinstance(res, tuple):
        assert tn == N
        r_specs, r_args, read_r = _seq_tiles(res, tm)
    elif res is not None:
        r_specs, r_args, read_r = [pl.BlockSpec((tm, tn), lambda i, j: (i, j))], [res], lambda refs, i: refs[0][...]
    else:
        r_specs, r_args, read_r = [], [], None

    def body(*refs):
        a_ref, b_ref, o_ref = refs[0], refs[1], refs[-1]
        av = a_ref[...].astype(BF16)
        prod = _dot_nt(av, b_ref[...]) if b_t else jnp.dot(av, b_ref[...], preferred_element_type=F32)
        o_ref[...] = (prod if read_r is None else prod + read_r(refs[2:-1], pl.program_id(0))).astype(out_dtype)

    b_spec = pl.BlockSpec((tn, K), lambda i, j: (j, 0)) if b_t else pl.BlockSpec((K, tn), lambda i, j: (0, j))
    in_specs = [pl.BlockSpec((tm, K), lambda i, j: (i, 0)), b_spec] + r_specs
    args = [a, b] + r_args
    out, xo = _call(
        body, name=name, grid=(M // tm, N // tn), in_specs=in_specs,
        out_specs=pl.BlockSpec((tm, tn), lambda i, j: (i, j)), out_shape=jax.ShapeDtypeStruct((M, N), out_dtype),
        sem=("parallel", "parallel"), args=args, xchg=xchg)
    return out if xchg is None else (out, xo)


def _mm_tn(a, b, name, *, halves=1, tq_cap=1408):
    L, Q = b.shape
    ph = a.shape[-1]
    P = ph * halves
    tl = _divisor(L, WGRAD_TILE_ROWS, HALO)
    tp = _divisor(ph, 1408, 128)
    tq = _divisor(Q, tq_cap, 128)
    pper = ph // tp
    nl = L // tl
    grid = (P // tp, Q // tq, nl)

    def body(a_ref, b_ref, o_ref, acc):
        prod = lax.dot_general(a_ref[...].astype(BF16), b_ref[...].astype(BF16), (((0,), (0,)), ((), ())),
                               preferred_element_type=F32)
        l = pl.program_id(2)
        if nl == 1:
            o_ref[...] = prod.astype(BF16)
            return

        @pl.when(l == 0)
        def _():
            acc[...] = prod

        @pl.when(jnp.logical_and(l > 0, l < nl - 1))
        def _():
            acc[...] += prod

        @pl.when(l == nl - 1)
        def _():
            o_ref[...] = (acc[...] + prod).astype(BF16)

    if halves > 1:
        a_spec = pl.BlockSpec((None, tl, tp), lambda p, q, l: (p // pper, l, p % pper))
    else:
        a_spec = pl.BlockSpec((tl, tp), lambda p, q, l: (l, p))
    return pl.pallas_call(
        body, name=name, grid=grid,
        in_specs=[a_spec, pl.BlockSpec((tl, tq), lambda p, q, l: (l, q))],
        out_specs=pl.BlockSpec((tp, tq), lambda p, q, l: (p, q)),
        out_shape=jax.ShapeDtypeStruct((P, Q), BF16),
        scratch_shapes=[pltpu.VMEM((tp, tq), F32)],
        compiler_params=_params(("parallel", "parallel", "arbitrary")),
    )(a, b)


def _loss_head(h, g, tgt, n_meta, name):
    L, D = h.shape
    tl = _token_tile(L)
    nt = L // tl

    def body(h_ref, g_ref, t_ref, dh_ref, dg_ref, loss_ref):
        i = pl.program_id(0)
        x = h_ref[...]
        r = lax.rsqrt(jnp.mean(x * x, axis=-1, keepdims=True) + EPS)
        xhat = x * r
        gg = g_ref[...]
        y = xhat * gg
        rows = i * tl + lax.broadcasted_iota(jnp.int32, (tl, 1), 0)
        t = t_ref[...]
        t = jnp.where(i == 0, pltpu.roll(t, n_meta, axis=0), t)
        err = jnp.where(rows >= n_meta, y - t, 0.0)
        dy = err * (1.0 / D)
        dxhat = dy * gg
        dh_ref[...] = r * (dxhat - xhat * jnp.mean(dxhat * xhat, axis=-1, keepdims=True))
        dg_part = jnp.sum(_rowsum8(dy * xhat), axis=0, keepdims=True)
        per_row = jnp.mean(err * err, axis=-1, keepdims=True)
        loss_part = jnp.broadcast_to(0.5 * jnp.sum(per_row, axis=0, keepdims=True), (1, 128))

        @pl.when(i == 0)
        def _():
            dg_ref[...] = dg_part
            loss_ref[...] = loss_part

        @pl.when(i > 0)
        def _():
            dg_ref[...] += dg_part
            loss_ref[...] += loss_part

    tile = pl.BlockSpec((tl, D), lambda i: (i, 0))
    row = pl.BlockSpec((1, D), lambda i: (0, 0))
    window = pl.BlockSpec((pl.Element(tl), pl.Element(D)),
                          lambda i: (pl.multiple_of(jnp.maximum(i * tl - n_meta, 0), SUBLANES), 0))
    return pl.pallas_call(
        body, name=name, grid=(nt,), in_specs=[tile, row, window],
        out_specs=(tile, row, pl.BlockSpec((1, 128), lambda i: (0, 0))),
        out_shape=(jax.ShapeDtypeStruct((L, D), F32), jax.ShapeDtypeStruct((1, D), F32),
                   jax.ShapeDtypeStruct((1, 128), F32)),
        compiler_params=_params(("arbitrary",)),
    )(h, g, tgt)


def _pool_fwd_block(pwin, pw_ref, row0, rb, g, gd, w, t0):
    wv = pwin[pl.ds(row0 + HALO - POOL_PAD, rb + POOL_PAD), g * gd:(g + 1) * gd]
    s = wv
    sh = 1
    while sh < w:
        s = s + pltpu.roll(s, sh, axis=0)
        sh *= 2
    win = s[POOL_PAD:POOL_PAD + rb]
    pt = wv[POOL_PAD:POOL_PAD + rb]
    tg = t0 + lax.broadcasted_iota(jnp.int32, (rb, 1), 0)
    cnt = jnp.minimum(tg + 1, w).astype(F32)
    return win / cnt - pt


def _fill_windows(i, zp_ref, zc_ref, u0w, pwin, tl, cc):
    keep = i > 0
    zp = zp_ref[...]
    u0w[0:HALO, :] = jnp.where(keep, zp[:, :cc] * _sigmoid(zp[:, cc:2 * cc]), 0.0)
    pwin[0:HALO, :] = jnp.where(keep, zp[:, 2 * cc:], 0.0)

    def fill(c, carry):
        b = pl.multiple_of(c * ROW_CHUNK, SUBLANES)
        zc = zc_ref[pl.ds(b, ROW_CHUNK), :]
        u0w[pl.ds(HALO + b, ROW_CHUNK), :] = zc[:, :cc] * _sigmoid(zc[:, cc:2 * cc])
        pwin[pl.ds(HALO + b, ROW_CHUNK), :] = zc[:, 2 * cc:]
        return carry

    lax.fori_loop(0, tl // ROW_CHUNK, fill, 0)


def _mixer_fwd(z, ck, cb, lg, lb, pw, ps, am, name, xchg=None):
    L, ci = z.shape
    kw, _, cc = ck.shape
    cp = ci - 2 * cc
    ng, gd = pw.shape[0], pw.shape[1]
    tl = _token_tile(L)
    nt = L // tl
    hb = tl // HALO
    rb = _stat_rows(tl)
    tap0 = CONV_PAD - (kw - 1)

    def body(zp_ref, zc_ref, ck_ref, cb_ref, lg_ref, lb_ref, pw_ref, ps_ref, am_ref, y_ref, u1_ref, u0w, pwin):
        i = pl.program_id(0)
        _fill_windows(i, zp_ref, zc_ref, u0w, pwin, tl, cc)

        def conv(c, carry):
            b = pl.multiple_of(c * ROW_CHUNK, SUBLANES)
            w = u0w[pl.ds(b + HALO - CONV_PAD, ROW_CHUNK + CONV_PAD), :]
            acc = jnp.broadcast_to(cb_ref[...], (ROW_CHUNK, cc))
            for j in range(kw):
                acc = acc + _rows_of(ck_ref[j], ROW_CHUNK) * w[tap0 + j:tap0 + j + ROW_CHUNK]
            u1_ref[pl.ds(b, ROW_CHUNK), :] = acc
            return carry

        lax.fori_loop(0, tl // ROW_CHUNK, conv, 0)

        def blocks(k, carry):
            b = pl.multiple_of(k * rb, SUBLANES)
            u1 = u1_ref[pl.ds(b, rb), :]
            xc = u1 - _head_mean(u1, am_ref)
            var = _head_mean(xc * xc, am_ref)
            u2 = (xc * lax.rsqrt(var + EPS)) * lg_ref[...] + lb_ref[...]
            y_ref[pl.ds(b, rb), 0:cc] = (u2 * _sigmoid(u2)).astype(y_ref.dtype)
            for g in range(ng):
                d = _pool_fwd_block(pwin, pw_ref, b, rb, g, gd, POOL_WINDOWS[g], i * tl + b)
                yp = jnp.dot(d.astype(BF16), pw_ref[g].astype(BF16), preferred_element_type=F32)
                yp = yp * ps_ref[:, g * gd:(g + 1) * gd]
                y_ref[pl.ds(b, rb), cc + g * gd:cc + (g + 1) * gd] = yp.astype(y_ref.dtype)
            return carry

        lax.fori_loop(0, tl // rb, blocks, 0)

    def full(a):
        nd = a.ndim
        return pl.BlockSpec(a.shape, lambda i: (0,) * nd)

    out, xo = _call(
        body, name=name, grid=(nt,),
        in_specs=[pl.BlockSpec((HALO, ci), lambda i: (jnp.maximum(i * hb - 1, 0), 0)),
                  pl.BlockSpec((tl, ci), lambda i: (i, 0)),
                  full(ck), full(cb), full(lg), full(lb), full(pw), full(ps), full(am)],
        out_specs=(pl.BlockSpec((tl, cc + cp), lambda i: (i, 0)), pl.BlockSpec((tl, cc), lambda i: (i, 0))),
        out_shape=(jax.ShapeDtypeStruct((L, cc + cp), BF16), jax.ShapeDtypeStruct((L, cc), F32)),
        scratch_shapes=[pltpu.VMEM((HALO + tl, cc), F32), pltpu.VMEM((HALO + tl, cp), F32)],
        sem=("parallel",), args=(z, z, ck, cb, lg, lb, pw, ps, am), xchg=xchg)
    return out if xchg is None else (out, xo)


def _mixer_bwd(z, u1, dy, ck, lg, lb, pw, ps, am, name, xchg=None):
    L, ci = z.shape
    kw, _, cc = ck.shape
    cp = ci - 2 * cc
    ng, gd = pw.shape[0], pw.shape[1]
    tl = _token_tile(L)
    nt = L // tl
    hb = tl // HALO
    rb = _stat_rows(tl)

    def body(zp_ref, zc_ref, u1c_ref, u1n_ref, dyc_ref, dyn_ref, ck_ref, lg_ref, lb_ref, pw_ref, ps_ref, am_ref,
             dz_ref, dck_ref, dcb_ref, dlg_ref, dlb_ref, dpw_ref, dps_ref,
             u0w, pwin, du1w, ddw, ew, dkacc, dcb8, dlg8, dlb8, dps8):
        i = pl.program_id(0)
        has_next = i < nt - 1

        @pl.when(i == 0)
        def _():
            dck_ref[...] = jnp.zeros_like(dck_ref)
            dcb_ref[...] = jnp.zeros_like(dcb_ref)
            dlg_ref[...] = jnp.zeros_like(dlg_ref)
            dlb_ref[...] = jnp.zeros_like(dlb_ref)
            dpw_ref[...] = jnp.zeros_like(dpw_ref)
            dps_ref[...] = jnp.zeros_like(dps_ref)

        dkacc[...] = jnp.zeros_like(dkacc)
        dcb8[...] = jnp.zeros_like(dcb8)
        dlg8[...] = jnp.zeros_like(dlg8)
        dlb8[...] = jnp.zeros_like(dlb8)
        dps8[...] = jnp.zeros_like(dps8)

        _fill_windows(i, zp_ref, zc_ref, u0w, pwin, tl, cc)

        def conv_side(u1, dyc, own):
            xc = u1 - _head_mean(u1, am_ref)
            rstd = lax.rsqrt(_head_mean(xc * xc, am_ref) + EPS)
            uh = xc * rstd
            lgv = lg_ref[...]
            u2 = uh * lgv + lb_ref[...]
            sg = _sigmoid(u2)
            du2 = dyc * (sg * (1.0 + u2 * (1.0 - sg)))
            if own:
                dlg8[...] += _rowsum8(du2 * uh)
                dlb8[...] += _rowsum8(du2)
            duh = du2 * lgv
            return rstd * (duh - _head_mean(duh, am_ref) - uh * _head_mean(duh * uh, am_ref))

        def pool_side(dyp, t0, rows):
            dds, es = [], []
            tg = t0 + lax.broadcasted_iota(jnp.int32, (rows, 1), 0)
            for g in range(ng):
                dypre = dyp[:, g * gd:(g + 1) * gd] * ps_ref[:, g * gd:(g + 1) * gd]
                dd = lax.dot_general(dypre.astype(BF16), pw_ref[g].astype(BF16), (((1,), (1,)), ((), ())),
                                     preferred_element_type=F32)
                cnt = jnp.minimum(tg + 1, POOL_WINDOWS[g]).astype(F32)
                dds.append(dd)
                es.append(dd / cnt)
            return jnp.concatenate(dds, axis=-1), jnp.concatenate(es, axis=-1)

        def blocks(k, carry):
            b = pl.multiple_of(k * rb, SUBLANES)
            dyb = dyc_ref[pl.ds(b, rb), :].astype(F32)
            du1 = conv_side(u1c_ref[pl.ds(b, rb), :], dyb[:, :cc], True)
            du1w[pl.ds(b, rb), :] = du1
            dcb8[...] += _rowsum8(du1)
            dyp = dyb[:, cc:]
            dd, e = pool_side(dyp, i * tl + b, rb)
            ddw[pl.ds(b, rb), :] = dd
            ew[pl.ds(b, rb), :] = e
            for g in range(ng):
                d = _pool_fwd_block(pwin, pw_ref, b, rb, g, gd, POOL_WINDOWS[g], i * tl + b)
                db16 = d.astype(BF16)
                dypg = dyp[:, g * gd:(g + 1) * gd]
                ypre = jnp.dot(db16, pw_ref[g].astype(BF16), preferred_element_type=F32)
                dps8[:, g * gd:(g + 1) * gd] += _rowsum8(dypg * ypre)
                dypre = (dypg * ps_ref[:, g * gd:(g + 1) * gd]).astype(BF16)
                dpw_ref[g] += lax.dot_general(db16, dypre, (((0,), (0,)), ((), ())), preferred_element_type=F32)
            return carry

        lax.fori_loop(0, tl // rb, blocks, 0)

        dyn = dyn_ref[...].astype(F32)
        du1n = conv_side(u1n_ref[...], dyn[:, :cc], False)
        du1w[tl:tl + HALO, :] = jnp.where(has_next, du1n, 0.0)
        ddn, en = pool_side(dyn[:, cc:], (i + 1) * tl, HALO)
        ew[tl:tl + HALO, :] = jnp.where(has_next, en, 0.0)

        def taps(c, carry):
            b = pl.multiple_of(c * ROW_CHUNK, SUBLANES)
            w = du1w[pl.ds(b, ROW_CHUNK + CONV_PAD), :]
            u0c = u0w[pl.ds(HALO + b, ROW_CHUNK), :]
            acc = jnp.zeros((ROW_CHUNK, cc), F32)
            for j in range(kw):
                o = kw - 1 - j
                sh = w[o:o + ROW_CHUNK]
                acc = acc + _rows_of(ck_ref[j], ROW_CHUNK) * sh
                dkacc[j] += _rowsum8(u0c * sh)
            zc = zc_ref[pl.ds(b, ROW_CHUNK), :]
            a = zc[:, :cc]
            sg = _sigmoid(zc[:, cc:2 * cc])
            dz_ref[pl.ds(b, ROW_CHUNK), 0:cc] = (acc * sg).astype(dz_ref.dtype)
            dz_ref[pl.ds(b, ROW_CHUNK), cc:2 * cc] = (acc * a * sg * (1.0 - sg)).astype(dz_ref.dtype)
            return carry

        lax.fori_loop(0, tl // ROW_CHUNK, taps, 0)

        def pool_back(k, carry):
            b = pl.multiple_of(k * rb, SUBLANES)
            n = rb + POOL_PAD
            for g in range(ng):
                s = ew[pl.ds(b, n), g * gd:(g + 1) * gd]
                sh = 1
                while sh < POOL_WINDOWS[g]:
                    s = s + pltpu.roll(s, n - sh, axis=0)
                    sh *= 2
                dp = s[0:rb] - ddw[pl.ds(b, rb), g * gd:(g + 1) * gd]
                dz_ref[pl.ds(b, rb), 2 * cc + g * gd:2 * cc + (g + 1) * gd] = dp.astype(dz_ref.dtype)
            return carry

        lax.fori_loop(0, tl // rb, pool_back, 0)

        dck_ref[...] += jnp.sum(dkacc[...], axis=1)
        dcb_ref[...] += jnp.sum(dcb8[...], axis=0, keepdims=True)
        dlg_ref[...] += jnp.sum(dlg8[...], axis=0, keepdims=True)
        dlb_ref[...] += jnp.sum(dlb8[...], axis=0, keepdims=True)
        dps_ref[...] += jnp.sum(dps8[...], axis=0, keepdims=True)

    def full(a):
        nd = a.ndim
        return pl.BlockSpec(a.shape, lambda i: (0,) * nd)

    nhb = L // HALO

    def prev_map(i):
        return (jnp.maximum(i * hb - 1, 0), 0)

    def next_map(i):
        return (jnp.minimum((i + 1) * hb, nhb - 1), 0)

    dcc = cc + cp
    row_cc = jax.ShapeDtypeStruct((1, cc), F32)
    out_shape = (jax.ShapeDtypeStruct((L, ci), BF16), jax.ShapeDtypeStruct((kw, cc), F32), row_cc, row_cc, row_cc,
                 jax.ShapeDtypeStruct((ng, gd, gd), F32), jax.ShapeDtypeStruct((1, cp), F32))
    acc_spec = [pl.BlockSpec((kw, cc), lambda i: (0, 0))] + [pl.BlockSpec((1, cc), lambda i: (0, 0))] * 3 + [
        pl.BlockSpec((ng, gd, gd), lambda i: (0, 0, 0)), pl.BlockSpec((1, cp), lambda i: (0, 0))]
    out, xo = _call(
        body, name=name, grid=(nt,),
        in_specs=[pl.BlockSpec((HALO, ci), prev_map), pl.BlockSpec((tl, ci), lambda i: (i, 0)),
                  pl.BlockSpec((tl, cc), lambda i: (i, 0)), pl.BlockSpec((HALO, cc), next_map),
                  pl.BlockSpec((tl, dcc), lambda i: (i, 0)), pl.BlockSpec((HALO, dcc), next_map),
                  full(ck), full(lg), full(lb), full(pw), full(ps), full(am)],
        out_specs=tuple([pl.BlockSpec((tl, ci), lambda i: (i, 0))] + acc_spec),
        out_shape=out_shape,
        scratch_shapes=[pltpu.VMEM((HALO + tl, cc), F32), pltpu.VMEM((HALO + tl, cp), F32),
                        pltpu.VMEM((tl + HALO, cc), F32), pltpu.VMEM((tl, cp), F32), pltpu.VMEM((tl + HALO, cp), F32),
                        pltpu.VMEM((kw, SUBLANES, cc), F32), pltpu.VMEM((SUBLANES, cc), F32),
                        pltpu.VMEM((SUBLANES, cc), F32), pltpu.VMEM((SUBLANES, cc), F32), pltpu.VMEM((SUBLANES, cp), F32)],
        sem=("arbitrary",), args=(z, z, u1, u1, dy, dy, ck, lg, lb, pw, ps, am), xchg=xchg)
    return out if xchg is None else (out, xo)


def _row_parts(nc, n=3):
    n = min(n, nc)
    cuts = [round(k * nc / n) for k in range(n + 1)]
    return [(cuts[k], cuts[k + 1]) for k in range(n)]


def _tap_rows(k_ref):
    return [jnp.broadcast_to(k_ref[j:j + 1, :], (SUBLANES, k_ref.shape[1])) for j in range(k_ref.shape[0])]


def _rows_of(tap, n):
    return tap if n == SUBLANES else jnp.concatenate([tap] * (n // SUBLANES), axis=0)


def _ffn_conv(win, taps, rows):
    kw = len(taps)
    o = FFN_PAD - (kw - 1)
    acc = _rows_of(taps[0], rows) * win[o:o + rows]
    for j in range(1, kw):
        acc = acc + _rows_of(taps[j], rows) * win[o + j:o + j + rows]
    return acc


def _ffn_block_fwd(h_mid, g, wup_t, kf, wdown, name, xchg=None):
    L, D = h_mid.shape
    f = wdown.shape[0]
    kw = kf.shape[0]
    tl = _token_tile(L)
    tc = _divisor(f, 256, 128)
    nj = f // tc
    nt = L // tl
    pad = 2 * SUBLANES
    hb = tl // pad
    rc = CONV3_ROWS
    parts = _row_parts(tl // rc)

    def body(hp_ref, hc_ref, g_ref, wg_ref, wv_ref, kg_ref, kv_ref, wd_ref, out_ref, hn_ref, act_ref, ux_ref, uc_ref,
             hn_halo, halo, ug_ref, acc):
        i = pl.program_id(0)
        kb = pl.program_id(1)

        @pl.when(kb == 0)
        def _():
            gg = g_ref[...]

            def norm(x):
                r = lax.rsqrt(jnp.mean(x * x, axis=-1, keepdims=True) + EPS)
                return ((x * r) * gg).astype(BF16)

            hn_halo[...] = jnp.where(i > 0, norm(hp_ref[...]), jnp.zeros((pad, D), BF16))
            hn_ref[...] = norm(hc_ref[...])
            acc[...] = jnp.zeros_like(acc)

        w_refs = (wg_ref, wv_ref)
        taps = (_tap_rows(kg_ref), _tap_rows(kv_ref))
        hh = hn_halo[...]
        for h in range(2):
            halo[h] = _dot_nt(hh, w_refs[h][...])[pad - FFN_PAD:]

        def up_part(lo, hi):
            a, b = lo * rc, hi * rc
            for h in range(2):
                ug_ref[h, a:b, :] = _dot_nt(hn_ref[a:b, :], w_refs[h][...])

        def down_part(lo, hi):
            a, b = lo * rc, hi * rc
            acc[a:b, :] += jnp.dot(act_ref[a:b, :], wd_ref[...], preferred_element_type=F32)

        def chunk_rows(lo, hi):
            for c in range(lo, hi):
                r0 = c * rc
                convd = []
                for h in range(2):
                    if c == 0:
                        win = jnp.concatenate([halo[h], ug_ref[h, 0:rc]], axis=0)
                    else:
                        win = ug_ref[h, r0 - FFN_PAD:r0 + rc]
                    convd.append(_ffn_conv(win, taps[h], rc))
                    ux_ref[h, r0:r0 + rc, :] = win[FFN_PAD:].astype(BF16)
                    uc_ref[h, r0:r0 + rc, :] = convd[h].astype(BF16)
                gate, val = convd
                act_ref[r0:r0 + rc, :] = ((gate * _sigmoid(gate)) * val).astype(BF16)

        for p, (lo, hi) in enumerate(parts):
            if p == 0:
                up_part(lo, hi)
            if p + 1 < len(parts):
                up_part(*parts[p + 1])
            if p > 0:
                down_part(*parts[p - 1])
            chunk_rows(lo, hi)
        down_part(*parts[-1])

        @pl.when(kb == nj - 1)
        def _():
            out_ref[...] = acc[...] + hc_ref[...]

    out, xo = _call(
        body, name=name, grid=(nt, nj),
        in_specs=[pl.BlockSpec((pad, D), lambda i, k: (jnp.maximum(i * hb - 1, 0), 0)),
                  pl.BlockSpec((tl, D), lambda i, k: (i, 0)),
                  pl.BlockSpec((1, D), lambda i, k: (0, 0)),
                  pl.BlockSpec((tc, D), lambda i, k: (k, 0)), pl.BlockSpec((tc, D), lambda i, k: (k + nj, 0)),
                  pl.BlockSpec((kw, tc), lambda i, k: (0, k)), pl.BlockSpec((kw, tc), lambda i, k: (0, k + nj)),
                  pl.BlockSpec((tc, D), lambda i, k: (k, 0))],
        out_specs=(pl.BlockSpec((tl, D), lambda i, k: (i, 0)), pl.BlockSpec((tl, D), lambda i, k: (i, 0)),
                   pl.BlockSpec((tl, tc), lambda i, k: (i, k)),
                   pl.BlockSpec((2, tl, tc), lambda i, k: (0, i, k)), pl.BlockSpec((2, tl, tc), lambda i, k: (0, i, k))),
        out_shape=(jax.ShapeDtypeStruct((L, D), F32), jax.ShapeDtypeStruct((L, D), BF16),
                   jax.ShapeDtypeStruct((L, f), BF16),
                   jax.ShapeDtypeStruct((2, L, f), BF16), jax.ShapeDtypeStruct((2, L, f), BF16)),
        scratch_shapes=[pltpu.VMEM((pad, D), BF16), pltpu.VMEM((2, FFN_PAD, tc), F32), pltpu.VMEM((2, tl, tc), F32),
                        pltpu.VMEM((tl, D), F32)],
        sem=("parallel", "arbitrary"), args=(h_mid, h_mid, g, wup_t, wup_t, kf, kf, wdown), xchg=xchg)
    return out if xchg is None else (out, xo)


def _ffn_block_bwd(dh, h_mid, g, ux, uc, kf, wdown, wup_t, name, xchg=None):
    L, D = dh.shape
    f = ux.shape[2]
    kw = kf.shape[0]
    tl = _token_tile(L)
    tc = _divisor(f, 256, 128)
    nj = f // tc
    nt = L // tl
    pad = 2 * SUBLANES
    rc = CONV3_ROWS
    nc = tl // rc
    parts = _row_parts(nc)

    def body(dhc_ref, dhn_ref, hm_ref, g_ref, xg_ref, xv_ref, cg_ref, cgn_ref, cv_ref, cvn_ref, kg_ref, kv_ref,
             wd_ref, wg_ref, wv_ref, dhm_ref, dg_ref, du_ref, dk_ref, dh_ext, dact_s, acc):
        i = pl.program_id(0)
        kb = pl.program_id(1)

        @pl.when(kb == 0)
        def _():
            dh_ext[0:tl, :] = dhc_ref[...].astype(BF16)
            dh_ext[tl:tl + pad, :] = dhn_ref[...].astype(BF16)
            acc[...] = jnp.zeros_like(acc)

        @pl.when(jnp.logical_and(i == 0, kb == 0))
        def _():
            dg_ref[...] = jnp.zeros_like(dg_ref)
            dk_ref[...] = jnp.zeros_like(dk_ref)

        x_refs, c_refs, nxt = (xg_ref, xv_ref), (cg_ref, cv_ref), (cgn_ref, cvn_ref)
        taps = (_tap_rows(kg_ref), _tap_rows(kv_ref))
        dk = [[jnp.zeros((SUBLANES, tc), F32) for _ in range(kw)] for _ in range(2)]

        def dact_part(lo, hi):
            a, b = lo * rc, hi * rc + pad
            dact_s[a:b, :] = _dot_nt(dh_ext[a:b, :], wd_ref[...])

        def dhn_part(lo, hi):
            a, b = lo * rc, hi * rc
            acc[a:b, :] += (jnp.dot(du_ref[0, a:b, :], wg_ref[...], preferred_element_type=F32)
                            + jnp.dot(du_ref[1, a:b, :], wv_ref[...], preferred_element_type=F32))

        for p, (lo, hi) in enumerate(parts):
            if p == 0:
                dact_part(lo, hi)
            if p + 1 < len(parts):
                dact_part(*parts[p + 1])
            if p > 0:
                dhn_part(*parts[p - 1])
            chunk_rows(lo, hi, x_refs, c_refs, nxt, taps, dk, i, dact_s, du_ref)
        dhn_part(*parts[-1])
        for h in range(2):
            for j in range(kw):
                dk_ref[kb, h, j:j + 1, :] += jnp.sum(dk[h][j], axis=0, keepdims=True)

        @pl.when(kb == nj - 1)
        def _():
            x = hm_ref[...]
            r = lax.rsqrt(jnp.mean(x * x, axis=-1, keepdims=True) + EPS)
            xhat = x * r
            dhn = acc[...]
            dxhat = dhn * g_ref[...]
            dhm_ref[...] = dhc_ref[...] + r * (dxhat - xhat * jnp.mean(dxhat * xhat, axis=-1, keepdims=True))
            dg_ref[...] += jnp.sum(_rowsum8(dhn * xhat), axis=0, keepdims=True)

    def chunk_rows(lo, hi, x_refs, c_refs, nxt, taps, dk, i, dact_s, du_ref):
        for c in range(lo, hi):
            r0 = c * rc
            n = rc + FFN_PAD
            convd = []
            for h in range(2):
                if c == nc - 1:
                    rows = jnp.concatenate([c_refs[h][r0:r0 + rc, :], nxt[h][...]], axis=0)
                else:
                    rows = c_refs[h][r0:r0 + rc + pad, :]
                convd.append(rows.astype(F32)[0:n])
            gate, val = convd
            xs = [x_refs[h][r0:r0 + rc, :].astype(F32) for h in range(2)]
            dact = dact_s[r0:r0 + n, :]
            sg = _sigmoid(gate)
            dcs = [dact * val * (sg * (1.0 + gate * (1.0 - sg))), dact * (gate * sg)]
            if c == nc - 1:
                live = jnp.logical_or(lax.broadcasted_iota(jnp.int32, (n, 1), 0) < rc, i < nt - 1)
                dcs = [jnp.where(live, d, 0.0) for d in dcs]
            for h in range(2):
                xc = xs[h]
                dx = None
                for j in range(kw):
                    o = kw - 1 - j
                    sh = dcs[h][o:o + rc]
                    term = _rows_of(taps[h][j], rc) * sh
                    dx = term if dx is None else dx + term
                    dk[h][j] = dk[h][j] + _rowsum8(xc * sh)
                du_ref[h, r0:r0 + rc, :] = dx.astype(BF16)

    def after(i):
        return jnp.minimum((i + 1) * (tl // pad), L // pad - 1)

    def half(h, rows, idx):
        return pl.BlockSpec((None, rows, tc), lambda i, k: (h,) + idx(i, k))

    def tile(i, k):
        return (i, k)

    def behind(i, k):
        return (after(i), k)

    out, xo = _call(
        body, name=name, grid=(nt, nj),
        in_specs=[pl.BlockSpec((tl, D), lambda i, k: (i, 0)),
                  pl.BlockSpec((pad, D), lambda i, k: (after(i), 0)),
                  pl.BlockSpec((tl, D), lambda i, k: (i, 0)), pl.BlockSpec((1, D), lambda i, k: (0, 0)),
                  half(0, tl, tile), half(1, tl, tile),
                  half(0, tl, tile), half(0, pad, behind), half(1, tl, tile), half(1, pad, behind),
                  pl.BlockSpec((kw, tc), lambda i, k: (0, k)), pl.BlockSpec((kw, tc), lambda i, k: (0, k + nj)),
                  pl.BlockSpec((tc, D), lambda i, k: (k, 0)),
                  pl.BlockSpec((tc, D), lambda i, k: (k, 0)), pl.BlockSpec((tc, D), lambda i, k: (k + nj, 0))],
        out_specs=(pl.BlockSpec((tl, D), lambda i, k: (i, 0)), pl.BlockSpec((1, D), lambda i, k: (0, 0)),
                   pl.BlockSpec((2, tl, tc), lambda i, k: (0, i, k)),
                   pl.BlockSpec((nj, 2, kw, tc), lambda i, k: (0, 0, 0, 0))),
        out_shape=(jax.ShapeDtypeStruct((L, D), F32), jax.ShapeDtypeStruct((1, D), F32),
                   jax.ShapeDtypeStruct((2, L, f), BF16), jax.ShapeDtypeStruct((nj, 2, kw, tc), F32)),
        scratch_shapes=[pltpu.VMEM((tl + pad, D), BF16), pltpu.VMEM((tl + pad, tc), F32), pltpu.VMEM((tl, D), F32)],
        sem=("arbitrary", "arbitrary"), args=(dh, dh, h_mid, g, ux, ux, uc, uc, uc, uc, kf, kf, wdown, wup_t, wup_t),
        xchg=xchg)
    return out if xchg is None else (out, xo)


def _adamw_math(w, g, m, v):
    m = ADAM_B1 * m + (1.0 - ADAM_B1) * g
    v = ADAM_B2 * v + (1.0 - ADAM_B2) * (g * g)
    m_hat = m / (1.0 - ADAM_B1 ** ADAM_STEP)
    v_hat = v / (1.0 - ADAM_B2 ** ADAM_STEP)
    delta = -ADAM_LR * (m_hat / (jnp.sqrt(v_hat) + ADAM_EPS) + ADAM_WD * w)
    return delta, m, v


def _sum_parts(parts_ref, idx):
    g = parts_ref[(0,) + idx].astype(F32)
    for q in range(1, N_DEV):
        g = g + parts_ref[(q,) + idx].astype(F32)
    return g


def _adamw_big(parts, w, m, v, name):
    nl, R, C = w.shape
    tr = _divisor(R, 256, 2 * SUBLANES)

    def body(*refs):
        p_refs = refs[:nl]
        w_ref, m_ref, v_ref, g_ref, d_ref, nm_ref, nv_ref = refs[nl:]
        layer = pl.program_id(0)
        for k in range(nl):
            @pl.when(layer == k)
            def _(k=k):
                g = _sum_parts(p_refs[k], ())
                d, nm, nv = _adamw_math(w_ref[0], g, m_ref[0], v_ref[0])
                g_ref[0] = g
                d_ref[0] = d
                nm_ref[0] = nm
                nv_ref[0] = nv

    def part_spec(k):
        return pl.BlockSpec((N_DEV, tr, C), lambda l, r: (0, jnp.where(l == k, r, 0), 0))

    blk = pl.BlockSpec((1, tr, C), lambda l, r: (l, r, 0))
    shp = jax.ShapeDtypeStruct((nl, R, C), F32)
    return pl.pallas_call(
        body, name=name, grid=(nl, R // tr),
        in_specs=[part_spec(k) for k in range(nl)] + [blk, blk, blk],
        out_specs=(blk, blk, blk, blk), out_shape=(shp, shp, shp, shp),
        compiler_params=_params(("arbitrary", "arbitrary")),
    )(*parts, w, m, v)


def _adamw_small(entries, name):
    n = len(entries)
    uniq = []
    for e in entries:
        if not any(e[0] is u for u in uniq):
            uniq.append(e[0])
    pidx = [next(k for k, u in enumerate(uniq) if u is e[0]) for e in entries]
    npart = len(uniq)

    def body(*refs):
        p_refs = refs[:npart]
        wmv = refs[npart:npart + 3 * n]
        outs = refs[npart + 3 * n:]
        for t, e in enumerate(entries):
            lo, w = e[1], e[2]
            rows = w.shape[0]
            pr = p_refs[pidx[t]]
            g = pr[0, lo:lo + rows].astype(F32)
            for q in range(1, N_DEV):
                g = g + pr[q, lo:lo + rows].astype(F32)
            d, nm, nv = _adamw_math(wmv[3 * t][...], g, wmv[3 * t + 1][...], wmv[3 * t + 2][...])
            outs[4 * t][...] = g
            outs[4 * t + 1][...] = d
            outs[4 * t + 2][...] = nm
            outs[4 * t + 3][...] = nv

    vm = pl.BlockSpec(memory_space=pltpu.VMEM)
    args = list(uniq)
    out_shape = []
    for e in entries:
        args += [e[2], e[3], e[4]]
        out_shape += [jax.ShapeDtypeStruct(e[2].shape, F32)] * 4
    res = pl.pallas_call(
        body, name=name, in_specs=[vm] * len(args), out_specs=tuple([vm] * len(out_shape)),
        out_shape=tuple(out_shape), compiler_params=_params(),
    )(*args)
    return [tuple(res[4 * t:4 * t + 4]) for t in range(n)]


def _head_matrix(cc):
    bw = min(256, cc)
    r = lax.broadcasted_iota(jnp.int32, (bw, bw), 0) // HEAD_DIM
    c = lax.broadcasted_iota(jnp.int32, (bw, bw), 1) // HEAD_DIM
    return jnp.where(r == c, 1.0 / HEAD_DIM, 0.0).astype(BF16)


def _cols_from_shards(g):
    nd = g.ndim
    perm = tuple(range(1, nd - 1)) + (0, nd - 1)
    t = jnp.transpose(g, perm)
    return t.reshape(t.shape[:-2] + (t.shape[-2] * t.shape[-1],))


def _cols_to_shards(a):
    nd = a.ndim
    t = a.reshape(a.shape[:-1] + (N_DEV, a.shape[-1] // N_DEV))
    perm = (nd - 1,) + tuple(range(nd - 1)) + (nd,)
    return jnp.transpose(t, perm)


def kernel(x, meta_tokens, norm1_g, w_in, conv_dw_k, conv_dw_b, conv_ln_g, conv_ln_b, pool_w, pool_scale, w_out, norm2_g, w_up, ffn_dw_k, w_down, final_g, loss_target, m_meta_tokens, m_norm1_g, m_w_in, m_conv_dw_k, m_conv_dw_b, m_conv_ln_g, m_conv_ln_b, m_pool_w, m_pool_scale, m_w_out, m_norm2_g, m_w_up, m_ffn_dw_k, m_w_down, m_final_g, v_meta_tokens, v_norm1_g, v_w_in, v_conv_dw_k, v_conv_dw_b, v_conv_ln_g, v_conv_ln_b, v_pool_w, v_pool_scale, v_w_out, v_norm2_g, v_w_up, v_ffn_dw_k, v_w_down, v_final_g):
    depth, D = norm1_g.shape
    n_meta = meta_tokens.shape[0]
    seq = x.shape[1]
    L = n_meta + seq
    cc = conv_dw_b.shape[1]
    ng, gd = pool_w.shape[1], pool_w.shape[2]
    f = w_down.shape[1] * N_DEV

    def rows(g):
        return g.reshape(-1, g.shape[-1])

    b16 = lambda a: a.astype(BF16)
    tr = lambda a: jnp.swapaxes(a, -1, -2)
    w_in_t, m_w_in_t, v_w_in_t = tr(w_in), tr(m_w_in), tr(v_w_in)
    w_up_t, m_w_up_t, v_w_up_t = tr(w_up), tr(m_w_up), tr(v_w_up)
    (g_in0, g_ck, g_kf, g_meta) = _exchange([b16(w_in_t[0]), conv_dw_k, ffn_dw_k, meta_tokens], ["gather"] * 4,
                                            "gather_first")
    ck_full = _cols_from_shards(g_ck)
    ck_rows = jnp.broadcast_to(ck_full[:, :, None, :], ck_full.shape[:2] + (SUBLANES, cc))
    kf_full = _cols_from_shards(g_kf)
    meta_full = _cols_from_shards(g_meta)
    am = _head_matrix(cc)
    win, wout, wup, wdown = [None] * depth, [None] * depth, [None] * depth, [None] * depth
    win[0] = rows(g_in0)

    h = (meta_full, x[0])
    saved = []
    for l in range(depth):
        more = l + 1 < depth
        if l == 0:
            (z, hn1), (g_out,) = _norm_proj(h, norm1_g[l:l + 1], win[l], f"in_proj_{l}", tn_cap=1536,
                                            xchg=([b16(w_out[l])], ["gather"]))
            wout[l] = rows(g_out)
            (ymix, u1), (g_up, g_down) = _mixer_fwd(z, ck_rows[l], conv_dw_b[l:l + 1], conv_ln_g[l:l + 1], conv_ln_b[l:l + 1],
                                                    pool_w[l], pool_scale[l:l + 1], am, f"mixer_fwd_{l}",
                                                    xchg=([b16(w_up_t[l]), b16(w_down[l])], ["gather"] * 2))
            wup[l], wdown[l] = rows(g_up), rows(g_down)
        else:
            z, hn1 = _norm_proj(h, norm1_g[l:l + 1], win[l], f"in_proj_{l}", tn_cap=1536)
            ymix, u1 = _mixer_fwd(z, ck_rows[l], conv_dw_b[l:l + 1], conv_ln_g[l:l + 1], conv_ln_b[l:l + 1], pool_w[l],
                                  pool_scale[l:l + 1], am, f"mixer_fwd_{l}")
        if more:
            h_mid, (g_in, g_out) = _mm(ymix, wout[l], f"out_proj_{l}", res=h, tn_cap=1024,
                                       xchg=([b16(w_in_t[l + 1]), b16(w_out[l + 1])], ["gather"] * 2))
            win[l + 1], wout[l + 1] = rows(g_in), rows(g_out)
            nxt = [b16(w_up_t[l + 1]), b16(w_down[l + 1])]
            (h_out, hn2, act, ux, uc), got = _ffn_block_fwd(h_mid, norm2_g[l:l + 1], wup[l], kf_full[l], wdown[l],
                                                            f"ffn_fwd_{l}", xchg=(nxt, ["gather"] * 2))
            wup[l + 1], wdown[l + 1] = rows(got[0]), rows(got[1])
        else:
            h_mid = _mm(ymix, wout[l], f"out_proj_{l}", res=h, tn_cap=1024)
            h_out, hn2, act, ux, uc = _ffn_block_fwd(h_mid, norm2_g[l:l + 1], wup[l], kf_full[l], wdown[l], f"ffn_fwd_{l}")
        saved.append((h, hn1, z, u1, ymix, h_mid, hn2, ux, uc, act))
        h = h_out

    dh, d_final_g, loss_part = _loss_head(h, final_g.reshape(1, D), loss_target[0], n_meta, "loss_head")

    def row_shards(gm):
        return gm.reshape(N_DEV, -1, gm.shape[-1])

    zero_row = jnp.zeros((1, D), F32)
    gw = {k: [None] * depth for k in ("ck", "cb", "lg", "lb", "pw", "ps", "kf", "n1", "n2")}
    parts = {k: [None] * depth for k in ("in", "out", "up", "down")}
    for l in reversed(range(depth)):
        h_in, hn1, z, u1, ymix, h_mid, hn2, ux, uc, act = saved[l]
        g_down = _mm_tn(act, dh, f"down_proj_wgrad_{l}", tq_cap=512)
        (dh_mid, gw["n2"][l], dug0, dkf), (parts["down"][l],) = _ffn_block_bwd(
            dh, h_mid, norm2_g[l:l + 1], ux, uc, kf_full[l], wdown[l], wup[l], f"ffn_bwd_{l}",
            xchg=([row_shards(g_down)], ["a2a"]))
        gw["kf"][l] = jnp.transpose(dkf, (2, 1, 0, 3)).reshape(dkf.shape[2], -1)
        g_up_t = _mm_tn(dug0, hn2, f"up_proj_wgrad_{l}", halves=2, tq_cap=1024)
        dymix = _mm(dh_mid, wout[l], f"out_proj_bwd_{l}", b_t=True, out_dtype=BF16, tn_cap=1024)
        g_out = _mm_tn(ymix, dh_mid, f"out_proj_wgrad_{l}", tq_cap=512)
        ((dz, gw["ck"][l], gw["cb"][l], gw["lg"][l], gw["lb"][l], gw["pw"][l], gw["ps"][l]),
         (parts["up"][l], parts["out"][l])) = _mixer_bwd(
            z, u1, dymix, ck_rows[l], conv_ln_g[l:l + 1], conv_ln_b[l:l + 1], pool_w[l], pool_scale[l:l + 1], am,
            f"mixer_bwd_{l}", xchg=([row_shards(g_up_t), row_shards(g_out)], ["a2a", "a2a"]))
        g_in_t = _mm_tn(dz, hn1, f"in_proj_wgrad_{l}", tq_cap=1024)
        if l > 0:
            (dh, gw["n1"][l]), (parts["in"][l],) = _proj_bwd_norm(dz, win[l], h_in, norm1_g[l:l + 1], dh_mid, zero_row,
                                                                  f"in_proj_bwd_{l}", xchg=([row_shards(g_in_t)], ["a2a"]))
        else:
            meta_rows, x_rows = h_in
            (grad_x, dg_x), (parts["in"][l],) = _proj_bwd_norm(dz, win[l], x_rows, norm1_g[l:l + 1], dh_mid, zero_row,
                                                               f"in_proj_bwd_{l}", skip=n_meta,
                                                               xchg=([row_shards(g_in_t)], ["a2a"]))
            d_meta, gw["n1"][l] = _proj_bwd_norm(dz[:n_meta], win[l], meta_rows, norm1_g[l:l + 1], dh_mid[:n_meta],
                                                 dg_x, f"in_proj_bwd_meta_{l}")
    grad_x = grad_x[None]

    pack_d = jnp.concatenate(gw["n1"] + gw["n2"] + [d_final_g, jnp.broadcast_to(loss_part[:, :1], (1, D)), zero_row, zero_row], axis=0)
    pack_c = jnp.concatenate(gw["cb"] + gw["lg"] + gw["lb"] + gw["ps"], axis=0)
    pack_pw = b16(jnp.stack(gw["pw"]).reshape(depth * ng * gd, gd))
    src = [_cols_to_shards(jnp.stack(gw["ck"])), _cols_to_shards(jnp.stack(gw["kf"])), _cols_to_shards(d_meta),
           pack_d, pack_c, pack_pw]
    r_ck, r_kf, r_meta, r_d, r_c, r_pw = _exchange(src, ["a2a"] * 3 + ["gather"] * 3, "exchange_small_grads")

    big = {
        "w_in": tuple(tr(a) for a in _adamw_big(parts["in"], w_in_t, m_w_in_t, v_w_in_t, "adamw_w_in")),
        "w_out": _adamw_big(parts["out"], w_out, m_w_out, v_w_out, "adamw_w_out"),
        "w_up": tuple(tr(a) for a in _adamw_big(parts["up"], w_up_t, m_w_up_t, v_w_up_t, "adamw_w_up")),
        "w_down": _adamw_big(parts["down"], w_down, m_w_down, v_w_down, "adamw_w_down"),
    }
    kwid = conv_dw_k.shape[1]
    fkw = ffn_dw_k.shape[1]
    row = lambda a: a.reshape(1, -1)
    entries = [
        (r_d, 0, norm1_g, m_norm1_g, v_norm1_g),
        (r_d, depth, norm2_g, m_norm2_g, v_norm2_g),
        (r_d, 2 * depth, row(final_g), row(m_final_g), row(v_final_g)),
        (r_c, 0, conv_dw_b, m_conv_dw_b, v_conv_dw_b),
        (r_c, depth, conv_ln_g, m_conv_ln_g, v_conv_ln_g),
        (r_c, 2 * depth, conv_ln_b, m_conv_ln_b, v_conv_ln_b),
        (r_c, 3 * depth, pool_scale, m_pool_scale, v_pool_scale),
        (r_pw, 0, pool_w.reshape(-1, gd), m_pool_w.reshape(-1, gd), v_pool_w.reshape(-1, gd)),
        (r_ck.reshape(N_DEV, depth * kwid, -1), 0, conv_dw_k.reshape(depth * kwid, -1),
         m_conv_dw_k.reshape(depth * kwid, -1), v_conv_dw_k.reshape(depth * kwid, -1)),
        (r_kf.reshape(N_DEV, depth * fkw, -1), 0, ffn_dw_k.reshape(depth * fkw, -1),
         m_ffn_dw_k.reshape(depth * fkw, -1), v_ffn_dw_k.reshape(depth * fkw, -1)),
        (r_meta, 0, meta_tokens, m_meta_tokens, v_meta_tokens),
        (r_d, 2 * depth + 1, zero_row, zero_row, zero_row),
    ]
    small = _adamw_small(entries, "adamw_small")
    names = ["norm1_g", "norm2_g", "final_g", "conv_dw_b", "conv_ln_g", "conv_ln_b", "pool_scale", "pool_w",
             "conv_dw_k", "ffn_dw_k", "meta_tokens"]
    shapes = {"final_g": final_g.shape, "pool_w": pool_w.shape, "conv_dw_k": conv_dw_k.shape, "ffn_dw_k": ffn_dw_k.shape}
    res = dict(big)
    for nme, quad in zip(names, small[:-1]):
        res[nme] = tuple(a.reshape(shapes[nme]) if nme in shapes else a for a in quad)
    loss = small[-1][0][0, 0]

    order = ["meta_tokens", "norm1_g", "w_in", "conv_dw_k", "conv_dw_b", "conv_ln_g", "conv_ln_b", "pool_w", "pool_scale",
             "w_out", "norm2_g", "w_up", "ffn_dw_k", "w_down", "final_g"]
    return (loss, grad_x, *[res[k][0] for k in order], *[res[k][1] for k in order], *[res[k][2] for k in order],
            *[res[k][3] for k in order])
```

```python
import functools

import jax
import jax.numpy as jnp
from jax import lax
from jax.experimental import pallas as pl
from jax.experimental.pallas import tpu as pltpu

F32 = jnp.float32
BF16 = jnp.bfloat16

EPS = 1e-6
HEAD_DIM = 64
POOL_WINDOWS = (2, 4, 8, 16)
ADAM_LR = 0.001
ADAM_B1 = 0.9
ADAM_B2 = 0.999
ADAM_EPS = 1e-08
ADAM_WD = 0.01
ADAM_STEP = 10

N_DEV = 8
OTHER_CHIPS = (2, 4, 6)
SUBLANES = 8
HALO = 48
CONV_PAD = 32
POOL_PAD = 16
FFN_PAD = 8
ROW_CHUNK = 24
CONV3_ROWS = 48
MAX_TILE_ROWS = 1024
WGRAD_TILE_ROWS = 2816
VMEM_LIMIT = 52 * 1024 * 1024


def _divisor(n, cap, mult):
    best = None
    for d in range(mult, min(n, cap) + 1, mult):
        if n % d == 0:
            best = d
    return n if best is None else best


def _token_tile(L):
    return _divisor(L, MAX_TILE_ROWS, HALO)


def _stat_rows(tl):
    return _divisor(tl, 512, SUBLANES)


def _params(sem=None):
    return pltpu.CompilerParams(dimension_semantics=sem, vmem_limit_bytes=VMEM_LIMIT)


def _rowsum8(x):
    acc = x[0:SUBLANES]
    for k in range(1, x.shape[0] // SUBLANES):
        acc = acc + x[k * SUBLANES:(k + 1) * SUBLANES]
    return acc


def _sigmoid(x):
    return jax.nn.sigmoid(x)


def _dot_nt(a, b):
    return lax.dot_general(a, b, (((1,), (1,)), ((), ())), preferred_element_type=F32)


def _head_mean(x, am_ref):
    bw = am_ref.shape[0]
    am = am_ref[...]
    outs = []
    for blk in range(x.shape[1] // bw):
        xb = x[:, blk * bw:(blk + 1) * bw]
        hi = xb.astype(BF16)
        lo = (xb - hi.astype(F32)).astype(BF16)
        outs.append(jnp.dot(hi, am, preferred_element_type=F32) + jnp.dot(lo, am, preferred_element_type=F32))
    return outs[0] if len(outs) == 1 else jnp.concatenate(outs, axis=-1)


def _xchg_out_shapes(srcs, modes):
    out = []
    for s, m in zip(srcs, modes):
        shp = ((N_DEV,) + tuple(s.shape)) if m == "gather" else tuple(s.shape)
        out.append(jax.ShapeDtypeStruct(shp, s.dtype))
    return out


def _xchg_sems(n):
    return [pltpu.SemaphoreType.DMA((n, N_DEV - 1)), pltpu.SemaphoreType.DMA((n, N_DEV - 1)), pltpu.SemaphoreType.DMA((n,))]


def _xchg_ops(src_refs, out_refs, sems, modes):
    n = len(src_refs)
    send_sems, recv_sems, local_sems = sems
    x, y, c = lax.axis_index("x"), lax.axis_index("y"), lax.axis_index("c")
    me = 4 * x + 2 * y + c

    def peer(d):
        return (x ^ ((d >> 2) & 1), y ^ ((d >> 1) & 1), c ^ (d & 1))

    def peer_id(d):
        px, py, pc = peer(d)
        return 4 * px + 2 * py + pc

    def remote(t, d):
        src = src_refs[t] if modes[t] == "gather" else src_refs[t].at[peer_id(d)]
        return pltpu.make_async_remote_copy(
            src_ref=src, dst_ref=out_refs[t].at[me], send_sem=send_sems.at[t, d - 1], recv_sem=recv_sems.at[t, d - 1],
            device_id=peer(d), device_id_type=pl.DeviceIdType.MESH)

    def arrival(t, d):
        src = src_refs[t] if modes[t] == "gather" else src_refs[t].at[me]
        return pltpu.make_async_remote_copy(
            src_ref=src, dst_ref=out_refs[t].at[peer_id(d)], send_sem=send_sems.at[t, d - 1],
            recv_sem=recv_sems.at[t, d - 1], device_id=peer(d), device_id_type=pl.DeviceIdType.MESH)

    def passed_on(t, d):
        blk = out_refs[t].at[peer_id(d)]
        return pltpu.make_async_remote_copy(
            src_ref=blk, dst_ref=blk, send_sem=send_sems.at[t, d], recv_sem=recv_sems.at[t, d],
            device_id=peer(1), device_id_type=pl.DeviceIdType.MESH)

    def local(t):
        src = src_refs[t] if modes[t] == "gather" else src_refs[t].at[me]
        return pltpu.make_async_copy(src, out_refs[t].at[me], local_sems.at[t])

    def sent_first(t):
        return OTHER_CHIPS + (1,) if modes[t] == "gather" else tuple(range(1, N_DEV))

    def start():
        for t in range(n):
            local(t).start()
        for t in range(n):
            for d in sent_first(t):
                remote(t, d).start()

    gathered = [t for t in range(n) if modes[t] == "gather"]

    def relay():
        for t in gathered:
            for d in OTHER_CHIPS:
                arrival(t, d).wait_recv()
                passed_on(t, d).start()

    def wait():
        for t in range(n):
            for d in range(1, N_DEV):
                if not (modes[t] == "gather" and d in OTHER_CHIPS):
                    arrival(t, d).wait_recv()
        for t in range(n):
            for d in sent_first(t):
                remote(t, d).wait_send()
        for t in gathered:
            for d in OTHER_CHIPS:
                passed_on(t, d).wait_send()
        for t in range(n):
            local(t).wait()

    return start, relay, wait


def _exchange(srcs, modes, name):
    n = len(srcs)

    def body(*refs):
        start, relay, wait = _xchg_ops(refs[:n], refs[n:2 * n], refs[2 * n:], modes)
        start()
        relay()
        wait()

    any_spec = pl.BlockSpec(memory_space=pl.ANY)
    return pl.pallas_call(
        body, name=name, out_shape=tuple(_xchg_out_shapes(srcs, modes)),
        in_specs=[any_spec] * n, out_specs=tuple([any_spec] * n),
        scratch_shapes=_xchg_sems(n),
        compiler_params=pltpu.CompilerParams(has_side_effects=True),
    )(*srcs)


def _call(body, *, name, grid, in_specs, out_specs, out_shape, args, scratch_shapes=(), sem=None, xchg=None):
    single = not isinstance(out_shape, (tuple, list))
    outs_shape = [out_shape] if single else list(out_shape)
    outs_spec = [out_specs] if single else list(out_specs)
    if xchg is None:
        res = pl.pallas_call(
            body, name=name, grid=grid, in_specs=list(in_specs), out_specs=out_specs, out_shape=out_shape,
            scratch_shapes=list(scratch_shapes), compiler_params=_params(sem))(*args)
        return res, ()
    srcs, modes = xchg
    n_in, n_out, n_scr, nx = len(in_specs), len(outs_shape), len(scratch_shapes), len(srcs)
    n_steps = functools.reduce(lambda a, b: a * b, grid, 1)
    relay_step = (3 * n_steps) // 4 if n_steps > 1 else 0

    def wrapped(*refs):
        ins = refs[:n_in]
        xs = refs[n_in:n_in + nx]
        o0 = n_in + nx
        outs = refs[o0:o0 + n_out]
        xo = refs[o0 + n_out:o0 + n_out + nx]
        s0 = o0 + n_out + nx
        scr = refs[s0:s0 + n_scr]
        start, relay, wait = _xchg_ops(xs, xo, refs[s0 + n_scr:], modes)
        step = functools.reduce(lambda acc, a: acc * grid[a] + pl.program_id(a), range(len(grid)), 0)

        @pl.when(step == 0)
        def _():
            start()

        body(*ins, *outs, *scr)

        @pl.when(step == relay_step)
        def _():
            relay()

        @pl.when(step == n_steps - 1)
        def _():
            wait()

    any_spec = pl.BlockSpec(memory_space=pl.ANY)
    res = pl.pallas_call(
        wrapped, name=name, grid=grid, in_specs=list(in_specs) + [any_spec] * nx,
        out_specs=tuple(outs_spec + [any_spec] * nx), out_shape=tuple(outs_shape + _xchg_out_shapes(srcs, modes)),
        scratch_shapes=list(scratch_shapes) + _xchg_sems(nx),
        compiler_params=_params(("arbitrary",) * len(grid)))(*args, *srcs)
    comp = res[:n_out]
    return (comp[0] if single else tuple(comp)), tuple(res[n_out:])


def _seq_rows(h):
    if isinstance(h, tuple):
        return h[0].shape[0] + h[1].shape[0], h[1].shape[1]
    return h.shape


def _seq_tiles(h, tm):
    if not isinstance(h, tuple):
        return [pl.BlockSpec((tm, h.shape[1]), lambda i, *_: (i, 0))], [h], lambda refs, i: refs[0][...]
    meta, x = h
    n, D = meta.shape

    def read(refs, i):
        t = refs[1][...]
        first = jnp.concatenate([refs[0][...], pltpu.roll(t, n, axis=0)[n:]], axis=0)
        return jnp.where(i == 0, first, t)

    window = pl.BlockSpec((pl.Element(tm), pl.Element(D)),
                          lambda i, *_: (pl.multiple_of(jnp.maximum(i * tm - n, 0), SUBLANES), 0))
    return [pl.BlockSpec((n, D), lambda *_: (0, 0)), window], [meta, x], read


def _norm_proj(h, g, w, name, *, tn_cap, xchg=None):
    L, D = _seq_rows(h)
    N = w.shape[0]
    tm = _token_tile(L)
    tn = _divisor(N, tn_cap, 128)
    h_specs, h_args, read_h = _seq_tiles(h, tm)
    nh = len(h_specs)

    def body(*refs):
        g_ref, w_ref, z_ref, hn_ref = refs[nh:]

        @pl.when(pl.program_id(1) == 0)
        def _():
            x = read_h(refs[:nh], pl.program_id(0))
            r = lax.rsqrt(jnp.mean(x * x, axis=-1, keepdims=True) + EPS)
            hn_ref[...] = ((x * r) * g_ref[...]).astype(BF16)

        z_ref[...] = _dot_nt(hn_ref[...], w_ref[...])

    out, xo = _call(
        body, name=name, grid=(L // tm, N // tn),
        in_specs=h_specs + [pl.BlockSpec((1, D), lambda i, j: (0, 0)), pl.BlockSpec((tn, D), lambda i, j: (j, 0))],
        out_specs=(pl.BlockSpec((tm, tn), lambda i, j: (i, j)), pl.BlockSpec((tm, D), lambda i, j: (i, 0))),
        out_shape=(jax.ShapeDtypeStruct((L, N), F32), jax.ShapeDtypeStruct((L, D), BF16)),
        sem=("parallel", "arbitrary"), args=(*h_args, g, w), xchg=xchg)
    return out if xchg is None else (out, xo)


def _proj_bwd_norm(a, b, h, g, dres, dg0, name, skip=0, xchg=None):
    L, K = a.shape
    D = b.shape[1]
    rows = L - skip
    tm = _divisor(rows, MAX_TILE_ROWS, 2 * SUBLANES) if skip else _token_tile(L)

    def body(a_ref, b_ref, h_ref, g_ref, dres_ref, dg0_ref, dh_ref, dg_ref):
        i = pl.program_id(0)
        dhn = jnp.dot(a_ref[...], b_ref[...], preferred_element_type=F32)
        x = h_ref[...]
        r = lax.rsqrt(jnp.mean(x * x, axis=-1, keepdims=True) + EPS)
        xhat = x * r
        dxhat = dhn * g_ref[...]
        dh_ref[...] = dres_ref[...] + r * (dxhat - xhat * jnp.mean(dxhat * xhat, axis=-1, keepdims=True))
        part = jnp.sum(_rowsum8(dhn * xhat), axis=0, keepdims=True)

        @pl.when(i == 0)
        def _():
            dg_ref[...] = dg0_ref[...] + part

        @pl.when(i > 0)
        def _():
            dg_ref[...] += part

    def rows_of(cols, first=skip):
        if not first:
            return pl.BlockSpec((tm, cols), lambda i: (i, 0))
        return pl.BlockSpec((pl.Element(tm), pl.Element(cols)), lambda i: (pl.multiple_of(first + i * tm, SUBLANES), 0))

    row = pl.BlockSpec((1, D), lambda i: (0, 0))
    h_first = skip if h.shape[0] == L else 0
    out, xo = _call(
        body, name=name, grid=(rows // tm,),
        in_specs=[rows_of(K), pl.BlockSpec((K, D), lambda i: (0, 0)), rows_of(D, h_first), row, rows_of(D), row],
        out_specs=(pl.BlockSpec((tm, D), lambda i: (i, 0)), row),
        out_shape=(jax.ShapeDtypeStruct((rows, D), F32), jax.ShapeDtypeStruct((1, D), F32)),
        sem=("arbitrary",), args=(a, b, h, g, dres, dg0), xchg=xchg)
    return out if xchg is None else (out, xo)


def _mm(a, b, name, *, res=None, b_t=False, out_dtype=F32, tn_cap=1408, xchg=None):
    M, K = a.shape
    N = b.shape[0] if b_t else b.shape[1]
    tm = _token_tile(M)
    tn = _divisor(N, tn_cap, 128)
    if isinstance(res, tuple):
        assert tn == N
        r_specs, r_args, read_r = _seq_tiles(res, tm)
    elif res is not None:
        r_specs, r_args, read_r = [pl.BlockSpec((tm, tn), lambda i, j: (i, j))], [res], lambda refs, i: refs[0][...]
    else:
        r_specs, r_args, read_r = [], [], None

    def body(*refs):
        a_ref, b_ref, o_ref = refs[0], refs[1], refs[-1]
        av = a_ref[...].astype(BF16)
        prod = _dot_nt(av, b_ref[...]) if b_t else jnp.dot(av, b_ref[...], preferred_element_type=F32)
        o_ref[...] = (prod if read_r is None else prod + read_r(refs[2:-1], pl.program_id(0))).astype(out_dtype)

    b_spec = pl.BlockSpec((tn, K), lambda i, j: (j, 0)) if b_t else pl.BlockSpec((K, tn), lambda i, j: (0, j))
    in_specs = [pl.BlockSpec((tm, K), lambda i, j: (i, 0)), b_spec] + r_specs
    args = [a, b] + r_args
    out, xo = _call(
        body, name=name, grid=(M // tm, N // tn), in_specs=in_specs,
        out_specs=pl.BlockSpec((tm, tn), lambda i, j: (i, j)), out_shape=jax.ShapeDtypeStruct((M, N), out_dtype),
        sem=("parallel", "parallel"), args=args, xchg=xchg)
    return out if xchg is None else (out, xo)


def _mm_tn(a, b, name, *, halves=1, tq_cap=1408):
    L, Q = b.shape
    ph = a.shape[-1]
    P = ph * halves
    tl = _divisor(L, WGRAD_TILE_ROWS, HALO)
    tp = _divisor(ph, 1408, 128)
    tq = _divisor(Q, tq_cap, 128)
    pper = ph // tp
    nl = L // tl
    grid = (P // tp, Q // tq, nl)

    def body(a_ref, b_ref, o_ref, acc):
        prod = lax.dot_general(a_ref[...].astype(BF16), b_ref[...].astype(BF16), (((0,), (0,)), ((), ())),
                               preferred_element_type=F32)
        l = pl.program_id(2)
        if nl == 1:
            o_ref[...] = prod.astype(BF16)
            return

        @pl.when(l == 0)
        def _():
            acc[...] = prod

        @pl.when(jnp.logical_and(l > 0, l < nl - 1))
        def _():
            acc[...] += prod

        @pl.when(l == nl - 1)
        def _():
            o_ref[...] = (acc[...] + prod).astype(BF16)

    if halves > 1:
        a_spec = pl.BlockSpec((None, tl, tp), lambda p, q, l: (p // pper, l, p % pper))
    else:
        a_spec = pl.BlockSpec((tl, tp), lambda p, q, l: (l, p))
    return pl.pallas_call(
        body, name=name, grid=grid,
        in_specs=[a_spec, pl.BlockSpec((tl, tq), lambda p, q, l: (l, q))],
        out_specs=pl.BlockSpec((tp, tq), lambda p, q, l: (p, q)),
        out_shape=jax.ShapeDtypeStruct((P, Q), BF16),
        scratch_shapes=[pltpu.VMEM((tp, tq), F32)],
        compiler_params=_params(("parallel", "parallel", "arbitrary")),
    )(a, b)


def _loss_head(h, g, tgt, n_meta, name):
    L, D = h.shape
    tl = _token_tile(L)
    nt = L // tl

    def body(h_ref, g_ref, t_ref, dh_ref, dg_ref, loss_ref):
        i = pl.program_id(0)
        x = h_ref[...]
        r = lax.rsqrt(jnp.mean(x * x, axis=-1, keepdims=True) + EPS)
        xhat = x * r
        gg = g_ref[...]
        y = xhat * gg
        rows = i * tl + lax.broadcasted_iota(jnp.int32, (tl, 1), 0)
        t = t_ref[...]
        t = jnp.where(i == 0, pltpu.roll(t, n_meta, axis=0), t)
        err = jnp.where(rows >= n_meta, y - t, 0.0)
        dy = err * (1.0 / D)
        dxhat = dy * gg
        dh_ref[...] = r * (dxhat - xhat * jnp.mean(dxhat * xhat, axis=-1, keepdims=True))
        dg_part = jnp.sum(_rowsum8(dy * xhat), axis=0, keepdims=True)
        per_row = jnp.mean(err * err, axis=-1, keepdims=True)
        loss_part = jnp.broadcast_to(0.5 * jnp.sum(per_row, axis=0, keepdims=True), (1, 128))

        @pl.when(i == 0)
        def _():
            dg_ref[...] = dg_part
            loss_ref[...] = loss_part

        @pl.when(i > 0)
        def _():
            dg_ref[...] += dg_part
            loss_ref[...] += loss_part

    tile = pl.BlockSpec((tl, D), lambda i: (i, 0))
    row = pl.BlockSpec((1, D), lambda i: (0, 0))
    window = pl.BlockSpec((pl.Element(tl), pl.Element(D)),
                          lambda i: (pl.multiple_of(jnp.maximum(i * tl - n_meta, 0), SUBLANES), 0))
    return pl.pallas_call(
        body, name=name, grid=(nt,), in_specs=[tile, row, window],
        out_specs=(tile, row, pl.BlockSpec((1, 128), lambda i: (0, 0))),
        out_shape=(jax.ShapeDtypeStruct((L, D), F32), jax.ShapeDtypeStruct((1, D), F32),
                   jax.ShapeDtypeStruct((1, 128), F32)),
        compiler_params=_params(("arbitrary",)),
    )(h, g, tgt)


def _pool_fwd_block(pwin, pw_ref, row0, rb, g, gd, w, t0):
    wv = pwin[pl.ds(row0 + HALO - POOL_PAD, rb + POOL_PAD), g * gd:(g + 1) * gd]
    s = wv
    sh = 1
    while sh < w:
        s = s + pltpu.roll(s, sh, axis=0)
        sh *= 2
    win = s[POOL_PAD:POOL_PAD + rb]
    pt = wv[POOL_PAD:POOL_PAD + rb]
    tg = t0 + lax.broadcasted_iota(jnp.int32, (rb, 1), 0)
    cnt = jnp.minimum(tg + 1, w).astype(F32)
    return win / cnt - pt


def _fill_windows(i, zp_ref, zc_ref, u0w, pwin, tl, cc):
    keep = i > 0
    zp = zp_ref[...]
    u0w[0:HALO, :] = jnp.where(keep, zp[:, :cc] * _sigmoid(zp[:, cc:2 * cc]), 0.0)
    pwin[0:HALO, :] = jnp.where(keep, zp[:, 2 * cc:], 0.0)

    def fill(c, carry):
        b = pl.multiple_of(c * ROW_CHUNK, SUBLANES)
        zc = zc_ref[pl.ds(b, ROW_CHUNK), :]
        u0w[pl.ds(HALO + b, ROW_CHUNK), :] = zc[:, :cc] * _sigmoid(zc[:, cc:2 * cc])
        pwin[pl.ds(HALO + b, ROW_CHUNK), :] = zc[:, 2 * cc:]
        return carry

    lax.fori_loop(0, tl // ROW_CHUNK, fill, 0)


def _mixer_fwd(z, ck, cb, lg, lb, pw, ps, am, name, xchg=None):
    L, ci = z.shape
    kw, _, cc = ck.shape
    cp = ci - 2 * cc
    ng, gd = pw.shape[0], pw.shape[1]
    tl = _token_tile(L)
    nt = L // tl
    hb = tl // HALO
    rb = _stat_rows(tl)
    tap0 = CONV_PAD - (kw - 1)

    def body(zp_ref, zc_ref, ck_ref, cb_ref, lg_ref, lb_ref, pw_ref, ps_ref, am_ref, y_ref, u1_ref, u0w, pwin):
        i = pl.program_id(0)
        _fill_windows(i, zp_ref, zc_ref, u0w, pwin, tl, cc)

        def conv(c, carry):
            b = pl.multiple_of(c * ROW_CHUNK, SUBLANES)
            w = u0w[pl.ds(b + HALO - CONV_PAD, ROW_CHUNK + CONV_PAD), :]
            acc = jnp.broadcast_to(cb_ref[...], (ROW_CHUNK, cc))
            for j in range(kw):
                acc = acc + _rows_of(ck_ref[j], ROW_CHUNK) * w[tap0 + j:tap0 + j + ROW_CHUNK]
            u1_ref[pl.ds(b, ROW_CHUNK), :] = acc
            return carry

        lax.fori_loop(0, tl // ROW_CHUNK, conv, 0)

        def blocks(k, carry):
            b = pl.multiple_of(k * rb, SUBLANES)
            u1 = u1_ref[pl.ds(b, rb), :]
            xc = u1 - _head_mean(u1, am_ref)
            var = _head_mean(xc * xc, am_ref)
            u2 = (xc * lax.rsqrt(var + EPS)) * lg_ref[...] + lb_ref[...]
            y_ref[pl.ds(b, rb), 0:cc] = (u2 * _sigmoid(u2)).astype(y_ref.dtype)
            for g in range(ng):
                d = _pool_fwd_block(pwin, pw_ref, b, rb, g, gd, POOL_WINDOWS[g], i * tl + b)
                yp = jnp.dot(d.astype(BF16), pw_ref[g].astype(BF16), preferred_element_type=F32)
                yp = yp * ps_ref[:, g * gd:(g + 1) * gd]
                y_ref[pl.ds(b, rb), cc + g * gd:cc + (g + 1) * gd] = yp.astype(y_ref.dtype)
            return carry

        lax.fori_loop(0, tl // rb, blocks, 0)

    def full(a):
        nd = a.ndim
        return pl.BlockSpec(a.shape, lambda i: (0,) * nd)

    out, xo = _call(
        body, name=name, grid=(nt,),
        in_specs=[pl.BlockSpec((HALO, ci), lambda i: (jnp.maximum(i * hb - 1, 0), 0)),
                  pl.BlockSpec((tl, ci), lambda i: (i, 0)),
                  full(ck), full(cb), full(lg), full(lb), full(pw), full(ps), full(am)],
        out_specs=(pl.BlockSpec((tl, cc + cp), lambda i: (i, 0)), pl.BlockSpec((tl, cc), lambda i: (i, 0))),
        out_shape=(jax.ShapeDtypeStruct((L, cc + cp), BF16), jax.ShapeDtypeStruct((L, cc), F32)),
        scratch_shapes=[pltpu.VMEM((HALO + tl, cc), F32), pltpu.VMEM((HALO + tl, cp), F32)],
        sem=("parallel",), args=(z, z, ck, cb, lg, lb, pw, ps, am), xchg=xchg)
    return out if xchg is None else (out, xo)


def _mixer_bwd(z, u1, dy, ck, lg, lb, pw, ps, am, name, xchg=None):
    L, ci = z.shape
    kw, _, cc = ck.shape
    cp = ci - 2 * cc
    ng, gd = pw.shape[0], pw.shape[1]
    tl = _token_tile(L)
    nt = L // tl
    hb = tl // HALO
    rb = _stat_rows(tl)

    def body(zp_ref, zc_ref, u1c_ref, u1n_ref, dyc_ref, dyn_ref, ck_ref, lg_ref, lb_ref, pw_ref, ps_ref, am_ref,
             dz_ref, dck_ref, dcb_ref, dlg_ref, dlb_ref, dpw_ref, dps_ref,
             u0w, pwin, du1w, ddw, ew, dkacc, dcb8, dlg8, dlb8, dps8):
        i = pl.program_id(0)
        has_next = i < nt - 1

        @pl.when(i == 0)
        def _():
            dck_ref[...] = jnp.zeros_like(dck_ref)
            dcb_ref[...] = jnp.zeros_like(dcb_ref)
            dlg_ref[...] = jnp.zeros_like(dlg_ref)
            dlb_ref[...] = jnp.zeros_like(dlb_ref)
            dpw_ref[...] = jnp.zeros_like(dpw_ref)
            dps_ref[...] = jnp.zeros_like(dps_ref)

        dkacc[...] = jnp.zeros_like(dkacc)
        dcb8[...] = jnp.zeros_like(dcb8)
        dlg8[...] = jnp.zeros_like(dlg8)
        dlb8[...] = jnp.zeros_like(dlb8)
        dps8[...] = jnp.zeros_like(dps8)

        _fill_windows(i, zp_ref, zc_ref, u0w, pwin, tl, cc)

        def conv_side(u1, dyc, own):
            xc = u1 - _head_mean(u1, am_ref)
            rstd = lax.rsqrt(_head_mean(xc * xc, am_ref) + EPS)
            uh = xc * rstd
            lgv = lg_ref[...]
            u2 = uh * lgv + lb_ref[...]
            sg = _sigmoid(u2)
            du2 = dyc * (sg * (1.0 + u2 * (1.0 - sg)))
            if own:
                dlg8[...] += _rowsum8(du2 * uh)
                dlb8[...] += _rowsum8(du2)
            duh = du2 * lgv
            return rstd * (duh - _head_mean(duh, am_ref) - uh * _head_mean(duh * uh, am_ref))

        def pool_side(dyp, t0, rows):
            dds, es = [], []
            tg = t0 + lax.broadcasted_iota(jnp.int32, (rows, 1), 0)
            for g in range(ng):
                dypre = dyp[:, g * gd:(g + 1) * gd] * ps_ref[:, g * gd:(g + 1) * gd]
                dd = lax.dot_general(dypre.astype(BF16), pw_ref[g].astype(BF16), (((1,), (1,)), ((), ())),
                                     preferred_element_type=F32)
                cnt = jnp.minimum(tg + 1, POOL_WINDOWS[g]).astype(F32)
                dds.append(dd)
                es.append(dd / cnt)
            return jnp.concatenate(dds, axis=-1), jnp.concatenate(es, axis=-1)

        def blocks(k, carry):
            b = pl.multiple_of(k * rb, SUBLANES)
            dyb = dyc_ref[pl.ds(b, rb), :].astype(F32)
            du1 = conv_side(u1c_ref[pl.ds(b, rb), :], dyb[:, :cc], True)
            du1w[pl.ds(b, rb), :] = du1
            dcb8[...] += _rowsum8(du1)
            dyp = dyb[:, cc:]
            dd, e = pool_side(dyp, i * tl + b, rb)
            ddw[pl.ds(b, rb), :] = dd
            ew[pl.ds(b, rb), :] = e
            for g in range(ng):
                d = _pool_fwd_block(pwin, pw_ref, b, rb, g, gd, POOL_WINDOWS[g], i * tl + b)
                db16 = d.astype(BF16)
                dypg = dyp[:, g * gd:(g + 1) * gd]
                ypre = jnp.dot(db16, pw_ref[g].astype(BF16), preferred_element_type=F32)
                dps8[:, g * gd:(g + 1) * gd] += _rowsum8(dypg * ypre)
                dypre = (dypg * ps_ref[:, g * gd:(g + 1) * gd]).astype(BF16)
                dpw_ref[g] += lax.dot_general(db16, dypre, (((0,), (0,)), ((), ())), preferred_element_type=F32)
            return carry

        lax.fori_loop(0, tl // rb, blocks, 0)

        dyn = dyn_ref[...].astype(F32)
        du1n = conv_side(u1n_ref[...], dyn[:, :cc], False)
        du1w[tl:tl + HALO, :] = jnp.where(has_next, du1n, 0.0)
        ddn, en = pool_side(dyn[:, cc:], (i + 1) * tl, HALO)
        ew[tl:tl + HALO, :] = jnp.where(has_next, en, 0.0)

        def taps(c, carry):
            b = pl.multiple_of(c * ROW_CHUNK, SUBLANES)
            w = du1w[pl.ds(b, ROW_CHUNK + CONV_PAD), :]
            u0c = u0w[pl.ds(HALO + b, ROW_CHUNK), :]
            acc = jnp.zeros((ROW_CHUNK, cc), F32)
            for j in range(kw):
                o = kw - 1 - j
                sh = w[o:o + ROW_CHUNK]
                acc = acc + _rows_of(ck_ref[j], ROW_CHUNK) * sh
                dkacc[j] += _rowsum8(u0c * sh)
            zc = zc_ref[pl.ds(b, ROW_CHUNK), :]
            a = zc[:, :cc]
            sg = _sigmoid(zc[:, cc:2 * cc])
            dz_ref[pl.ds(b, ROW_CHUNK), 0:cc] = (acc * sg).astype(dz_ref.dtype)
            dz_ref[pl.ds(b, ROW_CHUNK), cc:2 * cc] = (acc * a * sg * (1.0 - sg)).astype(dz_ref.dtype)
            return carry

        lax.fori_loop(0, tl // ROW_CHUNK, taps, 0)

        def pool_back(k, carry):
            b = pl.multiple_of(k * rb, SUBLANES)
            n = rb + POOL_PAD
            for g in range(ng):
                s = ew[pl.ds(b, n), g * gd:(g + 1) * gd]
                sh = 1
                while sh < POOL_WINDOWS[g]:
                    s = s + pltpu.roll(s, n - sh, axis=0)
                    sh *= 2
                dp = s[0:rb] - ddw[pl.ds(b, rb), g * gd:(g + 1) * gd]
                dz_ref[pl.ds(b, rb), 2 * cc + g * gd:2 * cc + (g + 1) * gd] = dp.astype(dz_ref.dtype)
            return carry

        lax.fori_loop(0, tl // rb, pool_back, 0)

        dck_ref[...] += jnp.sum(dkacc[...], axis=1)
        dcb_ref[...] += jnp.sum(dcb8[...], axis=0, keepdims=True)
        dlg_ref[...] += jnp.sum(dlg8[...], axis=0, keepdims=True)
        dlb_ref[...] += jnp.sum(dlb8[...], axis=0, keepdims=True)
        dps_ref[...] += jnp.sum(dps8[...], axis=0, keepdims=True)

    def full(a):
        nd = a.ndim
        return pl.BlockSpec(a.shape, lambda i: (0,) * nd)

    nhb = L // HALO

    def prev_map(i):
        return (jnp.maximum(i * hb - 1, 0), 0)

    def next_map(i):
        return (jnp.minimum((i + 1) * hb, nhb - 1), 0)

    dcc = cc + cp
    row_cc = jax.ShapeDtypeStruct((1, cc), F32)
    out_shape = (jax.ShapeDtypeStruct((L, ci), BF16), jax.ShapeDtypeStruct((kw, cc), F32), row_cc, row_cc, row_cc,
                 jax.ShapeDtypeStruct((ng, gd, gd), F32), jax.ShapeDtypeStruct((1, cp), F32))
    acc_spec = [pl.BlockSpec((kw, cc), lambda i: (0, 0))] + [pl.BlockSpec((1, cc), lambda i: (0, 0))] * 3 + [
        pl.BlockSpec((ng, gd, gd), lambda i: (0, 0, 0)), pl.BlockSpec((1, cp), lambda i: (0, 0))]
    out, xo = _call(
        body, name=name, grid=(nt,),
        in_specs=[pl.BlockSpec((HALO, ci), prev_map), pl.BlockSpec((tl, ci), lambda i: (i, 0)),
                  pl.BlockSpec((tl, cc), lambda i: (i, 0)), pl.BlockSpec((HALO, cc), next_map),
                  pl.BlockSpec((tl, dcc), lambda i: (i, 0)), pl.BlockSpec((HALO, dcc), next_map),
                  full(ck), full(lg), full(lb), full(pw), full(ps), full(am)],
        out_specs=tuple([pl.BlockSpec((tl, ci), lambda i: (i, 0))] + acc_spec),
        out_shape=out_shape,
        scratch_shapes=[pltpu.VMEM((HALO + tl, cc), F32), pltpu.VMEM((HALO + tl, cp), F32),
                        pltpu.VMEM((tl + HALO, cc), F32), pltpu.VMEM((tl, cp), F32), pltpu.VMEM((tl + HALO, cp), F32),
                        pltpu.VMEM((kw, SUBLANES, cc), F32), pltpu.VMEM((SUBLANES, cc), F32),
                        pltpu.VMEM((SUBLANES, cc), F32), pltpu.VMEM((SUBLANES, cc), F32), pltpu.VMEM((SUBLANES, cp), F32)],
        sem=("arbitrary",), args=(z, z, u1, u1, dy, dy, ck, lg, lb, pw, ps, am), xchg=xchg)
    return out if xchg is None else (out, xo)


def _row_parts(nc, n=3):
    n = min(n, nc)
    cuts = [round(k * nc / n) for k in range(n + 1)]
    return [(cuts[k], cuts[k + 1]) for k in range(n)]


def _tap_rows(k_ref):
    return [jnp.broadcast_to(k_ref[j:j + 1, :], (SUBLANES, k_ref.shape[1])) for j in range(k_ref.shape[0])]


def _rows_of(tap, n):
    return tap if n == SUBLANES else jnp.concatenate([tap] * (n // SUBLANES), axis=0)


def _ffn_conv(win, taps, rows):
    kw = len(taps)
    o = FFN_PAD - (kw - 1)
    acc = _rows_of(taps[0], rows) * win[o:o + rows]
    for j in range(1, kw):
        acc = acc + _rows_of(taps[j], rows) * win[o + j:o + j + rows]
    return acc


def _ffn_block_fwd(h_mid, g, wup_t, kf, wdown, name, xchg=None):
    L, D = h_mid.shape
    f = wdown.shape[0]
    kw = kf.shape[0]
    tl = _token_tile(L)
    tc = _divisor(f, 256, 128)
    nj = f // tc
    nt = L // tl
    pad = 2 * SUBLANES
    hb = tl // pad
    rc = CONV3_ROWS
    parts = _row_parts(tl // rc)

    def body(hp_ref, hc_ref, g_ref, wg_ref, wv_ref, kg_ref, kv_ref, wd_ref, out_ref, hn_ref, act_ref, ux_ref, uc_ref,
             hn_halo, halo, ug_ref, acc):
        i = pl.program_id(0)
        kb = pl.program_id(1)

        @pl.when(kb == 0)
        def _():
            gg = g_ref[...]

            def norm(x):
                r = lax.rsqrt(jnp.mean(x * x, axis=-1, keepdims=True) + EPS)
                return ((x * r) * gg).astype(BF16)

            hn_halo[...] = jnp.where(i > 0, norm(hp_ref[...]), jnp.zeros((pad, D), BF16))
            hn_ref[...] = norm(hc_ref[...])
            acc[...] = jnp.zeros_like(acc)

        w_refs = (wg_ref, wv_ref)
        taps = (_tap_rows(kg_ref), _tap_rows(kv_ref))
        hh = hn_halo[...]
        for h in range(2):
            halo[h] = _dot_nt(hh, w_refs[h][...])[pad - FFN_PAD:]

        def up_part(lo, hi):
            a, b = lo * rc, hi * rc
            for h in range(2):
                ug_ref[h, a:b, :] = _dot_nt(hn_ref[a:b, :], w_refs[h][...])

        def down_part(lo, hi):
            a, b = lo * rc, hi * rc
            acc[a:b, :] += jnp.dot(act_ref[a:b, :], wd_ref[...], preferred_element_type=F32)

        def chunk_rows(lo, hi):
            for c in range(lo, hi):
                r0 = c * rc
                convd = []
                for h in range(2):
                    if c == 0:
                        win = jnp.concatenate([halo[h], ug_ref[h, 0:rc]], axis=0)
                    else:
                        win = ug_ref[h, r0 - FFN_PAD:r0 + rc]
                    convd.append(_ffn_conv(win, taps[h], rc))
                    ux_ref[h, r0:r0 + rc, :] = win[FFN_PAD:].astype(BF16)
                    uc_ref[h, r0:r0 + rc, :] = convd[h].astype(BF16)
                gate, val = convd
                act_ref[r0:r0 + rc, :] = ((gate * _sigmoid(gate)) * val).astype(BF16)

        for p, (lo, hi) in enumerate(parts):
            if p == 0:
                up_part(lo, hi)
            if p + 1 < len(parts):
                up_part(*parts[p + 1])
            if p > 0:
                down_part(*parts[p - 1])
            chunk_rows(lo, hi)
        down_part(*parts[-1])

        @pl.when(kb == nj - 1)
        def _():
            out_ref[...] = acc[...] + hc_ref[...]

    out, xo = _call(
        body, name=name, grid=(nt, nj),
        in_specs=[pl.BlockSpec((pad, D), lambda i, k: (jnp.maximum(i * hb - 1, 0), 0)),
                  pl.BlockSpec((tl, D), lambda i, k: (i, 0)),
                  pl.BlockSpec((1, D), lambda i, k: (0, 0)),
                  pl.BlockSpec((tc, D), lambda i, k: (k, 0)), pl.BlockSpec((tc, D), lambda i, k: (k + nj, 0)),
                  pl.BlockSpec((kw, tc), lambda i, k: (0, k)), pl.BlockSpec((kw, tc), lambda i, k: (0, k + nj)),
                  pl.BlockSpec((tc, D), lambda i, k: (k, 0))],
        out_specs=(pl.BlockSpec((tl, D), lambda i, k: (i, 0)), pl.BlockSpec((tl, D), lambda i, k: (i, 0)),
                   pl.BlockSpec((tl, tc), lambda i, k: (i, k)),
                   pl.BlockSpec((2, tl, tc), lambda i, k: (0, i, k)), pl.BlockSpec((2, tl, tc), lambda i, k: (0, i, k))),
        out_shape=(jax.ShapeDtypeStruct((L, D), F32), jax.ShapeDtypeStruct((L, D), BF16),
                   jax.ShapeDtypeStruct((L, f), BF16),
                   jax.ShapeDtypeStruct((2, L, f), BF16), jax.ShapeDtypeStruct((2, L, f), BF16)),
        scratch_shapes=[pltpu.VMEM((pad, D), BF16), pltpu.VMEM((2, FFN_PAD, tc), F32), pltpu.VMEM((2, tl, tc), F32),
                        pltpu.VMEM((tl, D), F32)],
        sem=("parallel", "arbitrary"), args=(h_mid, h_mid, g, wup_t, wup_t, kf, kf, wdown), xchg=xchg)
    return out if xchg is None else (out, xo)


def _ffn_block_bwd(dh, h_mid, g, ux, uc, kf, wdown, wup_t, name, xchg=None):
    L, D = dh.shape
    f = ux.shape[2]
    kw = kf.shape[0]
    tl = _token_tile(L)
    tc = _divisor(f, 256, 128)
    nj = f // tc
    nt = L // tl
    pad = 2 * SUBLANES
    rc = CONV3_ROWS
    nc = tl // rc
    parts = _row_parts(nc)

    def body(dhc_ref, dhn_ref, hm_ref, g_ref, xg_ref, xv_ref, cg_ref, cgn_ref, cv_ref, cvn_ref, kg_ref, kv_ref,
             wd_ref, wg_ref, wv_ref, dhm_ref, dg_ref, du_ref, dk_ref, dh_ext, dact_s, acc):
        i = pl.program_id(0)
        kb = pl.program_id(1)

        @pl.when(kb == 0)
        def _():
            dh_ext[0:tl, :] = dhc_ref[...].astype(BF16)
            dh_ext[tl:tl + pad, :] = dhn_ref[...].astype(BF16)
            acc[...] = jnp.zeros_like(acc)

        @pl.when(jnp.logical_and(i == 0, kb == 0))
        def _():
            dg_ref[...] = jnp.zeros_like(dg_ref)
            dk_ref[...] = jnp.zeros_like(dk_ref)

        x_refs, c_refs, nxt = (xg_ref, xv_ref), (cg_ref, cv_ref), (cgn_ref, cvn_ref)
        taps = (_tap_rows(kg_ref), _tap_rows(kv_ref))
        dk = [[jnp.zeros((SUBLANES, tc), F32) for _ in range(kw)] for _ in range(2)]

        def dact_part(lo, hi):
            a, b = lo * rc, hi * rc + pad
            dact_s[a:b, :] = _dot_nt(dh_ext[a:b, :], wd_ref[...])

        def dhn_part(lo, hi):
            a, b = lo * rc, hi * rc
            acc[a:b, :] += (jnp.dot(du_ref[0, a:b, :], wg_ref[...], preferred_element_type=F32)
                            + jnp.dot(du_ref[1, a:b, :], wv_ref[...], preferred_element_type=F32))

        for p, (lo, hi) in enumerate(parts):
            if p == 0:
                dact_part(lo, hi)
            if p + 1 < len(parts):
                dact_part(*parts[p + 1])
            if p > 0:
                dhn_part(*parts[p - 1])
            chunk_rows(lo, hi, x_refs, c_refs, nxt, taps, dk, i, dact_s, du_ref)
        dhn_part(*parts[-1])
        for h in range(2):
            for j in range(kw):
                dk_ref[kb, h, j:j + 1, :] += jnp.sum(dk[h][j], axis=0, keepdims=True)

        @pl.when(kb == nj - 1)
        def _():
            x = hm_ref[...]
            r = lax.rsqrt(jnp.mean(x * x, axis=-1, keepdims=True) + EPS)
            xhat = x * r
            dhn = acc[...]
            dxhat = dhn * g_ref[...]
            dhm_ref[...] = dhc_ref[...] + r * (dxhat - xhat * jnp.mean(dxhat * xhat, axis=-1, keepdims=True))
            dg_ref[...] += jnp.sum(_rowsum8(dhn * xhat), axis=0, keepdims=True)

    def chunk_rows(lo, hi, x_refs, c_refs, nxt, taps, dk, i, dact_s, du_ref):
        for c in range(lo, hi):
            r0 = c * rc
            n = rc + FFN_PAD
            convd = []
            for h in range(2):
                if c == nc - 1:
                    rows = jnp.concatenate([c_refs[h][r0:r0 + rc, :], nxt[h][...]], axis=0)
                else:
                    rows = c_refs[h][r0:r0 + rc + pad, :]
                convd.append(rows.astype(F32)[0:n])
            gate, val = convd
            xs = [x_refs[h][r0:r0 + rc, :].astype(F32) for h in range(2)]
            dact = dact_s[r0:r0 + n, :]
            sg = _sigmoid(gate)
            dcs = [dact * val * (sg * (1.0 + gate * (1.0 - sg))), dact * (gate * sg)]
            if c == nc - 1:
                live = jnp.logical_or(lax.broadcasted_iota(jnp.int32, (n, 1), 0) < rc, i < nt - 1)
                dcs = [jnp.where(live, d, 0.0) for d in dcs]
            for h in range(2):
                xc = xs[h]
                dx = None
                for j in range(kw):
                    o = kw - 1 - j
                    sh = dcs[h][o:o + rc]
                    term = _rows_of(taps[h][j], rc) * sh
                    dx = term if dx is None else dx + term
                    dk[h][j] = dk[h][j] + _rowsum8(xc * sh)
                du_ref[h, r0:r0 + rc, :] = dx.astype(BF16)

    def after(i):
        return jnp.minimum((i + 1) * (tl // pad), L // pad - 1)

    def half(h, rows, idx):
        return pl.BlockSpec((None, rows, tc), lambda i, k: (h,) + idx(i, k))

    def tile(i, k):
        return (i, k)

    def behind(i, k):
        return (after(i), k)

    out, xo = _call(
        body, name=name, grid=(nt, nj),
        in_specs=[pl.BlockSpec((tl, D), lambda i, k: (i, 0)),
                  pl.BlockSpec((pad, D), lambda i, k: (after(i), 0)),
                  pl.BlockSpec((tl, D), lambda i, k: (i, 0)), pl.BlockSpec((1, D), lambda i, k: (0, 0)),
                  half(0, tl, tile), half(1, tl, tile),
                  half(0, tl, tile), half(0, pad, behind), half(1, tl, tile), half(1, pad, behind),
                  pl.BlockSpec((kw, tc), lambda i, k: (0, k)), pl.BlockSpec((kw, tc), lambda i, k: (0, k + nj)),
                  pl.BlockSpec((tc, D), lambda i, k: (k, 0)),
                  pl.BlockSpec((tc, D), lambda i, k: (k, 0)), pl.BlockSpec((tc, D), lambda i, k: (k + nj, 0))],
        out_specs=(pl.BlockSpec((tl, D), lambda i, k: (i, 0)), pl.BlockSpec((1, D), lambda i, k: (0, 0)),
                   pl.BlockSpec((2, tl, tc), lambda i, k: (0, i, k)),
                   pl.BlockSpec((nj, 2, kw, tc), lambda i, k: (0, 0, 0, 0))),
        out_shape=(jax.ShapeDtypeStruct((L, D), F32), jax.ShapeDtypeStruct((1, D), F32),
                   jax.ShapeDtypeStruct((2, L, f), BF16), jax.ShapeDtypeStruct((nj, 2, kw, tc), F32)),
        scratch_shapes=[pltpu.VMEM((tl + pad, D), BF16), pltpu.VMEM((tl + pad, tc), F32), pltpu.VMEM((tl, D), F32)],
        sem=("arbitrary", "arbitrary"), args=(dh, dh, h_mid, g, ux, ux, uc, uc, uc, uc, kf, kf, wdown, wup_t, wup_t),
        xchg=xchg)
    return out if xchg is None else (out, xo)


def _adamw_math(w, g, m, v):
    m = ADAM_B1 * m + (1.0 - ADAM_B1) * g
    v = ADAM_B2 * v + (1.0 - ADAM_B2) * (g * g)
    m_hat = m / (1.0 - ADAM_B1 ** ADAM_STEP)
    v_hat = v / (1.0 - ADAM_B2 ** ADAM_STEP)
    delta = -ADAM_LR * (m_hat / (jnp.sqrt(v_hat) + ADAM_EPS) + ADAM_WD * w)
    return delta, m, v


def _sum_parts(parts_ref, idx):
    g = parts_ref[(0,) + idx].astype(F32)
    for q in range(1, N_DEV):
        g = g + parts_ref[(q,) + idx].astype(F32)
    return g


def _adamw_big(parts, w, m, v, name):
    nl, R, C = w.shape
    tr = _divisor(R, 256, 2 * SUBLANES)

    def body(*refs):
        p_refs = refs[:nl]
        w_ref, m_ref, v_ref, g_ref, d_ref, nm_ref, nv_ref = refs[nl:]
        layer = pl.program_id(0)
        for k in range(nl):
            @pl.when(layer == k)
            def _(k=k):
                g = _sum_parts(p_refs[k], ())
                d, nm, nv = _adamw_math(w_ref[0], g, m_ref[0], v_ref[0])
                g_ref[0] = g
                d_ref[0] = d
                nm_ref[0] = nm
                nv_ref[0] = nv

    def part_spec(k):
        return pl.BlockSpec((N_DEV, tr, C), lambda l, r: (0, jnp.where(l == k, r, 0), 0))

    blk = pl.BlockSpec((1, tr, C), lambda l, r: (l, r, 0))
    shp = jax.ShapeDtypeStruct((nl, R, C), F32)
    return pl.pallas_call(
        body, name=name, grid=(nl, R // tr),
        in_specs=[part_spec(k) for k in range(nl)] + [blk, blk, blk],
        out_specs=(blk, blk, blk, blk), out_shape=(shp, shp, shp, shp),
        compiler_params=_params(("arbitrary", "arbitrary")),
    )(*parts, w, m, v)


def _adamw_small(entries, name):
    n = len(entries)
    uniq = []
    for e in entries:
        if not any(e[0] is u for u in uniq):
            uniq.append(e[0])
    pidx = [next(k for k, u in enumerate(uniq) if u is e[0]) for e in entries]
    npart = len(uniq)

    def body(*refs):
        p_refs = refs[:npart]
        wmv = refs[npart:npart + 3 * n]
        outs = refs[npart + 3 * n:]
        for t, e in enumerate(entries):
            lo, w = e[1], e[2]
            rows = w.shape[0]
            pr = p_refs[pidx[t]]
            g = pr[0, lo:lo + rows].astype(F32)
            for q in range(1, N_DEV):
                g = g + pr[q, lo:lo + rows].astype(F32)
            d, nm, nv = _adamw_math(wmv[3 * t][...], g, wmv[3 * t + 1][...], wmv[3 * t + 2][...])
            outs[4 * t][...] = g
            outs[4 * t + 1][...] = d
            outs[4 * t + 2][...] = nm
            outs[4 * t + 3][...] = nv

    vm = pl.BlockSpec(memory_space=pltpu.VMEM)
    args = list(uniq)
    out_shape = []
    for e in entries:
        args += [e[2], e[3], e[4]]
        out_shape += [jax.ShapeDtypeStruct(e[2].shape, F32)] * 4
    res = pl.pallas_call(
        body, name=name, in_specs=[vm] * len(args), out_specs=tuple([vm] * len(out_shape)),
        out_shape=tuple(out_shape), compiler_params=_params(),
    )(*args)
    return [tuple(res[4 * t:4 * t + 4]) for t in range(n)]


def _head_matrix(cc):
    bw = min(256, cc)
    r = lax.broadcasted_iota(jnp.int32, (bw, bw), 0) // HEAD_DIM
    c = lax.broadcasted_iota(jnp.int32, (bw, bw), 1) // HEAD_DIM
    return jnp.where(r == c, 1.0 / HEAD_DIM, 0.0).astype(BF16)


def _cols_from_shards(g):
    nd = g.ndim
    perm = tuple(range(1, nd - 1)) + (0, nd - 1)
    t = jnp.transpose(g, perm)
    return t.reshape(t.shape[:-2] + (t.shape[-2] * t.shape[-1],))


def _cols_to_shards(a):
    nd = a.ndim
    t = a.reshape(a.shape[:-1] + (N_DEV, a.shape[-1] // N_DEV))
    perm = (nd - 1,) + tuple(range(nd - 1)) + (nd,)
    return jnp.transpose(t, perm)


def kernel(x, meta_tokens, norm1_g, w_in, conv_dw_k, conv_dw_b, conv_ln_g, conv_ln_b, pool_w, pool_scale, w_out, norm2_g, w_up, ffn_dw_k, w_down, final_g, loss_target, m_meta_tokens, m_norm1_g, m_w_in, m_conv_dw_k, m_conv_dw_b, m_conv_ln_g, m_conv_ln_b, m_pool_w, m_pool_scale, m_w_out, m_norm2_g, m_w_up, m_ffn_dw_k, m_w_down, m_final_g, v_meta_tokens, v_norm1_g, v_w_in, v_conv_dw_k, v_conv_dw_b, v_conv_ln_g, v_conv_ln_b, v_pool_w, v_pool_scale, v_w_out, v_norm2_g, v_w_up, v_ffn_dw_k, v_w_down, v_final_g):
    depth, D = norm1_g.shape
    n_meta = meta_tokens.shape[0]
    seq = x.shape[1]
    L = n_meta + seq
    cc = conv_dw_b.shape[1]
    ng, gd = pool_w.shape[1], pool_w.shape[2]
    f = w_down.shape[1] * N_DEV

    def rows(g):
        return g.reshape(-1, g.shape[-1])

    b16 = lambda a: a.astype(BF16)
    tr = lambda a: jnp.swapaxes(a, -1, -2)
    w_in_t, m_w_in_t, v_w_in_t = tr(w_in), tr(m_w_in), tr(v_w_in)
    w_up_t, m_w_up_t, v_w_up_t = tr(w_up), tr(m_w_up), tr(v_w_up)
    (g_in0, g_ck, g_kf, g_meta) = _exchange([b16(w_in_t[0]), conv_dw_k, ffn_dw_k, meta_tokens], ["gather"] * 4,
                                            "gather_first")
    ck_full = _cols_from_shards(g_ck)
    ck_rows = jnp.broadcast_to(ck_full[:, :, None, :], ck_full.shape[:2] + (SUBLANES, cc))
    kf_full = _cols_from_shards(g_kf)
    meta_full = _cols_from_shards(g_meta)
    am = _head_matrix(cc)
    win, wout, wup, wdown = [None] * depth, [None] * depth, [None] * depth, [None] * depth
    win[0] = rows(g_in0)

    h = (meta_full, x[0])
    saved = []
    for l in range(depth):
        more = l + 1 < depth
        if l == 0:
            (z, hn1), (g_out, g_down) = _norm_proj(h, norm1_g[l:l + 1], win[l], f"in_proj_{l}", tn_cap=1536,
                                                   xchg=([b16(w_out[l]), b16(w_down[l])], ["gather"] * 2))
            wout[l], wdown[l] = rows(g_out), rows(g_down)
            (ymix, u1), (g_up,) = _mixer_fwd(z, ck_rows[l], conv_dw_b[l:l + 1], conv_ln_g[l:l + 1], conv_ln_b[l:l + 1],
                                             pool_w[l], pool_scale[l:l + 1], am, f"mixer_fwd_{l}",
                                             xchg=([b16(w_up_t[l])], ["gather"]))
            wup[l] = rows(g_up)
        else:
            z, hn1 = _norm_proj(h, norm1_g[l:l + 1], win[l], f"in_proj_{l}", tn_cap=1536)
            ymix, u1 = _mixer_fwd(z, ck_rows[l], conv_dw_b[l:l + 1], conv_ln_g[l:l + 1], conv_ln_b[l:l + 1], pool_w[l],
                                  pool_scale[l:l + 1], am, f"mixer_fwd_{l}")
        if more:
            h_mid, (g_in, g_out) = _mm(ymix, wout[l], f"out_proj_{l}", res=h, tn_cap=1024,
                                       xchg=([b16(w_in_t[l + 1]), b16(w_out[l + 1])], ["gather"] * 2))
            win[l + 1], wout[l + 1] = rows(g_in), rows(g_out)
            nxt = [b16(w_up_t[l + 1]), b16(w_down[l + 1])]
            (h_out, hn2, act, ux, uc), got = _ffn_block_fwd(h_mid, norm2_g[l:l + 1], wup[l], kf_full[l], wdown[l],
                                                            f"ffn_fwd_{l}", xchg=(nxt, ["gather"] * 2))
            wup[l + 1], wdown[l + 1] = rows(got[0]), rows(got[1])
        else:
            h_mid = _mm(ymix, wout[l], f"out_proj_{l}", res=h, tn_cap=1024)
            h_out, hn2, act, ux, uc = _ffn_block_fwd(h_mid, norm2_g[l:l + 1], wup[l], kf_full[l], wdown[l], f"ffn_fwd_{l}")
        saved.append((h, hn1, z, u1, ymix, h_mid, hn2, ux, uc, act))
        h = h_out

    dh, d_final_g, loss_part = _loss_head(h, final_g.reshape(1, D), loss_target[0], n_meta, "loss_head")

    def row_shards(gm):
        return gm.reshape(N_DEV, -1, gm.shape[-1])

    zero_row = jnp.zeros((1, D), F32)
    gw = {k: [None] * depth for k in ("ck", "cb", "lg", "lb", "pw", "ps", "kf", "n1", "n2")}
    parts = {k: [None] * depth for k in ("in", "out", "up", "down")}
    for l in reversed(range(depth)):
        h_in, hn1, z, u1, ymix, h_mid, hn2, ux, uc, act = saved[l]
        g_down = _mm_tn(act, dh, f"down_proj_wgrad_{l}", tq_cap=512)
        (dh_mid, gw["n2"][l], dug0, dkf), (parts["down"][l],) = _ffn_block_bwd(
            dh, h_mid, norm2_g[l:l + 1], ux, uc, kf_full[l], wdown[l], wup[l], f"ffn_bwd_{l}",
            xchg=([row_shards(g_down)], ["a2a"]))
        gw["kf"][l] = jnp.transpose(dkf, (2, 1, 0, 3)).reshape(dkf.shape[2], -1)
        g_up_t = _mm_tn(dug0, hn2, f"up_proj_wgrad_{l}", halves=2, tq_cap=1024)
        dymix = _mm(dh_mid, wout[l], f"out_proj_bwd_{l}", b_t=True, out_dtype=BF16, tn_cap=1024)
        g_out = _mm_tn(ymix, dh_mid, f"out_proj_wgrad_{l}", tq_cap=1024)
        ((dz, gw["ck"][l], gw["cb"][l], gw["lg"][l], gw["lb"][l], gw["pw"][l], gw["ps"][l]),
         (parts["up"][l], parts["out"][l])) = _mixer_bwd(
            z, u1, dymix, ck_rows[l], conv_ln_g[l:l + 1], conv_ln_b[l:l + 1], pool_w[l], pool_scale[l:l + 1], am,
            f"mixer_bwd_{l}", xchg=([row_shards(g_up_t), row_shards(g_out)], ["a2a", "a2a"]))
        g_in_t = _mm_tn(dz, hn1, f"in_proj_wgrad_{l}", tq_cap=1024)
        if l > 0:
            (dh, gw["n1"][l]), (parts["in"][l],) = _proj_bwd_norm(dz, win[l], h_in, norm1_g[l:l + 1], dh_mid, zero_row,
                                                                  f"in_proj_bwd_{l}", xchg=([row_shards(g_in_t)], ["a2a"]))
        else:
            meta_rows, x_rows = h_in
            (grad_x, dg_x), (parts["in"][l],) = _proj_bwd_norm(dz, win[l], x_rows, norm1_g[l:l + 1], dh_mid, zero_row,
                                                               f"in_proj_bwd_{l}", skip=n_meta,
                                                               xchg=([row_shards(g_in_t)], ["a2a"]))
            d_meta, gw["n1"][l] = _proj_bwd_norm(dz[:n_meta], win[l], meta_rows, norm1_g[l:l + 1], dh_mid[:n_meta],
                                                 dg_x, f"in_proj_bwd_meta_{l}")
    grad_x = grad_x[None]

    pack_d = jnp.concatenate(gw["n1"] + gw["n2"] + [d_final_g, jnp.broadcast_to(loss_part[:, :1], (1, D)), zero_row, zero_row], axis=0)
    pack_c = jnp.concatenate(gw["cb"] + gw["lg"] + gw["lb"] + gw["ps"], axis=0)
    pack_pw = b16(jnp.stack(gw["pw"]).reshape(depth * ng * gd, gd))
    src = [_cols_to_shards(jnp.stack(gw["ck"])), _cols_to_shards(jnp.stack(gw["kf"])), _cols_to_shards(d_meta),
           pack_d, pack_c, pack_pw]
    r_ck, r_kf, r_meta, r_d, r_c, r_pw = _exchange(src, ["a2a"] * 3 + ["gather"] * 3, "exchange_small_grads")

    big = {
        "w_in": tuple(tr(a) for a in _adamw_big(parts["in"], w_in_t, m_w_in_t, v_w_in_t, "adamw_w_in")),
        "w_out": _adamw_big(parts["out"], w_out, m_w_out, v_w_out, "adamw_w_out"),
        "w_up": tuple(tr(a) for a in _adamw_big(parts["up"], w_up_t, m_w_up_t, v_w_up_t, "adamw_w_up")),
        "w_down": _adamw_big(parts["down"], w_down, m_w_down, v_w_down, "adamw_w_down"),
    }
    kwid = conv_dw_k.shape[1]
    fkw = ffn_dw_k.shape[1]
    row = lambda a: a.reshape(1, -1)
    entries = [
        (r_d, 0, norm1_g, m_norm1_g, v_norm1_g),
        (r_d, depth, norm2_g, m_norm2_g, v_norm2_g),
        (r_d, 2 * depth, row(final_g), row(m_final_g), row(v_final_g)),
        (r_c, 0, conv_dw_b, m_conv_dw_b, v_conv_dw_b),
        (r_c, depth, conv_ln_g, m_conv_ln_g, v_conv_ln_g),
        (r_c, 2 * depth, conv_ln_b, m_conv_ln_b, v_conv_ln_b),
        (r_c, 3 * depth, pool_scale, m_pool_scale, v_pool_scale),
        (r_pw, 0, pool_w.reshape(-1, gd), m_pool_w.reshape(-1, gd), v_pool_w.reshape(-1, gd)),
        (r_ck.reshape(N_DEV, depth * kwid, -1), 0, conv_dw_k.reshape(depth * kwid, -1),
         m_conv_dw_k.reshape(depth * kwid, -1), v_conv_dw_k.reshape(depth * kwid, -1)),
        (r_kf.reshape(N_DEV, depth * fkw, -1), 0, ffn_dw_k.reshape(depth * fkw, -1),
         m_ffn_dw_k.reshape(depth * fkw, -1), v_ffn_dw_k.reshape(depth * fkw, -1)),
        (r_meta, 0, meta_tokens, m_meta_tokens, v_meta_tokens),
        (r_d, 2 * depth + 1, zero_row, zero_row, zero_row),
    ]
    small = _adamw_small(entries, "adamw_small")
    names = ["norm1_g", "norm2_g", "final_g", "conv_dw_b", "conv_ln_g", "conv_ln_b", "pool_scale", "pool_w",
             "conv_dw_k", "ffn_dw_k", "meta_tokens"]
    shapes = {"final_g": final_g.shape, "pool_w": pool_w.shape, "conv_dw_k": conv_dw_k.shape, "ffn_dw_k": ffn_dw_k.shape}
    res = dict(big)
    for nme, quad in zip(names, small[:-1]):
        res[nme] = tuple(a.reshape(shapes[nme]) if nme in shapes else a for a in quad)
    loss = small[-1][0][0, 0]

    order = ["meta_tokens", "norm1_g", "w_in", "conv_dw_k", "conv_dw_b", "conv_ln_g", "conv_ln_b", "pool_w", "pool_scale",
             "w_out", "norm2_g", "w_up", "ffn_dw_k", "w_down", "final_g"]
    return (loss, grad_x, *[res[k][0] for k in order], *[res[k][1] for k in order], *[res[k][2] for k in order],
            *[res[k][3] for k in order])
```

```python
import functools

import jax
import jax.numpy as jnp
from jax import lax
from jax.experimental import pallas as pl
from jax.experimental.pallas import tpu as pltpu

F32 = jnp.float32
BF16 = jnp.bfloat16

EPS = 1e-6
HEAD_DIM = 64
POOL_WINDOWS = (2, 4, 8, 16)
ADAM_LR = 0.001
ADAM_B1 = 0.9
ADAM_B2 = 0.999
ADAM_EPS = 1e-08
ADAM_WD = 0.01
ADAM_STEP = 10

N_DEV = 8
OTHER_CHIPS = (2, 4, 6)
SUBLANES = 8
HALO = 48
CONV_PAD = 32
POOL_PAD = 16
FFN_PAD = 8
ROW_CHUNK = 24
CONV3_ROWS = 48
MAX_TILE_ROWS = 1024
WGRAD_TILE_ROWS = 2816
VMEM_LIMIT = 52 * 1024 * 1024


def _divisor(n, cap, mult):
    best = None
    for d in range(mult, min(n, cap) + 1, mult):
        if n % d == 0:
            best = d
    return n if best is None else best


def _token_tile(L):
    return _divisor(L, MAX_TILE_ROWS, HALO)


def _stat_rows(tl):
    return _divisor(tl, 512, SUBLANES)


def _params(sem=None):
    return pltpu.CompilerParams(dimension_semantics=sem, vmem_limit_bytes=VMEM_LIMIT)


def _rowsum8(x):
    acc = x[0:SUBLANES]
    for k in range(1, x.shape[0] // SUBLANES):
        acc = acc + x[k * SUBLANES:(k + 1) * SUBLANES]
    return acc


def _sigmoid(x):
    return jax.nn.sigmoid(x)


def _dot_nt(a, b):
    return lax.dot_general(a, b, (((1,), (1,)), ((), ())), preferred_element_type=F32)


def _head_mean(x, am_ref):
    bw = am_ref.shape[0]
    am = am_ref[...]
    outs = []
    for blk in range(x.shape[1] // bw):
        xb = x[:, blk * bw:(blk + 1) * bw]
        hi = xb.astype(BF16)
        lo = (xb - hi.astype(F32)).astype(BF16)
        outs.append(jnp.dot(hi, am, preferred_element_type=F32) + jnp.dot(lo, am, preferred_element_type=F32))
    return outs[0] if len(outs) == 1 else jnp.concatenate(outs, axis=-1)


def _xchg_out_shapes(srcs, modes):
    out = []
    for s, m in zip(srcs, modes):
        shp = ((N_DEV,) + tuple(s.shape)) if m == "gather" else tuple(s.shape)
        out.append(jax.ShapeDtypeStruct(shp, s.dtype))
    return out


def _xchg_sems(n):
    return [pltpu.SemaphoreType.DMA((n, N_DEV - 1)), pltpu.SemaphoreType.DMA((n, N_DEV - 1)), pltpu.SemaphoreType.DMA((n,))]


def _xchg_ops(src_refs, out_refs, sems, modes):
    n = len(src_refs)
    send_sems, recv_sems, local_sems = sems
    x, y, c = lax.axis_index("x"), lax.axis_index("y"), lax.axis_index("c")
    me = 4 * x + 2 * y + c

    def peer(d):
        return (x ^ ((d >> 2) & 1), y ^ ((d >> 1) & 1), c ^ (d & 1))

    def peer_id(d):
        px, py, pc = peer(d)
        return 4 * px + 2 * py + pc

    def remote(t, d):
        src = src_refs[t] if modes[t] == "gather" else src_refs[t].at[peer_id(d)]
        return pltpu.make_async_remote_copy(
            src_ref=src, dst_ref=out_refs[t].at[me], send_sem=send_sems.at[t, d - 1], recv_sem=recv_sems.at[t, d - 1],
            device_id=peer(d), device_id_type=pl.DeviceIdType.MESH)

    def arrival(t, d):
        src = src_refs[t] if modes[t] == "gather" else src_refs[t].at[me]
        return pltpu.make_async_remote_copy(
            src_ref=src, dst_ref=out_refs[t].at[peer_id(d)], send_sem=send_sems.at[t, d - 1],
            recv_sem=recv_sems.at[t, d - 1], device_id=peer(d), device_id_type=pl.DeviceIdType.MESH)

    def passed_on(t, d):
        blk = out_refs[t].at[peer_id(d)]
        return pltpu.make_async_remote_copy(
            src_ref=blk, dst_ref=blk, send_sem=send_sems.at[t, d], recv_sem=recv_sems.at[t, d],
            device_id=peer(1), device_id_type=pl.DeviceIdType.MESH)

    def local(t):
        src = src_refs[t] if modes[t] == "gather" else src_refs[t].at[me]
        return pltpu.make_async_copy(src, out_refs[t].at[me], local_sems.at[t])

    def sent_first(t):
        return OTHER_CHIPS + (1,) if modes[t] == "gather" else tuple(range(1, N_DEV))

    def start():
        for t in range(n):
            local(t).start()
        for t in range(n):
            for d in sent_first(t):
                remote(t, d).start()

    gathered = [t for t in range(n) if modes[t] == "gather"]

    def relay():
        for t in gathered:
            for d in OTHER_CHIPS:
                arrival(t, d).wait_recv()
                passed_on(t, d).start()

    def wait():
        for t in range(n):
            for d in range(1, N_DEV):
                if not (modes[t] == "gather" and d in OTHER_CHIPS):
                    arrival(t, d).wait_recv()
        for t in range(n):
            for d in sent_first(t):
                remote(t, d).wait_send()
        for t in gathered:
            for d in OTHER_CHIPS:
                passed_on(t, d).wait_send()
        for t in range(n):
            local(t).wait()

    return start, relay, wait


def _exchange(srcs, modes, name):
    n = len(srcs)

    def body(*refs):
        start, relay, wait = _xchg_ops(refs[:n], refs[n:2 * n], refs[2 * n:], modes)
        start()
        relay()
        wait()

    any_spec = pl.BlockSpec(memory_space=pl.ANY)
    return pl.pallas_call(
        body, name=name, out_shape=tuple(_xchg_out_shapes(srcs, modes)),
        in_specs=[any_spec] * n, out_specs=tuple([any_spec] * n),
        scratch_shapes=_xchg_sems(n),
        compiler_params=pltpu.CompilerParams(has_side_effects=True),
    )(*srcs)


def _call(body, *, name, grid, in_specs, out_specs, out_shape, args, scratch_shapes=(), sem=None, xchg=None):
    single = not isinstance(out_shape, (tuple, list))
    outs_shape = [out_shape] if single else list(out_shape)
    outs_spec = [out_specs] if single else list(out_specs)
    if xchg is None:
        res = pl.pallas_call(
            body, name=name, grid=grid, in_specs=list(in_specs), out_specs=out_specs, out_shape=out_shape,
            scratch_shapes=list(scratch_shapes), compiler_params=_params(sem))(*args)
        return res, ()
    srcs, modes = xchg
    n_in, n_out, n_scr, nx = len(in_specs), len(outs_shape), len(scratch_shapes), len(srcs)
    n_steps = functools.reduce(lambda a, b: a * b, grid, 1)
    relay_step = n_steps // 2

    def wrapped(*refs):
        ins = refs[:n_in]
        xs = refs[n_in:n_in + nx]
        o0 = n_in + nx
        outs = refs[o0:o0 + n_out]
        xo = refs[o0 + n_out:o0 + n_out + nx]
        s0 = o0 + n_out + nx
        scr = refs[s0:s0 + n_scr]
        start, relay, wait = _xchg_ops(xs, xo, refs[s0 + n_scr:], modes)
        step = functools.reduce(lambda acc, a: acc * grid[a] + pl.program_id(a), range(len(grid)), 0)

        @pl.when(step == 0)
        def _():
            start()

        body(*ins, *outs, *scr)

        @pl.when(step == relay_step)
        def _():
            relay()

        @pl.when(step == n_steps - 1)
        def _():
            wait()

    any_spec = pl.BlockSpec(memory_space=pl.ANY)
    res = pl.pallas_call(
        wrapped, name=name, grid=grid, in_specs=list(in_specs) + [any_spec] * nx,
        out_specs=tuple(outs_spec + [any_spec] * nx), out_shape=tuple(outs_shape + _xchg_out_shapes(srcs, modes)),
        scratch_shapes=list(scratch_shapes) + _xchg_sems(nx),
        compiler_params=_params(("arbitrary",) * len(grid)))(*args, *srcs)
    comp = res[:n_out]
    return (comp[0] if single else tuple(comp)), tuple(res[n_out:])


def _seq_rows(h):
    if isinstance(h, tuple):
        return h[0].shape[0] + h[1].shape[0], h[1].shape[1]
    return h.shape


def _seq_tiles(h, tm):
    if not isinstance(h, tuple):
        return [pl.BlockSpec((tm, h.shape[1]), lambda i, *_: (i, 0))], [h], lambda refs, i: refs[0][...]
    meta, x = h
    n, D = meta.shape

    def read(refs, i):
        t = refs[1][...]
        first = jnp.concatenate([refs[0][...], pltpu.roll(t, n, axis=0)[n:]], axis=0)
        return jnp.where(i == 0, first, t)

    window = pl.BlockSpec((pl.Element(tm), pl.Element(D)),
                          lambda i, *_: (pl.multiple_of(jnp.maximum(i * tm - n, 0), SUBLANES), 0))
    return [pl.BlockSpec((n, D), lambda *_: (0, 0)), window], [meta, x], read


def _norm_proj(h, g, w, name, *, tn_cap, xchg=None):
    L, D = _seq_rows(h)
    N = w.shape[0]
    tm = _token_tile(L)
    tn = _divisor(N, tn_cap, 128)
    h_specs, h_args, read_h = _seq_tiles(h, tm)
    nh = len(h_specs)

    def body(*refs):
        g_ref, w_ref, z_ref, hn_ref = refs[nh:]

        @pl.when(pl.program_id(1) == 0)
        def _():
            x = read_h(refs[:nh], pl.program_id(0))
            r = lax.rsqrt(jnp.mean(x * x, axis=-1, keepdims=True) + EPS)
            hn_ref[...] = ((x * r) * g_ref[...]).astype(BF16)

        z_ref[...] = _dot_nt(hn_ref[...], w_ref[...])

    out, xo = _call(
        body, name=name, grid=(L // tm, N // tn),
        in_specs=h_specs + [pl.BlockSpec((1, D), lambda i, j: (0, 0)), pl.BlockSpec((tn, D), lambda i, j: (j, 0))],
        out_specs=(pl.BlockSpec((tm, tn), lambda i, j: (i, j)), pl.BlockSpec((tm, D), lambda i, j: (i, 0))),
        out_shape=(jax.ShapeDtypeStruct((L, N), F32), jax.ShapeDtypeStruct((L, D), BF16)),
        sem=("parallel", "arbitrary"), args=(*h_args, g, w), xchg=xchg)
    return out if xchg is None else (out, xo)


def _proj_bwd_norm(a, b, h, g, dres, dg0, name, skip=0, xchg=None):
    L, K = a.shape
    D = b.shape[1]
    rows = L - skip
    tm = _divisor(rows, MAX_TILE_ROWS, 2 * SUBLANES) if skip else _token_tile(L)

    def body(a_ref, b_ref, h_ref, g_ref, dres_ref, dg0_ref, dh_ref, dg_ref):
        i = pl.program_id(0)
        dhn = jnp.dot(a_ref[...], b_ref[...], preferred_element_type=F32)
        x = h_ref[...]
        r = lax.rsqrt(jnp.mean(x * x, axis=-1, keepdims=True) + EPS)
        xhat = x * r
        dxhat = dhn * g_ref[...]
        dh_ref[...] = dres_ref[...] + r * (dxhat - xhat * jnp.mean(dxhat * xhat, axis=-1, keepdims=True))
        part = jnp.sum(_rowsum8(dhn * xhat), axis=0, keepdims=True)

        @pl.when(i == 0)
        def _():
            dg_ref[...] = dg0_ref[...] + part

        @pl.when(i > 0)
        def _():
            dg_ref[...] += part

    def rows_of(cols, first=skip):
        if not first:
            return pl.BlockSpec((tm, cols), lambda i: (i, 0))
        return pl.BlockSpec((pl.Element(tm), pl.Element(cols)), lambda i: (pl.multiple_of(first + i * tm, SUBLANES), 0))

    row = pl.BlockSpec((1, D), lambda i: (0, 0))
    h_first = skip if h.shape[0] == L else 0
    out, xo = _call(
        body, name=name, grid=(rows // tm,),
        in_specs=[rows_of(K), pl.BlockSpec((K, D), lambda i: (0, 0)), rows_of(D, h_first), row, rows_of(D), row],
        out_specs=(pl.BlockSpec((tm, D), lambda i: (i, 0)), row),
        out_shape=(jax.ShapeDtypeStruct((rows, D), F32), jax.ShapeDtypeStruct((1, D), F32)),
        sem=("arbitrary",), args=(a, b, h, g, dres, dg0), xchg=xchg)
    return out if xchg is None else (out, xo)


def _mm(a, b, name, *, res=None, b_t=False, out_dtype=F32, tn_cap=1408, xchg=None):
    M, K = a.shape
    N = b.shape[0] if b_t else b.shape[1]
    tm = _token_tile(M)
    tn = _divisor(N, tn_cap, 128)
    if isinstance(res, tuple):
        assert tn == N
        r_specs, r_args, read_r = _seq_tiles(res, tm)
    elif res is not None:
        r_specs, r_args, read_r = [pl.BlockSpec((tm, tn), lambda i, j: (i, j))], [res], lambda refs, i: refs[0][...]
    else:
        r_specs, r_args, read_r = [], [], None

    def body(*refs):
        a_ref, b_ref, o_ref = refs[0], refs[1], refs[-1]
        av = a_ref[...].astype(BF16)
        prod = _dot_nt(av, b_ref[...]) if b_t else jnp.dot(av, b_ref[...], preferred_element_type=F32)
        o_ref[...] = (prod if read_r is None else prod + read_r(refs[2:-1], pl.program_id(0))).astype(out_dtype)

    b_spec = pl.BlockSpec((tn, K), lambda i, j: (j, 0)) if b_t else pl.BlockSpec((K, tn), lambda i, j: (0, j))
    in_specs = [pl.BlockSpec((tm, K), lambda i, j: (i, 0)), b_spec] + r_specs
    args = [a, b] + r_args
    out, xo = _call(
        body, name=name, grid=(M // tm, N // tn), in_specs=in_specs,
        out_specs=pl.BlockSpec((tm, tn), lambda i, j: (i, j)), out_shape=jax.ShapeDtypeStruct((M, N), out_dtype),
        sem=("parallel", "parallel"), args=args, xchg=xchg)
    return out if xchg is None else (out, xo)


def _mm_tn(a, b, name, *, halves=1, tq_cap=1408):
    L, Q = b.shape
    ph = a.shape[-1]
    P = ph * halves
    tl = _divisor(L, WGRAD_TILE_ROWS, HALO)
    tp = _divisor(ph, 1408, 128)
    tq = _divisor(Q, tq_cap, 128)
    pper = ph // tp
    nl = L // tl
    grid = (P // tp, Q // tq, nl)

    def body(a_ref, b_ref, o_ref, acc):
        prod = lax.dot_general(a_ref[...].astype(BF16), b_ref[...].astype(BF16), (((0,), (0,)), ((), ())),
                               preferred_element_type=F32)
        l = pl.program_id(2)
        if nl == 1:
            o_ref[...] = prod.astype(BF16)
            return

        @pl.when(l == 0)
        def _():
            acc[...] = prod

        @pl.when(jnp.logical_and(l > 0, l < nl - 1))
        def _():
            acc[...] += prod

        @pl.when(l == nl - 1)
        def _():
            o_ref[...] = (acc[...] + prod).astype(BF16)

    if halves > 1:
        a_spec = pl.BlockSpec((None, tl, tp), lambda p, q, l: (p // pper, l, p % pper))
    else:
        a_spec = pl.BlockSpec((tl, tp), lambda p, q, l: (l, p))
    return pl.pallas_call(
        body, name=name, grid=grid,
        in_specs=[a_spec, pl.BlockSpec((tl, tq), lambda p, q, l: (l, q))],
        out_specs=pl.BlockSpec((tp, tq), lambda p, q, l: (p, q)),
        out_shape=jax.ShapeDtypeStruct((P, Q), BF16),
        scratch_shapes=[pltpu.VMEM((tp, tq), F32)],
        compiler_params=_params(("parallel", "parallel", "arbitrary")),
    )(a, b)


def _loss_head(h, g, tgt, n_meta, name):
    L, D = h.shape
    tl = _token_tile(L)
    nt = L // tl

    def body(h_ref, g_ref, t_ref, dh_ref, dg_ref, loss_ref):
        i = pl.program_id(0)
        x = h_ref[...]
        r = lax.rsqrt(jnp.mean(x * x, axis=-1, keepdims=True) + EPS)
        xhat = x * r
        gg = g_ref[...]
        y = xhat * gg
        rows = i * tl + lax.broadcasted_iota(jnp.int32, (tl, 1), 0)
        t = t_ref[...]
        t = jnp.where(i == 0, pltpu.roll(t, n_meta, axis=0), t)
        err = jnp.where(rows >= n_meta, y - t, 0.0)
        dy = err * (1.0 / D)
        dxhat = dy * gg
        dh_ref[...] = r * (dxhat - xhat * jnp.mean(dxhat * xhat, axis=-1, keepdims=True))
        dg_part = jnp.sum(_rowsum8(dy * xhat), axis=0, keepdims=True)
        per_row = jnp.mean(err * err, axis=-1, keepdims=True)
        loss_part = jnp.broadcast_to(0.5 * jnp.sum(per_row, axis=0, keepdims=True), (1, 128))

        @pl.when(i == 0)
        def _():
            dg_ref[...] = dg_part
            loss_ref[...] = loss_part

        @pl.when(i > 0)
        def _():
            dg_ref[...] += dg_part
            loss_ref[...] += loss_part

    tile = pl.BlockSpec((tl, D), lambda i: (i, 0))
    row = pl.BlockSpec((1, D), lambda i: (0, 0))
    window = pl.BlockSpec((pl.Element(tl), pl.Element(D)),
                          lambda i: (pl.multiple_of(jnp.maximum(i * tl - n_meta, 0), SUBLANES), 0))
    return pl.pallas_call(
        body, name=name, grid=(nt,), in_specs=[tile, row, window],
        out_specs=(tile, row, pl.BlockSpec((1, 128), lambda i: (0, 0))),
        out_shape=(jax.ShapeDtypeStruct((L, D), F32), jax.ShapeDtypeStruct((1, D), F32),
                   jax.ShapeDtypeStruct((1, 128), F32)),
        compiler_params=_params(("arbitrary",)),
    )(h, g, tgt)


def _pool_fwd_block(pwin, pw_ref, row0, rb, g, gd, w, t0):
    wv = pwin[pl.ds(row0 + HALO - POOL_PAD, rb + POOL_PAD), g * gd:(g + 1) * gd]
    s = wv
    sh = 1
    while sh < w:
        s = s + pltpu.roll(s, sh, axis=0)
        sh *= 2
    win = s[POOL_PAD:POOL_PAD + rb]
    pt = wv[POOL_PAD:POOL_PAD + rb]
    tg = t0 + lax.broadcasted_iota(jnp.int32, (rb, 1), 0)
    cnt = jnp.minimum(tg + 1, w).astype(F32)
    return win / cnt - pt


def _fill_windows(i, zp_ref, zc_ref, u0w, pwin, tl, cc):
    keep = i > 0
    zp = zp_ref[...]
    u0w[0:HALO, :] = jnp.where(keep, zp[:, :cc] * _sigmoid(zp[:, cc:2 * cc]), 0.0)
    pwin[0:HALO, :] = jnp.where(keep, zp[:, 2 * cc:], 0.0)

    def fill(c, carry):
        b = pl.multiple_of(c * ROW_CHUNK, SUBLANES)
        zc = zc_ref[pl.ds(b, ROW_CHUNK), :]
        u0w[pl.ds(HALO + b, ROW_CHUNK), :] = zc[:, :cc] * _sigmoid(zc[:, cc:2 * cc])
        pwin[pl.ds(HALO + b, ROW_CHUNK), :] = zc[:, 2 * cc:]
        return carry

    lax.fori_loop(0, tl // ROW_CHUNK, fill, 0)


def _mixer_fwd(z, ck, cb, lg, lb, pw, ps, am, name, xchg=None):
    L, ci = z.shape
    kw, _, cc = ck.shape
    cp = ci - 2 * cc
    ng, gd = pw.shape[0], pw.shape[1]
    tl = _token_tile(L)
    nt = L // tl
    hb = tl // HALO
    rb = _stat_rows(tl)
    tap0 = CONV_PAD - (kw - 1)

    def body(zp_ref, zc_ref, ck_ref, cb_ref, lg_ref, lb_ref, pw_ref, ps_ref, am_ref, y_ref, u1_ref, u0w, pwin):
        i = pl.program_id(0)
        _fill_windows(i, zp_ref, zc_ref, u0w, pwin, tl, cc)

        def conv(c, carry):
            b = pl.multiple_of(c * ROW_CHUNK, SUBLANES)
            w = u0w[pl.ds(b + HALO - CONV_PAD, ROW_CHUNK + CONV_PAD), :]
            acc = jnp.broadcast_to(cb_ref[...], (ROW_CHUNK, cc))
            for j in range(kw):
                acc = acc + _rows_of(ck_ref[j], ROW_CHUNK) * w[tap0 + j:tap0 + j + ROW_CHUNK]
            u1_ref[pl.ds(b, ROW_CHUNK), :] = acc
            return carry

        lax.fori_loop(0, tl // ROW_CHUNK, conv, 0)

        def blocks(k, carry):
            b = pl.multiple_of(k * rb, SUBLANES)
            u1 = u1_ref[pl.ds(b, rb), :]
            xc = u1 - _head_mean(u1, am_ref)
            var = _head_mean(xc * xc, am_ref)
            u2 = (xc * lax.rsqrt(var + EPS)) * lg_ref[...] + lb_ref[...]
            y_ref[pl.ds(b, rb), 0:cc] = (u2 * _sigmoid(u2)).astype(y_ref.dtype)
            for g in range(ng):
                d = _pool_fwd_block(pwin, pw_ref, b, rb, g, gd, POOL_WINDOWS[g], i * tl + b)
                yp = jnp.dot(d.astype(BF16), pw_ref[g].astype(BF16), preferred_element_type=F32)
                yp = yp * ps_ref[:, g * gd:(g + 1) * gd]
                y_ref[pl.ds(b, rb), cc + g * gd:cc + (g + 1) * gd] = yp.astype(y_ref.dtype)
            return carry

        lax.fori_loop(0, tl // rb, blocks, 0)

    def full(a):
        nd = a.ndim
        return pl.BlockSpec(a.shape, lambda i: (0,) * nd)

    out, xo = _call(
        body, name=name, grid=(nt,),
        in_specs=[pl.BlockSpec((HALO, ci), lambda i: (jnp.maximum(i * hb - 1, 0), 0)),
                  pl.BlockSpec((tl, ci), lambda i: (i, 0)),
                  full(ck), full(cb), full(lg), full(lb), full(pw), full(ps), full(am)],
        out_specs=(pl.BlockSpec((tl, cc + cp), lambda i: (i, 0)), pl.BlockSpec((tl, cc), lambda i: (i, 0))),
        out_shape=(jax.ShapeDtypeStruct((L, cc + cp), BF16), jax.ShapeDtypeStruct((L, cc), F32)),
        scratch_shapes=[pltpu.VMEM((HALO + tl, cc), F32), pltpu.VMEM((HALO + tl, cp), F32)],
        sem=("parallel",), args=(z, z, ck, cb, lg, lb, pw, ps, am), xchg=xchg)
    return out if xchg is None else (out, xo)


def _mixer_bwd(z, u1, dy, ck, lg, lb, pw, ps, am, name, xchg=None):
    L, ci = z.shape
    kw, _, cc = ck.shape
    cp = ci - 2 * cc
    ng, gd = pw.shape[0], pw.shape[1]
    tl = _token_tile(L)
    nt = L // tl
    hb = tl // HALO
    rb = _stat_rows(tl)

    def body(zp_ref, zc_ref, u1c_ref, u1n_ref, dyc_ref, dyn_ref, ck_ref, lg_ref, lb_ref, pw_ref, ps_ref, am_ref,
             dz_ref, dck_ref, dcb_ref, dlg_ref, dlb_ref, dpw_ref, dps_ref,
             u0w, pwin, du1w, ddw, ew, dkacc, dcb8, dlg8, dlb8, dps8):
        i = pl.program_id(0)
        has_next = i < nt - 1

        @pl.when(i == 0)
        def _():
            dck_ref[...] = jnp.zeros_like(dck_ref)
            dcb_ref[...] = jnp.zeros_like(dcb_ref)
            dlg_ref[...] = jnp.zeros_like(dlg_ref)
            dlb_ref[...] = jnp.zeros_like(dlb_ref)
            dpw_ref[...] = jnp.zeros_like(dpw_ref)
            dps_ref[...] = jnp.zeros_like(dps_ref)

        dkacc[...] = jnp.zeros_like(dkacc)
        dcb8[...] = jnp.zeros_like(dcb8)
        dlg8[...] = jnp.zeros_like(dlg8)
        dlb8[...] = jnp.zeros_like(dlb8)
        dps8[...] = jnp.zeros_like(dps8)

        _fill_windows(i, zp_ref, zc_ref, u0w, pwin, tl, cc)

        def conv_side(u1, dyc, own):
            xc = u1 - _head_mean(u1, am_ref)
            rstd = lax.rsqrt(_head_mean(xc * xc, am_ref) + EPS)
            uh = xc * rstd
            lgv = lg_ref[...]
            u2 = uh * lgv + lb_ref[...]
            sg = _sigmoid(u2)
            du2 = dyc * (sg * (1.0 + u2 * (1.0 - sg)))
            if own:
                dlg8[...] += _rowsum8(du2 * uh)
                dlb8[...] += _rowsum8(du2)
            duh = du2 * lgv
            return rstd * (duh - _head_mean(duh, am_ref) - uh * _head_mean(duh * uh, am_ref))

        def pool_side(dyp, t0, rows):
            dds, es = [], []
            tg = t0 + lax.broadcasted_iota(jnp.int32, (rows, 1), 0)
            for g in range(ng):
                dypre = dyp[:, g * gd:(g + 1) * gd] * ps_ref[:, g * gd:(g + 1) * gd]
                dd = lax.dot_general(dypre.astype(BF16), pw_ref[g].astype(BF16), (((1,), (1,)), ((), ())),
                                     preferred_element_type=F32)
                cnt = jnp.minimum(tg + 1, POOL_WINDOWS[g]).astype(F32)
                dds.append(dd)
                es.append(dd / cnt)
            return jnp.concatenate(dds, axis=-1), jnp.concatenate(es, axis=-1)

        def blocks(k, carry):
            b = pl.multiple_of(k * rb, SUBLANES)
            dyb = dyc_ref[pl.ds(b, rb), :].astype(F32)
            du1 = conv_side(u1c_ref[pl.ds(b, rb), :], dyb[:, :cc], True)
            du1w[pl.ds(b, rb), :] = du1
            dcb8[...] += _rowsum8(du1)
            dyp = dyb[:, cc:]
            dd, e = pool_side(dyp, i * tl + b, rb)
            ddw[pl.ds(b, rb), :] = dd
            ew[pl.ds(b, rb), :] = e
            for g in range(ng):
                d = _pool_fwd_block(pwin, pw_ref, b, rb, g, gd, POOL_WINDOWS[g], i * tl + b)
                db16 = d.astype(BF16)
                dypg = dyp[:, g * gd:(g + 1) * gd]
                ypre = jnp.dot(db16, pw_ref[g].astype(BF16), preferred_element_type=F32)
                dps8[:, g * gd:(g + 1) * gd] += _rowsum8(dypg * ypre)
                dypre = (dypg * ps_ref[:, g * gd:(g + 1) * gd]).astype(BF16)
                dpw_ref[g] += lax.dot_general(db16, dypre, (((0,), (0,)), ((), ())), preferred_element_type=F32)
            return carry

        lax.fori_loop(0, tl // rb, blocks, 0)

        dyn = dyn_ref[...].astype(F32)
        du1n = conv_side(u1n_ref[...], dyn[:, :cc], False)
        du1w[tl:tl + HALO, :] = jnp.where(has_next, du1n, 0.0)
        ddn, en = pool_side(dyn[:, cc:], (i + 1) * tl, HALO)
        ew[tl:tl + HALO, :] = jnp.where(has_next, en, 0.0)

        def taps(c, carry):
            b = pl.multiple_of(c * ROW_CHUNK, SUBLANES)
            w = du1w[pl.ds(b, ROW_CHUNK + CONV_PAD), :]
            u0c = u0w[pl.ds(HALO + b, ROW_CHUNK), :]
            acc = jnp.zeros((ROW_CHUNK, cc), F32)
            for j in range(kw):
                o = kw - 1 - j
                sh = w[o:o + ROW_CHUNK]
                acc = acc + _rows_of(ck_ref[j], ROW_CHUNK) * sh
                dkacc[j] += _rowsum8(u0c * sh)
            zc = zc_ref[pl.ds(b, ROW_CHUNK), :]
            a = zc[:, :cc]
            sg = _sigmoid(zc[:, cc:2 * cc])
            dz_ref[pl.ds(b, ROW_CHUNK), 0:cc] = (acc * sg).astype(dz_ref.dtype)
            dz_ref[pl.ds(b, ROW_CHUNK), cc:2 * cc] = (acc * a * sg * (1.0 - sg)).astype(dz_ref.dtype)
            return carry

        lax.fori_loop(0, tl // ROW_CHUNK, taps, 0)

        def pool_back(k, carry):
            b = pl.multiple_of(k * rb, SUBLANES)
            n = rb + POOL_PAD
            for g in range(ng):
                s = ew[pl.ds(b, n), g * gd:(g + 1) * gd]
                sh = 1
                while sh < POOL_WINDOWS[g]:
                    s = s + pltpu.roll(s, n - sh, axis=0)
                    sh *= 2
                dp = s[0:rb] - ddw[pl.ds(b, rb), g * gd:(g + 1) * gd]
                dz_ref[pl.ds(b, rb), 2 * cc + g * gd:2 * cc + (g + 1) * gd] = dp.astype(dz_ref.dtype)
            return carry

        lax.fori_loop(0, tl // rb, pool_back, 0)

        dck_ref[...] += jnp.sum(dkacc[...], axis=1)
        dcb_ref[...] += jnp.sum(dcb8[...], axis=0, keepdims=True)
        dlg_ref[...] += jnp.sum(dlg8[...], axis=0, keepdims=True)
        dlb_ref[...] += jnp.sum(dlb8[...], axis=0, keepdims=True)
        dps_ref[...] += jnp.sum(dps8[...], axis=0, keepdims=True)

    def full(a):
        nd = a.ndim
        return pl.BlockSpec(a.shape, lambda i: (0,) * nd)

    nhb = L // HALO

    def prev_map(i):
        return (jnp.maximum(i * hb - 1, 0), 0)

    def next_map(i):
        return (jnp.minimum((i + 1) * hb, nhb - 1), 0)

    dcc = cc + cp
    row_cc = jax.ShapeDtypeStruct((1, cc), F32)
    out_shape = (jax.ShapeDtypeStruct((L, ci), BF16), jax.ShapeDtypeStruct((kw, cc), F32), row_cc, row_cc, row_cc,
                 jax.ShapeDtypeStruct((ng, gd, gd), F32), jax.ShapeDtypeStruct((1, cp), F32))
    acc_spec = [pl.BlockSpec((kw, cc), lambda i: (0, 0))] + [pl.BlockSpec((1, cc), lambda i: (0, 0))] * 3 + [
        pl.BlockSpec((ng, gd, gd), lambda i: (0, 0, 0)), pl.BlockSpec((1, cp), lambda i: (0, 0))]
    out, xo = _call(
        body, name=name, grid=(nt,),
        in_specs=[pl.BlockSpec((HALO, ci), prev_map), pl.BlockSpec((tl, ci), lambda i: (i, 0)),
                  pl.BlockSpec((tl, cc), lambda i: (i, 0)), pl.BlockSpec((HALO, cc), next_map),
                  pl.BlockSpec((tl, dcc), lambda i: (i, 0)), pl.BlockSpec((HALO, dcc), next_map),
                  full(ck), full(lg), full(lb), full(pw), full(ps), full(am)],
        out_specs=tuple([pl.BlockSpec((tl, ci), lambda i: (i, 0))] + acc_spec),
        out_shape=out_shape,
        scratch_shapes=[pltpu.VMEM((HALO + tl, cc), F32), pltpu.VMEM((HALO + tl, cp), F32),
                        pltpu.VMEM((tl + HALO, cc), F32), pltpu.VMEM((tl, cp), F32), pltpu.VMEM((tl + HALO, cp), F32),
                        pltpu.VMEM((kw, SUBLANES, cc), F32), pltpu.VMEM((SUBLANES, cc), F32),
                        pltpu.VMEM((SUBLANES, cc), F32), pltpu.VMEM((SUBLANES, cc), F32), pltpu.VMEM((SUBLANES, cp), F32)],
        sem=("arbitrary",), args=(z, z, u1, u1, dy, dy, ck, lg, lb, pw, ps, am), xchg=xchg)
    return out if xchg is None else (out, xo)


def _row_parts(nc, n=3):
    n = min(n, nc)
    cuts = [round(k * nc / n) for k in range(n + 1)]
    return [(cuts[k], cuts[k + 1]) for k in range(n)]


def _tap_rows(k_ref):
    return [jnp.broadcast_to(k_ref[j:j + 1, :], (SUBLANES, k_ref.shape[1])) for j in range(k_ref.shape[0])]


def _rows_of(tap, n):
    return tap if n == SUBLANES else jnp.concatenate([tap] * (n // SUBLANES), axis=0)


def _ffn_conv(win, taps, rows):
    kw = len(taps)
    o = FFN_PAD - (kw - 1)
    acc = _rows_of(taps[0], rows) * win[o:o + rows]
    for j in range(1, kw):
        acc = acc + _rows_of(taps[j], rows) * win[o + j:o + j + rows]
    return acc


def _ffn_block_fwd(h_mid, g, wup_t, kf, wdown, name, xchg=None):
    L, D = h_mid.shape
    f = wdown.shape[0]
    kw = kf.shape[0]
    tl = _token_tile(L)
    tc = _divisor(f, 256, 128)
    nj = f // tc
    nt = L // tl
    pad = 2 * SUBLANES
    hb = tl // pad
    rc = CONV3_ROWS
    parts = _row_parts(tl // rc)

    def body(hp_ref, hc_ref, g_ref, wg_ref, wv_ref, kg_ref, kv_ref, wd_ref, out_ref, hn_ref, act_ref, ux_ref, uc_ref,
             hn_halo, halo, ug_ref, acc):
        i = pl.program_id(0)
        kb = pl.program_id(1)

        @pl.when(kb == 0)
        def _():
            gg = g_ref[...]

            def norm(x):
                r = lax.rsqrt(jnp.mean(x * x, axis=-1, keepdims=True) + EPS)
                return ((x * r) * gg).astype(BF16)

            hn_halo[...] = jnp.where(i > 0, norm(hp_ref[...]), jnp.zeros((pad, D), BF16))
            hn_ref[...] = norm(hc_ref[...])
            acc[...] = jnp.zeros_like(acc)

        w_refs = (wg_ref, wv_ref)
        taps = (_tap_rows(kg_ref), _tap_rows(kv_ref))
        hh = hn_halo[...]
        for h in range(2):
            halo[h] = _dot_nt(hh, w_refs[h][...])[pad - FFN_PAD:]

        def up_part(lo, hi):
            a, b = lo * rc, hi * rc
            for h in range(2):
                ug_ref[h, a:b, :] = _dot_nt(hn_ref[a:b, :], w_refs[h][...])

        def down_part(lo, hi):
            a, b = lo * rc, hi * rc
            acc[a:b, :] += jnp.dot(act_ref[a:b, :], wd_ref[...], preferred_element_type=F32)

        def chunk_rows(lo, hi):
            for c in range(lo, hi):
                r0 = c * rc
                convd = []
                for h in range(2):
                    if c == 0:
                        win = jnp.concatenate([halo[h], ug_ref[h, 0:rc]], axis=0)
                    else:
                        win = ug_ref[h, r0 - FFN_PAD:r0 + rc]
                    convd.append(_ffn_conv(win, taps[h], rc))
                    ux_ref[h, r0:r0 + rc, :] = win[FFN_PAD:].astype(BF16)
                    uc_ref[h, r0:r0 + rc, :] = convd[h].astype(BF16)
                gate, val = convd
                act_ref[r0:r0 + rc, :] = ((gate * _sigmoid(gate)) * val).astype(BF16)

        for p, (lo, hi) in enumerate(parts):
            if p == 0:
                up_part(lo, hi)
            if p + 1 < len(parts):
                up_part(*parts[p + 1])
            if p > 0:
                down_part(*parts[p - 1])
            chunk_rows(lo, hi)
        down_part(*parts[-1])

        @pl.when(kb == nj - 1)
        def _():
            out_ref[...] = acc[...] + hc_ref[...]

    out, xo = _call(
        body, name=name, grid=(nt, nj),
        in_specs=[pl.BlockSpec((pad, D), lambda i, k: (jnp.maximum(i * hb - 1, 0), 0)),
                  pl.BlockSpec((tl, D), lambda i, k: (i, 0)),
                  pl.BlockSpec((1, D), lambda i, k: (0, 0)),
                  pl.BlockSpec((tc, D), lambda i, k: (k, 0)), pl.BlockSpec((tc, D), lambda i, k: (k + nj, 0)),
                  pl.BlockSpec((kw, tc), lambda i, k: (0, k)), pl.BlockSpec((kw, tc), lambda i, k: (0, k + nj)),
                  pl.BlockSpec((tc, D), lambda i, k: (k, 0))],
        out_specs=(pl.BlockSpec((tl, D), lambda i, k: (i, 0)), pl.BlockSpec((tl, D), lambda i, k: (i, 0)),
                   pl.BlockSpec((tl, tc), lambda i, k: (i, k)),
                   pl.BlockSpec((2, tl, tc), lambda i, k: (0, i, k)), pl.BlockSpec((2, tl, tc), lambda i, k: (0, i, k))),
        out_shape=(jax.ShapeDtypeStruct((L, D), F32), jax.ShapeDtypeStruct((L, D), BF16),
                   jax.ShapeDtypeStruct((L, f), BF16),
                   jax.ShapeDtypeStruct((2, L, f), BF16), jax.ShapeDtypeStruct((2, L, f), BF16)),
        scratch_shapes=[pltpu.VMEM((pad, D), BF16), pltpu.VMEM((2, FFN_PAD, tc), F32), pltpu.VMEM((2, tl, tc), F32),
                        pltpu.VMEM((tl, D), F32)],
        sem=("parallel", "arbitrary"), args=(h_mid, h_mid, g, wup_t, wup_t, kf, kf, wdown), xchg=xchg)
    return out if xchg is None else (out, xo)


def _ffn_block_bwd(dh, h_mid, g, ux, uc, kf, wdown, wup_t, name, xchg=None):
    L, D = dh.shape
    f = ux.shape[2]
    kw = kf.shape[0]
    tl = _token_tile(L)
    tc = _divisor(f, 256, 128)
    nj = f // tc
    nt = L // tl
    pad = 2 * SUBLANES
    rc = CONV3_ROWS
    nc = tl // rc
    parts = _row_parts(nc)

    def body(dhc_ref, dhn_ref, hm_ref, g_ref, xg_ref, xv_ref, cg_ref, cgn_ref, cv_ref, cvn_ref, kg_ref, kv_ref,
             wd_ref, wg_ref, wv_ref, dhm_ref, dg_ref, du_ref, dk_ref, dh_ext, dact_s, acc):
        i = pl.program_id(0)
        kb = pl.program_id(1)

        @pl.when(kb == 0)
        def _():
            dh_ext[0:tl, :] = dhc_ref[...].astype(BF16)
            dh_ext[tl:tl + pad, :] = dhn_ref[...].astype(BF16)
            acc[...] = jnp.zeros_like(acc)

        @pl.when(jnp.logical_and(i == 0, kb == 0))
        def _():
            dg_ref[...] = jnp.zeros_like(dg_ref)
            dk_ref[...] = jnp.zeros_like(dk_ref)

        x_refs, c_refs, nxt = (xg_ref, xv_ref), (cg_ref, cv_ref), (cgn_ref, cvn_ref)
        taps = (_tap_rows(kg_ref), _tap_rows(kv_ref))
        dk = [[jnp.zeros((SUBLANES, tc), F32) for _ in range(kw)] for _ in range(2)]

        def dact_part(lo, hi):
            a, b = lo * rc, hi * rc + pad
            dact_s[a:b, :] = _dot_nt(dh_ext[a:b, :], wd_ref[...])

        def dhn_part(lo, hi):
            a, b = lo * rc, hi * rc
            acc[a:b, :] += (jnp.dot(du_ref[0, a:b, :], wg_ref[...], preferred_element_type=F32)
                            + jnp.dot(du_ref[1, a:b, :], wv_ref[...], preferred_element_type=F32))

        for p, (lo, hi) in enumerate(parts):
            if p == 0:
                dact_part(lo, hi)
            if p + 1 < len(parts):
                dact_part(*parts[p + 1])
            if p > 0:
                dhn_part(*parts[p - 1])
            chunk_rows(lo, hi, x_refs, c_refs, nxt, taps, dk, i, dact_s, du_ref)
        dhn_part(*parts[-1])
        for h in range(2):
            for j in range(kw):
                dk_ref[kb, h, j:j + 1, :] += jnp.sum(dk[h][j], axis=0, keepdims=True)

        @pl.when(kb == nj - 1)
        def _():
            x = hm_ref[...]
            r = lax.rsqrt(jnp.mean(x * x, axis=-1, keepdims=True) + EPS)
            xhat = x * r
            dhn = acc[...]
            dxhat = dhn * g_ref[...]
            dhm_ref[...] = dhc_ref[...] + r * (dxhat - xhat * jnp.mean(dxhat * xhat, axis=-1, keepdims=True))
            dg_ref[...] += jnp.sum(_rowsum8(dhn * xhat), axis=0, keepdims=True)

    def chunk_rows(lo, hi, x_refs, c_refs, nxt, taps, dk, i, dact_s, du_ref):
        for c in range(lo, hi):
            r0 = c * rc
            n = rc + FFN_PAD
            convd = []
            for h in range(2):
                if c == nc - 1:
                    rows = jnp.concatenate([c_refs[h][r0:r0 + rc, :], nxt[h][...]], axis=0)
                else:
                    rows = c_refs[h][r0:r0 + rc + pad, :]
                convd.append(rows.astype(F32)[0:n])
            gate, val = convd
            xs = [x_refs[h][r0:r0 + rc, :].astype(F32) for h in range(2)]
            dact = dact_s[r0:r0 + n, :]
            sg = _sigmoid(gate)
            dcs = [dact * val * (sg * (1.0 + gate * (1.0 - sg))), dact * (gate * sg)]
            if c == nc - 1:
                live = jnp.logical_or(lax.broadcasted_iota(jnp.int32, (n, 1), 0) < rc, i < nt - 1)
                dcs = [jnp.where(live, d, 0.0) for d in dcs]
            for h in range(2):
                xc = xs[h]
                dx = None
                for j in range(kw):
                    o = kw - 1 - j
                    sh = dcs[h][o:o + rc]
                    term = _rows_of(taps[h][j], rc) * sh
                    dx = term if dx is None else dx + term
                    dk[h][j] = dk[h][j] + _rowsum8(xc * sh)
                du_ref[h, r0:r0 + rc, :] = dx.astype(BF16)

    def after(i):
        return jnp.minimum((i + 1) * (tl // pad), L // pad - 1)

    def half(h, rows, idx):
        return pl.BlockSpec((None, rows, tc), lambda i, k: (h,) + idx(i, k))

    def tile(i, k):
        return (i, k)

    def behind(i, k):
        return (after(i), k)

    out, xo = _call(
        body, name=name, grid=(nt, nj),
        in_specs=[pl.BlockSpec((tl, D), lambda i, k: (i, 0)),
                  pl.BlockSpec((pad, D), lambda i, k: (after(i), 0)),
                  pl.BlockSpec((tl, D), lambda i, k: (i, 0)), pl.BlockSpec((1, D), lambda i, k: (0, 0)),
                  half(0, tl, tile), half(1, tl, tile),
                  half(0, tl, tile), half(0, pad, behind), half(1, tl, tile), half(1, pad, behind),
                  pl.BlockSpec((kw, tc), lambda i, k: (0, k)), pl.BlockSpec((kw, tc), lambda i, k: (0, k + nj)),
                  pl.BlockSpec((tc, D), lambda i, k: (k, 0)),
                  pl.BlockSpec((tc, D), lambda i, k: (k, 0)), pl.BlockSpec((tc, D), lambda i, k: (k + nj, 0))],
        out_specs=(pl.BlockSpec((tl, D), lambda i, k: (i, 0)), pl.BlockSpec((1, D), lambda i, k: (0, 0)),
                   pl.BlockSpec((2, tl, tc), lambda i, k: (0, i, k)),
                   pl.BlockSpec((nj, 2, kw, tc), lambda i, k: (0, 0, 0, 0))),
        out_shape=(jax.ShapeDtypeStruct((L, D), F32), jax.ShapeDtypeStruct((1, D), F32),
                   jax.ShapeDtypeStruct((2, L, f), BF16), jax.ShapeDtypeStruct((nj, 2, kw, tc), F32)),
        scratch_shapes=[pltpu.VMEM((tl + pad, D), BF16), pltpu.VMEM((tl + pad, tc), F32), pltpu.VMEM((tl, D), F32)],
        sem=("arbitrary", "arbitrary"), args=(dh, dh, h_mid, g, ux, ux, uc, uc, uc, uc, kf, kf, wdown, wup_t, wup_t),
        xchg=xchg)
    return out if xchg is None else (out, xo)


def _adamw_math(w, g, m, v):
    m = ADAM_B1 * m + (1.0 - ADAM_B1) * g
    v = ADAM_B2 * v + (1.0 - ADAM_B2) * (g * g)
    m_hat = m / (1.0 - ADAM_B1 ** ADAM_STEP)
    v_hat = v / (1.0 - ADAM_B2 ** ADAM_STEP)
    delta = -ADAM_LR * (m_hat / (jnp.sqrt(v_hat) + ADAM_EPS) + ADAM_WD * w)
    return delta, m, v


def _sum_parts(parts_ref, idx):
    g = parts_ref[(0,) + idx].astype(F32)
    for q in range(1, N_DEV):
        g = g + parts_ref[(q,) + idx].astype(F32)
    return g


def _adamw_big(parts, w, m, v, name):
    nl, R, C = w.shape
    tr = _divisor(R, 256, 2 * SUBLANES)

    def body(*refs):
        p_refs = refs[:nl]
        w_ref, m_ref, v_ref, g_ref, d_ref, nm_ref, nv_ref = refs[nl:]
        layer = pl.program_id(0)
        for k in range(nl):
            @pl.when(layer == k)
            def _(k=k):
                g = _sum_parts(p_refs[k], ())
                d, nm, nv = _adamw_math(w_ref[0], g, m_ref[0], v_ref[0])
                g_ref[0] = g
                d_ref[0] = d
                nm_ref[0] = nm
                nv_ref[0] = nv

    def part_spec(k):
        return pl.BlockSpec((N_DEV, tr, C), lambda l, r: (0, jnp.where(l == k, r, 0), 0))

    blk = pl.BlockSpec((1, tr, C), lambda l, r: (l, r, 0))
    shp = jax.ShapeDtypeStruct((nl, R, C), F32)
    return pl.pallas_call(
        body, name=name, grid=(nl, R // tr),
        in_specs=[part_spec(k) for k in range(nl)] + [blk, blk, blk],
        out_specs=(blk, blk, blk, blk), out_shape=(shp, shp, shp, shp),
        compiler_params=_params(("arbitrary", "arbitrary")),
    )(*parts, w, m, v)


def _adamw_small(entries, name):
    n = len(entries)
    uniq = []
    for e in entries:
        if not any(e[0] is u for u in uniq):
            uniq.append(e[0])
    pidx = [next(k for k, u in enumerate(uniq) if u is e[0]) for e in entries]
    npart = len(uniq)

    def body(*refs):
        p_refs = refs[:npart]
        wmv = refs[npart:npart + 3 * n]
        outs = refs[npart + 3 * n:]
        for t, e in enumerate(entries):
            lo, w = e[1], e[2]
            rows = w.shape[0]
            pr = p_refs[pidx[t]]
            g = pr[0, lo:lo + rows].astype(F32)
            for q in range(1, N_DEV):
                g = g + pr[q, lo:lo + rows].astype(F32)
            d, nm, nv = _adamw_math(wmv[3 * t][...], g, wmv[3 * t + 1][...], wmv[3 * t + 2][...])
            outs[4 * t][...] = g
            outs[4 * t + 1][...] = d
            outs[4 * t + 2][...] = nm
            outs[4 * t + 3][...] = nv

    vm = pl.BlockSpec(memory_space=pltpu.VMEM)
    args = list(uniq)
    out_shape = []
    for e in entries:
        args += [e[2], e[3], e[4]]
        out_shape += [jax.ShapeDtypeStruct(e[2].shape, F32)] * 4
    res = pl.pallas_call(
        body, name=name, in_specs=[vm] * len(args), out_specs=tuple([vm] * len(out_shape)),
        out_shape=tuple(out_shape), compiler_params=_params(),
    )(*args)
    return [tuple(res[4 * t:4 * t + 4]) for t in range(n)]


def _head_matrix(cc):
    bw = min(256, cc)
    r = lax.broadcasted_iota(jnp.int32, (bw, bw), 0) // HEAD_DIM
    c = lax.broadcasted_iota(jnp.int32, (bw, bw), 1) // HEAD_DIM
    return jnp.where(r == c, 1.0 / HEAD_DIM, 0.0).astype(BF16)


def _cols_from_shards(g):
    nd = g.ndim
    perm = tuple(range(1, nd - 1)) + (0, nd - 1)
    t = jnp.transpose(g, perm)
    return t.reshape(t.shape[:-2] + (t.shape[-2] * t.shape[-1],))


def _cols_to_shards(a):
    nd = a.ndim
    t = a.reshape(a.shape[:-1] + (N_DEV, a.shape[-1] // N_DEV))
    perm = (nd - 1,) + tuple(range(nd - 1)) + (nd,)
    return jnp.transpose(t, perm)


def kernel(x, meta_tokens, norm1_g, w_in, conv_dw_k, conv_dw_b, conv_ln_g, conv_ln_b, pool_w, pool_scale, w_out, norm2_g, w_up, ffn_dw_k, w_down, final_g, loss_target, m_meta_tokens, m_norm1_g, m_w_in, m_conv_dw_k, m_conv_dw_b, m_conv_ln_g, m_conv_ln_b, m_pool_w, m_pool_scale, m_w_out, m_norm2_g, m_w_up, m_ffn_dw_k, m_w_down, m_final_g, v_meta_tokens, v_norm1_g, v_w_in, v_conv_dw_k, v_conv_dw_b, v_conv_ln_g, v_conv_ln_b, v_pool_w, v_pool_scale, v_w_out, v_norm2_g, v_w_up, v_ffn_dw_k, v_w_down, v_final_g):
    depth, D = norm1_g.shape
    n_meta = meta_tokens.shape[0]
    seq = x.shape[1]
    L = n_meta + seq
    cc = conv_dw_b.shape[1]
    ng, gd = pool_w.shape[1], pool_w.shape[2]
    f = w_down.shape[1] * N_DEV

    def rows(g):
        return g.reshape(-1, g.shape[-1])

    b16 = lambda a: a.astype(BF16)
    tr = lambda a: jnp.swapaxes(a, -1, -2)
    w_in_t, m_w_in_t, v_w_in_t = tr(w_in), tr(m_w_in), tr(v_w_in)
    w_up_t, m_w_up_t, v_w_up_t = tr(w_up), tr(m_w_up), tr(v_w_up)
    (g_in0, g_ck, g_kf, g_meta) = _exchange([b16(w_in_t[0]), conv_dw_k, ffn_dw_k, meta_tokens], ["gather"] * 4,
                                            "gather_first")
    ck_full = _cols_from_shards(g_ck)
    ck_rows = jnp.broadcast_to(ck_full[:, :, None, :], ck_full.shape[:2] + (SUBLANES, cc))
    kf_full = _cols_from_shards(g_kf)
    meta_full = _cols_from_shards(g_meta)
    am = _head_matrix(cc)
    win, wout, wup, wdown = [None] * depth, [None] * depth, [None] * depth, [None] * depth
    win[0] = rows(g_in0)

    h = (meta_full, x[0])
    saved = []
    for l in range(depth):
        more = l + 1 < depth
        if l == 0:
            (z, hn1), (g_out,) = _norm_proj(h, norm1_g[l:l + 1], win[l], f"in_proj_{l}", tn_cap=1536,
                                            xchg=([b16(w_out[l])], ["gather"]))
            wout[l] = rows(g_out)
            (ymix, u1), (g_up, g_down) = _mixer_fwd(z, ck_rows[l], conv_dw_b[l:l + 1], conv_ln_g[l:l + 1], conv_ln_b[l:l + 1],
                                                    pool_w[l], pool_scale[l:l + 1], am, f"mixer_fwd_{l}",
                                                    xchg=([b16(w_up_t[l]), b16(w_down[l])], ["gather"] * 2))
            wup[l], wdown[l] = rows(g_up), rows(g_down)
        else:
            z, hn1 = _norm_proj(h, norm1_g[l:l + 1], win[l], f"in_proj_{l}", tn_cap=1536)
            ymix, u1 = _mixer_fwd(z, ck_rows[l], conv_dw_b[l:l + 1], conv_ln_g[l:l + 1], conv_ln_b[l:l + 1], pool_w[l],
                                  pool_scale[l:l + 1], am, f"mixer_fwd_{l}")
        if more:
            h_mid, (g_in, g_out) = _mm(ymix, wout[l], f"out_proj_{l}", res=h, tn_cap=1024,
                                       xchg=([b16(w_in_t[l + 1]), b16(w_out[l + 1])], ["gather"] * 2))
            win[l + 1], wout[l + 1] = rows(g_in), rows(g_out)
            nxt = [b16(w_up_t[l + 1]), b16(w_down[l + 1])]
            (h_out, hn2, act, ux, uc), got = _ffn_block_fwd(h_mid, norm2_g[l:l + 1], wup[l], kf_full[l], wdown[l],
                                                            f"ffn_fwd_{l}", xchg=(nxt, ["gather"] * 2))
            wup[l + 1], wdown[l + 1] = rows(got[0]), rows(got[1])
        else:
            h_mid = _mm(ymix, wout[l], f"out_proj_{l}", res=h, tn_cap=1024)
            h_out, hn2, act, ux, uc = _ffn_block_fwd(h_mid, norm2_g[l:l + 1], wup[l], kf_full[l], wdown[l], f"ffn_fwd_{l}")
        saved.append((h, hn1, z, u1, ymix, h_mid, hn2, ux, uc, act))
        h = h_out

    dh, d_final_g, loss_part = _loss_head(h, final_g.reshape(1, D), loss_target[0], n_meta, "loss_head")

    def row_shards(gm):
        return gm.reshape(N_DEV, -1, gm.shape[-1])

    zero_row = jnp.zeros((1, D), F32)
    gw = {k: [None] * depth for k in ("ck", "cb", "lg", "lb", "pw", "ps", "kf", "n1", "n2")}
    parts = {k: [None] * depth for k in ("in", "out", "up", "down")}
    for l in reversed(range(depth)):
        h_in, hn1, z, u1, ymix, h_mid, hn2, ux, uc, act = saved[l]
        g_down = _mm_tn(act, dh, f"down_proj_wgrad_{l}", tq_cap=512)
        (dh_mid, gw["n2"][l], dug0, dkf), (parts["down"][l],) = _ffn_block_bwd(
            dh, h_mid, norm2_g[l:l + 1], ux, uc, kf_full[l], wdown[l], wup[l], f"ffn_bwd_{l}",
            xchg=([row_shards(g_down)], ["a2a"]))
        gw["kf"][l] = jnp.transpose(dkf, (2, 1, 0, 3)).reshape(dkf.shape[2], -1)
        g_up_t = _mm_tn(dug0, hn2, f"up_proj_wgrad_{l}", halves=2, tq_cap=1024)
        dymix = _mm(dh_mid, wout[l], f"out_proj_bwd_{l}", b_t=True, out_dtype=BF16, tn_cap=1024)
        g_out = _mm_tn(ymix, dh_mid, f"out_proj_wgrad_{l}", tq_cap=512)
        ((dz, gw["ck"][l], gw["cb"][l], gw["lg"][l], gw["lb"][l], gw["pw"][l], gw["ps"][l]),
         (parts["up"][l], parts["out"][l])) = _mixer_bwd(
            z, u1, dymix, ck_rows[l], conv_ln_g[l:l + 1], conv_ln_b[l:l + 1], pool_w[l], pool_scale[l:l + 1], am,
            f"mixer_bwd_{l}", xchg=([row_shards(g_up_t), row_shards(g_out)], ["a2a", "a2a"]))
        g_in_t = _mm_tn(dz, hn1, f"in_proj_wgrad_{l}", tq_cap=1024)
        if l > 0:
            (dh, gw["n1"][l]), (parts["in"][l],) = _proj_bwd_norm(dz, win[l], h_in, norm1_g[l:l + 1], dh_mid, zero_row,
                                                                  f"in_proj_bwd_{l}", xchg=([row_shards(g_in_t)], ["a2a"]))
        else:
            meta_rows, x_rows = h_in
            (grad_x, dg_x), (parts["in"][l],) = _proj_bwd_norm(dz, win[l], x_rows, norm1_g[l:l + 1], dh_mid, zero_row,
                                                               f"in_proj_bwd_{l}", skip=n_meta,
                                                               xchg=([row_shards(g_in_t)], ["a2a"]))
            d_meta, gw["n1"][l] = _proj_bwd_norm(dz[:n_meta], win[l], meta_rows, norm1_g[l:l + 1], dh_mid[:n_meta],
                                                 dg_x, f"in_proj_bwd_meta_{l}")
    grad_x = grad_x[None]

    pack_d = jnp.concatenate(gw["n1"] + gw["n2"] + [d_final_g, jnp.broadcast_to(loss_part[:, :1], (1, D)), zero_row, zero_row], axis=0)
    pack_c = jnp.concatenate(gw["cb"] + gw["lg"] + gw["lb"] + gw["ps"], axis=0)
    pack_pw = b16(jnp.stack(gw["pw"]).reshape(depth * ng * gd, gd))
    src = [_cols_to_shards(jnp.stack(gw["ck"])), _cols_to_shards(jnp.stack(gw["kf"])), _cols_to_shards(d_meta),
           pack_d, pack_c, pack_pw]
    r_ck, r_kf, r_meta, r_d, r_c, r_pw = _exchange(src, ["a2a"] * 3 + ["gather"] * 3, "exchange_small_grads")

    big = {
        "w_in": tuple(tr(a) for a in _adamw_big(parts["in"], w_in_t, m_w_in_t, v_w_in_t, "adamw_w_in")),
        "w_out": _adamw_big(parts["out"], w_out, m_w_out, v_w_out, "adamw_w_out"),
        "w_up": tuple(tr(a) for a in _adamw_big(parts["up"], w_up_t, m_w_up_t, v_w_up_t, "adamw_w_up")),
        "w_down": _adamw_big(parts["down"], w_down, m_w_down, v_w_down, "adamw_w_down"),
    }
    kwid = conv_dw_k.shape[1]
    fkw = ffn_dw_k.shape[1]
    row = lambda a: a.reshape(1, -1)
    entries = [
        (r_d, 0, norm1_g, m_norm1_g, v_norm1_g),
        (r_d, depth, norm2_g, m_norm2_g, v_norm2_g),
        (r_d, 2 * depth, row(final_g), row(m_final_g), row(v_final_g)),
        (r_c, 0, conv_dw_b, m_conv_dw_b, v_conv_dw_b),
        (r_c, depth, conv_ln_g, m_conv_ln_g, v_conv_ln_g),
        (r_c, 2 * depth, conv_ln_b, m_conv_ln_b, v_conv_ln_b),
        (r_c, 3 * depth, pool_scale, m_pool_scale, v_pool_scale),
        (r_pw, 0, pool_w.reshape(-1, gd), m_pool_w.reshape(-1, gd), v_pool_w.reshape(-1, gd)),
        (r_ck.reshape(N_DEV, depth * kwid, -1), 0, conv_dw_k.reshape(depth * kwid, -1),
         m_conv_dw_k.reshape(depth * kwid, -1), v_conv_dw_k.reshape(depth * kwid, -1)),
        (r_kf.reshape(N_DEV, depth * fkw, -1), 0, ffn_dw_k.reshape(depth * fkw, -1),
         m_ffn_dw_k.reshape(depth * fkw, -1), v_ffn_dw_k.reshape(depth * fkw, -1)),
        (r_meta, 0, meta_tokens, m_meta_tokens, v_meta_tokens),
        (r_d, 2 * depth + 1, zero_row, zero_row, zero_row),
    ]
    small = _adamw_small(entries, "adamw_small")
    names = ["norm1_g", "norm2_g", "final_g", "conv_dw_b", "conv_ln_g", "conv_ln_b", "pool_scale", "pool_w",
             "conv_dw_k", "ffn_dw_k", "meta_tokens"]
    shapes = {"final_g": final_g.shape, "pool_w": pool_w.shape, "conv_dw_k": conv_dw_k.shape, "ffn_dw_k": ffn_dw_k.shape}
    res = dict(big)
    for nme, quad in zip(names, small[:-1]):
        res[nme] = tuple(a.reshape(shapes[nme]) if nme in shapes else a for a in quad)
    loss = small[-1][0][0, 0]

    order = ["meta_tokens", "norm1_g", "w_in", "conv_dw_k", "conv_dw_b", "conv_ln_g", "conv_ln_b", "pool_w", "pool_scale",
             "w_out", "norm2_g", "w_up", "ffn_dw_k", "w_down", "final_g"]
    return (loss, grad_x, *[res[k][0] for k in order], *[res[k][1] for k in order], *[res[k][2] for k in order],
            *[res[k][3] for k in order])
```

```python
import functools

import jax
import jax.numpy as jnp
from jax import lax
from jax.experimental import pallas as pl
from jax.experimental.pallas import tpu as pltpu

F32 = jnp.float32
BF16 = jnp.bfloat16

EPS = 1e-6
HEAD_DIM = 64
POOL_WINDOWS = (2, 4, 8, 16)
ADAM_LR = 0.001
ADAM_B1 = 0.9
ADAM_B2 = 0.999
ADAM_EPS = 1e-08
ADAM_WD = 0.01
ADAM_STEP = 10

N_DEV = 8
OTHER_CHIPS = (2, 4, 6)
SUBLANES = 8
HALO = 48
CONV_PAD = 32
POOL_PAD = 16
FFN_PAD = 8
ROW_CHUNK = 24
CONV3_ROWS = 48
MAX_TILE_ROWS = 1024
WGRAD_TILE_ROWS = 2816
VMEM_LIMIT = 52 * 1024 * 1024


def _divisor(n, cap, mult):
    best = None
    for d in range(mult, min(n, cap) + 1, mult):
        if n % d == 0:
            best = d
    return n if best is None else best


def _token_tile(L):
    return _divisor(L, MAX_TILE_ROWS, HALO)


def _stat_rows(tl):
    return _divisor(tl, 512, SUBLANES)


def _params(sem=None):
    return pltpu.CompilerParams(dimension_semantics=sem, vmem_limit_bytes=VMEM_LIMIT)


def _rowsum8(x):
    acc = x[0:SUBLANES]
    for k in range(1, x.shape[0] // SUBLANES):
        acc = acc + x[k * SUBLANES:(k + 1) * SUBLANES]
    return acc


def _sigmoid(x):
    return jax.nn.sigmoid(x)


def _dot_nt(a, b):
    return lax.dot_general(a, b, (((1,), (1,)), ((), ())), preferred_element_type=F32)


def _head_mean(x, am_ref):
    bw = am_ref.shape[0]
    am = am_ref[...]
    outs = []
    for blk in range(x.shape[1] // bw):
        xb = x[:, blk * bw:(blk + 1) * bw]
        hi = xb.astype(BF16)
        lo = (xb - hi.astype(F32)).astype(BF16)
        outs.append(jnp.dot(hi, am, preferred_element_type=F32) + jnp.dot(lo, am, preferred_element_type=F32))
    return outs[0] if len(outs) == 1 else jnp.concatenate(outs, axis=-1)


def _xchg_out_shapes(srcs, modes):
    out = []
    for s, m in zip(srcs, modes):
        shp = ((N_DEV,) + tuple(s.shape)) if m == "gather" else tuple(s.shape)
        out.append(jax.ShapeDtypeStruct(shp, s.dtype))
    return out


def _xchg_sems(n):
    return [pltpu.SemaphoreType.DMA((n, N_DEV - 1)), pltpu.SemaphoreType.DMA((n, N_DEV - 1)), pltpu.SemaphoreType.DMA((n,))]


def _xchg_ops(src_refs, out_refs, sems, modes):
    n = len(src_refs)
    send_sems, recv_sems, local_sems = sems
    x, y, c = lax.axis_index("x"), lax.axis_index("y"), lax.axis_index("c")
    me = 4 * x + 2 * y + c

    def peer(d):
        return (x ^ ((d >> 2) & 1), y ^ ((d >> 1) & 1), c ^ (d & 1))

    def peer_id(d):
        px, py, pc = peer(d)
        return 4 * px + 2 * py + pc

    def remote(t, d):
        src = src_refs[t] if modes[t] == "gather" else src_refs[t].at[peer_id(d)]
        return pltpu.make_async_remote_copy(
            src_ref=src, dst_ref=out_refs[t].at[me], send_sem=send_sems.at[t, d - 1], recv_sem=recv_sems.at[t, d - 1],
            device_id=peer(d), device_id_type=pl.DeviceIdType.MESH)

    def arrival(t, d):
        src = src_refs[t] if modes[t] == "gather" else src_refs[t].at[me]
        return pltpu.make_async_remote_copy(
            src_ref=src, dst_ref=out_refs[t].at[peer_id(d)], send_sem=send_sems.at[t, d - 1],
            recv_sem=recv_sems.at[t, d - 1], device_id=peer(d), device_id_type=pl.DeviceIdType.MESH)

    def passed_on(t, d):
        blk = out_refs[t].at[peer_id(d)]
        return pltpu.make_async_remote_copy(
            src_ref=blk, dst_ref=blk, send_sem=send_sems.at[t, d], recv_sem=recv_sems.at[t, d],
            device_id=peer(1), device_id_type=pl.DeviceIdType.MESH)

    def local(t):
        src = src_refs[t] if modes[t] == "gather" else src_refs[t].at[me]
        return pltpu.make_async_copy(src, out_refs[t].at[me], local_sems.at[t])

    def sent_first(t):
        return OTHER_CHIPS + (1,) if modes[t] == "gather" else tuple(range(1, N_DEV))

    def start():
        for t in range(n):
            local(t).start()
        for t in range(n):
            for d in sent_first(t):
                remote(t, d).start()

    gathered = [t for t in range(n) if modes[t] == "gather"]

    def relay():
        for t in gathered:
            for d in OTHER_CHIPS:
                arrival(t, d).wait_recv()
                passed_on(t, d).start()

    def wait():
        for t in range(n):
            for d in range(1, N_DEV):
                if not (modes[t] == "gather" and d in OTHER_CHIPS):
                    arrival(t, d).wait_recv()
        for t in range(n):
            for d in sent_first(t):
                remote(t, d).wait_send()
        for t in gathered:
            for d in OTHER_CHIPS:
                passed_on(t, d).wait_send()
        for t in range(n):
            local(t).wait()

    return start, relay, wait


def _exchange(srcs, modes, name):
    n = len(srcs)

    def body(*refs):
        start, relay, wait = _xchg_ops(refs[:n], refs[n:2 * n], refs[2 * n:], modes)
        start()
        relay()
        wait()

    any_spec = pl.BlockSpec(memory_space=pl.ANY)
    return pl.pallas_call(
        body, name=name, out_shape=tuple(_xchg_out_shapes(srcs, modes)),
        in_specs=[any_spec] * n, out_specs=tuple([any_spec] * n),
        scratch_shapes=_xchg_sems(n),
        compiler_params=pltpu.CompilerParams(has_side_effects=True),
    )(*srcs)


def _call(body, *, name, grid, in_specs, out_specs, out_shape, args, scratch_shapes=(), sem=None, xchg=None):
    single = not isinstance(out_shape, (tuple, list))
    outs_shape = [out_shape] if single else list(out_shape)
    outs_spec = [out_specs] if single else list(out_specs)
    if xchg is None:
        res = pl.pallas_call(
            body, name=name, grid=grid, in_specs=list(in_specs), out_specs=out_specs, out_shape=out_shape,
            scratch_shapes=list(scratch_shapes), compiler_params=_params(sem))(*args)
        return res, ()
    srcs, modes = xchg
    n_in, n_out, n_scr, nx = len(in_specs), len(outs_shape), len(scratch_shapes), len(srcs)
    n_steps = functools.reduce(lambda a, b: a * b, grid, 1)
    relay_step = max(n_steps - 2, 0) if n_steps <= 16 else (3 * n_steps) // 4

    def wrapped(*refs):
        ins = refs[:n_in]
        xs = refs[n_in:n_in + nx]
        o0 = n_in + nx
        outs = refs[o0:o0 + n_out]
        xo = refs[o0 + n_out:o0 + n_out + nx]
        s0 = o0 + n_out + nx
        scr = refs[s0:s0 + n_scr]
        start, relay, wait = _xchg_ops(xs, xo, refs[s0 + n_scr:], modes)
        step = functools.reduce(lambda acc, a: acc * grid[a] + pl.program_id(a), range(len(grid)), 0)

        @pl.when(step == 0)
        def _():
            start()

        body(*ins, *outs, *scr)

        @pl.when(step == relay_step)
        def _():
            relay()

        @pl.when(step == n_steps - 1)
        def _():
            wait()

    any_spec = pl.BlockSpec(memory_space=pl.ANY)
    res = pl.pallas_call(
        wrapped, name=name, grid=grid, in_specs=list(in_specs) + [any_spec] * nx,
        out_specs=tuple(outs_spec + [any_spec] * nx), out_shape=tuple(outs_shape + _xchg_out_shapes(srcs, modes)),
        scratch_shapes=list(scratch_shapes) + _xchg_sems(nx),
        compiler_params=_params(("arbitrary",) * len(grid)))(*args, *srcs)
    comp = res[:n_out]
    return (comp[0] if single else tuple(comp)), tuple(res[n_out:])


def _seq_rows(h):
    if isinstance(h, tuple):
        return h[0].shape[0] + h[1].shape[0], h[1].shape[1]
    return h.shape


def _seq_tiles(h, tm):
    if not isinstance(h, tuple):
        return [pl.BlockSpec((tm, h.shape[1]), lambda i, *_: (i, 0))], [h], lambda refs, i: refs[0][...]
    meta, x = h
    n, D = meta.shape

    def read(refs, i):
        t = refs[1][...]
        first = jnp.concatenate([refs[0][...], pltpu.roll(t, n, axis=0)[n:]], axis=0)
        return jnp.where(i == 0, first, t)

    window = pl.BlockSpec((pl.Element(tm), pl.Element(D)),
                          lambda i, *_: (pl.multiple_of(jnp.maximum(i * tm - n, 0), SUBLANES), 0))
    return [pl.BlockSpec((n, D), lambda *_: (0, 0)), window], [meta, x], read


def _norm_proj(h, g, w, name, *, tn_cap, xchg=None):
    L, D = _seq_rows(h)
    N = w.shape[0]
    tm = _token_tile(L)
    tn = _divisor(N, tn_cap, 128)
    h_specs, h_args, read_h = _seq_tiles(h, tm)
    nh = len(h_specs)

    def body(*refs):
        g_ref, w_ref, z_ref, hn_ref = refs[nh:]

        @pl.when(pl.program_id(1) == 0)
        def _():
            x = read_h(refs[:nh], pl.program_id(0))
            r = lax.rsqrt(jnp.mean(x * x, axis=-1, keepdims=True) + EPS)
            hn_ref[...] = ((x * r) * g_ref[...]).astype(BF16)

        z_ref[...] = _dot_nt(hn_ref[...], w_ref[...])

    out, xo = _call(
        body, name=name, grid=(L // tm, N // tn),
        in_specs=h_specs + [pl.BlockSpec((1, D), lambda i, j: (0, 0)), pl.BlockSpec((tn, D), lambda i, j: (j, 0))],
        out_specs=(pl.BlockSpec((tm, tn), lambda i, j: (i, j)), pl.BlockSpec((tm, D), lambda i, j: (i, 0))),
        out_shape=(jax.ShapeDtypeStruct((L, N), F32), jax.ShapeDtypeStruct((L, D), BF16)),
        sem=("parallel", "arbitrary"), args=(*h_args, g, w), xchg=xchg)
    return out if xchg is None else (out, xo)


def _proj_bwd_norm(a, b, h, g, dres, dg0, name, skip=0, xchg=None):
    L, K = a.shape
    D = b.shape[1]
    rows = L - skip
    tm = _divisor(rows, MAX_TILE_ROWS, 2 * SUBLANES) if skip else _token_tile(L)

    def body(a_ref, b_ref, h_ref, g_ref, dres_ref, dg0_ref, dh_ref, dg_ref):
        i = pl.program_id(0)
        dhn = jnp.dot(a_ref[...], b_ref[...], preferred_element_type=F32)
        x = h_ref[...]
        r = lax.rsqrt(jnp.mean(x * x, axis=-1, keepdims=True) + EPS)
        xhat = x * r
        dxhat = dhn * g_ref[...]
        dh_ref[...] = dres_ref[...] + r * (dxhat - xhat * jnp.mean(dxhat * xhat, axis=-1, keepdims=True))
        part = jnp.sum(_rowsum8(dhn * xhat), axis=0, keepdims=True)

        @pl.when(i == 0)
        def _():
            dg_ref[...] = dg0_ref[...] + part

        @pl.when(i > 0)
        def _():
            dg_ref[...] += part

    def rows_of(cols, first=skip):
        if not first:
            return pl.BlockSpec((tm, cols), lambda i: (i, 0))
        return pl.BlockSpec((pl.Element(tm), pl.Element(cols)), lambda i: (pl.multiple_of(first + i * tm, SUBLANES), 0))

    row = pl.BlockSpec((1, D), lambda i: (0, 0))
    h_first = skip if h.shape[0] == L else 0
    out, xo = _call(
        body, name=name, grid=(rows // tm,),
        in_specs=[rows_of(K), pl.BlockSpec((K, D), lambda i: (0, 0)), rows_of(D, h_first), row, rows_of(D), row],
        out_specs=(pl.BlockSpec((tm, D), lambda i: (i, 0)), row),
        out_shape=(jax.ShapeDtypeStruct((rows, D), F32), jax.ShapeDtypeStruct((1, D), F32)),
        sem=("arbitrary",), args=(a, b, h, g, dres, dg0), xchg=xchg)
    return out if xchg is None else (out, xo)


def _mm(a, b, name, *, res=None, b_t=False, out_dtype=F32, tn_cap=1408, xchg=None):
    M, K = a.shape
    N = b.shape[0] if b_t else b.shape[1]
    tm = _token_tile(M)
    tn = _divisor(N, tn_cap, 128)
    if isinstance(res, tuple):
        assert tn == N
        r_specs, r_args, read_r = _seq_tiles(res, tm)
    elif res is not None:
        r_specs, r_args, read_r = [pl.BlockSpec((tm, tn), lambda i, j: (i, j))], [res], lambda refs, i: refs[0][...]
    else:
        r_specs, r_args, read_r = [], [], None

    def body(*refs):
        a_ref, b_ref, o_ref = refs[0], refs[1], refs[-1]
        av = a_ref[...].astype(BF16)
        prod = _dot_nt(av, b_ref[...]) if b_t else jnp.dot(av, b_ref[...], preferred_element_type=F32)
        o_ref[...] = (prod if read_r is None else prod + read_r(refs[2:-1], pl.program_id(0))).astype(out_dtype)

    b_spec = pl.BlockSpec((tn, K), lambda i, j: (j, 0)) if b_t else pl.BlockSpec((K, tn), lambda i, j: (0, j))
    in_specs = [pl.BlockSpec((tm, K), lambda i, j: (i, 0)), b_spec] + r_specs
    args = [a, b] + r_args
    out, xo = _call(
        body, name=name, grid=(M // tm, N // tn), in_specs=in_specs,
        out_specs=pl.BlockSpec((tm, tn), lambda i, j: (i, j)), out_shape=jax.ShapeDtypeStruct((M, N), out_dtype),
        sem=("parallel", "parallel"), args=args, xchg=xchg)
    return out if xchg is None else (out, xo)


def _mm_tn(a, b, name, *, halves=1, tq_cap=1408):
    L, Q = b.shape
    ph = a.shape[-1]
    P = ph * halves
    tl = _divisor(L, WGRAD_TILE_ROWS, HALO)
    tp = _divisor(ph, 1408, 128)
    tq = _divisor(Q, tq_cap, 128)
    pper = ph // tp
    nl = L // tl
    grid = (P // tp, Q // tq, nl)

    def body(a_ref, b_ref, o_ref, acc):
        prod = lax.dot_general(a_ref[...].astype(BF16), b_ref[...].astype(BF16), (((0,), (0,)), ((), ())),
                               preferred_element_type=F32)
        l = pl.program_id(2)
        if nl == 1:
            o_ref[...] = prod.astype(BF16)
            return

        @pl.when(l == 0)
        def _():
            acc[...] = prod

        @pl.when(jnp.logical_and(l > 0, l < nl - 1))
        def _():
            acc[...] += prod

        @pl.when(l == nl - 1)
        def _():
            o_ref[...] = (acc[...] + prod).astype(BF16)

    if halves > 1:
        a_spec = pl.BlockSpec((None, tl, tp), lambda p, q, l: (p // pper, l, p % pper))
    else:
        a_spec = pl.BlockSpec((tl, tp), lambda p, q, l: (l, p))
    return pl.pallas_call(
        body, name=name, grid=grid,
        in_specs=[a_spec, pl.BlockSpec((tl, tq), lambda p, q, l: (l, q))],
        out_specs=pl.BlockSpec((tp, tq), lambda p, q, l: (p, q)),
        out_shape=jax.ShapeDtypeStruct((P, Q), BF16),
        scratch_shapes=[pltpu.VMEM((tp, tq), F32)],
        compiler_params=_params(("parallel", "parallel", "arbitrary")),
    )(a, b)


def _loss_head(h, g, tgt, n_meta, name):
    L, D = h.shape
    tl = _token_tile(L)
    nt = L // tl

    def body(h_ref, g_ref, t_ref, dh_ref, dg_ref, loss_ref):
        i = pl.program_id(0)
        x = h_ref[...]
        r = lax.rsqrt(jnp.mean(x * x, axis=-1, keepdims=True) + EPS)
        xhat = x * r
        gg = g_ref[...]
        y = xhat * gg
        rows = i * tl + lax.broadcasted_iota(jnp.int32, (tl, 1), 0)
        t = t_ref[...]
        t = jnp.where(i == 0, pltpu.roll(t, n_meta, axis=0), t)
        err = jnp.where(rows >= n_meta, y - t, 0.0)
        dy = err * (1.0 / D)
        dxhat = dy * gg
        dh_ref[...] = r * (dxhat - xhat * jnp.mean(dxhat * xhat, axis=-1, keepdims=True))
        dg_part = jnp.sum(_rowsum8(dy * xhat), axis=0, keepdims=True)
        per_row = jnp.mean(err * err, axis=-1, keepdims=True)
        loss_part = jnp.broadcast_to(0.5 * jnp.sum(per_row, axis=0, keepdims=True), (1, 128))

        @pl.when(i == 0)
        def _():
            dg_ref[...] = dg_part
            loss_ref[...] = loss_part

        @pl.when(i > 0)
        def _():
            dg_ref[...] += dg_part
            loss_ref[...] += loss_part

    tile = pl.BlockSpec((tl, D), lambda i: (i, 0))
    row = pl.BlockSpec((1, D), lambda i: (0, 0))
    window = pl.BlockSpec((pl.Element(tl), pl.Element(D)),
                          lambda i: (pl.multiple_of(jnp.maximum(i * tl - n_meta, 0), SUBLANES), 0))
    return pl.pallas_call(
        body, name=name, grid=(nt,), in_specs=[tile, row, window],
        out_specs=(tile, row, pl.BlockSpec((1, 128), lambda i: (0, 0))),
        out_shape=(jax.ShapeDtypeStruct((L, D), F32), jax.ShapeDtypeStruct((1, D), F32),
                   jax.ShapeDtypeStruct((1, 128), F32)),
        compiler_params=_params(("arbitrary",)),
    )(h, g, tgt)


def _pool_fwd_block(pwin, pw_ref, row0, rb, g, gd, w, t0):
    wv = pwin[pl.ds(row0 + HALO - POOL_PAD, rb + POOL_PAD), g * gd:(g + 1) * gd]
    s = wv
    sh = 1
    while sh < w:
        s = s + pltpu.roll(s, sh, axis=0)
        sh *= 2
    win = s[POOL_PAD:POOL_PAD + rb]
    pt = wv[POOL_PAD:POOL_PAD + rb]
    tg = t0 + lax.broadcasted_iota(jnp.int32, (rb, 1), 0)
    cnt = jnp.minimum(tg + 1, w).astype(F32)
    return win / cnt - pt


def _fill_windows(i, zp_ref, zc_ref, u0w, pwin, tl, cc):
    keep = i > 0
    zp = zp_ref[...]
    u0w[0:HALO, :] = jnp.where(keep, zp[:, :cc] * _sigmoid(zp[:, cc:2 * cc]), 0.0)
    pwin[0:HALO, :] = jnp.where(keep, zp[:, 2 * cc:], 0.0)

    def fill(c, carry):
        b = pl.multiple_of(c * ROW_CHUNK, SUBLANES)
        zc = zc_ref[pl.ds(b, ROW_CHUNK), :]
        u0w[pl.ds(HALO + b, ROW_CHUNK), :] = zc[:, :cc] * _sigmoid(zc[:, cc:2 * cc])
        pwin[pl.ds(HALO + b, ROW_CHUNK), :] = zc[:, 2 * cc:]
        return carry

    lax.fori_loop(0, tl // ROW_CHUNK, fill, 0)


def _mixer_fwd(z, ck, cb, lg, lb, pw, ps, am, name, xchg=None):
    L, ci = z.shape
    kw, _, cc = ck.shape
    cp = ci - 2 * cc
    ng, gd = pw.shape[0], pw.shape[1]
    tl = _token_tile(L)
    nt = L // tl
    hb = tl // HALO
    rb = _stat_rows(tl)
    tap0 = CONV_PAD - (kw - 1)

    def body(zp_ref, zc_ref, ck_ref, cb_ref, lg_ref, lb_ref, pw_ref, ps_ref, am_ref, y_ref, u1_ref, u0w, pwin):
        i = pl.program_id(0)
        _fill_windows(i, zp_ref, zc_ref, u0w, pwin, tl, cc)

        def conv(c, carry):
            b = pl.multiple_of(c * ROW_CHUNK, SUBLANES)
            w = u0w[pl.ds(b + HALO - CONV_PAD, ROW_CHUNK + CONV_PAD), :]
            acc = jnp.broadcast_to(cb_ref[...], (ROW_CHUNK, cc))
            for j in range(kw):
                acc = acc + _rows_of(ck_ref[j], ROW_CHUNK) * w[tap0 + j:tap0 + j + ROW_CHUNK]
            u1_ref[pl.ds(b, ROW_CHUNK), :] = acc
            return carry

        lax.fori_loop(0, tl // ROW_CHUNK, conv, 0)

        def blocks(k, carry):
            b = pl.multiple_of(k * rb, SUBLANES)
            u1 = u1_ref[pl.ds(b, rb), :]
            xc = u1 - _head_mean(u1, am_ref)
            var = _head_mean(xc * xc, am_ref)
            u2 = (xc * lax.rsqrt(var + EPS)) * lg_ref[...] + lb_ref[...]
            y_ref[pl.ds(b, rb), 0:cc] = (u2 * _sigmoid(u2)).astype(y_ref.dtype)
            for g in range(ng):
                d = _pool_fwd_block(pwin, pw_ref, b, rb, g, gd, POOL_WINDOWS[g], i * tl + b)
                yp = jnp.dot(d.astype(BF16), pw_ref[g].astype(BF16), preferred_element_type=F32)
                yp = yp * ps_ref[:, g * gd:(g + 1) * gd]
                y_ref[pl.ds(b, rb), cc + g * gd:cc + (g + 1) * gd] = yp.astype(y_ref.dtype)
            return carry

        lax.fori_loop(0, tl // rb, blocks, 0)

    def full(a):
        nd = a.ndim
        return pl.BlockSpec(a.shape, lambda i: (0,) * nd)

    out, xo = _call(
        body, name=name, grid=(nt,),
        in_specs=[pl.BlockSpec((HALO, ci), lambda i: (jnp.maximum(i * hb - 1, 0), 0)),
                  pl.BlockSpec((tl, ci), lambda i: (i, 0)),
                  full(ck), full(cb), full(lg), full(lb), full(pw), full(ps), full(am)],
        out_specs=(pl.BlockSpec((tl, cc + cp), lambda i: (i, 0)), pl.BlockSpec((tl, cc), lambda i: (i, 0))),
        out_shape=(jax.ShapeDtypeStruct((L, cc + cp), BF16), jax.ShapeDtypeStruct((L, cc), F32)),
        scratch_shapes=[pltpu.VMEM((HALO + tl, cc), F32), pltpu.VMEM((HALO + tl, cp), F32)],
        sem=("parallel",), args=(z, z, ck, cb, lg, lb, pw, ps, am), xchg=xchg)
    return out if xchg is None else (out, xo)


def _mixer_bwd(z, u1, dy, ck, lg, lb, pw, ps, am, name, xchg=None):
    L, ci = z.shape
    kw, _, cc = ck.shape
    cp = ci - 2 * cc
    ng, gd = pw.shape[0], pw.shape[1]
    tl = _token_tile(L)
    nt = L // tl
    hb = tl // HALO
    rb = _stat_rows(tl)

    def body(zp_ref, zc_ref, u1c_ref, u1n_ref, dyc_ref, dyn_ref, ck_ref, lg_ref, lb_ref, pw_ref, ps_ref, am_ref,
             dz_ref, dck_ref, dcb_ref, dlg_ref, dlb_ref, dpw_ref, dps_ref,
             u0w, pwin, du1w, ddw, ew, dkacc, dcb8, dlg8, dlb8, dps8):
        i = pl.program_id(0)
        has_next = i < nt - 1

        @pl.when(i == 0)
        def _():
            dck_ref[...] = jnp.zeros_like(dck_ref)
            dcb_ref[...] = jnp.zeros_like(dcb_ref)
            dlg_ref[...] = jnp.zeros_like(dlg_ref)
            dlb_ref[...] = jnp.zeros_like(dlb_ref)
            dpw_ref[...] = jnp.zeros_like(dpw_ref)
            dps_ref[...] = jnp.zeros_like(dps_ref)

        dkacc[...] = jnp.zeros_like(dkacc)
        dcb8[...] = jnp.zeros_like(dcb8)
        dlg8[...] = jnp.zeros_like(dlg8)
        dlb8[...] = jnp.zeros_like(dlb8)
        dps8[...] = jnp.zeros_like(dps8)

        _fill_windows(i, zp_ref, zc_ref, u0w, pwin, tl, cc)

        def conv_side(u1, dyc, own):
            xc = u1 - _head_mean(u1, am_ref)
            rstd = lax.rsqrt(_head_mean(xc * xc, am_ref) + EPS)
            uh = xc * rstd
            lgv = lg_ref[...]
            u2 = uh * lgv + lb_ref[...]
            sg = _sigmoid(u2)
            du2 = dyc * (sg * (1.0 + u2 * (1.0 - sg)))
            if own:
                dlg8[...] += _rowsum8(du2 * uh)
                dlb8[...] += _rowsum8(du2)
            duh = du2 * lgv
            return rstd * (duh - _head_mean(duh, am_ref) - uh * _head_mean(duh * uh, am_ref))

        def pool_side(dyp, t0, rows):
            dds, es = [], []
            tg = t0 + lax.broadcasted_iota(jnp.int32, (rows, 1), 0)
            for g in range(ng):
                dypre = dyp[:, g * gd:(g + 1) * gd] * ps_ref[:, g * gd:(g + 1) * gd]
                dd = lax.dot_general(dypre.astype(BF16), pw_ref[g].astype(BF16), (((1,), (1,)), ((), ())),
                                     preferred_element_type=F32)
                cnt = jnp.minimum(tg + 1, POOL_WINDOWS[g]).astype(F32)
                dds.append(dd)
                es.append(dd / cnt)
            return jnp.concatenate(dds, axis=-1), jnp.concatenate(es, axis=-1)

        def blocks(k, carry):
            b = pl.multiple_of(k * rb, SUBLANES)
            dyb = dyc_ref[pl.ds(b, rb), :].astype(F32)
            du1 = conv_side(u1c_ref[pl.ds(b, rb), :], dyb[:, :cc], True)
            du1w[pl.ds(b, rb), :] = du1
            dcb8[...] += _rowsum8(du1)
            dyp = dyb[:, cc:]
            dd, e = pool_side(dyp, i * tl + b, rb)
            ddw[pl.ds(b, rb), :] = dd
            ew[pl.ds(b, rb), :] = e
            for g in range(ng):
                d = _pool_fwd_block(pwin, pw_ref, b, rb, g, gd, POOL_WINDOWS[g], i * tl + b)
                db16 = d.astype(BF16)
                dypg = dyp[:, g * gd:(g + 1) * gd]
                ypre = jnp.dot(db16, pw_ref[g].astype(BF16), preferred_element_type=F32)
                dps8[:, g * gd:(g + 1) * gd] += _rowsum8(dypg * ypre)
                dypre = (dypg * ps_ref[:, g * gd:(g + 1) * gd]).astype(BF16)
                dpw_ref[g] += lax.dot_general(db16, dypre, (((0,), (0,)), ((), ())), preferred_element_type=F32)
            return carry

        lax.fori_loop(0, tl // rb, blocks, 0)

        dyn = dyn_ref[...].astype(F32)
        du1n = conv_side(u1n_ref[...], dyn[:, :cc], False)
        du1w[tl:tl + HALO, :] = jnp.where(has_next, du1n, 0.0)
        ddn, en = pool_side(dyn[:, cc:], (i + 1) * tl, HALO)
        ew[tl:tl + HALO, :] = jnp.where(has_next, en, 0.0)

        def taps(c, carry):
            b = pl.multiple_of(c * ROW_CHUNK, SUBLANES)
            w = du1w[pl.ds(b, ROW_CHUNK + CONV_PAD), :]
            u0c = u0w[pl.ds(HALO + b, ROW_CHUNK), :]
            acc = jnp.zeros((ROW_CHUNK, cc), F32)
            for j in range(kw):
                o = kw - 1 - j
                sh = w[o:o + ROW_CHUNK]
                acc = acc + _rows_of(ck_ref[j], ROW_CHUNK) * sh
                dkacc[j] += _rowsum8(u0c * sh)
            zc = zc_ref[pl.ds(b, ROW_CHUNK), :]
            a = zc[:, :cc]
            sg = _sigmoid(zc[:, cc:2 * cc])
            dz_ref[pl.ds(b, ROW_CHUNK), 0:cc] = (acc * sg).astype(dz_ref.dtype)
            dz_ref[pl.ds(b, ROW_CHUNK), cc:2 * cc] = (acc * a * sg * (1.0 - sg)).astype(dz_ref.dtype)
            return carry

        lax.fori_loop(0, tl // ROW_CHUNK, taps, 0)

        def pool_back(k, carry):
            b = pl.multiple_of(k * rb, SUBLANES)
            n = rb + POOL_PAD
            for g in range(ng):
                s = ew[pl.ds(b, n), g * gd:(g + 1) * gd]
                sh = 1
                while sh < POOL_WINDOWS[g]:
                    s = s + pltpu.roll(s, n - sh, axis=0)
                    sh *= 2
                dp = s[0:rb] - ddw[pl.ds(b, rb), g * gd:(g + 1) * gd]
                dz_ref[pl.ds(b, rb), 2 * cc + g * gd:2 * cc + (g + 1) * gd] = dp.astype(dz_ref.dtype)
            return carry

        lax.fori_loop(0, tl // rb, pool_back, 0)

        dck_ref[...] += jnp.sum(dkacc[...], axis=1)
        dcb_ref[...] += jnp.sum(dcb8[...], axis=0, keepdims=True)
        dlg_ref[...] += jnp.sum(dlg8[...], axis=0, keepdims=True)
        dlb_ref[...] += jnp.sum(dlb8[...], axis=0, keepdims=True)
        dps_ref[...] += jnp.sum(dps8[...], axis=0, keepdims=True)

    def full(a):
        nd = a.ndim
        return pl.BlockSpec(a.shape, lambda i: (0,) * nd)

    nhb = L // HALO

    def prev_map(i):
        return (jnp.maximum(i * hb - 1, 0), 0)

    def next_map(i):
        return (jnp.minimum((i + 1) * hb, nhb - 1), 0)

    dcc = cc + cp
    row_cc = jax.ShapeDtypeStruct((1, cc), F32)
    out_shape = (jax.ShapeDtypeStruct((L, ci), BF16), jax.ShapeDtypeStruct((kw, cc), F32), row_cc, row_cc, row_cc,
                 jax.ShapeDtypeStruct((ng, gd, gd), F32), jax.ShapeDtypeStruct((1, cp), F32))
    acc_spec = [pl.BlockSpec((kw, cc), lambda i: (0, 0))] + [pl.BlockSpec((1, cc), lambda i: (0, 0))] * 3 + [
        pl.BlockSpec((ng, gd, gd), lambda i: (0, 0, 0)), pl.BlockSpec((1, cp), lambda i: (0, 0))]
    out, xo = _call(
        body, name=name, grid=(nt,),
        in_specs=[pl.BlockSpec((HALO, ci), prev_map), pl.BlockSpec((tl, ci), lambda i: (i, 0)),
                  pl.BlockSpec((tl, cc), lambda i: (i, 0)), pl.BlockSpec((HALO, cc), next_map),
                  pl.BlockSpec((tl, dcc), lambda i: (i, 0)), pl.BlockSpec((HALO, dcc), next_map),
                  full(ck), full(lg), full(lb), full(pw), full(ps), full(am)],
        out_specs=tuple([pl.BlockSpec((tl, ci), lambda i: (i, 0))] + acc_spec),
        out_shape=out_shape,
        scratch_shapes=[pltpu.VMEM((HALO + tl, cc), F32), pltpu.VMEM((HALO + tl, cp), F32),
                        pltpu.VMEM((tl + HALO, cc), F32), pltpu.VMEM((tl, cp), F32), pltpu.VMEM((tl + HALO, cp), F32),
                        pltpu.VMEM((kw, SUBLANES, cc), F32), pltpu.VMEM((SUBLANES, cc), F32),
                        pltpu.VMEM((SUBLANES, cc), F32), pltpu.VMEM((SUBLANES, cc), F32), pltpu.VMEM((SUBLANES, cp), F32)],
        sem=("arbitrary",), args=(z, z, u1, u1, dy, dy, ck, lg, lb, pw, ps, am), xchg=xchg)
    return out if xchg is None else (out, xo)


def _row_parts(nc, n=3):
    n = min(n, nc)
    cuts = [round(k * nc / n) for k in range(n + 1)]
    return [(cuts[k], cuts[k + 1]) for k in range(n)]


def _tap_rows(k_ref):
    return [jnp.broadcast_to(k_ref[j:j + 1, :], (SUBLANES, k_ref.shape[1])) for j in range(k_ref.shape[0])]


def _rows_of(tap, n):
    return tap if n == SUBLANES else jnp.concatenate([tap] * (n // SUBLANES), axis=0)


def _ffn_conv(win, taps, rows):
    kw = len(taps)
    o = FFN_PAD - (kw - 1)
    acc = _rows_of(taps[0], rows) * win[o:o + rows]
    for j in range(1, kw):
        acc = acc + _rows_of(taps[j], rows) * win[o + j:o + j + rows]
    return acc


def _ffn_block_fwd(h_mid, g, wup_t, kf, wdown, name, xchg=None):
    L, D = h_mid.shape
    f = wdown.shape[0]
    kw = kf.shape[0]
    tl = _token_tile(L)
    tc = _divisor(f, 256, 128)
    nj = f // tc
    nt = L // tl
    pad = 2 * SUBLANES
    hb = tl // pad
    rc = CONV3_ROWS
    parts = _row_parts(tl // rc)

    def body(hp_ref, hc_ref, g_ref, wg_ref, wv_ref, kg_ref, kv_ref, wd_ref, out_ref, hn_ref, act_ref, ux_ref, uc_ref,
             hn_halo, halo, ug_ref, acc):
        i = pl.program_id(0)
        kb = pl.program_id(1)

        @pl.when(kb == 0)
        def _():
            gg = g_ref[...]

            def norm(x):
                r = lax.rsqrt(jnp.mean(x * x, axis=-1, keepdims=True) + EPS)
                return ((x * r) * gg).astype(BF16)

            hn_halo[...] = jnp.where(i > 0, norm(hp_ref[...]), jnp.zeros((pad, D), BF16))
            hn_ref[...] = norm(hc_ref[...])
            acc[...] = jnp.zeros_like(acc)

        w_refs = (wg_ref, wv_ref)
        taps = (_tap_rows(kg_ref), _tap_rows(kv_ref))
        hh = hn_halo[...]
        for h in range(2):
            halo[h] = _dot_nt(hh, w_refs[h][...])[pad - FFN_PAD:]

        def up_part(lo, hi):
            a, b = lo * rc, hi * rc
            for h in range(2):
                ug_ref[h, a:b, :] = _dot_nt(hn_ref[a:b, :], w_refs[h][...])

        def down_part(lo, hi):
            a, b = lo * rc, hi * rc
            acc[a:b, :] += jnp.dot(act_ref[a:b, :], wd_ref[...], preferred_element_type=F32)

        def chunk_rows(lo, hi):
            for c in range(lo, hi):
                r0 = c * rc
                convd = []
                for h in range(2):
                    if c == 0:
                        win = jnp.concatenate([halo[h], ug_ref[h, 0:rc]], axis=0)
                    else:
                        win = ug_ref[h, r0 - FFN_PAD:r0 + rc]
                    convd.append(_ffn_conv(win, taps[h], rc))
                    ux_ref[h, r0:r0 + rc, :] = win[FFN_PAD:].astype(BF16)
                    uc_ref[h, r0:r0 + rc, :] = convd[h].astype(BF16)
                gate, val = convd
                act_ref[r0:r0 + rc, :] = ((gate * _sigmoid(gate)) * val).astype(BF16)

        for p, (lo, hi) in enumerate(parts):
            if p == 0:
                up_part(lo, hi)
            if p + 1 < len(parts):
                up_part(*parts[p + 1])
            if p > 0:
                down_part(*parts[p - 1])
            chunk_rows(lo, hi)
        down_part(*parts[-1])

        @pl.when(kb == nj - 1)
        def _():
            out_ref[...] = acc[...] + hc_ref[...]

    out, xo = _call(
        body, name=name, grid=(nt, nj),
        in_specs=[pl.BlockSpec((pad, D), lambda i, k: (jnp.maximum(i * hb - 1, 0), 0)),
                  pl.BlockSpec((tl, D), lambda i, k: (i, 0)),
                  pl.BlockSpec((1, D), lambda i, k: (0, 0)),
                  pl.BlockSpec((tc, D), lambda i, k: (k, 0)), pl.BlockSpec((tc, D), lambda i, k: (k + nj, 0)),
                  pl.BlockSpec((kw, tc), lambda i, k: (0, k)), pl.BlockSpec((kw, tc), lambda i, k: (0, k + nj)),
                  pl.BlockSpec((tc, D), lambda i, k: (k, 0))],
        out_specs=(pl.BlockSpec((tl, D), lambda i, k: (i, 0)), pl.BlockSpec((tl, D), lambda i, k: (i, 0)),
                   pl.BlockSpec((tl, tc), lambda i, k: (i, k)),
                   pl.BlockSpec((2, tl, tc), lambda i, k: (0, i, k)), pl.BlockSpec((2, tl, tc), lambda i, k: (0, i, k))),
        out_shape=(jax.ShapeDtypeStruct((L, D), F32), jax.ShapeDtypeStruct((L, D), BF16),
                   jax.ShapeDtypeStruct((L, f), BF16),
                   jax.ShapeDtypeStruct((2, L, f), BF16), jax.ShapeDtypeStruct((2, L, f), BF16)),
        scratch_shapes=[pltpu.VMEM((pad, D), BF16), pltpu.VMEM((2, FFN_PAD, tc), F32), pltpu.VMEM((2, tl, tc), F32),
                        pltpu.VMEM((tl, D), F32)],
        sem=("parallel", "arbitrary"), args=(h_mid, h_mid, g, wup_t, wup_t, kf, kf, wdown), xchg=xchg)
    return out if xchg is None else (out, xo)


def _ffn_block_bwd(dh, h_mid, g, ux, uc, kf, wdown, wup_t, name, xchg=None):
    L, D = dh.shape
    f = ux.shape[2]
    kw = kf.shape[0]
    tl = _token_tile(L)
    tc = _divisor(f, 256, 128)
    nj = f // tc
    nt = L // tl
    pad = 2 * SUBLANES
    rc = CONV3_ROWS
    nc = tl // rc
    parts = _row_parts(nc)

    def body(dhc_ref, dhn_ref, hm_ref, g_ref, xg_ref, xv_ref, cg_ref, cgn_ref, cv_ref, cvn_ref, kg_ref, kv_ref,
             wd_ref, wg_ref, wv_ref, dhm_ref, dg_ref, du_ref, dk_ref, dh_ext, dact_s, acc):
        i = pl.program_id(0)
        kb = pl.program_id(1)

        @pl.when(kb == 0)
        def _():
            dh_ext[0:tl, :] = dhc_ref[...].astype(BF16)
            dh_ext[tl:tl + pad, :] = dhn_ref[...].astype(BF16)
            acc[...] = jnp.zeros_like(acc)

        @pl.when(jnp.logical_and(i == 0, kb == 0))
        def _():
            dg_ref[...] = jnp.zeros_like(dg_ref)
            dk_ref[...] = jnp.zeros_like(dk_ref)

        x_refs, c_refs, nxt = (xg_ref, xv_ref), (cg_ref, cv_ref), (cgn_ref, cvn_ref)
        taps = (_tap_rows(kg_ref), _tap_rows(kv_ref))
        dk = [[jnp.zeros((SUBLANES, tc), F32) for _ in range(kw)] for _ in range(2)]

        def dact_part(lo, hi):
            a, b = lo * rc, hi * rc + pad
            dact_s[a:b, :] = _dot_nt(dh_ext[a:b, :], wd_ref[...])

        def dhn_part(lo, hi):
            a, b = lo * rc, hi * rc
            acc[a:b, :] += (jnp.dot(du_ref[0, a:b, :], wg_ref[...], preferred_element_type=F32)
                            + jnp.dot(du_ref[1, a:b, :], wv_ref[...], preferred_element_type=F32))

        for p, (lo, hi) in enumerate(parts):
            if p == 0:
                dact_part(lo, hi)
            if p + 1 < len(parts):
                dact_part(*parts[p + 1])
            if p > 0:
                dhn_part(*parts[p - 1])
            chunk_rows(lo, hi, x_refs, c_refs, nxt, taps, dk, i, dact_s, du_ref)
        dhn_part(*parts[-1])
        for h in range(2):
            for j in range(kw):
                dk_ref[kb, h, j:j + 1, :] += jnp.sum(dk[h][j], axis=0, keepdims=True)

        @pl.when(kb == nj - 1)
        def _():
            x = hm_ref[...]
            r = lax.rsqrt(jnp.mean(x * x, axis=-1, keepdims=True) + EPS)
            xhat = x * r
            dhn = acc[...]
            dxhat = dhn * g_ref[...]
            dhm_ref[...] = dhc_ref[...] + r * (dxhat - xhat * jnp.mean(dxhat * xhat, axis=-1, keepdims=True))
            dg_ref[...] += jnp.sum(_rowsum8(dhn * xhat), axis=0, keepdims=True)

    def chunk_rows(lo, hi, x_refs, c_refs, nxt, taps, dk, i, dact_s, du_ref):
        for c in range(lo, hi):
            r0 = c * rc
            n = rc + FFN_PAD
            convd = []
            for h in range(2):
                if c == nc - 1:
                    rows = jnp.concatenate([c_refs[h][r0:r0 + rc, :], nxt[h][...]], axis=0)
                else:
                    rows = c_refs[h][r0:r0 + rc + pad, :]
                convd.append(rows.astype(F32)[0:n])
            gate, val = convd
            xs = [x_refs[h][r0:r0 + rc, :].astype(F32) for h in range(2)]
            dact = dact_s[r0:r0 + n, :]
            sg = _sigmoid(gate)
            dcs = [dact * val * (sg * (1.0 + gate * (1.0 - sg))), dact * (gate * sg)]
            if c == nc - 1:
                live = jnp.logical_or(lax.broadcasted_iota(jnp.int32, (n, 1), 0) < rc, i < nt - 1)
                dcs = [jnp.where(live, d, 0.0) for d in dcs]
            for h in range(2):
                xc = xs[h]
                dx = None
                for j in range(kw):
                    o = kw - 1 - j
                    sh = dcs[h][o:o + rc]
                    term = _rows_of(taps[h][j], rc) * sh
                    dx = term if dx is None else dx + term
                    dk[h][j] = dk[h][j] + _rowsum8(xc * sh)
                du_ref[h, r0:r0 + rc, :] = dx.astype(BF16)

    def after(i):
        return jnp.minimum((i + 1) * (tl // pad), L // pad - 1)

    def half(h, rows, idx):
        return pl.BlockSpec((None, rows, tc), lambda i, k: (h,) + idx(i, k))

    def tile(i, k):
        return (i, k)

    def behind(i, k):
        return (after(i), k)

    out, xo = _call(
        body, name=name, grid=(nt, nj),
        in_specs=[pl.BlockSpec((tl, D), lambda i, k: (i, 0)),
                  pl.BlockSpec((pad, D), lambda i, k: (after(i), 0)),
                  pl.BlockSpec((tl, D), lambda i, k: (i, 0)), pl.BlockSpec((1, D), lambda i, k: (0, 0)),
                  half(0, tl, tile), half(1, tl, tile),
                  half(0, tl, tile), half(0, pad, behind), half(1, tl, tile), half(1, pad, behind),
                  pl.BlockSpec((kw, tc), lambda i, k: (0, k)), pl.BlockSpec((kw, tc), lambda i, k: (0, k + nj)),
                  pl.BlockSpec((tc, D), lambda i, k: (k, 0)),
                  pl.BlockSpec((tc, D), lambda i, k: (k, 0)), pl.BlockSpec((tc, D), lambda i, k: (k + nj, 0))],
        out_specs=(pl.BlockSpec((tl, D), lambda i, k: (i, 0)), pl.BlockSpec((1, D), lambda i, k: (0, 0)),
                   pl.BlockSpec((2, tl, tc), lambda i, k: (0, i, k)),
                   pl.BlockSpec((nj, 2, kw, tc), lambda i, k: (0, 0, 0, 0))),
        out_shape=(jax.ShapeDtypeStruct((L, D), F32), jax.ShapeDtypeStruct((1, D), F32),
                   jax.ShapeDtypeStruct((2, L, f), BF16), jax.ShapeDtypeStruct((nj, 2, kw, tc), F32)),
        scratch_shapes=[pltpu.VMEM((tl + pad, D), BF16), pltpu.VMEM((tl + pad, tc), F32), pltpu.VMEM((tl, D), F32)],
        sem=("arbitrary", "arbitrary"), args=(dh, dh, h_mid, g, ux, ux, uc, uc, uc, uc, kf, kf, wdown, wup_t, wup_t),
        xchg=xchg)
    return out if xchg is None else (out, xo)


def _adamw_math(w, g, m, v):
    m = ADAM_B1 * m + (1.0 - ADAM_B1) * g
    v = ADAM_B2 * v + (1.0 - ADAM_B2) * (g * g)
    m_hat = m / (1.0 - ADAM_B1 ** ADAM_STEP)
    v_hat = v / (1.0 - ADAM_B2 ** ADAM_STEP)
    delta = -ADAM_LR * (m_hat / (jnp.sqrt(v_hat) + ADAM_EPS) + ADAM_WD * w)
    return delta, m, v


def _sum_parts(parts_ref, idx):
    g = parts_ref[(0,) + idx].astype(F32)
    for q in range(1, N_DEV):
        g = g + parts_ref[(q,) + idx].astype(F32)
    return g


def _adamw_big(parts, w, m, v, name):
    nl, R, C = w.shape
    tr = _divisor(R, 256, 2 * SUBLANES)

    def body(*refs):
        p_refs = refs[:nl]
        w_ref, m_ref, v_ref, g_ref, d_ref, nm_ref, nv_ref = refs[nl:]
        layer = pl.program_id(0)
        for k in range(nl):
            @pl.when(layer == k)
            def _(k=k):
                g = _sum_parts(p_refs[k], ())
                d, nm, nv = _adamw_math(w_ref[0], g, m_ref[0], v_ref[0])
                g_ref[0] = g
                d_ref[0] = d
                nm_ref[0] = nm
                nv_ref[0] = nv

    def part_spec(k):
        return pl.BlockSpec((N_DEV, tr, C), lambda l, r: (0, jnp.where(l == k, r, 0), 0))

    blk = pl.BlockSpec((1, tr, C), lambda l, r: (l, r, 0))
    shp = jax.ShapeDtypeStruct((nl, R, C), F32)
    return pl.pallas_call(
        body, name=name, grid=(nl, R // tr),
        in_specs=[part_spec(k) for k in range(nl)] + [blk, blk, blk],
        out_specs=(blk, blk, blk, blk), out_shape=(shp, shp, shp, shp),
        compiler_params=_params(("arbitrary", "arbitrary")),
    )(*parts, w, m, v)


def _adamw_small(entries, name):
    n = len(entries)
    uniq = []
    for e in entries:
        if not any(e[0] is u for u in uniq):
            uniq.append(e[0])
    pidx = [next(k for k, u in enumerate(uniq) if u is e[0]) for e in entries]
    npart = len(uniq)

    def body(*refs):
        p_refs = refs[:npart]
        wmv = refs[npart:npart + 3 * n]
        outs = refs[npart + 3 * n:]
        for t, e in enumerate(entries):
            lo, w = e[1], e[2]
            rows = w.shape[0]
            pr = p_refs[pidx[t]]
            g = pr[0, lo:lo + rows].astype(F32)
            for q in range(1, N_DEV):
                g = g + pr[q, lo:lo + rows].astype(F32)
            d, nm, nv = _adamw_math(wmv[3 * t][...], g, wmv[3 * t + 1][...], wmv[3 * t + 2][...])
            outs[4 * t][...] = g
            outs[4 * t + 1][...] = d
            outs[4 * t + 2][...] = nm
            outs[4 * t + 3][...] = nv

    vm = pl.BlockSpec(memory_space=pltpu.VMEM)
    args = list(uniq)
    out_shape = []
    for e in entries:
        args += [e[2], e[3], e[4]]
        out_shape += [jax.ShapeDtypeStruct(e[2].shape, F32)] * 4
    res = pl.pallas_call(
        body, name=name, in_specs=[vm] * len(args), out_specs=tuple([vm] * len(out_shape)),
        out_shape=tuple(out_shape), compiler_params=_params(),
    )(*args)
    return [tuple(res[4 * t:4 * t + 4]) for t in range(n)]


def _head_matrix(cc):
    bw = min(256, cc)
    r = lax.broadcasted_iota(jnp.int32, (bw, bw), 0) // HEAD_DIM
    c = lax.broadcasted_iota(jnp.int32, (bw, bw), 1) // HEAD_DIM
    return jnp.where(r == c, 1.0 / HEAD_DIM, 0.0).astype(BF16)


def _cols_from_shards(g):
    nd = g.ndim
    perm = tuple(range(1, nd - 1)) + (0, nd - 1)
    t = jnp.transpose(g, perm)
    return t.reshape(t.shape[:-2] + (t.shape[-2] * t.shape[-1],))


def _cols_to_shards(a):
    nd = a.ndim
    t = a.reshape(a.shape[:-1] + (N_DEV, a.shape[-1] // N_DEV))
    perm = (nd - 1,) + tuple(range(nd - 1)) + (nd,)
    return jnp.transpose(t, perm)


def kernel(x, meta_tokens, norm1_g, w_in, conv_dw_k, conv_dw_b, conv_ln_g, conv_ln_b, pool_w, pool_scale, w_out, norm2_g, w_up, ffn_dw_k, w_down, final_g, loss_target, m_meta_tokens, m_norm1_g, m_w_in, m_conv_dw_k, m_conv_dw_b, m_conv_ln_g, m_conv_ln_b, m_pool_w, m_pool_scale, m_w_out, m_norm2_g, m_w_up, m_ffn_dw_k, m_w_down, m_final_g, v_meta_tokens, v_norm1_g, v_w_in, v_conv_dw_k, v_conv_dw_b, v_conv_ln_g, v_conv_ln_b, v_pool_w, v_pool_scale, v_w_out, v_norm2_g, v_w_up, v_ffn_dw_k, v_w_down, v_final_g):
    depth, D = norm1_g.shape
    n_meta = meta_tokens.shape[0]
    seq = x.shape[1]
    L = n_meta + seq
    cc = conv_dw_b.shape[1]
    ng, gd = pool_w.shape[1], pool_w.shape[2]
    f = w_down.shape[1] * N_DEV

    def rows(g):
        return g.reshape(-1, g.shape[-1])

    b16 = lambda a: a.astype(BF16)
    tr = lambda a: jnp.swapaxes(a, -1, -2)
    w_in_t, m_w_in_t, v_w_in_t = tr(w_in), tr(m_w_in), tr(v_w_in)
    w_up_t, m_w_up_t, v_w_up_t = tr(w_up), tr(m_w_up), tr(v_w_up)
    (g_in0, g_ck, g_kf, g_meta) = _exchange([b16(w_in_t[0]), conv_dw_k, ffn_dw_k, meta_tokens], ["gather"] * 4,
                                            "gather_first")
    ck_full = _cols_from_shards(g_ck)
    ck_rows = jnp.broadcast_to(ck_full[:, :, None, :], ck_full.shape[:2] + (SUBLANES, cc))
    kf_full = _cols_from_shards(g_kf)
    meta_full = _cols_from_shards(g_meta)
    am = _head_matrix(cc)
    win, wout, wup, wdown = [None] * depth, [None] * depth, [None] * depth, [None] * depth
    win[0] = rows(g_in0)

    h = (meta_full, x[0])
    saved = []
    for l in range(depth):
        more = l + 1 < depth
        if l == 0:
            (z, hn1), (g_out,) = _norm_proj(h, norm1_g[l:l + 1], win[l], f"in_proj_{l}", tn_cap=1536,
                                            xchg=([b16(w_out[l])], ["gather"]))
            wout[l] = rows(g_out)
            (ymix, u1), (g_up, g_down) = _mixer_fwd(z, ck_rows[l], conv_dw_b[l:l + 1], conv_ln_g[l:l + 1], conv_ln_b[l:l + 1],
                                                    pool_w[l], pool_scale[l:l + 1], am, f"mixer_fwd_{l}",
                                                    xchg=([b16(w_up_t[l]), b16(w_down[l])], ["gather"] * 2))
            wup[l], wdown[l] = rows(g_up), rows(g_down)
        else:
            z, hn1 = _norm_proj(h, norm1_g[l:l + 1], win[l], f"in_proj_{l}", tn_cap=1536)
            ymix, u1 = _mixer_fwd(z, ck_rows[l], conv_dw_b[l:l + 1], conv_ln_g[l:l + 1], conv_ln_b[l:l + 1], pool_w[l],
                                  pool_scale[l:l + 1], am, f"mixer_fwd_{l}")
        if more:
            h_mid, (g_in, g_out) = _mm(ymix, wout[l], f"out_proj_{l}", res=h, tn_cap=1024,
                                       xchg=([b16(w_in_t[l + 1]), b16(w_out[l + 1])], ["gather"] * 2))
            win[l + 1], wout[l + 1] = rows(g_in), rows(g_out)
            nxt = [b16(w_up_t[l + 1]), b16(w_down[l + 1])]
            (h_out, hn2, act, ux, uc), got = _ffn_block_fwd(h_mid, norm2_g[l:l + 1], wup[l], kf_full[l], wdown[l],
                                                            f"ffn_fwd_{l}", xchg=(nxt, ["gather"] * 2))
            wup[l + 1], wdown[l + 1] = rows(got[0]), rows(got[1])
        else:
            h_mid = _mm(ymix, wout[l], f"out_proj_{l}", res=h, tn_cap=1024)
            h_out, hn2, act, ux, uc = _ffn_block_fwd(h_mid, norm2_g[l:l + 1], wup[l], kf_full[l], wdown[l], f"ffn_fwd_{l}")
        saved.append((h, hn1, z, u1, ymix, h_mid, hn2, ux, uc, act))
        h = h_out

    dh, d_final_g, loss_part = _loss_head(h, final_g.reshape(1, D), loss_target[0], n_meta, "loss_head")

    def row_shards(gm):
        return gm.reshape(N_DEV, -1, gm.shape[-1])

    zero_row = jnp.zeros((1, D), F32)
    gw = {k: [None] * depth for k in ("ck", "cb", "lg", "lb", "pw", "ps", "kf", "n1", "n2")}
    parts = {k: [None] * depth for k in ("in", "out", "up", "down")}
    for l in reversed(range(depth)):
        h_in, hn1, z, u1, ymix, h_mid, hn2, ux, uc, act = saved[l]
        g_down = _mm_tn(act, dh, f"down_proj_wgrad_{l}", tq_cap=512)
        (dh_mid, gw["n2"][l], dug0, dkf), (parts["down"][l],) = _ffn_block_bwd(
            dh, h_mid, norm2_g[l:l + 1], ux, uc, kf_full[l], wdown[l], wup[l], f"ffn_bwd_{l}",
            xchg=([row_shards(g_down)], ["a2a"]))
        gw["kf"][l] = jnp.transpose(dkf, (2, 1, 0, 3)).reshape(dkf.shape[2], -1)
        g_up_t = _mm_tn(dug0, hn2, f"up_proj_wgrad_{l}", halves=2, tq_cap=1024)
        dymix = _mm(dh_mid, wout[l], f"out_proj_bwd_{l}", b_t=True, out_dtype=BF16, tn_cap=1024)
        g_out = _mm_tn(ymix, dh_mid, f"out_proj_wgrad_{l}", tq_cap=512)
        ((dz, gw["ck"][l], gw["cb"][l], gw["lg"][l], gw["lb"][l], gw["pw"][l], gw["ps"][l]),
         (parts["up"][l], parts["out"][l])) = _mixer_bwd(
            z, u1, dymix, ck_rows[l], conv_ln_g[l:l + 1], conv_ln_b[l:l + 1], pool_w[l], pool_scale[l:l + 1], am,
            f"mixer_bwd_{l}", xchg=([row_shards(g_up_t), row_shards(g_out)], ["a2a", "a2a"]))
        g_in_t = _mm_tn(dz, hn1, f"in_proj_wgrad_{l}", tq_cap=1024)
        if l > 0:
            (dh, gw["n1"][l]), (parts["in"][l],) = _proj_bwd_norm(dz, win[l], h_in, norm1_g[l:l + 1], dh_mid, zero_row,
                                                                  f"in_proj_bwd_{l}", xchg=([row_shards(g_in_t)], ["a2a"]))
        else:
            meta_rows, x_rows = h_in
            (grad_x, dg_x), (parts["in"][l],) = _proj_bwd_norm(dz, win[l], x_rows, norm1_g[l:l + 1], dh_mid, zero_row,
                                                               f"in_proj_bwd_{l}", skip=n_meta,
                                                               xchg=([row_shards(g_in_t)], ["a2a"]))
            d_meta, gw["n1"][l] = _proj_bwd_norm(dz[:n_meta], win[l], meta_rows, norm1_g[l:l + 1], dh_mid[:n_meta],
                                                 dg_x, f"in_proj_bwd_meta_{l}")
    grad_x = grad_x[None]

    pack_d = jnp.concatenate(gw["n1"] + gw["n2"] + [d_final_g, jnp.broadcast_to(loss_part[:, :1], (1, D)), zero_row, zero_row], axis=0)
    pack_c = jnp.concatenate(gw["cb"] + gw["lg"] + gw["lb"] + gw["ps"], axis=0)
    pack_pw = b16(jnp.stack(gw["pw"]).reshape(depth * ng * gd, gd))
    src = [_cols_to_shards(jnp.stack(gw["ck"])), _cols_to_shards(jnp.stack(gw["kf"])), _cols_to_shards(d_meta),
           pack_d, pack_c, pack_pw]
    r_ck, r_kf, r_meta, r_d, r_c, r_pw = _exchange(src, ["a2a"] * 3 + ["gather"] * 3, "exchange_small_grads")

    big = {
        "w_in": tuple(tr(a) for a in _adamw_big(parts["in"], w_in_t, m_w_in_t, v_w_in_t, "adamw_w_in")),
        "w_out": _adamw_big(parts["out"], w_out, m_w_out, v_w_out, "adamw_w_out"),
        "w_up": tuple(tr(a) for a in _adamw_big(parts["up"], w_up_t, m_w_up_t, v_w_up_t, "adamw_w_up")),
        "w_down": _adamw_big(parts["down"], w_down, m_w_down, v_w_down, "adamw_w_down"),
    }
    kwid = conv_dw_k.shape[1]
    fkw = ffn_dw_k.shape[1]
    row = lambda a: a.reshape(1, -1)
    entries = [
        (r_d, 0, norm1_g, m_norm1_g, v_norm1_g),
        (r_d, depth, norm2_g, m_norm2_g, v_norm2_g),
        (r_d, 2 * depth, row(final_g), row(m_final_g), row(v_final_g)),
        (r_c, 0, conv_dw_b, m_conv_dw_b, v_conv_dw_b),
        (r_c, depth, conv_ln_g, m_conv_ln_g, v_conv_ln_g),
        (r_c, 2 * depth, conv_ln_b, m_conv_ln_b, v_conv_ln_b),
        (r_c, 3 * depth, pool_scale, m_pool_scale, v_pool_scale),
        (r_pw, 0, pool_w.reshape(-1, gd), m_pool_w.reshape(-1, gd), v_pool_w.reshape(-1, gd)),
        (r_ck.reshape(N_DEV, depth * kwid, -1), 0, conv_dw_k.reshape(depth * kwid, -1),
         m_conv_dw_k.reshape(depth * kwid, -1), v_conv_dw_k.reshape(depth * kwid, -1)),
        (r_kf.reshape(N_DEV, depth * fkw, -1), 0, ffn_dw_k.reshape(depth * fkw, -1),
         m_ffn_dw_k.reshape(depth * fkw, -1), v_ffn_dw_k.reshape(depth * fkw, -1)),
        (r_meta, 0, meta_tokens, m_meta_tokens, v_meta_tokens),
        (r_d, 2 * depth + 1, zero_row, zero_row, zero_row),
    ]
    small = _adamw_small(entries, "adamw_small")
    names = ["norm1_g", "norm2_g", "final_g", "conv_dw_b", "conv_ln_g", "conv_ln_b", "pool_scale", "pool_w",
             "conv_dw_k", "ffn_dw_k", "meta_tokens"]
    shapes = {"final_g": final_g.shape, "pool_w": pool_w.shape, "conv_dw_k": conv_dw_k.shape, "ffn_dw_k": ffn_dw_k.shape}
    res = dict(big)
    for nme, quad in zip(names, small[:-1]):
        res[nme] = tuple(a.reshape(shapes[nme]) if nme in shapes else a for a in quad)
    loss = small[-1][0][0, 0]

    order = ["meta_tokens", "norm1_g", "w_in", "conv_dw_k", "conv_dw_b", "conv_ln_g", "conv_ln_b", "pool_w", "pool_scale",
             "w_out", "norm2_g", "w_up", "ffn_dw_k", "w_down", "final_g"]
    return (loss, grad_x, *[res[k][0] for k in order], *[res[k][1] for k in order], *[res[k][2] for k in order],
            *[res[k][3] for k in order])
```

```python
import functools

import jax
import jax.numpy as jnp
from jax import lax
from jax.experimental import pallas as pl
from jax.experimental.pallas import tpu as pltpu

F32 = jnp.float32
BF16 = jnp.bfloat16

EPS = 1e-6
HEAD_DIM = 64
POOL_WINDOWS = (2, 4, 8, 16)
ADAM_LR = 0.001
ADAM_B1 = 0.9
ADAM_B2 = 0.999
ADAM_EPS = 1e-08
ADAM_WD = 0.01
ADAM_STEP = 10

N_DEV = 8
OTHER_CHIPS = (2, 4, 6)
SUBLANES = 8
HALO = 48
CONV_PAD = 32
POOL_PAD = 16
FFN_PAD = 8
ROW_CHUNK = 24
CONV3_ROWS = 48
MAX_TILE_ROWS = 1024
WGRAD_TILE_ROWS = 2816
VMEM_LIMIT = 52 * 1024 * 1024


def _divisor(n, cap, mult):
    best = None
    for d in range(mult, min(n, cap) + 1, mult):
        if n % d == 0:
            best = d
    return n if best is None else best


def _token_tile(L):
    return _divisor(L, MAX_TILE_ROWS, HALO)


def _stat_rows(tl):
    return _divisor(tl, 512, SUBLANES)


def _params(sem=None):
    return pltpu.CompilerParams(dimension_semantics=sem, vmem_limit_bytes=VMEM_LIMIT)


def _rowsum8(x):
    acc = x[0:SUBLANES]
    for k in range(1, x.shape[0] // SUBLANES):
        acc = acc + x[k * SUBLANES:(k + 1) * SUBLANES]
    return acc


def _sigmoid(x):
    return jax.nn.sigmoid(x)


def _dot_nt(a, b):
    return lax.dot_general(a, b, (((1,), (1,)), ((), ())), preferred_element_type=F32)


def _head_mean(x, am_ref):
    bw = am_ref.shape[0]
    am = am_ref[...]
    outs = []
    for blk in range(x.shape[1] // bw):
        xb = x[:, blk * bw:(blk + 1) * bw]
        hi = xb.astype(BF16)
        lo = (xb - hi.astype(F32)).astype(BF16)
        outs.append(jnp.dot(hi, am, preferred_element_type=F32) + jnp.dot(lo, am, preferred_element_type=F32))
    return outs[0] if len(outs) == 1 else jnp.concatenate(outs, axis=-1)


def _xchg_out_shapes(srcs, modes):
    out = []
    for s, m in zip(srcs, modes):
        shp = ((N_DEV,) + tuple(s.shape)) if m == "gather" else tuple(s.shape)
        out.append(jax.ShapeDtypeStruct(shp, s.dtype))
    return out


def _xchg_sems(n):
    return [pltpu.SemaphoreType.DMA((n, N_DEV - 1)), pltpu.SemaphoreType.DMA((n, N_DEV - 1)), pltpu.SemaphoreType.DMA((n,))]


def _xchg_ops(src_refs, out_refs, sems, modes):
    n = len(src_refs)
    send_sems, recv_sems, local_sems = sems
    x, y, c = lax.axis_index("x"), lax.axis_index("y"), lax.axis_index("c")
    me = 4 * x + 2 * y + c

    def peer(d):
        return (x ^ ((d >> 2) & 1), y ^ ((d >> 1) & 1), c ^ (d & 1))

    def peer_id(d):
        px, py, pc = peer(d)
        return 4 * px + 2 * py + pc

    def remote(t, d):
        src = src_refs[t] if modes[t] == "gather" else src_refs[t].at[peer_id(d)]
        return pltpu.make_async_remote_copy(
            src_ref=src, dst_ref=out_refs[t].at[me], send_sem=send_sems.at[t, d - 1], recv_sem=recv_sems.at[t, d - 1],
            device_id=peer(d), device_id_type=pl.DeviceIdType.MESH)

    def arrival(t, d):
        src = src_refs[t] if modes[t] == "gather" else src_refs[t].at[me]
        return pltpu.make_async_remote_copy(
            src_ref=src, dst_ref=out_refs[t].at[peer_id(d)], send_sem=send_sems.at[t, d - 1],
            recv_sem=recv_sems.at[t, d - 1], device_id=peer(d), device_id_type=pl.DeviceIdType.MESH)

    def passed_on(t, d):
        blk = out_refs[t].at[peer_id(d)]
        return pltpu.make_async_remote_copy(
            src_ref=blk, dst_ref=blk, send_sem=send_sems.at[t, d], recv_sem=recv_sems.at[t, d],
            device_id=peer(1), device_id_type=pl.DeviceIdType.MESH)

    def local(t):
        src = src_refs[t] if modes[t] == "gather" else src_refs[t].at[me]
        return pltpu.make_async_copy(src, out_refs[t].at[me], local_sems.at[t])

    def sent_first(t):
        return OTHER_CHIPS + (1,) if modes[t] == "gather" else tuple(range(1, N_DEV))

    def start():
        for t in range(n):
            local(t).start()
        for t in range(n):
            for d in sent_first(t):
                remote(t, d).start()

    gathered = [t for t in range(n) if modes[t] == "gather"]

    def relay():
        for t in gathered:
            for d in OTHER_CHIPS:
                arrival(t, d).wait_recv()
                passed_on(t, d).start()

    def wait():
        for t in range(n):
            for d in range(1, N_DEV):
                if not (modes[t] == "gather" and d in OTHER_CHIPS):
                    arrival(t, d).wait_recv()
        for t in range(n):
            for d in sent_first(t):
                remote(t, d).wait_send()
        for t in gathered:
            for d in OTHER_CHIPS:
                passed_on(t, d).wait_send()
        for t in range(n):
            local(t).wait()

    return start, relay, wait


def _exchange(srcs, modes, name):
    n = len(srcs)

    def body(*refs):
        start, relay, wait = _xchg_ops(refs[:n], refs[n:2 * n], refs[2 * n:], modes)
        start()
        relay()
        wait()

    any_spec = pl.BlockSpec(memory_space=pl.ANY)
    return pl.pallas_call(
        body, name=name, out_shape=tuple(_xchg_out_shapes(srcs, modes)),
        in_specs=[any_spec] * n, out_specs=tuple([any_spec] * n),
        scratch_shapes=_xchg_sems(n),
        compiler_params=pltpu.CompilerParams(has_side_effects=True),
    )(*srcs)


def _call(body, *, name, grid, in_specs, out_specs, out_shape, args, scratch_shapes=(), sem=None, xchg=None):
    single = not isinstance(out_shape, (tuple, list))
    outs_shape = [out_shape] if single else list(out_shape)
    outs_spec = [out_specs] if single else list(out_specs)
    if xchg is None:
        res = pl.pallas_call(
            body, name=name, grid=grid, in_specs=list(in_specs), out_specs=out_specs, out_shape=out_shape,
            scratch_shapes=list(scratch_shapes), compiler_params=_params(sem))(*args)
        return res, ()
    srcs, modes = xchg
    n_in, n_out, n_scr, nx = len(in_specs), len(outs_shape), len(scratch_shapes), len(srcs)
    n_steps = functools.reduce(lambda a, b: a * b, grid, 1)
    relay_step = max(n_steps - 2, 0) if n_steps <= 16 else (3 * n_steps) // 4

    def wrapped(*refs):
        ins = refs[:n_in]
        xs = refs[n_in:n_in + nx]
        o0 = n_in + nx
        outs = refs[o0:o0 + n_out]
        xo = refs[o0 + n_out:o0 + n_out + nx]
        s0 = o0 + n_out + nx
        scr = refs[s0:s0 + n_scr]
        start, relay, wait = _xchg_ops(xs, xo, refs[s0 + n_scr:], modes)
        step = functools.reduce(lambda acc, a: acc * grid[a] + pl.program_id(a), range(len(grid)), 0)

        @pl.when(step == 0)
        def _():
            start()

        body(*ins, *outs, *scr)

        @pl.when(step == relay_step)
        def _():
            relay()

        @pl.when(step == n_steps - 1)
        def _():
            wait()

    any_spec = pl.BlockSpec(memory_space=pl.ANY)
    res = pl.pallas_call(
        wrapped, name=name, grid=grid, in_specs=list(in_specs) + [any_spec] * nx,
        out_specs=tuple(outs_spec + [any_spec] * nx), out_shape=tuple(outs_shape + _xchg_out_shapes(srcs, modes)),
        scratch_shapes=list(scratch_shapes) + _xchg_sems(nx),
        compiler_params=_params(("arbitrary",) * len(grid)))(*args, *srcs)
    comp = res[:n_out]
    return (comp[0] if single else tuple(comp)), tuple(res[n_out:])


def _seq_rows(h):
    if isinstance(h, tuple):
        return h[0].shape[0] + h[1].shape[0], h[1].shape[1]
    return h.shape


def _seq_tiles(h, tm):
    if not isinstance(h, tuple):
        return [pl.BlockSpec((tm, h.shape[1]), lambda i, *_: (i, 0))], [h], lambda refs, i: refs[0][...]
    meta, x = h
    n, D = meta.shape

    def read(refs, i):
        t = refs[1][...]
        first = jnp.concatenate([refs[0][...], pltpu.roll(t, n, axis=0)[n:]], axis=0)
        return jnp.where(i == 0, first, t)

    window = pl.BlockSpec((pl.Element(tm), pl.Element(D)),
                          lambda i, *_: (pl.multiple_of(jnp.maximum(i * tm - n, 0), SUBLANES), 0))
    return [pl.BlockSpec((n, D), lambda *_: (0, 0)), window], [meta, x], read


def _norm_proj(h, g, w, name, *, tn_cap, xchg=None):
    L, D = _seq_rows(h)
    N = w.shape[0]
    tm = _token_tile(L)
    tn = _divisor(N, tn_cap, 128)
    h_specs, h_args, read_h = _seq_tiles(h, tm)
    nh = len(h_specs)

    def body(*refs):
        g_ref, w_ref, z_ref, hn_ref = refs[nh:]

        @pl.when(pl.program_id(1) == 0)
        def _():
            x = read_h(refs[:nh], pl.program_id(0))
            r = lax.rsqrt(jnp.mean(x * x, axis=-1, keepdims=True) + EPS)
            hn_ref[...] = ((x * r) * g_ref[...]).astype(BF16)

        z_ref[...] = _dot_nt(hn_ref[...], w_ref[...])

    out, xo = _call(
        body, name=name, grid=(L // tm, N // tn),
        in_specs=h_specs + [pl.BlockSpec((1, D), lambda i, j: (0, 0)), pl.BlockSpec((tn, D), lambda i, j: (j, 0))],
        out_specs=(pl.BlockSpec((tm, tn), lambda i, j: (i, j)), pl.BlockSpec((tm, D), lambda i, j: (i, 0))),
        out_shape=(jax.ShapeDtypeStruct((L, N), F32), jax.ShapeDtypeStruct((L, D), BF16)),
        sem=("parallel", "arbitrary"), args=(*h_args, g, w), xchg=xchg)
    return out if xchg is None else (out, xo)


def _proj_bwd_norm(a, b, h, g, dres, dg0, name, skip=0, xchg=None):
    L, K = a.shape
    D = b.shape[1]
    rows = L - skip
    tm = _divisor(rows, MAX_TILE_ROWS, 2 * SUBLANES) if skip else _token_tile(L)

    def body(a_ref, b_ref, h_ref, g_ref, dres_ref, dg0_ref, dh_ref, dg_ref):
        i = pl.program_id(0)
        dhn = jnp.dot(a_ref[...], b_ref[...], preferred_element_type=F32)
        x = h_ref[...]
        r = lax.rsqrt(jnp.mean(x * x, axis=-1, keepdims=True) + EPS)
        xhat = x * r
        dxhat = dhn * g_ref[...]
        dh_ref[...] = dres_ref[...] + r * (dxhat - xhat * jnp.mean(dxhat * xhat, axis=-1, keepdims=True))
        part = jnp.sum(_rowsum8(dhn * xhat), axis=0, keepdims=True)

        @pl.when(i == 0)
        def _():
            dg_ref[...] = dg0_ref[...] + part

        @pl.when(i > 0)
        def _():
            dg_ref[...] += part

    def rows_of(cols, first=skip):
        if not first:
            return pl.BlockSpec((tm, cols), lambda i: (i, 0))
        return pl.BlockSpec((pl.Element(tm), pl.Element(cols)), lambda i: (pl.multiple_of(first + i * tm, SUBLANES), 0))

    row = pl.BlockSpec((1, D), lambda i: (0, 0))
    h_first = skip if h.shape[0] == L else 0
    out, xo = _call(
        body, name=name, grid=(rows // tm,),
        in_specs=[rows_of(K), pl.BlockSpec((K, D), lambda i: (0, 0)), rows_of(D, h_first), row, rows_of(D), row],
        out_specs=(pl.BlockSpec((tm, D), lambda i: (i, 0)), row),
        out_shape=(jax.ShapeDtypeStruct((rows, D), F32), jax.ShapeDtypeStruct((1, D), F32)),
        sem=("arbitrary",), args=(a, b, h, g, dres, dg0), xchg=xchg)
    return out if xchg is None else (out, xo)


def _mm(a, b, name, *, res=None, b_t=False, out_dtype=F32, tn_cap=1408, xchg=None):
    M, K = a.shape
    N = b.shape[0] if b_t else b.shape[1]
    tm = _token_tile(M)
    tn = _divisor(N, tn_cap, 128)
    if isinstance(res, tuple):
        assert tn == N
        r_specs, r_args, read_r = _seq_tiles(res, tm)
    elif res is not None:
        r_specs, r_args, read_r = [pl.BlockSpec((tm, tn), lambda i, j: (i, j))], [res], lambda refs, i: refs[0][...]
    else:
        r_specs, r_args, read_r = [], [], None

    def body(*refs):
        a_ref, b_ref, o_ref = refs[0], refs[1], refs[-1]
        av = a_ref[...].astype(BF16)
        prod = _dot_nt(av, b_ref[...]) if b_t else jnp.dot(av, b_ref[...], preferred_element_type=F32)
        o_ref[...] = (prod if read_r is None else prod + read_r(refs[2:-1], pl.program_id(0))).astype(out_dtype)

    b_spec = pl.BlockSpec((tn, K), lambda i, j: (j, 0)) if b_t else pl.BlockSpec((K, tn), lambda i, j: (0, j))
    in_specs = [pl.BlockSpec((tm, K), lambda i, j: (i, 0)), b_spec] + r_specs
    args = [a, b] + r_args
    out, xo = _call(
        body, name=name, grid=(M // tm, N // tn), in_specs=in_specs,
        out_specs=pl.BlockSpec((tm, tn), lambda i, j: (i, j)), out_shape=jax.ShapeDtypeStruct((M, N), out_dtype),
        sem=("parallel", "parallel"), args=args, xchg=xchg)
    return out if xchg is None else (out, xo)


def _mm_tn(a, b, name, *, halves=1, tq_cap=1408):
    L, Q = b.shape
    ph = a.shape[-1]
    P = ph * halves
    tl = _divisor(L, WGRAD_TILE_ROWS, HALO)
    tp = _divisor(ph, 1408, 128)
    tq = _divisor(Q, tq_cap, 128)
    pper = ph // tp
    nl = L // tl
    grid = (P // tp, Q // tq, nl)

    def body(a_ref, b_ref, o_ref, acc):
        prod = lax.dot_general(a_ref[...].astype(BF16), b_ref[...].astype(BF16), (((0,), (0,)), ((), ())),
                               preferred_element_type=F32)
        l = pl.program_id(2)
        if nl == 1:
            o_ref[...] = prod.astype(BF16)
            return

        @pl.when(l == 0)
        def _():
            acc[...] = prod

        @pl.when(jnp.logical_and(l > 0, l < nl - 1))
        def _():
            acc[...] += prod

        @pl.when(l == nl - 1)
        def _():
            o_ref[...] = (acc[...] + prod).astype(BF16)

    if halves > 1:
        a_spec = pl.BlockSpec((None, tl, tp), lambda p, q, l: (p // pper, l, p % pper))
    else:
        a_spec = pl.BlockSpec((tl, tp), lambda p, q, l: (l, p))
    return pl.pallas_call(
        body, name=name, grid=grid,
        in_specs=[a_spec, pl.BlockSpec((tl, tq), lambda p, q, l: (l, q))],
        out_specs=pl.BlockSpec((tp, tq), lambda p, q, l: (p, q)),
        out_shape=jax.ShapeDtypeStruct((P, Q), BF16),
        scratch_shapes=[pltpu.VMEM((tp, tq), F32)],
        compiler_params=_params(("parallel", "parallel", "arbitrary")),
    )(a, b)


def _loss_head(h, g, tgt, n_meta, name):
    L, D = h.shape
    tl = _token_tile(L)
    nt = L // tl

    def body(h_ref, g_ref, t_ref, dh_ref, dg_ref, loss_ref):
        i = pl.program_id(0)
        x = h_ref[...]
        r = lax.rsqrt(jnp.mean(x * x, axis=-1, keepdims=True) + EPS)
        xhat = x * r
        gg = g_ref[...]
        y = xhat * gg
        rows = i * tl + lax.broadcasted_iota(jnp.int32, (tl, 1), 0)
        t = t_ref[...]
        t = jnp.where(i == 0, pltpu.roll(t, n_meta, axis=0), t)
        err = jnp.where(rows >= n_meta, y - t, 0.0)
        dy = err * (1.0 / D)
        dxhat = dy * gg
        dh_ref[...] = r * (dxhat - xhat * jnp.mean(dxhat * xhat, axis=-1, keepdims=True))
        dg_part = jnp.sum(_rowsum8(dy * xhat), axis=0, keepdims=True)
        per_row = jnp.mean(err * err, axis=-1, keepdims=True)
        loss_part = jnp.broadcast_to(0.5 * jnp.sum(per_row, axis=0, keepdims=True), (1, 128))

        @pl.when(i == 0)
        def _():
            dg_ref[...] = dg_part
            loss_ref[...] = loss_part

        @pl.when(i > 0)
        def _():
            dg_ref[...] += dg_part
            loss_ref[...] += loss_part

    tile = pl.BlockSpec((tl, D), lambda i: (i, 0))
    row = pl.BlockSpec((1, D), lambda i: (0, 0))
    window = pl.BlockSpec((pl.Element(tl), pl.Element(D)),
                          lambda i: (pl.multiple_of(jnp.maximum(i * tl - n_meta, 0), SUBLANES), 0))
    return pl.pallas_call(
        body, name=name, grid=(nt,), in_specs=[tile, row, window],
        out_specs=(tile, row, pl.BlockSpec((1, 128), lambda i: (0, 0))),
        out_shape=(jax.ShapeDtypeStruct((L, D), F32), jax.ShapeDtypeStruct((1, D), F32),
                   jax.ShapeDtypeStruct((1, 128), F32)),
        compiler_params=_params(("arbitrary",)),
    )(h, g, tgt)


def _pool_fwd_block(pwin, pw_ref, row0, rb, g, gd, w, t0):
    wv = pwin[pl.ds(row0 + HALO - POOL_PAD, rb + POOL_PAD), g * gd:(g + 1) * gd]
    s = wv
    sh = 1
    while sh < w:
        s = s + pltpu.roll(s, sh, axis=0)
        sh *= 2
    win = s[POOL_PAD:POOL_PAD + rb]
    pt = wv[POOL_PAD:POOL_PAD + rb]
    tg = t0 + lax.broadcasted_iota(jnp.int32, (rb, 1), 0)
    cnt = jnp.minimum(tg + 1, w).astype(F32)
    return win / cnt - pt


def _fill_windows(i, zp_ref, zc_ref, u0w, pwin, tl, cc):
    keep = i > 0
    zp = zp_ref[...]
    u0w[0:HALO, :] = jnp.where(keep, zp[:, :cc] * _sigmoid(zp[:, cc:2 * cc]), 0.0)
    pwin[0:HALO, :] = jnp.where(keep, zp[:, 2 * cc:], 0.0)

    def fill(c, carry):
        b = pl.multiple_of(c * ROW_CHUNK, SUBLANES)
        zc = zc_ref[pl.ds(b, ROW_CHUNK), :]
        u0w[pl.ds(HALO + b, ROW_CHUNK), :] = zc[:, :cc] * _sigmoid(zc[:, cc:2 * cc])
        pwin[pl.ds(HALO + b, ROW_CHUNK), :] = zc[:, 2 * cc:]
        return carry

    lax.fori_loop(0, tl // ROW_CHUNK, fill, 0)


def _mixer_fwd(z, ck, cb, lg, lb, pw, ps, am, name, xchg=None):
    L, ci = z.shape
    kw, _, cc = ck.shape
    cp = ci - 2 * cc
    ng, gd = pw.shape[0], pw.shape[1]
    tl = _token_tile(L)
    nt = L // tl
    hb = tl // HALO
    rb = _stat_rows(tl)
    tap0 = CONV_PAD - (kw - 1)

    def body(zp_ref, zc_ref, ck_ref, cb_ref, lg_ref, lb_ref, pw_ref, ps_ref, am_ref, y_ref, u1_ref, u0w, pwin):
        i = pl.program_id(0)
        _fill_windows(i, zp_ref, zc_ref, u0w, pwin, tl, cc)

        def conv(c, carry):
            b = pl.multiple_of(c * ROW_CHUNK, SUBLANES)
            w = u0w[pl.ds(b + HALO - CONV_PAD, ROW_CHUNK + CONV_PAD), :]
            acc = jnp.broadcast_to(cb_ref[...], (ROW_CHUNK, cc))
            for j in range(kw):
                acc = acc + _rows_of(ck_ref[j], ROW_CHUNK) * w[tap0 + j:tap0 + j + ROW_CHUNK]
            u1_ref[pl.ds(b, ROW_CHUNK), :] = acc
            return carry

        lax.fori_loop(0, tl // ROW_CHUNK, conv, 0)

        def blocks(k, carry):
            b = pl.multiple_of(k * rb, SUBLANES)
            u1 = u1_ref[pl.ds(b, rb), :]
            xc = u1 - _head_mean(u1, am_ref)
            var = _head_mean(xc * xc, am_ref)
            u2 = (xc * lax.rsqrt(var + EPS)) * lg_ref[...] + lb_ref[...]
            y_ref[pl.ds(b, rb), 0:cc] = (u2 * _sigmoid(u2)).astype(y_ref.dtype)
            for g in range(ng):
                d = _pool_fwd_block(pwin, pw_ref, b, rb, g, gd, POOL_WINDOWS[g], i * tl + b)
                yp = jnp.dot(d.astype(BF16), pw_ref[g].astype(BF16), preferred_element_type=F32)
                yp = yp * ps_ref[:, g * gd:(g + 1) * gd]
                y_ref[pl.ds(b, rb), cc + g * gd:cc + (g + 1) * gd] = yp.astype(y_ref.dtype)
            return carry

        lax.fori_loop(0, tl // rb, blocks, 0)

    def full(a):
        nd = a.ndim
        return pl.BlockSpec(a.shape, lambda i: (0,) * nd)

    out, xo = _call(
        body, name=name, grid=(nt,),
        in_specs=[pl.BlockSpec((HALO, ci), lambda i: (jnp.maximum(i * hb - 1, 0), 0)),
                  pl.BlockSpec((tl, ci), lambda i: (i, 0)),
                  full(ck), full(cb), full(lg), full(lb), full(pw), full(ps), full(am)],
        out_specs=(pl.BlockSpec((tl, cc + cp), lambda i: (i, 0)), pl.BlockSpec((tl, cc), lambda i: (i, 0))),
        out_shape=(jax.ShapeDtypeStruct((L, cc + cp), BF16), jax.ShapeDtypeStruct((L, cc), F32)),
        scratch_shapes=[pltpu.VMEM((HALO + tl, cc), F32), pltpu.VMEM((HALO + tl, cp), F32)],
        sem=("parallel",), args=(z, z, ck, cb, lg, lb, pw, ps, am), xchg=xchg)
    return out if xchg is None else (out, xo)


def _mixer_bwd(z, u1, dy, ck, lg, lb, pw, ps, am, name, xchg=None):
    L, ci = z.shape
    kw, _, cc = ck.shape
    cp = ci - 2 * cc
    ng, gd = pw.shape[0], pw.shape[1]
    tl = _token_tile(L)
    nt = L // tl
    hb = tl // HALO
    rb = _stat_rows(tl)

    def body(zp_ref, zc_ref, u1c_ref, u1n_ref, dyc_ref, dyn_ref, ck_ref, lg_ref, lb_ref, pw_ref, ps_ref, am_ref,
             dz_ref, dck_ref, dcb_ref, dlg_ref, dlb_ref, dpw_ref, dps_ref,
             u0w, pwin, du1w, ddw, ew, dkacc, dcb8, dlg8, dlb8, dps8):
        i = pl.program_id(0)
        has_next = i < nt - 1

        @pl.when(i == 0)
        def _():
            dck_ref[...] = jnp.zeros_like(dck_ref)
            dcb_ref[...] = jnp.zeros_like(dcb_ref)
            dlg_ref[...] = jnp.zeros_like(dlg_ref)
            dlb_ref[...] = jnp.zeros_like(dlb_ref)
            dpw_ref[...] = jnp.zeros_like(dpw_ref)
            dps_ref[...] = jnp.zeros_like(dps_ref)

        dkacc[...] = jnp.zeros_like(dkacc)
        dcb8[...] = jnp.zeros_like(dcb8)
        dlg8[...] = jnp.zeros_like(dlg8)
        dlb8[...] = jnp.zeros_like(dlb8)
        dps8[...] = jnp.zeros_like(dps8)

        _fill_windows(i, zp_ref, zc_ref, u0w, pwin, tl, cc)

        def conv_side(u1, dyc, own):
            xc = u1 - _head_mean(u1, am_ref)
            rstd = lax.rsqrt(_head_mean(xc * xc, am_ref) + EPS)
            uh = xc * rstd
            lgv = lg_ref[...]
            u2 = uh * lgv + lb_ref[...]
            sg = _sigmoid(u2)
            du2 = dyc * (sg * (1.0 + u2 * (1.0 - sg)))
            if own:
                dlg8[...] += _rowsum8(du2 * uh)
                dlb8[...] += _rowsum8(du2)
            duh = du2 * lgv
            return rstd * (duh - _head_mean(duh, am_ref) - uh * _head_mean(duh * uh, am_ref))

        def pool_side(dyp, t0, rows):
            dds, es = [], []
            tg = t0 + lax.broadcasted_iota(jnp.int32, (rows, 1), 0)
            for g in range(ng):
                dypre = dyp[:, g * gd:(g + 1) * gd] * ps_ref[:, g * gd:(g + 1) * gd]
                dd = lax.dot_general(dypre.astype(BF16), pw_ref[g].astype(BF16), (((1,), (1,)), ((), ())),
                                     preferred_element_type=F32)
                cnt = jnp.minimum(tg + 1, POOL_WINDOWS[g]).astype(F32)
                dds.append(dd)
                es.append(dd / cnt)
            return jnp.concatenate(dds, axis=-1), jnp.concatenate(es, axis=-1)

        def blocks(k, carry):
            b = pl.multiple_of(k * rb, SUBLANES)
            dyb = dyc_ref[pl.ds(b, rb), :].astype(F32)
            du1 = conv_side(u1c_ref[pl.ds(b, rb), :], dyb[:, :cc], True)
            du1w[pl.ds(b, rb), :] = du1
            dcb8[...] += _rowsum8(du1)
            dyp = dyb[:, cc:]
            dd, e = pool_side(dyp, i * tl + b, rb)
            ddw[pl.ds(b, rb), :] = dd
            ew[pl.ds(b, rb), :] = e
            for g in range(ng):
                d = _pool_fwd_block(pwin, pw_ref, b, rb, g, gd, POOL_WINDOWS[g], i * tl + b)
                db16 = d.astype(BF16)
                dypg = dyp[:, g * gd:(g + 1) * gd]
                ypre = jnp.dot(db16, pw_ref[g].astype(BF16), preferred_element_type=F32)
                dps8[:, g * gd:(g + 1) * gd] += _rowsum8(dypg * ypre)
                dypre = (dypg * ps_ref[:, g * gd:(g + 1) * gd]).astype(BF16)
                dpw_ref[g] += lax.dot_general(db16, dypre, (((0,), (0,)), ((), ())), preferred_element_type=F32)
            return carry

        lax.fori_loop(0, tl // rb, blocks, 0)

        dyn = dyn_ref[...].astype(F32)
        du1n = conv_side(u1n_ref[...], dyn[:, :cc], False)
        du1w[tl:tl + HALO, :] = jnp.where(has_next, du1n, 0.0)
        ddn, en = pool_side(dyn[:, cc:], (i + 1) * tl, HALO)
        ew[tl:tl + HALO, :] = jnp.where(has_next, en, 0.0)

        def taps(c, carry):
            b = pl.multiple_of(c * ROW_CHUNK, SUBLANES)
            w = du1w[pl.ds(b, ROW_CHUNK + CONV_PAD), :]
            u0c = u0w[pl.ds(HALO + b, ROW_CHUNK), :]
            acc = jnp.zeros((ROW_CHUNK, cc), F32)
            for j in range(kw):
                o = kw - 1 - j
                sh = w[o:o + ROW_CHUNK]
                acc = acc + _rows_of(ck_ref[j], ROW_CHUNK) * sh
                dkacc[j] += _rowsum8(u0c * sh)
            zc = zc_ref[pl.ds(b, ROW_CHUNK), :]
            a = zc[:, :cc]
            sg = _sigmoid(zc[:, cc:2 * cc])
            dz_ref[pl.ds(b, ROW_CHUNK), 0:cc] = (acc * sg).astype(dz_ref.dtype)
            dz_ref[pl.ds(b, ROW_CHUNK), cc:2 * cc] = (acc * a * sg * (1.0 - sg)).astype(dz_ref.dtype)
            return carry

        lax.fori_loop(0, tl // ROW_CHUNK, taps, 0)

        def pool_back(k, carry):
            b = pl.multiple_of(k * rb, SUBLANES)
            n = rb + POOL_PAD
            for g in range(ng):
                s = ew[pl.ds(b, n), g * gd:(g + 1) * gd]
                sh = 1
                while sh < POOL_WINDOWS[g]:
                    s = s + pltpu.roll(s, n - sh, axis=0)
                    sh *= 2
                dp = s[0:rb] - ddw[pl.ds(b, rb), g * gd:(g + 1) * gd]
                dz_ref[pl.ds(b, rb), 2 * cc + g * gd:2 * cc + (g + 1) * gd] = dp.astype(dz_ref.dtype)
            return carry

        lax.fori_loop(0, tl // rb, pool_back, 0)

        dck_ref[...] += jnp.sum(dkacc[...], axis=1)
        dcb_ref[...] += jnp.sum(dcb8[...], axis=0, keepdims=True)
        dlg_ref[...] += jnp.sum(dlg8[...], axis=0, keepdims=True)
        dlb_ref[...] += jnp.sum(dlb8[...], axis=0, keepdims=True)
        dps_ref[...] += jnp.sum(dps8[...], axis=0, keepdims=True)

    def full(a):
        nd = a.ndim
        return pl.BlockSpec(a.shape, lambda i: (0,) * nd)

    nhb = L // HALO

    def prev_map(i):
        return (jnp.maximum(i * hb - 1, 0), 0)

    def next_map(i):
        return (jnp.minimum((i + 1) * hb, nhb - 1), 0)

    dcc = cc + cp
    row_cc = jax.ShapeDtypeStruct((1, cc), F32)
    out_shape = (jax.ShapeDtypeStruct((L, ci), BF16), jax.ShapeDtypeStruct((kw, cc), F32), row_cc, row_cc, row_cc,
                 jax.ShapeDtypeStruct((ng, gd, gd), F32), jax.ShapeDtypeStruct((1, cp), F32))
    acc_spec = [pl.BlockSpec((kw, cc), lambda i: (0, 0))] + [pl.BlockSpec((1, cc), lambda i: (0, 0))] * 3 + [
        pl.BlockSpec((ng, gd, gd), lambda i: (0, 0, 0)), pl.BlockSpec((1, cp), lambda i: (0, 0))]
    out, xo = _call(
        body, name=name, grid=(nt,),
        in_specs=[pl.BlockSpec((HALO, ci), prev_map), pl.BlockSpec((tl, ci), lambda i: (i, 0)),
                  pl.BlockSpec((tl, cc), lambda i: (i, 0)), pl.BlockSpec((HALO, cc), next_map),
                  pl.BlockSpec((tl, dcc), lambda i: (i, 0)), pl.BlockSpec((HALO, dcc), next_map),
                  full(ck), full(lg), full(lb), full(pw), full(ps), full(am)],
        out_specs=tuple([pl.BlockSpec((tl, ci), lambda i: (i, 0))] + acc_spec),
        out_shape=out_shape,
        scratch_shapes=[pltpu.VMEM((HALO + tl, cc), F32), pltpu.VMEM((HALO + tl, cp), F32),
                        pltpu.VMEM((tl + HALO, cc), F32), pltpu.VMEM((tl, cp), F32), pltpu.VMEM((tl + HALO, cp), F32),
                        pltpu.VMEM((kw, SUBLANES, cc), F32), pltpu.VMEM((SUBLANES, cc), F32),
                        pltpu.VMEM((SUBLANES, cc), F32), pltpu.VMEM((SUBLANES, cc), F32), pltpu.VMEM((SUBLANES, cp), F32)],
        sem=("arbitrary",), args=(z, z, u1, u1, dy, dy, ck, lg, lb, pw, ps, am), xchg=xchg)
    return out if xchg is None else (out, xo)


def _row_parts(nc, n=3):
    n = min(n, nc)
    cuts = [round(k * nc / n) for k in range(n + 1)]
    return [(cuts[k], cuts[k + 1]) for k in range(n)]


def _tap_rows(k_ref):
    return [jnp.broadcast_to(k_ref[j:j + 1, :], (SUBLANES, k_ref.shape[1])) for j in range(k_ref.shape[0])]


def _rows_of(tap, n):
    return tap if n == SUBLANES else jnp.concatenate([tap] * (n // SUBLANES), axis=0)


def _ffn_conv(win, taps, rows):
    kw = len(taps)
    o = FFN_PAD - (kw - 1)
    acc = _rows_of(taps[0], rows) * win[o:o + rows]
    for j in range(1, kw):
        acc = acc + _rows_of(taps[j], rows) * win[o + j:o + j + rows]
    return acc


def _ffn_block_fwd(h_mid, g, wup_t, kf, wdown, name, xchg=None):
    L, D = h_mid.shape
    f = wdown.shape[0]
    kw = kf.shape[0]
    tl = _token_tile(L)
    tc = _divisor(f, 256, 128)
    nj = f // tc
    nt = L // tl
    pad = 2 * SUBLANES
    hb = tl // pad
    rc = CONV3_ROWS
    parts = _row_parts(tl // rc)

    def body(hp_ref, hc_ref, g_ref, wg_ref, wv_ref, kg_ref, kv_ref, wd_ref, out_ref, hn_ref, act_ref, ux_ref, uc_ref,
             hn_halo, halo, ug_ref, acc):
        i = pl.program_id(0)
        kb = pl.program_id(1)

        @pl.when(kb == 0)
        def _():
            gg = g_ref[...]

            def norm(x):
                r = lax.rsqrt(jnp.mean(x * x, axis=-1, keepdims=True) + EPS)
                return ((x * r) * gg).astype(BF16)

            hn_halo[...] = jnp.where(i > 0, norm(hp_ref[...]), jnp.zeros((pad, D), BF16))
            hn_ref[...] = norm(hc_ref[...])
            acc[...] = jnp.zeros_like(acc)

        w_refs = (wg_ref, wv_ref)
        taps = (_tap_rows(kg_ref), _tap_rows(kv_ref))
        hh = hn_halo[...]
        for h in range(2):
            halo[h] = _dot_nt(hh, w_refs[h][...])[pad - FFN_PAD:]

        def up_part(lo, hi):
            a, b = lo * rc, hi * rc
            for h in range(2):
                ug_ref[h, a:b, :] = _dot_nt(hn_ref[a:b, :], w_refs[h][...])

        def down_part(lo, hi):
            a, b = lo * rc, hi * rc
            acc[a:b, :] += jnp.dot(act_ref[a:b, :], wd_ref[...], preferred_element_type=F32)

        def chunk_rows(lo, hi):
            for c in range(lo, hi):
                r0 = c * rc
                convd = []
                for h in range(2):
                    if c == 0:
                        win = jnp.concatenate([halo[h], ug_ref[h, 0:rc]], axis=0)
                    else:
                        win = ug_ref[h, r0 - FFN_PAD:r0 + rc]
                    convd.append(_ffn_conv(win, taps[h], rc))
                    ux_ref[h, r0:r0 + rc, :] = win[FFN_PAD:].astype(BF16)
                    uc_ref[h, r0:r0 + rc, :] = convd[h].astype(BF16)
                gate, val = convd
                act_ref[r0:r0 + rc, :] = ((gate * _sigmoid(gate)) * val).astype(BF16)

        for p, (lo, hi) in enumerate(parts):
            if p == 0:
                up_part(lo, hi)
            if p + 1 < len(parts):
                up_part(*parts[p + 1])
            if p > 0:
                down_part(*parts[p - 1])
            chunk_rows(lo, hi)
        down_part(*parts[-1])

        @pl.when(kb == nj - 1)
        def _():
            out_ref[...] = acc[...] + hc_ref[...]

    out, xo = _call(
        body, name=name, grid=(nt, nj),
        in_specs=[pl.BlockSpec((pad, D), lambda i, k: (jnp.maximum(i * hb - 1, 0), 0)),
                  pl.BlockSpec((tl, D), lambda i, k: (i, 0)),
                  pl.BlockSpec((1, D), lambda i, k: (0, 0)),
                  pl.BlockSpec((tc, D), lambda i, k: (k, 0)), pl.BlockSpec((tc, D), lambda i, k: (k + nj, 0)),
                  pl.BlockSpec((kw, tc), lambda i, k: (0, k)), pl.BlockSpec((kw, tc), lambda i, k: (0, k + nj)),
                  pl.BlockSpec((tc, D), lambda i, k: (k, 0))],
        out_specs=(pl.BlockSpec((tl, D), lambda i, k: (i, 0)), pl.BlockSpec((tl, D), lambda i, k: (i, 0)),
                   pl.BlockSpec((tl, tc), lambda i, k: (i, k)),
                   pl.BlockSpec((2, tl, tc), lambda i, k: (0, i, k)), pl.BlockSpec((2, tl, tc), lambda i, k: (0, i, k))),
        out_shape=(jax.ShapeDtypeStruct((L, D), F32), jax.ShapeDtypeStruct((L, D), BF16),
                   jax.ShapeDtypeStruct((L, f), BF16),
                   jax.ShapeDtypeStruct((2, L, f), BF16), jax.ShapeDtypeStruct((2, L, f), BF16)),
        scratch_shapes=[pltpu.VMEM((pad, D), BF16), pltpu.VMEM((2, FFN_PAD, tc), F32), pltpu.VMEM((2, tl, tc), F32),
                        pltpu.VMEM((tl, D), F32)],
        sem=("parallel", "arbitrary"), args=(h_mid, h_mid, g, wup_t, wup_t, kf, kf, wdown), xchg=xchg)
    return out if xchg is None else (out, xo)


def _ffn_block_bwd(dh, h_mid, g, ux, uc, kf, wdown, wup_t, name, xchg=None):
    L, D = dh.shape
    f = ux.shape[2]
    kw = kf.shape[0]
    tl = _token_tile(L)
    tc = _divisor(f, 256, 128)
    nj = f // tc
    nt = L // tl
    pad = 2 * SUBLANES
    rc = CONV3_ROWS
    nc = tl // rc
    parts = _row_parts(nc)

    def body(dhc_ref, dhn_ref, hm_ref, g_ref, xg_ref, xv_ref, cg_ref, cgn_ref, cv_ref, cvn_ref, kg_ref, kv_ref,
             wd_ref, wg_ref, wv_ref, dhm_ref, dg_ref, du_ref, dk_ref, dh_ext, dact_s, acc):
        i = pl.program_id(0)
        kb = pl.program_id(1)

        @pl.when(kb == 0)
        def _():
            dh_ext[0:tl, :] = dhc_ref[...].astype(BF16)
            dh_ext[tl:tl + pad, :] = dhn_ref[...].astype(BF16)
            acc[...] = jnp.zeros_like(acc)

        @pl.when(jnp.logical_and(i == 0, kb == 0))
        def _():
            dg_ref[...] = jnp.zeros_like(dg_ref)
            dk_ref[...] = jnp.zeros_like(dk_ref)

        x_refs, c_refs, nxt = (xg_ref, xv_ref), (cg_ref, cv_ref), (cgn_ref, cvn_ref)
        taps = (_tap_rows(kg_ref), _tap_rows(kv_ref))
        dk = [[jnp.zeros((SUBLANES, tc), F32) for _ in range(kw)] for _ in range(2)]

        def dact_part(lo, hi):
            a, b = lo * rc, hi * rc + pad
            dact_s[a:b, :] = _dot_nt(dh_ext[a:b, :], wd_ref[...])

        def dhn_part(lo, hi):
            a, b = lo * rc, hi * rc
            acc[a:b, :] += (jnp.dot(du_ref[0, a:b, :], wg_ref[...], preferred_element_type=F32)
                            + jnp.dot(du_ref[1, a:b, :], wv_ref[...], preferred_element_type=F32))

        for p, (lo, hi) in enumerate(parts):
            if p == 0:
                dact_part(lo, hi)
            if p + 1 < len(parts):
                dact_part(*parts[p + 1])
            if p > 0:
                dhn_part(*parts[p - 1])
            chunk_rows(lo, hi, x_refs, c_refs, nxt, taps, dk, i, dact_s, du_ref)
        dhn_part(*parts[-1])
        for h in range(2):
            for j in range(kw):
                dk_ref[kb, h, j:j + 1, :] += jnp.sum(dk[h][j], axis=0, keepdims=True)

        @pl.when(kb == nj - 1)
        def _():
            x = hm_ref[...]
            r = lax.rsqrt(jnp.mean(x * x, axis=-1, keepdims=True) + EPS)
            xhat = x * r
            dhn = acc[...]
            dxhat = dhn * g_ref[...]
            dhm_ref[...] = dhc_ref[...] + r * (dxhat - xhat * jnp.mean(dxhat * xhat, axis=-1, keepdims=True))
            dg_ref[...] += jnp.sum(_rowsum8(dhn * xhat), axis=0, keepdims=True)

    def chunk_rows(lo, hi, x_refs, c_refs, nxt, taps, dk, i, dact_s, du_ref):
        for c in range(lo, hi):
            r0 = c * rc
            n = rc + FFN_PAD
            convd = []
            for h in range(2):
                if c == nc - 1:
                    rows = jnp.concatenate([c_refs[h][r0:r0 + rc, :], nxt[h][...]], axis=0)
                else:
                    rows = c_refs[h][r0:r0 + rc + pad, :]
                convd.append(rows.astype(F32)[0:n])
            gate, val = convd
            xs = [x_refs[h][r0:r0 + rc, :].astype(F32) for h in range(2)]
            dact = dact_s[r0:r0 + n, :]
            sg = _sigmoid(gate)
            dcs = [dact * val * (sg * (1.0 + gate * (1.0 - sg))), dact * (gate * sg)]
            if c == nc - 1:
                live = jnp.logical_or(lax.broadcasted_iota(jnp.int32, (n, 1), 0) < rc, i < nt - 1)
                dcs = [jnp.where(live, d, 0.0) for d in dcs]
            for h in range(2):
                xc = xs[h]
                dx = None
                for j in range(kw):
                    o = kw - 1 - j
                    sh = dcs[h][o:o + rc]
                    term = _rows_of(taps[h][j], rc) * sh
                    dx = term if dx is None else dx + term
                    dk[h][j] = dk[h][j] + _rowsum8(xc * sh)
                du_ref[h, r0:r0 + rc, :] = dx.astype(BF16)

    def after(i):
        return jnp.minimum((i + 1) * (tl // pad), L // pad - 1)

    def half(h, rows, idx):
        return pl.BlockSpec((None, rows, tc), lambda i, k: (h,) + idx(i, k))

    def tile(i, k):
        return (i, k)

    def behind(i, k):
        return (after(i), k)

    out, xo = _call(
        body, name=name, grid=(nt, nj),
        in_specs=[pl.BlockSpec((tl, D), lambda i, k: (i, 0)),
                  pl.BlockSpec((pad, D), lambda i, k: (after(i), 0)),
                  pl.BlockSpec((tl, D), lambda i, k: (i, 0)), pl.BlockSpec((1, D), lambda i, k: (0, 0)),
                  half(0, tl, tile), half(1, tl, tile),
                  half(0, tl, tile), half(0, pad, behind), half(1, tl, tile), half(1, pad, behind),
                  pl.BlockSpec((kw, tc), lambda i, k: (0, k)), pl.BlockSpec((kw, tc), lambda i, k: (0, k + nj)),
                  pl.BlockSpec((tc, D), lambda i, k: (k, 0)),
                  pl.BlockSpec((tc, D), lambda i, k: (k, 0)), pl.BlockSpec((tc, D), lambda i, k: (k + nj, 0))],
        out_specs=(pl.BlockSpec((tl, D), lambda i, k: (i, 0)), pl.BlockSpec((1, D), lambda i, k: (0, 0)),
                   pl.BlockSpec((2, tl, tc), lambda i, k: (0, i, k)),
                   pl.BlockSpec((nj, 2, kw, tc), lambda i, k: (0, 0, 0, 0))),
        out_shape=(jax.ShapeDtypeStruct((L, D), F32), jax.ShapeDtypeStruct((1, D), F32),
                   jax.ShapeDtypeStruct((2, L, f), BF16), jax.ShapeDtypeStruct((nj, 2, kw, tc), F32)),
        scratch_shapes=[pltpu.VMEM((tl + pad, D), BF16), pltpu.VMEM((tl + pad, tc), F32), pltpu.VMEM((tl, D), F32)],
        sem=("arbitrary", "arbitrary"), args=(dh, dh, h_mid, g, ux, ux, uc, uc, uc, uc, kf, kf, wdown, wup_t, wup_t),
        xchg=xchg)
    return out if xchg is None else (out, xo)


def _adamw_math(w, g, m, v):
    m = ADAM_B1 * m + (1.0 - ADAM_B1) * g
    v = ADAM_B2 * v + (1.0 - ADAM_B2) * (g * g)
    m_hat = m / (1.0 - ADAM_B1 ** ADAM_STEP)
    v_hat = v / (1.0 - ADAM_B2 ** ADAM_STEP)
    delta = -ADAM_LR * (m_hat / (jnp.sqrt(v_hat) + ADAM_EPS) + ADAM_WD * w)
    return delta, m, v


def _sum_parts(parts_ref, idx):
    g = parts_ref[(0,) + idx].astype(F32)
    for q in range(1, N_DEV):
        g = g + parts_ref[(q,) + idx].astype(F32)
    return g


def _adamw_big(parts, w, m, v, name):
    nl, R, C = w.shape
    tr = _divisor(R, 256, 2 * SUBLANES)

    def body(*refs):
        p_refs = refs[:nl]
        w_ref, m_ref, v_ref, g_ref, d_ref, nm_ref, nv_ref = refs[nl:]
        layer = pl.program_id(0)
        for k in range(nl):
            @pl.when(layer == k)
            def _(k=k):
                g = _sum_parts(p_refs[k], ())
                d, nm, nv = _adamw_math(w_ref[0], g, m_ref[0], v_ref[0])
                g_ref[0] = g
                d_ref[0] = d
                nm_ref[0] = nm
                nv_ref[0] = nv

    def part_spec(k):
        return pl.BlockSpec((N_DEV, tr, C), lambda l, r: (0, jnp.where(l == k, r, 0), 0))

    blk = pl.BlockSpec((1, tr, C), lambda l, r: (l, r, 0))
    shp = jax.ShapeDtypeStruct((nl, R, C), F32)
    return pl.pallas_call(
        body, name=name, grid=(nl, R // tr),
        in_specs=[part_spec(k) for k in range(nl)] + [blk, blk, blk],
        out_specs=(blk, blk, blk, blk), out_shape=(shp, shp, shp, shp),
        compiler_params=_params(("arbitrary", "arbitrary")),
    )(*parts, w, m, v)


def _adamw_small(entries, name):
    n = len(entries)
    uniq = []
    for e in entries:
        if not any(e[0] is u for u in uniq):
            uniq.append(e[0])
    pidx = [next(k for k, u in enumerate(uniq) if u is e[0]) for e in entries]
    npart = len(uniq)

    def body(*refs):
        p_refs = refs[:npart]
        wmv = refs[npart:npart + 3 * n]
        outs = refs[npart + 3 * n:]
        for t, e in enumerate(entries):
            lo, w = e[1], e[2]
            rows = w.shape[0]
            pr = p_refs[pidx[t]]
            g = pr[0, lo:lo + rows].astype(F32)
            for q in range(1, N_DEV):
                g = g + pr[q, lo:lo + rows].astype(F32)
            d, nm, nv = _adamw_math(wmv[3 * t][...], g, wmv[3 * t + 1][...], wmv[3 * t + 2][...])
            outs[4 * t][...] = g
            outs[4 * t + 1][...] = d
            outs[4 * t + 2][...] = nm
            outs[4 * t + 3][...] = nv

    vm = pl.BlockSpec(memory_space=pltpu.VMEM)
    args = list(uniq)
    out_shape = []
    for e in entries:
        args += [e[2], e[3], e[4]]
        out_shape += [jax.ShapeDtypeStruct(e[2].shape, F32)] * 4
    res = pl.pallas_call(
        body, name=name, in_specs=[vm] * len(args), out_specs=tuple([vm] * len(out_shape)),
        out_shape=tuple(out_shape), compiler_params=_params(),
    )(*args)
    return [tuple(res[4 * t:4 * t + 4]) for t in range(n)]


def _head_matrix(cc):
    bw = min(256, cc)
    r = lax.broadcasted_iota(jnp.int32, (bw, bw), 0) // HEAD_DIM
    c = lax.broadcasted_iota(jnp.int32, (bw, bw), 1) // HEAD_DIM
    return jnp.where(r == c, 1.0 / HEAD_DIM, 0.0).astype(BF16)


def _cols_from_shards(g):
    nd = g.ndim
    perm = tuple(range(1, nd - 1)) + (0, nd - 1)
    t = jnp.transpose(g, perm)
    return t.reshape(t.shape[:-2] + (t.shape[-2] * t.shape[-1],))


def _cols_to_shards(a):
    nd = a.ndim
    t = a.reshape(a.shape[:-1] + (N_DEV, a.shape[-1] // N_DEV))
    perm = (nd - 1,) + tuple(range(nd - 1)) + (nd,)
    return jnp.transpose(t, perm)


def kernel(x, meta_tokens, norm1_g, w_in, conv_dw_k, conv_dw_b, conv_ln_g, conv_ln_b, pool_w, pool_scale, w_out, norm2_g, w_up, ffn_dw_k, w_down, final_g, loss_target, m_meta_tokens, m_norm1_g, m_w_in, m_conv_dw_k, m_conv_dw_b, m_conv_ln_g, m_conv_ln_b, m_pool_w, m_pool_scale, m_w_out, m_norm2_g, m_w_up, m_ffn_dw_k, m_w_down, m_final_g, v_meta_tokens, v_norm1_g, v_w_in, v_conv_dw_k, v_conv_dw_b, v_conv_ln_g, v_conv_ln_b, v_pool_w, v_pool_scale, v_w_out, v_norm2_g, v_w_up, v_ffn_dw_k, v_w_down, v_final_g):
    depth, D = norm1_g.shape
    n_meta = meta_tokens.shape[0]
    seq = x.shape[1]
    L = n_meta + seq
    cc = conv_dw_b.shape[1]
    ng, gd = pool_w.shape[1], pool_w.shape[2]
    f = w_down.shape[1] * N_DEV

    def rows(g):
        return g.reshape(-1, g.shape[-1])

    b16 = lambda a: a.astype(BF16)
    tr = lambda a: jnp.swapaxes(a, -1, -2)
    w_in_t, m_w_in_t, v_w_in_t = tr(w_in), tr(m_w_in), tr(v_w_in)
    w_up_t, m_w_up_t, v_w_up_t = tr(w_up), tr(m_w_up), tr(v_w_up)
    (g_in0, g_ck, g_kf, g_meta) = _exchange([b16(w_in_t[0]), conv_dw_k, ffn_dw_k, meta_tokens], ["gather"] * 4,
                                            "gather_first")
    ck_full = _cols_from_shards(g_ck)
    ck_rows = jnp.broadcast_to(ck_full[:, :, None, :], ck_full.shape[:2] + (SUBLANES, cc))
    kf_full = _cols_from_shards(g_kf)
    meta_full = _cols_from_shards(g_meta)
    am = _head_matrix(cc)
    win, wout, wup, wdown = [None] * depth, [None] * depth, [None] * depth, [None] * depth
    win[0] = rows(g_in0)

    h = (meta_full, x[0])
    saved = []
    for l in range(depth):
        more = l + 1 < depth
        if l == 0:
            (z, hn1), (g_out, g_down) = _norm_proj(h, norm1_g[l:l + 1], win[l], f"in_proj_{l}", tn_cap=1536,
                                                   xchg=([b16(w_out[l]), b16(w_down[l])], ["gather"] * 2))
            wout[l], wdown[l] = rows(g_out), rows(g_down)
            (ymix, u1), (g_up,) = _mixer_fwd(z, ck_rows[l], conv_dw_b[l:l + 1], conv_ln_g[l:l + 1], conv_ln_b[l:l + 1],
                                             pool_w[l], pool_scale[l:l + 1], am, f"mixer_fwd_{l}",
                                             xchg=([b16(w_up_t[l])], ["gather"]))
            wup[l] = rows(g_up)
        else:
            z, hn1 = _norm_proj(h, norm1_g[l:l + 1], win[l], f"in_proj_{l}", tn_cap=1536)
            ymix, u1 = _mixer_fwd(z, ck_rows[l], conv_dw_b[l:l + 1], conv_ln_g[l:l + 1], conv_ln_b[l:l + 1], pool_w[l],
                                  pool_scale[l:l + 1], am, f"mixer_fwd_{l}")
        if more:
            h_mid, (g_in, g_out) = _mm(ymix, wout[l], f"out_proj_{l}", res=h, tn_cap=1024,
                                       xchg=([b16(w_in_t[l + 1]), b16(w_out[l + 1])], ["gather"] * 2))
            win[l + 1], wout[l + 1] = rows(g_in), rows(g_out)
            nxt = [b16(w_up_t[l + 1]), b16(w_down[l + 1])]
            (h_out, hn2, act, ux, uc), got = _ffn_block_fwd(h_mid, norm2_g[l:l + 1], wup[l], kf_full[l], wdown[l],
                                                            f"ffn_fwd_{l}", xchg=(nxt, ["gather"] * 2))
            wup[l + 1], wdown[l + 1] = rows(got[0]), rows(got[1])
        else:
            h_mid = _mm(ymix, wout[l], f"out_proj_{l}", res=h, tn_cap=1024)
            h_out, hn2, act, ux, uc = _ffn_block_fwd(h_mid, norm2_g[l:l + 1], wup[l], kf_full[l], wdown[l], f"ffn_fwd_{l}")
        saved.append((h, hn1, z, u1, ymix, h_mid, hn2, ux, uc, act))
        h = h_out

    dh, d_final_g, loss_part = _loss_head(h, final_g.reshape(1, D), loss_target[0], n_meta, "loss_head")

    def row_shards(gm):
        return gm.reshape(N_DEV, -1, gm.shape[-1])

    zero_row = jnp.zeros((1, D), F32)
    gw = {k: [None] * depth for k in ("ck", "cb", "lg", "lb", "pw", "ps", "kf", "n1", "n2")}
    parts = {k: [None] * depth for k in ("in", "out", "up", "down")}
    for l in reversed(range(depth)):
        h_in, hn1, z, u1, ymix, h_mid, hn2, ux, uc, act = saved[l]
        g_down = _mm_tn(act, dh, f"down_proj_wgrad_{l}", tq_cap=512)
        (dh_mid, gw["n2"][l], dug0, dkf), (parts["down"][l],) = _ffn_block_bwd(
            dh, h_mid, norm2_g[l:l + 1], ux, uc, kf_full[l], wdown[l], wup[l], f"ffn_bwd_{l}",
            xchg=([row_shards(g_down)], ["a2a"]))
        gw["kf"][l] = jnp.transpose(dkf, (2, 1, 0, 3)).reshape(dkf.shape[2], -1)
        g_up_t = _mm_tn(dug0, hn2, f"up_proj_wgrad_{l}", halves=2, tq_cap=1024)
        dymix = _mm(dh_mid, wout[l], f"out_proj_bwd_{l}", b_t=True, out_dtype=BF16, tn_cap=1024)
        g_out = _mm_tn(ymix, dh_mid, f"out_proj_wgrad_{l}", tq_cap=512)
        ((dz, gw["ck"][l], gw["cb"][l], gw["lg"][l], gw["lb"][l], gw["pw"][l], gw["ps"][l]),
         (parts["up"][l], parts["out"][l])) = _mixer_bwd(
            z, u1, dymix, ck_rows[l], conv_ln_g[l:l + 1], conv_ln_b[l:l + 1], pool_w[l], pool_scale[l:l + 1], am,
            f"mixer_bwd_{l}", xchg=([row_shards(g_up_t), row_shards(g_out)], ["a2a", "a2a"]))
        g_in_t = _mm_tn(dz, hn1, f"in_proj_wgrad_{l}", tq_cap=1024)
        if l > 0:
            (dh, gw["n1"][l]), (parts["in"][l],) = _proj_bwd_norm(dz, win[l], h_in, norm1_g[l:l + 1], dh_mid, zero_row,
                                                                  f"in_proj_bwd_{l}", xchg=([row_shards(g_in_t)], ["a2a"]))
        else:
            meta_rows, x_rows = h_in
            (grad_x, dg_x), (parts["in"][l],) = _proj_bwd_norm(dz, win[l], x_rows, norm1_g[l:l + 1], dh_mid, zero_row,
                                                               f"in_proj_bwd_{l}", skip=n_meta,
                                                               xchg=([row_shards(g_in_t)], ["a2a"]))
            d_meta, gw["n1"][l] = _proj_bwd_norm(dz[:n_meta], win[l], meta_rows, norm1_g[l:l + 1], dh_mid[:n_meta],
                                                 dg_x, f"in_proj_bwd_meta_{l}")
    grad_x = grad_x[None]

    pack_d = jnp.concatenate(gw["n1"] + gw["n2"] + [d_final_g, jnp.broadcast_to(loss_part[:, :1], (1, D)), zero_row, zero_row], axis=0)
    pack_c = jnp.concatenate(gw["cb"] + gw["lg"] + gw["lb"] + gw["ps"], axis=0)
    pack_pw = b16(jnp.stack(gw["pw"]).reshape(depth * ng * gd, gd))
    src = [_cols_to_shards(jnp.stack(gw["ck"])), _cols_to_shards(jnp.stack(gw["kf"])), _cols_to_shards(d_meta),
           pack_d, pack_c, pack_pw]
    r_ck, r_kf, r_meta, r_d, r_c, r_pw = _exchange(src, ["a2a"] * 3 + ["gather"] * 3, "exchange_small_grads")

    big = {
        "w_in": tuple(tr(a) for a in _adamw_big(parts["in"], w_in_t, m_w_in_t, v_w_in_t, "adamw_w_in")),
        "w_out": _adamw_big(parts["out"], w_out, m_w_out, v_w_out, "adamw_w_out"),
        "w_up": tuple(tr(a) for a in _adamw_big(parts["up"], w_up_t, m_w_up_t, v_w_up_t, "adamw_w_up")),
        "w_down": _adamw_big(parts["down"], w_down, m_w_down, v_w_down, "adamw_w_down"),
    }
    kwid = conv_dw_k.shape[1]
    fkw = ffn_dw_k.shape[1]
    row = lambda a: a.reshape(1, -1)
    entries = [
        (r_d, 0, norm1_g, m_norm1_g, v_norm1_g),
        (r_d, depth, norm2_g, m_norm2_g, v_norm2_g),
        (r_d, 2 * depth, row(final_g), row(m_final_g), row(v_final_g)),
        (r_c, 0, conv_dw_b, m_conv_dw_b, v_conv_dw_b),
        (r_c, depth, conv_ln_g, m_conv_ln_g, v_conv_ln_g),
        (r_c, 2 * depth, conv_ln_b, m_conv_ln_b, v_conv_ln_b),
        (r_c, 3 * depth, pool_scale, m_pool_scale, v_pool_scale),
        (r_pw, 0, pool_w.reshape(-1, gd), m_pool_w.reshape(-1, gd), v_pool_w.reshape(-1, gd)),
        (r_ck.reshape(N_DEV, depth * kwid, -1), 0, conv_dw_k.reshape(depth * kwid, -1),
         m_conv_dw_k.reshape(depth * kwid, -1), v_conv_dw_k.reshape(depth * kwid, -1)),
        (r_kf.reshape(N_DEV, depth * fkw, -1), 0, ffn_dw_k.reshape(depth * fkw, -1),
         m_ffn_dw_k.reshape(depth * fkw, -1), v_ffn_dw_k.reshape(depth * fkw, -1)),
        (r_meta, 0, meta_tokens, m_meta_tokens, v_meta_tokens),
        (r_d, 2 * depth + 1, zero_row, zero_row, zero_row),
    ]
    small = _adamw_small(entries, "adamw_small")
    names = ["norm1_g", "norm2_g", "final_g", "conv_dw_b", "conv_ln_g", "conv_ln_b", "pool_scale", "pool_w",
             "conv_dw_k", "ffn_dw_k", "meta_tokens"]
    shapes = {"final_g": final_g.shape, "pool_w": pool_w.shape, "conv_dw_k": conv_dw_k.shape, "ffn_dw_k": ffn_dw_k.shape}
    res = dict(big)
    for nme, quad in zip(names, small[:-1]):
        res[nme] = tuple(a.reshape(shapes[nme]) if nme in shapes else a for a in quad)
    loss = small[-1][0][0, 0]

    order = ["meta_tokens", "norm1_g", "w_in", "conv_dw_k", "conv_dw_b", "conv_ln_g", "conv_ln_b", "pool_w", "pool_scale",
             "w_out", "norm2_g", "w_up", "ffn_dw_k", "w_down", "final_g"]
    return (loss, grad_x, *[res[k][0] for k in order], *[res[k][1] for k in order], *[res[k][2] for k in order],
            *[res[k][3] for k in order])
```
